```python
import math
import jax, jax.numpy as jnp
from jax import lax
import numpy as np

D_MODEL = 1024
BATCH = 16
SEQ = 256
DEPTH = 2
DEC_BATCH = 2
DEC_SEQ = 1024
PAST_LEN = 256

GRID_W = 64
HEAD_DIM = 64
Q_BLOCK = 128
WINDOW = 128
ROPE_THETA = 10000.0
EPS = 1e-6
NEG_INF = -1e30
A_HEADS = D_MODEL // (2 * HEAD_DIM)
A_KV_HEADS = 2
B_HEADS = D_MODEL // (2 * HEAD_DIM)
B_KV_HEADS = 2
C_HEADS = D_MODEL // (2 * HEAD_DIM)
C_VDIM = 2 * HEAD_DIM
A_W = A_HEADS * HEAD_DIM
B_W = B_HEADS * HEAD_DIM
A_KVW = A_KV_HEADS * HEAD_DIM
B_KVW = B_KV_HEADS * HEAD_DIM
C_QW = C_HEADS * 2 * HEAD_DIM
C_W = C_HEADS * C_VDIM
EVEN_IN = 2 * A_W + 2 * A_KVW + 2 * B_W + 2 * B_KVW
ODD_IN = 2 * C_QW + 2 * C_W
N_EVEN = (DEPTH + 1) // 2
N_ODD = DEPTH // 2
ALPHA = (2 * DEPTH) ** 0.25
BETA = (8 * DEPTH) ** -0.25

kernel_name = 'hybrid_diffusion_prefix_trunk_step'


def _split(p, sizes):
    idx = [int(v) for v in np.cumsum(sizes)[:-1]]
    return jnp.split(p, idx, axis=-1)


def rms_norm(x, w):
    xf = x.astype(jnp.float32)
    y = xf * lax.rsqrt(jnp.mean(xf * xf, -1, keepdims=True) + EPS)
    return (y * w.astype(jnp.float32)).astype(x.dtype)


def layer_norm(x, g, b):
    xf = x.astype(jnp.float32)
    mu = jnp.mean(xf, -1, keepdims=True)
    xc = xf - mu
    var = jnp.mean(xc * xc, -1, keepdims=True)
    y = xc * lax.rsqrt(var + EPS) * g.astype(jnp.float32) + b.astype(jnp.float32)
    return y.astype(x.dtype)


def adaln(cvec, w_mod, b_mod):
    m = jax.nn.silu(cvec) @ w_mod + b_mod
    if m.ndim == 2:
        m = m[:, None, :]
    return jnp.split(m, 3, axis=-1)


def axial_angles(n_tokens):
    t = jnp.arange(n_tokens)
    row = (t // GRID_W).astype(jnp.float32)
    col = (t % GRID_W).astype(jnp.float32)
    n_freq = HEAD_DIM // 4
    freqs = ROPE_THETA ** (-jnp.arange(n_freq, dtype=jnp.float32) / n_freq)
    return row[:, None] * freqs, col[:, None] * freqs


def _rot(x, ang):
    m = ang.shape[-1]
    c = jnp.cos(ang)[:, None, :]
    s = jnp.sin(ang)[:, None, :]
    x1, x2 = x[..., :m], x[..., m:]
    return jnp.concatenate([x1 * c - x2 * s, x1 * s + x2 * c], -1)


def rope_2d(x, ang_row, ang_col):
    half = HEAD_DIM // 2
    xf = x.astype(jnp.float32)
    out = jnp.concatenate([_rot(xf[..., :half], ang_row), _rot(xf[..., half:], ang_col)], -1)
    return out.astype(x.dtype)


def dense_attention(q, k, v, sink=None):
    b, sq, kvh, g, dq = q.shape
    nb = sq // Q_BLOCK
    scale = dq ** -0.5
    kf = k.astype(jnp.float32)
    vf = v.astype(jnp.float32)
    qb = q.reshape(b, nb, Q_BLOCK, kvh, g, dq).transpose(1, 0, 2, 3, 4, 5)

    def block(qblk):
        s = jnp.einsum('bqkgd,bskd->bkgqs', qblk.astype(jnp.float32), kf) * scale
        if sink is not None:
            sk = jnp.broadcast_to(sink.astype(jnp.float32)[None, :, :, None, None], s.shape[:-1] + (1,))
            p = jax.nn.softmax(jnp.concatenate([s, sk], -1), axis=-1)[..., :-1]
        else:
            p = jax.nn.softmax(s, axis=-1)
        return jnp.einsum('bkgqs,bskd->bqkgd', p, vf)

    o = lax.map(block, qb)
    return o.transpose(1, 0, 2, 3, 4, 5).reshape(b, sq, kvh, g, -1).astype(q.dtype)


def window_attention(q, k, v, k_ctx, v_ctx, sink):
    b, s, kvh, g, d = q.shape
    nb = s // Q_BLOCK
    scale = d ** -0.5
    pad = ((0, 0), (Q_BLOCK, Q_BLOCK), (0, 0), (0, 0))
    kp = jnp.pad(k.astype(jnp.float32), pad).reshape(b, nb + 2, Q_BLOCK, kvh, d)
    vp = jnp.pad(v.astype(jnp.float32), pad).reshape(b, nb + 2, Q_BLOCK, kvh, d)
    kband = jnp.concatenate([kp[:, :-2], kp[:, 1:-1], kp[:, 2:]], axis=2)
    vband = jnp.concatenate([vp[:, :-2], vp[:, 1:-1], vp[:, 2:]], axis=2)
    qb = q.astype(jnp.float32).reshape(b, nb, Q_BLOCK, kvh, g, d)
    s_loc = jnp.einsum('bnqkgd,bnskd->bnkgqs', qb, kband) * scale
    blk = jnp.arange(nb)[:, None, None]
    i = blk * Q_BLOCK + jnp.arange(Q_BLOCK)[None, :, None]
    j = blk * Q_BLOCK - Q_BLOCK + jnp.arange(3 * Q_BLOCK)[None, None, :]
    valid = (jnp.abs(j - i) <= WINDOW) & (j >= 0) & (j < s)
    s_loc = jnp.where(valid[None, :, None, None], s_loc, NEG_INF)
    s_ctx = jnp.einsum('bnqkgd,bskd->bnkgqs', qb, k_ctx.astype(jnp.float32)) * scale
    sk = jnp.broadcast_to(sink.astype(jnp.float32)[None, None, :, :, None, None], s_loc.shape[:-1] + (1,))
    p = jax.nn.softmax(jnp.concatenate([s_loc, s_ctx, sk], -1), axis=-1)
    n_loc = 3 * Q_BLOCK
    o = (jnp.einsum('bnkgqs,bnskd->bnqkgd', p[..., :n_loc], vband)
         + jnp.einsum('bnkgqs,bskd->bnqkgd', p[..., n_loc:-1], v_ctx.astype(jnp.float32)))
    return o.reshape(b, s, kvh, g, d).astype(q.dtype)


def even_mixer(h, w_in, w_out, q_norm, k_norm, sink, angles, kv_ctx):
    b, s, _ = h.shape
    qa, ka, va, ga, qb, kb, vb, gb = _split(h @ w_in, [A_W, A_KVW, A_KVW, A_W, B_W, B_KVW, B_KVW, B_W])
    qa = rms_norm(qa.reshape(b, s, A_HEADS, HEAD_DIM), q_norm)
    ka = rms_norm(ka.reshape(b, s, A_KV_HEADS, HEAD_DIM), k_norm)
    va = va.reshape(b, s, A_KV_HEADS, HEAD_DIM)
    qb = qb.reshape(b, s, B_HEADS, HEAD_DIM)
    kb = kb.reshape(b, s, B_KV_HEADS, HEAD_DIM)
    vb = vb.reshape(b, s, B_KV_HEADS, HEAD_DIM)
    new_kv = (ka, va, kb, vb)
    sink_g = sink.reshape(B_KV_HEADS, B_HEADS // B_KV_HEADS)
    if angles is None:
        oa = dense_attention(qa.reshape(b, s, A_KV_HEADS, -1, HEAD_DIM), ka, va)
        ob = dense_attention(qb.reshape(b, s, B_KV_HEADS, -1, HEAD_DIM), kb, vb, sink_g)
    else:
        ka_c, va_c, kb_c, vb_c = kv_ctx
        qa = rope_2d(qa, *angles)
        ka_r = rope_2d(ka, *angles)
        qb = rope_2d(qb, *angles)
        kb_r = rope_2d(kb, *angles)
        oa = dense_attention(qa.reshape(b, s, A_KV_HEADS, -1, HEAD_DIM),
                             jnp.concatenate([ka_r, ka_c.astype(ka_r.dtype)], 1),
                             jnp.concatenate([va, va_c.astype(va.dtype)], 1))
        ob = window_attention(qb.reshape(b, s, B_KV_HEADS, -1, HEAD_DIM), kb_r, vb, kb_c, vb_c, sink_g)
    oa = oa.reshape(b, s, A_W) * jax.nn.silu(ga)
    ob = ob.reshape(b, s, B_W) * jax.nn.silu(gb)
    return jnp.concatenate([oa, ob], -1) @ w_out, new_kv


def odd_mixer(h, w_in, w_out, lq1, lk1, lq2, lk2, subln, lam_init, angles, kv_ctx):
    b, s, _ = h.shape
    q, k, v, g = _split(h @ w_in, [C_QW, C_QW, C_W, C_W])
    q = q.reshape(b, s, 2 * C_HEADS, HEAD_DIM)
    k = k.reshape(b, s, 2 * C_HEADS, HEAD_DIM)
    v = v.reshape(b, s, C_HEADS, C_VDIM)
    if angles is not None:
        q = rope_2d(q, *angles)
        k = rope_2d(k, *angles)
    k = k.reshape(b, s, C_HEADS, 2 * HEAD_DIM)
    new_kv = (k, v)
    if kv_ctx is None:
        k_all, v_all = k, v
    else:
        k_all = jnp.concatenate([k, kv_ctx[0].astype(k.dtype)], 1)
        v_all = jnp.concatenate([v, kv_ctx[1].astype(v.dtype)], 1)
    q = q.reshape(b, s, C_HEADS, 1, 2 * HEAD_DIM)
    o1 = dense_attention(q[..., :HEAD_DIM], k_all[..., :HEAD_DIM], v_all)
    o2 = dense_attention(q[..., HEAD_DIM:], k_all[..., HEAD_DIM:], v_all)
    lam = (jnp.exp(jnp.sum(lq1.astype(jnp.float32) * lk1.astype(jnp.float32)))
           - jnp.exp(jnp.sum(lq2.astype(jnp.float32) * lk2.astype(jnp.float32))) + lam_init)
    o = (o1.astype(jnp.float32) - lam * o2.astype(jnp.float32))[:, :, :, 0, :]
    o = rms_norm(o, subln) * (1.0 - lam_init)
    o = o.reshape(b, s, C_W).astype(h.dtype) * jax.nn.silu(g)
    return o @ w_out, new_kv


def setup_inputs(seed: int = 0) -> dict:
    key = jax.random.key(seed)
    ks = jax.random.split(key, 32)
    f32 = jnp.float32
    nrm = lambda k, shape, sc=1.0: (jax.random.normal(k, shape, f32) * sc).astype(f32)
    d = D_MODEL
    return {
        'x_prompt': nrm(ks[0], (BATCH, SEQ, d)),
        'x_sample': nrm(ks[1], (DEC_BATCH, DEC_SEQ, d)),
        'cache_a_k': nrm(ks[2], (DEC_BATCH, N_EVEN, PAST_LEN, A_KV_HEADS, HEAD_DIM)),
        'cache_a_v': nrm(ks[3], (DEC_BATCH, N_EVEN, PAST_LEN, A_KV_HEADS, HEAD_DIM)),
        'cache_b_k': nrm(ks[4], (DEC_BATCH, N_EVEN, PAST_LEN, B_KV_HEADS, HEAD_DIM)),
        'cache_b_v': nrm(ks[5], (DEC_BATCH, N_EVEN, PAST_LEN, B_KV_HEADS, HEAD_DIM)),
        'cache_c_k': nrm(ks[6], (DEC_BATCH, N_ODD, PAST_LEN, C_HEADS, 2 * HEAD_DIM)),
        'cache_c_v': nrm(ks[7], (DEC_BATCH, N_ODD, PAST_LEN, C_HEADS, C_VDIM)),
        'c': nrm(ks[8], (DEC_BATCH, d)),
        'c_ctx': nrm(ks[9], (d,)),
        'w_mod': nrm(ks[10], (DEPTH, d, 3 * d), 0.5 * d ** -0.5),
        'b_mod': nrm(ks[11], (DEPTH, 3 * d), 0.01),
        'ln_g': 1.0 + nrm(ks[12], (DEPTH, d), 0.05),
        'ln_b': nrm(ks[13], (DEPTH, d), 0.01),
        'w_in_even': nrm(ks[14], (N_EVEN, d, EVEN_IN), d ** -0.5),
        'w_out_even': nrm(ks[15], (N_EVEN, A_W + B_W, d), BETA * (A_W + B_W) ** -0.5),
        'q_norm_a': 1.0 + nrm(ks[16], (N_EVEN, HEAD_DIM), 0.05),
        'k_norm_a': 1.0 + nrm(ks[17], (N_EVEN, HEAD_DIM), 0.05),
        'sink_b': nrm(ks[18], (N_EVEN, B_HEADS)),
        'w_in_odd': nrm(ks[19], (N_ODD, d, ODD_IN), d ** -0.5),
        'w_out_odd': nrm(ks[20], (N_ODD, C_W, d), BETA * C_W ** -0.5),
        'lambda_q1': nrm(ks[21], (N_ODD, HEAD_DIM), 0.1),
        'lambda_k1': nrm(ks[22], (N_ODD, HEAD_DIM), 0.1),
        'lambda_q2': nrm(ks[23], (N_ODD, HEAD_DIM), 0.1),
        'lambda_k2': nrm(ks[24], (N_ODD, HEAD_DIM), 0.1),
        'subln_c': 1.0 + nrm(ks[25], (N_ODD, C_VDIM), 0.05),
    }


def reference(x_prompt, x_sample, cache_a_k, cache_a_v, cache_b_k, cache_b_v, cache_c_k, cache_c_v,
              c, c_ctx, w_mod, b_mod, ln_g, ln_b, w_in_even, w_out_even, q_norm_a, k_norm_a, sink_b,
              w_in_odd, w_out_odd, lambda_q1, lambda_k1, lambda_q2, lambda_k2, subln_c):
    x = x_prompt
    a_k, a_v, b_k, b_v, c_k, c_v = [], [], [], [], [], []
    for l in range(DEPTH):
        shift, scale, gate = adaln(c_ctx, w_mod[l], b_mod[l])
        h = x * (1 + scale) + shift
        if l % 2 == 0:
            e = l // 2
            out, (ka, va, kb, vb) = even_mixer(h, w_in_even[e], w_out_even[e], q_norm_a[e], k_norm_a[e],
                                               sink_b[e], None, None)
            a_k.append(ka); a_v.append(va); b_k.append(kb); b_v.append(vb)
        else:
            o = l // 2
            lam_init = 0.8 - 0.6 * math.exp(-0.3 * l)
            out, (kc, vc) = odd_mixer(h, w_in_odd[o], w_out_odd[o], lambda_q1[o], lambda_k1[o],
                                      lambda_q2[o], lambda_k2[o], subln_c[o], lam_init, None, None)
            c_k.append(kc); c_v.append(vc)
        x = layer_norm(ALPHA * x + gate * out, ln_g[l], ln_b[l])
    y_prompt = x
    new_a_k = jnp.stack(a_k, axis=1)
    new_a_v = jnp.stack(a_v, axis=1)
    new_b_k = jnp.stack(b_k, axis=1)
    new_b_v = jnp.stack(b_v, axis=1)
    new_c_k = jnp.stack(c_k, axis=1)
    new_c_v = jnp.stack(c_v, axis=1)

    angles = axial_angles(x_sample.shape[1])
    x = x_sample
    for l in range(DEPTH):
        shift, scale, gate = adaln(c, w_mod[l], b_mod[l])
        h = x * (1 + scale) + shift
        if l % 2 == 0:
            e = l // 2
            kv_ctx = (cache_a_k[:, e], cache_a_v[:, e], cache_b_k[:, e], cache_b_v[:, e])
            out, _ = even_mixer(h, w_in_even[e], w_out_even[e], q_norm_a[e], k_norm_a[e],
                                sink_b[e], angles, kv_ctx)
        else:
            o = l // 2
            lam_init = 0.8 - 0.6 * math.exp(-0.3 * l)
            out, _ = odd_mixer(h, w_in_odd[o], w_out_odd[o], lambda_q1[o], lambda_k1[o],
                               lambda_q2[o], lambda_k2[o], subln_c[o], lam_init, angles,
                               (cache_c_k[:, o], cache_c_v[:, o]))
        x = layer_norm(ALPHA * x + gate * out, ln_g[l], ln_b[l])
    y_sample = x
    return (y_prompt, y_sample, new_a_k, new_a_v, new_b_k, new_b_v, new_c_k, new_c_v)
```

```python
import functools
import math

import jax
import jax.numpy as jnp
import numpy as np
from jax import lax
from jax.experimental import pallas as pl
from jax.experimental.pallas import tpu as pltpu

F32 = jnp.float32
BF16 = jnp.bfloat16

D_MODEL = 1024
HEAD_DIM = 64
GRID_W = 64
WINDOW = 128
ROPE_THETA = 10000.0
EPS = 1e-6
NEG_INF = -1e30
LOG2E = 1.4426950408889634
Q_SCALE = HEAD_DIM ** -0.5 * LOG2E
LANES = 128
ROW_CHUNK = 256
VMEM_LIMIT = 60000 * 1024

EVEN_IN = 2560
ODD_IN = 4096


def _silu(x):
    return x / (1.0 + jnp.exp(-x))


def _dot(a, b):
    return jnp.dot(a, b, preferred_element_type=F32)


def _dot_nt(a, b):
    return lax.dot_general(a, b, (((1,), (1,)), ((), ())), preferred_element_type=F32)


def _lane_iota(rows):
    return lax.broadcasted_iota(jnp.int32, (rows, LANES), 1)


def _rope(a, cos, sin_signed):
    lane = _lane_iota(a.shape[0])
    fwd = pltpu.roll(a, LANES - 16, 1)
    bwd = pltpu.roll(a, 16, 1)
    partner = jnp.where((lane & 16) == 0, fwd, bwd)
    return a * cos + partner * sin_signed


def _store_kv_variants(scr, rows, a):
    lane = _lane_iota(a.shape[0])
    lo = lane < HEAD_DIM
    swapped = pltpu.roll(a, HEAD_DIM, 1)
    zero = jnp.zeros_like(a)
    scr[0, rows, :] = jnp.where(lo, a, zero).astype(BF16)
    scr[1, rows, :] = jnp.where(lo, zero, swapped).astype(BF16)
    scr[2, rows, :] = jnp.where(lo, swapped, zero).astype(BF16)
    scr[3, rows, :] = jnp.where(lo, zero, a).astype(BF16)


def _layer_norm_rows(z, g, b):
    mu = jnp.mean(z, axis=-1, keepdims=True)
    zc = z - mu
    var = jnp.mean(zc * zc, axis=-1, keepdims=True)
    return zc * lax.rsqrt(var + EPS) * g + b


def _modulate(x_ref, mod_ref, mod_row, h_scr, n_rows):
    shift = mod_ref[0, pl.ds(mod_row, 1), 0:D_MODEL]
    scale = mod_ref[0, pl.ds(mod_row, 1), D_MODEL:2 * D_MODEL]

    def body(i, carry):
        rows = pl.ds(pl.multiple_of(i * ROW_CHUNK, ROW_CHUNK), ROW_CHUNK)
        h_scr[rows, :] = (x_ref[rows, :] * (1.0 + scale) + shift).astype(BF16)
        return carry

    lax.fori_loop(0, n_rows // ROW_CHUNK, body, 0)


def _out_proj_norm(x_ref, mod_ref, mod_row, attn_scr, w_out_ref, lng_ref, lnb_ref, y_ref, n_rows, alpha):
    gate = mod_ref[0, pl.ds(mod_row, 1), 2 * D_MODEL:3 * D_MODEL]
    g = lng_ref[0]
    b = lnb_ref[0]

    def body(i, carry):
        rows = pl.ds(pl.multiple_of(i * ROW_CHUNK, ROW_CHUNK), ROW_CHUNK)
        out = _dot(attn_scr[rows, :], w_out_ref[...])
        z = alpha * x_ref[rows, :] + gate * out
        y_ref[rows, :] = _layer_norm_rows(z, g, b)
        return carry

    lax.fori_loop(0, n_rows // ROW_CHUNK, body, 0)


def _softmax_heads(q, segs, sinks):
    acc = None
    inv = []
    for par in (0, 1):
        scores = []
        for k, _, mask in segs[par]:
            s = _dot_nt(q, k)
            if mask is not None:
                s = jnp.where(mask, s, NEG_INF)
            scores.append(s)
        m = jnp.max(scores[0], axis=1, keepdims=True)
        for s in scores[1:]:
            m = jnp.maximum(m, jnp.max(s, axis=1, keepdims=True))
        if sinks is not None:
            m = jnp.maximum(m, sinks[par])
        denom = None
        for s, (_, v, _) in zip(scores, segs[par]):
            p = jnp.exp2(s - m)
            psum = jnp.sum(p, axis=1, keepdims=True)
            denom = psum if denom is None else denom + psum
            pv = _dot(p.astype(BF16), v)
            acc = pv if acc is None else acc + pv
        if sinks is not None:
            denom = denom + jnp.exp2(sinks[par] - m)
        inv.append(1.0 / denom)
    lane = _lane_iota(q.shape[0])
    return acc * jnp.where(lane < HEAD_DIM, inv[0], inv[1])


def _even_kernel(latent, n_rows, seq, alpha, *refs):
    if latent:
        (x_ref, mod_ref, w_in_ref, w_out_ref, qn_ref, kn_ref, sink_ref, lng_ref, lnb_ref, pm_ref,
         cos_ref, sin_ref, cak_ref, cav_ref, cbk_ref, cbv_ref,
         y_ref,
         h_scr, qa_scr, qb_scr, ka_scr, va_scr, kb_scr, vb_scr, g_scr, attn_scr) = refs
    else:
        (x_ref, mod_ref, w_in_ref, w_out_ref, qn_ref, kn_ref, sink_ref, lng_ref, lnb_ref, pm_ref,
         y_ref, nak_ref, nav_ref, nbk_ref, nbv_ref,
         h_scr, qa_scr, qb_scr, ka_scr, va_scr, kb_scr, vb_scr, g_scr, attn_scr) = refs

    step = pl.program_id(0)
    mod_row = step + 1 if latent else 0
    _modulate(x_ref, mod_ref, mod_row, h_scr, n_rows)

    if latent:
        past = pl.ds(seq, cak_ref.shape[1])
        _store_kv_variants(ka_scr, past, cak_ref[0])
        _store_kv_variants(va_scr, past, cav_ref[0])
        _store_kv_variants(kb_scr, past, cbk_ref[0])
        _store_kv_variants(vb_scr, past, cbv_ref[0])

    pm = pm_ref[...]
    qn = qn_ref[...]
    kn = kn_ref[...]

    def proj(i, carry):
        rows = pl.ds(pl.multiple_of(i * ROW_CHUNK, ROW_CHUNK), ROW_CHUNK)
        hh = h_scr[rows, :]
        if latent:
            cos = cos_ref[rows, :]
            sin = sin_ref[rows, :]
            rot = lambda a: _rope(a, cos, sin)
        else:
            rot = lambda a: a

        def rms(a, w):
            ms = _dot((a * a).astype(BF16), pm)
            return a * lax.rsqrt(ms + EPS) * w

        acc = _dot(hh, w_in_ref[:, 0:512])
        for j in range(4):
            a = rot(rms(acc[:, LANES * j:LANES * (j + 1)], qn))
            qa_scr[rows, LANES * j:LANES * (j + 1)] = (a * Q_SCALE).astype(BF16)
        kv = _dot(hh, w_in_ref[:, 512:768])
        ka = rms(kv[:, 0:LANES], kn)
        va = kv[:, LANES:2 * LANES]
        if not latent:
            nak_ref[rows, :] = ka
            nav_ref[rows, :] = va
        _store_kv_variants(ka_scr, rows, rot(ka))
        _store_kv_variants(va_scr, rows, va)
        g_scr[rows, 0:512] = _silu(_dot(hh, w_in_ref[:, 768:1280]))

        acc = _dot(hh, w_in_ref[:, 1280:1792])
        for j in range(4):
            a = rot(acc[:, LANES * j:LANES * (j + 1)])
            qb_scr[rows, LANES * j:LANES * (j + 1)] = (a * Q_SCALE).astype(BF16)
        kv = _dot(hh, w_in_ref[:, 1792:2048])
        kb = kv[:, 0:LANES]
        vb = kv[:, LANES:2 * LANES]
        if not latent:
            nbk_ref[rows, :] = kb
            nbv_ref[rows, :] = vb
        _store_kv_variants(kb_scr, rows, rot(kb))
        _store_kv_variants(vb_scr, rows, vb)
        g_scr[rows, 512:1024] = _silu(_dot(hh, w_in_ref[:, 2048:2560]))
        return carry

    lax.fori_loop(0, n_rows // ROW_CHUNK, proj, 0)

    sinks = [sink_ref[h] * LOG2E for h in range(8)]

    def attend(i, carry):
        r0 = pl.multiple_of(i * ROW_CHUNK, ROW_CHUNK)
        rows = pl.ds(r0, ROW_CHUNK)
        if latent:
            k_all = pl.ds(0, seq + cak_ref.shape[1])
            k_past = pl.ds(seq, cak_ref.shape[1])
            win = 2 * ROW_CHUNK
            w0 = pl.multiple_of(jnp.clip(r0 - WINDOW, 0, seq - win), WINDOW)
            k_win = pl.ds(w0, win)
            qi = r0 + lax.broadcasted_iota(jnp.int32, (ROW_CHUNK, win), 0)
            kj = w0 + lax.broadcasted_iota(jnp.int32, (ROW_CHUNK, win), 1)
            band = jnp.abs(kj - qi) <= WINDOW
        else:
            k_all = rows
        for p in range(4):
            cols = slice(LANES * p, LANES * (p + 1))
            kvh = p // 2
            segs = [[(ka_scr[2 * kvh + par, k_all, :], va_scr[2 * kvh + par, k_all, :], None)]
                    for par in (0, 1)]
            o = _softmax_heads(qa_scr[rows, cols], segs, None)
            attn_scr[rows, cols] = (o * g_scr[rows, cols]).astype(BF16)
            if latent:
                segs = [[(kb_scr[2 * kvh + par, k_win, :], vb_scr[2 * kvh + par, k_win, :], band),
                         (kb_scr[2 * kvh + par, k_past, :], vb_scr[2 * kvh + par, k_past, :], None)]
                        for par in (0, 1)]
            else:
                segs = [[(kb_scr[2 * kvh + par, k_all, :], vb_scr[2 * kvh + par, k_all, :], None)]
                        for par in (0, 1)]
            o = _softmax_heads(qb_scr[rows, cols], segs, (sinks[2 * p], sinks[2 * p + 1]))
            bcols = slice(512 + LANES * p, 512 + LANES * (p + 1))
            attn_scr[rows, bcols] = (o * g_scr[rows, bcols]).astype(BF16)
        return carry

    lax.fori_loop(0, n_rows // ROW_CHUNK, attend, 0)

    _out_proj_norm(x_ref, mod_ref, mod_row, attn_scr, w_out_ref, lng_ref, lnb_ref, y_ref, n_rows, alpha)


def _odd_kernel(latent, n_rows, seq, alpha, lam_init, *refs):
    if latent:
        (x_ref, mod_ref, w_in_ref, w_out_ref, lq1_ref, lk1_ref, lq2_ref, lk2_ref, sub_ref, lng_ref, lnb_ref,
         cos_ref, sin_ref, cck_ref, ccv_ref,
         y_ref,
         h_scr, q_scr, ke_scr, ko_scr, v_scr, g_scr, attn_scr) = refs
    else:
        (x_ref, mod_ref, w_in_ref, w_out_ref, lq1_ref, lk1_ref, lq2_ref, lk2_ref, sub_ref, lng_ref, lnb_ref,
         y_ref, nck_ref, ncv_ref,
         h_scr, q_scr, ke_scr, ko_scr, v_scr, g_scr, attn_scr) = refs

    step = pl.program_id(0)
    mod_row = step + 1 if latent else 0
    _modulate(x_ref, mod_ref, mod_row, h_scr, n_rows)

    n_heads = D_MODEL // LANES
    lo = _lane_iota(ROW_CHUNK) < HEAD_DIM

    def store_k(rows, j, a):
        cols = slice(LANES * j, LANES * (j + 1))
        zero = jnp.zeros_like(a)
        ke_scr[rows, cols] = jnp.where(lo, a, zero).astype(BF16)
        ko_scr[rows, cols] = jnp.where(lo, zero, a).astype(BF16)

    if latent:
        n_past = cck_ref.shape[1]
        past = pl.ds(seq, n_past)
        for j in range(n_heads):
            cols = slice(LANES * j, LANES * (j + 1))
            store_k(past, j, cck_ref[0, :, cols])
        v_scr[past, :] = ccv_ref[0].astype(BF16)

    def proj(i, carry):
        rows = pl.ds(pl.multiple_of(i * ROW_CHUNK, ROW_CHUNK), ROW_CHUNK)
        hh = h_scr[rows, :]
        if latent:
            cos = cos_ref[rows, :]
            sin = sin_ref[rows, :]
            rot = lambda a: _rope(a, cos, sin)
        else:
            rot = lambda a: a
        for half in range(2):
            acc = _dot(hh, w_in_ref[:, 512 * half:512 * (half + 1)])
            for j in range(4):
                a = rot(acc[:, LANES * j:LANES * (j + 1)])
                cols = slice(512 * half + LANES * j, 512 * half + LANES * (j + 1))
                q_scr[rows, cols] = (a * Q_SCALE).astype(BF16)
        for half in range(2):
            acc = _dot(hh, w_in_ref[:, 1024 + 512 * half:1024 + 512 * (half + 1)])
            if not latent:
                nck_ref[rows, 512 * half:512 * (half + 1)] = acc
            for j in range(4):
                store_k(rows, 4 * half + j, rot(acc[:, LANES * j:LANES * (j + 1)]))
        for half in range(2):
            acc = _dot(hh, w_in_ref[:, 2048 + 512 * half:2048 + 512 * (half + 1)])
            if not latent:
                ncv_ref[rows, 512 * half:512 * (half + 1)] = acc
            v_scr[rows, 512 * half:512 * (half + 1)] = acc.astype(BF16)
        for half in range(2):
            acc = _dot(hh, w_in_ref[:, 3072 + 512 * half:3072 + 512 * (half + 1)])
            g_scr[rows, 512 * half:512 * (half + 1)] = _silu(acc)
        return carry

    lax.fori_loop(0, n_rows // ROW_CHUNK, proj, 0)

    lam = (jnp.exp(jnp.sum(lq1_ref[...] * lk1_ref[...], axis=1, keepdims=True))
           - jnp.exp(jnp.sum(lq2_ref[...] * lk2_ref[...], axis=1, keepdims=True)) + lam_init)
    sub = sub_ref[...] * (1.0 - lam_init)

    def attend(i, carry):
        r0 = pl.multiple_of(i * ROW_CHUNK, ROW_CHUNK)
        rows = pl.ds(r0, ROW_CHUNK)
        k_all = pl.ds(0, seq + cck_ref.shape[1]) if latent else rows
        for h in range(n_heads):
            cols = slice(LANES * h, LANES * (h + 1))
            q = q_scr[rows, cols]
            s1 = _dot_nt(q, ke_scr[k_all, cols])
            s2 = _dot_nt(q, ko_scr[k_all, cols])
            e1 = jnp.exp2(s1 - jnp.max(s1, axis=1, keepdims=True))
            e2 = jnp.exp2(s2 - jnp.max(s2, axis=1, keepdims=True))
            w1 = 1.0 / jnp.sum(e1, axis=1, keepdims=True)
            w2 = lam / jnp.sum(e2, axis=1, keepdims=True)
            pd = (e1 * w1 - e2 * w2).astype(BF16)
            o = _dot(pd, v_scr[k_all, cols])
            ms = jnp.mean(o * o, axis=1, keepdims=True)
            o = o * lax.rsqrt(ms + EPS) * sub
            attn_scr[rows, cols] = (o * g_scr[rows, cols]).astype(BF16)
        return carry

    lax.fori_loop(0, n_rows // ROW_CHUNK, attend, 0)

    _out_proj_norm(x_ref, mod_ref, mod_row, attn_scr, w_out_ref, lng_ref, lnb_ref, y_ref, n_rows, alpha)


def _mod_kernel(cv_ref, w_ref, b_ref, o_ref):
    s = _silu(cv_ref[...])
    o_ref[0] = jnp.dot(s, w_ref[0], precision=lax.Precision.HIGHEST, preferred_element_type=F32) + b_ref[0]


def _full(shape):
    zeros = (0,) * len(shape)
    return pl.BlockSpec(shape, lambda i: zeros)


def _rope_tables(seq):
    t = np.arange(seq)
    n_freq = HEAD_DIM // 4
    freqs = ROPE_THETA ** (-np.arange(n_freq, dtype=np.float64) / n_freq)
    ang_row = (t // GRID_W)[:, None] * freqs
    ang_col = (t % GRID_W)[:, None] * freqs
    ang = np.concatenate([ang_row, ang_row, ang_col, ang_col], axis=1)
    sign = np.concatenate([-np.ones(n_freq), np.ones(n_freq)] * 2)[None, :]
    cos = np.tile(np.cos(ang), (1, 2)).astype(np.float32)
    sin = np.tile(np.sin(ang) * sign, (1, 2)).astype(np.float32)
    return jnp.asarray(cos), jnp.asarray(sin)


def _head_mean_matrix():
    idx = np.arange(LANES) // HEAD_DIM
    return jnp.asarray((idx[:, None] == idx[None, :]).astype(np.float32) / HEAD_DIM, dtype=BF16)


def _modulation(c, c_ctx, w_mod, b_mod):
    depth = w_mod.shape[0]
    cv = jnp.concatenate([c_ctx[None, :], c, jnp.zeros((8 - 1 - c.shape[0], D_MODEL), F32)], axis=0)
    n_blk = 3 * D_MODEL // 1024
    return pl.pallas_call(
        _mod_kernel,
        grid=(depth, n_blk),
        in_specs=[pl.BlockSpec((8, D_MODEL), lambda l, n: (0, 0)),
                  pl.BlockSpec((1, D_MODEL, 1024), lambda l, n: (l, 0, n)),
                  pl.BlockSpec((1, 1, 1024), lambda l, n: (l, 0, n))],
        out_specs=pl.BlockSpec((1, 8, 1024), lambda l, n: (l, 0, n)),
        out_shape=jax.ShapeDtypeStruct((depth, 8, 3 * D_MODEL), F32),
        compiler_params=pltpu.CompilerParams(dimension_semantics=("arbitrary", "arbitrary")),
        name="adaln_modulation",
    )(cv, w_mod, b_mod.reshape(depth, 1, 3 * D_MODEL))


def _even_layer(x, mod, layer, w_in, w_out, qn, kn, sink, ln_g, ln_b, latent, seq, n_rows, alpha, extras=()):
    total = x.shape[0]
    grid = (total // n_rows,)
    row_blk = lambda width: pl.BlockSpec((n_rows, width), lambda i: (i, 0))
    in_specs = [row_blk(D_MODEL),
                pl.BlockSpec((1, 8, 3 * D_MODEL), lambda i: (layer, 0, 0)),
                _full((D_MODEL, EVEN_IN)), _full((D_MODEL, D_MODEL)),
                _full((1, LANES)), _full((1, LANES)),
                pl.BlockSpec(memory_space=pltpu.SMEM),
                pl.BlockSpec((1, 1, D_MODEL), lambda i: (layer, 0, 0)),
                pl.BlockSpec((1, 1, D_MODEL), lambda i: (layer, 0, 0)),
                _full((LANES, LANES))]
    args = [x, mod, w_in, w_out, qn, kn, sink, ln_g, ln_b, _head_mean_matrix()]
    y_shape = jax.ShapeDtypeStruct((total, D_MODEL), F32)
    if latent:
        cos, sin, cak, cav, cbk, cbv = extras
        n_past = cak.shape[1]
        in_specs += [_full((seq, LANES)), _full((seq, LANES))]
        in_specs += [pl.BlockSpec((1, n_past, LANES), lambda i: (i, 0, 0))] * 4
        args += [cos, sin, cak, cav, cbk, cbv]
        out_specs = row_blk(D_MODEL)
        out_shape = y_shape
        n_keys = seq + n_past
    else:
        out_specs = [row_blk(D_MODEL)] + [row_blk(LANES)] * 4
        out_shape = [y_shape] + [jax.ShapeDtypeStruct((total, LANES), F32)] * 4
        n_keys = n_rows
    scratch = [pltpu.VMEM((n_rows, D_MODEL), BF16),
               pltpu.VMEM((n_rows, 512), BF16), pltpu.VMEM((n_rows, 512), BF16),
               pltpu.VMEM((4, n_keys, LANES), BF16), pltpu.VMEM((4, n_keys, LANES), BF16),
               pltpu.VMEM((4, n_keys, LANES), BF16), pltpu.VMEM((4, n_keys, LANES), BF16),
               pltpu.VMEM((n_rows, D_MODEL), F32),
               pltpu.VMEM((n_rows, D_MODEL), BF16)]
    return pl.pallas_call(
        functools.partial(_even_kernel, latent, n_rows, seq, alpha),
        grid=grid, in_specs=in_specs, out_specs=out_specs, out_shape=out_shape,
        scratch_shapes=scratch,
        compiler_params=pltpu.CompilerParams(dimension_semantics=("arbitrary",), vmem_limit_bytes=VMEM_LIMIT),
        name="even_layer_latent" if latent else "even_layer_context",
    )(*args)


def _odd_layer(x, mod, layer, w_in, w_out, lams, sub, ln_g, ln_b, latent, seq, n_rows, alpha, lam_init, extras=()):
    total = x.shape[0]
    grid = (total // n_rows,)
    row_blk = lambda width: pl.BlockSpec((n_rows, width), lambda i: (i, 0))
    single = pl.Buffered(1)
    in_specs = [row_blk(D_MODEL),
                pl.BlockSpec((1, 8, 3 * D_MODEL), lambda i: (layer, 0, 0)),
                pl.BlockSpec((D_MODEL, ODD_IN), lambda i: (0, 0), pipeline_mode=single),
                pl.BlockSpec((D_MODEL, D_MODEL), lambda i: (0, 0), pipeline_mode=single),
                _full((1, HEAD_DIM)), _full((1, HEAD_DIM)), _full((1, HEAD_DIM)), _full((1, HEAD_DIM)),
                _full((1, LANES)),
                pl.BlockSpec((1, 1, D_MODEL), lambda i: (layer, 0, 0)),
                pl.BlockSpec((1, 1, D_MODEL), lambda i: (layer, 0, 0))]
    args = [x, mod, w_in, w_out, *lams, sub, ln_g, ln_b]
    y_shape = jax.ShapeDtypeStruct((total, D_MODEL), F32)
    if latent:
        cos, sin, cck, ccv = extras
        n_past = cck.shape[1]
        in_specs += [_full((seq, LANES)), _full((seq, LANES))]
        in_specs += [pl.BlockSpec((1, n_past, D_MODEL), lambda i: (i, 0, 0))] * 2
        args += [cos, sin, cck, ccv]
        out_specs = row_blk(D_MODEL)
        out_shape = y_shape
        n_keys = seq + n_past
    else:
        out_specs = [row_blk(D_MODEL)] * 3
        out_shape = [y_shape] * 3
        n_keys = n_rows
    scratch = [pltpu.VMEM((n_rows, D_MODEL), BF16),
               pltpu.VMEM((n_rows, D_MODEL), BF16),
               pltpu.VMEM((n_keys, D_MODEL), BF16), pltpu.VMEM((n_keys, D_MODEL), BF16),
               pltpu.VMEM((n_keys, D_MODEL), BF16),
               pltpu.VMEM((n_rows, D_MODEL), F32),
               pltpu.VMEM((n_rows, D_MODEL), BF16)]
    return pl.pallas_call(
        functools.partial(_odd_kernel, latent, n_rows, seq, alpha, lam_init),
        grid=grid, in_specs=in_specs, out_specs=out_specs, out_shape=out_shape,
        scratch_shapes=scratch,
        compiler_params=pltpu.CompilerParams(dimension_semantics=("arbitrary",), vmem_limit_bytes=VMEM_LIMIT),
        name="odd_layer_latent" if latent else "odd_layer_context",
    )(*args)


def kernel(x_prompt, x_sample, cache_a_k, cache_a_v, cache_b_k, cache_b_v, cache_c_k, cache_c_v, c, c_ctx,
           w_mod, b_mod, ln_g, ln_b, w_in_even, w_out_even, q_norm_a, k_norm_a, sink_b, w_in_odd, w_out_odd,
           lambda_q1, lambda_k1, lambda_q2, lambda_k2, subln_c):
    depth = w_mod.shape[0]
    batch, seq, _ = x_prompt.shape
    dec_batch, dec_seq, _ = x_sample.shape
    n_past = cache_a_k.shape[2]
    alpha = (2 * depth) ** 0.25

    mod = _modulation(c, c_ctx, w_mod, b_mod)
    ln_g3 = ln_g.reshape(depth, 1, D_MODEL)
    ln_b3 = ln_b.reshape(depth, 1, D_MODEL)
    cos, sin = _rope_tables(dec_seq)

    def run(x, latent, s, rows_even, rows_odd):
        kv = {"a_k": [], "a_v": [], "b_k": [], "b_v": [], "c_k": [], "c_v": []}
        for l in range(depth):
            if l % 2 == 0:
                e = l // 2
                extras = ()
                if latent:
                    flat = lambda t: t[:, e].reshape(dec_batch, n_past, LANES)
                    extras = (cos, sin, flat(cache_a_k), flat(cache_a_v), flat(cache_b_k), flat(cache_b_v))
                res = _even_layer(x, mod, l, w_in_even[e].astype(BF16), w_out_even[e].astype(BF16),
                                  jnp.tile(q_norm_a[e], 2)[None, :], jnp.tile(k_norm_a[e], 2)[None, :],
                                  sink_b[e], ln_g3, ln_b3, latent, s, rows_even, alpha, extras)
                if latent:
                    x = res
                else:
                    x = res[0]
                    for name, t in zip(("a_k", "a_v", "b_k", "b_v"), res[1:]):
                        kv[name].append(t.reshape(batch, s, 2, HEAD_DIM))
            else:
                o = l // 2
                lam_init = 0.8 - 0.6 * math.exp(-0.3 * l)
                extras = ()
                if latent:
                    flat = lambda t: t[:, o].reshape(dec_batch, n_past, D_MODEL)
                    extras = (cos, sin, flat(cache_c_k), flat(cache_c_v))
                lams = [t[o][None, :] for t in (lambda_q1, lambda_k1, lambda_q2, lambda_k2)]
                res = _odd_layer(x, mod, l, w_in_odd[o].astype(BF16), w_out_odd[o].astype(BF16), lams,
                                 subln_c[o][None, :], ln_g3, ln_b3, latent, s, rows_odd, alpha, lam_init, extras)
                if latent:
                    x = res
                else:
                    x = res[0]
                    kv["c_k"].append(res[1].reshape(batch, s, 8, LANES))
                    kv["c_v"].append(res[2].reshape(batch, s, 8, LANES))
        return x, kv

    y_ctx, kv = run(x_prompt.reshape(batch * seq, D_MODEL), False, seq, 1024, 512)
    y_lat, _ = run(x_sample.reshape(dec_batch * dec_seq, D_MODEL), True, dec_seq, dec_seq, dec_seq)

    stack = lambda name: jnp.stack(kv[name], axis=1)
    return (y_ctx.reshape(batch, seq, D_MODEL), y_lat.reshape(dec_batch, dec_seq, D_MODEL),
            stack("a_k"), stack("a_v"), stack("b_k"), stack("b_v"), stack("c_k"), stack("c_v"))
```

```python
import functools
import math

import jax
import jax.numpy as jnp
import numpy as np
from jax import lax
from jax.experimental import pallas as pl
from jax.experimental.pallas import tpu as pltpu

F32 = jnp.float32
BF16 = jnp.bfloat16

D_MODEL = 1024
HEAD_DIM = 64
GRID_W = 64
WINDOW = 128
ROPE_THETA = 10000.0
EPS = 1e-6
NEG_INF = -1e30
LOG2E = 1.4426950408889634
Q_SCALE = HEAD_DIM ** -0.5 * LOG2E
LANES = 128
ROW_CHUNK = 256
SOFTMAX_VREGS = 40
VMEM_LIMIT = 60000 * 1024


def _silu(x):
    return x / (1.0 + jnp.exp(-x))


def _dot(a, b):
    return jnp.dot(a, b, preferred_element_type=F32)


def _dot_nt(a, b):
    return lax.dot_general(a, b, (((1,), (1,)), ((), ())), preferred_element_type=F32)


def _lane_iota(rows):
    return lax.broadcasted_iota(jnp.int32, (rows, LANES), 1)


def _chunk_rows(i):
    return pl.ds(pl.multiple_of(i * ROW_CHUNK, ROW_CHUNK), ROW_CHUNK)


def _softmax_rows(n_cols):
    rows = 8
    while rows * 2 * n_cols <= SOFTMAX_VREGS * 1024 and rows * 2 <= ROW_CHUNK:
        rows *= 2
    return rows


def _rope(a, cos, sin_signed):
    lane = _lane_iota(a.shape[0])
    fwd = pltpu.roll(a, LANES - 16, 1)
    bwd = pltpu.roll(a, 16, 1)
    partner = jnp.where((lane & 16) == 0, fwd, bwd)
    return a * cos + partner * sin_signed


def _rope_t(a, cos_t, sin_t):
    blocks = [a[16 * b:16 * (b + 1), :] for b in range(a.shape[0] // 16)]
    partner = jnp.concatenate([blocks[b ^ 1] for b in range(len(blocks))], axis=0)
    return a * cos_t + partner * sin_t


def _store_kt_variants(scr, chunk, kt):
    zero = jnp.zeros((HEAD_DIM, kt.shape[1]), F32)
    for j in range(2):
        kj = kt[HEAD_DIM * j:HEAD_DIM * (j + 1), :]
        scr[2 * j, chunk] = jnp.concatenate([kj, zero], axis=0).astype(BF16)
        scr[2 * j + 1, chunk] = jnp.concatenate([zero, kj], axis=0).astype(BF16)


def _store_v_variants(scr, rows, a):
    lane = _lane_iota(a.shape[0])
    lo = lane < HEAD_DIM
    swapped = pltpu.roll(a, HEAD_DIM, 1)
    one = jnp.ones_like(a)
    scr[0, rows, :] = jnp.where(lo, a, one).astype(BF16)
    scr[1, rows, :] = jnp.where(lo, one, swapped).astype(BF16)
    scr[2, rows, :] = jnp.where(lo, swapped, one).astype(BF16)
    scr[3, rows, :] = jnp.where(lo, one, a).astype(BF16)


def _layer_norm_rows(z, g, b):
    mu = jnp.mean(z, axis=-1, keepdims=True)
    zc = z - mu
    var = jnp.mean(zc * zc, axis=-1, keepdims=True)
    return zc * lax.rsqrt(var + EPS) * g + b


def _modulate(x_ref, mod_ref, mod_row, h_scr, n_rows):
    shift = mod_ref[0, pl.ds(mod_row, 1), 0:D_MODEL]
    scale = mod_ref[0, pl.ds(mod_row, 1), D_MODEL:2 * D_MODEL]

    def body(i, carry):
        rows = _chunk_rows(i)
        h_scr[rows, :] = (x_ref[rows, :] * (1.0 + scale) + shift).astype(BF16)
        return carry

    lax.fori_loop(0, n_rows // ROW_CHUNK, body, 0)


def _out_proj_norm(x_ref, mod_ref, mod_row, attn_scr, w_out_ref, lng_ref, lnb_ref, y_ref, n_rows, alpha):
    gate = mod_ref[0, pl.ds(mod_row, 1), 2 * D_MODEL:3 * D_MODEL]
    g = lng_ref[0]
    b = lnb_ref[0]

    def body(i, carry):
        rows = _chunk_rows(i)
        out = _dot(attn_scr[rows, :], w_out_ref[...])
        z = alpha * x_ref[rows, :] + gate * out
        y_ref[rows, :] = _layer_norm_rows(z, g, b)
        return carry

    lax.fori_loop(0, n_rows // ROW_CHUNK, body, 0, unroll=2)


def _run_pipeline(n_items, stages):
    for u in range(n_items + len(stages) - 1):
        for k, stage in enumerate(stages):
            t = u - k
            if 0 <= t < n_items:
                stage(t, t % 2)


def _even_kernel(latent, n_rows, seq, alpha, *refs):
    if latent:
        (x_ref, mod_ref, w_main_ref, w_kvt_ref, w_out_ref, qn_ref, knt_ref, sink_ref, lng_ref, lnb_ref, pm_ref,
         cos_ref, sin_ref, cost_ref, sint_ref, cakt_ref, cav_ref, cbkt_ref, cbv_ref,
         y_ref,
         ha_scr, qa_scr, qb_scr, ka_scr, va_scr, kb_scr, vb_scr, g_scr, s_scr, p_scr, es_scr, bias_scr) = refs
    else:
        (x_ref, mod_ref, w_main_ref, w_kvt_ref, w_out_ref, qn_ref, knt_ref, sink_ref, lng_ref, lnb_ref, pm_ref,
         y_ref, nakt_ref, navt_ref, nbkt_ref, nbvt_ref,
         ha_scr, qa_scr, qb_scr, ka_scr, va_scr, kb_scr, vb_scr, g_scr, s_scr, p_scr, es_scr) = refs

    step = pl.program_id(0)
    mod_row = step + 1 if latent else 0
    _modulate(x_ref, mod_ref, mod_row, ha_scr, n_rows)

    n_lat_chunks = seq // ROW_CHUNK
    if latent:
        n_past = cav_ref.shape[1]
        past_rows = pl.ds(seq, n_past)
        _store_kt_variants(ka_scr, n_lat_chunks, cakt_ref[0])
        _store_kt_variants(kb_scr, n_lat_chunks, cbkt_ref[0])
        _store_v_variants(va_scr, past_rows, cav_ref[0])
        _store_v_variants(vb_scr, past_rows, cbv_ref[0])

    pm = pm_ref[...]
    qn = qn_ref[...]
    knt = knt_ref[...]

    def proj(i, carry):
        rows = _chunk_rows(i)
        hh = ha_scr[rows, :]
        if latent:
            cos = cos_ref[rows, :]
            sin = sin_ref[rows, :]
            rot = lambda a: _rope(a, cos, sin)
            rot_t = lambda a: _rope_t(a, cost_ref[i], sint_ref[i])
        else:
            rot = rot_t = lambda a: a

        acc = _dot(hh, w_main_ref[:, 0:512])
        for j in range(4):
            a = acc[:, LANES * j:LANES * (j + 1)]
            ms = _dot((a * a).astype(BF16), pm)
            a = rot(a * lax.rsqrt(ms + EPS) * qn)
            qa_scr[rows, LANES * j:LANES * (j + 1)] = (a * Q_SCALE).astype(BF16)
        acc = _dot(hh, w_main_ref[:, 512:1024])
        for j in range(4):
            a = rot(acc[:, LANES * j:LANES * (j + 1)])
            qb_scr[rows, LANES * j:LANES * (j + 1)] = (a * Q_SCALE).astype(BF16)
        g_scr[rows, 0:512] = _silu(_dot(hh, w_main_ref[:, 1024:1536]))
        g_scr[rows, 512:1024] = _silu(_dot(hh, w_main_ref[:, 1536:2048]))
        v = _dot(hh, w_main_ref[:, 2048:2304])
        _store_v_variants(va_scr, rows, v[:, 0:LANES])
        _store_v_variants(vb_scr, rows, v[:, LANES:2 * LANES])

        kt = _dot_nt(w_kvt_ref[0:2 * LANES, :], hh)
        kat = kt[0:LANES, :]
        ms = _dot(pm, (kat * kat).astype(BF16))
        kat = kat * lax.rsqrt(ms + EPS) * knt
        kbt = kt[LANES:2 * LANES, :]
        if not latent:
            vt = _dot_nt(w_kvt_ref[2 * LANES:4 * LANES, :], hh)
            nakt_ref[i] = kat
            nbkt_ref[i] = kbt
            navt_ref[i] = vt[0:LANES, :]
            nbvt_ref[i] = vt[LANES:2 * LANES, :]
        _store_kt_variants(ka_scr, i, rot_t(kat))
        _store_kt_variants(kb_scr, i, rot_t(kbt))
        return carry

    lax.fori_loop(0, n_rows // ROW_CHUNK, proj, 0)

    sinks = [sink_ref[h] * LOG2E for h in range(8)]
    ck = ROW_CHUNK
    n_win = 3

    def attend(i, carry):
        rows = _chunk_rows(i)
        if latent:
            a_chunks = list(range(n_lat_chunks + n_past // ck))
            a_keys = pl.ds(0, seq + n_past)
            w0 = jnp.clip(i - 1, 0, n_lat_chunks - n_win)
            win_rows = pl.ds(pl.multiple_of(w0 * ck, ck), n_win * ck)
            dist = (lax.broadcasted_iota(jnp.int32, (ROW_CHUNK, ck), 1)
                    - lax.broadcasted_iota(jnp.int32, (ROW_CHUNK, ck), 0))
            for c in range(n_win):
                off = (w0 + c - i) * ck
                bias_scr[c] = jnp.where(jnp.abs(dist + off) <= WINDOW, 0.0, NEG_INF).astype(F32)
            b_chunks = [w0 + c for c in range(n_win)] + [n_lat_chunks]
            b_cols = (n_win + 1) * ck
        else:
            a_chunks = [i]
            a_keys = rows
            b_chunks = [i]
            b_cols = ck
        a_cols = len(a_chunks) * ck

        def qk(t, slot):
            p, branch = divmod(t, 2)
            cols = slice(LANES * p, LANES * (p + 1))
            kvh = p // 2
            q = (qb_scr if branch else qa_scr)[rows, cols]
            k_scr = kb_scr if branch else ka_scr
            chunks = b_chunks if branch else a_chunks
            for par in (0, 1):
                for c, chunk in enumerate(chunks):
                    s = _dot(q, k_scr[2 * kvh + par, chunk])
                    if latent and branch and c < n_win:
                        s = s + bias_scr[c]
                    s_scr[slot, par, :, c * ck:(c + 1) * ck] = s

        def softmax(t, slot):
            p, branch = divmod(t, 2)
            n_cols = b_cols if branch else a_cols
            rb = _softmax_rows(n_cols)
            for par in (0, 1):
                for r in range(ROW_CHUNK // rb):
                    sub = slice(r * rb, (r + 1) * rb)
                    s = s_scr[slot, par, sub, 0:n_cols]
                    m = jnp.max(s, axis=1, keepdims=True)
                    if branch:
                        sink = sinks[2 * p + par]
                        m = jnp.maximum(m, sink)
                        es_scr[slot, par, sub, :] = jnp.broadcast_to(jnp.exp2(sink - m), (rb, LANES))
                    p_scr[slot, par, sub, 0:n_cols] = jnp.exp2((s - m).astype(BF16))

        def pv(t, slot):
            p, branch = divmod(t, 2)
            kvh = p // 2
            v_scr = vb_scr if branch else va_scr
            outs = []
            for par in (0, 1):
                var = 2 * kvh + par
                if latent and branch:
                    n_loc = n_win * ck
                    acc = (_dot(p_scr[slot, par, :, 0:n_loc], v_scr[var, win_rows, :])
                           + _dot(p_scr[slot, par, :, n_loc:b_cols], v_scr[var, past_rows, :]))
                else:
                    acc = _dot(p_scr[slot, par, :, 0:a_cols], v_scr[var, a_keys, :])
                denom = pltpu.roll(acc, HEAD_DIM, 1)
                if branch:
                    denom = denom + es_scr[slot, par]
                outs.append(acc / denom)
            o = jnp.where(_lane_iota(ROW_CHUNK) < HEAD_DIM, outs[0], outs[1])
            ocols = slice(512 * branch + LANES * p, 512 * branch + LANES * (p + 1))
            ha_scr[rows, ocols] = (o * g_scr[rows, ocols]).astype(BF16)

        _run_pipeline(8, (qk, softmax, pv))
        return carry

    lax.fori_loop(0, n_rows // ROW_CHUNK, attend, 0)

    _out_proj_norm(x_ref, mod_ref, mod_row, ha_scr, w_out_ref, lng_ref, lnb_ref, y_ref, n_rows, alpha)


def _odd_kernel(latent, n_rows, seq, alpha, lam_init, *refs):
    if latent:
        (x_ref, mod_ref, w_in_ref, w_out_ref, lq1_ref, lk1_ref, lq2_ref, lk2_ref, sub_ref, lng_ref, lnb_ref,
         cos_ref, sin_ref, cck_ref, ccv_ref,
         y_ref,
         ha_scr, q_scr, k_scr, v_scr, g_scr, s_scr, p_scr) = refs
    else:
        (x_ref, mod_ref, w_in_ref, w_out_ref, lq1_ref, lk1_ref, lq2_ref, lk2_ref, sub_ref, lng_ref, lnb_ref,
         y_ref, nck_ref, ncv_ref,
         ha_scr, q_scr, k_scr, v_scr, g_scr, s_scr, p_scr) = refs

    step = pl.program_id(0)
    mod_row = step + 1 if latent else 0
    _modulate(x_ref, mod_ref, mod_row, ha_scr, n_rows)

    n_heads = D_MODEL // LANES
    lo = _lane_iota(ROW_CHUNK) < HEAD_DIM

    def store_k(rows, h, a):
        cols = slice(LANES * h, LANES * (h + 1))
        zero = jnp.zeros_like(a)
        k_scr[0, rows, cols] = jnp.where(lo, a, zero).astype(BF16)
        k_scr[1, rows, cols] = jnp.where(lo, zero, a).astype(BF16)

    if latent:
        n_past = cck_ref.shape[2]
        past = pl.ds(seq, n_past)
        for h in range(n_heads):
            store_k(past, h, cck_ref[0, 0, :, h, :])
            v_scr[past, LANES * h:LANES * (h + 1)] = ccv_ref[0, 0, :, h, :].astype(BF16)

    def proj(i, carry):
        rows = _chunk_rows(i)
        hh = ha_scr[rows, :]
        if latent:
            cos = cos_ref[rows, :]
            sin = sin_ref[rows, :]
            rot = lambda a: _rope(a, cos, sin)
        else:
            rot = lambda a: a
        for half in range(2):
            acc = _dot(hh, w_in_ref[:, 512 * half:512 * (half + 1)])
            for j in range(4):
                a = rot(acc[:, LANES * j:LANES * (j + 1)])
                cols = slice(512 * half + LANES * j, 512 * half + LANES * (j + 1))
                q_scr[rows, cols] = (a * Q_SCALE).astype(BF16)
        for half in range(2):
            acc = _dot(hh, w_in_ref[:, 1024 + 512 * half:1024 + 512 * (half + 1)])
            for j in range(4):
                a = acc[:, LANES * j:LANES * (j + 1)]
                if not latent:
                    nck_ref[i, 0, :, 4 * half + j, :] = a
                store_k(rows, 4 * half + j, rot(a))
        for half in range(2):
            acc = _dot(hh, w_in_ref[:, 2048 + 512 * half:2048 + 512 * (half + 1)])
            if not latent:
                for j in range(4):
                    ncv_ref[i, 0, :, 4 * half + j, :] = acc[:, LANES * j:LANES * (j + 1)]
            v_scr[rows, 512 * half:512 * (half + 1)] = acc.astype(BF16)
        for half in range(2):
            acc = _dot(hh, w_in_ref[:, 3072 + 512 * half:3072 + 512 * (half + 1)])
            g_scr[rows, 512 * half:512 * (half + 1)] = _silu(acc)
        return carry

    lax.fori_loop(0, n_rows // ROW_CHUNK, proj, 0)

    lam = (jnp.exp(jnp.sum(lq1_ref[...] * lk1_ref[...], axis=1, keepdims=True))
           - jnp.exp(jnp.sum(lq2_ref[...] * lk2_ref[...], axis=1, keepdims=True)) + lam_init)
    sub = sub_ref[...] * (1.0 - lam_init)
    n_keys = seq + n_past if latent else ROW_CHUNK
    rb = _softmax_rows(2 * n_keys)

    def attend(i, carry):
        rows = _chunk_rows(i)
        keys = pl.ds(0, n_keys) if latent else rows

        def qk(h, slot):
            cols = slice(LANES * h, LANES * (h + 1))
            q = q_scr[rows, cols]
            for m in (0, 1):
                s_scr[slot, m] = _dot_nt(q, k_scr[m, keys, cols])

        def softmax(h, slot):
            for r in range(ROW_CHUNK // rb):
                sub_rows = slice(r * rb, (r + 1) * rb)
                s1 = s_scr[slot, 0, sub_rows, :]
                s2 = s_scr[slot, 1, sub_rows, :]
                e1 = jnp.exp2(s1 - jnp.max(s1, axis=1, keepdims=True))
                e2 = jnp.exp2(s2 - jnp.max(s2, axis=1, keepdims=True))
                w1 = 1.0 / jnp.sum(e1, axis=1, keepdims=True)
                w2 = lam / jnp.sum(e2, axis=1, keepdims=True)
                p_scr[slot, sub_rows, :] = (e1 * w1 - e2 * w2).astype(BF16)

        def pv(h, slot):
            cols = slice(LANES * h, LANES * (h + 1))
            o = _dot(p_scr[slot], v_scr[keys, cols])
            ms = jnp.mean(o * o, axis=1, keepdims=True)
            o = o * lax.rsqrt(ms + EPS) * sub
            ha_scr[rows, cols] = (o * g_scr[rows, cols]).astype(BF16)

        _run_pipeline(n_heads, (qk, softmax, pv))
        return carry

    lax.fori_loop(0, n_rows // ROW_CHUNK, attend, 0)

    _out_proj_norm(x_ref, mod_ref, mod_row, ha_scr, w_out_ref, lng_ref, lnb_ref, y_ref, n_rows, alpha)


def _mod_kernel(cv_ref, w_ref, b_ref, o_ref):
    s = _silu(cv_ref[...])
    o_ref[0] = jnp.dot(s, w_ref[0], precision=lax.Precision.HIGHEST, preferred_element_type=F32) + b_ref[0]


def _full(shape, **kw):
    zeros = (0,) * len(shape)
    return pl.BlockSpec(shape, lambda i: zeros, **kw)


def _rope_tables(seq):
    t = np.arange(seq)
    n_freq = HEAD_DIM // 4
    freqs = ROPE_THETA ** (-np.arange(n_freq, dtype=np.float64) / n_freq)
    ang_row = (t // GRID_W)[:, None] * freqs
    ang_col = (t % GRID_W)[:, None] * freqs
    ang = np.concatenate([ang_row, ang_row, ang_col, ang_col], axis=1)
    sign = np.concatenate([-np.ones(n_freq), np.ones(n_freq)] * 2)[None, :]
    cos = np.tile(np.cos(ang), (1, 2)).astype(np.float32)
    sin = np.tile(np.sin(ang) * sign, (1, 2)).astype(np.float32)
    chunked_t = lambda a: a.reshape(seq // ROW_CHUNK, ROW_CHUNK, LANES).transpose(0, 2, 1)
    return jnp.asarray(cos), jnp.asarray(sin), jnp.asarray(chunked_t(cos)), jnp.asarray(chunked_t(sin))


def _head_mean_matrix():
    idx = np.arange(LANES) // HEAD_DIM
    return jnp.asarray((idx[:, None] == idx[None, :]).astype(np.float32) / HEAD_DIM, dtype=BF16)


def _modulation(c, c_ctx, w_mod, b_mod):
    depth = w_mod.shape[0]
    cv = jnp.concatenate([c_ctx[None, :], c, jnp.zeros((8 - 1 - c.shape[0], D_MODEL), F32)], axis=0)
    n_blk = 3 * D_MODEL // 1024
    return pl.pallas_call(
        _mod_kernel,
        grid=(depth, n_blk),
        in_specs=[pl.BlockSpec((8, D_MODEL), lambda l, n: (0, 0)),
                  pl.BlockSpec((1, D_MODEL, 1024), lambda l, n: (l, 0, n)),
                  pl.BlockSpec((1, 1, 1024), lambda l, n: (l, 0, n))],
        out_specs=pl.BlockSpec((1, 8, 1024), lambda l, n: (l, 0, n)),
        out_shape=jax.ShapeDtypeStruct((depth, 8, 3 * D_MODEL), F32),
        compiler_params=pltpu.CompilerParams(dimension_semantics=("arbitrary", "arbitrary")),
        name="adaln_modulation",
    )(cv, w_mod, b_mod.reshape(depth, 1, 3 * D_MODEL))


def _even_layer(x, mod, layer, w_in, w_out, q_norm, k_norm, sink, ln_g, ln_b, latent, seq, n_rows, alpha, extras=()):
    total = x.shape[0]
    grid = (total // n_rows,)
    single = pl.Buffered(1)
    sec = lambda lo, hi: w_in[:, lo:hi]
    w_main = jnp.concatenate([sec(0, 512), sec(1280, 1792), sec(768, 1280), sec(2048, 2560),
                              sec(640, 768), sec(1920, 2048)], axis=1).astype(BF16)
    w_kvt = jnp.concatenate([sec(512, 640), sec(1792, 1920), sec(640, 768), sec(1920, 2048)], axis=1).T.astype(BF16)
    qn = jnp.tile(q_norm, 2)[None, :]
    knt = jnp.broadcast_to(jnp.tile(k_norm, 2)[:, None], (LANES, ROW_CHUNK))

    row_blk = lambda width: pl.BlockSpec((n_rows, width), lambda i: (i, 0))
    in_specs = [row_blk(D_MODEL),
                pl.BlockSpec((1, 8, 3 * D_MODEL), lambda i: (layer, 0, 0)),
                _full(w_main.shape, pipeline_mode=single), _full(w_kvt.shape, pipeline_mode=single),
                _full((D_MODEL, D_MODEL), pipeline_mode=single),
                _full((1, LANES)), _full((LANES, ROW_CHUNK)),
                pl.BlockSpec(memory_space=pltpu.SMEM),
                pl.BlockSpec((1, 1, D_MODEL), lambda i: (layer, 0, 0)),
                pl.BlockSpec((1, 1, D_MODEL), lambda i: (layer, 0, 0)),
                _full((LANES, LANES))]
    args = [x, mod, w_main, w_kvt, w_out.astype(BF16), qn, knt, sink, ln_g, ln_b, _head_mean_matrix()]
    y_shape = jax.ShapeDtypeStruct((total, D_MODEL), F32)
    n_blocks = n_rows // ROW_CHUNK
    if latent:
        cos, sin, cos_t, sin_t, cakt, cav, cbkt, cbv = extras
        n_past = cav.shape[1]
        in_specs += [_full(cos.shape, pipeline_mode=single), _full(sin.shape, pipeline_mode=single),
                     _full(cos_t.shape, pipeline_mode=single), _full(sin_t.shape, pipeline_mode=single)]
        in_specs += [pl.BlockSpec((1, LANES, n_past), lambda i: (i, 0, 0)),
                     pl.BlockSpec((1, n_past, LANES), lambda i: (i, 0, 0))] * 2
        args += [cos, sin, cos_t, sin_t, cakt, cav, cbkt, cbv]
        out_specs = row_blk(D_MODEL)
        out_shape = y_shape
        n_keys = seq + n_past
    else:
        kv_blk = pl.BlockSpec((n_blocks, LANES, ROW_CHUNK), lambda i: (i, 0, 0))
        out_specs = [row_blk(D_MODEL)] + [kv_blk] * 4
        out_shape = [y_shape] + [jax.ShapeDtypeStruct((total // seq, LANES, seq), F32)] * 4
        n_keys = n_rows
    n_kchunks = n_keys // ROW_CHUNK
    n_cols = n_keys if latent else ROW_CHUNK
    scratch = [pltpu.VMEM((n_rows, D_MODEL), BF16),
               pltpu.VMEM((n_rows, 512), BF16), pltpu.VMEM((n_rows, 512), BF16),
               pltpu.VMEM((4, n_kchunks, LANES, ROW_CHUNK), BF16), pltpu.VMEM((4, n_keys, LANES), BF16),
               pltpu.VMEM((4, n_kchunks, LANES, ROW_CHUNK), BF16), pltpu.VMEM((4, n_keys, LANES), BF16),
               pltpu.VMEM((n_rows, D_MODEL), F32),
               pltpu.VMEM((2, 2, ROW_CHUNK, n_cols), F32),
               pltpu.VMEM((2, 2, ROW_CHUNK, n_cols), BF16),
               pltpu.VMEM((2, 2, ROW_CHUNK, LANES), F32)]
    if latent:
        scratch.append(pltpu.VMEM((3, ROW_CHUNK, ROW_CHUNK), F32))
    return pl.pallas_call(
        functools.partial(_even_kernel, latent, n_rows, seq, alpha),
        grid=grid, in_specs=in_specs, out_specs=out_specs, out_shape=out_shape,
        scratch_shapes=scratch,
        compiler_params=pltpu.CompilerParams(dimension_semantics=("arbitrary",), vmem_limit_bytes=VMEM_LIMIT),
        name="even_layer_latent" if latent else "even_layer_context",
    )(*args)


def _odd_layer(x, mod, layer, w_in, w_out, lams, sub, ln_g, ln_b, latent, seq, n_rows, alpha, lam_init, extras=()):
    total = x.shape[0]
    grid = (total // n_rows,)
    row_blk = lambda width: pl.BlockSpec((n_rows, width), lambda i: (i, 0))
    single = pl.Buffered(1)
    in_specs = [row_blk(D_MODEL),
                pl.BlockSpec((1, 8, 3 * D_MODEL), lambda i: (layer, 0, 0)),
                _full(w_in.shape, pipeline_mode=single), _full(w_out.shape, pipeline_mode=single),
                _full((1, HEAD_DIM)), _full((1, HEAD_DIM)), _full((1, HEAD_DIM)), _full((1, HEAD_DIM)),
                _full((1, LANES)),
                pl.BlockSpec((1, 1, D_MODEL), lambda i: (layer, 0, 0)),
                pl.BlockSpec((1, 1, D_MODEL), lambda i: (layer, 0, 0))]
    args = [x, mod, w_in.astype(BF16), w_out.astype(BF16), *lams, sub, ln_g, ln_b]
    y_shape = jax.ShapeDtypeStruct((total, D_MODEL), F32)
    n_heads = D_MODEL // LANES
    n_blocks = n_rows // ROW_CHUNK
    if latent:
        cos, sin, cck, ccv = extras
        n_past = cck.shape[2]
        in_specs += [_full(cos.shape, pipeline_mode=single), _full(sin.shape, pipeline_mode=single)]
        in_specs += [pl.BlockSpec((1, 1, n_past, n_heads, LANES), lambda i: (i, layer // 2, 0, 0, 0))] * 2
        args += [cos, sin, cck, ccv]
        out_specs = row_blk(D_MODEL)
        out_shape = y_shape
        n_keys = seq + n_past
    else:
        kv_blk = pl.BlockSpec((n_blocks, 1, seq, n_heads, LANES), lambda i: (i, 0, 0, 0, 0))
        out_specs = [row_blk(D_MODEL), kv_blk, kv_blk]
        out_shape = [y_shape] + [jax.ShapeDtypeStruct((total // seq, 1, seq, n_heads, LANES), F32)] * 2
        n_keys = n_rows
    n_cols = n_keys if latent else ROW_CHUNK
    scratch = [pltpu.VMEM((n_rows, D_MODEL), BF16),
               pltpu.VMEM((n_rows, D_MODEL), BF16),
               pltpu.VMEM((2, n_keys, D_MODEL), BF16),
               pltpu.VMEM((n_keys, D_MODEL), BF16),
               pltpu.VMEM((n_rows, D_MODEL), F32),
               pltpu.VMEM((2, 2, ROW_CHUNK, n_cols), F32),
               pltpu.VMEM((2, ROW_CHUNK, n_cols), BF16)]
    return pl.pallas_call(
        functools.partial(_odd_kernel, latent, n_rows, seq, alpha, lam_init),
        grid=grid, in_specs=in_specs, out_specs=out_specs, out_shape=out_shape,
        scratch_shapes=scratch,
        compiler_params=pltpu.CompilerParams(dimension_semantics=("arbitrary",), vmem_limit_bytes=VMEM_LIMIT),
        name="odd_layer_latent" if latent else "odd_layer_context",
    )(*args)


def kernel(x_prompt, x_sample, cache_a_k, cache_a_v, cache_b_k, cache_b_v, cache_c_k, cache_c_v, c, c_ctx,
           w_mod, b_mod, ln_g, ln_b, w_in_even, w_out_even, q_norm_a, k_norm_a, sink_b, w_in_odd, w_out_odd,
           lambda_q1, lambda_k1, lambda_q2, lambda_k2, subln_c):
    depth = w_mod.shape[0]
    batch, seq, _ = x_prompt.shape
    dec_batch, dec_seq, _ = x_sample.shape
    n_past = cache_a_k.shape[2]
    alpha = (2 * depth) ** 0.25
    assert seq == ROW_CHUNK and n_past % ROW_CHUNK == 0 and dec_seq % ROW_CHUNK == 0

    mod = _modulation(c, c_ctx, w_mod, b_mod)
    ln_g3 = ln_g.reshape(depth, 1, D_MODEL)
    ln_b3 = ln_b.reshape(depth, 1, D_MODEL)
    cos, sin, cos_t, sin_t = _rope_tables(dec_seq)

    def run(x, latent, n_batch, s, rows_even, rows_odd):
        kv = {"a_k": [], "a_v": [], "b_k": [], "b_v": [], "c_k": [], "c_v": []}
        for l in range(depth):
            if l % 2 == 0:
                e = l // 2
                extras = ()
                if latent:
                    k_t = lambda t: t[:, e].transpose(0, 2, 3, 1).reshape(n_batch, LANES, n_past)
                    v_n = lambda t: t[:, e].reshape(n_batch, n_past, LANES)
                    extras = (cos, sin, cos_t, sin_t,
                              k_t(cache_a_k), v_n(cache_a_v), k_t(cache_b_k), v_n(cache_b_v))
                res = _even_layer(x, mod, l, w_in_even[e], w_out_even[e], q_norm_a[e], k_norm_a[e],
                                  sink_b[e], ln_g3, ln_b3, latent, s, rows_even, alpha, extras)
                if latent:
                    x = res
                else:
                    x = res[0]
                    for name, t in zip(("a_k", "a_v", "b_k", "b_v"), res[1:]):
                        kv[name].append(t.reshape(n_batch, 2, HEAD_DIM, s).transpose(0, 3, 1, 2))
            else:
                o = l // 2
                lam_init = 0.8 - 0.6 * math.exp(-0.3 * l)
                extras = (cos, sin, cache_c_k, cache_c_v) if latent else ()
                lams = [t[o][None, :] for t in (lambda_q1, lambda_k1, lambda_q2, lambda_k2)]
                res = _odd_layer(x, mod, l, w_in_odd[o], w_out_odd[o], lams,
                                 subln_c[o][None, :], ln_g3, ln_b3, latent, s, rows_odd, alpha, lam_init, extras)
                if latent:
                    x = res
                else:
                    x = res[0]
                    kv["c_k"].append(res[1][:, 0])
                    kv["c_v"].append(res[2][:, 0])
        return x, kv

    y_ctx, kv = run(x_prompt.reshape(batch * seq, D_MODEL), False, batch, seq, 1024, 512)
    y_lat, _ = run(x_sample.reshape(dec_batch * dec_seq, D_MODEL), True, dec_batch, dec_seq, dec_seq, dec_seq)

    stack = lambda name: jnp.stack(kv[name], axis=1)
    return (y_ctx.reshape(batch, seq, D_MODEL), y_lat.reshape(dec_batch, dec_seq, D_MODEL),
            stack("a_k"), stack("a_v"), stack("b_k"), stack("b_v"), stack("c_k"), stack("c_v"))
```

```python
import functools
import math

import jax
import jax.numpy as jnp
import numpy as np
from jax import lax
from jax.experimental import pallas as pl
from jax.experimental.pallas import tpu as pltpu

F32 = jnp.float32
BF16 = jnp.bfloat16

D_MODEL = 1024
HEAD_DIM = 64
GRID_W = 64
WINDOW = 128
ROPE_THETA = 10000.0
EPS = 1e-6
NEG_INF = -1e30
LOG2E = 1.4426950408889634
Q_SCALE = HEAD_DIM ** -0.5 * LOG2E
LANES = 128
ROW_CHUNK = 256
SOFTMAX_VREGS = 40
VMEM_LIMIT = 60000 * 1024


def _silu(x):
    return x / (1.0 + jnp.exp(-x))


def _dot(a, b):
    return jnp.dot(a, b, preferred_element_type=F32)


def _dot_nt(a, b):
    return lax.dot_general(a, b, (((1,), (1,)), ((), ())), preferred_element_type=F32)


def _lane_iota(rows):
    return lax.broadcasted_iota(jnp.int32, (rows, LANES), 1)


def _chunk_rows(i):
    if isinstance(i, int):
        return pl.ds(i * ROW_CHUNK, ROW_CHUNK)
    return pl.ds(pl.multiple_of(i * ROW_CHUNK, ROW_CHUNK), ROW_CHUNK)


def _softmax_rows(n_cols):
    rows = 8
    while rows * 2 * n_cols <= SOFTMAX_VREGS * 1024 and rows * 2 <= ROW_CHUNK:
        rows *= 2
    return rows


def _rope(a, cos, sin_signed):
    lane = _lane_iota(a.shape[0])
    fwd = pltpu.roll(a, LANES - 16, 1)
    bwd = pltpu.roll(a, 16, 1)
    partner = jnp.where((lane & 16) == 0, fwd, bwd)
    return a * cos + partner * sin_signed


def _rope_t(a, cos_t, sin_t):
    blocks = [a[16 * b:16 * (b + 1), :] for b in range(a.shape[0] // 16)]
    partner = jnp.concatenate([blocks[b ^ 1] for b in range(len(blocks))], axis=0)
    return a * cos_t + partner * sin_t


def _store_kt_variants(scr, chunk, kt):
    zero = jnp.zeros((HEAD_DIM, kt.shape[1]), F32)
    for j in range(2):
        kj = kt[HEAD_DIM * j:HEAD_DIM * (j + 1), :]
        scr[2 * j, chunk] = jnp.concatenate([kj, zero], axis=0).astype(BF16)
        scr[2 * j + 1, chunk] = jnp.concatenate([zero, kj], axis=0).astype(BF16)


def _store_v_variants(scr, rows, a):
    lane = _lane_iota(a.shape[0])
    lo = lane < HEAD_DIM
    swapped = pltpu.roll(a, HEAD_DIM, 1)
    one = jnp.ones_like(a)
    scr[0, rows, :] = jnp.where(lo, a, one).astype(BF16)
    scr[1, rows, :] = jnp.where(lo, one, swapped).astype(BF16)
    scr[2, rows, :] = jnp.where(lo, swapped, one).astype(BF16)
    scr[3, rows, :] = jnp.where(lo, one, a).astype(BF16)


def _layer_norm_rows(z, g, b):
    mu = jnp.mean(z, axis=-1, keepdims=True)
    zc = z - mu
    var = jnp.mean(zc * zc, axis=-1, keepdims=True)
    return zc * lax.rsqrt(var + EPS) * g + b


def _modulate(x_ref, mod_ref, mod_row, h_scr, n_rows):
    shift = mod_ref[0, pl.ds(mod_row, 1), 0:D_MODEL]
    scale = mod_ref[0, pl.ds(mod_row, 1), D_MODEL:2 * D_MODEL]

    def body(i, carry):
        rows = _chunk_rows(i)
        h_scr[rows, :] = (x_ref[rows, :] * (1.0 + scale) + shift).astype(BF16)
        return carry

    lax.fori_loop(0, n_rows // ROW_CHUNK, body, 0)


def _out_proj_norm(x_ref, mod_ref, mod_row, attn_scr, w_out_ref, lng_ref, lnb_ref, y_ref, n_rows, alpha):
    gate = mod_ref[0, pl.ds(mod_row, 1), 2 * D_MODEL:3 * D_MODEL]
    g = lng_ref[0]
    b = lnb_ref[0]

    def body(i, carry):
        rows = _chunk_rows(i)
        out = _dot(attn_scr[rows, :], w_out_ref[...])
        z = alpha * x_ref[rows, :] + gate * out
        y_ref[rows, :] = _layer_norm_rows(z, g, b)
        return carry

    lax.fori_loop(0, n_rows // ROW_CHUNK, body, 0, unroll=True)


def _run_pipeline(n_items, stages):
    for u in range(n_items + len(stages) - 1):
        for k, stage in enumerate(stages):
            t = u - k
            if 0 <= t < n_items:
                stage(t, t % 2)


def _even_kernel(latent, n_rows, seq, alpha, *refs):
    if latent:
        (x_ref, mod_ref, w_main_ref, w_kvt_ref, w_out_ref, qn_ref, knt_ref, sink_ref, lng_ref, lnb_ref, pm_ref,
         cos_ref, sin_ref, cost_ref, sint_ref, cakt_ref, cav_ref, cbkt_ref, cbv_ref,
         y_ref,
         ha_scr, qa_scr, qb_scr, ka_scr, va_scr, kb_scr, vb_scr, g_scr, s_scr, p_scr, es_scr, bias_scr) = refs
    else:
        (x_ref, mod_ref, w_main_ref, w_kvt_ref, w_out_ref, qn_ref, knt_ref, sink_ref, lng_ref, lnb_ref, pm_ref,
         y_ref, nakt_ref, navt_ref, nbkt_ref, nbvt_ref,
         ha_scr, qa_scr, qb_scr, ka_scr, va_scr, kb_scr, vb_scr, g_scr, s_scr, p_scr, es_scr) = refs

    step = pl.program_id(0)
    mod_row = step + 1 if latent else 0
    _modulate(x_ref, mod_ref, mod_row, ha_scr, n_rows)

    n_lat_chunks = seq // ROW_CHUNK
    if latent:
        n_past = cav_ref.shape[1]
        past_rows = pl.ds(seq, n_past)
        _store_kt_variants(ka_scr, n_lat_chunks, cakt_ref[0])
        _store_kt_variants(kb_scr, n_lat_chunks, cbkt_ref[0])
        _store_v_variants(va_scr, past_rows, cav_ref[0])
        _store_v_variants(vb_scr, past_rows, cbv_ref[0])

    pm = pm_ref[...]
    qn = qn_ref[...]
    knt = knt_ref[...]

    def proj(i, carry):
        rows = _chunk_rows(i)
        hh = ha_scr[rows, :]
        if latent:
            cos = cos_ref[rows, :]
            sin = sin_ref[rows, :]
            rot = lambda a: _rope(a, cos, sin)
            rot_t = lambda a: _rope_t(a, cost_ref[i], sint_ref[i])
        else:
            rot = rot_t = lambda a: a

        acc = _dot(hh, w_main_ref[:, 0:512])
        for j in range(4):
            a = acc[:, LANES * j:LANES * (j + 1)]
            ms = _dot((a * a).astype(BF16), pm)
            a = rot(a * lax.rsqrt(ms + EPS) * qn)
            qa_scr[rows, LANES * j:LANES * (j + 1)] = (a * Q_SCALE).astype(BF16)
        acc = _dot(hh, w_main_ref[:, 512:1024])
        for j in range(4):
            a = rot(acc[:, LANES * j:LANES * (j + 1)])
            qb_scr[rows, LANES * j:LANES * (j + 1)] = (a * Q_SCALE).astype(BF16)
        g_scr[rows, 0:512] = _silu(_dot(hh, w_main_ref[:, 1024:1536]))
        g_scr[rows, 512:1024] = _silu(_dot(hh, w_main_ref[:, 1536:2048]))
        v = _dot(hh, w_main_ref[:, 2048:2304])
        _store_v_variants(va_scr, rows, v[:, 0:LANES])
        _store_v_variants(vb_scr, rows, v[:, LANES:2 * LANES])

        kt = _dot_nt(w_kvt_ref[0:2 * LANES, :], hh)
        kat = kt[0:LANES, :]
        ms = _dot(pm, (kat * kat).astype(BF16))
        kat = kat * lax.rsqrt(ms + EPS) * knt
        kbt = kt[LANES:2 * LANES, :]
        if not latent:
            vt = _dot_nt(w_kvt_ref[2 * LANES:4 * LANES, :], hh)
            nakt_ref[i] = kat
            nbkt_ref[i] = kbt
            navt_ref[i] = vt[0:LANES, :]
            nbvt_ref[i] = vt[LANES:2 * LANES, :]
        _store_kt_variants(ka_scr, i, rot_t(kat))
        _store_kt_variants(kb_scr, i, rot_t(kbt))
        return carry

    lax.fori_loop(0, n_rows // ROW_CHUNK, proj, 0, unroll=2)

    sinks = [sink_ref[h] * LOG2E for h in range(8)]
    ck = ROW_CHUNK
    n_win = 3

    def attend(i, carry):
        rows = _chunk_rows(i)
        if latent:
            a_chunks = list(range(n_lat_chunks + n_past // ck))
            a_keys = pl.ds(0, seq + n_past)
            w0 = jnp.clip(i - 1, 0, n_lat_chunks - n_win)
            win_rows = pl.ds(pl.multiple_of(w0 * ck, ck), n_win * ck)
            dist = (lax.broadcasted_iota(jnp.int32, (ROW_CHUNK, ck), 1)
                    - lax.broadcasted_iota(jnp.int32, (ROW_CHUNK, ck), 0))
            for c in range(n_win):
                off = (w0 + c - i) * ck
                bias_scr[c] = jnp.where(jnp.abs(dist + off) <= WINDOW, 0.0, NEG_INF).astype(F32)
            b_chunks = [w0 + c for c in range(n_win)] + [n_lat_chunks]
            b_cols = (n_win + 1) * ck
        else:
            a_chunks = [i]
            a_keys = rows
            b_chunks = [i]
            b_cols = ck
        a_cols = len(a_chunks) * ck

        def qk(t, slot):
            p, branch = divmod(t, 2)
            cols = slice(LANES * p, LANES * (p + 1))
            kvh = p // 2
            q = (qb_scr if branch else qa_scr)[rows, cols]
            k_scr = kb_scr if branch else ka_scr
            chunks = b_chunks if branch else a_chunks
            for par in (0, 1):
                for c, chunk in enumerate(chunks):
                    s = _dot(q, k_scr[2 * kvh + par, chunk])
                    if latent and branch and c < n_win:
                        s = s + bias_scr[c]
                    s_scr[slot, par, :, c * ck:(c + 1) * ck] = s

        def softmax(t, slot):
            p, branch = divmod(t, 2)
            n_cols = b_cols if branch else a_cols
            rb = _softmax_rows(n_cols)
            for par in (0, 1):
                for r in range(ROW_CHUNK // rb):
                    sub = slice(r * rb, (r + 1) * rb)
                    s = s_scr[slot, par, sub, 0:n_cols]
                    m = jnp.max(s, axis=1, keepdims=True)
                    if branch:
                        sink = sinks[2 * p + par]
                        m = jnp.maximum(m, sink)
                        es_scr[slot, sub, HEAD_DIM * par:HEAD_DIM * (par + 1)] = jnp.broadcast_to(
                            jnp.exp2(sink - m), (rb, HEAD_DIM))
                    p_scr[slot, par, sub, 0:n_cols] = jnp.exp2((s - m).astype(BF16))

        def pv(t, slot):
            p, branch = divmod(t, 2)
            kvh = p // 2
            v_scr = vb_scr if branch else va_scr
            accs = []
            for par in (0, 1):
                var = 2 * kvh + par
                if latent and branch:
                    n_loc = n_win * ck
                    accs.append(_dot(p_scr[slot, par, :, 0:n_loc], v_scr[var, win_rows, :])
                                + _dot(p_scr[slot, par, :, n_loc:b_cols], v_scr[var, past_rows, :]))
                else:
                    accs.append(_dot(p_scr[slot, par, :, 0:a_cols], v_scr[var, a_keys, :]))
            lo = _lane_iota(ROW_CHUNK) < HEAD_DIM
            denom = pltpu.roll(jnp.where(lo, accs[1], accs[0]), HEAD_DIM, 1)
            if branch:
                denom = denom + es_scr[slot]
            o = jnp.where(lo, accs[0], accs[1]) / denom
            ocols = slice(512 * branch + LANES * p, 512 * branch + LANES * (p + 1))
            ha_scr[rows, ocols] = (o * g_scr[rows, ocols]).astype(BF16)

        _run_pipeline(8, (qk, softmax, pv))
        return carry

    lax.fori_loop(0, n_rows // ROW_CHUNK, attend, 0)

    _out_proj_norm(x_ref, mod_ref, mod_row, ha_scr, w_out_ref, lng_ref, lnb_ref, y_ref, n_rows, alpha)


def _odd_kernel(latent, n_rows, seq, alpha, lam_init, *refs):
    if latent:
        (x_ref, mod_ref, w_in_ref, w_out_ref, lq1_ref, lk1_ref, lq2_ref, lk2_ref, sub_ref, lng_ref, lnb_ref,
         cos_ref, sin_ref, cck_ref, ccv_ref,
         y_ref,
         ha_scr, q_scr, k_scr, v_scr, g_scr, s_scr, p_scr) = refs
    else:
        (x_ref, mod_ref, w_in_ref, w_out_ref, lq1_ref, lk1_ref, lq2_ref, lk2_ref, sub_ref, lng_ref, lnb_ref,
         y_ref, nck_hbm, ncv_hbm,
         ha_scr, q_scr, k_scr, v_scr, g_scr, s_scr, p_scr, kv_stage, kv_sems) = refs

    step = pl.program_id(0)
    mod_row = step + 1 if latent else 0
    _modulate(x_ref, mod_ref, mod_row, ha_scr, n_rows)

    n_heads = D_MODEL // LANES
    n_blocks = n_rows // ROW_CHUNK
    lo = _lane_iota(ROW_CHUNK) < HEAD_DIM

    def kv_out_copies(blk):
        elem = step * n_blocks + blk
        return [pltpu.make_async_copy(kv_stage.at[blk, t, :, pl.ds(LANES * h, LANES)],
                                      out.at[elem, 0, :, h, :], kv_sems.at[blk, t])
                for t, out in enumerate((nck_hbm, ncv_hbm)) for h in range(n_heads)]

    def store_k(rows, h, a):
        cols = slice(LANES * h, LANES * (h + 1))
        zero = jnp.zeros_like(a)
        k_scr[0, rows, cols] = jnp.where(lo, a, zero).astype(BF16)
        k_scr[1, rows, cols] = jnp.where(lo, zero, a).astype(BF16)

    if latent:
        n_past = cck_ref.shape[2]
        past = pl.ds(seq, n_past)
        for h in range(n_heads):
            store_k(past, h, cck_ref[0, 0, :, h, :])
            v_scr[past, LANES * h:LANES * (h + 1)] = ccv_ref[0, 0, :, h, :].astype(BF16)

    def proj(i, carry):
        rows = _chunk_rows(i)
        hh = ha_scr[rows, :]
        if latent:
            cos = cos_ref[rows, :]
            sin = sin_ref[rows, :]
            rot = lambda a: _rope(a, cos, sin)
        else:
            rot = lambda a: a
        for half in range(2):
            acc = _dot(hh, w_in_ref[:, 512 * half:512 * (half + 1)])
            for j in range(4):
                a = rot(acc[:, LANES * j:LANES * (j + 1)])
                cols = slice(512 * half + LANES * j, 512 * half + LANES * (j + 1))
                q_scr[rows, cols] = (a * Q_SCALE).astype(BF16)
        for half in range(2):
            acc = _dot(hh, w_in_ref[:, 1024 + 512 * half:1024 + 512 * (half + 1)])
            if not latent:
                kv_stage[i, 0, :, 512 * half:512 * (half + 1)] = acc
            for j in range(4):
                store_k(rows, 4 * half + j, rot(acc[:, LANES * j:LANES * (j + 1)]))
        for half in range(2):
            acc = _dot(hh, w_in_ref[:, 2048 + 512 * half:2048 + 512 * (half + 1)])
            if not latent:
                kv_stage[i, 1, :, 512 * half:512 * (half + 1)] = acc
            v_scr[rows, 512 * half:512 * (half + 1)] = acc.astype(BF16)
        if not latent:
            for copy in kv_out_copies(i):
                copy.start()
        for half in range(2):
            acc = _dot(hh, w_in_ref[:, 3072 + 512 * half:3072 + 512 * (half + 1)])
            g_scr[rows, 512 * half:512 * (half + 1)] = _silu(acc)
        return carry

    if latent:
        lax.fori_loop(0, n_blocks, proj, 0, unroll=2)
    else:
        for blk in range(n_blocks):
            proj(blk, 0)

    lam = (jnp.exp(jnp.sum(lq1_ref[...] * lk1_ref[...], axis=1, keepdims=True))
           - jnp.exp(jnp.sum(lq2_ref[...] * lk2_ref[...], axis=1, keepdims=True)) + lam_init)
    sub = sub_ref[...] * (1.0 - lam_init)
    n_keys = seq + n_past if latent else ROW_CHUNK
    rb = _softmax_rows(n_keys)
    ones = jnp.ones((n_keys, LANES), BF16)

    def attend(i, carry):
        rows = _chunk_rows(i)
        keys = pl.ds(0, n_keys) if latent else rows

        def qk(h, slot):
            cols = slice(LANES * h, LANES * (h + 1))
            q = q_scr[rows, cols]
            for m in (0, 1):
                s_scr[slot, m] = _dot_nt(q, k_scr[m, keys, cols])

        def softmax(h, slot):
            for m in (0, 1):
                for r in range(ROW_CHUNK // rb):
                    sub_rows = slice(r * rb, (r + 1) * rb)
                    s = s_scr[slot, m, sub_rows, :]
                    top = jnp.max(s, axis=1, keepdims=True)
                    p_scr[slot, m, sub_rows, :] = jnp.exp2((s - top).astype(BF16))

        def pv(h, slot):
            cols = slice(LANES * h, LANES * (h + 1))
            v_ext = jnp.concatenate([v_scr[keys, cols], ones], axis=1)
            maps = []
            for m in (0, 1):
                acc = _dot(p_scr[slot, m], v_ext)
                maps.append(acc[:, 0:LANES] / acc[:, LANES:2 * LANES])
            o = maps[0] - lam * maps[1]
            ms = jnp.mean(o * o, axis=1, keepdims=True)
            o = o * lax.rsqrt(ms + EPS) * sub
            ha_scr[rows, cols] = (o * g_scr[rows, cols]).astype(BF16)

        _run_pipeline(n_heads, (qk, softmax, pv))
        return carry

    lax.fori_loop(0, n_rows // ROW_CHUNK, attend, 0)

    _out_proj_norm(x_ref, mod_ref, mod_row, ha_scr, w_out_ref, lng_ref, lnb_ref, y_ref, n_rows, alpha)

    if not latent:
        for blk in range(n_blocks):
            for copy in kv_out_copies(blk):
                copy.wait()


def _mod_kernel(cv_ref, w_ref, b_ref, o_ref):
    s = _silu(cv_ref[...])
    o_ref[0] = jnp.dot(s, w_ref[0], precision=lax.Precision.HIGHEST, preferred_element_type=F32) + b_ref[0]


def _full(shape, **kw):
    zeros = (0,) * len(shape)
    return pl.BlockSpec(shape, lambda i: zeros, **kw)


def _rope_tables(seq):
    t = np.arange(seq)
    n_freq = HEAD_DIM // 4
    freqs = ROPE_THETA ** (-np.arange(n_freq, dtype=np.float64) / n_freq)
    ang_row = (t // GRID_W)[:, None] * freqs
    ang_col = (t % GRID_W)[:, None] * freqs
    ang = np.concatenate([ang_row, ang_row, ang_col, ang_col], axis=1)
    sign = np.concatenate([-np.ones(n_freq), np.ones(n_freq)] * 2)[None, :]
    cos = np.tile(np.cos(ang), (1, 2)).astype(np.float32)
    sin = np.tile(np.sin(ang) * sign, (1, 2)).astype(np.float32)
    chunked_t = lambda a: a.reshape(seq // ROW_CHUNK, ROW_CHUNK, LANES).transpose(0, 2, 1)
    return jnp.asarray(cos), jnp.asarray(sin), jnp.asarray(chunked_t(cos)), jnp.asarray(chunked_t(sin))


def _head_mean_matrix():
    idx = np.arange(LANES) // HEAD_DIM
    return jnp.asarray((idx[:, None] == idx[None, :]).astype(np.float32) / HEAD_DIM, dtype=BF16)


def _modulation(c, c_ctx, w_mod, b_mod):
    depth = w_mod.shape[0]
    cv = jnp.concatenate([c_ctx[None, :], c, jnp.zeros((8 - 1 - c.shape[0], D_MODEL), F32)], axis=0)
    n_blk = 3 * D_MODEL // 1024
    return pl.pallas_call(
        _mod_kernel,
        grid=(depth, n_blk),
        in_specs=[pl.BlockSpec((8, D_MODEL), lambda l, n: (0, 0)),
                  pl.BlockSpec((1, D_MODEL, 1024), lambda l, n: (l, 0, n)),
                  pl.BlockSpec((1, 1, 1024), lambda l, n: (l, 0, n))],
        out_specs=pl.BlockSpec((1, 8, 1024), lambda l, n: (l, 0, n)),
        out_shape=jax.ShapeDtypeStruct((depth, 8, 3 * D_MODEL), F32),
        compiler_params=pltpu.CompilerParams(dimension_semantics=("arbitrary", "arbitrary")),
        name="adaln_modulation",
    )(cv, w_mod, b_mod.reshape(depth, 1, 3 * D_MODEL))


def _even_layer(x, mod, layer, w_in, w_out, q_norm, k_norm, sink, ln_g, ln_b, latent, seq, n_rows, alpha, extras=()):
    total = x.shape[0]
    grid = (total // n_rows,)
    single = pl.Buffered(1)
    sec = lambda lo, hi: w_in[:, lo:hi]
    w_main = jnp.concatenate([sec(0, 512), sec(1280, 1792), sec(768, 1280), sec(2048, 2560),
                              sec(640, 768), sec(1920, 2048)], axis=1).astype(BF16)
    w_kvt = jnp.concatenate([sec(512, 640), sec(1792, 1920), sec(640, 768), sec(1920, 2048)], axis=1).T.astype(BF16)
    qn = jnp.tile(q_norm, 2)[None, :]
    knt = jnp.broadcast_to(jnp.tile(k_norm, 2)[:, None], (LANES, ROW_CHUNK))

    row_blk = lambda width: pl.BlockSpec((n_rows, width), lambda i: (i, 0))
    in_specs = [row_blk(D_MODEL),
                pl.BlockSpec((1, 8, 3 * D_MODEL), lambda i: (layer, 0, 0)),
                _full(w_main.shape, pipeline_mode=single), _full(w_kvt.shape, pipeline_mode=single),
                _full((D_MODEL, D_MODEL), pipeline_mode=single),
                _full((1, LANES)), _full((LANES, ROW_CHUNK)),
                pl.BlockSpec(memory_space=pltpu.SMEM),
                pl.BlockSpec((1, 1, D_MODEL), lambda i: (layer, 0, 0)),
                pl.BlockSpec((1, 1, D_MODEL), lambda i: (layer, 0, 0)),
                _full((LANES, LANES))]
    args = [x, mod, w_main, w_kvt, w_out.astype(BF16), qn, knt, sink, ln_g, ln_b, _head_mean_matrix()]
    y_shape = jax.ShapeDtypeStruct((total, D_MODEL), F32)
    n_blocks = n_rows // ROW_CHUNK
    if latent:
        cos, sin, cos_t, sin_t, cakt, cav, cbkt, cbv = extras
        n_past = cav.shape[1]
        in_specs += [_full(cos.shape, pipeline_mode=single), _full(sin.shape, pipeline_mode=single),
                     _full(cos_t.shape, pipeline_mode=single), _full(sin_t.shape, pipeline_mode=single)]
        in_specs += [pl.BlockSpec((1, LANES, n_past), lambda i: (i, 0, 0)),
                     pl.BlockSpec((1, n_past, LANES), lambda i: (i, 0, 0))] * 2
        args += [cos, sin, cos_t, sin_t, cakt, cav, cbkt, cbv]
        out_specs = row_blk(D_MODEL)
        out_shape = y_shape
        n_keys = seq + n_past
    else:
        kv_blk = pl.BlockSpec((n_blocks, LANES, ROW_CHUNK), lambda i: (i, 0, 0))
        out_specs = [row_blk(D_MODEL)] + [kv_blk] * 4
        out_shape = [y_shape] + [jax.ShapeDtypeStruct((total // seq, LANES, seq), F32)] * 4
        n_keys = n_rows
    n_kchunks = n_keys // ROW_CHUNK
    n_cols = n_keys if latent else ROW_CHUNK
    scratch = [pltpu.VMEM((n_rows, D_MODEL), BF16),
               pltpu.VMEM((n_rows, 512), BF16), pltpu.VMEM((n_rows, 512), BF16),
               pltpu.VMEM((4, n_kchunks, LANES, ROW_CHUNK), BF16), pltpu.VMEM((4, n_keys, LANES), BF16),
               pltpu.VMEM((4, n_kchunks, LANES, ROW_CHUNK), BF16), pltpu.VMEM((4, n_keys, LANES), BF16),
               pltpu.VMEM((n_rows, D_MODEL), F32),
               pltpu.VMEM((2, 2, ROW_CHUNK, n_cols), F32),
               pltpu.VMEM((2, 2, ROW_CHUNK, n_cols), BF16),
               pltpu.VMEM((2, ROW_CHUNK, LANES), F32)]
    if latent:
        scratch.append(pltpu.VMEM((3, ROW_CHUNK, ROW_CHUNK), F32))
    return pl.pallas_call(
        functools.partial(_even_kernel, latent, n_rows, seq, alpha),
        grid=grid, in_specs=in_specs, out_specs=out_specs, out_shape=out_shape,
        scratch_shapes=scratch,
        compiler_params=pltpu.CompilerParams(dimension_semantics=("arbitrary",), vmem_limit_bytes=VMEM_LIMIT),
        name="even_layer_latent" if latent else "even_layer_context",
    )(*args)


def _odd_layer(x, mod, layer, w_in, w_out, lams, sub, ln_g, ln_b, latent, seq, n_rows, alpha, lam_init, extras=()):
    total = x.shape[0]
    grid = (total // n_rows,)
    row_blk = lambda width: pl.BlockSpec((n_rows, width), lambda i: (i, 0))
    single = pl.Buffered(1)
    in_specs = [row_blk(D_MODEL),
                pl.BlockSpec((1, 8, 3 * D_MODEL), lambda i: (layer, 0, 0)),
                _full(w_in.shape, pipeline_mode=single), _full(w_out.shape, pipeline_mode=single),
                _full((1, HEAD_DIM)), _full((1, HEAD_DIM)), _full((1, HEAD_DIM)), _full((1, HEAD_DIM)),
                _full((1, LANES)),
                pl.BlockSpec((1, 1, D_MODEL), lambda i: (layer, 0, 0)),
                pl.BlockSpec((1, 1, D_MODEL), lambda i: (layer, 0, 0))]
    args = [x, mod, w_in.astype(BF16), w_out.astype(BF16), *lams, sub, ln_g, ln_b]
    y_shape = jax.ShapeDtypeStruct((total, D_MODEL), F32)
    n_heads = D_MODEL // LANES
    n_blocks = n_rows // ROW_CHUNK
    if latent:
        cos, sin, cck, ccv = extras
        n_past = cck.shape[2]
        in_specs += [_full(cos.shape, pipeline_mode=single), _full(sin.shape, pipeline_mode=single)]
        in_specs += [pl.BlockSpec((1, 1, n_past, n_heads, LANES), lambda i: (i, layer // 2, 0, 0, 0))] * 2
        args += [cos, sin, cck, ccv]
        out_specs = row_blk(D_MODEL)
        out_shape = y_shape
        n_keys = seq + n_past
    else:
        kv_hbm = pl.BlockSpec(memory_space=pl.ANY)
        out_specs = [row_blk(D_MODEL), kv_hbm, kv_hbm]
        out_shape = [y_shape] + [jax.ShapeDtypeStruct((total // seq, 1, seq, n_heads, LANES), F32)] * 2
        n_keys = n_rows
    n_cols = n_keys if latent else ROW_CHUNK
    scratch = [pltpu.VMEM((n_rows, D_MODEL), BF16),
               pltpu.VMEM((n_rows, D_MODEL), BF16),
               pltpu.VMEM((2, n_keys, D_MODEL), BF16),
               pltpu.VMEM((n_keys, D_MODEL), BF16),
               pltpu.VMEM((n_rows, D_MODEL), F32),
               pltpu.VMEM((2, 2, ROW_CHUNK, n_cols), F32),
               pltpu.VMEM((2, 2, ROW_CHUNK, n_cols), BF16)]
    if not latent:
        scratch += [pltpu.VMEM((n_blocks, 2, ROW_CHUNK, D_MODEL), F32),
                    pltpu.SemaphoreType.DMA((n_blocks, 2))]
    return pl.pallas_call(
        functools.partial(_odd_kernel, latent, n_rows, seq, alpha, lam_init),
        grid=grid, in_specs=in_specs, out_specs=out_specs, out_shape=out_shape,
        scratch_shapes=scratch,
        compiler_params=pltpu.CompilerParams(dimension_semantics=("arbitrary",), vmem_limit_bytes=VMEM_LIMIT),
        name="odd_layer_latent" if latent else "odd_layer_context",
    )(*args)


def kernel(x_prompt, x_sample, cache_a_k, cache_a_v, cache_b_k, cache_b_v, cache_c_k, cache_c_v, c, c_ctx,
           w_mod, b_mod, ln_g, ln_b, w_in_even, w_out_even, q_norm_a, k_norm_a, sink_b, w_in_odd, w_out_odd,
           lambda_q1, lambda_k1, lambda_q2, lambda_k2, subln_c):
    depth = w_mod.shape[0]
    batch, seq, _ = x_prompt.shape
    dec_batch, dec_seq, _ = x_sample.shape
    n_past = cache_a_k.shape[2]
    alpha = (2 * depth) ** 0.25
    assert seq == ROW_CHUNK and n_past % ROW_CHUNK == 0 and dec_seq % ROW_CHUNK == 0

    mod = _modulation(c, c_ctx, w_mod, b_mod)
    ln_g3 = ln_g.reshape(depth, 1, D_MODEL)
    ln_b3 = ln_b.reshape(depth, 1, D_MODEL)
    cos, sin, cos_t, sin_t = _rope_tables(dec_seq)

    def run(x, latent, n_batch, s, rows_even, rows_odd):
        kv = {"a_k": [], "a_v": [], "b_k": [], "b_v": [], "c_k": [], "c_v": []}
        for l in range(depth):
            if l % 2 == 0:
                e = l // 2
                extras = ()
                if latent:
                    k_t = lambda t: t[:, e].transpose(0, 2, 3, 1).reshape(n_batch, LANES, n_past)
                    v_n = lambda t: t[:, e].reshape(n_batch, n_past, LANES)
                    extras = (cos, sin, cos_t, sin_t,
                              k_t(cache_a_k), v_n(cache_a_v), k_t(cache_b_k), v_n(cache_b_v))
                res = _even_layer(x, mod, l, w_in_even[e], w_out_even[e], q_norm_a[e], k_norm_a[e],
                                  sink_b[e], ln_g3, ln_b3, latent, s, rows_even, alpha, extras)
                if latent:
                    x = res
                else:
                    x = res[0]
                    for name, t in zip(("a_k", "a_v", "b_k", "b_v"), res[1:]):
                        kv[name].append(t.reshape(n_batch, 2, HEAD_DIM, s).transpose(0, 3, 1, 2))
            else:
                o = l // 2
                lam_init = 0.8 - 0.6 * math.exp(-0.3 * l)
                extras = (cos, sin, cache_c_k, cache_c_v) if latent else ()
                lams = [t[o][None, :] for t in (lambda_q1, lambda_k1, lambda_q2, lambda_k2)]
                res = _odd_layer(x, mod, l, w_in_odd[o], w_out_odd[o], lams,
                                 subln_c[o][None, :], ln_g3, ln_b3, latent, s, rows_odd, alpha, lam_init, extras)
                if latent:
                    x = res
                else:
                    x = res[0]
                    kv["c_k"].append(res[1][:, 0])
                    kv["c_v"].append(res[2][:, 0])
        return x, kv

    y_ctx, kv = run(x_prompt.reshape(batch * seq, D_MODEL), False, batch, seq, 1024, 512)
    y_lat, _ = run(x_sample.reshape(dec_batch * dec_seq, D_MODEL), True, dec_batch, dec_seq, dec_seq, dec_seq)

    stack = lambda name: jnp.stack(kv[name], axis=1)
    return (y_ctx.reshape(batch, seq, D_MODEL), y_lat.reshape(dec_batch, dec_seq, D_MODEL),
            stack("a_k"), stack("a_v"), stack("b_k"), stack("b_v"), stack("c_k"), stack("c_v"))
```

```python
import functools
import math

import jax
import jax.numpy as jnp
import numpy as np
from jax import lax
from jax.experimental import pallas as pl
from jax.experimental.pallas import tpu as pltpu

F32 = jnp.float32
BF16 = jnp.bfloat16

D_MODEL = 1024
HEAD_DIM = 64
GRID_W = 64
WINDOW = 128
ROPE_THETA = 10000.0
EPS = 1e-6
NEG_INF = -1e30
LOG2E = 1.4426950408889634
Q_SCALE = HEAD_DIM ** -0.5 * LOG2E
LANES = 128
ROW_CHUNK = 256
SOFTMAX_VREGS = 40
VMEM_LIMIT = 60000 * 1024


def _silu(x):
    return x / (1.0 + jnp.exp(-x))


def _dot(a, b):
    return jnp.dot(a, b, preferred_element_type=F32)


def _dot_nt(a, b):
    return lax.dot_general(a, b, (((1,), (1,)), ((), ())), preferred_element_type=F32)


def _lane_iota(rows):
    return lax.broadcasted_iota(jnp.int32, (rows, LANES), 1)


def _chunk_rows(i):
    if isinstance(i, int):
        return pl.ds(i * ROW_CHUNK, ROW_CHUNK)
    return pl.ds(pl.multiple_of(i * ROW_CHUNK, ROW_CHUNK), ROW_CHUNK)


def _softmax_rows(n_cols):
    rows = 8
    while rows * 2 * n_cols <= SOFTMAX_VREGS * 1024 and rows * 2 <= ROW_CHUNK:
        rows *= 2
    return rows


def _rope(a, cos, sin_signed):
    lane = _lane_iota(a.shape[0])
    fwd = pltpu.roll(a, LANES - 16, 1)
    bwd = pltpu.roll(a, 16, 1)
    partner = jnp.where((lane & 16) == 0, fwd, bwd)
    return a * cos + partner * sin_signed


def _rope_t(a, cos_t, sin_t):
    blocks = [a[16 * b:16 * (b + 1), :] for b in range(a.shape[0] // 16)]
    partner = jnp.concatenate([blocks[b ^ 1] for b in range(len(blocks))], axis=0)
    return a * cos_t + partner * sin_t


def _store_kt_variants(scr, chunk, kt):
    zero = jnp.zeros((HEAD_DIM, kt.shape[1]), F32)
    for j in range(2):
        kj = kt[HEAD_DIM * j:HEAD_DIM * (j + 1), :]
        scr[2 * j, chunk] = jnp.concatenate([kj, zero], axis=0).astype(BF16)
        scr[2 * j + 1, chunk] = jnp.concatenate([zero, kj], axis=0).astype(BF16)


def _store_v_variants(scr, rows, a):
    lane = _lane_iota(a.shape[0])
    lo = lane < HEAD_DIM
    swapped = pltpu.roll(a, HEAD_DIM, 1)
    one = jnp.ones_like(a)
    scr[0, rows, :] = jnp.where(lo, a, one).astype(BF16)
    scr[1, rows, :] = jnp.where(lo, one, swapped).astype(BF16)
    scr[2, rows, :] = jnp.where(lo, swapped, one).astype(BF16)
    scr[3, rows, :] = jnp.where(lo, one, a).astype(BF16)


def _layer_norm_rows(z, g, b):
    mu = jnp.mean(z, axis=-1, keepdims=True)
    zc = z - mu
    var = jnp.mean(zc * zc, axis=-1, keepdims=True)
    return zc * lax.rsqrt(var + EPS) * g + b


def _modulate(x_ref, mod_ref, mod_row, h_scr, n_rows):
    shift = mod_ref[0, pl.ds(mod_row, 1), 0:D_MODEL]
    scale = mod_ref[0, pl.ds(mod_row, 1), D_MODEL:2 * D_MODEL]

    def body(i, carry):
        rows = _chunk_rows(i)
        h_scr[rows, :] = (x_ref[rows, :] * (1.0 + scale) + shift).astype(BF16)
        return carry

    lax.fori_loop(0, n_rows // ROW_CHUNK, body, 0)


def _out_proj_norm(x_ref, mod_ref, mod_row, attn_scr, w_out_ref, lng_ref, lnb_ref, y_ref, n_rows, alpha):
    gate = mod_ref[0, pl.ds(mod_row, 1), 2 * D_MODEL:3 * D_MODEL]
    g = lng_ref[0]
    b = lnb_ref[0]

    def body(i, carry):
        rows = _chunk_rows(i)
        out = _dot(attn_scr[rows, :], w_out_ref[...])
        z = alpha * x_ref[rows, :] + gate * out
        y_ref[rows, :] = _layer_norm_rows(z, g, b)
        return carry

    lax.fori_loop(0, n_rows // ROW_CHUNK, body, 0, unroll=True)


def _run_pipeline(n_items, stages):
    for u in range(n_items + len(stages) - 1):
        for k, stage in enumerate(stages):
            t = u - k
            if 0 <= t < n_items:
                stage(t, t % 2)


def _even_kernel(latent, n_rows, seq, alpha, *refs):
    if latent:
        (x_ref, mod_ref, w_in_ref, w_out_ref, qn_ref, knt_ref, sink_ref, lng_ref, lnb_ref, pm_ref,
         cos_ref, sin_ref, cost_ref, sint_ref, cakt_ref, cav_ref, cbkt_ref, cbv_ref,
         y_ref,
         ha_scr, qa_scr, qb_scr, ka_scr, va_scr, kb_scr, vb_scr, g_scr, s_scr, p_scr, es_scr, wkt_scr, wv_scr,
         bias_scr) = refs
    else:
        (x_ref, mod_ref, w_in_ref, w_out_ref, qn_ref, knt_ref, sink_ref, lng_ref, lnb_ref, pm_ref,
         y_ref, nakt_ref, navt_ref, nbkt_ref, nbvt_ref,
         ha_scr, qa_scr, qb_scr, ka_scr, va_scr, kb_scr, vb_scr, g_scr, s_scr, p_scr, es_scr, wkt_scr,
         wv_scr) = refs

    step = pl.program_id(0)
    mod_row = step + 1 if latent else 0
    _modulate(x_ref, mod_ref, mod_row, ha_scr, n_rows)

    col_ka, col_va, col_kb, col_vb = 512, 640, 1792, 1920

    @pl.when(step == 0)
    def _():
        for r, c0 in enumerate((col_ka, col_kb, col_va, col_vb)):
            wkt_scr[LANES * r:LANES * (r + 1), :] = w_in_ref[:, c0:c0 + LANES].T
        wv_scr[:, 0:LANES] = w_in_ref[:, col_va:col_va + LANES]
        wv_scr[:, LANES:2 * LANES] = w_in_ref[:, col_vb:col_vb + LANES]

    n_lat_chunks = seq // ROW_CHUNK
    if latent:
        n_past = cav_ref.shape[1]
        past_rows = pl.ds(seq, n_past)
        _store_kt_variants(ka_scr, n_lat_chunks, cakt_ref[0])
        _store_kt_variants(kb_scr, n_lat_chunks, cbkt_ref[0])
        _store_v_variants(va_scr, past_rows, cav_ref[0])
        _store_v_variants(vb_scr, past_rows, cbv_ref[0])

    pm = pm_ref[...]
    qn = qn_ref[...]
    knt = knt_ref[...]

    def proj(i, carry):
        rows = _chunk_rows(i)
        hh = ha_scr[rows, :]
        if latent:
            cos = cos_ref[rows, :]
            sin = sin_ref[rows, :]
            rot = lambda a: _rope(a, cos, sin)
            rot_t = lambda a: _rope_t(a, cost_ref[i], sint_ref[i])
        else:
            rot = rot_t = lambda a: a

        acc = _dot(hh, w_in_ref[:, 0:512])
        for j in range(4):
            a = acc[:, LANES * j:LANES * (j + 1)]
            ms = _dot((a * a).astype(BF16), pm)
            a = rot(a * lax.rsqrt(ms + EPS) * qn)
            qa_scr[rows, LANES * j:LANES * (j + 1)] = (a * Q_SCALE).astype(BF16)
        acc = _dot(hh, w_in_ref[:, 1280:1792])
        for j in range(4):
            a = rot(acc[:, LANES * j:LANES * (j + 1)])
            qb_scr[rows, LANES * j:LANES * (j + 1)] = (a * Q_SCALE).astype(BF16)
        g_scr[rows, 0:512] = _silu(_dot(hh, w_in_ref[:, 768:1280]))
        g_scr[rows, 512:1024] = _silu(_dot(hh, w_in_ref[:, 2048:2560]))
        v = _dot(hh, wv_scr[...])
        _store_v_variants(va_scr, rows, v[:, 0:LANES])
        _store_v_variants(vb_scr, rows, v[:, LANES:2 * LANES])

        kt = _dot_nt(wkt_scr[0:2 * LANES, :], hh)
        kat = kt[0:LANES, :]
        ms = _dot(pm, (kat * kat).astype(BF16))
        kat = kat * lax.rsqrt(ms + EPS) * knt
        kbt = kt[LANES:2 * LANES, :]
        if not latent:
            vt = _dot_nt(wkt_scr[2 * LANES:4 * LANES, :], hh)
            nakt_ref[i] = kat
            nbkt_ref[i] = kbt
            navt_ref[i] = vt[0:LANES, :]
            nbvt_ref[i] = vt[LANES:2 * LANES, :]
        _store_kt_variants(ka_scr, i, rot_t(kat))
        _store_kt_variants(kb_scr, i, rot_t(kbt))
        return carry

    lax.fori_loop(0, n_rows // ROW_CHUNK, proj, 0, unroll=2)

    sinks = [sink_ref[h] * LOG2E for h in range(8)]
    ck = ROW_CHUNK
    n_win = 3

    def attend(i, carry):
        rows = _chunk_rows(i)
        if latent:
            a_chunks = list(range(n_lat_chunks + n_past // ck))
            a_keys = pl.ds(0, seq + n_past)
            w0 = jnp.clip(i - 1, 0, n_lat_chunks - n_win)
            win_rows = pl.ds(pl.multiple_of(w0 * ck, ck), n_win * ck)
            dist = (lax.broadcasted_iota(jnp.int32, (ROW_CHUNK, ck), 1)
                    - lax.broadcasted_iota(jnp.int32, (ROW_CHUNK, ck), 0))
            for c in range(n_win):
                off = (w0 + c - i) * ck
                bias_scr[c] = jnp.where(jnp.abs(dist + off) <= WINDOW, 0.0, NEG_INF).astype(F32)
            b_chunks = [w0 + c for c in range(n_win)] + [n_lat_chunks]
            b_cols = (n_win + 1) * ck
        else:
            a_chunks = [i]
            a_keys = rows
            b_chunks = [i]
            b_cols = ck
        a_cols = len(a_chunks) * ck

        def qk(t, slot):
            p, branch = divmod(t, 2)
            cols = slice(LANES * p, LANES * (p + 1))
            kvh = p // 2
            q = (qb_scr if branch else qa_scr)[rows, cols]
            k_scr = kb_scr if branch else ka_scr
            chunks = b_chunks if branch else a_chunks
            for par in (0, 1):
                for c, chunk in enumerate(chunks):
                    s = _dot(q, k_scr[2 * kvh + par, chunk])
                    if latent and branch and c < n_win:
                        s = s + bias_scr[c]
                    s_scr[slot, par, :, c * ck:(c + 1) * ck] = s

        def softmax(t, slot):
            p, branch = divmod(t, 2)
            n_cols = b_cols if branch else a_cols
            rb = _softmax_rows(n_cols)
            for par in (0, 1):
                for r in range(ROW_CHUNK // rb):
                    sub = slice(r * rb, (r + 1) * rb)
                    s = s_scr[slot, par, sub, 0:n_cols]
                    m = jnp.max(s, axis=1, keepdims=True)
                    if branch:
                        sink = sinks[2 * p + par]
                        m = jnp.maximum(m, sink)
                        es_scr[slot, sub, HEAD_DIM * par:HEAD_DIM * (par + 1)] = jnp.broadcast_to(
                            jnp.exp2(sink - m), (rb, HEAD_DIM))
                    p_scr[slot, par, sub, 0:n_cols] = jnp.exp2((s - m).astype(BF16))

        def pv(t, slot):
            p, branch = divmod(t, 2)
            kvh = p // 2
            v_scr = vb_scr if branch else va_scr
            accs = []
            for par in (0, 1):
                var = 2 * kvh + par
                if latent and branch:
                    n_loc = n_win * ck
                    accs.append(_dot(p_scr[slot, par, :, 0:n_loc], v_scr[var, win_rows, :])
                                + _dot(p_scr[slot, par, :, n_loc:b_cols], v_scr[var, past_rows, :]))
                else:
                    accs.append(_dot(p_scr[slot, par, :, 0:a_cols], v_scr[var, a_keys, :]))
            lo = _lane_iota(ROW_CHUNK) < HEAD_DIM
            denom = pltpu.roll(jnp.where(lo, accs[1], accs[0]), HEAD_DIM, 1)
            if branch:
                denom = denom + es_scr[slot]
            o = jnp.where(lo, accs[0], accs[1]) / denom
            ocols = slice(512 * branch + LANES * p, 512 * branch + LANES * (p + 1))
            ha_scr[rows, ocols] = (o * g_scr[rows, ocols]).astype(BF16)

        _run_pipeline(8, (qk, softmax, pv))
        return carry

    lax.fori_loop(0, n_rows // ROW_CHUNK, attend, 0)

    _out_proj_norm(x_ref, mod_ref, mod_row, ha_scr, w_out_ref, lng_ref, lnb_ref, y_ref, n_rows, alpha)


def _odd_kernel(latent, n_rows, seq, alpha, lam_init, *refs):
    if latent:
        (x_ref, mod_ref, w_in_ref, w_out_ref, lq1_ref, lk1_ref, lq2_ref, lk2_ref, sub_ref, lng_ref, lnb_ref,
         cos_ref, sin_ref, cck_ref, ccv_ref,
         y_ref,
         ha_scr, q_scr, k_scr, v_scr, g_scr, s_scr, p_scr) = refs
    else:
        (x_ref, mod_ref, w_in_ref, w_out_ref, lq1_ref, lk1_ref, lq2_ref, lk2_ref, sub_ref, lng_ref, lnb_ref,
         y_ref, nck_hbm, ncv_hbm,
         ha_scr, q_scr, k_scr, v_scr, g_scr, s_scr, p_scr, kv_stage, kv_sems) = refs

    step = pl.program_id(0)
    mod_row = step + 1 if latent else 0
    _modulate(x_ref, mod_ref, mod_row, ha_scr, n_rows)

    n_heads = D_MODEL // LANES
    n_blocks = n_rows // ROW_CHUNK
    lo = _lane_iota(ROW_CHUNK) < HEAD_DIM

    def kv_out_copies(blk):
        elem = step * n_blocks + blk
        return [pltpu.make_async_copy(kv_stage.at[blk, t, :, pl.ds(LANES * h, LANES)],
                                      out.at[elem, 0, :, h, :], kv_sems.at[blk, t])
                for t, out in enumerate((nck_hbm, ncv_hbm)) for h in range(n_heads)]

    def store_k(rows, h, a):
        cols = slice(LANES * h, LANES * (h + 1))
        zero = jnp.zeros_like(a)
        k_scr[0, rows, cols] = jnp.where(lo, a, zero).astype(BF16)
        k_scr[1, rows, cols] = jnp.where(lo, zero, a).astype(BF16)

    if latent:
        n_past = cck_ref.shape[2]
        past = pl.ds(seq, n_past)
        for h in range(n_heads):
            store_k(past, h, cck_ref[0, 0, :, h, :])
            v_scr[past, LANES * h:LANES * (h + 1)] = ccv_ref[0, 0, :, h, :].astype(BF16)

    def proj(i, carry):
        rows = _chunk_rows(i)
        hh = ha_scr[rows, :]
        if latent:
            cos = cos_ref[rows, :]
            sin = sin_ref[rows, :]
            rot = lambda a: _rope(a, cos, sin)
        else:
            rot = lambda a: a
        for half in range(2):
            acc = _dot(hh, w_in_ref[:, 512 * half:512 * (half + 1)])
            for j in range(4):
                a = rot(acc[:, LANES * j:LANES * (j + 1)])
                cols = slice(512 * half + LANES * j, 512 * half + LANES * (j + 1))
                q_scr[rows, cols] = (a * Q_SCALE).astype(BF16)
        for half in range(2):
            acc = _dot(hh, w_in_ref[:, 1024 + 512 * half:1024 + 512 * (half + 1)])
            if not latent:
                kv_stage[i, 0, :, 512 * half:512 * (half + 1)] = acc
            for j in range(4):
                store_k(rows, 4 * half + j, rot(acc[:, LANES * j:LANES * (j + 1)]))
        for half in range(2):
            acc = _dot(hh, w_in_ref[:, 2048 + 512 * half:2048 + 512 * (half + 1)])
            if not latent:
                kv_stage[i, 1, :, 512 * half:512 * (half + 1)] = acc
            v_scr[rows, 512 * half:512 * (half + 1)] = acc.astype(BF16)
        if not latent:
            for copy in kv_out_copies(i):
                copy.start()
        for half in range(2):
            acc = _dot(hh, w_in_ref[:, 3072 + 512 * half:3072 + 512 * (half + 1)])
            g_scr[rows, 512 * half:512 * (half + 1)] = _silu(acc)
        return carry

    if latent:
        lax.fori_loop(0, n_blocks, proj, 0, unroll=2)
    else:
        for blk in range(n_blocks):
            proj(blk, 0)

    lam = (jnp.exp(jnp.sum(lq1_ref[...] * lk1_ref[...], axis=1, keepdims=True))
           - jnp.exp(jnp.sum(lq2_ref[...] * lk2_ref[...], axis=1, keepdims=True)) + lam_init)
    sub = sub_ref[...] * (1.0 - lam_init)
    n_keys = seq + n_past if latent else ROW_CHUNK
    rb = _softmax_rows(n_keys)
    ones = jnp.ones((n_keys, LANES), BF16)

    def attend(i, carry):
        rows = _chunk_rows(i)
        keys = pl.ds(0, n_keys) if latent else rows

        def qk(h, slot):
            cols = slice(LANES * h, LANES * (h + 1))
            q = q_scr[rows, cols]
            for m in (0, 1):
                s_scr[slot, m] = _dot_nt(q, k_scr[m, keys, cols])

        def softmax(h, slot):
            for m in (0, 1):
                for r in range(ROW_CHUNK // rb):
                    sub_rows = slice(r * rb, (r + 1) * rb)
                    s = s_scr[slot, m, sub_rows, :]
                    top = jnp.max(s, axis=1, keepdims=True)
                    p_scr[slot, m, sub_rows, :] = jnp.exp2((s - top).astype(BF16))

        def pv(h, slot):
            cols = slice(LANES * h, LANES * (h + 1))
            v_ext = jnp.concatenate([v_scr[keys, cols], ones], axis=1)
            maps = []
            for m in (0, 1):
                acc = _dot(p_scr[slot, m], v_ext)
                maps.append(acc[:, 0:LANES] / acc[:, LANES:2 * LANES])
            o = maps[0] - lam * maps[1]
            ms = jnp.mean(o * o, axis=1, keepdims=True)
            o = o * lax.rsqrt(ms + EPS) * sub
            ha_scr[rows, cols] = (o * g_scr[rows, cols]).astype(BF16)

        _run_pipeline(n_heads, (qk, softmax, pv))
        return carry

    lax.fori_loop(0, n_rows // ROW_CHUNK, attend, 0)

    _out_proj_norm(x_ref, mod_ref, mod_row, ha_scr, w_out_ref, lng_ref, lnb_ref, y_ref, n_rows, alpha)

    if not latent:
        for blk in range(n_blocks):
            for copy in kv_out_copies(blk):
                copy.wait()


def _mod_kernel(n_cond, cvb_ref, w_ref, b_ref, o_ref, sb_scr):
    @pl.when((pl.program_id(0) == 0) & (pl.program_id(1) == 0))
    def _():
        sb_scr[...] = _silu(cvb_ref[...])

    n_out = w_ref.shape[2]
    sublanes = 8

    def body(kb, accs):
        rows = pl.ds(pl.multiple_of(kb * sublanes, sublanes), sublanes)
        w = w_ref[0, rows, :]
        return tuple(acc + w * jnp.tile(sb_scr[r, rows, :], (1, n_out // LANES)) for r, acc in enumerate(accs))

    zero = jnp.zeros((sublanes, n_out), F32)
    accs = lax.fori_loop(0, D_MODEL // sublanes, body, (zero,) * n_cond, unroll=8)
    out_rows = [jnp.sum(acc, axis=0, keepdims=True) + b_ref[0] for acc in accs]
    o_ref[0] = jnp.concatenate(out_rows + [jnp.zeros((8 - n_cond, n_out), F32)], axis=0)


def _full(shape, **kw):
    zeros = (0,) * len(shape)
    return pl.BlockSpec(shape, lambda i: zeros, **kw)


def _rope_tables(seq):
    t = np.arange(seq)
    n_freq = HEAD_DIM // 4
    freqs = ROPE_THETA ** (-np.arange(n_freq, dtype=np.float64) / n_freq)
    ang_row = (t // GRID_W)[:, None] * freqs
    ang_col = (t % GRID_W)[:, None] * freqs
    ang = np.concatenate([ang_row, ang_row, ang_col, ang_col], axis=1)
    sign = np.concatenate([-np.ones(n_freq), np.ones(n_freq)] * 2)[None, :]
    cos = np.tile(np.cos(ang), (1, 2)).astype(np.float32)
    sin = np.tile(np.sin(ang) * sign, (1, 2)).astype(np.float32)
    chunked_t = lambda a: a.reshape(seq // ROW_CHUNK, ROW_CHUNK, LANES).transpose(0, 2, 1)
    return jnp.asarray(cos), jnp.asarray(sin), jnp.asarray(chunked_t(cos)), jnp.asarray(chunked_t(sin))


def _head_mean_matrix():
    idx = np.arange(LANES) // HEAD_DIM
    return jnp.asarray((idx[:, None] == idx[None, :]).astype(np.float32) / HEAD_DIM, dtype=BF16)


def _modulation(c, c_ctx, w_mod, b_mod):
    depth = w_mod.shape[0]
    cv = jnp.concatenate([c_ctx[None, :], c], axis=0)
    n_cond = cv.shape[0]
    cvb = jnp.broadcast_to(cv[:, :, None], (n_cond, D_MODEL, LANES))
    n_blk = 3 * D_MODEL // 1024
    return pl.pallas_call(
        functools.partial(_mod_kernel, n_cond),
        grid=(depth, n_blk),
        in_specs=[pl.BlockSpec((n_cond, D_MODEL, LANES), lambda l, n: (0, 0, 0)),
                  pl.BlockSpec((1, D_MODEL, 1024), lambda l, n: (l, 0, n)),
                  pl.BlockSpec((1, 1, 1024), lambda l, n: (l, 0, n))],
        out_specs=pl.BlockSpec((1, 8, 1024), lambda l, n: (l, 0, n)),
        out_shape=jax.ShapeDtypeStruct((depth, 8, 3 * D_MODEL), F32),
        scratch_shapes=[pltpu.VMEM((n_cond, D_MODEL, LANES), F32)],
        compiler_params=pltpu.CompilerParams(dimension_semantics=("arbitrary", "arbitrary")),
        name="adaln_modulation",
    )(cvb, w_mod, b_mod.reshape(depth, 1, 3 * D_MODEL))


def _even_layer(x, mod, layer, w_in, w_out, q_norm, k_norm, sink, ln_g, ln_b, latent, seq, n_rows, alpha, extras=()):
    total = x.shape[0]
    grid = (total // n_rows,)
    single = pl.Buffered(1)
    qn = jnp.tile(q_norm, 2)[None, :]
    knt = jnp.broadcast_to(jnp.tile(k_norm, 2)[:, None], (LANES, ROW_CHUNK))

    row_blk = lambda width: pl.BlockSpec((n_rows, width), lambda i: (i, 0))
    in_specs = [row_blk(D_MODEL),
                pl.BlockSpec((1, 8, 3 * D_MODEL), lambda i: (layer, 0, 0)),
                _full(w_in.shape, pipeline_mode=single),
                _full((D_MODEL, D_MODEL), pipeline_mode=single),
                _full((1, LANES)), _full((LANES, ROW_CHUNK)),
                pl.BlockSpec(memory_space=pltpu.SMEM),
                pl.BlockSpec((1, 1, D_MODEL), lambda i: (layer, 0, 0)),
                pl.BlockSpec((1, 1, D_MODEL), lambda i: (layer, 0, 0)),
                _full((LANES, LANES))]
    args = [x, mod, w_in.astype(BF16), w_out.astype(BF16), qn, knt, sink, ln_g, ln_b, _head_mean_matrix()]
    y_shape = jax.ShapeDtypeStruct((total, D_MODEL), F32)
    n_blocks = n_rows // ROW_CHUNK
    if latent:
        cos, sin, cos_t, sin_t, cakt, cav, cbkt, cbv = extras
        n_past = cav.shape[1]
        in_specs += [_full(cos.shape, pipeline_mode=single), _full(sin.shape, pipeline_mode=single),
                     _full(cos_t.shape, pipeline_mode=single), _full(sin_t.shape, pipeline_mode=single)]
        in_specs += [pl.BlockSpec((1, LANES, n_past), lambda i: (i, 0, 0)),
                     pl.BlockSpec((1, n_past, LANES), lambda i: (i, 0, 0))] * 2
        args += [cos, sin, cos_t, sin_t, cakt, cav, cbkt, cbv]
        out_specs = row_blk(D_MODEL)
        out_shape = y_shape
        n_keys = seq + n_past
    else:
        kv_blk = pl.BlockSpec((n_blocks, LANES, ROW_CHUNK), lambda i: (i, 0, 0))
        out_specs = [row_blk(D_MODEL)] + [kv_blk] * 4
        out_shape = [y_shape] + [jax.ShapeDtypeStruct((total // seq, LANES, seq), F32)] * 4
        n_keys = n_rows
    n_kchunks = n_keys // ROW_CHUNK
    n_cols = n_keys if latent else ROW_CHUNK
    scratch = [pltpu.VMEM((n_rows, D_MODEL), BF16),
               pltpu.VMEM((n_rows, 512), BF16), pltpu.VMEM((n_rows, 512), BF16),
               pltpu.VMEM((4, n_kchunks, LANES, ROW_CHUNK), BF16), pltpu.VMEM((4, n_keys, LANES), BF16),
               pltpu.VMEM((4, n_kchunks, LANES, ROW_CHUNK), BF16), pltpu.VMEM((4, n_keys, LANES), BF16),
               pltpu.VMEM((n_rows, D_MODEL), F32),
               pltpu.VMEM((2, 2, ROW_CHUNK, n_cols), F32),
               pltpu.VMEM((2, 2, ROW_CHUNK, n_cols), BF16),
               pltpu.VMEM((2, ROW_CHUNK, LANES), F32),
               pltpu.VMEM((4 * LANES, D_MODEL), BF16),
               pltpu.VMEM((D_MODEL, 2 * LANES), BF16)]
    if latent:
        scratch.append(pltpu.VMEM((3, ROW_CHUNK, ROW_CHUNK), F32))
    return pl.pallas_call(
        functools.partial(_even_kernel, latent, n_rows, seq, alpha),
        grid=grid, in_specs=in_specs, out_specs=out_specs, out_shape=out_shape,
        scratch_shapes=scratch,
        compiler_params=pltpu.CompilerParams(dimension_semantics=("arbitrary",), vmem_limit_bytes=VMEM_LIMIT),
        name="even_layer_latent" if latent else "even_layer_context",
    )(*args)


def _odd_layer(x, mod, layer, w_in, w_out, lams, sub, ln_g, ln_b, latent, seq, n_rows, alpha, lam_init, extras=()):
    total = x.shape[0]
    grid = (total // n_rows,)
    row_blk = lambda width: pl.BlockSpec((n_rows, width), lambda i: (i, 0))
    single = pl.Buffered(1)
    in_specs = [row_blk(D_MODEL),
                pl.BlockSpec((1, 8, 3 * D_MODEL), lambda i: (layer, 0, 0)),
                _full(w_in.shape, pipeline_mode=single), _full(w_out.shape, pipeline_mode=single),
                _full((1, HEAD_DIM)), _full((1, HEAD_DIM)), _full((1, HEAD_DIM)), _full((1, HEAD_DIM)),
                _full((1, LANES)),
                pl.BlockSpec((1, 1, D_MODEL), lambda i: (layer, 0, 0)),
                pl.BlockSpec((1, 1, D_MODEL), lambda i: (layer, 0, 0))]
    args = [x, mod, w_in.astype(BF16), w_out.astype(BF16), *lams, sub, ln_g, ln_b]
    y_shape = jax.ShapeDtypeStruct((total, D_MODEL), F32)
    n_heads = D_MODEL // LANES
    n_blocks = n_rows // ROW_CHUNK
    if latent:
        cos, sin, cck, ccv = extras
        n_past = cck.shape[2]
        in_specs += [_full(cos.shape, pipeline_mode=single), _full(sin.shape, pipeline_mode=single)]
        in_specs += [pl.BlockSpec((1, 1, n_past, n_heads, LANES), lambda i: (i, layer // 2, 0, 0, 0))] * 2
        args += [cos, sin, cck, ccv]
        out_specs = row_blk(D_MODEL)
        out_shape = y_shape
        n_keys = seq + n_past
    else:
        kv_hbm = pl.BlockSpec(memory_space=pl.ANY)
        out_specs = [row_blk(D_MODEL), kv_hbm, kv_hbm]
        out_shape = [y_shape] + [jax.ShapeDtypeStruct((total // seq, 1, seq, n_heads, LANES), F32)] * 2
        n_keys = n_rows
    n_cols = n_keys if latent else ROW_CHUNK
    scratch = [pltpu.VMEM((n_rows, D_MODEL), BF16),
               pltpu.VMEM((n_rows, D_MODEL), BF16),
               pltpu.VMEM((2, n_keys, D_MODEL), BF16),
               pltpu.VMEM((n_keys, D_MODEL), BF16),
               pltpu.VMEM((n_rows, D_MODEL), F32),
               pltpu.VMEM((2, 2, ROW_CHUNK, n_cols), F32),
               pltpu.VMEM((2, 2, ROW_CHUNK, n_cols), BF16)]
    if not latent:
        scratch += [pltpu.VMEM((n_blocks, 2, ROW_CHUNK, D_MODEL), F32),
                    pltpu.SemaphoreType.DMA((n_blocks, 2))]
    return pl.pallas_call(
        functools.partial(_odd_kernel, latent, n_rows, seq, alpha, lam_init),
        grid=grid, in_specs=in_specs, out_specs=out_specs, out_shape=out_shape,
        scratch_shapes=scratch,
        compiler_params=pltpu.CompilerParams(dimension_semantics=("arbitrary",), vmem_limit_bytes=VMEM_LIMIT),
        name="odd_layer_latent" if latent else "odd_layer_context",
    )(*args)


def kernel(x_prompt, x_sample, cache_a_k, cache_a_v, cache_b_k, cache_b_v, cache_c_k, cache_c_v, c, c_ctx,
           w_mod, b_mod, ln_g, ln_b, w_in_even, w_out_even, q_norm_a, k_norm_a, sink_b, w_in_odd, w_out_odd,
           lambda_q1, lambda_k1, lambda_q2, lambda_k2, subln_c):
    depth = w_mod.shape[0]
    batch, seq, _ = x_prompt.shape
    dec_batch, dec_seq, _ = x_sample.shape
    n_past = cache_a_k.shape[2]
    alpha = (2 * depth) ** 0.25
    assert seq == ROW_CHUNK and n_past % ROW_CHUNK == 0 and dec_seq % ROW_CHUNK == 0

    mod = _modulation(c, c_ctx, w_mod, b_mod)
    ln_g3 = ln_g.reshape(depth, 1, D_MODEL)
    ln_b3 = ln_b.reshape(depth, 1, D_MODEL)
    cos, sin, cos_t, sin_t = _rope_tables(dec_seq)

    def run(x, latent, n_batch, s, rows_even, rows_odd):
        kv = {"a_k": [], "a_v": [], "b_k": [], "b_v": [], "c_k": [], "c_v": []}
        for l in range(depth):
            if l % 2 == 0:
                e = l // 2
                extras = ()
                if latent:
                    k_t = lambda t: t[:, e].transpose(0, 2, 3, 1).reshape(n_batch, LANES, n_past)
                    v_n = lambda t: t[:, e].reshape(n_batch, n_past, LANES)
                    extras = (cos, sin, cos_t, sin_t,
                              k_t(cache_a_k), v_n(cache_a_v), k_t(cache_b_k), v_n(cache_b_v))
                res = _even_layer(x, mod, l, w_in_even[e], w_out_even[e], q_norm_a[e], k_norm_a[e],
                                  sink_b[e], ln_g3, ln_b3, latent, s, rows_even, alpha, extras)
                if latent:
                    x = res
                else:
                    x = res[0]
                    for name, t in zip(("a_k", "a_v", "b_k", "b_v"), res[1:]):
                        kv[name].append(t.reshape(n_batch, 2, HEAD_DIM, s).transpose(0, 3, 1, 2))
            else:
                o = l // 2
                lam_init = 0.8 - 0.6 * math.exp(-0.3 * l)
                extras = (cos, sin, cache_c_k, cache_c_v) if latent else ()
                lams = [t[o][None, :] for t in (lambda_q1, lambda_k1, lambda_q2, lambda_k2)]
                res = _odd_layer(x, mod, l, w_in_odd[o], w_out_odd[o], lams,
                                 subln_c[o][None, :], ln_g3, ln_b3, latent, s, rows_odd, alpha, lam_init, extras)
                if latent:
                    x = res
                else:
                    x = res[0]
                    kv["c_k"].append(res[1][:, 0])
                    kv["c_v"].append(res[2][:, 0])
        return x, kv

    y_ctx, kv = run(x_prompt.reshape(batch * seq, D_MODEL), False, batch, seq, 1024, 512)
    y_lat, _ = run(x_sample.reshape(dec_batch * dec_seq, D_MODEL), True, dec_batch, dec_seq, dec_seq, dec_seq)

    stack = lambda name: jnp.stack(kv[name], axis=1)
    return (y_ctx.reshape(batch, seq, D_MODEL), y_lat.reshape(dec_batch, dec_seq, D_MODEL),
            stack("a_k"), stack("a_v"), stack("b_k"), stack("b_v"), stack("c_k"), stack("c_v"))
```

```python
import functools
import math

import jax
import jax.numpy as jnp
import numpy as np
from jax import lax
from jax.experimental import pallas as pl
from jax.experimental.pallas import tpu as pltpu

F32 = jnp.float32
BF16 = jnp.bfloat16

D_MODEL = 1024
HEAD_DIM = 64
GRID_W = 64
WINDOW = 128
ROPE_THETA = 10000.0
EPS = 1e-6
NEG_INF = -1e30
LOG2E = 1.4426950408889634
Q_SCALE = HEAD_DIM ** -0.5 * LOG2E
LANES = 128
ROW_CHUNK = 256
SOFTMAX_VREGS = 40
VMEM_LIMIT = 60000 * 1024


def _silu(x):
    return x / (1.0 + jnp.exp(-x))


def _dot(a, b):
    return jnp.dot(a, b, preferred_element_type=F32)


def _dot_nt(a, b):
    return lax.dot_general(a, b, (((1,), (1,)), ((), ())), preferred_element_type=F32)


def _lane_iota(rows):
    return lax.broadcasted_iota(jnp.int32, (rows, LANES), 1)


def _chunk_rows(i):
    if isinstance(i, int):
        return pl.ds(i * ROW_CHUNK, ROW_CHUNK)
    return pl.ds(pl.multiple_of(i * ROW_CHUNK, ROW_CHUNK), ROW_CHUNK)


def _softmax_rows(n_cols):
    rows = 8
    while rows * 2 * n_cols <= SOFTMAX_VREGS * 1024 and rows * 2 <= ROW_CHUNK:
        rows *= 2
    return rows


def _rope(a, cos, sin_signed):
    lane = _lane_iota(a.shape[0])
    fwd = pltpu.roll(a, LANES - 16, 1)
    bwd = pltpu.roll(a, 16, 1)
    partner = jnp.where((lane & 16) == 0, fwd, bwd)
    return a * cos + partner * sin_signed


def _rope_t(a, cos_t, sin_t):
    blocks = [a[16 * b:16 * (b + 1), :] for b in range(a.shape[0] // 16)]
    partner = jnp.concatenate([blocks[b ^ 1] for b in range(len(blocks))], axis=0)
    return a * cos_t + partner * sin_t


def _store_kt_variants(scr, chunk, kt):
    width = scr.shape[-1]
    per_block = kt.shape[1] // width
    zero = jnp.zeros((HEAD_DIM, kt.shape[1]), F32)
    for j in range(2):
        kj = kt[HEAD_DIM * j:HEAD_DIM * (j + 1), :]
        for par, full in enumerate((jnp.concatenate([kj, zero], axis=0), jnp.concatenate([zero, kj], axis=0))):
            full = full.astype(BF16)
            for c in range(per_block):
                scr[2 * j + par, chunk * per_block + c] = full[:, width * c:width * (c + 1)]


def _store_v_variants(scr, rows, a):
    lane = _lane_iota(a.shape[0])
    lo = lane < HEAD_DIM
    swapped = pltpu.roll(a, HEAD_DIM, 1)
    one = jnp.ones_like(a)
    scr[0, rows, :] = jnp.where(lo, a, one).astype(BF16)
    scr[1, rows, :] = jnp.where(lo, one, swapped).astype(BF16)
    scr[2, rows, :] = jnp.where(lo, swapped, one).astype(BF16)
    scr[3, rows, :] = jnp.where(lo, one, a).astype(BF16)


def _layer_norm_rows(z, g, b):
    mu = jnp.mean(z, axis=-1, keepdims=True)
    zc = z - mu
    var = jnp.mean(zc * zc, axis=-1, keepdims=True)
    return zc * lax.rsqrt(var + EPS) * g + b


def _modulate(x_ref, mod_ref, mod_row, h_scr, n_rows):
    shift = mod_ref[0, pl.ds(mod_row, 1), 0:D_MODEL]
    scale = mod_ref[0, pl.ds(mod_row, 1), D_MODEL:2 * D_MODEL]

    def body(i, carry):
        rows = _chunk_rows(i)
        h_scr[rows, :] = (x_ref[rows, :] * (1.0 + scale) + shift).astype(BF16)
        return carry

    lax.fori_loop(0, n_rows // ROW_CHUNK, body, 0)


def _out_proj_norm(x_ref, mod_ref, mod_row, attn_scr, w_out_ref, lng_ref, lnb_ref, y_ref, n_rows, alpha):
    gate = mod_ref[0, pl.ds(mod_row, 1), 2 * D_MODEL:3 * D_MODEL]
    g = lng_ref[0]
    b = lnb_ref[0]

    def body(i, carry):
        rows = _chunk_rows(i)
        out = _dot(attn_scr[rows, :], w_out_ref[...])
        z = alpha * x_ref[rows, :] + gate * out
        y_ref[rows, :] = _layer_norm_rows(z, g, b)
        return carry

    lax.fori_loop(0, n_rows // ROW_CHUNK, body, 0, unroll=True)


def _run_pipeline(n_items, stages):
    for u in range(n_items + len(stages) - 1):
        for k, stage in enumerate(stages):
            t = u - k
            if 0 <= t < n_items:
                stage(t, t % 2)


def _even_kernel(latent, n_rows, seq, alpha, *refs):
    if latent:
        (x_ref, mod_ref, w_in_ref, w_out_ref, qn_ref, knt_ref, sink_ref, lng_ref, lnb_ref, pm_ref,
         cos_ref, sin_ref, cost_ref, sint_ref, cakt_ref, cav_ref, cbkt_ref, cbv_ref,
         y_ref,
         ha_scr, qa_scr, qb_scr, ka_scr, va_scr, kb_scr, vb_scr, g_scr, s_scr, p_scr, es_scr, wkt_scr, wv_scr,
         bias_scr) = refs
    else:
        (x_ref, mod_ref, w_in_ref, w_out_ref, qn_ref, knt_ref, sink_ref, lng_ref, lnb_ref, pm_ref,
         y_ref, nakt_ref, navt_ref, nbkt_ref, nbvt_ref,
         ha_scr, qa_scr, qb_scr, ka_scr, va_scr, kb_scr, vb_scr, g_scr, s_scr, p_scr, es_scr, wkt_scr,
         wv_scr) = refs

    step = pl.program_id(0)
    mod_row = step + 1 if latent else 0
    _modulate(x_ref, mod_ref, mod_row, ha_scr, n_rows)

    col_ka, col_va, col_kb, col_vb = 512, 640, 1792, 1920

    @pl.when(step == 0)
    def _():
        for r, c0 in enumerate((col_ka, col_kb)):
            wkt_scr[LANES * r:LANES * (r + 1), :] = w_in_ref[:, c0:c0 + LANES].T
        wv_scr[:, 0:LANES] = w_in_ref[:, col_va:col_va + LANES]
        wv_scr[:, LANES:2 * LANES] = w_in_ref[:, col_vb:col_vb + LANES]

    n_lat_chunks = seq // ROW_CHUNK
    if latent:
        n_past = cav_ref.shape[1]
        past_rows = pl.ds(seq, n_past)
        _store_kt_variants(ka_scr, n_lat_chunks, cakt_ref[0])
        _store_kt_variants(kb_scr, n_lat_chunks, cbkt_ref[0])
        _store_v_variants(va_scr, past_rows, cav_ref[0])
        _store_v_variants(vb_scr, past_rows, cbv_ref[0])

    pm = pm_ref[...]
    qn = qn_ref[...]
    knt = knt_ref[...]

    def proj(i, carry):
        rows = _chunk_rows(i)
        hh = ha_scr[rows, :]
        if latent:
            cos = cos_ref[rows, :]
            sin = sin_ref[rows, :]
            rot = lambda a: _rope(a, cos, sin)
            rot_t = lambda a: _rope_t(a, cost_ref[i], sint_ref[i])
        else:
            rot = rot_t = lambda a: a

        acc = _dot(hh, w_in_ref[:, 0:512])
        for j in range(4):
            a = acc[:, LANES * j:LANES * (j + 1)]
            ms = _dot((a * a).astype(BF16), pm)
            a = rot(a * lax.rsqrt(ms + EPS) * qn)
            qa_scr[rows, LANES * j:LANES * (j + 1)] = (a * Q_SCALE).astype(BF16)
        acc = _dot(hh, w_in_ref[:, 1280:1792])
        for j in range(4):
            a = rot(acc[:, LANES * j:LANES * (j + 1)])
            qb_scr[rows, LANES * j:LANES * (j + 1)] = (a * Q_SCALE).astype(BF16)
        g_scr[rows, 0:512] = _silu(_dot(hh, w_in_ref[:, 768:1280]))
        g_scr[rows, 512:1024] = _silu(_dot(hh, w_in_ref[:, 2048:2560]))
        v = _dot(hh, wv_scr[...])
        _store_v_variants(va_scr, rows, v[:, 0:LANES])
        _store_v_variants(vb_scr, rows, v[:, LANES:2 * LANES])

        kt = _dot_nt(wkt_scr[0:2 * LANES, :], hh)
        kat = kt[0:LANES, :]
        ms = _dot(pm, (kat * kat).astype(BF16))
        kat = kat * lax.rsqrt(ms + EPS) * knt
        kbt = kt[LANES:2 * LANES, :]
        if not latent:
            vt = v.T
            nakt_ref[i] = kat
            nbkt_ref[i] = kbt
            navt_ref[i] = vt[0:LANES, :]
            nbvt_ref[i] = vt[LANES:2 * LANES, :]
        _store_kt_variants(ka_scr, i, rot_t(kat))
        _store_kt_variants(kb_scr, i, rot_t(kbt))
        return carry

    lax.fori_loop(0, n_rows // ROW_CHUNK, proj, 0, unroll=2)

    sinks = [sink_ref[h] * LOG2E for h in range(8)]
    ck = ROW_CHUNK
    bk = kb_scr.shape[-1]
    win = ROW_CHUNK + 2 * WINDOW

    def attend(i, carry):
        rows = _chunk_rows(i)
        if latent:
            a_chunks = list(range(n_lat_chunks + n_past // ck))
            a_keys = pl.ds(0, seq + n_past)
            w0 = jnp.clip(i * (ck // bk) - WINDOW // bk, 0, (seq - win) // bk)
            win_rows = pl.ds(pl.multiple_of(w0 * bk, bk), win)
            dist = (lax.broadcasted_iota(jnp.int32, (ROW_CHUNK, ck), 1)
                    - lax.broadcasted_iota(jnp.int32, (ROW_CHUNK, ck), 0))
            for c in range(win // ck):
                off = w0 * bk + c * ck - i * ck
                bias_scr[c] = jnp.where(jnp.abs(dist + off) <= WINDOW, 0.0, NEG_INF).astype(F32)
            b_first = [w0 + c * (ck // bk) for c in range(win // ck)] + [seq // bk]
            n_biased = win // ck
            b_cols = win + n_past
        else:
            a_chunks = [i]
            a_keys = rows
            b_first = [i * (ck // bk)]
            n_biased = 0
            b_cols = ck
        a_cols = len(a_chunks) * ck

        def qk(t, slot):
            p, branch = divmod(t, 2)
            cols = slice(LANES * p, LANES * (p + 1))
            kvh = p // 2
            q = (qb_scr if branch else qa_scr)[rows, cols]
            for par in (0, 1):
                var = 2 * kvh + par
                if branch:
                    tiles = [jnp.concatenate([kb_scr[var, first + d] for d in range(ck // bk)], axis=1)
                             for first in b_first]
                else:
                    tiles = [ka_scr[var, chunk] for chunk in a_chunks]
                for c, kt in enumerate(tiles):
                    s = _dot(q, kt)
                    if branch and c < n_biased:
                        s = s + bias_scr[c]
                    s_scr[slot, par, :, c * ck:(c + 1) * ck] = s

        def softmax(t, slot):
            p, branch = divmod(t, 2)
            n_cols = b_cols if branch else a_cols
            rb = _softmax_rows(n_cols)
            for par in (0, 1):
                for r in range(ROW_CHUNK // rb):
                    sub = slice(r * rb, (r + 1) * rb)
                    s = s_scr[slot, par, sub, 0:n_cols]
                    m = jnp.max(s, axis=1, keepdims=True)
                    if branch:
                        sink = sinks[2 * p + par]
                        m = jnp.maximum(m, sink)
                        es_scr[slot, sub, HEAD_DIM * par:HEAD_DIM * (par + 1)] = jnp.broadcast_to(
                            jnp.exp2(sink - m), (rb, HEAD_DIM))
                    p_scr[slot, par, sub, 0:n_cols] = jnp.exp2((s - m).astype(BF16))

        def pv(t, slot):
            p, branch = divmod(t, 2)
            kvh = p // 2
            v_scr = vb_scr if branch else va_scr
            accs = []
            for par in (0, 1):
                var = 2 * kvh + par
                if latent and branch:
                    n_loc = win
                    accs.append(_dot(p_scr[slot, par, :, 0:n_loc], v_scr[var, win_rows, :])
                                + _dot(p_scr[slot, par, :, n_loc:b_cols], v_scr[var, past_rows, :]))
                else:
                    accs.append(_dot(p_scr[slot, par, :, 0:a_cols], v_scr[var, a_keys, :]))
            lo = _lane_iota(ROW_CHUNK) < HEAD_DIM
            denom = pltpu.roll(jnp.where(lo, accs[1], accs[0]), HEAD_DIM, 1)
            if branch:
                denom = denom + es_scr[slot]
            o = jnp.where(lo, accs[0], accs[1]) / denom
            ocols = slice(512 * branch + LANES * p, 512 * branch + LANES * (p + 1))
            ha_scr[rows, ocols] = (o * g_scr[rows, ocols]).astype(BF16)

        _run_pipeline(8, (qk, softmax, pv))
        return carry

    lax.fori_loop(0, n_rows // ROW_CHUNK, attend, 0)

    _out_proj_norm(x_ref, mod_ref, mod_row, ha_scr, w_out_ref, lng_ref, lnb_ref, y_ref, n_rows, alpha)


def _odd_kernel(latent, n_rows, seq, alpha, lam_init, *refs):
    if latent:
        (x_ref, mod_ref, w_in_ref, w_out_ref, lq1_ref, lk1_ref, lq2_ref, lk2_ref, sub_ref, lng_ref, lnb_ref,
         cos_ref, sin_ref, cck_ref, ccv_ref,
         y_ref,
         ha_scr, q_scr, k_scr, v_scr, g_scr, s_scr, p_scr) = refs
    else:
        (x_ref, mod_ref, w_in_ref, w_out_ref, lq1_ref, lk1_ref, lq2_ref, lk2_ref, sub_ref, lng_ref, lnb_ref,
         y_ref, nck_hbm, ncv_hbm,
         ha_scr, q_scr, k_scr, v_scr, g_scr, s_scr, p_scr, kv_stage, kv_sems) = refs

    step = pl.program_id(0)
    mod_row = step + 1 if latent else 0
    _modulate(x_ref, mod_ref, mod_row, ha_scr, n_rows)

    n_heads = D_MODEL // LANES
    n_blocks = n_rows // ROW_CHUNK
    lo = _lane_iota(ROW_CHUNK) < HEAD_DIM

    def kv_out_copies(blk):
        elem = step * n_blocks + blk
        return [pltpu.make_async_copy(kv_stage.at[blk, t, :, pl.ds(LANES * h, LANES)],
                                      out.at[elem, 0, :, h, :], kv_sems.at[blk, t])
                for t, out in enumerate((nck_hbm, ncv_hbm)) for h in range(n_heads)]

    def store_k(rows, h, a):
        cols = slice(LANES * h, LANES * (h + 1))
        zero = jnp.zeros_like(a)
        k_scr[0, rows, cols] = jnp.where(lo, a, zero).astype(BF16)
        k_scr[1, rows, cols] = jnp.where(lo, zero, a).astype(BF16)

    if latent:
        n_past = cck_ref.shape[2]
        past = pl.ds(seq, n_past)
        for h in range(n_heads):
            store_k(past, h, cck_ref[0, 0, :, h, :])
            v_scr[past, LANES * h:LANES * (h + 1)] = ccv_ref[0, 0, :, h, :].astype(BF16)

    def proj(i, carry):
        rows = _chunk_rows(i)
        hh = ha_scr[rows, :]
        if latent:
            cos = cos_ref[rows, :]
            sin = sin_ref[rows, :]
            rot = lambda a: _rope(a, cos, sin)
        else:
            rot = lambda a: a
        for half in range(2):
            acc = _dot(hh, w_in_ref[:, 512 * half:512 * (half + 1)])
            for j in range(4):
                a = rot(acc[:, LANES * j:LANES * (j + 1)])
                cols = slice(512 * half + LANES * j, 512 * half + LANES * (j + 1))
                q_scr[rows, cols] = (a * Q_SCALE).astype(BF16)
        for half in range(2):
            acc = _dot(hh, w_in_ref[:, 1024 + 512 * half:1024 + 512 * (half + 1)])
            if not latent:
                kv_stage[i, 0, :, 512 * half:512 * (half + 1)] = acc
            for j in range(4):
                store_k(rows, 4 * half + j, rot(acc[:, LANES * j:LANES * (j + 1)]))
        for half in range(2):
            acc = _dot(hh, w_in_ref[:, 2048 + 512 * half:2048 + 512 * (half + 1)])
            if not latent:
                kv_stage[i, 1, :, 512 * half:512 * (half + 1)] = acc
            v_scr[rows, 512 * half:512 * (half + 1)] = acc.astype(BF16)
        if not latent:
            for copy in kv_out_copies(i):
                copy.start()
        for half in range(2):
            acc = _dot(hh, w_in_ref[:, 3072 + 512 * half:3072 + 512 * (half + 1)])
            g_scr[rows, 512 * half:512 * (half + 1)] = _silu(acc)
        return carry

    if latent:
        lax.fori_loop(0, n_blocks, proj, 0, unroll=2)
    else:
        for blk in range(n_blocks):
            proj(blk, 0)

    lam = (jnp.exp(jnp.sum(lq1_ref[...] * lk1_ref[...], axis=1, keepdims=True))
           - jnp.exp(jnp.sum(lq2_ref[...] * lk2_ref[...], axis=1, keepdims=True)) + lam_init)
    sub = sub_ref[...] * (1.0 - lam_init)
    n_keys = seq + n_past if latent else ROW_CHUNK
    rb = _softmax_rows(n_keys)
    ones = jnp.ones((n_keys, LANES), BF16)

    def attend(i, carry):
        rows = _chunk_rows(i)
        keys = pl.ds(0, n_keys) if latent else rows

        def qk(h, slot):
            cols = slice(LANES * h, LANES * (h + 1))
            q = q_scr[rows, cols]
            for m in (0, 1):
                s_scr[slot, m] = _dot_nt(q, k_scr[m, keys, cols])

        def softmax(h, slot):
            for m in (0, 1):
                for r in range(ROW_CHUNK // rb):
                    sub_rows = slice(r * rb, (r + 1) * rb)
                    s = s_scr[slot, m, sub_rows, :]
                    top = jnp.max(s, axis=1, keepdims=True)
                    p_scr[slot, m, sub_rows, :] = jnp.exp2((s - top).astype(BF16))

        def pv(h, slot):
            cols = slice(LANES * h, LANES * (h + 1))
            v_ext = jnp.concatenate([v_scr[keys, cols], ones], axis=1)
            maps = []
            for m in (0, 1):
                acc = _dot(p_scr[slot, m], v_ext)
                maps.append(acc[:, 0:LANES] / acc[:, LANES:2 * LANES])
            o = maps[0] - lam * maps[1]
            ms = jnp.mean(o * o, axis=1, keepdims=True)
            o = o * lax.rsqrt(ms + EPS) * sub
            ha_scr[rows, cols] = (o * g_scr[rows, cols]).astype(BF16)

        _run_pipeline(n_heads, (qk, softmax, pv))
        return carry

    lax.fori_loop(0, n_rows // ROW_CHUNK, attend, 0)

    _out_proj_norm(x_ref, mod_ref, mod_row, ha_scr, w_out_ref, lng_ref, lnb_ref, y_ref, n_rows, alpha)

    if not latent:
        for blk in range(n_blocks):
            for copy in kv_out_copies(blk):
                copy.wait()


def _mod_kernel(n_cond, cvb_ref, w_ref, b_ref, o_ref, sb_scr):
    @pl.when((pl.program_id(0) == 0) & (pl.program_id(1) == 0))
    def _():
        sb_scr[...] = _silu(cvb_ref[...])

    n_out = w_ref.shape[2]
    sublanes = 8

    def body(kb, accs):
        rows = pl.ds(pl.multiple_of(kb * sublanes, sublanes), sublanes)
        w = w_ref[0, rows, :]
        return tuple(acc + w * jnp.tile(sb_scr[r, rows, :], (1, n_out // LANES)) for r, acc in enumerate(accs))

    zero = jnp.zeros((sublanes, n_out), F32)
    accs = lax.fori_loop(0, D_MODEL // sublanes, body, (zero,) * n_cond, unroll=8)
    out_rows = [jnp.sum(acc, axis=0, keepdims=True) + b_ref[0] for acc in accs]
    o_ref[0] = jnp.concatenate(out_rows + [jnp.zeros((8 - n_cond, n_out), F32)], axis=0)


def _full(shape, **kw):
    zeros = (0,) * len(shape)
    return pl.BlockSpec(shape, lambda i: zeros, **kw)


def _rope_tables(seq):
    t = np.arange(seq)
    n_freq = HEAD_DIM // 4
    freqs = ROPE_THETA ** (-np.arange(n_freq, dtype=np.float64) / n_freq)
    ang_row = (t // GRID_W)[:, None] * freqs
    ang_col = (t % GRID_W)[:, None] * freqs
    ang = np.concatenate([ang_row, ang_row, ang_col, ang_col], axis=1)
    sign = np.concatenate([-np.ones(n_freq), np.ones(n_freq)] * 2)[None, :]
    cos = np.tile(np.cos(ang), (1, 2)).astype(np.float32)
    sin = np.tile(np.sin(ang) * sign, (1, 2)).astype(np.float32)
    chunked_t = lambda a: a.reshape(seq // ROW_CHUNK, ROW_CHUNK, LANES).transpose(0, 2, 1)
    return jnp.asarray(cos), jnp.asarray(sin), jnp.asarray(chunked_t(cos)), jnp.asarray(chunked_t(sin))


def _head_mean_matrix():
    idx = np.arange(LANES) // HEAD_DIM
    return jnp.asarray((idx[:, None] == idx[None, :]).astype(np.float32) / HEAD_DIM, dtype=BF16)


def _modulation(c, c_ctx, w_mod, b_mod):
    depth = w_mod.shape[0]
    cv = jnp.concatenate([c_ctx[None, :], c], axis=0)
    n_cond = cv.shape[0]
    cvb = jnp.broadcast_to(cv[:, :, None], (n_cond, D_MODEL, LANES))
    n_blk = 3 * D_MODEL // 1024
    return pl.pallas_call(
        functools.partial(_mod_kernel, n_cond),
        grid=(depth, n_blk),
        in_specs=[pl.BlockSpec((n_cond, D_MODEL, LANES), lambda l, n: (0, 0, 0)),
                  pl.BlockSpec((1, D_MODEL, 1024), lambda l, n: (l, 0, n)),
                  pl.BlockSpec((1, 1, 1024), lambda l, n: (l, 0, n))],
        out_specs=pl.BlockSpec((1, 8, 1024), lambda l, n: (l, 0, n)),
        out_shape=jax.ShapeDtypeStruct((depth, 8, 3 * D_MODEL), F32),
        scratch_shapes=[pltpu.VMEM((n_cond, D_MODEL, LANES), F32)],
        compiler_params=pltpu.CompilerParams(dimension_semantics=("arbitrary", "arbitrary")),
        name="adaln_modulation",
    )(cvb, w_mod, b_mod.reshape(depth, 1, 3 * D_MODEL))


def _even_layer(x, mod, layer, w_in, w_out, q_norm, k_norm, sink, ln_g, ln_b, latent, seq, n_rows, alpha, extras=()):
    total = x.shape[0]
    grid = (total // n_rows,)
    single = pl.Buffered(1)
    qn = jnp.tile(q_norm, 2)[None, :]
    knt = jnp.broadcast_to(jnp.tile(k_norm, 2)[:, None], (LANES, ROW_CHUNK))

    row_blk = lambda width: pl.BlockSpec((n_rows, width), lambda i: (i, 0))
    in_specs = [row_blk(D_MODEL),
                pl.BlockSpec((1, 8, 3 * D_MODEL), lambda i: (layer, 0, 0)),
                _full(w_in.shape, pipeline_mode=single),
                _full((D_MODEL, D_MODEL), pipeline_mode=single),
                _full((1, LANES)), _full((LANES, ROW_CHUNK)),
                pl.BlockSpec(memory_space=pltpu.SMEM),
                pl.BlockSpec((1, 1, D_MODEL), lambda i: (layer, 0, 0)),
                pl.BlockSpec((1, 1, D_MODEL), lambda i: (layer, 0, 0)),
                _full((LANES, LANES))]
    args = [x, mod, w_in.astype(BF16), w_out.astype(BF16), qn, knt, sink, ln_g, ln_b, _head_mean_matrix()]
    y_shape = jax.ShapeDtypeStruct((total, D_MODEL), F32)
    n_blocks = n_rows // ROW_CHUNK
    if latent:
        cos, sin, cos_t, sin_t, cakt, cav, cbkt, cbv = extras
        n_past = cav.shape[1]
        in_specs += [_full(cos.shape, pipeline_mode=single), _full(sin.shape, pipeline_mode=single),
                     _full(cos_t.shape, pipeline_mode=single), _full(sin_t.shape, pipeline_mode=single)]
        in_specs += [pl.BlockSpec((1, LANES, n_past), lambda i: (i, 0, 0)),
                     pl.BlockSpec((1, n_past, LANES), lambda i: (i, 0, 0))] * 2
        args += [cos, sin, cos_t, sin_t, cakt, cav, cbkt, cbv]
        out_specs = row_blk(D_MODEL)
        out_shape = y_shape
        n_keys = seq + n_past
    else:
        kv_blk = pl.BlockSpec((n_blocks, LANES, ROW_CHUNK), lambda i: (i, 0, 0))
        out_specs = [row_blk(D_MODEL)] + [kv_blk] * 4
        out_shape = [y_shape] + [jax.ShapeDtypeStruct((total // seq, LANES, seq), F32)] * 4
        n_keys = n_rows
    n_kchunks = n_keys // ROW_CHUNK
    n_cols = n_keys if latent else ROW_CHUNK
    scratch = [pltpu.VMEM((n_rows, D_MODEL), BF16),
               pltpu.VMEM((n_rows, 512), BF16), pltpu.VMEM((n_rows, 512), BF16),
               pltpu.VMEM((4, n_kchunks, LANES, ROW_CHUNK), BF16), pltpu.VMEM((4, n_keys, LANES), BF16),
               pltpu.VMEM((4, n_keys // WINDOW, LANES, WINDOW), BF16), pltpu.VMEM((4, n_keys, LANES), BF16),
               pltpu.VMEM((n_rows, D_MODEL), F32),
               pltpu.VMEM((2, 2, ROW_CHUNK, n_cols), F32),
               pltpu.VMEM((2, 2, ROW_CHUNK, n_cols), BF16),
               pltpu.VMEM((2, ROW_CHUNK, LANES), F32),
               pltpu.VMEM((2 * LANES, D_MODEL), BF16),
               pltpu.VMEM((D_MODEL, 2 * LANES), BF16)]
    if latent:
        scratch.append(pltpu.VMEM((1 + 2 * WINDOW // ROW_CHUNK, ROW_CHUNK, ROW_CHUNK), F32))
    return pl.pallas_call(
        functools.partial(_even_kernel, latent, n_rows, seq, alpha),
        grid=grid, in_specs=in_specs, out_specs=out_specs, out_shape=out_shape,
        scratch_shapes=scratch,
        compiler_params=pltpu.CompilerParams(dimension_semantics=("arbitrary",), vmem_limit_bytes=VMEM_LIMIT),
        name="even_layer_latent" if latent else "even_layer_context",
    )(*args)


def _odd_layer(x, mod, layer, w_in, w_out, lams, sub, ln_g, ln_b, latent, seq, n_rows, alpha, lam_init, extras=()):
    total = x.shape[0]
    grid = (total // n_rows,)
    row_blk = lambda width: pl.BlockSpec((n_rows, width), lambda i: (i, 0))
    single = pl.Buffered(1)
    in_specs = [row_blk(D_MODEL),
                pl.BlockSpec((1, 8, 3 * D_MODEL), lambda i: (layer, 0, 0)),
                _full(w_in.shape, pipeline_mode=single), _full(w_out.shape, pipeline_mode=single),
                _full((1, HEAD_DIM)), _full((1, HEAD_DIM)), _full((1, HEAD_DIM)), _full((1, HEAD_DIM)),
                _full((1, LANES)),
                pl.BlockSpec((1, 1, D_MODEL), lambda i: (layer, 0, 0)),
                pl.BlockSpec((1, 1, D_MODEL), lambda i: (layer, 0, 0))]
    args = [x, mod, w_in.astype(BF16), w_out.astype(BF16), *lams, sub, ln_g, ln_b]
    y_shape = jax.ShapeDtypeStruct((total, D_MODEL), F32)
    n_heads = D_MODEL // LANES
    n_blocks = n_rows // ROW_CHUNK
    if latent:
        cos, sin, cck, ccv = extras
        n_past = cck.shape[2]
        in_specs += [_full(cos.shape, pipeline_mode=single), _full(sin.shape, pipeline_mode=single)]
        in_specs += [pl.BlockSpec((1, 1, n_past, n_heads, LANES), lambda i: (i, layer // 2, 0, 0, 0))] * 2
        args += [cos, sin, cck, ccv]
        out_specs = row_blk(D_MODEL)
        out_shape = y_shape
        n_keys = seq + n_past
    else:
        kv_hbm = pl.BlockSpec(memory_space=pl.ANY)
        out_specs = [row_blk(D_MODEL), kv_hbm, kv_hbm]
        out_shape = [y_shape] + [jax.ShapeDtypeStruct((total // seq, 1, seq, n_heads, LANES), F32)] * 2
        n_keys = n_rows
    n_cols = n_keys if latent else ROW_CHUNK
    scratch = [pltpu.VMEM((n_rows, D_MODEL), BF16),
               pltpu.VMEM((n_rows, D_MODEL), BF16),
               pltpu.VMEM((2, n_keys, D_MODEL), BF16),
               pltpu.VMEM((n_keys, D_MODEL), BF16),
               pltpu.VMEM((n_rows, D_MODEL), F32),
               pltpu.VMEM((2, 2, ROW_CHUNK, n_cols), F32),
               pltpu.VMEM((2, 2, ROW_CHUNK, n_cols), BF16)]
    if not latent:
        scratch += [pltpu.VMEM((n_blocks, 2, ROW_CHUNK, D_MODEL), F32),
                    pltpu.SemaphoreType.DMA((n_blocks, 2))]
    return pl.pallas_call(
        functools.partial(_odd_kernel, latent, n_rows, seq, alpha, lam_init),
        grid=grid, in_specs=in_specs, out_specs=out_specs, out_shape=out_shape,
        scratch_shapes=scratch,
        compiler_params=pltpu.CompilerParams(dimension_semantics=("arbitrary",), vmem_limit_bytes=VMEM_LIMIT),
        name="odd_layer_latent" if latent else "odd_layer_context",
    )(*args)


def kernel(x_prompt, x_sample, cache_a_k, cache_a_v, cache_b_k, cache_b_v, cache_c_k, cache_c_v, c, c_ctx,
           w_mod, b_mod, ln_g, ln_b, w_in_even, w_out_even, q_norm_a, k_norm_a, sink_b, w_in_odd, w_out_odd,
           lambda_q1, lambda_k1, lambda_q2, lambda_k2, subln_c):
    depth = w_mod.shape[0]
    batch, seq, _ = x_prompt.shape
    dec_batch, dec_seq, _ = x_sample.shape
    n_past = cache_a_k.shape[2]
    alpha = (2 * depth) ** 0.25
    assert seq == ROW_CHUNK and n_past % ROW_CHUNK == 0 and dec_seq % ROW_CHUNK == 0

    mod = _modulation(c, c_ctx, w_mod, b_mod)
    ln_g3 = ln_g.reshape(depth, 1, D_MODEL)
    ln_b3 = ln_b.reshape(depth, 1, D_MODEL)
    cos, sin, cos_t, sin_t = _rope_tables(dec_seq)

    def run(x, latent, n_batch, s, rows_even, rows_odd):
        kv = {"a_k": [], "a_v": [], "b_k": [], "b_v": [], "c_k": [], "c_v": []}
        for l in range(depth):
            if l % 2 == 0:
                e = l // 2
                extras = ()
                if latent:
                    k_t = lambda t: t[:, e].transpose(0, 2, 3, 1).reshape(n_batch, LANES, n_past)
                    v_n = lambda t: t[:, e].reshape(n_batch, n_past, LANES)
                    extras = (cos, sin, cos_t, sin_t,
                              k_t(cache_a_k), v_n(cache_a_v), k_t(cache_b_k), v_n(cache_b_v))
                res = _even_layer(x, mod, l, w_in_even[e], w_out_even[e], q_norm_a[e], k_norm_a[e],
                                  sink_b[e], ln_g3, ln_b3, latent, s, rows_even, alpha, extras)
                if latent:
                    x = res
                else:
                    x = res[0]
                    for name, t in zip(("a_k", "a_v", "b_k", "b_v"), res[1:]):
                        kv[name].append(t.reshape(n_batch, 2, HEAD_DIM, s).transpose(0, 3, 1, 2))
            else:
                o = l // 2
                lam_init = 0.8 - 0.6 * math.exp(-0.3 * l)
                extras = (cos, sin, cache_c_k, cache_c_v) if latent else ()
                lams = [t[o][None, :] for t in (lambda_q1, lambda_k1, lambda_q2, lambda_k2)]
                res = _odd_layer(x, mod, l, w_in_odd[o], w_out_odd[o], lams,
                                 subln_c[o][None, :], ln_g3, ln_b3, latent, s, rows_odd, alpha, lam_init, extras)
                if latent:
                    x = res
                else:
                    x = res[0]
                    kv["c_k"].append(res[1][:, 0])
                    kv["c_v"].append(res[2][:, 0])
        return x, kv

    y_ctx, kv = run(x_prompt.reshape(batch * seq, D_MODEL), False, batch, seq, 1024, 512)
    y_lat, _ = run(x_sample.reshape(dec_batch * dec_seq, D_MODEL), True, dec_batch, dec_seq, dec_seq, dec_seq)

    stack = lambda name: jnp.stack(kv[name], axis=1)
    return (y_ctx.reshape(batch, seq, D_MODEL), y_lat.reshape(dec_batch, dec_seq, D_MODEL),
            stack("a_k"), stack("a_v"), stack("b_k"), stack("b_v"), stack("c_k"), stack("c_v"))
```

```python
import functools
import math

import jax
import jax.numpy as jnp
import numpy as np
from jax import lax
from jax.experimental import pallas as pl
from jax.experimental.pallas import tpu as pltpu

F32 = jnp.float32
BF16 = jnp.bfloat16

D_MODEL = 1024
HEAD_DIM = 64
GRID_W = 64
WINDOW = 128
ROPE_THETA = 10000.0
EPS = 1e-6
NEG_INF = -1e30
LOG2E = 1.4426950408889634
Q_SCALE = HEAD_DIM ** -0.5 * LOG2E
LANES = 128
ROW_CHUNK = 256
SOFTMAX_VREGS = 40
VMEM_LIMIT = 60000 * 1024


def _silu(x):
    return x / (1.0 + jnp.exp(-x))


def _dot(a, b):
    return jnp.dot(a, b, preferred_element_type=F32)


def _dot_nt(a, b):
    return lax.dot_general(a, b, (((1,), (1,)), ((), ())), preferred_element_type=F32)


def _lane_iota(rows):
    return lax.broadcasted_iota(jnp.int32, (rows, LANES), 1)


def _chunk_rows(i):
    if isinstance(i, int):
        return pl.ds(i * ROW_CHUNK, ROW_CHUNK)
    return pl.ds(pl.multiple_of(i * ROW_CHUNK, ROW_CHUNK), ROW_CHUNK)


def _softmax_rows(n_cols):
    rows = 8
    while rows * 2 * n_cols <= SOFTMAX_VREGS * 1024 and rows * 2 <= ROW_CHUNK:
        rows *= 2
    return rows


def _rope(a, cos, sin_signed):
    lane = _lane_iota(a.shape[0])
    fwd = pltpu.roll(a, LANES - 16, 1)
    bwd = pltpu.roll(a, 16, 1)
    partner = jnp.where((lane & 16) == 0, fwd, bwd)
    return a * cos + partner * sin_signed


def _rope_t(a, cos_t, sin_t):
    blocks = [a[16 * b:16 * (b + 1), :] for b in range(a.shape[0] // 16)]
    partner = jnp.concatenate([blocks[b ^ 1] for b in range(len(blocks))], axis=0)
    return a * cos_t + partner * sin_t


def _store_kt_variants(scr, chunk, kt):
    width = scr.shape[-1]
    per_block = kt.shape[1] // width
    zero = jnp.zeros((HEAD_DIM, kt.shape[1]), F32)
    for j in range(2):
        kj = kt[HEAD_DIM * j:HEAD_DIM * (j + 1), :]
        for par, full in enumerate((jnp.concatenate([kj, zero], axis=0), jnp.concatenate([zero, kj], axis=0))):
            full = full.astype(BF16)
            for c in range(per_block):
                scr[2 * j + par, chunk * per_block + c] = full[:, width * c:width * (c + 1)]


def _store_v_variants(scr, rows, a):
    lane = _lane_iota(a.shape[0])
    lo = lane < HEAD_DIM
    swapped = pltpu.roll(a, HEAD_DIM, 1)
    one = jnp.ones_like(a)
    scr[0, rows, :] = jnp.where(lo, a, one).astype(BF16)
    scr[1, rows, :] = jnp.where(lo, one, swapped).astype(BF16)
    scr[2, rows, :] = jnp.where(lo, swapped, one).astype(BF16)
    scr[3, rows, :] = jnp.where(lo, one, a).astype(BF16)


def _layer_norm_rows(z, g, b):
    mu = jnp.mean(z, axis=-1, keepdims=True)
    zc = z - mu
    var = jnp.mean(zc * zc, axis=-1, keepdims=True)
    return zc * lax.rsqrt(var + EPS) * g + b


def _modulate(x_ref, mod_ref, mod_row, h_scr, n_rows):
    shift = mod_ref[0, pl.ds(mod_row, 1), 0:D_MODEL]
    scale = mod_ref[0, pl.ds(mod_row, 1), D_MODEL:2 * D_MODEL]

    def body(i, carry):
        rows = _chunk_rows(i)
        h_scr[rows, :] = (x_ref[rows, :] * (1.0 + scale) + shift).astype(BF16)
        return carry

    lax.fori_loop(0, n_rows // ROW_CHUNK, body, 0)


def _out_proj_norm(x_ref, mod_ref, mod_row, attn_scr, w_out_ref, lng_ref, lnb_ref, y_ref, n_rows, alpha):
    gate = mod_ref[0, pl.ds(mod_row, 1), 2 * D_MODEL:3 * D_MODEL]
    g = lng_ref[0]
    b = lnb_ref[0]

    def body(i, carry):
        rows = _chunk_rows(i)
        out = _dot(attn_scr[rows, :], w_out_ref[...])
        z = alpha * x_ref[rows, :] + gate * out
        y_ref[rows, :] = _layer_norm_rows(z, g, b)
        return carry

    lax.fori_loop(0, n_rows // ROW_CHUNK, body, 0, unroll=True)


def _run_pipeline(n_items, stages):
    for u in range(n_items + len(stages) - 1):
        for k, stage in enumerate(stages):
            t = u - k
            if 0 <= t < n_items:
                stage(t, t % 2)


def _next_mod_partial(step, chunk, chunks_per_step, n_steps, sb_scr, wmod_ref, macc_scr):
    n_cond = sb_scr.shape[0]
    sublanes = 8
    rows_per_chunk = D_MODEL // (n_steps * chunks_per_step)
    w_base = chunk * rows_per_chunk
    s_base = step * (rows_per_chunk * chunks_per_step) + w_base
    col_blk = 8 * LANES
    for cb in range(wmod_ref.shape[2] // col_blk):
        cols = slice(col_blk * cb, col_blk * (cb + 1))
        accs = [macc_scr[r, :, cols] for r in range(n_cond)]
        for kb in range(rows_per_chunk // sublanes):
            w = wmod_ref[0, pl.ds(pl.multiple_of(w_base + sublanes * kb, sublanes), sublanes), cols]
            s_rows = pl.ds(pl.multiple_of(s_base + sublanes * kb, sublanes), sublanes)
            accs = [acc + w * jnp.tile(sb_scr[r, s_rows, :], (1, col_blk // LANES)) for r, acc in enumerate(accs)]
        for r in range(n_cond):
            macc_scr[r, :, cols] = accs[r]


def _even_kernel(latent, fold_mod, n_rows, seq, alpha, *refs):
    if fold_mod:
        refs = list(refs)
        cvb_ref, wmod_ref, bmod_ref = refs[10:13]
        nmod_ref = refs[18]
        sb_scr, macc_scr = refs[-2:]
        refs = refs[:10] + refs[13:18] + refs[19:-2]
    if latent:
        (x_ref, mod_ref, w_in_ref, w_out_ref, qn_ref, knt_ref, sink_ref, lng_ref, lnb_ref, pm_ref,
         cos_ref, sin_ref, cost_ref, sint_ref, cakt_ref, cav_ref, cbkt_ref, cbv_ref,
         y_ref,
         ha_scr, qa_scr, qb_scr, ka_scr, va_scr, kb_scr, vb_scr, g_scr, s_scr, p_scr, es_scr, wkt_scr, wv_scr,
         bias_scr) = refs
    else:
        (x_ref, mod_ref, w_in_ref, w_out_ref, qn_ref, knt_ref, sink_ref, lng_ref, lnb_ref, pm_ref,
         y_ref, nakt_ref, navt_ref, nbkt_ref, nbvt_ref,
         ha_scr, qa_scr, qb_scr, ka_scr, va_scr, kb_scr, vb_scr, g_scr, s_scr, p_scr, es_scr, wkt_scr,
         wv_scr) = refs

    step = pl.program_id(0)
    mod_row = step + 1 if latent else 0
    _modulate(x_ref, mod_ref, mod_row, ha_scr, n_rows)

    col_ka, col_va, col_kb, col_vb = 512, 640, 1792, 1920

    @pl.when(step == 0)
    def _():
        for r, c0 in enumerate((col_ka, col_kb)):
            wkt_scr[LANES * r:LANES * (r + 1), :] = w_in_ref[:, c0:c0 + LANES].T
        wv_scr[:, 0:LANES] = w_in_ref[:, col_va:col_va + LANES]
        wv_scr[:, LANES:2 * LANES] = w_in_ref[:, col_vb:col_vb + LANES]
        if fold_mod:
            sb_scr[...] = _silu(cvb_ref[...])
            macc_scr[...] = jnp.zeros(macc_scr.shape, F32)

    n_lat_chunks = seq // ROW_CHUNK
    if latent:
        n_past = cav_ref.shape[1]
        past_rows = pl.ds(seq, n_past)
        _store_kt_variants(ka_scr, n_lat_chunks, cakt_ref[0])
        _store_kt_variants(kb_scr, n_lat_chunks, cbkt_ref[0])
        _store_v_variants(va_scr, past_rows, cav_ref[0])
        _store_v_variants(vb_scr, past_rows, cbv_ref[0])

    pm = pm_ref[...]
    qn = qn_ref[...]
    knt = knt_ref[...]

    def proj(i, carry):
        rows = _chunk_rows(i)
        hh = ha_scr[rows, :]
        if latent:
            cos = cos_ref[rows, :]
            sin = sin_ref[rows, :]
            rot = lambda a: _rope(a, cos, sin)
            rot_t = lambda a: _rope_t(a, cost_ref[i], sint_ref[i])
        else:
            rot = rot_t = lambda a: a

        acc = _dot(hh, w_in_ref[:, 0:512])
        for j in range(4):
            a = acc[:, LANES * j:LANES * (j + 1)]
            ms = _dot((a * a).astype(BF16), pm)
            a = rot(a * lax.rsqrt(ms + EPS) * qn)
            qa_scr[rows, LANES * j:LANES * (j + 1)] = (a * Q_SCALE).astype(BF16)
        acc = _dot(hh, w_in_ref[:, 1280:1792])
        for j in range(4):
            a = rot(acc[:, LANES * j:LANES * (j + 1)])
            qb_scr[rows, LANES * j:LANES * (j + 1)] = (a * Q_SCALE).astype(BF16)
        g_scr[rows, 0:512] = _silu(_dot(hh, w_in_ref[:, 768:1280]))
        g_scr[rows, 512:1024] = _silu(_dot(hh, w_in_ref[:, 2048:2560]))
        v = _dot(hh, wv_scr[...])
        _store_v_variants(va_scr, rows, v[:, 0:LANES])
        _store_v_variants(vb_scr, rows, v[:, LANES:2 * LANES])

        kt = _dot_nt(wkt_scr[0:2 * LANES, :], hh)
        kat = kt[0:LANES, :]
        ms = _dot(pm, (kat * kat).astype(BF16))
        kat = kat * lax.rsqrt(ms + EPS) * knt
        kbt = kt[LANES:2 * LANES, :]
        if not latent:
            vt = v.T
            nakt_ref[i] = kat
            nbkt_ref[i] = kbt
            navt_ref[i] = vt[0:LANES, :]
            nbvt_ref[i] = vt[LANES:2 * LANES, :]
        _store_kt_variants(ka_scr, i, rot_t(kat))
        _store_kt_variants(kb_scr, i, rot_t(kbt))
        if fold_mod:
            _next_mod_partial(step, i, n_rows // ROW_CHUNK, fold_mod, sb_scr, wmod_ref, macc_scr)
        return carry

    lax.fori_loop(0, n_rows // ROW_CHUNK, proj, 0, unroll=2)

    sinks = [sink_ref[h] * LOG2E for h in range(8)]
    ck = ROW_CHUNK
    bk = kb_scr.shape[-1]
    win = ROW_CHUNK + 2 * WINDOW

    def attend(i, carry):
        rows = _chunk_rows(i)
        if latent:
            a_chunks = list(range(n_lat_chunks + n_past // ck))
            a_keys = pl.ds(0, seq + n_past)
            w0 = jnp.clip(i * (ck // bk) - WINDOW // bk, 0, (seq - win) // bk)
            win_rows = pl.ds(pl.multiple_of(w0 * bk, bk), win)
            dist = (lax.broadcasted_iota(jnp.int32, (ROW_CHUNK, ck), 1)
                    - lax.broadcasted_iota(jnp.int32, (ROW_CHUNK, ck), 0))
            for c in range(win // ck):
                off = w0 * bk + c * ck - i * ck
                bias_scr[c] = jnp.where(jnp.abs(dist + off) <= WINDOW, 0.0, NEG_INF).astype(F32)
            b_first = [w0 + c * (ck // bk) for c in range(win // ck)] + [seq // bk]
            n_biased = win // ck
            b_cols = win + n_past
        else:
            a_chunks = [i]
            a_keys = rows
            b_first = [i * (ck // bk)]
            n_biased = 0
            b_cols = ck
        a_cols = len(a_chunks) * ck

        def qk(t, slot):
            p, branch = divmod(t, 2)
            cols = slice(LANES * p, LANES * (p + 1))
            kvh = p // 2
            q = (qb_scr if branch else qa_scr)[rows, cols]
            for par in (0, 1):
                var = 2 * kvh + par
                if branch:
                    tiles = [jnp.concatenate([kb_scr[var, first + d] for d in range(ck // bk)], axis=1)
                             for first in b_first]
                else:
                    tiles = [ka_scr[var, chunk] for chunk in a_chunks]
                for c, kt in enumerate(tiles):
                    s = _dot(q, kt)
                    if branch and c < n_biased:
                        s = s + bias_scr[c]
                    s_scr[slot, par, :, c * ck:(c + 1) * ck] = s

        def softmax(t, slot):
            p, branch = divmod(t, 2)
            n_cols = b_cols if branch else a_cols
            rb = _softmax_rows(n_cols)
            for par in (0, 1):
                for r in range(ROW_CHUNK // rb):
                    sub = slice(r * rb, (r + 1) * rb)
                    s = s_scr[slot, par, sub, 0:n_cols]
                    m = jnp.max(s, axis=1, keepdims=True)
                    if branch:
                        sink = sinks[2 * p + par]
                        m = jnp.maximum(m, sink)
                        es_scr[slot, sub, HEAD_DIM * par:HEAD_DIM * (par + 1)] = jnp.broadcast_to(
                            jnp.exp2(sink - m), (rb, HEAD_DIM))
                    p_scr[slot, par, sub, 0:n_cols] = jnp.exp2((s - m).astype(BF16))

        def pv(t, slot):
            p, branch = divmod(t, 2)
            kvh = p // 2
            v_scr = vb_scr if branch else va_scr
            accs = []
            for par in (0, 1):
                var = 2 * kvh + par
                if latent and branch:
                    n_loc = win
                    accs.append(_dot(p_scr[slot, par, :, 0:n_loc], v_scr[var, win_rows, :])
                                + _dot(p_scr[slot, par, :, n_loc:b_cols], v_scr[var, past_rows, :]))
                else:
                    accs.append(_dot(p_scr[slot, par, :, 0:a_cols], v_scr[var, a_keys, :]))
            lo = _lane_iota(ROW_CHUNK) < HEAD_DIM
            denom = pltpu.roll(jnp.where(lo, accs[1], accs[0]), HEAD_DIM, 1)
            if branch:
                denom = denom + es_scr[slot]
            o = jnp.where(lo, accs[0], accs[1]) / denom
            ocols = slice(512 * branch + LANES * p, 512 * branch + LANES * (p + 1))
            ha_scr[rows, ocols] = (o * g_scr[rows, ocols]).astype(BF16)

        _run_pipeline(8, (qk, softmax, pv))
        return carry

    lax.fori_loop(0, n_rows // ROW_CHUNK, attend, 0)

    _out_proj_norm(x_ref, mod_ref, mod_row, ha_scr, w_out_ref, lng_ref, lnb_ref, y_ref, n_rows, alpha)

    if fold_mod:
        @pl.when(step == fold_mod - 1)
        def _():
            n_cond = sb_scr.shape[0]
            rows = [jnp.sum(macc_scr[r], axis=0, keepdims=True) + bmod_ref[0] for r in range(n_cond)]
            nmod_ref[0] = jnp.concatenate(rows + [jnp.zeros((8 - n_cond, rows[0].shape[1]), F32)], axis=0)


def _odd_kernel(latent, n_rows, seq, alpha, lam_init, *refs):
    if latent:
        (x_ref, mod_ref, w_in_ref, w_out_ref, lq1_ref, lk1_ref, lq2_ref, lk2_ref, sub_ref, lng_ref, lnb_ref,
         cos_ref, sin_ref, cck_ref, ccv_ref,
         y_ref,
         ha_scr, q_scr, k_scr, v_scr, g_scr, s_scr, p_scr) = refs
    else:
        (x_ref, mod_ref, w_in_ref, w_out_ref, lq1_ref, lk1_ref, lq2_ref, lk2_ref, sub_ref, lng_ref, lnb_ref,
         y_ref, nck_hbm, ncv_hbm,
         ha_scr, q_scr, k_scr, v_scr, g_scr, s_scr, p_scr, kv_stage, kv_sems) = refs

    step = pl.program_id(0)
    mod_row = step + 1 if latent else 0
    _modulate(x_ref, mod_ref, mod_row, ha_scr, n_rows)

    n_heads = D_MODEL // LANES
    n_blocks = n_rows // ROW_CHUNK
    lo = _lane_iota(ROW_CHUNK) < HEAD_DIM

    def kv_out_copies(blk):
        elem = step * n_blocks + blk
        return [pltpu.make_async_copy(kv_stage.at[blk, t, :, pl.ds(LANES * h, LANES)],
                                      out.at[elem, 0, :, h, :], kv_sems.at[blk, t])
                for t, out in enumerate((nck_hbm, ncv_hbm)) for h in range(n_heads)]

    def store_k(rows, h, a):
        cols = slice(LANES * h, LANES * (h + 1))
        zero = jnp.zeros_like(a)
        k_scr[0, rows, cols] = jnp.where(lo, a, zero).astype(BF16)
        k_scr[1, rows, cols] = jnp.where(lo, zero, a).astype(BF16)

    if latent:
        n_past = cck_ref.shape[2]
        past = pl.ds(seq, n_past)
        for h in range(n_heads):
            store_k(past, h, cck_ref[0, 0, :, h, :])
            v_scr[past, LANES * h:LANES * (h + 1)] = ccv_ref[0, 0, :, h, :].astype(BF16)

    def proj(i, carry):
        rows = _chunk_rows(i)
        hh = ha_scr[rows, :]
        if latent:
            cos = cos_ref[rows, :]
            sin = sin_ref[rows, :]
            rot = lambda a: _rope(a, cos, sin)
        else:
            rot = lambda a: a
        for half in range(2):
            acc = _dot(hh, w_in_ref[:, 512 * half:512 * (half + 1)])
            for j in range(4):
                a = rot(acc[:, LANES * j:LANES * (j + 1)])
                cols = slice(512 * half + LANES * j, 512 * half + LANES * (j + 1))
                q_scr[rows, cols] = (a * Q_SCALE).astype(BF16)
        for half in range(2):
            acc = _dot(hh, w_in_ref[:, 1024 + 512 * half:1024 + 512 * (half + 1)])
            if not latent:
                kv_stage[i, 0, :, 512 * half:512 * (half + 1)] = acc
            for j in range(4):
                store_k(rows, 4 * half + j, rot(acc[:, LANES * j:LANES * (j + 1)]))
        for half in range(2):
            acc = _dot(hh, w_in_ref[:, 2048 + 512 * half:2048 + 512 * (half + 1)])
            if not latent:
                kv_stage[i, 1, :, 512 * half:512 * (half + 1)] = acc
            v_scr[rows, 512 * half:512 * (half + 1)] = acc.astype(BF16)
        if not latent:
            for copy in kv_out_copies(i):
                copy.start()
        for half in range(2):
            acc = _dot(hh, w_in_ref[:, 3072 + 512 * half:3072 + 512 * (half + 1)])
            g_scr[rows, 512 * half:512 * (half + 1)] = _silu(acc)
        return carry

    if latent:
        lax.fori_loop(0, n_blocks, proj, 0, unroll=2)
    else:
        for blk in range(n_blocks):
            proj(blk, 0)

    lam = (jnp.exp(jnp.sum(lq1_ref[...] * lk1_ref[...], axis=1, keepdims=True))
           - jnp.exp(jnp.sum(lq2_ref[...] * lk2_ref[...], axis=1, keepdims=True)) + lam_init)
    sub = sub_ref[...] * (1.0 - lam_init)
    n_keys = seq + n_past if latent else ROW_CHUNK
    rb = _softmax_rows(n_keys)
    ones = jnp.ones((n_keys, LANES), BF16)

    def attend(i, carry):
        rows = _chunk_rows(i)
        keys = pl.ds(0, n_keys) if latent else rows

        def qk(h, slot):
            cols = slice(LANES * h, LANES * (h + 1))
            q = q_scr[rows, cols]
            for m in (0, 1):
                s_scr[slot, m] = _dot_nt(q, k_scr[m, keys, cols])

        def softmax(h, slot):
            for m in (0, 1):
                for r in range(ROW_CHUNK // rb):
                    sub_rows = slice(r * rb, (r + 1) * rb)
                    s = s_scr[slot, m, sub_rows, :]
                    top = jnp.max(s, axis=1, keepdims=True)
                    p_scr[slot, m, sub_rows, :] = jnp.exp2((s - top).astype(BF16))

        def pv(h, slot):
            cols = slice(LANES * h, LANES * (h + 1))
            v_ext = jnp.concatenate([v_scr[keys, cols], ones], axis=1)
            maps = []
            for m in (0, 1):
                acc = _dot(p_scr[slot, m], v_ext)
                maps.append(acc[:, 0:LANES] / acc[:, LANES:2 * LANES])
            o = maps[0] - lam * maps[1]
            ms = jnp.mean(o * o, axis=1, keepdims=True)
            o = o * lax.rsqrt(ms + EPS) * sub
            ha_scr[rows, cols] = (o * g_scr[rows, cols]).astype(BF16)

        _run_pipeline(n_heads, (qk, softmax, pv))
        return carry

    lax.fori_loop(0, n_rows // ROW_CHUNK, attend, 0)

    _out_proj_norm(x_ref, mod_ref, mod_row, ha_scr, w_out_ref, lng_ref, lnb_ref, y_ref, n_rows, alpha)

    if not latent:
        for blk in range(n_blocks):
            for copy in kv_out_copies(blk):
                copy.wait()


def _mod_kernel(n_cond, cvb_ref, w_ref, b_ref, o_ref, sb_scr):
    @pl.when((pl.program_id(0) == 0) & (pl.program_id(1) == 0))
    def _():
        sb_scr[...] = _silu(cvb_ref[...])

    n_out = w_ref.shape[2]
    sublanes = 8

    def body(kb, accs):
        rows = pl.ds(pl.multiple_of(kb * sublanes, sublanes), sublanes)
        w = w_ref[0, rows, :]
        return tuple(acc + w * jnp.tile(sb_scr[r, rows, :], (1, n_out // LANES)) for r, acc in enumerate(accs))

    zero = jnp.zeros((sublanes, n_out), F32)
    accs = lax.fori_loop(0, D_MODEL // sublanes, body, (zero,) * n_cond, unroll=8)
    out_rows = [jnp.sum(acc, axis=0, keepdims=True) + b_ref[0] for acc in accs]
    o_ref[0] = jnp.concatenate(out_rows + [jnp.zeros((8 - n_cond, n_out), F32)], axis=0)


def _full(shape, **kw):
    zeros = (0,) * len(shape)
    return pl.BlockSpec(shape, lambda i: zeros, **kw)


def _rope_tables(seq):
    t = np.arange(seq)
    n_freq = HEAD_DIM // 4
    freqs = ROPE_THETA ** (-np.arange(n_freq, dtype=np.float64) / n_freq)
    ang_row = (t // GRID_W)[:, None] * freqs
    ang_col = (t % GRID_W)[:, None] * freqs
    ang = np.concatenate([ang_row, ang_row, ang_col, ang_col], axis=1)
    sign = np.concatenate([-np.ones(n_freq), np.ones(n_freq)] * 2)[None, :]
    cos = np.tile(np.cos(ang), (1, 2)).astype(np.float32)
    sin = np.tile(np.sin(ang) * sign, (1, 2)).astype(np.float32)
    chunked_t = lambda a: a.reshape(seq // ROW_CHUNK, ROW_CHUNK, LANES).transpose(0, 2, 1)
    return jnp.asarray(cos), jnp.asarray(sin), jnp.asarray(chunked_t(cos)), jnp.asarray(chunked_t(sin))


def _head_mean_matrix():
    idx = np.arange(LANES) // HEAD_DIM
    return jnp.asarray((idx[:, None] == idx[None, :]).astype(np.float32) / HEAD_DIM, dtype=BF16)


def _modulation(cvb, w_mod, b_mod3, stride):
    n_cond = cvb.shape[0]
    n_layers = -(-w_mod.shape[0] // stride)
    n_blk = 3 * D_MODEL // 1024
    return pl.pallas_call(
        functools.partial(_mod_kernel, n_cond),
        grid=(n_layers, n_blk),
        in_specs=[pl.BlockSpec((n_cond, D_MODEL, LANES), lambda l, n: (0, 0, 0)),
                  pl.BlockSpec((1, D_MODEL, 1024), lambda l, n: (l * stride, 0, n)),
                  pl.BlockSpec((1, 1, 1024), lambda l, n: (l * stride, 0, n))],
        out_specs=pl.BlockSpec((1, 8, 1024), lambda l, n: (l, 0, n)),
        out_shape=jax.ShapeDtypeStruct((n_layers, 8, 3 * D_MODEL), F32),
        scratch_shapes=[pltpu.VMEM((n_cond, D_MODEL, LANES), F32)],
        compiler_params=pltpu.CompilerParams(dimension_semantics=("arbitrary", "arbitrary")),
        name="adaln_modulation",
    )(cvb, w_mod, b_mod3)


def _even_layer(x, mod, mod_layer, layer, w_in, w_out, q_norm, k_norm, sink, ln_g, ln_b, latent, seq, n_rows, alpha,
                extras=(), next_mod=None):
    total = x.shape[0]
    grid = (total // n_rows,)
    single = pl.Buffered(1)
    qn = jnp.tile(q_norm, 2)[None, :]
    knt = jnp.broadcast_to(jnp.tile(k_norm, 2)[:, None], (LANES, ROW_CHUNK))

    row_blk = lambda width: pl.BlockSpec((n_rows, width), lambda i: (i, 0))
    in_specs = [row_blk(D_MODEL),
                pl.BlockSpec((1, 8, 3 * D_MODEL), lambda i: (mod_layer, 0, 0)),
                _full(w_in.shape, pipeline_mode=single),
                _full((D_MODEL, D_MODEL), pipeline_mode=single),
                _full((1, LANES)), _full((LANES, ROW_CHUNK)),
                pl.BlockSpec(memory_space=pltpu.SMEM),
                pl.BlockSpec((1, 1, D_MODEL), lambda i: (layer, 0, 0)),
                pl.BlockSpec((1, 1, D_MODEL), lambda i: (layer, 0, 0)),
                _full((LANES, LANES))]
    args = [x, mod, w_in.astype(BF16), w_out.astype(BF16), qn, knt, sink, ln_g, ln_b, _head_mean_matrix()]
    y_shape = jax.ShapeDtypeStruct((total, D_MODEL), F32)
    n_blocks = n_rows // ROW_CHUNK
    if latent:
        cos, sin, cos_t, sin_t, cakt, cav, cbkt, cbv = extras
        n_past = cav.shape[1]
        in_specs += [_full(cos.shape, pipeline_mode=single), _full(sin.shape, pipeline_mode=single),
                     _full(cos_t.shape, pipeline_mode=single), _full(sin_t.shape, pipeline_mode=single)]
        in_specs += [pl.BlockSpec((1, LANES, n_past), lambda i: (i, 0, 0)),
                     pl.BlockSpec((1, n_past, LANES), lambda i: (i, 0, 0))] * 2
        args += [cos, sin, cos_t, sin_t, cakt, cav, cbkt, cbv]
        out_specs = row_blk(D_MODEL)
        out_shape = y_shape
        n_keys = seq + n_past
    else:
        kv_blk = pl.BlockSpec((n_blocks, LANES, ROW_CHUNK), lambda i: (i, 0, 0))
        out_specs = [row_blk(D_MODEL)] + [kv_blk] * 4
        out_shape = [y_shape] + [jax.ShapeDtypeStruct((total // seq, LANES, seq), F32)] * 4
        n_keys = n_rows
        if next_mod is not None:
            cvb, w_mod, b_mod3, next_layer = next_mod
            w_rows = D_MODEL // grid[0]
            assert w_rows % (8 * n_blocks) == 0
            in_specs += [_full(cvb.shape, pipeline_mode=single),
                         pl.BlockSpec((1, w_rows, 3 * D_MODEL), lambda i: (next_layer, i, 0)),
                         pl.BlockSpec((1, 1, 3 * D_MODEL), lambda i: (next_layer, 0, 0))]
            args += [cvb, w_mod, b_mod3]
            out_specs.append(pl.BlockSpec((1, 8, 3 * D_MODEL), lambda i: (0, 0, 0)))
            out_shape.append(jax.ShapeDtypeStruct((1, 8, 3 * D_MODEL), F32))
    n_kchunks = n_keys // ROW_CHUNK
    n_cols = n_keys if latent else ROW_CHUNK
    scratch = [pltpu.VMEM((n_rows, D_MODEL), BF16),
               pltpu.VMEM((n_rows, 512), BF16), pltpu.VMEM((n_rows, 512), BF16),
               pltpu.VMEM((4, n_kchunks, LANES, ROW_CHUNK), BF16), pltpu.VMEM((4, n_keys, LANES), BF16),
               pltpu.VMEM((4, n_keys // WINDOW, LANES, WINDOW), BF16), pltpu.VMEM((4, n_keys, LANES), BF16),
               pltpu.VMEM((n_rows, D_MODEL), F32),
               pltpu.VMEM((2, 2, ROW_CHUNK, n_cols), F32),
               pltpu.VMEM((2, 2, ROW_CHUNK, n_cols), BF16),
               pltpu.VMEM((2, ROW_CHUNK, LANES), F32),
               pltpu.VMEM((2 * LANES, D_MODEL), BF16),
               pltpu.VMEM((D_MODEL, 2 * LANES), BF16)]
    if latent:
        scratch.append(pltpu.VMEM((1 + 2 * WINDOW // ROW_CHUNK, ROW_CHUNK, ROW_CHUNK), F32))
    fold_mod = 0
    if next_mod is not None and not latent:
        fold_mod = grid[0]
        scratch += [pltpu.VMEM(next_mod[0].shape, F32), pltpu.VMEM((next_mod[0].shape[0], 8, 3 * D_MODEL), F32)]
    return pl.pallas_call(
        functools.partial(_even_kernel, latent, fold_mod, n_rows, seq, alpha),
        grid=grid, in_specs=in_specs, out_specs=out_specs, out_shape=out_shape,
        scratch_shapes=scratch,
        compiler_params=pltpu.CompilerParams(dimension_semantics=("arbitrary",), vmem_limit_bytes=VMEM_LIMIT),
        name="even_layer_latent" if latent else "even_layer_context",
    )(*args)


def _odd_layer(x, mod, mod_layer, layer, w_in, w_out, lams, sub, ln_g, ln_b, latent, seq, n_rows, alpha, lam_init,
               extras=()):
    total = x.shape[0]
    grid = (total // n_rows,)
    row_blk = lambda width: pl.BlockSpec((n_rows, width), lambda i: (i, 0))
    single = pl.Buffered(1)
    in_specs = [row_blk(D_MODEL),
                pl.BlockSpec((1, 8, 3 * D_MODEL), lambda i: (mod_layer, 0, 0)),
                _full(w_in.shape, pipeline_mode=single), _full(w_out.shape, pipeline_mode=single),
                _full((1, HEAD_DIM)), _full((1, HEAD_DIM)), _full((1, HEAD_DIM)), _full((1, HEAD_DIM)),
                _full((1, LANES)),
                pl.BlockSpec((1, 1, D_MODEL), lambda i: (layer, 0, 0)),
                pl.BlockSpec((1, 1, D_MODEL), lambda i: (layer, 0, 0))]
    args = [x, mod, w_in.astype(BF16), w_out.astype(BF16), *lams, sub, ln_g, ln_b]
    y_shape = jax.ShapeDtypeStruct((total, D_MODEL), F32)
    n_heads = D_MODEL // LANES
    n_blocks = n_rows // ROW_CHUNK
    if latent:
        cos, sin, cck, ccv = extras
        n_past = cck.shape[2]
        in_specs += [_full(cos.shape, pipeline_mode=single), _full(sin.shape, pipeline_mode=single)]
        in_specs += [pl.BlockSpec((1, 1, n_past, n_heads, LANES), lambda i: (i, layer // 2, 0, 0, 0))] * 2
        args += [cos, sin, cck, ccv]
        out_specs = row_blk(D_MODEL)
        out_shape = y_shape
        n_keys = seq + n_past
    else:
        kv_hbm = pl.BlockSpec(memory_space=pl.ANY)
        out_specs = [row_blk(D_MODEL), kv_hbm, kv_hbm]
        out_shape = [y_shape] + [jax.ShapeDtypeStruct((total // seq, 1, seq, n_heads, LANES), F32)] * 2
        n_keys = n_rows
    n_cols = n_keys if latent else ROW_CHUNK
    scratch = [pltpu.VMEM((n_rows, D_MODEL), BF16),
               pltpu.VMEM((n_rows, D_MODEL), BF16),
               pltpu.VMEM((2, n_keys, D_MODEL), BF16),
               pltpu.VMEM((n_keys, D_MODEL), BF16),
               pltpu.VMEM((n_rows, D_MODEL), F32),
               pltpu.VMEM((2, 2, ROW_CHUNK, n_cols), F32),
               pltpu.VMEM((2, 2, ROW_CHUNK, n_cols), BF16)]
    if not latent:
        scratch += [pltpu.VMEM((n_blocks, 2, ROW_CHUNK, D_MODEL), F32),
                    pltpu.SemaphoreType.DMA((n_blocks, 2))]
    return pl.pallas_call(
        functools.partial(_odd_kernel, latent, n_rows, seq, alpha, lam_init),
        grid=grid, in_specs=in_specs, out_specs=out_specs, out_shape=out_shape,
        scratch_shapes=scratch,
        compiler_params=pltpu.CompilerParams(dimension_semantics=("arbitrary",), vmem_limit_bytes=VMEM_LIMIT),
        name="odd_layer_latent" if latent else "odd_layer_context",
    )(*args)


def kernel(x_prompt, x_sample, cache_a_k, cache_a_v, cache_b_k, cache_b_v, cache_c_k, cache_c_v, c, c_ctx,
           w_mod, b_mod, ln_g, ln_b, w_in_even, w_out_even, q_norm_a, k_norm_a, sink_b, w_in_odd, w_out_odd,
           lambda_q1, lambda_k1, lambda_q2, lambda_k2, subln_c):
    depth = w_mod.shape[0]
    batch, seq, _ = x_prompt.shape
    dec_batch, dec_seq, _ = x_sample.shape
    n_past = cache_a_k.shape[2]
    alpha = (2 * depth) ** 0.25
    assert seq == ROW_CHUNK and n_past % ROW_CHUNK == 0 and dec_seq % ROW_CHUNK == 0

    cv = jnp.concatenate([c_ctx[None, :], c], axis=0)
    cvb = jnp.broadcast_to(cv[:, :, None], (cv.shape[0], D_MODEL, LANES))
    b_mod3 = b_mod.reshape(depth, 1, 3 * D_MODEL)
    mod_even = _modulation(cvb, w_mod, b_mod3, 2)
    mod_odd = {}
    ln_g3 = ln_g.reshape(depth, 1, D_MODEL)
    ln_b3 = ln_b.reshape(depth, 1, D_MODEL)
    cos, sin, cos_t, sin_t = _rope_tables(dec_seq)

    def run(x, latent, n_batch, s, rows_even, rows_odd):
        kv = {"a_k": [], "a_v": [], "b_k": [], "b_v": [], "c_k": [], "c_v": []}
        for l in range(depth):
            if l % 2 == 0:
                e = l // 2
                extras = ()
                if latent:
                    k_t = lambda t: t[:, e].transpose(0, 2, 3, 1).reshape(n_batch, LANES, n_past)
                    v_n = lambda t: t[:, e].reshape(n_batch, n_past, LANES)
                    extras = (cos, sin, cos_t, sin_t,
                              k_t(cache_a_k), v_n(cache_a_v), k_t(cache_b_k), v_n(cache_b_v))
                next_mod = (cvb, w_mod, b_mod3, l + 1) if (not latent and l + 1 < depth) else None
                res = _even_layer(x, mod_even, e, l, w_in_even[e], w_out_even[e], q_norm_a[e], k_norm_a[e],
                                  sink_b[e], ln_g3, ln_b3, latent, s, rows_even, alpha, extras, next_mod)
                if latent:
                    x = res
                else:
                    x = res[0]
                    if next_mod is not None:
                        mod_odd[l + 1] = res[5]
                    for name, t in zip(("a_k", "a_v", "b_k", "b_v"), res[1:5]):
                        kv[name].append(t.reshape(n_batch, 2, HEAD_DIM, s).transpose(0, 3, 1, 2))
            else:
                o = l // 2
                lam_init = 0.8 - 0.6 * math.exp(-0.3 * l)
                extras = (cos, sin, cache_c_k, cache_c_v) if latent else ()
                lams = [t[o][None, :] for t in (lambda_q1, lambda_k1, lambda_q2, lambda_k2)]
                res = _odd_layer(x, mod_odd[l], 0, l, w_in_odd[o], w_out_odd[o], lams,
                                 subln_c[o][None, :], ln_g3, ln_b3, latent, s, rows_odd, alpha, lam_init, extras)
                if latent:
                    x = res
                else:
                    x = res[0]
                    kv["c_k"].append(res[1][:, 0])
                    kv["c_v"].append(res[2][:, 0])
        return x, kv

    y_ctx, kv = run(x_prompt.reshape(batch * seq, D_MODEL), False, batch, seq, 1024, 512)
    y_lat, _ = run(x_sample.reshape(dec_batch * dec_seq, D_MODEL), True, dec_batch, dec_seq, dec_seq, dec_seq)

    stack = lambda name: jnp.stack(kv[name], axis=1)
    return (y_ctx.reshape(batch, seq, D_MODEL), y_lat.reshape(dec_batch, dec_seq, D_MODEL),
            stack("a_k"), stack("a_v"), stack("b_k"), stack("b_v"), stack("c_k"), stack("c_v"))
```

```python
import functools
import math

import jax
import jax.numpy as jnp
import numpy as np
from jax import lax
from jax.experimental import pallas as pl
from jax.experimental.pallas import tpu as pltpu

F32 = jnp.float32
BF16 = jnp.bfloat16

D_MODEL = 1024
HEAD_DIM = 64
GRID_W = 64
WINDOW = 128
ROPE_THETA = 10000.0
EPS = 1e-6
NEG_INF = -1e30
LOG2E = 1.4426950408889634
Q_SCALE = HEAD_DIM ** -0.5 * LOG2E
LANES = 128
ROW_CHUNK = 256
SOFTMAX_VREGS = 40
VMEM_LIMIT = 60000 * 1024


def _silu(x):
    return x / (1.0 + jnp.exp(-x))


def _dot(a, b):
    return jnp.dot(a, b, preferred_element_type=F32)


def _dot_nt(a, b):
    return lax.dot_general(a, b, (((1,), (1,)), ((), ())), preferred_element_type=F32)


def _lane_iota(rows):
    return lax.broadcasted_iota(jnp.int32, (rows, LANES), 1)


def _chunk_rows(i):
    if isinstance(i, int):
        return pl.ds(i * ROW_CHUNK, ROW_CHUNK)
    return pl.ds(pl.multiple_of(i * ROW_CHUNK, ROW_CHUNK), ROW_CHUNK)


def _softmax_rows(n_cols):
    rows = 8
    while rows * 2 * n_cols <= SOFTMAX_VREGS * 1024 and rows * 2 <= ROW_CHUNK:
        rows *= 2
    return rows


def _rope(a, cos, sin_signed):
    lane = _lane_iota(a.shape[0])
    fwd = pltpu.roll(a, LANES - 16, 1)
    bwd = pltpu.roll(a, 16, 1)
    partner = jnp.where((lane & 16) == 0, fwd, bwd)
    return a * cos + partner * sin_signed


def _rope_t(a, cos_t, sin_t):
    blocks = [a[16 * b:16 * (b + 1), :] for b in range(a.shape[0] // 16)]
    partner = jnp.concatenate([blocks[b ^ 1] for b in range(len(blocks))], axis=0)
    return a * cos_t + partner * sin_t


def _store_kt_variants(scr, chunk, kt):
    width = scr.shape[-1]
    per_block = kt.shape[1] // width
    zero = jnp.zeros((HEAD_DIM, kt.shape[1]), F32)
    for j in range(2):
        kj = kt[HEAD_DIM * j:HEAD_DIM * (j + 1), :]
        for par, full in enumerate((jnp.concatenate([kj, zero], axis=0), jnp.concatenate([zero, kj], axis=0))):
            full = full.astype(BF16)
            for c in range(per_block):
                scr[2 * j + par, chunk * per_block + c] = full[:, width * c:width * (c + 1)]


def _store_v_variants(scr, rows, a):
    lane = _lane_iota(a.shape[0])
    lo = lane < HEAD_DIM
    swapped = pltpu.roll(a, HEAD_DIM, 1)
    one = jnp.ones_like(a)
    scr[0, rows, :] = jnp.where(lo, a, one).astype(BF16)
    scr[1, rows, :] = jnp.where(lo, one, swapped).astype(BF16)
    scr[2, rows, :] = jnp.where(lo, swapped, one).astype(BF16)
    scr[3, rows, :] = jnp.where(lo, one, a).astype(BF16)


def _layer_norm_rows(z, g, b):
    mu = jnp.mean(z, axis=-1, keepdims=True)
    zc = z - mu
    var = jnp.mean(zc * zc, axis=-1, keepdims=True)
    return zc * lax.rsqrt(var + EPS) * g + b


def _modulate(x_ref, mod_ref, mod_row, h_scr, n_rows):
    shift = mod_ref[0, pl.ds(mod_row, 1), 0:D_MODEL]
    scale = mod_ref[0, pl.ds(mod_row, 1), D_MODEL:2 * D_MODEL]

    def body(i, carry):
        rows = _chunk_rows(i)
        h_scr[rows, :] = (x_ref[rows, :] * (1.0 + scale) + shift).astype(BF16)
        return carry

    lax.fori_loop(0, n_rows // ROW_CHUNK, body, 0)


def _out_proj_norm(x_ref, mod_ref, mod_row, attn_scr, w_out_ref, lng_ref, lnb_ref, y_ref, n_rows, alpha):
    gate = mod_ref[0, pl.ds(mod_row, 1), 2 * D_MODEL:3 * D_MODEL]
    g = lng_ref[0]
    b = lnb_ref[0]

    def body(i, carry):
        rows = _chunk_rows(i)
        out = _dot(attn_scr[rows, :], w_out_ref[...])
        z = alpha * x_ref[rows, :] + gate * out
        y_ref[rows, :] = _layer_norm_rows(z, g, b)
        return carry

    lax.fori_loop(0, n_rows // ROW_CHUNK, body, 0, unroll=True)


def _run_pipeline(n_items, stages):
    for u in range(n_items + len(stages) - 1):
        for k, stage in enumerate(stages):
            t = u - k
            if 0 <= t < n_items:
                stage(t, t % 2)


def _attend_blocks(block_stages, n_blocks, n_items, unrolled):
    assert n_items % 2 == 0
    if unrolled:
        per_block = [block_stages(i) for i in range(n_blocks)]
        stages = [lambda g, slot, k=k: per_block[g // n_items][k](g % n_items, slot) for k in range(3)]
        _run_pipeline(n_blocks * n_items, stages)
    else:
        def body(i, carry):
            _run_pipeline(n_items, block_stages(i))
            return carry

        lax.fori_loop(0, n_blocks, body, 0)


def _even_kernel(latent, n_rows, seq, alpha, *refs):
    if latent:
        (x_ref, mod_ref, w_in_ref, w_out_ref, qn_ref, knt_ref, sink_ref, lng_ref, lnb_ref, pm_ref,
         cos_ref, sin_ref, cost_ref, sint_ref, cakt_ref, cav_ref, cbkt_ref, cbv_ref,
         y_ref,
         ha_scr, qa_scr, qb_scr, ka_scr, va_scr, kb_scr, vb_scr, g_scr, s_scr, p_scr, es_scr, wkt_scr, wv_scr,
         bias_scr) = refs
    else:
        (x_ref, mod_ref, w_in_ref, w_out_ref, qn_ref, knt_ref, sink_ref, lng_ref, lnb_ref, pm_ref,
         y_ref, nakt_ref, navt_ref, nbkt_ref, nbvt_ref,
         ha_scr, qa_scr, qb_scr, ka_scr, va_scr, kb_scr, vb_scr, g_scr, s_scr, p_scr, es_scr, wkt_scr,
         wv_scr) = refs

    step = pl.program_id(0)
    mod_row = step + 1 if latent else 0
    _modulate(x_ref, mod_ref, mod_row, ha_scr, n_rows)

    col_ka, col_va, col_kb, col_vb = 512, 640, 1792, 1920

    @pl.when(step == 0)
    def _():
        for r, c0 in enumerate((col_ka, col_kb)):
            wkt_scr[LANES * r:LANES * (r + 1), :] = w_in_ref[:, c0:c0 + LANES].T
        wv_scr[:, 0:LANES] = w_in_ref[:, col_va:col_va + LANES]
        wv_scr[:, LANES:2 * LANES] = w_in_ref[:, col_vb:col_vb + LANES]

    n_lat_chunks = seq // ROW_CHUNK
    if latent:
        n_past = cav_ref.shape[1]
        past_rows = pl.ds(seq, n_past)
        _store_kt_variants(ka_scr, n_lat_chunks, cakt_ref[0])
        _store_kt_variants(kb_scr, n_lat_chunks, cbkt_ref[0])
        _store_v_variants(va_scr, past_rows, cav_ref[0])
        _store_v_variants(vb_scr, past_rows, cbv_ref[0])

    pm = pm_ref[...]
    qn = qn_ref[...]
    knt = knt_ref[...]

    def proj(i, carry):
        rows = _chunk_rows(i)
        hh = ha_scr[rows, :]
        if latent:
            cos = cos_ref[rows, :]
            sin = sin_ref[rows, :]
            rot = lambda a: _rope(a, cos, sin)
            rot_t = lambda a: _rope_t(a, cost_ref[i], sint_ref[i])
        else:
            rot = rot_t = lambda a: a

        acc = _dot(hh, w_in_ref[:, 0:512])
        for j in range(4):
            a = acc[:, LANES * j:LANES * (j + 1)]
            ms = _dot((a * a).astype(BF16), pm)
            a = rot(a * lax.rsqrt(ms + EPS) * qn)
            qa_scr[rows, LANES * j:LANES * (j + 1)] = (a * Q_SCALE).astype(BF16)
        acc = _dot(hh, w_in_ref[:, 1280:1792])
        for j in range(4):
            a = rot(acc[:, LANES * j:LANES * (j + 1)])
            qb_scr[rows, LANES * j:LANES * (j + 1)] = (a * Q_SCALE).astype(BF16)
        g_scr[rows, 0:512] = _silu(_dot(hh, w_in_ref[:, 768:1280]))
        g_scr[rows, 512:1024] = _silu(_dot(hh, w_in_ref[:, 2048:2560]))
        v = _dot(hh, wv_scr[...])
        _store_v_variants(va_scr, rows, v[:, 0:LANES])
        _store_v_variants(vb_scr, rows, v[:, LANES:2 * LANES])

        kt = _dot_nt(wkt_scr[0:2 * LANES, :], hh)
        kat = kt[0:LANES, :]
        ms = _dot(pm, (kat * kat).astype(BF16))
        kat = kat * lax.rsqrt(ms + EPS) * knt
        kbt = kt[LANES:2 * LANES, :]
        if not latent:
            vt = v.T
            nakt_ref[i] = kat
            nbkt_ref[i] = kbt
            navt_ref[i] = vt[0:LANES, :]
            nbvt_ref[i] = vt[LANES:2 * LANES, :]
        _store_kt_variants(ka_scr, i, rot_t(kat))
        _store_kt_variants(kb_scr, i, rot_t(kbt))
        return carry

    lax.fori_loop(0, n_rows // ROW_CHUNK, proj, 0, unroll=2)

    sinks = [sink_ref[h] * LOG2E for h in range(8)]
    ck = ROW_CHUNK
    bk = kb_scr.shape[-1]
    win = ROW_CHUNK + 2 * WINDOW
    n_items = 8

    def block_stages(i):
        rows = _chunk_rows(i)
        if latent:
            a_chunks = list(range(n_lat_chunks + n_past // ck))
            a_keys = pl.ds(0, seq + n_past)
            w0 = jnp.clip(i * (ck // bk) - WINDOW // bk, 0, (seq - win) // bk)
            win_rows = pl.ds(pl.multiple_of(w0 * bk, bk), win)
            dist = (lax.broadcasted_iota(jnp.int32, (ROW_CHUNK, ck), 1)
                    - lax.broadcasted_iota(jnp.int32, (ROW_CHUNK, ck), 0))
            for c in range(win // ck):
                off = w0 * bk + c * ck - i * ck
                bias_scr[c] = jnp.where(jnp.abs(dist + off) <= WINDOW, 0.0, NEG_INF).astype(F32)
            b_first = [w0 + c * (ck // bk) for c in range(win // ck)] + [seq // bk]
            n_biased = win // ck
            b_cols = win + n_past
        else:
            a_chunks = [i]
            a_keys = rows
            b_first = [i * (ck // bk)]
            n_biased = 0
            b_cols = ck
        a_cols = len(a_chunks) * ck

        def qk(t, slot):
            p, branch = divmod(t, 2)
            cols = slice(LANES * p, LANES * (p + 1))
            kvh = p // 2
            q = (qb_scr if branch else qa_scr)[rows, cols]
            for par in (0, 1):
                var = 2 * kvh + par
                if branch:
                    tiles = [jnp.concatenate([kb_scr[var, first + d] for d in range(ck // bk)], axis=1)
                             for first in b_first]
                else:
                    tiles = [ka_scr[var, chunk] for chunk in a_chunks]
                for c, kt in enumerate(tiles):
                    s = _dot(q, kt)
                    if branch and c < n_biased:
                        s = s + bias_scr[c]
                    s_scr[slot, par, :, c * ck:(c + 1) * ck] = s

        def softmax(t, slot):
            p, branch = divmod(t, 2)
            n_cols = b_cols if branch else a_cols
            rb = _softmax_rows(n_cols)
            for par in (0, 1):
                for r in range(ROW_CHUNK // rb):
                    sub = slice(r * rb, (r + 1) * rb)
                    s = s_scr[slot, par, sub, 0:n_cols]
                    m = jnp.max(s, axis=1, keepdims=True)
                    if branch:
                        sink = sinks[2 * p + par]
                        m = jnp.maximum(m, sink)
                        es_scr[slot, sub, HEAD_DIM * par:HEAD_DIM * (par + 1)] = jnp.broadcast_to(
                            jnp.exp2(sink - m), (rb, HEAD_DIM))
                    p_scr[slot, par, sub, 0:n_cols] = jnp.exp2((s - m).astype(BF16))

        def pv(t, slot):
            p, branch = divmod(t, 2)
            kvh = p // 2
            v_scr = vb_scr if branch else va_scr
            accs = []
            for par in (0, 1):
                var = 2 * kvh + par
                if latent and branch:
                    n_loc = win
                    accs.append(_dot(p_scr[slot, par, :, 0:n_loc], v_scr[var, win_rows, :])
                                + _dot(p_scr[slot, par, :, n_loc:b_cols], v_scr[var, past_rows, :]))
                else:
                    accs.append(_dot(p_scr[slot, par, :, 0:a_cols], v_scr[var, a_keys, :]))
            lo = _lane_iota(ROW_CHUNK) < HEAD_DIM
            denom = pltpu.roll(jnp.where(lo, accs[1], accs[0]), HEAD_DIM, 1)
            if branch:
                denom = denom + es_scr[slot]
            o = jnp.where(lo, accs[0], accs[1]) / denom
            ocols = slice(512 * branch + LANES * p, 512 * branch + LANES * (p + 1))
            ha_scr[rows, ocols] = (o * g_scr[rows, ocols]).astype(BF16)

        return qk, softmax, pv

    _attend_blocks(block_stages, n_rows // ROW_CHUNK, n_items, unrolled=not latent)

    _out_proj_norm(x_ref, mod_ref, mod_row, ha_scr, w_out_ref, lng_ref, lnb_ref, y_ref, n_rows, alpha)


def _odd_kernel(latent, n_rows, seq, alpha, lam_init, *refs):
    if latent:
        (x_ref, mod_ref, w_in_ref, w_out_ref, lq1_ref, lk1_ref, lq2_ref, lk2_ref, sub_ref, lng_ref, lnb_ref,
         cos_ref, sin_ref, cck_ref, ccv_ref,
         y_ref,
         ha_scr, q_scr, k_scr, v_scr, g_scr, s_scr, p_scr) = refs
    else:
        (x_ref, mod_ref, w_in_ref, w_out_ref, lq1_ref, lk1_ref, lq2_ref, lk2_ref, sub_ref, lng_ref, lnb_ref,
         y_ref, nck_hbm, ncv_hbm,
         ha_scr, q_scr, k_scr, v_scr, g_scr, s_scr, p_scr, kv_stage, kv_sems) = refs

    step = pl.program_id(0)
    mod_row = step + 1 if latent else 0
    _modulate(x_ref, mod_ref, mod_row, ha_scr, n_rows)

    n_heads = D_MODEL // LANES
    n_blocks = n_rows // ROW_CHUNK
    lo = _lane_iota(ROW_CHUNK) < HEAD_DIM

    def kv_out_copies(blk):
        elem = step * n_blocks + blk
        return [pltpu.make_async_copy(kv_stage.at[blk, t, :, pl.ds(LANES * h, LANES)],
                                      out.at[elem, 0, :, h, :], kv_sems.at[blk, t])
                for t, out in enumerate((nck_hbm, ncv_hbm)) for h in range(n_heads)]

    def store_k(rows, h, a):
        cols = slice(LANES * h, LANES * (h + 1))
        zero = jnp.zeros_like(a)
        k_scr[0, rows, cols] = jnp.where(lo, a, zero).astype(BF16)
        k_scr[1, rows, cols] = jnp.where(lo, zero, a).astype(BF16)

    if latent:
        n_past = cck_ref.shape[2]
        past = pl.ds(seq, n_past)
        for h in range(n_heads):
            store_k(past, h, cck_ref[0, 0, :, h, :])
            v_scr[past, LANES * h:LANES * (h + 1)] = ccv_ref[0, 0, :, h, :].astype(BF16)

    def proj(i, carry):
        rows = _chunk_rows(i)
        hh = ha_scr[rows, :]
        if latent:
            cos = cos_ref[rows, :]
            sin = sin_ref[rows, :]
            rot = lambda a: _rope(a, cos, sin)
        else:
            rot = lambda a: a
        for half in range(2):
            acc = _dot(hh, w_in_ref[:, 512 * half:512 * (half + 1)])
            for j in range(4):
                a = rot(acc[:, LANES * j:LANES * (j + 1)])
                cols = slice(512 * half + LANES * j, 512 * half + LANES * (j + 1))
                q_scr[rows, cols] = (a * Q_SCALE).astype(BF16)
        for half in range(2):
            acc = _dot(hh, w_in_ref[:, 1024 + 512 * half:1024 + 512 * (half + 1)])
            if not latent:
                kv_stage[i, 0, :, 512 * half:512 * (half + 1)] = acc
            for j in range(4):
                store_k(rows, 4 * half + j, rot(acc[:, LANES * j:LANES * (j + 1)]))
        for half in range(2):
            acc = _dot(hh, w_in_ref[:, 2048 + 512 * half:2048 + 512 * (half + 1)])
            if not latent:
                kv_stage[i, 1, :, 512 * half:512 * (half + 1)] = acc
            v_scr[rows, 512 * half:512 * (half + 1)] = acc.astype(BF16)
        if not latent:
            for copy in kv_out_copies(i):
                copy.start()
        for half in range(2):
            acc = _dot(hh, w_in_ref[:, 3072 + 512 * half:3072 + 512 * (half + 1)])
            g_scr[rows, 512 * half:512 * (half + 1)] = _silu(acc)
        return carry

    if latent:
        lax.fori_loop(0, n_blocks, proj, 0, unroll=2)
    else:
        for blk in range(n_blocks):
            proj(blk, 0)

    lam = (jnp.exp(jnp.sum(lq1_ref[...] * lk1_ref[...], axis=1, keepdims=True))
           - jnp.exp(jnp.sum(lq2_ref[...] * lk2_ref[...], axis=1, keepdims=True)) + lam_init)
    sub = sub_ref[...] * (1.0 - lam_init)
    n_keys = seq + n_past if latent else ROW_CHUNK
    rb = _softmax_rows(n_keys)
    ones = jnp.ones((n_keys, LANES), BF16)

    def block_stages(i):
        rows = _chunk_rows(i)
        keys = pl.ds(0, n_keys) if latent else rows

        def qk(h, slot):
            cols = slice(LANES * h, LANES * (h + 1))
            q = q_scr[rows, cols]
            for m in (0, 1):
                s_scr[slot, m] = _dot_nt(q, k_scr[m, keys, cols])

        def softmax(h, slot):
            for m in (0, 1):
                for r in range(ROW_CHUNK // rb):
                    sub_rows = slice(r * rb, (r + 1) * rb)
                    s = s_scr[slot, m, sub_rows, :]
                    top = jnp.max(s, axis=1, keepdims=True)
                    p_scr[slot, m, sub_rows, :] = jnp.exp2((s - top).astype(BF16))

        def pv(h, slot):
            cols = slice(LANES * h, LANES * (h + 1))
            v_ext = jnp.concatenate([v_scr[keys, cols], ones], axis=1)
            maps = []
            for m in (0, 1):
                acc = _dot(p_scr[slot, m], v_ext)
                maps.append(acc[:, 0:LANES] / acc[:, LANES:2 * LANES])
            o = maps[0] - lam * maps[1]
            ms = jnp.mean(o * o, axis=1, keepdims=True)
            o = o * lax.rsqrt(ms + EPS) * sub
            ha_scr[rows, cols] = (o * g_scr[rows, cols]).astype(BF16)

        return qk, softmax, pv

    _attend_blocks(block_stages, n_blocks, n_heads, unrolled=not latent)

    _out_proj_norm(x_ref, mod_ref, mod_row, ha_scr, w_out_ref, lng_ref, lnb_ref, y_ref, n_rows, alpha)

    if not latent:
        for blk in range(n_blocks):
            for copy in kv_out_copies(blk):
                copy.wait()


def _mod_kernel(n_cond, cvb_ref, w_ref, b_ref, o_ref, sb_scr):
    @pl.when((pl.program_id(0) == 0) & (pl.program_id(1) == 0))
    def _():
        sb_scr[...] = _silu(cvb_ref[...])

    n_out = w_ref.shape[2]
    sublanes = 8

    def body(kb, accs):
        rows = pl.ds(pl.multiple_of(kb * sublanes, sublanes), sublanes)
        w = w_ref[0, rows, :]
        return tuple(acc + w * jnp.tile(sb_scr[r, rows, :], (1, n_out // LANES)) for r, acc in enumerate(accs))

    zero = jnp.zeros((sublanes, n_out), F32)
    accs = lax.fori_loop(0, D_MODEL // sublanes, body, (zero,) * n_cond, unroll=8)
    out_rows = [jnp.sum(acc, axis=0, keepdims=True) + b_ref[0] for acc in accs]
    o_ref[0] = jnp.concatenate(out_rows + [jnp.zeros((8 - n_cond, n_out), F32)], axis=0)


def _full(shape, **kw):
    zeros = (0,) * len(shape)
    return pl.BlockSpec(shape, lambda i: zeros, **kw)


def _rope_tables(seq):
    t = np.arange(seq)
    n_freq = HEAD_DIM // 4
    freqs = ROPE_THETA ** (-np.arange(n_freq, dtype=np.float64) / n_freq)
    ang_row = (t // GRID_W)[:, None] * freqs
    ang_col = (t % GRID_W)[:, None] * freqs
    ang = np.concatenate([ang_row, ang_row, ang_col, ang_col], axis=1)
    sign = np.concatenate([-np.ones(n_freq), np.ones(n_freq)] * 2)[None, :]
    cos = np.tile(np.cos(ang), (1, 2)).astype(np.float32)
    sin = np.tile(np.sin(ang) * sign, (1, 2)).astype(np.float32)
    chunked_t = lambda a: a.reshape(seq // ROW_CHUNK, ROW_CHUNK, LANES).transpose(0, 2, 1)
    return jnp.asarray(cos), jnp.asarray(sin), jnp.asarray(chunked_t(cos)), jnp.asarray(chunked_t(sin))


def _head_mean_matrix():
    idx = np.arange(LANES) // HEAD_DIM
    return jnp.asarray((idx[:, None] == idx[None, :]).astype(np.float32) / HEAD_DIM, dtype=BF16)


def _modulation(c, c_ctx, w_mod, b_mod):
    depth = w_mod.shape[0]
    cv = jnp.concatenate([c_ctx[None, :], c], axis=0)
    n_cond = cv.shape[0]
    cvb = jnp.broadcast_to(cv[:, :, None], (n_cond, D_MODEL, LANES))
    n_blk = 3 * D_MODEL // 1024
    return pl.pallas_call(
        functools.partial(_mod_kernel, n_cond),
        grid=(depth, n_blk),
        in_specs=[pl.BlockSpec((n_cond, D_MODEL, LANES), lambda l, n: (0, 0, 0)),
                  pl.BlockSpec((1, D_MODEL, 1024), lambda l, n: (l, 0, n)),
                  pl.BlockSpec((1, 1, 1024), lambda l, n: (l, 0, n))],
        out_specs=pl.BlockSpec((1, 8, 1024), lambda l, n: (l, 0, n)),
        out_shape=jax.ShapeDtypeStruct((depth, 8, 3 * D_MODEL), F32),
        scratch_shapes=[pltpu.VMEM((n_cond, D_MODEL, LANES), F32)],
        compiler_params=pltpu.CompilerParams(dimension_semantics=("arbitrary", "arbitrary")),
        name="adaln_modulation",
    )(cvb, w_mod, b_mod.reshape(depth, 1, 3 * D_MODEL))


def _even_layer(x, mod, layer, w_in, w_out, q_norm, k_norm, sink, ln_g, ln_b, latent, seq, n_rows, alpha, extras=()):
    total = x.shape[0]
    grid = (total // n_rows,)
    single = pl.Buffered(1)
    qn = jnp.tile(q_norm, 2)[None, :]
    knt = jnp.broadcast_to(jnp.tile(k_norm, 2)[:, None], (LANES, ROW_CHUNK))

    row_blk = lambda width: pl.BlockSpec((n_rows, width), lambda i: (i, 0))
    in_specs = [row_blk(D_MODEL),
                pl.BlockSpec((1, 8, 3 * D_MODEL), lambda i: (layer, 0, 0)),
                _full(w_in.shape, pipeline_mode=single),
                _full((D_MODEL, D_MODEL), pipeline_mode=single),
                _full((1, LANES)), _full((LANES, ROW_CHUNK)),
                pl.BlockSpec(memory_space=pltpu.SMEM),
                pl.BlockSpec((1, 1, D_MODEL), lambda i: (layer, 0, 0)),
                pl.BlockSpec((1, 1, D_MODEL), lambda i: (layer, 0, 0)),
                _full((LANES, LANES))]
    args = [x, mod, w_in.astype(BF16), w_out.astype(BF16), qn, knt, sink, ln_g, ln_b, _head_mean_matrix()]
    y_shape = jax.ShapeDtypeStruct((total, D_MODEL), F32)
    n_blocks = n_rows // ROW_CHUNK
    if latent:
        cos, sin, cos_t, sin_t, cakt, cav, cbkt, cbv = extras
        n_past = cav.shape[1]
        in_specs += [_full(cos.shape, pipeline_mode=single), _full(sin.shape, pipeline_mode=single),
                     _full(cos_t.shape, pipeline_mode=single), _full(sin_t.shape, pipeline_mode=single)]
        in_specs += [pl.BlockSpec((1, LANES, n_past), lambda i: (i, 0, 0)),
                     pl.BlockSpec((1, n_past, LANES), lambda i: (i, 0, 0))] * 2
        args += [cos, sin, cos_t, sin_t, cakt, cav, cbkt, cbv]
        out_specs = row_blk(D_MODEL)
        out_shape = y_shape
        n_keys = seq + n_past
    else:
        kv_blk = pl.BlockSpec((n_blocks, LANES, ROW_CHUNK), lambda i: (i, 0, 0))
        out_specs = [row_blk(D_MODEL)] + [kv_blk] * 4
        out_shape = [y_shape] + [jax.ShapeDtypeStruct((total // seq, LANES, seq), F32)] * 4
        n_keys = n_rows
    n_kchunks = n_keys // ROW_CHUNK
    n_cols = n_keys if latent else ROW_CHUNK
    scratch = [pltpu.VMEM((n_rows, D_MODEL), BF16),
               pltpu.VMEM((n_rows, 512), BF16), pltpu.VMEM((n_rows, 512), BF16),
               pltpu.VMEM((4, n_kchunks, LANES, ROW_CHUNK), BF16), pltpu.VMEM((4, n_keys, LANES), BF16),
               pltpu.VMEM((4, n_keys // WINDOW, LANES, WINDOW), BF16), pltpu.VMEM((4, n_keys, LANES), BF16),
               pltpu.VMEM((n_rows, D_MODEL), F32),
               pltpu.VMEM((2, 2, ROW_CHUNK, n_cols), F32),
               pltpu.VMEM((2, 2, ROW_CHUNK, n_cols), BF16),
               pltpu.VMEM((2, ROW_CHUNK, LANES), F32),
               pltpu.VMEM((2 * LANES, D_MODEL), BF16),
               pltpu.VMEM((D_MODEL, 2 * LANES), BF16)]
    if latent:
        scratch.append(pltpu.VMEM((1 + 2 * WINDOW // ROW_CHUNK, ROW_CHUNK, ROW_CHUNK), F32))
    return pl.pallas_call(
        functools.partial(_even_kernel, latent, n_rows, seq, alpha),
        grid=grid, in_specs=in_specs, out_specs=out_specs, out_shape=out_shape,
        scratch_shapes=scratch,
        compiler_params=pltpu.CompilerParams(dimension_semantics=("arbitrary",), vmem_limit_bytes=VMEM_LIMIT),
        name="even_layer_latent" if latent else "even_layer_context",
    )(*args)


def _odd_layer(x, mod, layer, w_in, w_out, lams, sub, ln_g, ln_b, latent, seq, n_rows, alpha, lam_init, extras=()):
    total = x.shape[0]
    grid = (total // n_rows,)
    row_blk = lambda width: pl.BlockSpec((n_rows, width), lambda i: (i, 0))
    single = pl.Buffered(1)
    in_specs = [row_blk(D_MODEL),
                pl.BlockSpec((1, 8, 3 * D_MODEL), lambda i: (layer, 0, 0)),
                _full(w_in.shape, pipeline_mode=single), _full(w_out.shape, pipeline_mode=single),
                _full((1, HEAD_DIM)), _full((1, HEAD_DIM)), _full((1, HEAD_DIM)), _full((1, HEAD_DIM)),
                _full((1, LANES)),
                pl.BlockSpec((1, 1, D_MODEL), lambda i: (layer, 0, 0)),
                pl.BlockSpec((1, 1, D_MODEL), lambda i: (layer, 0, 0))]
    args = [x, mod, w_in.astype(BF16), w_out.astype(BF16), *lams, sub, ln_g, ln_b]
    y_shape = jax.ShapeDtypeStruct((total, D_MODEL), F32)
    n_heads = D_MODEL // LANES
    n_blocks = n_rows // ROW_CHUNK
    if latent:
        cos, sin, cck, ccv = extras
        n_past = cck.shape[2]
        in_specs += [_full(cos.shape, pipeline_mode=single), _full(sin.shape, pipeline_mode=single)]
        in_specs += [pl.BlockSpec((1, 1, n_past, n_heads, LANES), lambda i: (i, layer // 2, 0, 0, 0))] * 2
        args += [cos, sin, cck, ccv]
        out_specs = row_blk(D_MODEL)
        out_shape = y_shape
        n_keys = seq + n_past
    else:
        kv_hbm = pl.BlockSpec(memory_space=pl.ANY)
        out_specs = [row_blk(D_MODEL), kv_hbm, kv_hbm]
        out_shape = [y_shape] + [jax.ShapeDtypeStruct((total // seq, 1, seq, n_heads, LANES), F32)] * 2
        n_keys = n_rows
    n_cols = n_keys if latent else ROW_CHUNK
    scratch = [pltpu.VMEM((n_rows, D_MODEL), BF16),
               pltpu.VMEM((n_rows, D_MODEL), BF16),
               pltpu.VMEM((2, n_keys, D_MODEL), BF16),
               pltpu.VMEM((n_keys, D_MODEL), BF16),
               pltpu.VMEM((n_rows, D_MODEL), F32),
               pltpu.VMEM((2, 2, ROW_CHUNK, n_cols), F32),
               pltpu.VMEM((2, 2, ROW_CHUNK, n_cols), BF16)]
    if not latent:
        scratch += [pltpu.VMEM((n_blocks, 2, ROW_CHUNK, D_MODEL), F32),
                    pltpu.SemaphoreType.DMA((n_blocks, 2))]
    return pl.pallas_call(
        functools.partial(_odd_kernel, latent, n_rows, seq, alpha, lam_init),
        grid=grid, in_specs=in_specs, out_specs=out_specs, out_shape=out_shape,
        scratch_shapes=scratch,
        compiler_params=pltpu.CompilerParams(dimension_semantics=("arbitrary",), vmem_limit_bytes=VMEM_LIMIT),
        name="odd_layer_latent" if latent else "odd_layer_context",
    )(*args)


def kernel(x_prompt, x_sample, cache_a_k, cache_a_v, cache_b_k, cache_b_v, cache_c_k, cache_c_v, c, c_ctx,
           w_mod, b_mod, ln_g, ln_b, w_in_even, w_out_even, q_norm_a, k_norm_a, sink_b, w_in_odd, w_out_odd,
           lambda_q1, lambda_k1, lambda_q2, lambda_k2, subln_c):
    depth = w_mod.shape[0]
    batch, seq, _ = x_prompt.shape
    dec_batch, dec_seq, _ = x_sample.shape
    n_past = cache_a_k.shape[2]
    alpha = (2 * depth) ** 0.25
    assert seq == ROW_CHUNK and n_past % ROW_CHUNK == 0 and dec_seq % ROW_CHUNK == 0

    mod = _modulation(c, c_ctx, w_mod, b_mod)
    ln_g3 = ln_g.reshape(depth, 1, D_MODEL)
    ln_b3 = ln_b.reshape(depth, 1, D_MODEL)
    cos, sin, cos_t, sin_t = _rope_tables(dec_seq)

    def run(x, latent, n_batch, s, rows_even, rows_odd):
        kv = {"a_k": [], "a_v": [], "b_k": [], "b_v": [], "c_k": [], "c_v": []}
        for l in range(depth):
            if l % 2 == 0:
                e = l // 2
                extras = ()
                if latent:
                    k_t = lambda t: t[:, e].transpose(0, 2, 3, 1).reshape(n_batch, LANES, n_past)
                    v_n = lambda t: t[:, e].reshape(n_batch, n_past, LANES)
                    extras = (cos, sin, cos_t, sin_t,
                              k_t(cache_a_k), v_n(cache_a_v), k_t(cache_b_k), v_n(cache_b_v))
                res = _even_layer(x, mod, l, w_in_even[e], w_out_even[e], q_norm_a[e], k_norm_a[e],
                                  sink_b[e], ln_g3, ln_b3, latent, s, rows_even, alpha, extras)
                if latent:
                    x = res
                else:
                    x = res[0]
                    for name, t in zip(("a_k", "a_v", "b_k", "b_v"), res[1:]):
                        kv[name].append(t.reshape(n_batch, 2, HEAD_DIM, s).transpose(0, 3, 1, 2))
            else:
                o = l // 2
                lam_init = 0.8 - 0.6 * math.exp(-0.3 * l)
                extras = (cos, sin, cache_c_k, cache_c_v) if latent else ()
                lams = [t[o][None, :] for t in (lambda_q1, lambda_k1, lambda_q2, lambda_k2)]
                res = _odd_layer(x, mod, l, w_in_odd[o], w_out_odd[o], lams,
                                 subln_c[o][None, :], ln_g3, ln_b3, latent, s, rows_odd, alpha, lam_init, extras)
                if latent:
                    x = res
                else:
                    x = res[0]
                    kv["c_k"].append(res[1][:, 0])
                    kv["c_v"].append(res[2][:, 0])
        return x, kv

    y_ctx, kv = run(x_prompt.reshape(batch * seq, D_MODEL), False, batch, seq, 1024, 512)
    y_lat, _ = run(x_sample.reshape(dec_batch * dec_seq, D_MODEL), True, dec_batch, dec_seq, dec_seq, dec_seq)

    stack = lambda name: jnp.stack(kv[name], axis=1)
    return (y_ctx.reshape(batch, seq, D_MODEL), y_lat.reshape(dec_batch, dec_seq, D_MODEL),
            stack("a_k"), stack("a_v"), stack("b_k"), stack("b_v"), stack("c_k"), stack("c_v"))
```

```python
import functools
import math

import jax
import jax.numpy as jnp
import numpy as np
from jax import lax
from jax.experimental import pallas as pl
from jax.experimental.pallas import tpu as pltpu

F32 = jnp.float32
BF16 = jnp.bfloat16

D_MODEL = 1024
HEAD_DIM = 64
GRID_W = 64
WINDOW = 128
ROPE_THETA = 10000.0
EPS = 1e-6
NEG_INF = -1e30
LOG2E = 1.4426950408889634
Q_SCALE = HEAD_DIM ** -0.5 * LOG2E
LANES = 128
ROW_CHUNK = 256
SOFTMAX_VREGS = 40
VMEM_LIMIT = 60000 * 1024
W_SLAB_ROWS = 128


def _silu(x):
    return x / (1.0 + jnp.exp(-x))


def _dot(a, b):
    return jnp.dot(a, b, preferred_element_type=F32)


def _dot_nt(a, b):
    return lax.dot_general(a, b, (((1,), (1,)), ((), ())), preferred_element_type=F32)


def _lane_iota(rows):
    return lax.broadcasted_iota(jnp.int32, (rows, LANES), 1)


def _chunk_rows(i):
    if isinstance(i, int):
        return pl.ds(i * ROW_CHUNK, ROW_CHUNK)
    return pl.ds(pl.multiple_of(i * ROW_CHUNK, ROW_CHUNK), ROW_CHUNK)


def _softmax_rows(n_cols):
    rows = 8
    while rows * 2 * n_cols <= SOFTMAX_VREGS * 1024 and rows * 2 <= ROW_CHUNK:
        rows *= 2
    return rows


def _rope(a, cos, sin_signed):
    lane = _lane_iota(a.shape[0])
    fwd = pltpu.roll(a, LANES - 16, 1)
    bwd = pltpu.roll(a, 16, 1)
    partner = jnp.where((lane & 16) == 0, fwd, bwd)
    return a * cos + partner * sin_signed


def _rope_t(a, cos_t, sin_t):
    blocks = [a[16 * b:16 * (b + 1), :] for b in range(a.shape[0] // 16)]
    partner = jnp.concatenate([blocks[b ^ 1] for b in range(len(blocks))], axis=0)
    return a * cos_t + partner * sin_t


def _store_kt_variants(scr, chunk, kt):
    width = scr.shape[-1]
    per_block = kt.shape[1] // width
    zero = jnp.zeros((HEAD_DIM, kt.shape[1]), F32)
    for j in range(2):
        kj = kt[HEAD_DIM * j:HEAD_DIM * (j + 1), :]
        for par, full in enumerate((jnp.concatenate([kj, zero], axis=0), jnp.concatenate([zero, kj], axis=0))):
            full = full.astype(BF16)
            for c in range(per_block):
                scr[2 * j + par, chunk * per_block + c] = full[:, width * c:width * (c + 1)]


def _store_v_variants(scr, rows, a):
    lane = _lane_iota(a.shape[0])
    lo = lane < HEAD_DIM
    swapped = pltpu.roll(a, HEAD_DIM, 1)
    one = jnp.ones_like(a)
    scr[0, rows, :] = jnp.where(lo, a, one).astype(BF16)
    scr[1, rows, :] = jnp.where(lo, one, swapped).astype(BF16)
    scr[2, rows, :] = jnp.where(lo, swapped, one).astype(BF16)
    scr[3, rows, :] = jnp.where(lo, one, a).astype(BF16)


def _layer_norm_rows(z, g, b):
    mu = jnp.mean(z, axis=-1, keepdims=True)
    zc = z - mu
    var = jnp.mean(zc * zc, axis=-1, keepdims=True)
    return zc * lax.rsqrt(var + EPS) * g + b


def _modulate(x_ref, mod_ref, mod_row, h_scr, n_rows):
    shift = mod_ref[0, pl.ds(mod_row, 1), 0:D_MODEL]
    scale = mod_ref[0, pl.ds(mod_row, 1), D_MODEL:2 * D_MODEL]

    def body(i, carry):
        rows = _chunk_rows(i)
        h_scr[rows, :] = (x_ref[rows, :] * (1.0 + scale) + shift).astype(BF16)
        return carry

    lax.fori_loop(0, n_rows // ROW_CHUNK, body, 0)


def _out_proj_norm(x_ref, mod_ref, mod_row, attn_scr, w_out_ref, lng_ref, lnb_ref, y_ref, n_rows, alpha):
    gate = mod_ref[0, pl.ds(mod_row, 1), 2 * D_MODEL:3 * D_MODEL]
    g = lng_ref[0]
    b = lnb_ref[0]

    def body(i, carry):
        rows = _chunk_rows(i)
        out = _dot(attn_scr[rows, :], w_out_ref[...])
        z = alpha * x_ref[rows, :] + gate * out
        y_ref[rows, :] = _layer_norm_rows(z, g, b)
        return carry

    lax.fori_loop(0, n_rows // ROW_CHUNK, body, 0, unroll=True)


def _load_cast_weight(w_hbm, w_scr, stage, sems):
    n_cols = w_hbm.shape[1]
    n_slabs = w_hbm.shape[0] // W_SLAB_ROWS

    def slab_copy(s):
        return pltpu.make_async_copy(w_hbm.at[pl.ds(s * W_SLAB_ROWS, W_SLAB_ROWS), :],
                                     stage.at[s % 2, :, pl.ds(0, n_cols)], sems.at[s % 2])

    slab_copy(0).start()
    for s in range(n_slabs):
        if s + 1 < n_slabs:
            slab_copy(s + 1).start()
        slab_copy(s).wait()
        w_scr[pl.ds(s * W_SLAB_ROWS, W_SLAB_ROWS), :] = stage[s % 2, :, 0:n_cols].astype(BF16)


def _context_weights(step, w_hbm_pairs, stage, sems, out_sems):
    out_copies = [pltpu.make_async_copy(w_scr, w_bf_hbm, out_sems.at[n])
                  for n, (_, w_scr, w_bf_hbm) in enumerate(w_hbm_pairs)]

    @pl.when(step == 0)
    def _():
        for w_hbm, w_scr, _ in w_hbm_pairs:
            _load_cast_weight(w_hbm, w_scr, stage, sems)
        for copy in out_copies:
            copy.start()

    return out_copies


def _wait_first_step(step, copies):
    @pl.when(step == 0)
    def _():
        for copy in copies:
            copy.wait()


def _run_pipeline(n_items, stages):
    for u in range(n_items + len(stages) - 1):
        for k, stage in enumerate(stages):
            t = u - k
            if 0 <= t < n_items:
                stage(t, t % 2)


def _attend_blocks(block_stages, n_blocks, n_items, unrolled):
    assert n_items % 2 == 0
    if unrolled:
        per_block = [block_stages(i) for i in range(n_blocks)]
        stages = [lambda g, slot, k=k: per_block[g // n_items][k](g % n_items, slot) for k in range(3)]
        _run_pipeline(n_blocks * n_items, stages)
    else:
        def body(i, carry):
            _run_pipeline(n_items, block_stages(i))
            return carry

        lax.fori_loop(0, n_blocks, body, 0)


def _even_kernel(latent, n_rows, seq, alpha, *refs):
    if latent:
        (x_ref, mod_ref, w_in_ref, w_out_ref, qn_ref, knt_ref, sink_ref, lng_ref, lnb_ref, pm_ref,
         cos_ref, sin_ref, cost_ref, sint_ref, cakt_ref, cav_ref, cbkt_ref, cbv_ref,
         y_ref,
         ha_scr, qa_scr, qb_scr, ka_scr, va_scr, kb_scr, vb_scr, g_scr, s_scr, p_scr, es_scr, wkt_scr, wv_scr,
         bias_scr) = refs
    else:
        (x_ref, mod_ref, w_in_hbm, w_out_hbm, qn_ref, knt_ref, sink_ref, lng_ref, lnb_ref, pm_ref,
         y_ref, nakt_ref, navt_ref, nbkt_ref, nbvt_ref, w_in_bf_hbm, w_out_bf_hbm,
         ha_scr, qa_scr, qb_scr, ka_scr, va_scr, kb_scr, vb_scr, g_scr, s_scr, p_scr, es_scr, wkt_scr,
         wv_scr, w_in_ref, w_out_ref, w_stage, w_sems, w_out_sems) = refs

    step = pl.program_id(0)
    if not latent:
        weight_copies = _context_weights(step, [(w_in_hbm, w_in_ref, w_in_bf_hbm), (w_out_hbm, w_out_ref, w_out_bf_hbm)],
                                         w_stage, w_sems, w_out_sems)
    mod_row = step + 1 if latent else 0
    _modulate(x_ref, mod_ref, mod_row, ha_scr, n_rows)

    col_ka, col_va, col_kb, col_vb = 512, 640, 1792, 1920

    @pl.when(step == 0)
    def _():
        for r, c0 in enumerate((col_ka, col_kb)):
            wkt_scr[LANES * r:LANES * (r + 1), :] = w_in_ref[:, c0:c0 + LANES].T
        wv_scr[:, 0:LANES] = w_in_ref[:, col_va:col_va + LANES]
        wv_scr[:, LANES:2 * LANES] = w_in_ref[:, col_vb:col_vb + LANES]

    n_lat_chunks = seq // ROW_CHUNK
    if latent:
        n_past = cav_ref.shape[1]
        past_rows = pl.ds(seq, n_past)
        _store_kt_variants(ka_scr, n_lat_chunks, cakt_ref[0])
        _store_kt_variants(kb_scr, n_lat_chunks, cbkt_ref[0])
        _store_v_variants(va_scr, past_rows, cav_ref[0])
        _store_v_variants(vb_scr, past_rows, cbv_ref[0])

    pm = pm_ref[...]
    qn = qn_ref[...]
    knt = knt_ref[...]

    def proj(i, carry):
        rows = _chunk_rows(i)
        hh = ha_scr[rows, :]
        if latent:
            cos = cos_ref[rows, :]
            sin = sin_ref[rows, :]
            rot = lambda a: _rope(a, cos, sin)
            rot_t = lambda a: _rope_t(a, cost_ref[i], sint_ref[i])
        else:
            rot = rot_t = lambda a: a

        acc = _dot(hh, w_in_ref[:, 0:512])
        for j in range(4):
            a = acc[:, LANES * j:LANES * (j + 1)]
            ms = _dot((a * a).astype(BF16), pm)
            a = rot(a * lax.rsqrt(ms + EPS) * qn)
            qa_scr[rows, LANES * j:LANES * (j + 1)] = (a * Q_SCALE).astype(BF16)
        acc = _dot(hh, w_in_ref[:, 1280:1792])
        for j in range(4):
            a = rot(acc[:, LANES * j:LANES * (j + 1)])
            qb_scr[rows, LANES * j:LANES * (j + 1)] = (a * Q_SCALE).astype(BF16)
        g_scr[rows, 0:512] = _silu(_dot(hh, w_in_ref[:, 768:1280]))
        g_scr[rows, 512:1024] = _silu(_dot(hh, w_in_ref[:, 2048:2560]))
        v = _dot(hh, wv_scr[...])
        _store_v_variants(va_scr, rows, v[:, 0:LANES])
        _store_v_variants(vb_scr, rows, v[:, LANES:2 * LANES])

        kt = _dot_nt(wkt_scr[0:2 * LANES, :], hh)
        kat = kt[0:LANES, :]
        ms = _dot(pm, (kat * kat).astype(BF16))
        kat = kat * lax.rsqrt(ms + EPS) * knt
        kbt = kt[LANES:2 * LANES, :]
        if not latent:
            vt = v.T
            nakt_ref[i] = kat
            nbkt_ref[i] = kbt
            navt_ref[i] = vt[0:LANES, :]
            nbvt_ref[i] = vt[LANES:2 * LANES, :]
        _store_kt_variants(ka_scr, i, rot_t(kat))
        _store_kt_variants(kb_scr, i, rot_t(kbt))
        return carry

    lax.fori_loop(0, n_rows // ROW_CHUNK, proj, 0, unroll=2)

    sinks = [sink_ref[h] * LOG2E for h in range(8)]
    ck = ROW_CHUNK
    bk = kb_scr.shape[-1]
    win = ROW_CHUNK + 2 * WINDOW
    n_items = 8

    def block_stages(i):
        rows = _chunk_rows(i)
        if latent:
            a_chunks = list(range(n_lat_chunks + n_past // ck))
            a_keys = pl.ds(0, seq + n_past)
            w0 = jnp.clip(i * (ck // bk) - WINDOW // bk, 0, (seq - win) // bk)
            win_rows = pl.ds(pl.multiple_of(w0 * bk, bk), win)
            dist = (lax.broadcasted_iota(jnp.int32, (ROW_CHUNK, ck), 1)
                    - lax.broadcasted_iota(jnp.int32, (ROW_CHUNK, ck), 0))
            for c in range(win // ck):
                off = w0 * bk + c * ck - i * ck
                bias_scr[c] = jnp.where(jnp.abs(dist + off) <= WINDOW, 0.0, NEG_INF).astype(F32)
            b_first = [w0 + c * (ck // bk) for c in range(win // ck)] + [seq // bk]
            n_biased = win // ck
            b_cols = win + n_past
        else:
            a_chunks = [i]
            a_keys = rows
            b_first = [i * (ck // bk)]
            n_biased = 0
            b_cols = ck
        a_cols = len(a_chunks) * ck

        def qk(t, slot):
            p, branch = divmod(t, 2)
            cols = slice(LANES * p, LANES * (p + 1))
            kvh = p // 2
            q = (qb_scr if branch else qa_scr)[rows, cols]
            for par in (0, 1):
                var = 2 * kvh + par
                if branch:
                    tiles = [jnp.concatenate([kb_scr[var, first + d] for d in range(ck // bk)], axis=1)
                             for first in b_first]
                else:
                    tiles = [ka_scr[var, chunk] for chunk in a_chunks]
                for c, kt in enumerate(tiles):
                    s = _dot(q, kt)
                    if branch and c < n_biased:
                        s = s + bias_scr[c]
                    s_scr[slot, par, :, c * ck:(c + 1) * ck] = s

        def softmax(t, slot):
            p, branch = divmod(t, 2)
            n_cols = b_cols if branch else a_cols
            rb = _softmax_rows(n_cols)
            for par in (0, 1):
                for r in range(ROW_CHUNK // rb):
                    sub = slice(r * rb, (r + 1) * rb)
                    s = s_scr[slot, par, sub, 0:n_cols]
                    m = jnp.max(s, axis=1, keepdims=True)
                    if branch:
                        sink = sinks[2 * p + par]
                        m = jnp.maximum(m, sink)
                        es_scr[slot, sub, HEAD_DIM * par:HEAD_DIM * (par + 1)] = jnp.broadcast_to(
                            jnp.exp2(sink - m), (rb, HEAD_DIM))
                    p_scr[slot, par, sub, 0:n_cols] = jnp.exp2((s - m).astype(BF16))

        def pv(t, slot):
            p, branch = divmod(t, 2)
            kvh = p // 2
            v_scr = vb_scr if branch else va_scr
            accs = []
            for par in (0, 1):
                var = 2 * kvh + par
                if latent and branch:
                    n_loc = win
                    accs.append(_dot(p_scr[slot, par, :, 0:n_loc], v_scr[var, win_rows, :])
                                + _dot(p_scr[slot, par, :, n_loc:b_cols], v_scr[var, past_rows, :]))
                else:
                    accs.append(_dot(p_scr[slot, par, :, 0:a_cols], v_scr[var, a_keys, :]))
            lo = _lane_iota(ROW_CHUNK) < HEAD_DIM
            denom = pltpu.roll(jnp.where(lo, accs[1], accs[0]), HEAD_DIM, 1)
            if branch:
                denom = denom + es_scr[slot]
            o = jnp.where(lo, accs[0], accs[1]) / denom
            ocols = slice(512 * branch + LANES * p, 512 * branch + LANES * (p + 1))
            ha_scr[rows, ocols] = (o * g_scr[rows, ocols]).astype(BF16)

        return qk, softmax, pv

    _attend_blocks(block_stages, n_rows // ROW_CHUNK, n_items, unrolled=not latent)

    _out_proj_norm(x_ref, mod_ref, mod_row, ha_scr, w_out_ref, lng_ref, lnb_ref, y_ref, n_rows, alpha)
    if not latent:
        _wait_first_step(step, weight_copies)


def _odd_kernel(latent, n_rows, seq, alpha, lam_init, *refs):
    if latent:
        (x_ref, mod_ref, w_in_ref, w_out_ref, lq1_ref, lk1_ref, lq2_ref, lk2_ref, sub_ref, lng_ref, lnb_ref,
         cos_ref, sin_ref, cck_ref, ccv_ref,
         y_ref,
         ha_scr, q_scr, k_scr, v_scr, g_scr, s_scr, p_scr) = refs
    else:
        (x_ref, mod_ref, w_in_hbm, w_out_hbm, lq1_ref, lk1_ref, lq2_ref, lk2_ref, sub_ref, lng_ref, lnb_ref,
         y_ref, nck_hbm, ncv_hbm, w_in_bf_hbm, w_out_bf_hbm,
         ha_scr, q_scr, k_scr, v_scr, g_scr, s_scr, p_scr, kv_stage, kv_sems,
         w_in_ref, w_out_ref, w_stage, w_sems, w_out_sems) = refs

    step = pl.program_id(0)
    if not latent:
        weight_copies = _context_weights(step, [(w_in_hbm, w_in_ref, w_in_bf_hbm), (w_out_hbm, w_out_ref, w_out_bf_hbm)],
                                         w_stage, w_sems, w_out_sems)
    mod_row = step + 1 if latent else 0
    _modulate(x_ref, mod_ref, mod_row, ha_scr, n_rows)

    n_heads = D_MODEL // LANES
    n_blocks = n_rows // ROW_CHUNK
    lo = _lane_iota(ROW_CHUNK) < HEAD_DIM

    def kv_out_copies(blk):
        elem = step * n_blocks + blk
        return [pltpu.make_async_copy(kv_stage.at[blk, t, :, pl.ds(LANES * h, LANES)],
                                      out.at[elem, 0, :, h, :], kv_sems.at[blk, t])
                for t, out in enumerate((nck_hbm, ncv_hbm)) for h in range(n_heads)]

    def store_k(rows, h, a):
        cols = slice(LANES * h, LANES * (h + 1))
        zero = jnp.zeros_like(a)
        k_scr[0, rows, cols] = jnp.where(lo, a, zero).astype(BF16)
        k_scr[1, rows, cols] = jnp.where(lo, zero, a).astype(BF16)

    if latent:
        n_past = cck_ref.shape[2]
        past = pl.ds(seq, n_past)
        for h in range(n_heads):
            store_k(past, h, cck_ref[0, 0, :, h, :])
            v_scr[past, LANES * h:LANES * (h + 1)] = ccv_ref[0, 0, :, h, :].astype(BF16)

    def proj(i, carry):
        rows = _chunk_rows(i)
        hh = ha_scr[rows, :]
        if latent:
            cos = cos_ref[rows, :]
            sin = sin_ref[rows, :]
            rot = lambda a: _rope(a, cos, sin)
        else:
            rot = lambda a: a
        for half in range(2):
            acc = _dot(hh, w_in_ref[:, 512 * half:512 * (half + 1)])
            for j in range(4):
                a = rot(acc[:, LANES * j:LANES * (j + 1)])
                cols = slice(512 * half + LANES * j, 512 * half + LANES * (j + 1))
                q_scr[rows, cols] = (a * Q_SCALE).astype(BF16)
        for half in range(2):
            acc = _dot(hh, w_in_ref[:, 1024 + 512 * half:1024 + 512 * (half + 1)])
            if not latent:
                kv_stage[i, 0, :, 512 * half:512 * (half + 1)] = acc
            for j in range(4):
                store_k(rows, 4 * half + j, rot(acc[:, LANES * j:LANES * (j + 1)]))
        for half in range(2):
            acc = _dot(hh, w_in_ref[:, 2048 + 512 * half:2048 + 512 * (half + 1)])
            if not latent:
                kv_stage[i, 1, :, 512 * half:512 * (half + 1)] = acc
            v_scr[rows, 512 * half:512 * (half + 1)] = acc.astype(BF16)
        if not latent:
            for copy in kv_out_copies(i):
                copy.start()
        for half in range(2):
            acc = _dot(hh, w_in_ref[:, 3072 + 512 * half:3072 + 512 * (half + 1)])
            g_scr[rows, 512 * half:512 * (half + 1)] = _silu(acc)
        return carry

    if latent:
        lax.fori_loop(0, n_blocks, proj, 0, unroll=2)
    else:
        for blk in range(n_blocks):
            proj(blk, 0)

    lam = (jnp.exp(jnp.sum(lq1_ref[...] * lk1_ref[...], axis=1, keepdims=True))
           - jnp.exp(jnp.sum(lq2_ref[...] * lk2_ref[...], axis=1, keepdims=True)) + lam_init)
    sub = sub_ref[...] * (1.0 - lam_init)
    n_keys = seq + n_past if latent else ROW_CHUNK
    rb = _softmax_rows(n_keys)
    ones = jnp.ones((n_keys, LANES), BF16)

    def block_stages(i):
        rows = _chunk_rows(i)
        keys = pl.ds(0, n_keys) if latent else rows

        def qk(h, slot):
            cols = slice(LANES * h, LANES * (h + 1))
            q = q_scr[rows, cols]
            for m in (0, 1):
                s_scr[slot, m] = _dot_nt(q, k_scr[m, keys, cols])

        def softmax(h, slot):
            for m in (0, 1):
                for r in range(ROW_CHUNK // rb):
                    sub_rows = slice(r * rb, (r + 1) * rb)
                    s = s_scr[slot, m, sub_rows, :]
                    top = jnp.max(s, axis=1, keepdims=True)
                    p_scr[slot, m, sub_rows, :] = jnp.exp2((s - top).astype(BF16))

        def pv(h, slot):
            cols = slice(LANES * h, LANES * (h + 1))
            v_ext = jnp.concatenate([v_scr[keys, cols], ones], axis=1)
            maps = []
            for m in (0, 1):
                acc = _dot(p_scr[slot, m], v_ext)
                maps.append(acc[:, 0:LANES] / acc[:, LANES:2 * LANES])
            o = maps[0] - lam * maps[1]
            ms = jnp.mean(o * o, axis=1, keepdims=True)
            o = o * lax.rsqrt(ms + EPS) * sub
            ha_scr[rows, cols] = (o * g_scr[rows, cols]).astype(BF16)

        return qk, softmax, pv

    _attend_blocks(block_stages, n_blocks, n_heads, unrolled=not latent)

    _out_proj_norm(x_ref, mod_ref, mod_row, ha_scr, w_out_ref, lng_ref, lnb_ref, y_ref, n_rows, alpha)

    if not latent:
        for blk in range(n_blocks):
            for copy in kv_out_copies(blk):
                copy.wait()
        _wait_first_step(step, weight_copies)


def _mod_kernel(n_cond, cvb_ref, w_ref, b_ref, o_ref, sb_scr):
    @pl.when((pl.program_id(0) == 0) & (pl.program_id(1) == 0))
    def _():
        sb_scr[...] = _silu(cvb_ref[...])

    n_out = w_ref.shape[2]
    sublanes = 8

    def body(kb, accs):
        rows = pl.ds(pl.multiple_of(kb * sublanes, sublanes), sublanes)
        w = w_ref[0, rows, :]
        return tuple(acc + w * jnp.tile(sb_scr[r, rows, :], (1, n_out // LANES)) for r, acc in enumerate(accs))

    zero = jnp.zeros((sublanes, n_out), F32)
    accs = lax.fori_loop(0, D_MODEL // sublanes, body, (zero,) * n_cond, unroll=8)
    out_rows = [jnp.sum(acc, axis=0, keepdims=True) + b_ref[0] for acc in accs]
    o_ref[0] = jnp.concatenate(out_rows + [jnp.zeros((8 - n_cond, n_out), F32)], axis=0)


def _full(shape, **kw):
    zeros = (0,) * len(shape)
    return pl.BlockSpec(shape, lambda i: zeros, **kw)


def _weight_specs(latent, w_in, w_out):
    if latent:
        single = pl.Buffered(1)
        return [_full(w_in.shape, pipeline_mode=single), _full(w_out.shape, pipeline_mode=single)]
    return [pl.BlockSpec(memory_space=pl.ANY), pl.BlockSpec(memory_space=pl.ANY)]


def _weight_scratch(w_in, w_out):
    assert w_in.shape[0] % W_SLAB_ROWS == 0 and w_out.shape[0] % W_SLAB_ROWS == 0
    return [pltpu.VMEM(w_in.shape, BF16), pltpu.VMEM(w_out.shape, BF16),
            pltpu.VMEM((2, W_SLAB_ROWS, max(w_in.shape[1], w_out.shape[1])), F32),
            pltpu.SemaphoreType.DMA((2,)), pltpu.SemaphoreType.DMA((2,))]


def _rope_tables(seq):
    t = np.arange(seq)
    n_freq = HEAD_DIM // 4
    freqs = ROPE_THETA ** (-np.arange(n_freq, dtype=np.float64) / n_freq)
    ang_row = (t // GRID_W)[:, None] * freqs
    ang_col = (t % GRID_W)[:, None] * freqs
    ang = np.concatenate([ang_row, ang_row, ang_col, ang_col], axis=1)
    sign = np.concatenate([-np.ones(n_freq), np.ones(n_freq)] * 2)[None, :]
    cos = np.tile(np.cos(ang), (1, 2)).astype(np.float32)
    sin = np.tile(np.sin(ang) * sign, (1, 2)).astype(np.float32)
    chunked_t = lambda a: a.reshape(seq // ROW_CHUNK, ROW_CHUNK, LANES).transpose(0, 2, 1)
    return jnp.asarray(cos), jnp.asarray(sin), jnp.asarray(chunked_t(cos)), jnp.asarray(chunked_t(sin))


def _head_mean_matrix():
    idx = np.arange(LANES) // HEAD_DIM
    return jnp.asarray((idx[:, None] == idx[None, :]).astype(np.float32) / HEAD_DIM, dtype=BF16)


def _modulation(c, c_ctx, w_mod, b_mod):
    depth = w_mod.shape[0]
    cv = jnp.concatenate([c_ctx[None, :], c], axis=0)
    n_cond = cv.shape[0]
    cvb = jnp.broadcast_to(cv[:, :, None], (n_cond, D_MODEL, LANES))
    n_blk = 3 * D_MODEL // 1024
    return pl.pallas_call(
        functools.partial(_mod_kernel, n_cond),
        grid=(depth, n_blk),
        in_specs=[pl.BlockSpec((n_cond, D_MODEL, LANES), lambda l, n: (0, 0, 0)),
                  pl.BlockSpec((1, D_MODEL, 1024), lambda l, n: (l, 0, n)),
                  pl.BlockSpec((1, 1, 1024), lambda l, n: (l, 0, n))],
        out_specs=pl.BlockSpec((1, 8, 1024), lambda l, n: (l, 0, n)),
        out_shape=jax.ShapeDtypeStruct((depth, 8, 3 * D_MODEL), F32),
        scratch_shapes=[pltpu.VMEM((n_cond, D_MODEL, LANES), F32)],
        compiler_params=pltpu.CompilerParams(dimension_semantics=("arbitrary", "arbitrary")),
        name="adaln_modulation",
    )(cvb, w_mod, b_mod.reshape(depth, 1, 3 * D_MODEL))


def _even_layer(x, mod, layer, w_in, w_out, q_norm, k_norm, sink, ln_g, ln_b, latent, seq, n_rows, alpha, extras=()):
    total = x.shape[0]
    grid = (total // n_rows,)
    single = pl.Buffered(1)
    qn = jnp.tile(q_norm, 2)[None, :]
    knt = jnp.broadcast_to(jnp.tile(k_norm, 2)[:, None], (LANES, ROW_CHUNK))

    row_blk = lambda width: pl.BlockSpec((n_rows, width), lambda i: (i, 0))
    in_specs = [row_blk(D_MODEL),
                pl.BlockSpec((1, 8, 3 * D_MODEL), lambda i: (layer, 0, 0)),
                *_weight_specs(latent, w_in, w_out),
                _full((1, LANES)), _full((LANES, ROW_CHUNK)),
                pl.BlockSpec(memory_space=pltpu.SMEM),
                pl.BlockSpec((1, 1, D_MODEL), lambda i: (layer, 0, 0)),
                pl.BlockSpec((1, 1, D_MODEL), lambda i: (layer, 0, 0)),
                _full((LANES, LANES))]
    args = [x, mod, w_in, w_out, qn, knt, sink, ln_g, ln_b, _head_mean_matrix()]
    y_shape = jax.ShapeDtypeStruct((total, D_MODEL), F32)
    n_blocks = n_rows // ROW_CHUNK
    if latent:
        cos, sin, cos_t, sin_t, cakt, cav, cbkt, cbv = extras
        n_past = cav.shape[1]
        in_specs += [_full(cos.shape, pipeline_mode=single), _full(sin.shape, pipeline_mode=single),
                     _full(cos_t.shape, pipeline_mode=single), _full(sin_t.shape, pipeline_mode=single)]
        in_specs += [pl.BlockSpec((1, LANES, n_past), lambda i: (i, 0, 0)),
                     pl.BlockSpec((1, n_past, LANES), lambda i: (i, 0, 0))] * 2
        args += [cos, sin, cos_t, sin_t, cakt, cav, cbkt, cbv]
        out_specs = row_blk(D_MODEL)
        out_shape = y_shape
        n_keys = seq + n_past
    else:
        kv_blk = pl.BlockSpec((n_blocks, LANES, ROW_CHUNK), lambda i: (i, 0, 0))
        hbm = pl.BlockSpec(memory_space=pl.ANY)
        out_specs = [row_blk(D_MODEL)] + [kv_blk] * 4 + [hbm, hbm]
        out_shape = ([y_shape] + [jax.ShapeDtypeStruct((total // seq, LANES, seq), F32)] * 4
                     + [jax.ShapeDtypeStruct(w_in.shape, BF16), jax.ShapeDtypeStruct(w_out.shape, BF16)])
        n_keys = n_rows
    n_kchunks = n_keys // ROW_CHUNK
    n_cols = n_keys if latent else ROW_CHUNK
    scratch = [pltpu.VMEM((n_rows, D_MODEL), BF16),
               pltpu.VMEM((n_rows, 512), BF16), pltpu.VMEM((n_rows, 512), BF16),
               pltpu.VMEM((4, n_kchunks, LANES, ROW_CHUNK), BF16), pltpu.VMEM((4, n_keys, LANES), BF16),
               pltpu.VMEM((4, n_keys // WINDOW, LANES, WINDOW), BF16), pltpu.VMEM((4, n_keys, LANES), BF16),
               pltpu.VMEM((n_rows, D_MODEL), F32),
               pltpu.VMEM((2, 2, ROW_CHUNK, n_cols), F32),
               pltpu.VMEM((2, 2, ROW_CHUNK, n_cols), BF16),
               pltpu.VMEM((2, ROW_CHUNK, LANES), F32),
               pltpu.VMEM((2 * LANES, D_MODEL), BF16),
               pltpu.VMEM((D_MODEL, 2 * LANES), BF16)]
    if latent:
        scratch.append(pltpu.VMEM((1 + 2 * WINDOW // ROW_CHUNK, ROW_CHUNK, ROW_CHUNK), F32))
    else:
        scratch += _weight_scratch(w_in, w_out)
    return pl.pallas_call(
        functools.partial(_even_kernel, latent, n_rows, seq, alpha),
        grid=grid, in_specs=in_specs, out_specs=out_specs, out_shape=out_shape,
        scratch_shapes=scratch,
        compiler_params=pltpu.CompilerParams(dimension_semantics=("arbitrary",), vmem_limit_bytes=VMEM_LIMIT),
        name="even_layer_latent" if latent else "even_layer_context",
    )(*args)


def _odd_layer(x, mod, layer, w_in, w_out, lams, sub, ln_g, ln_b, latent, seq, n_rows, alpha, lam_init, extras=()):
    total = x.shape[0]
    grid = (total // n_rows,)
    row_blk = lambda width: pl.BlockSpec((n_rows, width), lambda i: (i, 0))
    single = pl.Buffered(1)
    in_specs = [row_blk(D_MODEL),
                pl.BlockSpec((1, 8, 3 * D_MODEL), lambda i: (layer, 0, 0)),
                *_weight_specs(latent, w_in, w_out),
                _full((1, HEAD_DIM)), _full((1, HEAD_DIM)), _full((1, HEAD_DIM)), _full((1, HEAD_DIM)),
                _full((1, LANES)),
                pl.BlockSpec((1, 1, D_MODEL), lambda i: (layer, 0, 0)),
                pl.BlockSpec((1, 1, D_MODEL), lambda i: (layer, 0, 0))]
    args = [x, mod, w_in, w_out, *lams, sub, ln_g, ln_b]
    y_shape = jax.ShapeDtypeStruct((total, D_MODEL), F32)
    n_heads = D_MODEL // LANES
    n_blocks = n_rows // ROW_CHUNK
    if latent:
        cos, sin, cck, ccv = extras
        n_past = cck.shape[2]
        in_specs += [_full(cos.shape, pipeline_mode=single), _full(sin.shape, pipeline_mode=single)]
        in_specs += [pl.BlockSpec((1, 1, n_past, n_heads, LANES), lambda i: (i, layer // 2, 0, 0, 0))] * 2
        args += [cos, sin, cck, ccv]
        out_specs = row_blk(D_MODEL)
        out_shape = y_shape
        n_keys = seq + n_past
    else:
        hbm = pl.BlockSpec(memory_space=pl.ANY)
        out_specs = [row_blk(D_MODEL), hbm, hbm, hbm, hbm]
        out_shape = ([y_shape] + [jax.ShapeDtypeStruct((total // seq, 1, seq, n_heads, LANES), F32)] * 2
                     + [jax.ShapeDtypeStruct(w_in.shape, BF16), jax.ShapeDtypeStruct(w_out.shape, BF16)])
        n_keys = n_rows
    n_cols = n_keys if latent else ROW_CHUNK
    scratch = [pltpu.VMEM((n_rows, D_MODEL), BF16),
               pltpu.VMEM((n_rows, D_MODEL), BF16),
               pltpu.VMEM((2, n_keys, D_MODEL), BF16),
               pltpu.VMEM((n_keys, D_MODEL), BF16),
               pltpu.VMEM((n_rows, D_MODEL), F32),
               pltpu.VMEM((2, 2, ROW_CHUNK, n_cols), F32),
               pltpu.VMEM((2, 2, ROW_CHUNK, n_cols), BF16)]
    if not latent:
        scratch += [pltpu.VMEM((n_blocks, 2, ROW_CHUNK, D_MODEL), F32),
                    pltpu.SemaphoreType.DMA((n_blocks, 2))]
        scratch += _weight_scratch(w_in, w_out)
    return pl.pallas_call(
        functools.partial(_odd_kernel, latent, n_rows, seq, alpha, lam_init),
        grid=grid, in_specs=in_specs, out_specs=out_specs, out_shape=out_shape,
        scratch_shapes=scratch,
        compiler_params=pltpu.CompilerParams(dimension_semantics=("arbitrary",), vmem_limit_bytes=VMEM_LIMIT),
        name="odd_layer_latent" if latent else "odd_layer_context",
    )(*args)


def kernel(x_prompt, x_sample, cache_a_k, cache_a_v, cache_b_k, cache_b_v, cache_c_k, cache_c_v, c, c_ctx,
           w_mod, b_mod, ln_g, ln_b, w_in_even, w_out_even, q_norm_a, k_norm_a, sink_b, w_in_odd, w_out_odd,
           lambda_q1, lambda_k1, lambda_q2, lambda_k2, subln_c):
    depth = w_mod.shape[0]
    batch, seq, _ = x_prompt.shape
    dec_batch, dec_seq, _ = x_sample.shape
    n_past = cache_a_k.shape[2]
    alpha = (2 * depth) ** 0.25
    assert seq == ROW_CHUNK and n_past % ROW_CHUNK == 0 and dec_seq % ROW_CHUNK == 0

    mod = _modulation(c, c_ctx, w_mod, b_mod)
    ln_g3 = ln_g.reshape(depth, 1, D_MODEL)
    ln_b3 = ln_b.reshape(depth, 1, D_MODEL)
    cos, sin, cos_t, sin_t = _rope_tables(dec_seq)

    bf16_weights = {}

    def run(x, latent, n_batch, s, rows_even, rows_odd):
        kv = {"a_k": [], "a_v": [], "b_k": [], "b_v": [], "c_k": [], "c_v": []}
        for l in range(depth):
            if l % 2 == 0:
                e = l // 2
                extras = ()
                if latent:
                    k_t = lambda t: t[:, e].transpose(0, 2, 3, 1).reshape(n_batch, LANES, n_past)
                    v_n = lambda t: t[:, e].reshape(n_batch, n_past, LANES)
                    extras = (cos, sin, cos_t, sin_t,
                              k_t(cache_a_k), v_n(cache_a_v), k_t(cache_b_k), v_n(cache_b_v))
                w_in, w_out = bf16_weights[l] if latent else (w_in_even[e], w_out_even[e])
                res = _even_layer(x, mod, l, w_in, w_out, q_norm_a[e], k_norm_a[e],
                                  sink_b[e], ln_g3, ln_b3, latent, s, rows_even, alpha, extras)
                if latent:
                    x = res
                else:
                    x = res[0]
                    bf16_weights[l] = res[5:7]
                    for name, t in zip(("a_k", "a_v", "b_k", "b_v"), res[1:5]):
                        kv[name].append(t.reshape(n_batch, 2, HEAD_DIM, s).transpose(0, 3, 1, 2))
            else:
                o = l // 2
                lam_init = 0.8 - 0.6 * math.exp(-0.3 * l)
                extras = (cos, sin, cache_c_k, cache_c_v) if latent else ()
                lams = [t[o][None, :] for t in (lambda_q1, lambda_k1, lambda_q2, lambda_k2)]
                w_in, w_out = bf16_weights[l] if latent else (w_in_odd[o], w_out_odd[o])
                res = _odd_layer(x, mod, l, w_in, w_out, lams,
                                 subln_c[o][None, :], ln_g3, ln_b3, latent, s, rows_odd, alpha, lam_init, extras)
                if latent:
                    x = res
                else:
                    x = res[0]
                    bf16_weights[l] = res[3:5]
                    kv["c_k"].append(res[1][:, 0])
                    kv["c_v"].append(res[2][:, 0])
        return x, kv

    y_ctx, kv = run(x_prompt.reshape(batch * seq, D_MODEL), False, batch, seq, 1024, 512)
    y_lat, _ = run(x_sample.reshape(dec_batch * dec_seq, D_MODEL), True, dec_batch, dec_seq, dec_seq, dec_seq)

    stack = lambda name: jnp.stack(kv[name], axis=1)
    return (y_ctx.reshape(batch, seq, D_MODEL), y_lat.reshape(dec_batch, dec_seq, D_MODEL),
            stack("a_k"), stack("a_v"), stack("b_k"), stack("b_v"), stack("c_k"), stack("c_v"))
```

```python
import functools
import math

import jax
import jax.numpy as jnp
import numpy as np
from jax import lax
from jax.experimental import pallas as pl
from jax.experimental.pallas import tpu as pltpu

F32 = jnp.float32
BF16 = jnp.bfloat16

D_MODEL = 1024
HEAD_DIM = 64
GRID_W = 64
WINDOW = 128
ROPE_THETA = 10000.0
EPS = 1e-6
NEG_INF = -1e30
LOG2E = 1.4426950408889634
Q_SCALE = HEAD_DIM ** -0.5 * LOG2E
LANES = 128
ROW_CHUNK = 256
SOFTMAX_VREGS = 40
VMEM_LIMIT = 60000 * 1024
W_SLAB_ROWS = 128


def _silu(x):
    return x / (1.0 + jnp.exp(-x))


def _dot(a, b):
    return jnp.dot(a, b, preferred_element_type=F32)


def _dot_nt(a, b):
    return lax.dot_general(a, b, (((1,), (1,)), ((), ())), preferred_element_type=F32)


def _lane_iota(rows):
    return lax.broadcasted_iota(jnp.int32, (rows, LANES), 1)


def _chunk_rows(i):
    if isinstance(i, int):
        return pl.ds(i * ROW_CHUNK, ROW_CHUNK)
    return pl.ds(pl.multiple_of(i * ROW_CHUNK, ROW_CHUNK), ROW_CHUNK)


def _softmax_rows(n_cols):
    rows = 8
    while rows * 2 * n_cols <= SOFTMAX_VREGS * 1024 and rows * 2 <= ROW_CHUNK:
        rows *= 2
    return rows


def _rope(a, cos, sin_signed):
    lane = _lane_iota(a.shape[0])
    fwd = pltpu.roll(a, LANES - 16, 1)
    bwd = pltpu.roll(a, 16, 1)
    partner = jnp.where((lane & 16) == 0, fwd, bwd)
    return a * cos + partner * sin_signed


def _rope_t(a, cos_t, sin_t):
    blocks = [a[16 * b:16 * (b + 1), :] for b in range(a.shape[0] // 16)]
    partner = jnp.concatenate([blocks[b ^ 1] for b in range(len(blocks))], axis=0)
    return a * cos_t + partner * sin_t


def _store_kt_variants(scr, chunk, kt):
    width = scr.shape[-1]
    per_block = kt.shape[1] // width
    zero = jnp.zeros((HEAD_DIM, kt.shape[1]), F32)
    for j in range(2):
        kj = kt[HEAD_DIM * j:HEAD_DIM * (j + 1), :]
        for par, full in enumerate((jnp.concatenate([kj, zero], axis=0), jnp.concatenate([zero, kj], axis=0))):
            full = full.astype(BF16)
            for c in range(per_block):
                scr[2 * j + par, chunk * per_block + c] = full[:, width * c:width * (c + 1)]


def _store_v_variants(scr, rows, a):
    lane = _lane_iota(a.shape[0])
    lo = lane < HEAD_DIM
    swapped = pltpu.roll(a, HEAD_DIM, 1)
    one = jnp.ones_like(a)
    scr[0, rows, :] = jnp.where(lo, a, one).astype(BF16)
    scr[1, rows, :] = jnp.where(lo, one, swapped).astype(BF16)
    scr[2, rows, :] = jnp.where(lo, swapped, one).astype(BF16)
    scr[3, rows, :] = jnp.where(lo, one, a).astype(BF16)


def _layer_norm_rows(z, g, b):
    mu = jnp.mean(z, axis=-1, keepdims=True)
    zc = z - mu
    var = jnp.mean(zc * zc, axis=-1, keepdims=True)
    return zc * lax.rsqrt(var + EPS) * g + b


def _modulate(x_ref, mod_ref, mod_row, h_scr, n_rows):
    shift = mod_ref[0, pl.ds(mod_row, 1), 0:D_MODEL]
    scale = mod_ref[0, pl.ds(mod_row, 1), D_MODEL:2 * D_MODEL]

    def body(i, carry):
        rows = _chunk_rows(i)
        h_scr[rows, :] = (x_ref[rows, :] * (1.0 + scale) + shift).astype(BF16)
        return carry

    lax.fori_loop(0, n_rows // ROW_CHUNK, body, 0)


def _out_proj_norm(x_ref, mod_ref, mod_row, attn_scr, w_out_ref, lng_ref, lnb_ref, y_ref, n_rows, alpha):
    gate = mod_ref[0, pl.ds(mod_row, 1), 2 * D_MODEL:3 * D_MODEL]
    g = lng_ref[0]
    b = lnb_ref[0]

    def body(i, carry):
        rows = _chunk_rows(i)
        out = _dot(attn_scr[rows, :], w_out_ref[...])
        z = alpha * x_ref[rows, :] + gate * out
        y_ref[rows, :] = _layer_norm_rows(z, g, b)
        return carry

    lax.fori_loop(0, n_rows // ROW_CHUNK, body, 0, unroll=True)


class _ContextWeights:
    def __init__(self, step, w_in, w_out, stage, sems, out_sems):
        self.step, self.w_in, self.w_out, self.stage, self.sems = step, w_in, w_out, stage, sems
        self.out_copies = [pltpu.make_async_copy(w[1], w[2], out_sems.at[n]) for n, w in enumerate((w_in, w_out))]

    def _slab_copies(self, w_hbm):
        n_cols = w_hbm.shape[1]
        return [pltpu.make_async_copy(w_hbm.at[pl.ds(s * W_SLAB_ROWS, W_SLAB_ROWS), :],
                                      self.stage.at[s, :, pl.ds(0, n_cols)], self.sems.at[s])
                for s in range(w_hbm.shape[0] // W_SLAB_ROWS)]

    def _cast(self, w_hbm, w_scr):
        n_cols = w_hbm.shape[1]
        for s, copy in enumerate(self._slab_copies(w_hbm)):
            copy.wait()
            w_scr[pl.ds(s * W_SLAB_ROWS, W_SLAB_ROWS), :] = self.stage[s, :, 0:n_cols].astype(BF16)

    def load_in_proj(self):
        @pl.when(self.step == 0)
        def _():
            for copy in self._slab_copies(self.w_in[0]):
                copy.start()
            self._cast(self.w_in[0], self.w_in[1])
            self.out_copies[0].start()
            for copy in self._slab_copies(self.w_out[0]):
                copy.start()

    def load_out_proj(self):
        @pl.when(self.step == 0)
        def _():
            self._cast(self.w_out[0], self.w_out[1])
            self.out_copies[1].start()

    def finish(self):
        @pl.when(self.step == 0)
        def _():
            for copy in self.out_copies:
                copy.wait()


def _run_pipeline(n_items, stages):
    for u in range(n_items + len(stages) - 1):
        for k, stage in enumerate(stages):
            t = u - k
            if 0 <= t < n_items:
                stage(t, t % 2)


def _attend_blocks(block_stages, n_blocks, n_items, unrolled):
    assert n_items % 2 == 0
    if unrolled:
        per_block = [block_stages(i) for i in range(n_blocks)]
        stages = [lambda g, slot, k=k: per_block[g // n_items][k](g % n_items, slot) for k in range(3)]
        _run_pipeline(n_blocks * n_items, stages)
    else:
        def body(i, carry):
            _run_pipeline(n_items, block_stages(i))
            return carry

        lax.fori_loop(0, n_blocks, body, 0)


def _even_kernel(latent, n_rows, seq, alpha, *refs):
    if latent:
        (x_ref, mod_ref, w_in_ref, w_out_ref, qn_ref, knt_ref, sink_ref, lng_ref, lnb_ref, pm_ref,
         cos_ref, sin_ref, cost_ref, sint_ref, cakt_ref, cav_ref, cbkt_ref, cbv_ref,
         y_ref,
         ha_scr, qa_scr, qb_scr, ka_scr, va_scr, kb_scr, vb_scr, g_scr, s_scr, p_scr, es_scr, wkt_scr, wv_scr,
         bias_scr) = refs
    else:
        (x_ref, mod_ref, w_in_hbm, w_out_hbm, qn_ref, knt_ref, sink_ref, lng_ref, lnb_ref, pm_ref,
         y_ref, nakt_ref, navt_ref, nbkt_ref, nbvt_ref, w_in_bf_hbm, w_out_bf_hbm,
         ha_scr, qa_scr, qb_scr, ka_scr, va_scr, kb_scr, vb_scr, g_scr, s_scr, p_scr, es_scr, wkt_scr,
         wv_scr, w_in_ref, w_out_ref, w_stage, w_sems, w_out_sems) = refs

    step = pl.program_id(0)
    if not latent:
        weights = _ContextWeights(step, (w_in_hbm, w_in_ref, w_in_bf_hbm), (w_out_hbm, w_out_ref, w_out_bf_hbm),
                                  w_stage, w_sems, w_out_sems)
        weights.load_in_proj()
    mod_row = step + 1 if latent else 0
    _modulate(x_ref, mod_ref, mod_row, ha_scr, n_rows)

    col_ka, col_va, col_kb, col_vb = 512, 640, 1792, 1920

    @pl.when(step == 0)
    def _():
        for r, c0 in enumerate((col_ka, col_kb)):
            wkt_scr[LANES * r:LANES * (r + 1), :] = w_in_ref[:, c0:c0 + LANES].T
        wv_scr[:, 0:LANES] = w_in_ref[:, col_va:col_va + LANES]
        wv_scr[:, LANES:2 * LANES] = w_in_ref[:, col_vb:col_vb + LANES]

    n_lat_chunks = seq // ROW_CHUNK
    if latent:
        n_past = cav_ref.shape[1]
        past_rows = pl.ds(seq, n_past)
        _store_kt_variants(ka_scr, n_lat_chunks, cakt_ref[0])
        _store_kt_variants(kb_scr, n_lat_chunks, cbkt_ref[0])
        _store_v_variants(va_scr, past_rows, cav_ref[0])
        _store_v_variants(vb_scr, past_rows, cbv_ref[0])

    pm = pm_ref[...]
    qn = qn_ref[...]
    knt = knt_ref[...]

    def proj(i, carry):
        rows = _chunk_rows(i)
        hh = ha_scr[rows, :]
        if latent:
            cos = cos_ref[rows, :]
            sin = sin_ref[rows, :]
            rot = lambda a: _rope(a, cos, sin)
            rot_t = lambda a: _rope_t(a, cost_ref[i], sint_ref[i])
        else:
            rot = rot_t = lambda a: a

        acc = _dot(hh, w_in_ref[:, 0:512])
        for j in range(4):
            a = acc[:, LANES * j:LANES * (j + 1)]
            ms = _dot((a * a).astype(BF16), pm)
            a = rot(a * lax.rsqrt(ms + EPS) * qn)
            qa_scr[rows, LANES * j:LANES * (j + 1)] = (a * Q_SCALE).astype(BF16)
        acc = _dot(hh, w_in_ref[:, 1280:1792])
        for j in range(4):
            a = rot(acc[:, LANES * j:LANES * (j + 1)])
            qb_scr[rows, LANES * j:LANES * (j + 1)] = (a * Q_SCALE).astype(BF16)
        g_scr[rows, 0:512] = _silu(_dot(hh, w_in_ref[:, 768:1280]))
        g_scr[rows, 512:1024] = _silu(_dot(hh, w_in_ref[:, 2048:2560]))
        v = _dot(hh, wv_scr[...])
        _store_v_variants(va_scr, rows, v[:, 0:LANES])
        _store_v_variants(vb_scr, rows, v[:, LANES:2 * LANES])

        kt = _dot_nt(wkt_scr[0:2 * LANES, :], hh)
        kat = kt[0:LANES, :]
        ms = _dot(pm, (kat * kat).astype(BF16))
        kat = kat * lax.rsqrt(ms + EPS) * knt
        kbt = kt[LANES:2 * LANES, :]
        if not latent:
            vt = v.T
            nakt_ref[i] = kat
            nbkt_ref[i] = kbt
            navt_ref[i] = vt[0:LANES, :]
            nbvt_ref[i] = vt[LANES:2 * LANES, :]
        _store_kt_variants(ka_scr, i, rot_t(kat))
        _store_kt_variants(kb_scr, i, rot_t(kbt))
        return carry

    lax.fori_loop(0, n_rows // ROW_CHUNK, proj, 0, unroll=2)

    sinks = [sink_ref[h] * LOG2E for h in range(8)]
    ck = ROW_CHUNK
    bk = kb_scr.shape[-1]
    win = ROW_CHUNK + 2 * WINDOW
    n_items = 8

    def block_stages(i):
        rows = _chunk_rows(i)
        if latent:
            a_chunks = list(range(n_lat_chunks + n_past // ck))
            a_keys = pl.ds(0, seq + n_past)
            w0 = jnp.clip(i * (ck // bk) - WINDOW // bk, 0, (seq - win) // bk)
            win_rows = pl.ds(pl.multiple_of(w0 * bk, bk), win)
            dist = (lax.broadcasted_iota(jnp.int32, (ROW_CHUNK, ck), 1)
                    - lax.broadcasted_iota(jnp.int32, (ROW_CHUNK, ck), 0))
            for c in range(win // ck):
                off = w0 * bk + c * ck - i * ck
                bias_scr[c] = jnp.where(jnp.abs(dist + off) <= WINDOW, 0.0, NEG_INF).astype(F32)
            b_first = [w0 + c * (ck // bk) for c in range(win // ck)] + [seq // bk]
            n_biased = win // ck
            b_cols = win + n_past
        else:
            a_chunks = [i]
            a_keys = rows
            b_first = [i * (ck // bk)]
            n_biased = 0
            b_cols = ck
        a_cols = len(a_chunks) * ck

        def qk(t, slot):
            p, branch = divmod(t, 2)
            cols = slice(LANES * p, LANES * (p + 1))
            kvh = p // 2
            q = (qb_scr if branch else qa_scr)[rows, cols]
            for par in (0, 1):
                var = 2 * kvh + par
                if branch:
                    tiles = [jnp.concatenate([kb_scr[var, first + d] for d in range(ck // bk)], axis=1)
                             for first in b_first]
                else:
                    tiles = [ka_scr[var, chunk] for chunk in a_chunks]
                for c, kt in enumerate(tiles):
                    s = _dot(q, kt)
                    if branch and c < n_biased:
                        s = s + bias_scr[c]
                    s_scr[slot, par, :, c * ck:(c + 1) * ck] = s

        def softmax(t, slot):
            p, branch = divmod(t, 2)
            n_cols = b_cols if branch else a_cols
            rb = _softmax_rows(n_cols)
            for par in (0, 1):
                for r in range(ROW_CHUNK // rb):
                    sub = slice(r * rb, (r + 1) * rb)
                    s = s_scr[slot, par, sub, 0:n_cols]
                    m = jnp.max(s, axis=1, keepdims=True)
                    if branch:
                        sink = sinks[2 * p + par]
                        m = jnp.maximum(m, sink)
                        es_scr[slot, sub, HEAD_DIM * par:HEAD_DIM * (par + 1)] = jnp.broadcast_to(
                            jnp.exp2(sink - m), (rb, HEAD_DIM))
                    p_scr[slot, par, sub, 0:n_cols] = jnp.exp2((s - m).astype(BF16))

        def pv(t, slot):
            p, branch = divmod(t, 2)
            kvh = p // 2
            v_scr = vb_scr if branch else va_scr
            accs = []
            for par in (0, 1):
                var = 2 * kvh + par
                if latent and branch:
                    n_loc = win
                    accs.append(_dot(p_scr[slot, par, :, 0:n_loc], v_scr[var, win_rows, :])
                                + _dot(p_scr[slot, par, :, n_loc:b_cols], v_scr[var, past_rows, :]))
                else:
                    accs.append(_dot(p_scr[slot, par, :, 0:a_cols], v_scr[var, a_keys, :]))
            lo = _lane_iota(ROW_CHUNK) < HEAD_DIM
            denom = pltpu.roll(jnp.where(lo, accs[1], accs[0]), HEAD_DIM, 1)
            if branch:
                denom = denom + es_scr[slot]
            o = jnp.where(lo, accs[0], accs[1]) / denom
            ocols = slice(512 * branch + LANES * p, 512 * branch + LANES * (p + 1))
            ha_scr[rows, ocols] = (o * g_scr[rows, ocols]).astype(BF16)

        return qk, softmax, pv

    _attend_blocks(block_stages, n_rows // ROW_CHUNK, n_items, unrolled=not latent)

    if not latent:
        weights.load_out_proj()
    _out_proj_norm(x_ref, mod_ref, mod_row, ha_scr, w_out_ref, lng_ref, lnb_ref, y_ref, n_rows, alpha)
    if not latent:
        weights.finish()


def _odd_kernel(latent, n_rows, seq, alpha, lam_init, *refs):
    if latent:
        (x_ref, mod_ref, w_in_ref, w_out_ref, lq1_ref, lk1_ref, lq2_ref, lk2_ref, sub_ref, lng_ref, lnb_ref,
         cos_ref, sin_ref, cck_ref, ccv_ref,
         y_ref,
         ha_scr, q_scr, k_scr, v_scr, g_scr, s_scr, p_scr) = refs
    else:
        (x_ref, mod_ref, w_in_hbm, w_out_hbm, lq1_ref, lk1_ref, lq2_ref, lk2_ref, sub_ref, lng_ref, lnb_ref,
         y_ref, nck_hbm, ncv_hbm, w_in_bf_hbm, w_out_bf_hbm,
         ha_scr, q_scr, k_scr, v_scr, g_scr, s_scr, p_scr, kv_stage, kv_sems,
         w_in_ref, w_out_ref, w_stage, w_sems, w_out_sems) = refs

    step = pl.program_id(0)
    if not latent:
        weights = _ContextWeights(step, (w_in_hbm, w_in_ref, w_in_bf_hbm), (w_out_hbm, w_out_ref, w_out_bf_hbm),
                                  w_stage, w_sems, w_out_sems)
        weights.load_in_proj()
    mod_row = step + 1 if latent else 0
    _modulate(x_ref, mod_ref, mod_row, ha_scr, n_rows)

    n_heads = D_MODEL // LANES
    n_blocks = n_rows // ROW_CHUNK
    lo = _lane_iota(ROW_CHUNK) < HEAD_DIM

    def kv_out_copies(blk):
        elem = step * n_blocks + blk
        return [pltpu.make_async_copy(kv_stage.at[blk, t, :, pl.ds(LANES * h, LANES)],
                                      out.at[elem, 0, :, h, :], kv_sems.at[blk, t])
                for t, out in enumerate((nck_hbm, ncv_hbm)) for h in range(n_heads)]

    def store_k(rows, h, a):
        cols = slice(LANES * h, LANES * (h + 1))
        zero = jnp.zeros_like(a)
        k_scr[0, rows, cols] = jnp.where(lo, a, zero).astype(BF16)
        k_scr[1, rows, cols] = jnp.where(lo, zero, a).astype(BF16)

    if latent:
        n_past = cck_ref.shape[2]
        past = pl.ds(seq, n_past)
        for h in range(n_heads):
            store_k(past, h, cck_ref[0, 0, :, h, :])
            v_scr[past, LANES * h:LANES * (h + 1)] = ccv_ref[0, 0, :, h, :].astype(BF16)

    def proj(i, carry):
        rows = _chunk_rows(i)
        hh = ha_scr[rows, :]
        if latent:
            cos = cos_ref[rows, :]
            sin = sin_ref[rows, :]
            rot = lambda a: _rope(a, cos, sin)
        else:
            rot = lambda a: a
        for half in range(2):
            acc = _dot(hh, w_in_ref[:, 512 * half:512 * (half + 1)])
            for j in range(4):
                a = rot(acc[:, LANES * j:LANES * (j + 1)])
                cols = slice(512 * half + LANES * j, 512 * half + LANES * (j + 1))
                q_scr[rows, cols] = (a * Q_SCALE).astype(BF16)
        for half in range(2):
            acc = _dot(hh, w_in_ref[:, 1024 + 512 * half:1024 + 512 * (half + 1)])
            if not latent:
                kv_stage[i, 0, :, 512 * half:512 * (half + 1)] = acc
            for j in range(4):
                store_k(rows, 4 * half + j, rot(acc[:, LANES * j:LANES * (j + 1)]))
        for half in range(2):
            acc = _dot(hh, w_in_ref[:, 2048 + 512 * half:2048 + 512 * (half + 1)])
            if not latent:
                kv_stage[i, 1, :, 512 * half:512 * (half + 1)] = acc
            v_scr[rows, 512 * half:512 * (half + 1)] = acc.astype(BF16)
        if not latent:
            for copy in kv_out_copies(i):
                copy.start()
        for half in range(2):
            acc = _dot(hh, w_in_ref[:, 3072 + 512 * half:3072 + 512 * (half + 1)])
            g_scr[rows, 512 * half:512 * (half + 1)] = _silu(acc)
        return carry

    if latent:
        lax.fori_loop(0, n_blocks, proj, 0, unroll=2)
    else:
        for blk in range(n_blocks):
            proj(blk, 0)

    lam = (jnp.exp(jnp.sum(lq1_ref[...] * lk1_ref[...], axis=1, keepdims=True))
           - jnp.exp(jnp.sum(lq2_ref[...] * lk2_ref[...], axis=1, keepdims=True)) + lam_init)
    sub = sub_ref[...] * (1.0 - lam_init)
    n_keys = seq + n_past if latent else ROW_CHUNK
    rb = _softmax_rows(n_keys)
    ones = jnp.ones((n_keys, LANES), BF16)

    def block_stages(i):
        rows = _chunk_rows(i)
        keys = pl.ds(0, n_keys) if latent else rows

        def qk(h, slot):
            cols = slice(LANES * h, LANES * (h + 1))
            q = q_scr[rows, cols]
            for m in (0, 1):
                s_scr[slot, m] = _dot_nt(q, k_scr[m, keys, cols])

        def softmax(h, slot):
            for m in (0, 1):
                for r in range(ROW_CHUNK // rb):
                    sub_rows = slice(r * rb, (r + 1) * rb)
                    s = s_scr[slot, m, sub_rows, :]
                    top = jnp.max(s, axis=1, keepdims=True)
                    p_scr[slot, m, sub_rows, :] = jnp.exp2((s - top).astype(BF16))

        def pv(h, slot):
            cols = slice(LANES * h, LANES * (h + 1))
            v_ext = jnp.concatenate([v_scr[keys, cols], ones], axis=1)
            maps = []
            for m in (0, 1):
                acc = _dot(p_scr[slot, m], v_ext)
                maps.append(acc[:, 0:LANES] / acc[:, LANES:2 * LANES])
            o = maps[0] - lam * maps[1]
            ms = jnp.mean(o * o, axis=1, keepdims=True)
            o = o * lax.rsqrt(ms + EPS) * sub
            ha_scr[rows, cols] = (o * g_scr[rows, cols]).astype(BF16)

        return qk, softmax, pv

    _attend_blocks(block_stages, n_blocks, n_heads, unrolled=not latent)

    if not latent:
        weights.load_out_proj()
    _out_proj_norm(x_ref, mod_ref, mod_row, ha_scr, w_out_ref, lng_ref, lnb_ref, y_ref, n_rows, alpha)

    if not latent:
        for blk in range(n_blocks):
            for copy in kv_out_copies(blk):
                copy.wait()
        weights.finish()


def _mod_kernel(n_cond, cvb_ref, w_ref, b_ref, o_ref, sb_scr):
    @pl.when((pl.program_id(0) == 0) & (pl.program_id(1) == 0))
    def _():
        sb_scr[...] = _silu(cvb_ref[...])

    n_out = w_ref.shape[2]
    sublanes = 8

    def body(kb, accs):
        rows = pl.ds(pl.multiple_of(kb * sublanes, sublanes), sublanes)
        w = w_ref[0, rows, :]
        return tuple(acc + w * jnp.tile(sb_scr[r, rows, :], (1, n_out // LANES)) for r, acc in enumerate(accs))

    zero = jnp.zeros((sublanes, n_out), F32)
    accs = lax.fori_loop(0, D_MODEL // sublanes, body, (zero,) * n_cond, unroll=8)
    out_rows = [jnp.sum(acc, axis=0, keepdims=True) + b_ref[0] for acc in accs]
    o_ref[0] = jnp.concatenate(out_rows + [jnp.zeros((8 - n_cond, n_out), F32)], axis=0)


def _full(shape, **kw):
    zeros = (0,) * len(shape)
    return pl.BlockSpec(shape, lambda i: zeros, **kw)


def _weight_specs(latent, w_in, w_out):
    if latent:
        single = pl.Buffered(1)
        return [_full(w_in.shape, pipeline_mode=single), _full(w_out.shape, pipeline_mode=single)]
    return [pl.BlockSpec(memory_space=pl.ANY), pl.BlockSpec(memory_space=pl.ANY)]


def _weight_scratch(w_in, w_out):
    assert w_in.shape[0] % W_SLAB_ROWS == 0 and w_out.shape[0] % W_SLAB_ROWS == 0
    n_slabs = max(w_in.shape[0], w_out.shape[0]) // W_SLAB_ROWS
    return [pltpu.VMEM(w_in.shape, BF16), pltpu.VMEM(w_out.shape, BF16),
            pltpu.VMEM((n_slabs, W_SLAB_ROWS, max(w_in.shape[1], w_out.shape[1])), F32),
            pltpu.SemaphoreType.DMA((n_slabs,)), pltpu.SemaphoreType.DMA((2,))]


def _rope_tables(seq):
    t = np.arange(seq)
    n_freq = HEAD_DIM // 4
    freqs = ROPE_THETA ** (-np.arange(n_freq, dtype=np.float64) / n_freq)
    ang_row = (t // GRID_W)[:, None] * freqs
    ang_col = (t % GRID_W)[:, None] * freqs
    ang = np.concatenate([ang_row, ang_row, ang_col, ang_col], axis=1)
    sign = np.concatenate([-np.ones(n_freq), np.ones(n_freq)] * 2)[None, :]
    cos = np.tile(np.cos(ang), (1, 2)).astype(np.float32)
    sin = np.tile(np.sin(ang) * sign, (1, 2)).astype(np.float32)
    chunked_t = lambda a: a.reshape(seq // ROW_CHUNK, ROW_CHUNK, LANES).transpose(0, 2, 1)
    return jnp.asarray(cos), jnp.asarray(sin), jnp.asarray(chunked_t(cos)), jnp.asarray(chunked_t(sin))


def _head_mean_matrix():
    idx = np.arange(LANES) // HEAD_DIM
    return jnp.asarray((idx[:, None] == idx[None, :]).astype(np.float32) / HEAD_DIM, dtype=BF16)


def _modulation(c, c_ctx, w_mod, b_mod):
    depth = w_mod.shape[0]
    cv = jnp.concatenate([c_ctx[None, :], c], axis=0)
    n_cond = cv.shape[0]
    cvb = jnp.broadcast_to(cv[:, :, None], (n_cond, D_MODEL, LANES))
    n_blk = 3 * D_MODEL // 1024
    return pl.pallas_call(
        functools.partial(_mod_kernel, n_cond),
        grid=(depth, n_blk),
        in_specs=[pl.BlockSpec((n_cond, D_MODEL, LANES), lambda l, n: (0, 0, 0)),
                  pl.BlockSpec((1, D_MODEL, 1024), lambda l, n: (l, 0, n)),
                  pl.BlockSpec((1, 1, 1024), lambda l, n: (l, 0, n))],
        out_specs=pl.BlockSpec((1, 8, 1024), lambda l, n: (l, 0, n)),
        out_shape=jax.ShapeDtypeStruct((depth, 8, 3 * D_MODEL), F32),
        scratch_shapes=[pltpu.VMEM((n_cond, D_MODEL, LANES), F32)],
        compiler_params=pltpu.CompilerParams(dimension_semantics=("arbitrary", "arbitrary")),
        name="adaln_modulation",
    )(cvb, w_mod, b_mod.reshape(depth, 1, 3 * D_MODEL))


def _even_layer(x, mod, layer, w_in, w_out, q_norm, k_norm, sink, ln_g, ln_b, latent, seq, n_rows, alpha, extras=()):
    total = x.shape[0]
    grid = (total // n_rows,)
    single = pl.Buffered(1)
    qn = jnp.tile(q_norm, 2)[None, :]
    knt = jnp.broadcast_to(jnp.tile(k_norm, 2)[:, None], (LANES, ROW_CHUNK))

    row_blk = lambda width: pl.BlockSpec((n_rows, width), lambda i: (i, 0))
    in_specs = [row_blk(D_MODEL),
                pl.BlockSpec((1, 8, 3 * D_MODEL), lambda i: (layer, 0, 0)),
                *_weight_specs(latent, w_in, w_out),
                _full((1, LANES)), _full((LANES, ROW_CHUNK)),
                pl.BlockSpec(memory_space=pltpu.SMEM),
                pl.BlockSpec((1, 1, D_MODEL), lambda i: (layer, 0, 0)),
                pl.BlockSpec((1, 1, D_MODEL), lambda i: (layer, 0, 0)),
                _full((LANES, LANES))]
    args = [x, mod, w_in, w_out, qn, knt, sink, ln_g, ln_b, _head_mean_matrix()]
    y_shape = jax.ShapeDtypeStruct((total, D_MODEL), F32)
    n_blocks = n_rows // ROW_CHUNK
    if latent:
        cos, sin, cos_t, sin_t, cakt, cav, cbkt, cbv = extras
        n_past = cav.shape[1]
        in_specs += [_full(cos.shape, pipeline_mode=single), _full(sin.shape, pipeline_mode=single),
                     _full(cos_t.shape, pipeline_mode=single), _full(sin_t.shape, pipeline_mode=single)]
        in_specs += [pl.BlockSpec((1, LANES, n_past), lambda i: (i, 0, 0)),
                     pl.BlockSpec((1, n_past, LANES), lambda i: (i, 0, 0))] * 2
        args += [cos, sin, cos_t, sin_t, cakt, cav, cbkt, cbv]
        out_specs = row_blk(D_MODEL)
        out_shape = y_shape
        n_keys = seq + n_past
    else:
        kv_blk = pl.BlockSpec((n_blocks, LANES, ROW_CHUNK), lambda i: (i, 0, 0))
        hbm = pl.BlockSpec(memory_space=pl.ANY)
        out_specs = [row_blk(D_MODEL)] + [kv_blk] * 4 + [hbm, hbm]
        out_shape = ([y_shape] + [jax.ShapeDtypeStruct((total // seq, LANES, seq), F32)] * 4
                     + [jax.ShapeDtypeStruct(w_in.shape, BF16), jax.ShapeDtypeStruct(w_out.shape, BF16)])
        n_keys = n_rows
    n_kchunks = n_keys // ROW_CHUNK
    n_cols = n_keys if latent else ROW_CHUNK
    scratch = [pltpu.VMEM((n_rows, D_MODEL), BF16),
               pltpu.VMEM((n_rows, 512), BF16), pltpu.VMEM((n_rows, 512), BF16),
               pltpu.VMEM((4, n_kchunks, LANES, ROW_CHUNK), BF16), pltpu.VMEM((4, n_keys, LANES), BF16),
               pltpu.VMEM((4, n_keys // WINDOW, LANES, WINDOW), BF16), pltpu.VMEM((4, n_keys, LANES), BF16),
               pltpu.VMEM((n_rows, D_MODEL), F32),
               pltpu.VMEM((2, 2, ROW_CHUNK, n_cols), F32),
               pltpu.VMEM((2, 2, ROW_CHUNK, n_cols), BF16),
               pltpu.VMEM((2, ROW_CHUNK, LANES), F32),
               pltpu.VMEM((2 * LANES, D_MODEL), BF16),
               pltpu.VMEM((D_MODEL, 2 * LANES), BF16)]
    if latent:
        scratch.append(pltpu.VMEM((1 + 2 * WINDOW // ROW_CHUNK, ROW_CHUNK, ROW_CHUNK), F32))
    else:
        scratch += _weight_scratch(w_in, w_out)
    return pl.pallas_call(
        functools.partial(_even_kernel, latent, n_rows, seq, alpha),
        grid=grid, in_specs=in_specs, out_specs=out_specs, out_shape=out_shape,
        scratch_shapes=scratch,
        compiler_params=pltpu.CompilerParams(dimension_semantics=("arbitrary",), vmem_limit_bytes=VMEM_LIMIT),
        name="even_layer_latent" if latent else "even_layer_context",
    )(*args)


def _odd_layer(x, mod, layer, w_in, w_out, lams, sub, ln_g, ln_b, latent, seq, n_rows, alpha, lam_init, extras=()):
    total = x.shape[0]
    grid = (total // n_rows,)
    row_blk = lambda width: pl.BlockSpec((n_rows, width), lambda i: (i, 0))
    single = pl.Buffered(1)
    in_specs = [row_blk(D_MODEL),
                pl.BlockSpec((1, 8, 3 * D_MODEL), lambda i: (layer, 0, 0)),
                *_weight_specs(latent, w_in, w_out),
                _full((1, HEAD_DIM)), _full((1, HEAD_DIM)), _full((1, HEAD_DIM)), _full((1, HEAD_DIM)),
                _full((1, LANES)),
                pl.BlockSpec((1, 1, D_MODEL), lambda i: (layer, 0, 0)),
                pl.BlockSpec((1, 1, D_MODEL), lambda i: (layer, 0, 0))]
    args = [x, mod, w_in, w_out, *lams, sub, ln_g, ln_b]
    y_shape = jax.ShapeDtypeStruct((total, D_MODEL), F32)
    n_heads = D_MODEL // LANES
    n_blocks = n_rows // ROW_CHUNK
    if latent:
        cos, sin, cck, ccv = extras
        n_past = cck.shape[2]
        in_specs += [_full(cos.shape, pipeline_mode=single), _full(sin.shape, pipeline_mode=single)]
        in_specs += [pl.BlockSpec((1, 1, n_past, n_heads, LANES), lambda i: (i, layer // 2, 0, 0, 0))] * 2
        args += [cos, sin, cck, ccv]
        out_specs = row_blk(D_MODEL)
        out_shape = y_shape
        n_keys = seq + n_past
    else:
        hbm = pl.BlockSpec(memory_space=pl.ANY)
        out_specs = [row_blk(D_MODEL), hbm, hbm, hbm, hbm]
        out_shape = ([y_shape] + [jax.ShapeDtypeStruct((total // seq, 1, seq, n_heads, LANES), F32)] * 2
                     + [jax.ShapeDtypeStruct(w_in.shape, BF16), jax.ShapeDtypeStruct(w_out.shape, BF16)])
        n_keys = n_rows
    n_cols = n_keys if latent else ROW_CHUNK
    scratch = [pltpu.VMEM((n_rows, D_MODEL), BF16),
               pltpu.VMEM((n_rows, D_MODEL), BF16),
               pltpu.VMEM((2, n_keys, D_MODEL), BF16),
               pltpu.VMEM((n_keys, D_MODEL), BF16),
               pltpu.VMEM((n_rows, D_MODEL), F32),
               pltpu.VMEM((2, 2, ROW_CHUNK, n_cols), F32),
               pltpu.VMEM((2, 2, ROW_CHUNK, n_cols), BF16)]
    if not latent:
        scratch += [pltpu.VMEM((n_blocks, 2, ROW_CHUNK, D_MODEL), F32),
                    pltpu.SemaphoreType.DMA((n_blocks, 2))]
        scratch += _weight_scratch(w_in, w_out)
    return pl.pallas_call(
        functools.partial(_odd_kernel, latent, n_rows, seq, alpha, lam_init),
        grid=grid, in_specs=in_specs, out_specs=out_specs, out_shape=out_shape,
        scratch_shapes=scratch,
        compiler_params=pltpu.CompilerParams(dimension_semantics=("arbitrary",), vmem_limit_bytes=VMEM_LIMIT),
        name="odd_layer_latent" if latent else "odd_layer_context",
    )(*args)


def kernel(x_prompt, x_sample, cache_a_k, cache_a_v, cache_b_k, cache_b_v, cache_c_k, cache_c_v, c, c_ctx,
           w_mod, b_mod, ln_g, ln_b, w_in_even, w_out_even, q_norm_a, k_norm_a, sink_b, w_in_odd, w_out_odd,
           lambda_q1, lambda_k1, lambda_q2, lambda_k2, subln_c):
    depth = w_mod.shape[0]
    batch, seq, _ = x_prompt.shape
    dec_batch, dec_seq, _ = x_sample.shape
    n_past = cache_a_k.shape[2]
    alpha = (2 * depth) ** 0.25
    assert seq == ROW_CHUNK and n_past % ROW_CHUNK == 0 and dec_seq % ROW_CHUNK == 0

    mod = _modulation(c, c_ctx, w_mod, b_mod)
    ln_g3 = ln_g.reshape(depth, 1, D_MODEL)
    ln_b3 = ln_b.reshape(depth, 1, D_MODEL)
    cos, sin, cos_t, sin_t = _rope_tables(dec_seq)

    bf16_weights = {}

    def run(x, latent, n_batch, s, rows_even, rows_odd):
        kv = {"a_k": [], "a_v": [], "b_k": [], "b_v": [], "c_k": [], "c_v": []}
        for l in range(depth):
            if l % 2 == 0:
                e = l // 2
                extras = ()
                if latent:
                    k_t = lambda t: t[:, e].transpose(0, 2, 3, 1).reshape(n_batch, LANES, n_past)
                    v_n = lambda t: t[:, e].reshape(n_batch, n_past, LANES)
                    extras = (cos, sin, cos_t, sin_t,
                              k_t(cache_a_k), v_n(cache_a_v), k_t(cache_b_k), v_n(cache_b_v))
                w_in, w_out = bf16_weights[l] if latent else (w_in_even[e], w_out_even[e])
                res = _even_layer(x, mod, l, w_in, w_out, q_norm_a[e], k_norm_a[e],
                                  sink_b[e], ln_g3, ln_b3, latent, s, rows_even, alpha, extras)
                if latent:
                    x = res
                else:
                    x = res[0]
                    bf16_weights[l] = res[5:7]
                    for name, t in zip(("a_k", "a_v", "b_k", "b_v"), res[1:5]):
                        kv[name].append(t.reshape(n_batch, 2, HEAD_DIM, s).transpose(0, 3, 1, 2))
            else:
                o = l // 2
                lam_init = 0.8 - 0.6 * math.exp(-0.3 * l)
                extras = (cos, sin, cache_c_k, cache_c_v) if latent else ()
                lams = [t[o][None, :] for t in (lambda_q1, lambda_k1, lambda_q2, lambda_k2)]
                w_in, w_out = bf16_weights[l] if latent else (w_in_odd[o], w_out_odd[o])
                res = _odd_layer(x, mod, l, w_in, w_out, lams,
                                 subln_c[o][None, :], ln_g3, ln_b3, latent, s, rows_odd, alpha, lam_init, extras)
                if latent:
                    x = res
                else:
                    x = res[0]
                    bf16_weights[l] = res[3:5]
                    kv["c_k"].append(res[1][:, 0])
                    kv["c_v"].append(res[2][:, 0])
        return x, kv

    y_ctx, kv = run(x_prompt.reshape(batch * seq, D_MODEL), False, batch, seq, 1024, 512)
    y_lat, _ = run(x_sample.reshape(dec_batch * dec_seq, D_MODEL), True, dec_batch, dec_seq, dec_seq, dec_seq)

    stack = lambda name: jnp.stack(kv[name], axis=1)
    return (y_ctx.reshape(batch, seq, D_MODEL), y_lat.reshape(dec_batch, dec_seq, D_MODEL),
            stack("a_k"), stack("a_v"), stack("b_k"), stack("b_v"), stack("c_k"), stack("c_v"))
```

```python
import functools
import math

import jax
import jax.numpy as jnp
import numpy as np
from jax import lax
from jax.experimental import pallas as pl
from jax.experimental.pallas import tpu as pltpu

F32 = jnp.float32
BF16 = jnp.bfloat16

D_MODEL = 1024
HEAD_DIM = 64
GRID_W = 64
WINDOW = 128
ROPE_THETA = 10000.0
EPS = 1e-6
NEG_INF = -1e30
LOG2E = 1.4426950408889634
Q_SCALE = HEAD_DIM ** -0.5 * LOG2E
LANES = 128
ROW_CHUNK = 256
SOFTMAX_VREGS = 40
VMEM_LIMIT = 60000 * 1024
W_SLAB_ROWS = 128


def _silu(x):
    return x / (1.0 + jnp.exp(-x))


def _dot(a, b):
    return jnp.dot(a, b, preferred_element_type=F32)


def _dot_nt(a, b):
    return lax.dot_general(a, b, (((1,), (1,)), ((), ())), preferred_element_type=F32)


def _lane_iota(rows):
    return lax.broadcasted_iota(jnp.int32, (rows, LANES), 1)


def _chunk_rows(i):
    if isinstance(i, int):
        return pl.ds(i * ROW_CHUNK, ROW_CHUNK)
    return pl.ds(pl.multiple_of(i * ROW_CHUNK, ROW_CHUNK), ROW_CHUNK)


def _softmax_rows(n_cols):
    rows = 8
    while rows * 2 * n_cols <= SOFTMAX_VREGS * 1024 and rows * 2 <= ROW_CHUNK:
        rows *= 2
    return rows


def _rope(a, cos, sin_signed):
    lane = _lane_iota(a.shape[0])
    fwd = pltpu.roll(a, LANES - 16, 1)
    bwd = pltpu.roll(a, 16, 1)
    partner = jnp.where((lane & 16) == 0, fwd, bwd)
    return a * cos + partner * sin_signed


def _rope_t(a, cos_t, sin_t):
    blocks = [a[16 * b:16 * (b + 1), :] for b in range(a.shape[0] // 16)]
    partner = jnp.concatenate([blocks[b ^ 1] for b in range(len(blocks))], axis=0)
    return a * cos_t + partner * sin_t


def _store_kt_variants(scr, chunk, kt):
    width = scr.shape[-1]
    per_block = kt.shape[1] // width
    zero = jnp.zeros((HEAD_DIM, kt.shape[1]), F32)
    for j in range(2):
        kj = kt[HEAD_DIM * j:HEAD_DIM * (j + 1), :]
        for par, full in enumerate((jnp.concatenate([kj, zero], axis=0), jnp.concatenate([zero, kj], axis=0))):
            full = full.astype(BF16)
            for c in range(per_block):
                scr[2 * j + par, chunk * per_block + c] = full[:, width * c:width * (c + 1)]


def _store_v_variants(scr, rows, a):
    lane = _lane_iota(a.shape[0])
    lo = lane < HEAD_DIM
    swapped = pltpu.roll(a, HEAD_DIM, 1)
    one = jnp.ones_like(a)
    scr[0, rows, :] = jnp.where(lo, a, one).astype(BF16)
    scr[1, rows, :] = jnp.where(lo, one, swapped).astype(BF16)
    scr[2, rows, :] = jnp.where(lo, swapped, one).astype(BF16)
    scr[3, rows, :] = jnp.where(lo, one, a).astype(BF16)


def _layer_norm_rows(z, g, b):
    mu = jnp.mean(z, axis=-1, keepdims=True)
    zc = z - mu
    var = jnp.mean(zc * zc, axis=-1, keepdims=True)
    return zc * lax.rsqrt(var + EPS) * g + b


def _modulate(x_ref, mod_ref, mod_row, h_scr, n_rows):
    shift = mod_ref[0, pl.ds(mod_row, 1), 0:D_MODEL]
    scale = mod_ref[0, pl.ds(mod_row, 1), D_MODEL:2 * D_MODEL]

    def body(i, carry):
        rows = _chunk_rows(i)
        h_scr[rows, :] = (x_ref[rows, :] * (1.0 + scale) + shift).astype(BF16)
        return carry

    lax.fori_loop(0, n_rows // ROW_CHUNK, body, 0)


def _out_proj_norm(x_ref, mod_ref, mod_row, attn_scr, w_out_ref, lng_ref, lnb_ref, y_ref, n_rows, alpha):
    gate = mod_ref[0, pl.ds(mod_row, 1), 2 * D_MODEL:3 * D_MODEL]
    g = lng_ref[0]
    b = lnb_ref[0]

    def body(i, carry):
        rows = _chunk_rows(i)
        out = _dot(attn_scr[rows, :], w_out_ref[...])
        z = alpha * x_ref[rows, :] + gate * out
        y_ref[rows, :] = _layer_norm_rows(z, g, b)
        return carry

    lax.fori_loop(0, n_rows // ROW_CHUNK, body, 0, unroll=True)


class _ContextWeights:
    def __init__(self, step, w_in, w_out, stage, sems, out_sems):
        self.step, self.w_in, self.w_out, self.stage, self.sems = step, w_in, w_out, stage, sems
        self.out_copies = [pltpu.make_async_copy(w[1], w[2], out_sems.at[n]) for n, w in enumerate((w_in, w_out))]

    def _slab_copies(self, w_hbm):
        n_cols = w_hbm.shape[1]
        return [pltpu.make_async_copy(w_hbm.at[pl.ds(s * W_SLAB_ROWS, W_SLAB_ROWS), :],
                                      self.stage.at[s, :, pl.ds(0, n_cols)], self.sems.at[s])
                for s in range(w_hbm.shape[0] // W_SLAB_ROWS)]

    def _cast(self, w_hbm, w_scr):
        n_cols = w_hbm.shape[1]
        for s, copy in enumerate(self._slab_copies(w_hbm)):
            copy.wait()
            w_scr[pl.ds(s * W_SLAB_ROWS, W_SLAB_ROWS), :] = self.stage[s, :, 0:n_cols].astype(BF16)

    def load_in_proj(self):
        @pl.when(self.step == 0)
        def _():
            for copy in self._slab_copies(self.w_in[0]):
                copy.start()
            self._cast(self.w_in[0], self.w_in[1])
            self.out_copies[0].start()
            for copy in self._slab_copies(self.w_out[0]):
                copy.start()

    def load_out_proj(self):
        @pl.when(self.step == 0)
        def _():
            self._cast(self.w_out[0], self.w_out[1])
            self.out_copies[1].start()

    def finish(self):
        @pl.when(self.step == 0)
        def _():
            for copy in self.out_copies:
                copy.wait()


def _run_pipeline(n_items, stages):
    for u in range(n_items + len(stages) - 1):
        for k, stage in enumerate(stages):
            t = u - k
            if 0 <= t < n_items:
                stage(t, t % 2)


def _attend_blocks(block_stages, n_blocks, n_items, unrolled):
    assert n_items % 2 == 0
    if unrolled:
        per_block = [block_stages(i) for i in range(n_blocks)]
        stages = [lambda g, slot, k=k: per_block[g // n_items][k](g % n_items, slot) for k in range(3)]
        _run_pipeline(n_blocks * n_items, stages)
    else:
        def body(i, carry):
            _run_pipeline(n_items, block_stages(i))
            return carry

        lax.fori_loop(0, n_blocks, body, 0)


def _even_kernel(latent, n_rows, seq, alpha, *refs):
    if latent:
        (x_ref, mod_ref, w_in_ref, w_out_ref, qn_ref, knt_ref, sink_ref, lng_ref, lnb_ref, pm_ref,
         cos_ref, sin_ref, cost_ref, sint_ref, cakt_ref, cav_ref, cbkt_ref, cbv_ref,
         y_ref,
         ha_scr, qa_scr, qb_scr, ka_scr, va_scr, kb_scr, vb_scr, g_scr, s_scr, p_scr, es_scr, wkt_scr, wv_scr,
         bias_scr) = refs
    else:
        (x_ref, mod_ref, w_in_hbm, w_out_hbm, qn_ref, knt_ref, sink_ref, lng_ref, lnb_ref, pm_ref,
         y_ref, nakt_ref, navt_ref, nbkt_ref, nbvt_ref, w_in_bf_hbm, w_out_bf_hbm,
         ha_scr, qa_scr, qb_scr, ka_scr, va_scr, kb_scr, vb_scr, g_scr, s_scr, p_scr, es_scr, wkt_scr,
         wv_scr, w_in_ref, w_out_ref, w_stage, w_sems, w_out_sems) = refs

    step = pl.program_id(0)
    if not latent:
        weights = _ContextWeights(step, (w_in_hbm, w_in_ref, w_in_bf_hbm), (w_out_hbm, w_out_ref, w_out_bf_hbm),
                                  w_stage, w_sems, w_out_sems)
        weights.load_in_proj()
    mod_row = step + 1 if latent else 0
    _modulate(x_ref, mod_ref, mod_row, ha_scr, n_rows)

    col_ka, col_va, col_kb, col_vb = 512, 640, 1792, 1920

    @pl.when(step == 0)
    def _():
        for r, c0 in enumerate((col_ka, col_kb)):
            wkt_scr[LANES * r:LANES * (r + 1), :] = w_in_ref[:, c0:c0 + LANES].T
        wv_scr[:, 0:LANES] = w_in_ref[:, col_va:col_va + LANES]
        wv_scr[:, LANES:2 * LANES] = w_in_ref[:, col_vb:col_vb + LANES]

    n_lat_chunks = seq // ROW_CHUNK
    if latent:
        n_past = cav_ref.shape[1]
        past_rows = pl.ds(seq, n_past)
        _store_kt_variants(ka_scr, n_lat_chunks, cakt_ref[0])
        _store_kt_variants(kb_scr, n_lat_chunks, cbkt_ref[0])
        _store_v_variants(va_scr, past_rows, cav_ref[0])
        _store_v_variants(vb_scr, past_rows, cbv_ref[0])

    pm = pm_ref[...]
    qn = qn_ref[...]
    knt = knt_ref[...]

    def proj(i, carry):
        rows = _chunk_rows(i)
        hh = ha_scr[rows, :]
        if latent:
            cos = cos_ref[rows, :]
            sin = sin_ref[rows, :]
            rot = lambda a: _rope(a, cos, sin)
            rot_t = lambda a: _rope_t(a, cost_ref[i], sint_ref[i])
        else:
            rot = rot_t = lambda a: a

        acc = _dot(hh, w_in_ref[:, 0:512])
        for j in range(4):
            a = acc[:, LANES * j:LANES * (j + 1)]
            ms = _dot((a * a).astype(BF16), pm)
            a = rot(a * lax.rsqrt(ms + EPS) * qn)
            qa_scr[rows, LANES * j:LANES * (j + 1)] = (a * Q_SCALE).astype(BF16)
        acc = _dot(hh, w_in_ref[:, 1280:1792])
        for j in range(4):
            a = rot(acc[:, LANES * j:LANES * (j + 1)])
            qb_scr[rows, LANES * j:LANES * (j + 1)] = (a * Q_SCALE).astype(BF16)
        g_scr[rows, 0:512] = _silu(_dot(hh, w_in_ref[:, 768:1280]))
        g_scr[rows, 512:1024] = _silu(_dot(hh, w_in_ref[:, 2048:2560]))
        v = _dot(hh, wv_scr[...])
        _store_v_variants(va_scr, rows, v[:, 0:LANES])
        _store_v_variants(vb_scr, rows, v[:, LANES:2 * LANES])

        kt = _dot_nt(wkt_scr[0:2 * LANES, :], hh)
        kat = kt[0:LANES, :]
        ms = _dot(pm, (kat * kat).astype(BF16))
        kat = kat * lax.rsqrt(ms + EPS) * knt
        kbt = kt[LANES:2 * LANES, :]
        if not latent:
            vt = v.T
            nakt_ref[i] = kat
            nbkt_ref[i] = kbt
            navt_ref[i] = vt[0:LANES, :]
            nbvt_ref[i] = vt[LANES:2 * LANES, :]
        _store_kt_variants(ka_scr, i, rot_t(kat))
        _store_kt_variants(kb_scr, i, rot_t(kbt))
        return carry

    lax.fori_loop(0, n_rows // ROW_CHUNK, proj, 0, unroll=2)

    sinks = [sink_ref[h] * LOG2E for h in range(8)]
    ck = ROW_CHUNK
    bk = kb_scr.shape[-1]
    win = ROW_CHUNK + 2 * WINDOW
    n_items = 8

    def block_stages(i):
        rows = _chunk_rows(i)
        if latent:
            a_chunks = list(range(n_lat_chunks + n_past // ck))
            a_keys = pl.ds(0, seq + n_past)
            w0 = jnp.clip(i * (ck // bk) - WINDOW // bk, 0, (seq - win) // bk)
            win_rows = pl.ds(pl.multiple_of(w0 * bk, bk), win)
            dist = (lax.broadcasted_iota(jnp.int32, (ROW_CHUNK, ck), 1)
                    - lax.broadcasted_iota(jnp.int32, (ROW_CHUNK, ck), 0))
            for c in range(win // ck):
                off = w0 * bk + c * ck - i * ck
                bias_scr[c] = jnp.where(jnp.abs(dist + off) <= WINDOW, 0.0, NEG_INF).astype(F32)
            b_first = [w0 + c * (ck // bk) for c in range(win // ck)] + [seq // bk]
            n_biased = win // ck
            b_cols = win + n_past
        else:
            a_chunks = [i]
            a_keys = rows
            b_first = [i * (ck // bk)]
            n_biased = 0
            b_cols = ck
        a_cols = len(a_chunks) * ck

        def qk(t, slot):
            p, branch = divmod(t, 2)
            cols = slice(LANES * p, LANES * (p + 1))
            kvh = p // 2
            q = (qb_scr if branch else qa_scr)[rows, cols]
            for par in (0, 1):
                var = 2 * kvh + par
                if branch:
                    tiles = [jnp.concatenate([kb_scr[var, first + d] for d in range(ck // bk)], axis=1)
                             for first in b_first]
                else:
                    tiles = [ka_scr[var, chunk] for chunk in a_chunks]
                for c, kt in enumerate(tiles):
                    s = _dot(q, kt)
                    if branch and c < n_biased:
                        s = s + bias_scr[c]
                    s_scr[slot, par, :, c * ck:(c + 1) * ck] = s

        def softmax(t, slot):
            p, branch = divmod(t, 2)
            n_cols = b_cols if branch else a_cols
            rb = _softmax_rows(n_cols)
            for par in (0, 1):
                for r in range(ROW_CHUNK // rb):
                    sub = slice(r * rb, (r + 1) * rb)
                    s = s_scr[slot, par, sub, 0:n_cols]
                    m = jnp.max(s, axis=1, keepdims=True)
                    if branch:
                        sink = sinks[2 * p + par]
                        m = jnp.maximum(m, sink)
                        es_scr[slot, sub, HEAD_DIM * par:HEAD_DIM * (par + 1)] = jnp.broadcast_to(
                            jnp.exp2(sink - m), (rb, HEAD_DIM))
                    p_scr[slot, par, sub, 0:n_cols] = jnp.exp2((s - m).astype(BF16))

        def pv(t, slot):
            p, branch = divmod(t, 2)
            kvh = p // 2
            v_scr = vb_scr if branch else va_scr
            accs = []
            for par in (0, 1):
                var = 2 * kvh + par
                if latent and branch:
                    n_loc = win
                    accs.append(_dot(p_scr[slot, par, :, 0:n_loc], v_scr[var, win_rows, :])
                                + _dot(p_scr[slot, par, :, n_loc:b_cols], v_scr[var, past_rows, :]))
                else:
                    accs.append(_dot(p_scr[slot, par, :, 0:a_cols], v_scr[var, a_keys, :]))
            lo = _lane_iota(ROW_CHUNK) < HEAD_DIM
            denom = pltpu.roll(jnp.where(lo, accs[1], accs[0]), HEAD_DIM, 1)
            if branch:
                denom = denom + es_scr[slot]
            o = jnp.where(lo, accs[0], accs[1]) / denom
            ocols = slice(512 * branch + LANES * p, 512 * branch + LANES * (p + 1))
            ha_scr[rows, ocols] = (o * g_scr[rows, ocols]).astype(BF16)

        return qk, softmax, pv

    _attend_blocks(block_stages, n_rows // ROW_CHUNK, n_items, unrolled=not latent)

    if not latent:
        weights.load_out_proj()
    _out_proj_norm(x_ref, mod_ref, mod_row, ha_scr, w_out_ref, lng_ref, lnb_ref, y_ref, n_rows, alpha)
    if not latent:
        weights.finish()


def _odd_kernel(latent, n_rows, seq, alpha, lam_init, *refs):
    if latent:
        (x_ref, mod_ref, w_in_ref, w_out_ref, lq1_ref, lk1_ref, lq2_ref, lk2_ref, sub_ref, lng_ref, lnb_ref,
         cos_ref, sin_ref, cck_ref, ccv_ref,
         y_ref,
         ha_scr, q_scr, k_scr, v_scr, g_scr, s_scr, p_scr) = refs
    else:
        (x_ref, mod_ref, w_in_hbm, w_out_hbm, lq1_ref, lk1_ref, lq2_ref, lk2_ref, sub_ref, lng_ref, lnb_ref,
         y_ref, nck_hbm, ncv_hbm, w_in_bf_hbm, w_out_bf_hbm,
         ha_scr, q_scr, k_scr, v_scr, g_scr, s_scr, p_scr, kv_stage, kv_sems,
         w_in_ref, w_out_ref, w_stage, w_sems, w_out_sems) = refs

    step = pl.program_id(0)
    if not latent:
        weights = _ContextWeights(step, (w_in_hbm, w_in_ref, w_in_bf_hbm), (w_out_hbm, w_out_ref, w_out_bf_hbm),
                                  w_stage, w_sems, w_out_sems)
        weights.load_in_proj()
    mod_row = step + 1 if latent else 0
    _modulate(x_ref, mod_ref, mod_row, ha_scr, n_rows)

    n_heads = D_MODEL // LANES
    n_blocks = n_rows // ROW_CHUNK
    lo = _lane_iota(ROW_CHUNK) < HEAD_DIM

    def kv_out_copies(blk):
        elem = step * n_blocks + blk
        return [pltpu.make_async_copy(kv_stage.at[blk, t, :, pl.ds(LANES * h, LANES)],
                                      out.at[elem, 0, :, h, :], kv_sems.at[blk, t])
                for t, out in enumerate((nck_hbm, ncv_hbm)) for h in range(n_heads)]

    def store_k(rows, h, a):
        cols = slice(LANES * h, LANES * (h + 1))
        zero = jnp.zeros_like(a)
        k_scr[0, rows, cols] = jnp.where(lo, a, zero).astype(BF16)
        k_scr[1, rows, cols] = jnp.where(lo, zero, a).astype(BF16)

    if latent:
        n_past = cck_ref.shape[2]
        past = pl.ds(seq, n_past)
        for h in range(n_heads):
            store_k(past, h, cck_ref[0, 0, :, h, :])
            v_scr[past, LANES * h:LANES * (h + 1)] = ccv_ref[0, 0, :, h, :].astype(BF16)

    def proj(i, carry):
        rows = _chunk_rows(i)
        hh = ha_scr[rows, :]
        if latent:
            cos = cos_ref[rows, :]
            sin = sin_ref[rows, :]
            rot = lambda a: _rope(a, cos, sin)
        else:
            rot = lambda a: a
        for half in range(2):
            acc = _dot(hh, w_in_ref[:, 512 * half:512 * (half + 1)])
            for j in range(4):
                a = rot(acc[:, LANES * j:LANES * (j + 1)])
                cols = slice(512 * half + LANES * j, 512 * half + LANES * (j + 1))
                q_scr[rows, cols] = (a * Q_SCALE).astype(BF16)
        for half in range(2):
            acc = _dot(hh, w_in_ref[:, 1024 + 512 * half:1024 + 512 * (half + 1)])
            if not latent:
                kv_stage[i, 0, :, 512 * half:512 * (half + 1)] = acc
            for j in range(4):
                store_k(rows, 4 * half + j, rot(acc[:, LANES * j:LANES * (j + 1)]))
        for half in range(2):
            acc = _dot(hh, w_in_ref[:, 2048 + 512 * half:2048 + 512 * (half + 1)])
            if not latent:
                kv_stage[i, 1, :, 512 * half:512 * (half + 1)] = acc
            v_scr[rows, 512 * half:512 * (half + 1)] = acc.astype(BF16)
        if not latent:
            for copy in kv_out_copies(i):
                copy.start()
        for half in range(2):
            acc = _dot(hh, w_in_ref[:, 3072 + 512 * half:3072 + 512 * (half + 1)])
            g_scr[rows, 512 * half:512 * (half + 1)] = _silu(acc)
        return carry

    if latent:
        lax.fori_loop(0, n_blocks, proj, 0, unroll=2)
    else:
        for blk in range(n_blocks):
            proj(blk, 0)

    lam = (jnp.exp(jnp.sum(lq1_ref[...] * lk1_ref[...], axis=1, keepdims=True))
           - jnp.exp(jnp.sum(lq2_ref[...] * lk2_ref[...], axis=1, keepdims=True)) + lam_init)
    sub = sub_ref[...] * (1.0 - lam_init)
    n_keys = seq + n_past if latent else ROW_CHUNK
    rb = _softmax_rows(n_keys)
    ones = jnp.ones((n_keys, LANES), BF16)

    def block_stages(i):
        rows = _chunk_rows(i)
        keys = pl.ds(0, n_keys) if latent else rows

        def qk(h, slot):
            cols = slice(LANES * h, LANES * (h + 1))
            q = q_scr[rows, cols]
            for m in (0, 1):
                s_scr[slot, m] = _dot_nt(q, k_scr[m, keys, cols])

        def softmax(h, slot):
            for m in (0, 1):
                for r in range(ROW_CHUNK // rb):
                    sub_rows = slice(r * rb, (r + 1) * rb)
                    s = s_scr[slot, m, sub_rows, :]
                    top = jnp.max(s, axis=1, keepdims=True)
                    p_scr[slot, m, sub_rows, :] = jnp.exp2((s - top).astype(BF16))

        def pv(h, slot):
            cols = slice(LANES * h, LANES * (h + 1))
            v_ext = jnp.concatenate([v_scr[keys, cols], ones], axis=1)
            maps = []
            for m in (0, 1):
                acc = _dot(p_scr[slot, m], v_ext)
                maps.append(acc[:, 0:LANES] / acc[:, LANES:2 * LANES])
            o = maps[0] - lam * maps[1]
            ms = jnp.mean(o * o, axis=1, keepdims=True)
            o = o * lax.rsqrt(ms + EPS) * sub
            ha_scr[rows, cols] = (o * g_scr[rows, cols]).astype(BF16)

        return qk, softmax, pv

    _attend_blocks(block_stages, n_blocks, n_heads, unrolled=not latent)

    if not latent:
        weights.load_out_proj()
    _out_proj_norm(x_ref, mod_ref, mod_row, ha_scr, w_out_ref, lng_ref, lnb_ref, y_ref, n_rows, alpha)

    if not latent:
        for blk in range(n_blocks):
            for copy in kv_out_copies(blk):
                copy.wait()
        weights.finish()


MOD_SLAB = (256, 1024)
MOD_SLOTS = 4


def _mod_kernel(n_cond, cvb_ref, w_hbm, b_ref, o_ref, sb_scr, ring, sems):
    depth, n_in, n_out = w_hbm.shape
    slab_rows, slab_cols = MOD_SLAB
    sublanes = 8
    slabs = [(l, cb, rs) for l in range(depth) for cb in range(n_out // slab_cols) for rs in range(n_in // slab_rows)]

    def slab_copy(n):
        l, cb, rs = slabs[n]
        return pltpu.make_async_copy(w_hbm.at[l, pl.ds(rs * slab_rows, slab_rows), pl.ds(cb * slab_cols, slab_cols)],
                                     ring.at[n % MOD_SLOTS], sems.at[n % MOD_SLOTS])

    for n in range(min(MOD_SLOTS, len(slabs))):
        slab_copy(n).start()
    sb_scr[...] = _silu(cvb_ref[...])

    accs = None
    for n, (l, cb, rs) in enumerate(slabs):
        slab_copy(n).wait()
        if rs == 0:
            accs = (jnp.zeros((sublanes, slab_cols), F32),) * n_cond

        def body(kb, accs, n=n, rs=rs):
            w = ring[n % MOD_SLOTS, pl.ds(pl.multiple_of(kb * sublanes, sublanes), sublanes), :]
            s_rows = pl.ds(pl.multiple_of(rs * slab_rows + kb * sublanes, sublanes), sublanes)
            return tuple(acc + w * jnp.tile(sb_scr[r, s_rows, :], (1, slab_cols // LANES)) for r, acc in enumerate(accs))

        accs = lax.fori_loop(0, slab_rows // sublanes, body, accs, unroll=8)
        if n + MOD_SLOTS < len(slabs):
            slab_copy(n + MOD_SLOTS).start()
        if rs == n_in // slab_rows - 1:
            cols = pl.ds(cb * slab_cols, slab_cols)
            rows = [jnp.sum(acc, axis=0, keepdims=True) + b_ref[l:l + 1, cols] for acc in accs]
            o_ref[l, :, cols] = jnp.concatenate(rows + [jnp.zeros((8 - n_cond, slab_cols), F32)], axis=0)


def _full(shape, **kw):
    zeros = (0,) * len(shape)
    return pl.BlockSpec(shape, lambda i: zeros, **kw)


def _weight_specs(latent, w_in, w_out):
    if latent:
        single = pl.Buffered(1)
        return [_full(w_in.shape, pipeline_mode=single), _full(w_out.shape, pipeline_mode=single)]
    return [pl.BlockSpec(memory_space=pl.ANY), pl.BlockSpec(memory_space=pl.ANY)]


def _weight_scratch(w_in, w_out):
    assert w_in.shape[0] % W_SLAB_ROWS == 0 and w_out.shape[0] % W_SLAB_ROWS == 0
    n_slabs = max(w_in.shape[0], w_out.shape[0]) // W_SLAB_ROWS
    return [pltpu.VMEM(w_in.shape, BF16), pltpu.VMEM(w_out.shape, BF16),
            pltpu.VMEM((n_slabs, W_SLAB_ROWS, max(w_in.shape[1], w_out.shape[1])), F32),
            pltpu.SemaphoreType.DMA((n_slabs,)), pltpu.SemaphoreType.DMA((2,))]


def _rope_tables(seq):
    t = np.arange(seq)
    n_freq = HEAD_DIM // 4
    freqs = ROPE_THETA ** (-np.arange(n_freq, dtype=np.float64) / n_freq)
    ang_row = (t // GRID_W)[:, None] * freqs
    ang_col = (t % GRID_W)[:, None] * freqs
    ang = np.concatenate([ang_row, ang_row, ang_col, ang_col], axis=1)
    sign = np.concatenate([-np.ones(n_freq), np.ones(n_freq)] * 2)[None, :]
    cos = np.tile(np.cos(ang), (1, 2)).astype(np.float32)
    sin = np.tile(np.sin(ang) * sign, (1, 2)).astype(np.float32)
    chunked_t = lambda a: a.reshape(seq // ROW_CHUNK, ROW_CHUNK, LANES).transpose(0, 2, 1)
    return jnp.asarray(cos), jnp.asarray(sin), jnp.asarray(chunked_t(cos)), jnp.asarray(chunked_t(sin))


def _head_mean_matrix():
    idx = np.arange(LANES) // HEAD_DIM
    return jnp.asarray((idx[:, None] == idx[None, :]).astype(np.float32) / HEAD_DIM, dtype=BF16)


def _modulation(c, c_ctx, w_mod, b_mod):
    depth = w_mod.shape[0]
    cv = jnp.concatenate([c_ctx[None, :], c], axis=0)
    n_cond = cv.shape[0]
    cvb = jnp.broadcast_to(cv[:, :, None], (n_cond, D_MODEL, LANES))
    assert D_MODEL % MOD_SLAB[0] == 0 and (3 * D_MODEL) % MOD_SLAB[1] == 0
    return pl.pallas_call(
        functools.partial(_mod_kernel, n_cond),
        grid=(1,),
        in_specs=[_full(cvb.shape), pl.BlockSpec(memory_space=pl.ANY), _full(b_mod.shape)],
        out_specs=_full((depth, 8, 3 * D_MODEL)),
        out_shape=jax.ShapeDtypeStruct((depth, 8, 3 * D_MODEL), F32),
        scratch_shapes=[pltpu.VMEM((n_cond, D_MODEL, LANES), F32),
                        pltpu.VMEM((MOD_SLOTS,) + MOD_SLAB, F32),
                        pltpu.SemaphoreType.DMA((MOD_SLOTS,))],
        compiler_params=pltpu.CompilerParams(dimension_semantics=("arbitrary",)),
        name="adaln_modulation",
    )(cvb, w_mod, b_mod)


def _even_layer(x, mod, layer, w_in, w_out, q_norm, k_norm, sink, ln_g, ln_b, latent, seq, n_rows, alpha, extras=()):
    total = x.shape[0]
    grid = (total // n_rows,)
    single = pl.Buffered(1)
    qn = jnp.tile(q_norm, 2)[None, :]
    knt = jnp.broadcast_to(jnp.tile(k_norm, 2)[:, None], (LANES, ROW_CHUNK))

    row_blk = lambda width: pl.BlockSpec((n_rows, width), lambda i: (i, 0))
    in_specs = [row_blk(D_MODEL),
                pl.BlockSpec((1, 8, 3 * D_MODEL), lambda i: (layer, 0, 0)),
                *_weight_specs(latent, w_in, w_out),
                _full((1, LANES)), _full((LANES, ROW_CHUNK)),
                pl.BlockSpec(memory_space=pltpu.SMEM),
                pl.BlockSpec((1, 1, D_MODEL), lambda i: (layer, 0, 0)),
                pl.BlockSpec((1, 1, D_MODEL), lambda i: (layer, 0, 0)),
                _full((LANES, LANES))]
    args = [x, mod, w_in, w_out, qn, knt, sink, ln_g, ln_b, _head_mean_matrix()]
    y_shape = jax.ShapeDtypeStruct((total, D_MODEL), F32)
    n_blocks = n_rows // ROW_CHUNK
    if latent:
        cos, sin, cos_t, sin_t, cakt, cav, cbkt, cbv = extras
        n_past = cav.shape[1]
        in_specs += [_full(cos.shape, pipeline_mode=single), _full(sin.shape, pipeline_mode=single),
                     _full(cos_t.shape, pipeline_mode=single), _full(sin_t.shape, pipeline_mode=single)]
        in_specs += [pl.BlockSpec((1, LANES, n_past), lambda i: (i, 0, 0)),
                     pl.BlockSpec((1, n_past, LANES), lambda i: (i, 0, 0))] * 2
        args += [cos, sin, cos_t, sin_t, cakt, cav, cbkt, cbv]
        out_specs = row_blk(D_MODEL)
        out_shape = y_shape
        n_keys = seq + n_past
    else:
        kv_blk = pl.BlockSpec((n_blocks, LANES, ROW_CHUNK), lambda i: (i, 0, 0))
        hbm = pl.BlockSpec(memory_space=pl.ANY)
        out_specs = [row_blk(D_MODEL)] + [kv_blk] * 4 + [hbm, hbm]
        out_shape = ([y_shape] + [jax.ShapeDtypeStruct((total // seq, LANES, seq), F32)] * 4
                     + [jax.ShapeDtypeStruct(w_in.shape, BF16), jax.ShapeDtypeStruct(w_out.shape, BF16)])
        n_keys = n_rows
    n_kchunks = n_keys // ROW_CHUNK
    n_cols = n_keys if latent else ROW_CHUNK
    scratch = [pltpu.VMEM((n_rows, D_MODEL), BF16),
               pltpu.VMEM((n_rows, 512), BF16), pltpu.VMEM((n_rows, 512), BF16),
               pltpu.VMEM((4, n_kchunks, LANES, ROW_CHUNK), BF16), pltpu.VMEM((4, n_keys, LANES), BF16),
               pltpu.VMEM((4, n_keys // WINDOW, LANES, WINDOW), BF16), pltpu.VMEM((4, n_keys, LANES), BF16),
               pltpu.VMEM((n_rows, D_MODEL), F32),
               pltpu.VMEM((2, 2, ROW_CHUNK, n_cols), F32),
               pltpu.VMEM((2, 2, ROW_CHUNK, n_cols), BF16),
               pltpu.VMEM((2, ROW_CHUNK, LANES), F32),
               pltpu.VMEM((2 * LANES, D_MODEL), BF16),
               pltpu.VMEM((D_MODEL, 2 * LANES), BF16)]
    if latent:
        scratch.append(pltpu.VMEM((1 + 2 * WINDOW // ROW_CHUNK, ROW_CHUNK, ROW_CHUNK), F32))
    else:
        scratch += _weight_scratch(w_in, w_out)
    return pl.pallas_call(
        functools.partial(_even_kernel, latent, n_rows, seq, alpha),
        grid=grid, in_specs=in_specs, out_specs=out_specs, out_shape=out_shape,
        scratch_shapes=scratch,
        compiler_params=pltpu.CompilerParams(dimension_semantics=("arbitrary",), vmem_limit_bytes=VMEM_LIMIT),
        name="even_layer_latent" if latent else "even_layer_context",
    )(*args)


def _odd_layer(x, mod, layer, w_in, w_out, lams, sub, ln_g, ln_b, latent, seq, n_rows, alpha, lam_init, extras=()):
    total = x.shape[0]
    grid = (total // n_rows,)
    row_blk = lambda width: pl.BlockSpec((n_rows, width), lambda i: (i, 0))
    single = pl.Buffered(1)
    in_specs = [row_blk(D_MODEL),
                pl.BlockSpec((1, 8, 3 * D_MODEL), lambda i: (layer, 0, 0)),
                *_weight_specs(latent, w_in, w_out),
                _full((1, HEAD_DIM)), _full((1, HEAD_DIM)), _full((1, HEAD_DIM)), _full((1, HEAD_DIM)),
                _full((1, LANES)),
                pl.BlockSpec((1, 1, D_MODEL), lambda i: (layer, 0, 0)),
                pl.BlockSpec((1, 1, D_MODEL), lambda i: (layer, 0, 0))]
    args = [x, mod, w_in, w_out, *lams, sub, ln_g, ln_b]
    y_shape = jax.ShapeDtypeStruct((total, D_MODEL), F32)
    n_heads = D_MODEL // LANES
    n_blocks = n_rows // ROW_CHUNK
    if latent:
        cos, sin, cck, ccv = extras
        n_past = cck.shape[2]
        in_specs += [_full(cos.shape, pipeline_mode=single), _full(sin.shape, pipeline_mode=single)]
        in_specs += [pl.BlockSpec((1, 1, n_past, n_heads, LANES), lambda i: (i, layer // 2, 0, 0, 0))] * 2
        args += [cos, sin, cck, ccv]
        out_specs = row_blk(D_MODEL)
        out_shape = y_shape
        n_keys = seq + n_past
    else:
        hbm = pl.BlockSpec(memory_space=pl.ANY)
        out_specs = [row_blk(D_MODEL), hbm, hbm, hbm, hbm]
        out_shape = ([y_shape] + [jax.ShapeDtypeStruct((total // seq, 1, seq, n_heads, LANES), F32)] * 2
                     + [jax.ShapeDtypeStruct(w_in.shape, BF16), jax.ShapeDtypeStruct(w_out.shape, BF16)])
        n_keys = n_rows
    n_cols = n_keys if latent else ROW_CHUNK
    scratch = [pltpu.VMEM((n_rows, D_MODEL), BF16),
               pltpu.VMEM((n_rows, D_MODEL), BF16),
               pltpu.VMEM((2, n_keys, D_MODEL), BF16),
               pltpu.VMEM((n_keys, D_MODEL), BF16),
               pltpu.VMEM((n_rows, D_MODEL), F32),
               pltpu.VMEM((2, 2, ROW_CHUNK, n_cols), F32),
               pltpu.VMEM((2, 2, ROW_CHUNK, n_cols), BF16)]
    if not latent:
        scratch += [pltpu.VMEM((n_blocks, 2, ROW_CHUNK, D_MODEL), F32),
                    pltpu.SemaphoreType.DMA((n_blocks, 2))]
        scratch += _weight_scratch(w_in, w_out)
    return pl.pallas_call(
        functools.partial(_odd_kernel, latent, n_rows, seq, alpha, lam_init),
        grid=grid, in_specs=in_specs, out_specs=out_specs, out_shape=out_shape,
        scratch_shapes=scratch,
        compiler_params=pltpu.CompilerParams(dimension_semantics=("arbitrary",), vmem_limit_bytes=VMEM_LIMIT),
        name="odd_layer_latent" if latent else "odd_layer_context",
    )(*args)


def kernel(x_prompt, x_sample, cache_a_k, cache_a_v, cache_b_k, cache_b_v, cache_c_k, cache_c_v, c, c_ctx,
           w_mod, b_mod, ln_g, ln_b, w_in_even, w_out_even, q_norm_a, k_norm_a, sink_b, w_in_odd, w_out_odd,
           lambda_q1, lambda_k1, lambda_q2, lambda_k2, subln_c):
    depth = w_mod.shape[0]
    batch, seq, _ = x_prompt.shape
    dec_batch, dec_seq, _ = x_sample.shape
    n_past = cache_a_k.shape[2]
    alpha = (2 * depth) ** 0.25
    assert seq == ROW_CHUNK and n_past % ROW_CHUNK == 0 and dec_seq % ROW_CHUNK == 0

    mod = _modulation(c, c_ctx, w_mod, b_mod)
    ln_g3 = ln_g.reshape(depth, 1, D_MODEL)
    ln_b3 = ln_b.reshape(depth, 1, D_MODEL)
    cos, sin, cos_t, sin_t = _rope_tables(dec_seq)

    bf16_weights = {}

    def run(x, latent, n_batch, s, rows_even, rows_odd):
        kv = {"a_k": [], "a_v": [], "b_k": [], "b_v": [], "c_k": [], "c_v": []}
        for l in range(depth):
            if l % 2 == 0:
                e = l // 2
                extras = ()
                if latent:
                    k_t = lambda t: t[:, e].transpose(0, 2, 3, 1).reshape(n_batch, LANES, n_past)
                    v_n = lambda t: t[:, e].reshape(n_batch, n_past, LANES)
                    extras = (cos, sin, cos_t, sin_t,
                              k_t(cache_a_k), v_n(cache_a_v), k_t(cache_b_k), v_n(cache_b_v))
                w_in, w_out = bf16_weights[l] if latent else (w_in_even[e], w_out_even[e])
                res = _even_layer(x, mod, l, w_in, w_out, q_norm_a[e], k_norm_a[e],
                                  sink_b[e], ln_g3, ln_b3, latent, s, rows_even, alpha, extras)
                if latent:
                    x = res
                else:
                    x = res[0]
                    bf16_weights[l] = res[5:7]
                    for name, t in zip(("a_k", "a_v", "b_k", "b_v"), res[1:5]):
                        kv[name].append(t.reshape(n_batch, 2, HEAD_DIM, s).transpose(0, 3, 1, 2))
            else:
                o = l // 2
                lam_init = 0.8 - 0.6 * math.exp(-0.3 * l)
                extras = (cos, sin, cache_c_k, cache_c_v) if latent else ()
                lams = [t[o][None, :] for t in (lambda_q1, lambda_k1, lambda_q2, lambda_k2)]
                w_in, w_out = bf16_weights[l] if latent else (w_in_odd[o], w_out_odd[o])
                res = _odd_layer(x, mod, l, w_in, w_out, lams,
                                 subln_c[o][None, :], ln_g3, ln_b3, latent, s, rows_odd, alpha, lam_init, extras)
                if latent:
                    x = res
                else:
                    x = res[0]
                    bf16_weights[l] = res[3:5]
                    kv["c_k"].append(res[1][:, 0])
                    kv["c_v"].append(res[2][:, 0])
        return x, kv

    y_ctx, kv = run(x_prompt.reshape(batch * seq, D_MODEL), False, batch, seq, 1024, 512)
    y_lat, _ = run(x_sample.reshape(dec_batch * dec_seq, D_MODEL), True, dec_batch, dec_seq, dec_seq, dec_seq)

    stack = lambda name: jnp.stack(kv[name], axis=1)
    return (y_ctx.reshape(batch, seq, D_MODEL), y_lat.reshape(dec_batch, dec_seq, D_MODEL),
            stack("a_k"), stack("a_v"), stack("b_k"), stack("b_v"), stack("c_k"), stack("c_v"))
```

```python
import functools
import math

import jax
import jax.numpy as jnp
import numpy as np
from jax import lax
from jax.experimental import pallas as pl
from jax.experimental.pallas import tpu as pltpu

F32 = jnp.float32
BF16 = jnp.bfloat16

D_MODEL = 1024
HEAD_DIM = 64
GRID_W = 64
WINDOW = 128
ROPE_THETA = 10000.0
EPS = 1e-6
NEG_INF = -1e30
LOG2E = 1.4426950408889634
Q_SCALE = HEAD_DIM ** -0.5 * LOG2E
LANES = 128
ROW_CHUNK = 256
SOFTMAX_VREGS = 40
VMEM_LIMIT = 60000 * 1024
W_SLAB_ROWS = 128


def _silu(x):
    return x / (1.0 + jnp.exp(-x))


def _dot(a, b):
    return jnp.dot(a, b, preferred_element_type=F32)


def _dot_nt(a, b):
    return lax.dot_general(a, b, (((1,), (1,)), ((), ())), preferred_element_type=F32)


def _lane_iota(rows):
    return lax.broadcasted_iota(jnp.int32, (rows, LANES), 1)


def _chunk_rows(i):
    if isinstance(i, int):
        return pl.ds(i * ROW_CHUNK, ROW_CHUNK)
    return pl.ds(pl.multiple_of(i * ROW_CHUNK, ROW_CHUNK), ROW_CHUNK)


def _softmax_rows(n_cols):
    rows = 8
    while rows * 2 * n_cols <= SOFTMAX_VREGS * 1024 and rows * 2 <= ROW_CHUNK:
        rows *= 2
    return rows


def _rope(a, cos, sin_signed):
    lane = _lane_iota(a.shape[0])
    fwd = pltpu.roll(a, LANES - 16, 1)
    bwd = pltpu.roll(a, 16, 1)
    partner = jnp.where((lane & 16) == 0, fwd, bwd)
    return a * cos + partner * sin_signed


def _rope_t(a, cos_t, sin_t):
    blocks = [a[16 * b:16 * (b + 1), :] for b in range(a.shape[0] // 16)]
    partner = jnp.concatenate([blocks[b ^ 1] for b in range(len(blocks))], axis=0)
    return a * cos_t + partner * sin_t


def _store_kt_variants(scr, chunk, kt):
    width = scr.shape[-1]
    per_block = kt.shape[1] // width
    zero = jnp.zeros((HEAD_DIM, kt.shape[1]), F32)
    for j in range(2):
        kj = kt[HEAD_DIM * j:HEAD_DIM * (j + 1), :]
        for par, full in enumerate((jnp.concatenate([kj, zero], axis=0), jnp.concatenate([zero, kj], axis=0))):
            full = full.astype(BF16)
            for c in range(per_block):
                scr[2 * j + par, chunk * per_block + c] = full[:, width * c:width * (c + 1)]


def _store_v_variants(scr, rows, a):
    lane = _lane_iota(a.shape[0])
    lo = lane < HEAD_DIM
    swapped = pltpu.roll(a, HEAD_DIM, 1)
    one = jnp.ones_like(a)
    scr[0, rows, :] = jnp.where(lo, a, one).astype(BF16)
    scr[1, rows, :] = jnp.where(lo, one, swapped).astype(BF16)
    scr[2, rows, :] = jnp.where(lo, swapped, one).astype(BF16)
    scr[3, rows, :] = jnp.where(lo, one, a).astype(BF16)


def _layer_norm_rows(z, g, b):
    mu = jnp.mean(z, axis=-1, keepdims=True)
    zc = z - mu
    var = jnp.mean(zc * zc, axis=-1, keepdims=True)
    return zc * lax.rsqrt(var + EPS) * g + b


def _modulate(x_ref, mod_ref, mod_row, h_scr, n_rows):
    shift = mod_ref[0, pl.ds(mod_row, 1), 0:D_MODEL]
    scale = mod_ref[0, pl.ds(mod_row, 1), D_MODEL:2 * D_MODEL]

    def body(i, carry):
        rows = _chunk_rows(i)
        h_scr[rows, :] = (x_ref[rows, :] * (1.0 + scale) + shift).astype(BF16)
        return carry

    lax.fori_loop(0, n_rows // ROW_CHUNK, body, 0)


def _out_proj_norm(x_ref, mod_ref, mod_row, attn_scr, w_out_ref, lng_ref, lnb_ref, y_ref, n_rows, alpha):
    gate = mod_ref[0, pl.ds(mod_row, 1), 2 * D_MODEL:3 * D_MODEL]
    g = lng_ref[0]
    b = lnb_ref[0]

    def body(i, carry):
        rows = _chunk_rows(i)
        out = _dot(attn_scr[rows, :], w_out_ref[...])
        z = alpha * x_ref[rows, :] + gate * out
        y_ref[rows, :] = _layer_norm_rows(z, g, b)
        return carry

    lax.fori_loop(0, n_rows // ROW_CHUNK, body, 0, unroll=True)


class _ContextWeights:
    def __init__(self, step, w_in, w_out, stage, sems, out_sems):
        self.step, self.w_in, self.w_out, self.stage, self.sems = step, w_in, w_out, stage, sems
        self.out_copies = [pltpu.make_async_copy(w[1], w[2], out_sems.at[n]) for n, w in enumerate((w_in, w_out))]

    def _slab_copies(self, w_hbm):
        n_cols = w_hbm.shape[1]
        return [pltpu.make_async_copy(w_hbm.at[pl.ds(s * W_SLAB_ROWS, W_SLAB_ROWS), :],
                                      self.stage.at[s, :, pl.ds(0, n_cols)], self.sems.at[s])
                for s in range(w_hbm.shape[0] // W_SLAB_ROWS)]

    def _cast(self, w_hbm, w_scr):
        n_cols = w_hbm.shape[1]
        for s, copy in enumerate(self._slab_copies(w_hbm)):
            copy.wait()
            w_scr[pl.ds(s * W_SLAB_ROWS, W_SLAB_ROWS), :] = self.stage[s, :, 0:n_cols].astype(BF16)

    def load_in_proj(self):
        @pl.when(self.step == 0)
        def _():
            for copy in self._slab_copies(self.w_in[0]):
                copy.start()
            self._cast(self.w_in[0], self.w_in[1])
            self.out_copies[0].start()
            for copy in self._slab_copies(self.w_out[0]):
                copy.start()

    def load_out_proj(self):
        @pl.when(self.step == 0)
        def _():
            self._cast(self.w_out[0], self.w_out[1])
            self.out_copies[1].start()

    def finish(self):
        @pl.when(self.step == 0)
        def _():
            for copy in self.out_copies:
                copy.wait()


def _run_pipeline(n_items, stages):
    for u in range(n_items + len(stages) - 1):
        for k, stage in enumerate(stages):
            t = u - k
            if 0 <= t < n_items:
                stage(t, t % 2)


def _attend_blocks(block_stages, n_blocks, n_items, unrolled):
    assert n_items % 2 == 0
    if unrolled:
        per_block = [block_stages(i) for i in range(n_blocks)]
        stages = [lambda g, slot, k=k: per_block[g // n_items][k](g % n_items, slot) for k in range(3)]
        _run_pipeline(n_blocks * n_items, stages)
    else:
        def body(i, carry):
            _run_pipeline(n_items, block_stages(i))
            return carry

        lax.fori_loop(0, n_blocks, body, 0)


def _even_kernel(latent, n_rows, seq, alpha, *refs):
    if latent:
        (x_ref, mod_ref, w_in_ref, w_out_ref, qn_ref, knt_ref, sink_ref, lng_ref, lnb_ref, pm_ref,
         cos_ref, sin_ref, cost_ref, sint_ref, cakt_ref, cav_ref, cbkt_ref, cbv_ref,
         y_ref,
         ha_scr, qa_scr, qb_scr, ka_scr, va_scr, kb_scr, vb_scr, g_scr, s_scr, p_scr, es_scr, wkt_scr, wv_scr,
         bias_scr) = refs
    else:
        (x_ref, mod_ref, w_in_hbm, w_out_hbm, qn_ref, knt_ref, sink_ref, lng_ref, lnb_ref, pm_ref,
         y_ref, nakt_ref, navt_ref, nbkt_ref, nbvt_ref, w_in_bf_hbm, w_out_bf_hbm,
         ha_scr, qa_scr, qb_scr, ka_scr, va_scr, kb_scr, vb_scr, g_scr, s_scr, p_scr, es_scr, wkt_scr,
         wv_scr, w_in_ref, w_out_ref, w_stage, w_sems, w_out_sems) = refs

    step = pl.program_id(0)
    if not latent:
        weights = _ContextWeights(step, (w_in_hbm, w_in_ref, w_in_bf_hbm), (w_out_hbm, w_out_ref, w_out_bf_hbm),
                                  w_stage, w_sems, w_out_sems)
        weights.load_in_proj()
    mod_row = step + 1 if latent else 0
    _modulate(x_ref, mod_ref, mod_row, ha_scr, n_rows)

    col_ka, col_va, col_kb, col_vb = 512, 640, 1792, 1920

    @pl.when(step == 0)
    def _():
        for r, c0 in enumerate((col_ka, col_kb)):
            wkt_scr[LANES * r:LANES * (r + 1), :] = w_in_ref[:, c0:c0 + LANES].T
        wv_scr[:, 0:LANES] = w_in_ref[:, col_va:col_va + LANES]
        wv_scr[:, LANES:2 * LANES] = w_in_ref[:, col_vb:col_vb + LANES]

    n_lat_chunks = seq // ROW_CHUNK
    if latent:
        n_past = cav_ref.shape[1]
        past_rows = pl.ds(seq, n_past)
        _store_kt_variants(ka_scr, n_lat_chunks, cakt_ref[0])
        _store_kt_variants(kb_scr, n_lat_chunks, cbkt_ref[0])
        _store_v_variants(va_scr, past_rows, cav_ref[0])
        _store_v_variants(vb_scr, past_rows, cbv_ref[0])

    pm = pm_ref[...]
    qn = qn_ref[...]
    knt = knt_ref[...]

    def proj(i, carry):
        rows = _chunk_rows(i)
        hh = ha_scr[rows, :]
        if latent:
            cos = cos_ref[rows, :]
            sin = sin_ref[rows, :]
            rot = lambda a: _rope(a, cos, sin)
            rot_t = lambda a: _rope_t(a, cost_ref[i], sint_ref[i])
        else:
            rot = rot_t = lambda a: a

        acc = _dot(hh, w_in_ref[:, 0:512])
        for j in range(4):
            a = acc[:, LANES * j:LANES * (j + 1)]
            ms = _dot((a * a).astype(BF16), pm)
            a = rot(a * lax.rsqrt(ms + EPS) * qn)
            qa_scr[rows, LANES * j:LANES * (j + 1)] = (a * Q_SCALE).astype(BF16)
        acc = _dot(hh, w_in_ref[:, 1280:1792])
        for j in range(4):
            a = rot(acc[:, LANES * j:LANES * (j + 1)])
            qb_scr[rows, LANES * j:LANES * (j + 1)] = (a * Q_SCALE).astype(BF16)
        g_scr[rows, 0:512] = _silu(_dot(hh, w_in_ref[:, 768:1280]))
        g_scr[rows, 512:1024] = _silu(_dot(hh, w_in_ref[:, 2048:2560]))
        v = _dot(hh, wv_scr[...])
        _store_v_variants(va_scr, rows, v[:, 0:LANES])
        _store_v_variants(vb_scr, rows, v[:, LANES:2 * LANES])

        kt = _dot_nt(wkt_scr[0:2 * LANES, :], hh)
        kat = kt[0:LANES, :]
        ms = _dot(pm, (kat * kat).astype(BF16))
        kat = kat * lax.rsqrt(ms + EPS) * knt
        kbt = kt[LANES:2 * LANES, :]
        if not latent:
            vt = v.T
            nakt_ref[i] = kat
            nbkt_ref[i] = kbt
            navt_ref[i] = vt[0:LANES, :]
            nbvt_ref[i] = vt[LANES:2 * LANES, :]
        _store_kt_variants(ka_scr, i, rot_t(kat))
        _store_kt_variants(kb_scr, i, rot_t(kbt))
        return carry

    lax.fori_loop(0, n_rows // ROW_CHUNK, proj, 0, unroll=2)

    sinks = [sink_ref[h] * LOG2E for h in range(8)]
    ck = ROW_CHUNK
    bk = kb_scr.shape[-1]
    win = ROW_CHUNK + 2 * WINDOW
    n_items = 8

    def block_stages(i):
        rows = _chunk_rows(i)
        if latent:
            a_chunks = list(range(n_lat_chunks + n_past // ck))
            a_keys = pl.ds(0, seq + n_past)
            w0 = jnp.clip(i * (ck // bk) - WINDOW // bk, 0, (seq - win) // bk)
            win_rows = pl.ds(pl.multiple_of(w0 * bk, bk), win)
            dist = (lax.broadcasted_iota(jnp.int32, (ROW_CHUNK, ck), 1)
                    - lax.broadcasted_iota(jnp.int32, (ROW_CHUNK, ck), 0))
            for c in range(win // ck):
                off = w0 * bk + c * ck - i * ck
                bias_scr[c] = jnp.where(jnp.abs(dist + off) <= WINDOW, 0.0, NEG_INF).astype(F32)
            b_first = [w0 + c * (ck // bk) for c in range(win // ck)] + [seq // bk]
            n_biased = win // ck
            b_cols = win + n_past
        else:
            a_chunks = [i]
            a_keys = rows
            b_first = [i * (ck // bk)]
            n_biased = 0
            b_cols = ck
        a_cols = len(a_chunks) * ck

        def qk(t, slot):
            p, branch = divmod(t, 2)
            cols = slice(LANES * p, LANES * (p + 1))
            kvh = p // 2
            q = (qb_scr if branch else qa_scr)[rows, cols]
            for par in (0, 1):
                var = 2 * kvh + par
                if branch:
                    tiles = [jnp.concatenate([kb_scr[var, first + d] for d in range(ck // bk)], axis=1)
                             for first in b_first]
                else:
                    tiles = [ka_scr[var, chunk] for chunk in a_chunks]
                for c, kt in enumerate(tiles):
                    s = _dot(q, kt)
                    if branch and c < n_biased:
                        s = s + bias_scr[c]
                    s_scr[slot, par, :, c * ck:(c + 1) * ck] = s

        def softmax(t, slot):
            p, branch = divmod(t, 2)
            n_cols = b_cols if branch else a_cols
            rb = _softmax_rows(n_cols)
            for par in (0, 1):
                for r in range(ROW_CHUNK // rb):
                    sub = slice(r * rb, (r + 1) * rb)
                    s = s_scr[slot, par, sub, 0:n_cols]
                    m = jnp.max(s, axis=1, keepdims=True)
                    if branch:
                        sink = sinks[2 * p + par]
                        m = jnp.maximum(m, sink)
                        es_scr[slot, sub, HEAD_DIM * par:HEAD_DIM * (par + 1)] = jnp.broadcast_to(
                            jnp.exp2(sink - m), (rb, HEAD_DIM))
                    p_scr[slot, par, sub, 0:n_cols] = jnp.exp2((s - m).astype(BF16))

        def pv(t, slot):
            p, branch = divmod(t, 2)
            kvh = p // 2
            v_scr = vb_scr if branch else va_scr
            accs = []
            for par in (0, 1):
                var = 2 * kvh + par
                if latent and branch:
                    n_loc = win
                    accs.append(_dot(p_scr[slot, par, :, 0:n_loc], v_scr[var, win_rows, :])
                                + _dot(p_scr[slot, par, :, n_loc:b_cols], v_scr[var, past_rows, :]))
                else:
                    accs.append(_dot(p_scr[slot, par, :, 0:a_cols], v_scr[var, a_keys, :]))
            lo = _lane_iota(ROW_CHUNK) < HEAD_DIM
            denom = pltpu.roll(jnp.where(lo, accs[1], accs[0]), HEAD_DIM, 1)
            if branch:
                denom = denom + es_scr[slot]
            o = jnp.where(lo, accs[0], accs[1]) / denom
            ocols = slice(512 * branch + LANES * p, 512 * branch + LANES * (p + 1))
            ha_scr[rows, ocols] = (o * g_scr[rows, ocols]).astype(BF16)

        return qk, softmax, pv

    _attend_blocks(block_stages, n_rows // ROW_CHUNK, n_items, unrolled=not latent)

    if not latent:
        weights.load_out_proj()
    _out_proj_norm(x_ref, mod_ref, mod_row, ha_scr, w_out_ref, lng_ref, lnb_ref, y_ref, n_rows, alpha)
    if not latent:
        weights.finish()


def _odd_kernel(latent, n_rows, seq, alpha, lam_init, *refs):
    if latent:
        (x_ref, mod_ref, w_in_ref, w_out_ref, lq1_ref, lk1_ref, lq2_ref, lk2_ref, sub_ref, lng_ref, lnb_ref,
         cos_ref, sin_ref, cck_ref, ccv_ref,
         y_ref,
         ha_scr, q_scr, k_scr, v_scr, g_scr, s_scr, p_scr) = refs
    else:
        (x_ref, mod_ref, w_in_hbm, w_out_hbm, lq1_ref, lk1_ref, lq2_ref, lk2_ref, sub_ref, lng_ref, lnb_ref,
         y_ref, nck_hbm, ncv_hbm, w_in_bf_hbm, w_out_bf_hbm,
         ha_scr, q_scr, k_scr, v_scr, g_scr, s_scr, p_scr, kv_stage, kv_sems,
         w_in_ref, w_out_ref, w_stage, w_sems, w_out_sems) = refs

    step = pl.program_id(0)
    if not latent:
        weights = _ContextWeights(step, (w_in_hbm, w_in_ref, w_in_bf_hbm), (w_out_hbm, w_out_ref, w_out_bf_hbm),
                                  w_stage, w_sems, w_out_sems)
        weights.load_in_proj()
    mod_row = step + 1 if latent else 0
    _modulate(x_ref, mod_ref, mod_row, ha_scr, n_rows)

    n_heads = D_MODEL // LANES
    n_blocks = n_rows // ROW_CHUNK
    lo = _lane_iota(ROW_CHUNK) < HEAD_DIM

    def kv_out_copies(blk):
        elem = step * n_blocks + blk
        return [pltpu.make_async_copy(kv_stage.at[blk, t, :, pl.ds(LANES * h, LANES)],
                                      out.at[elem, 0, :, h, :], kv_sems.at[blk, t])
                for t, out in enumerate((nck_hbm, ncv_hbm)) for h in range(n_heads)]

    def store_k(rows, h, a):
        cols = slice(LANES * h, LANES * (h + 1))
        zero = jnp.zeros_like(a)
        k_scr[0, rows, cols] = jnp.where(lo, a, zero).astype(BF16)
        k_scr[1, rows, cols] = jnp.where(lo, zero, a).astype(BF16)

    if latent:
        n_past = cck_ref.shape[2]
        past = pl.ds(seq, n_past)
        for h in range(n_heads):
            store_k(past, h, cck_ref[0, 0, :, h, :])
            v_scr[past, LANES * h:LANES * (h + 1)] = ccv_ref[0, 0, :, h, :].astype(BF16)

    def proj(i, carry):
        rows = _chunk_rows(i)
        hh = ha_scr[rows, :]
        if latent:
            cos = cos_ref[rows, :]
            sin = sin_ref[rows, :]
            rot = lambda a: _rope(a, cos, sin)
        else:
            rot = lambda a: a
        for half in range(2):
            acc = _dot(hh, w_in_ref[:, 512 * half:512 * (half + 1)])
            for j in range(4):
                a = rot(acc[:, LANES * j:LANES * (j + 1)])
                cols = slice(512 * half + LANES * j, 512 * half + LANES * (j + 1))
                q_scr[rows, cols] = (a * Q_SCALE).astype(BF16)
        for half in range(2):
            acc = _dot(hh, w_in_ref[:, 1024 + 512 * half:1024 + 512 * (half + 1)])
            if not latent:
                kv_stage[i, 0, :, 512 * half:512 * (half + 1)] = acc
            for j in range(4):
                store_k(rows, 4 * half + j, rot(acc[:, LANES * j:LANES * (j + 1)]))
        for half in range(2):
            acc = _dot(hh, w_in_ref[:, 2048 + 512 * half:2048 + 512 * (half + 1)])
            if not latent:
                kv_stage[i, 1, :, 512 * half:512 * (half + 1)] = acc
            v_scr[rows, 512 * half:512 * (half + 1)] = acc.astype(BF16)
        if not latent:
            for copy in kv_out_copies(i):
                copy.start()
        for half in range(2):
            acc = _dot(hh, w_in_ref[:, 3072 + 512 * half:3072 + 512 * (half + 1)])
            g_scr[rows, 512 * half:512 * (half + 1)] = _silu(acc)
        return carry

    if latent:
        lax.fori_loop(0, n_blocks, proj, 0, unroll=2)
    else:
        for blk in range(n_blocks):
            proj(blk, 0)

    lam = (jnp.exp(jnp.sum(lq1_ref[...] * lk1_ref[...], axis=1, keepdims=True))
           - jnp.exp(jnp.sum(lq2_ref[...] * lk2_ref[...], axis=1, keepdims=True)) + lam_init)
    sub = sub_ref[...] * (1.0 - lam_init)
    n_keys = seq + n_past if latent else ROW_CHUNK
    rb = _softmax_rows(n_keys)
    ones = jnp.ones((n_keys, LANES), BF16)

    def block_stages(i):
        rows = _chunk_rows(i)
        keys = pl.ds(0, n_keys) if latent else rows

        def qk(h, slot):
            cols = slice(LANES * h, LANES * (h + 1))
            q = q_scr[rows, cols]
            for m in (0, 1):
                s_scr[slot, m] = _dot_nt(q, k_scr[m, keys, cols])

        def softmax(h, slot):
            for m in (0, 1):
                for r in range(ROW_CHUNK // rb):
                    sub_rows = slice(r * rb, (r + 1) * rb)
                    s = s_scr[slot, m, sub_rows, :]
                    top = jnp.max(s, axis=1, keepdims=True)
                    p_scr[slot, m, sub_rows, :] = jnp.exp2((s - top).astype(BF16))

        def pv(h, slot):
            cols = slice(LANES * h, LANES * (h + 1))
            v_ext = jnp.concatenate([v_scr[keys, cols], ones], axis=1)
            maps = []
            for m in (0, 1):
                acc = _dot(p_scr[slot, m], v_ext)
                maps.append(acc[:, 0:LANES] / acc[:, LANES:2 * LANES])
            o = maps[0] - lam * maps[1]
            ms = jnp.mean(o * o, axis=1, keepdims=True)
            o = o * lax.rsqrt(ms + EPS) * sub
            ha_scr[rows, cols] = (o * g_scr[rows, cols]).astype(BF16)

        return qk, softmax, pv

    _attend_blocks(block_stages, n_blocks, n_heads, unrolled=not latent)

    if not latent:
        weights.load_out_proj()
    _out_proj_norm(x_ref, mod_ref, mod_row, ha_scr, w_out_ref, lng_ref, lnb_ref, y_ref, n_rows, alpha)

    if not latent:
        for blk in range(n_blocks):
            for copy in kv_out_copies(blk):
                copy.wait()
        weights.finish()


MOD_SLAB_ROWS = 128
MOD_COL_BLOCK = 1024
MOD_SLOTS = 4


def _mod_kernel(n_cond, cvb_ref, w_hbm, b_ref, o_ref, sb_scr, ring, acc_scr, sems):
    depth, n_in, n_out = w_hbm.shape
    sublanes = 8
    slabs_per_layer = n_in // MOD_SLAB_ROWS
    slabs = [(l, rs) for l in range(depth) for rs in range(slabs_per_layer)]

    def slab_copy(n):
        l, rs = slabs[n]
        return pltpu.make_async_copy(w_hbm.at[l, pl.ds(rs * MOD_SLAB_ROWS, MOD_SLAB_ROWS), :],
                                     ring.at[n % MOD_SLOTS], sems.at[n % MOD_SLOTS])

    for n in range(min(MOD_SLOTS, len(slabs))):
        slab_copy(n).start()
    sb_scr[...] = _silu(cvb_ref[...])

    for n, (l, rs) in enumerate(slabs):
        slab_copy(n).wait()
        for cb in range(n_out // MOD_COL_BLOCK):
            cols = pl.ds(cb * MOD_COL_BLOCK, MOD_COL_BLOCK)
            if rs == 0:
                accs = (jnp.zeros((sublanes, MOD_COL_BLOCK), F32),) * n_cond
            else:
                accs = tuple(acc_scr[r, :, cols] for r in range(n_cond))

            def body(kb, accs, n=n, rs=rs, cols=cols):
                w = ring[n % MOD_SLOTS, pl.ds(pl.multiple_of(kb * sublanes, sublanes), sublanes), cols]
                s_rows = pl.ds(pl.multiple_of(rs * MOD_SLAB_ROWS + kb * sublanes, sublanes), sublanes)
                return tuple(acc + w * jnp.tile(sb_scr[r, s_rows, :], (1, MOD_COL_BLOCK // LANES))
                             for r, acc in enumerate(accs))

            accs = lax.fori_loop(0, MOD_SLAB_ROWS // sublanes, body, accs, unroll=8)
            if rs < slabs_per_layer - 1:
                for r in range(n_cond):
                    acc_scr[r, :, cols] = accs[r]
            else:
                rows = [jnp.sum(acc, axis=0, keepdims=True) + b_ref[l:l + 1, cols] for acc in accs]
                o_ref[l, :, cols] = jnp.concatenate(rows + [jnp.zeros((8 - n_cond, MOD_COL_BLOCK), F32)], axis=0)
        if n + MOD_SLOTS < len(slabs):
            slab_copy(n + MOD_SLOTS).start()


def _full(shape, **kw):
    zeros = (0,) * len(shape)
    return pl.BlockSpec(shape, lambda i: zeros, **kw)


def _weight_specs(latent, w_in, w_out):
    if latent:
        single = pl.Buffered(1)
        return [_full(w_in.shape, pipeline_mode=single), _full(w_out.shape, pipeline_mode=single)]
    return [pl.BlockSpec(memory_space=pl.ANY), pl.BlockSpec(memory_space=pl.ANY)]


def _weight_scratch(w_in, w_out):
    assert w_in.shape[0] % W_SLAB_ROWS == 0 and w_out.shape[0] % W_SLAB_ROWS == 0
    n_slabs = max(w_in.shape[0], w_out.shape[0]) // W_SLAB_ROWS
    return [pltpu.VMEM(w_in.shape, BF16), pltpu.VMEM(w_out.shape, BF16),
            pltpu.VMEM((n_slabs, W_SLAB_ROWS, max(w_in.shape[1], w_out.shape[1])), F32),
            pltpu.SemaphoreType.DMA((n_slabs,)), pltpu.SemaphoreType.DMA((2,))]


def _rope_tables(seq):
    t = np.arange(seq)
    n_freq = HEAD_DIM // 4
    freqs = ROPE_THETA ** (-np.arange(n_freq, dtype=np.float64) / n_freq)
    ang_row = (t // GRID_W)[:, None] * freqs
    ang_col = (t % GRID_W)[:, None] * freqs
    ang = np.concatenate([ang_row, ang_row, ang_col, ang_col], axis=1)
    sign = np.concatenate([-np.ones(n_freq), np.ones(n_freq)] * 2)[None, :]
    cos = np.tile(np.cos(ang), (1, 2)).astype(np.float32)
    sin = np.tile(np.sin(ang) * sign, (1, 2)).astype(np.float32)
    chunked_t = lambda a: a.reshape(seq // ROW_CHUNK, ROW_CHUNK, LANES).transpose(0, 2, 1)
    return jnp.asarray(cos), jnp.asarray(sin), jnp.asarray(chunked_t(cos)), jnp.asarray(chunked_t(sin))


def _head_mean_matrix():
    idx = np.arange(LANES) // HEAD_DIM
    return jnp.asarray((idx[:, None] == idx[None, :]).astype(np.float32) / HEAD_DIM, dtype=BF16)


def _modulation(c, c_ctx, w_mod, b_mod):
    depth = w_mod.shape[0]
    cv = jnp.concatenate([c_ctx[None, :], c], axis=0)
    n_cond = cv.shape[0]
    cvb = jnp.broadcast_to(cv[:, :, None], (n_cond, D_MODEL, LANES))
    assert D_MODEL % MOD_SLAB_ROWS == 0 and (3 * D_MODEL) % MOD_COL_BLOCK == 0
    return pl.pallas_call(
        functools.partial(_mod_kernel, n_cond),
        grid=(1,),
        in_specs=[_full(cvb.shape), pl.BlockSpec(memory_space=pl.ANY), _full(b_mod.shape)],
        out_specs=_full((depth, 8, 3 * D_MODEL)),
        out_shape=jax.ShapeDtypeStruct((depth, 8, 3 * D_MODEL), F32),
        scratch_shapes=[pltpu.VMEM((n_cond, D_MODEL, LANES), F32),
                        pltpu.VMEM((MOD_SLOTS, MOD_SLAB_ROWS, 3 * D_MODEL), F32),
                        pltpu.VMEM((n_cond, 8, 3 * D_MODEL), F32),
                        pltpu.SemaphoreType.DMA((MOD_SLOTS,))],
        compiler_params=pltpu.CompilerParams(dimension_semantics=("arbitrary",)),
        name="adaln_modulation",
    )(cvb, w_mod, b_mod)


def _even_layer(x, mod, layer, w_in, w_out, q_norm, k_norm, sink, ln_g, ln_b, latent, seq, n_rows, alpha, extras=()):
    total = x.shape[0]
    grid = (total // n_rows,)
    single = pl.Buffered(1)
    qn = jnp.tile(q_norm, 2)[None, :]
    knt = jnp.broadcast_to(jnp.tile(k_norm, 2)[:, None], (LANES, ROW_CHUNK))

    row_blk = lambda width: pl.BlockSpec((n_rows, width), lambda i: (i, 0))
    in_specs = [row_blk(D_MODEL),
                pl.BlockSpec((1, 8, 3 * D_MODEL), lambda i: (layer, 0, 0)),
                *_weight_specs(latent, w_in, w_out),
                _full((1, LANES)), _full((LANES, ROW_CHUNK)),
                pl.BlockSpec(memory_space=pltpu.SMEM),
                pl.BlockSpec((1, 1, D_MODEL), lambda i: (layer, 0, 0)),
                pl.BlockSpec((1, 1, D_MODEL), lambda i: (layer, 0, 0)),
                _full((LANES, LANES))]
    args = [x, mod, w_in, w_out, qn, knt, sink, ln_g, ln_b, _head_mean_matrix()]
    y_shape = jax.ShapeDtypeStruct((total, D_MODEL), F32)
    n_blocks = n_rows // ROW_CHUNK
    if latent:
        cos, sin, cos_t, sin_t, cakt, cav, cbkt, cbv = extras
        n_past = cav.shape[1]
        in_specs += [_full(cos.shape, pipeline_mode=single), _full(sin.shape, pipeline_mode=single),
                     _full(cos_t.shape, pipeline_mode=single), _full(sin_t.shape, pipeline_mode=single)]
        in_specs += [pl.BlockSpec((1, LANES, n_past), lambda i: (i, 0, 0)),
                     pl.BlockSpec((1, n_past, LANES), lambda i: (i, 0, 0))] * 2
        args += [cos, sin, cos_t, sin_t, cakt, cav, cbkt, cbv]
        out_specs = row_blk(D_MODEL)
        out_shape = y_shape
        n_keys = seq + n_past
    else:
        kv_blk = pl.BlockSpec((n_blocks, LANES, ROW_CHUNK), lambda i: (i, 0, 0))
        hbm = pl.BlockSpec(memory_space=pl.ANY)
        out_specs = [row_blk(D_MODEL)] + [kv_blk] * 4 + [hbm, hbm]
        out_shape = ([y_shape] + [jax.ShapeDtypeStruct((total // seq, LANES, seq), F32)] * 4
                     + [jax.ShapeDtypeStruct(w_in.shape, BF16), jax.ShapeDtypeStruct(w_out.shape, BF16)])
        n_keys = n_rows
    n_kchunks = n_keys // ROW_CHUNK
    n_cols = n_keys if latent else ROW_CHUNK
    scratch = [pltpu.VMEM((n_rows, D_MODEL), BF16),
               pltpu.VMEM((n_rows, 512), BF16), pltpu.VMEM((n_rows, 512), BF16),
               pltpu.VMEM((4, n_kchunks, LANES, ROW_CHUNK), BF16), pltpu.VMEM((4, n_keys, LANES), BF16),
               pltpu.VMEM((4, n_keys // WINDOW, LANES, WINDOW), BF16), pltpu.VMEM((4, n_keys, LANES), BF16),
               pltpu.VMEM((n_rows, D_MODEL), F32),
               pltpu.VMEM((2, 2, ROW_CHUNK, n_cols), F32),
               pltpu.VMEM((2, 2, ROW_CHUNK, n_cols), BF16),
               pltpu.VMEM((2, ROW_CHUNK, LANES), F32),
               pltpu.VMEM((2 * LANES, D_MODEL), BF16),
               pltpu.VMEM((D_MODEL, 2 * LANES), BF16)]
    if latent:
        scratch.append(pltpu.VMEM((1 + 2 * WINDOW // ROW_CHUNK, ROW_CHUNK, ROW_CHUNK), F32))
    else:
        scratch += _weight_scratch(w_in, w_out)
    return pl.pallas_call(
        functools.partial(_even_kernel, latent, n_rows, seq, alpha),
        grid=grid, in_specs=in_specs, out_specs=out_specs, out_shape=out_shape,
        scratch_shapes=scratch,
        compiler_params=pltpu.CompilerParams(dimension_semantics=("arbitrary",), vmem_limit_bytes=VMEM_LIMIT),
        name="even_layer_latent" if latent else "even_layer_context",
    )(*args)


def _odd_layer(x, mod, layer, w_in, w_out, lams, sub, ln_g, ln_b, latent, seq, n_rows, alpha, lam_init, extras=()):
    total = x.shape[0]
    grid = (total // n_rows,)
    row_blk = lambda width: pl.BlockSpec((n_rows, width), lambda i: (i, 0))
    single = pl.Buffered(1)
    in_specs = [row_blk(D_MODEL),
                pl.BlockSpec((1, 8, 3 * D_MODEL), lambda i: (layer, 0, 0)),
                *_weight_specs(latent, w_in, w_out),
                _full((1, HEAD_DIM)), _full((1, HEAD_DIM)), _full((1, HEAD_DIM)), _full((1, HEAD_DIM)),
                _full((1, LANES)),
                pl.BlockSpec((1, 1, D_MODEL), lambda i: (layer, 0, 0)),
                pl.BlockSpec((1, 1, D_MODEL), lambda i: (layer, 0, 0))]
    args = [x, mod, w_in, w_out, *lams, sub, ln_g, ln_b]
    y_shape = jax.ShapeDtypeStruct((total, D_MODEL), F32)
    n_heads = D_MODEL // LANES
    n_blocks = n_rows // ROW_CHUNK
    if latent:
        cos, sin, cck, ccv = extras
        n_past = cck.shape[2]
        in_specs += [_full(cos.shape, pipeline_mode=single), _full(sin.shape, pipeline_mode=single)]
        in_specs += [pl.BlockSpec((1, 1, n_past, n_heads, LANES), lambda i: (i, layer // 2, 0, 0, 0))] * 2
        args += [cos, sin, cck, ccv]
        out_specs = row_blk(D_MODEL)
        out_shape = y_shape
        n_keys = seq + n_past
    else:
        hbm = pl.BlockSpec(memory_space=pl.ANY)
        out_specs = [row_blk(D_MODEL), hbm, hbm, hbm, hbm]
        out_shape = ([y_shape] + [jax.ShapeDtypeStruct((total // seq, 1, seq, n_heads, LANES), F32)] * 2
                     + [jax.ShapeDtypeStruct(w_in.shape, BF16), jax.ShapeDtypeStruct(w_out.shape, BF16)])
        n_keys = n_rows
    n_cols = n_keys if latent else ROW_CHUNK
    scratch = [pltpu.VMEM((n_rows, D_MODEL), BF16),
               pltpu.VMEM((n_rows, D_MODEL), BF16),
               pltpu.VMEM((2, n_keys, D_MODEL), BF16),
               pltpu.VMEM((n_keys, D_MODEL), BF16),
               pltpu.VMEM((n_rows, D_MODEL), F32),
               pltpu.VMEM((2, 2, ROW_CHUNK, n_cols), F32),
               pltpu.VMEM((2, 2, ROW_CHUNK, n_cols), BF16)]
    if not latent:
        scratch += [pltpu.VMEM((n_blocks, 2, ROW_CHUNK, D_MODEL), F32),
                    pltpu.SemaphoreType.DMA((n_blocks, 2))]
        scratch += _weight_scratch(w_in, w_out)
    return pl.pallas_call(
        functools.partial(_odd_kernel, latent, n_rows, seq, alpha, lam_init),
        grid=grid, in_specs=in_specs, out_specs=out_specs, out_shape=out_shape,
        scratch_shapes=scratch,
        compiler_params=pltpu.CompilerParams(dimension_semantics=("arbitrary",), vmem_limit_bytes=VMEM_LIMIT),
        name="odd_layer_latent" if latent else "odd_layer_context",
    )(*args)


def kernel(x_prompt, x_sample, cache_a_k, cache_a_v, cache_b_k, cache_b_v, cache_c_k, cache_c_v, c, c_ctx,
           w_mod, b_mod, ln_g, ln_b, w_in_even, w_out_even, q_norm_a, k_norm_a, sink_b, w_in_odd, w_out_odd,
           lambda_q1, lambda_k1, lambda_q2, lambda_k2, subln_c):
    depth = w_mod.shape[0]
    batch, seq, _ = x_prompt.shape
    dec_batch, dec_seq, _ = x_sample.shape
    n_past = cache_a_k.shape[2]
    alpha = (2 * depth) ** 0.25
    assert seq == ROW_CHUNK and n_past % ROW_CHUNK == 0 and dec_seq % ROW_CHUNK == 0

    mod = _modulation(c, c_ctx, w_mod, b_mod)
    ln_g3 = ln_g.reshape(depth, 1, D_MODEL)
    ln_b3 = ln_b.reshape(depth, 1, D_MODEL)
    cos, sin, cos_t, sin_t = _rope_tables(dec_seq)

    bf16_weights = {}

    def run(x, latent, n_batch, s, rows_even, rows_odd):
        kv = {"a_k": [], "a_v": [], "b_k": [], "b_v": [], "c_k": [], "c_v": []}
        for l in range(depth):
            if l % 2 == 0:
                e = l // 2
                extras = ()
                if latent:
                    k_t = lambda t: t[:, e].transpose(0, 2, 3, 1).reshape(n_batch, LANES, n_past)
                    v_n = lambda t: t[:, e].reshape(n_batch, n_past, LANES)
                    extras = (cos, sin, cos_t, sin_t,
                              k_t(cache_a_k), v_n(cache_a_v), k_t(cache_b_k), v_n(cache_b_v))
                w_in, w_out = bf16_weights[l] if latent else (w_in_even[e], w_out_even[e])
                res = _even_layer(x, mod, l, w_in, w_out, q_norm_a[e], k_norm_a[e],
                                  sink_b[e], ln_g3, ln_b3, latent, s, rows_even, alpha, extras)
                if latent:
                    x = res
                else:
                    x = res[0]
                    bf16_weights[l] = res[5:7]
                    for name, t in zip(("a_k", "a_v", "b_k", "b_v"), res[1:5]):
                        kv[name].append(t.reshape(n_batch, 2, HEAD_DIM, s).transpose(0, 3, 1, 2))
            else:
                o = l // 2
                lam_init = 0.8 - 0.6 * math.exp(-0.3 * l)
                extras = (cos, sin, cache_c_k, cache_c_v) if latent else ()
                lams = [t[o][None, :] for t in (lambda_q1, lambda_k1, lambda_q2, lambda_k2)]
                w_in, w_out = bf16_weights[l] if latent else (w_in_odd[o], w_out_odd[o])
                res = _odd_layer(x, mod, l, w_in, w_out, lams,
                                 subln_c[o][None, :], ln_g3, ln_b3, latent, s, rows_odd, alpha, lam_init, extras)
                if latent:
                    x = res
                else:
                    x = res[0]
                    bf16_weights[l] = res[3:5]
                    kv["c_k"].append(res[1][:, 0])
                    kv["c_v"].append(res[2][:, 0])
        return x, kv

    y_ctx, kv = run(x_prompt.reshape(batch * seq, D_MODEL), False, batch, seq, 1024, 512)
    y_lat, _ = run(x_sample.reshape(dec_batch * dec_seq, D_MODEL), True, dec_batch, dec_seq, dec_seq, dec_seq)

    stack = lambda name: jnp.stack(kv[name], axis=1)
    return (y_ctx.reshape(batch, seq, D_MODEL), y_lat.reshape(dec_batch, dec_seq, D_MODEL),
            stack("a_k"), stack("a_v"), stack("b_k"), stack("b_v"), stack("c_k"), stack("c_v"))
```

```python
import functools
import math

import jax
import jax.numpy as jnp
import numpy as np
from jax import lax
from jax.experimental import pallas as pl
from jax.experimental.pallas import tpu as pltpu

F32 = jnp.float32
BF16 = jnp.bfloat16

D_MODEL = 1024
HEAD_DIM = 64
GRID_W = 64
WINDOW = 128
ROPE_THETA = 10000.0
EPS = 1e-6
NEG_INF = -1e30
LOG2E = 1.4426950408889634
Q_SCALE = HEAD_DIM ** -0.5 * LOG2E
LANES = 128
ROW_CHUNK = 256
SOFTMAX_VREGS = 40
VMEM_LIMIT = 60000 * 1024
W_SLAB_ROWS = 128


def _silu(x):
    return x / (1.0 + jnp.exp(-x))


def _dot(a, b):
    return jnp.dot(a, b, preferred_element_type=F32)


def _dot_nt(a, b):
    return lax.dot_general(a, b, (((1,), (1,)), ((), ())), preferred_element_type=F32)


def _lane_iota(rows):
    return lax.broadcasted_iota(jnp.int32, (rows, LANES), 1)


def _chunk_rows(i):
    if isinstance(i, int):
        return pl.ds(i * ROW_CHUNK, ROW_CHUNK)
    return pl.ds(pl.multiple_of(i * ROW_CHUNK, ROW_CHUNK), ROW_CHUNK)


def _softmax_rows(n_cols):
    rows = 8
    while rows * 2 * n_cols <= SOFTMAX_VREGS * 1024 and rows * 2 <= ROW_CHUNK:
        rows *= 2
    return rows


def _rope(a, cos, sin_signed):
    lane = _lane_iota(a.shape[0])
    fwd = pltpu.roll(a, LANES - 16, 1)
    bwd = pltpu.roll(a, 16, 1)
    partner = jnp.where((lane & 16) == 0, fwd, bwd)
    return a * cos + partner * sin_signed


def _rope_t(a, cos_t, sin_t):
    blocks = [a[16 * b:16 * (b + 1), :] for b in range(a.shape[0] // 16)]
    partner = jnp.concatenate([blocks[b ^ 1] for b in range(len(blocks))], axis=0)
    return a * cos_t + partner * sin_t


def _store_kt_variants(scr, chunk, kt):
    width = scr.shape[-1]
    per_block = kt.shape[1] // width
    zero = jnp.zeros((HEAD_DIM, kt.shape[1]), F32)
    for j in range(2):
        kj = kt[HEAD_DIM * j:HEAD_DIM * (j + 1), :]
        for par, full in enumerate((jnp.concatenate([kj, zero], axis=0), jnp.concatenate([zero, kj], axis=0))):
            full = full.astype(BF16)
            for c in range(per_block):
                scr[2 * j + par, chunk * per_block + c] = full[:, width * c:width * (c + 1)]


def _store_v_variants(scr, rows, a):
    lane = _lane_iota(a.shape[0])
    lo = lane < HEAD_DIM
    swapped = pltpu.roll(a, HEAD_DIM, 1)
    one = jnp.ones_like(a)
    scr[0, rows, :] = jnp.where(lo, a, one).astype(BF16)
    scr[1, rows, :] = jnp.where(lo, one, swapped).astype(BF16)
    scr[2, rows, :] = jnp.where(lo, swapped, one).astype(BF16)
    scr[3, rows, :] = jnp.where(lo, one, a).astype(BF16)


def _layer_norm_rows(z, g, b):
    mu = jnp.mean(z, axis=-1, keepdims=True)
    zc = z - mu
    var = jnp.mean(zc * zc, axis=-1, keepdims=True)
    return zc * lax.rsqrt(var + EPS) * g + b


def _modulate(x_ref, mod_ref, mod_row, h_scr, n_rows):
    shift = mod_ref[0, pl.ds(mod_row, 1), 0:D_MODEL]
    scale = mod_ref[0, pl.ds(mod_row, 1), D_MODEL:2 * D_MODEL]

    def body(i, carry):
        rows = _chunk_rows(i)
        h_scr[rows, :] = (x_ref[rows, :] * (1.0 + scale) + shift).astype(BF16)
        return carry

    lax.fori_loop(0, n_rows // ROW_CHUNK, body, 0)


def _out_proj_norm(x_ref, mod_ref, mod_row, attn_scr, w_out_ref, lng_ref, lnb_ref, layer, y_ref, n_rows, alpha):
    gate = mod_ref[0, pl.ds(mod_row, 1), 2 * D_MODEL:3 * D_MODEL]
    g = lng_ref[layer:layer + 1, :]
    b = lnb_ref[layer:layer + 1, :]

    def body(i, carry):
        rows = _chunk_rows(i)
        out = _dot(attn_scr[rows, :], w_out_ref[...])
        z = alpha * x_ref[rows, :] + gate * out
        y_ref[rows, :] = _layer_norm_rows(z, g, b)
        return carry

    lax.fori_loop(0, n_rows // ROW_CHUNK, body, 0, unroll=True)


class _ContextWeights:
    def __init__(self, step, w_in, w_out, stage, sems, out_sems):
        self.step, self.w_in, self.w_out, self.stage, self.sems = step, w_in, w_out, stage, sems
        self.out_copies = [pltpu.make_async_copy(w[1], w[2], out_sems.at[n]) for n, w in enumerate((w_in, w_out))]

    def _slab_copies(self, w_hbm):
        n_cols = w_hbm.shape[1]
        return [pltpu.make_async_copy(w_hbm.at[pl.ds(s * W_SLAB_ROWS, W_SLAB_ROWS), :],
                                      self.stage.at[s, :, pl.ds(0, n_cols)], self.sems.at[s])
                for s in range(w_hbm.shape[0] // W_SLAB_ROWS)]

    def _cast(self, w_hbm, w_scr):
        n_cols = w_hbm.shape[1]
        for s, copy in enumerate(self._slab_copies(w_hbm)):
            copy.wait()
            w_scr[pl.ds(s * W_SLAB_ROWS, W_SLAB_ROWS), :] = self.stage[s, :, 0:n_cols].astype(BF16)

    def load_in_proj(self):
        @pl.when(self.step == 0)
        def _():
            for copy in self._slab_copies(self.w_in[0]):
                copy.start()
            self._cast(self.w_in[0], self.w_in[1])
            self.out_copies[0].start()
            for copy in self._slab_copies(self.w_out[0]):
                copy.start()

    def load_out_proj(self):
        @pl.when(self.step == 0)
        def _():
            self._cast(self.w_out[0], self.w_out[1])
            self.out_copies[1].start()

    def finish(self):
        @pl.when(self.step == 0)
        def _():
            for copy in self.out_copies:
                copy.wait()


def _run_pipeline(n_items, stages):
    for u in range(n_items + len(stages) - 1):
        for k, stage in enumerate(stages):
            t = u - k
            if 0 <= t < n_items:
                stage(t, t % 2)


def _attend_blocks(block_stages, n_blocks, n_items, unrolled):
    assert n_items % 2 == 0
    if unrolled:
        per_block = [block_stages(i) for i in range(n_blocks)]
        stages = [lambda g, slot, k=k: per_block[g // n_items][k](g % n_items, slot) for k in range(3)]
        _run_pipeline(n_blocks * n_items, stages)
    else:
        def body(i, carry):
            _run_pipeline(n_items, block_stages(i))
            return carry

        lax.fori_loop(0, n_blocks, body, 0)


def _even_kernel(latent, layer, n_rows, seq, alpha, *refs):
    if latent:
        (x_ref, mod_ref, w_in_ref, w_out_ref, qn_ref, knt_ref, sink_ref, lng_ref, lnb_ref, pm_ref,
         cos_ref, sin_ref, cost_ref, sint_ref, cakt_ref, cav_ref, cbkt_ref, cbv_ref,
         y_ref,
         ha_scr, qa_scr, qb_scr, ka_scr, va_scr, kb_scr, vb_scr, g_scr, s_scr, p_scr, es_scr, wkt_scr, wv_scr,
         bias_scr) = refs
    else:
        (x_ref, mod_ref, w_in_hbm, w_out_hbm, qn_ref, knt_ref, sink_ref, lng_ref, lnb_ref, pm_ref,
         y_ref, nakt_ref, navt_ref, nbkt_ref, nbvt_ref, w_in_bf_hbm, w_out_bf_hbm,
         ha_scr, qa_scr, qb_scr, ka_scr, va_scr, kb_scr, vb_scr, g_scr, s_scr, p_scr, es_scr, wkt_scr,
         wv_scr, w_in_ref, w_out_ref, w_stage, w_sems, w_out_sems) = refs

    step = pl.program_id(0)
    if not latent:
        weights = _ContextWeights(step, (w_in_hbm, w_in_ref, w_in_bf_hbm), (w_out_hbm, w_out_ref, w_out_bf_hbm),
                                  w_stage, w_sems, w_out_sems)
        weights.load_in_proj()
    mod_row = step + 1 if latent else 0
    _modulate(x_ref, mod_ref, mod_row, ha_scr, n_rows)

    col_ka, col_va, col_kb, col_vb = 512, 640, 1792, 1920

    @pl.when(step == 0)
    def _():
        for r, c0 in enumerate((col_ka, col_kb)):
            wkt_scr[LANES * r:LANES * (r + 1), :] = w_in_ref[:, c0:c0 + LANES].T
        wv_scr[:, 0:LANES] = w_in_ref[:, col_va:col_va + LANES]
        wv_scr[:, LANES:2 * LANES] = w_in_ref[:, col_vb:col_vb + LANES]

    n_lat_chunks = seq // ROW_CHUNK
    if latent:
        n_past = cav_ref.shape[2]
        past_rows = pl.ds(seq, n_past)
        _store_kt_variants(ka_scr, n_lat_chunks, cakt_ref[0])
        _store_kt_variants(kb_scr, n_lat_chunks, cbkt_ref[0])
        _store_v_variants(va_scr, past_rows, cav_ref[0].T)
        _store_v_variants(vb_scr, past_rows, cbv_ref[0].T)

    pm = pm_ref[...]
    qn = qn_ref[...]
    knt = knt_ref[...]

    def proj(i, carry):
        rows = _chunk_rows(i)
        hh = ha_scr[rows, :]
        if latent:
            cos = cos_ref[rows, :]
            sin = sin_ref[rows, :]
            rot = lambda a: _rope(a, cos, sin)
            rot_t = lambda a: _rope_t(a, cost_ref[i], sint_ref[i])
        else:
            rot = rot_t = lambda a: a

        acc = _dot(hh, w_in_ref[:, 0:512])
        for j in range(4):
            a = acc[:, LANES * j:LANES * (j + 1)]
            ms = _dot((a * a).astype(BF16), pm)
            a = rot(a * lax.rsqrt(ms + EPS) * qn)
            qa_scr[rows, LANES * j:LANES * (j + 1)] = (a * Q_SCALE).astype(BF16)
        acc = _dot(hh, w_in_ref[:, 1280:1792])
        for j in range(4):
            a = rot(acc[:, LANES * j:LANES * (j + 1)])
            qb_scr[rows, LANES * j:LANES * (j + 1)] = (a * Q_SCALE).astype(BF16)
        g_scr[rows, 0:512] = _silu(_dot(hh, w_in_ref[:, 768:1280]))
        g_scr[rows, 512:1024] = _silu(_dot(hh, w_in_ref[:, 2048:2560]))
        v = _dot(hh, wv_scr[...])
        _store_v_variants(va_scr, rows, v[:, 0:LANES])
        _store_v_variants(vb_scr, rows, v[:, LANES:2 * LANES])

        kt = _dot_nt(wkt_scr[0:2 * LANES, :], hh)
        kat = kt[0:LANES, :]
        ms = _dot(pm, (kat * kat).astype(BF16))
        kat = kat * lax.rsqrt(ms + EPS) * knt
        kbt = kt[LANES:2 * LANES, :]
        if not latent:
            vt = v.T
            nakt_ref[i] = kat
            nbkt_ref[i] = kbt
            navt_ref[i] = vt[0:LANES, :]
            nbvt_ref[i] = vt[LANES:2 * LANES, :]
        _store_kt_variants(ka_scr, i, rot_t(kat))
        _store_kt_variants(kb_scr, i, rot_t(kbt))
        return carry

    lax.fori_loop(0, n_rows // ROW_CHUNK, proj, 0, unroll=2)

    sinks = [sink_ref[h] * LOG2E for h in range(8)]
    ck = ROW_CHUNK
    bk = kb_scr.shape[-1]
    win = ROW_CHUNK + 2 * WINDOW
    n_items = 8

    def block_stages(i):
        rows = _chunk_rows(i)
        if latent:
            a_chunks = list(range(n_lat_chunks + n_past // ck))
            a_keys = pl.ds(0, seq + n_past)
            w0 = jnp.clip(i * (ck // bk) - WINDOW // bk, 0, (seq - win) // bk)
            win_rows = pl.ds(pl.multiple_of(w0 * bk, bk), win)
            dist = (lax.broadcasted_iota(jnp.int32, (ROW_CHUNK, ck), 1)
                    - lax.broadcasted_iota(jnp.int32, (ROW_CHUNK, ck), 0))
            for c in range(win // ck):
                off = w0 * bk + c * ck - i * ck
                bias_scr[c] = jnp.where(jnp.abs(dist + off) <= WINDOW, 0.0, NEG_INF).astype(F32)
            b_first = [w0 + c * (ck // bk) for c in range(win // ck)] + [seq // bk]
            n_biased = win // ck
            b_cols = win + n_past
        else:
            a_chunks = [i]
            a_keys = rows
            b_first = [i * (ck // bk)]
            n_biased = 0
            b_cols = ck
        a_cols = len(a_chunks) * ck

        def qk(t, slot):
            p, branch = divmod(t, 2)
            cols = slice(LANES * p, LANES * (p + 1))
            kvh = p // 2
            q = (qb_scr if branch else qa_scr)[rows, cols]
            for par in (0, 1):
                var = 2 * kvh + par
                if branch:
                    tiles = [jnp.concatenate([kb_scr[var, first + d] for d in range(ck // bk)], axis=1)
                             for first in b_first]
                else:
                    tiles = [ka_scr[var, chunk] for chunk in a_chunks]
                for c, kt in enumerate(tiles):
                    s = _dot(q, kt)
                    if branch and c < n_biased:
                        s = s + bias_scr[c]
                    s_scr[slot, par, :, c * ck:(c + 1) * ck] = s

        def softmax(t, slot):
            p, branch = divmod(t, 2)
            n_cols = b_cols if branch else a_cols
            rb = _softmax_rows(n_cols)
            for par in (0, 1):
                for r in range(ROW_CHUNK // rb):
                    sub = slice(r * rb, (r + 1) * rb)
                    s = s_scr[slot, par, sub, 0:n_cols]
                    m = jnp.max(s, axis=1, keepdims=True)
                    if branch:
                        sink = sinks[2 * p + par]
                        m = jnp.maximum(m, sink)
                        es_scr[slot, sub, HEAD_DIM * par:HEAD_DIM * (par + 1)] = jnp.broadcast_to(
                            jnp.exp2(sink - m), (rb, HEAD_DIM))
                    p_scr[slot, par, sub, 0:n_cols] = jnp.exp2((s - m).astype(BF16))

        def pv(t, slot):
            p, branch = divmod(t, 2)
            kvh = p // 2
            v_scr = vb_scr if branch else va_scr
            accs = []
            for par in (0, 1):
                var = 2 * kvh + par
                if latent and branch:
                    n_loc = win
                    accs.append(_dot(p_scr[slot, par, :, 0:n_loc], v_scr[var, win_rows, :])
                                + _dot(p_scr[slot, par, :, n_loc:b_cols], v_scr[var, past_rows, :]))
                else:
                    accs.append(_dot(p_scr[slot, par, :, 0:a_cols], v_scr[var, a_keys, :]))
            lo = _lane_iota(ROW_CHUNK) < HEAD_DIM
            denom = pltpu.roll(jnp.where(lo, accs[1], accs[0]), HEAD_DIM, 1)
            if branch:
                denom = denom + es_scr[slot]
            o = jnp.where(lo, accs[0], accs[1]) / denom
            ocols = slice(512 * branch + LANES * p, 512 * branch + LANES * (p + 1))
            ha_scr[rows, ocols] = (o * g_scr[rows, ocols]).astype(BF16)

        return qk, softmax, pv

    _attend_blocks(block_stages, n_rows // ROW_CHUNK, n_items, unrolled=not latent)

    if not latent:
        weights.load_out_proj()
    _out_proj_norm(x_ref, mod_ref, mod_row, ha_scr, w_out_ref, lng_ref, lnb_ref, layer, y_ref, n_rows, alpha)
    if not latent:
        weights.finish()


def _odd_kernel(latent, layer, n_rows, seq, alpha, lam_init, *refs):
    if latent:
        (x_ref, mod_ref, w_in_ref, w_out_ref, lq1_ref, lk1_ref, lq2_ref, lk2_ref, sub_ref, lng_ref, lnb_ref,
         cos_ref, sin_ref, cck_ref, ccv_ref,
         y_ref,
         ha_scr, q_scr, k_scr, v_scr, g_scr, s_scr, p_scr) = refs
    else:
        (x_ref, mod_ref, w_in_hbm, w_out_hbm, lq1_ref, lk1_ref, lq2_ref, lk2_ref, sub_ref, lng_ref, lnb_ref,
         y_ref, nck_hbm, ncv_hbm, w_in_bf_hbm, w_out_bf_hbm,
         ha_scr, q_scr, k_scr, v_scr, g_scr, s_scr, p_scr, kv_stage, kv_sems,
         w_in_ref, w_out_ref, w_stage, w_sems, w_out_sems) = refs

    step = pl.program_id(0)
    if not latent:
        weights = _ContextWeights(step, (w_in_hbm, w_in_ref, w_in_bf_hbm), (w_out_hbm, w_out_ref, w_out_bf_hbm),
                                  w_stage, w_sems, w_out_sems)
        weights.load_in_proj()
    mod_row = step + 1 if latent else 0
    _modulate(x_ref, mod_ref, mod_row, ha_scr, n_rows)

    n_heads = D_MODEL // LANES
    n_blocks = n_rows // ROW_CHUNK
    lo = _lane_iota(ROW_CHUNK) < HEAD_DIM

    def kv_out_copies(blk):
        elem = step * n_blocks + blk
        return [pltpu.make_async_copy(kv_stage.at[blk, t, :, pl.ds(LANES * h, LANES)],
                                      out.at[elem, 0, :, h, :], kv_sems.at[blk, t])
                for t, out in enumerate((nck_hbm, ncv_hbm)) for h in range(n_heads)]

    def store_k(rows, h, a):
        cols = slice(LANES * h, LANES * (h + 1))
        zero = jnp.zeros_like(a)
        k_scr[0, rows, cols] = jnp.where(lo, a, zero).astype(BF16)
        k_scr[1, rows, cols] = jnp.where(lo, zero, a).astype(BF16)

    if latent:
        n_past = cck_ref.shape[2]
        past = pl.ds(seq, n_past)
        for h in range(n_heads):
            store_k(past, h, cck_ref[0, 0, :, h, :])
            v_scr[past, LANES * h:LANES * (h + 1)] = ccv_ref[0, 0, :, h, :].astype(BF16)

    def proj(i, carry):
        rows = _chunk_rows(i)
        hh = ha_scr[rows, :]
        if latent:
            cos = cos_ref[rows, :]
            sin = sin_ref[rows, :]
            rot = lambda a: _rope(a, cos, sin)
        else:
            rot = lambda a: a
        for half in range(2):
            acc = _dot(hh, w_in_ref[:, 512 * half:512 * (half + 1)])
            for j in range(4):
                a = rot(acc[:, LANES * j:LANES * (j + 1)])
                cols = slice(512 * half + LANES * j, 512 * half + LANES * (j + 1))
                q_scr[rows, cols] = (a * Q_SCALE).astype(BF16)
        for half in range(2):
            acc = _dot(hh, w_in_ref[:, 1024 + 512 * half:1024 + 512 * (half + 1)])
            if not latent:
                kv_stage[i, 0, :, 512 * half:512 * (half + 1)] = acc
            for j in range(4):
                store_k(rows, 4 * half + j, rot(acc[:, LANES * j:LANES * (j + 1)]))
        for half in range(2):
            acc = _dot(hh, w_in_ref[:, 2048 + 512 * half:2048 + 512 * (half + 1)])
            if not latent:
                kv_stage[i, 1, :, 512 * half:512 * (half + 1)] = acc
            v_scr[rows, 512 * half:512 * (half + 1)] = acc.astype(BF16)
        if not latent:
            for copy in kv_out_copies(i):
                copy.start()
        for half in range(2):
            acc = _dot(hh, w_in_ref[:, 3072 + 512 * half:3072 + 512 * (half + 1)])
            g_scr[rows, 512 * half:512 * (half + 1)] = _silu(acc)
        return carry

    if latent:
        lax.fori_loop(0, n_blocks, proj, 0, unroll=2)
    else:
        for blk in range(n_blocks):
            proj(blk, 0)

    lam = (jnp.exp(jnp.sum(lq1_ref[...] * lk1_ref[...], axis=1, keepdims=True))
           - jnp.exp(jnp.sum(lq2_ref[...] * lk2_ref[...], axis=1, keepdims=True)) + lam_init)
    sub = sub_ref[...] * (1.0 - lam_init)
    n_keys = seq + n_past if latent else ROW_CHUNK
    rb = _softmax_rows(n_keys)
    ones = jnp.ones((n_keys, LANES), BF16)

    def block_stages(i):
        rows = _chunk_rows(i)
        keys = pl.ds(0, n_keys) if latent else rows

        def qk(h, slot):
            cols = slice(LANES * h, LANES * (h + 1))
            q = q_scr[rows, cols]
            for m in (0, 1):
                s_scr[slot, m] = _dot_nt(q, k_scr[m, keys, cols])

        def softmax(h, slot):
            for m in (0, 1):
                for r in range(ROW_CHUNK // rb):
                    sub_rows = slice(r * rb, (r + 1) * rb)
                    s = s_scr[slot, m, sub_rows, :]
                    top = jnp.max(s, axis=1, keepdims=True)
                    p_scr[slot, m, sub_rows, :] = jnp.exp2((s - top).astype(BF16))

        def pv(h, slot):
            cols = slice(LANES * h, LANES * (h + 1))
            v_ext = jnp.concatenate([v_scr[keys, cols], ones], axis=1)
            maps = []
            for m in (0, 1):
                acc = _dot(p_scr[slot, m], v_ext)
                maps.append(acc[:, 0:LANES] / acc[:, LANES:2 * LANES])
            o = maps[0] - lam * maps[1]
            ms = jnp.mean(o * o, axis=1, keepdims=True)
            o = o * lax.rsqrt(ms + EPS) * sub
            ha_scr[rows, cols] = (o * g_scr[rows, cols]).astype(BF16)

        return qk, softmax, pv

    _attend_blocks(block_stages, n_blocks, n_heads, unrolled=not latent)

    if not latent:
        weights.load_out_proj()
    _out_proj_norm(x_ref, mod_ref, mod_row, ha_scr, w_out_ref, lng_ref, lnb_ref, layer, y_ref, n_rows, alpha)

    if not latent:
        for blk in range(n_blocks):
            for copy in kv_out_copies(blk):
                copy.wait()
        weights.finish()


MOD_SLAB_ROWS = 128
MOD_COL_BLOCK = 1024
MOD_SLOTS = 4


def _mod_kernel(n_cond, cvb_ref, w_hbm, b_ref, o_ref, sb_scr, ring, acc_scr, sems):
    depth, n_in, n_out = w_hbm.shape
    sublanes = 8
    slabs_per_layer = n_in // MOD_SLAB_ROWS
    slabs = [(l, rs) for l in range(depth) for rs in range(slabs_per_layer)]

    def slab_copy(n):
        l, rs = slabs[n]
        return pltpu.make_async_copy(w_hbm.at[l, pl.ds(rs * MOD_SLAB_ROWS, MOD_SLAB_ROWS), :],
                                     ring.at[n % MOD_SLOTS], sems.at[n % MOD_SLOTS])

    for n in range(min(MOD_SLOTS, len(slabs))):
        slab_copy(n).start()
    sb_scr[...] = _silu(cvb_ref[...])

    for n, (l, rs) in enumerate(slabs):
        slab_copy(n).wait()
        for cb in range(n_out // MOD_COL_BLOCK):
            cols = pl.ds(cb * MOD_COL_BLOCK, MOD_COL_BLOCK)
            if rs == 0:
                accs = (jnp.zeros((sublanes, MOD_COL_BLOCK), F32),) * n_cond
            else:
                accs = tuple(acc_scr[r, :, cols] for r in range(n_cond))

            def body(kb, accs, n=n, rs=rs, cols=cols):
                w = ring[n % MOD_SLOTS, pl.ds(pl.multiple_of(kb * sublanes, sublanes), sublanes), cols]
                s_rows = pl.ds(pl.multiple_of(rs * MOD_SLAB_ROWS + kb * sublanes, sublanes), sublanes)
                return tuple(acc + w * jnp.tile(sb_scr[r, s_rows, :], (1, MOD_COL_BLOCK // LANES))
                             for r, acc in enumerate(accs))

            accs = lax.fori_loop(0, MOD_SLAB_ROWS // sublanes, body, accs, unroll=8)
            if rs < slabs_per_layer - 1:
                for r in range(n_cond):
                    acc_scr[r, :, cols] = accs[r]
            else:
                rows = [jnp.sum(acc, axis=0, keepdims=True) + b_ref[l:l + 1, cols] for acc in accs]
                o_ref[l, :, cols] = jnp.concatenate(rows + [jnp.zeros((8 - n_cond, MOD_COL_BLOCK), F32)], axis=0)
        if n + MOD_SLOTS < len(slabs):
            slab_copy(n + MOD_SLOTS).start()


def _full(shape, **kw):
    zeros = (0,) * len(shape)
    return pl.BlockSpec(shape, lambda i: zeros, **kw)


def _weight_specs(latent, w_in, w_out):
    if latent:
        single = pl.Buffered(1)
        return [_full(w_in.shape, pipeline_mode=single), _full(w_out.shape, pipeline_mode=single)]
    return [pl.BlockSpec(memory_space=pl.ANY), pl.BlockSpec(memory_space=pl.ANY)]


def _weight_scratch(w_in, w_out):
    assert w_in.shape[0] % W_SLAB_ROWS == 0 and w_out.shape[0] % W_SLAB_ROWS == 0
    n_slabs = max(w_in.shape[0], w_out.shape[0]) // W_SLAB_ROWS
    return [pltpu.VMEM(w_in.shape, BF16), pltpu.VMEM(w_out.shape, BF16),
            pltpu.VMEM((n_slabs, W_SLAB_ROWS, max(w_in.shape[1], w_out.shape[1])), F32),
            pltpu.SemaphoreType.DMA((n_slabs,)), pltpu.SemaphoreType.DMA((2,))]


def _rope_tables(seq):
    t = np.arange(seq)
    n_freq = HEAD_DIM // 4
    freqs = ROPE_THETA ** (-np.arange(n_freq, dtype=np.float64) / n_freq)
    ang_row = (t // GRID_W)[:, None] * freqs
    ang_col = (t % GRID_W)[:, None] * freqs
    ang = np.concatenate([ang_row, ang_row, ang_col, ang_col], axis=1)
    sign = np.concatenate([-np.ones(n_freq), np.ones(n_freq)] * 2)[None, :]
    cos = np.tile(np.cos(ang), (1, 2)).astype(np.float32)
    sin = np.tile(np.sin(ang) * sign, (1, 2)).astype(np.float32)
    chunked_t = lambda a: a.reshape(seq // ROW_CHUNK, ROW_CHUNK, LANES).transpose(0, 2, 1)
    return jnp.asarray(cos), jnp.asarray(sin), jnp.asarray(chunked_t(cos)), jnp.asarray(chunked_t(sin))


def _head_mean_matrix():
    idx = np.arange(LANES) // HEAD_DIM
    return jnp.asarray((idx[:, None] == idx[None, :]).astype(np.float32) / HEAD_DIM, dtype=BF16)


def _modulation(c, c_ctx, w_mod, b_mod):
    depth = w_mod.shape[0]
    cv = jnp.concatenate([c_ctx[None, :], c], axis=0)
    n_cond = cv.shape[0]
    cvb = jnp.broadcast_to(cv[:, :, None], (n_cond, D_MODEL, LANES))
    assert D_MODEL % MOD_SLAB_ROWS == 0 and (3 * D_MODEL) % MOD_COL_BLOCK == 0
    return pl.pallas_call(
        functools.partial(_mod_kernel, n_cond),
        grid=(1,),
        in_specs=[_full(cvb.shape), pl.BlockSpec(memory_space=pl.ANY), _full(b_mod.shape)],
        out_specs=_full((depth, 8, 3 * D_MODEL)),
        out_shape=jax.ShapeDtypeStruct((depth, 8, 3 * D_MODEL), F32),
        scratch_shapes=[pltpu.VMEM((n_cond, D_MODEL, LANES), F32),
                        pltpu.VMEM((MOD_SLOTS, MOD_SLAB_ROWS, 3 * D_MODEL), F32),
                        pltpu.VMEM((n_cond, 8, 3 * D_MODEL), F32),
                        pltpu.SemaphoreType.DMA((MOD_SLOTS,))],
        compiler_params=pltpu.CompilerParams(dimension_semantics=("arbitrary",)),
        name="adaln_modulation",
    )(cvb, w_mod, b_mod)


def _even_layer(x, mod, layer, w_in, w_out, q_norm, k_norm, sink, ln_g, ln_b, latent, seq, n_rows, alpha, extras=()):
    total = x.shape[0]
    grid = (total // n_rows,)
    single = pl.Buffered(1)
    qn = jnp.tile(q_norm, 2)[None, :]
    knt = jnp.broadcast_to(jnp.tile(k_norm, 2)[:, None], (LANES, ROW_CHUNK))

    row_blk = lambda width: pl.BlockSpec((n_rows, width), lambda i: (i, 0))
    in_specs = [row_blk(D_MODEL),
                pl.BlockSpec((1, 8, 3 * D_MODEL), lambda i: (layer, 0, 0)),
                *_weight_specs(latent, w_in, w_out),
                _full((1, LANES)), _full((LANES, ROW_CHUNK)),
                pl.BlockSpec(memory_space=pltpu.SMEM),
                _full(ln_g.shape), _full(ln_b.shape),
                _full((LANES, LANES))]
    args = [x, mod, w_in, w_out, qn, knt, sink, ln_g, ln_b, _head_mean_matrix()]
    y_shape = jax.ShapeDtypeStruct((total, D_MODEL), F32)
    n_blocks = n_rows // ROW_CHUNK
    if latent:
        cos, sin, cos_t, sin_t, cakt, cav, cbkt, cbv = extras
        n_past = cav.shape[2]
        in_specs += [_full(cos.shape, pipeline_mode=single), _full(sin.shape, pipeline_mode=single),
                     _full(cos_t.shape, pipeline_mode=single), _full(sin_t.shape, pipeline_mode=single)]
        in_specs += [pl.BlockSpec((1, LANES, n_past), lambda i: (i, 0, 0))] * 4
        args += [cos, sin, cos_t, sin_t, cakt, cav, cbkt, cbv]
        out_specs = row_blk(D_MODEL)
        out_shape = y_shape
        n_keys = seq + n_past
    else:
        kv_blk = pl.BlockSpec((n_blocks, LANES, ROW_CHUNK), lambda i: (i, 0, 0))
        hbm = pl.BlockSpec(memory_space=pl.ANY)
        out_specs = [row_blk(D_MODEL)] + [kv_blk] * 4 + [hbm, hbm]
        out_shape = ([y_shape] + [jax.ShapeDtypeStruct((total // seq, LANES, seq), F32)] * 4
                     + [jax.ShapeDtypeStruct(w_in.shape, BF16), jax.ShapeDtypeStruct(w_out.shape, BF16)])
        n_keys = n_rows
    n_kchunks = n_keys // ROW_CHUNK
    n_cols = n_keys if latent else ROW_CHUNK
    scratch = [pltpu.VMEM((n_rows, D_MODEL), BF16),
               pltpu.VMEM((n_rows, 512), BF16), pltpu.VMEM((n_rows, 512), BF16),
               pltpu.VMEM((4, n_kchunks, LANES, ROW_CHUNK), BF16), pltpu.VMEM((4, n_keys, LANES), BF16),
               pltpu.VMEM((4, n_keys // WINDOW, LANES, WINDOW), BF16), pltpu.VMEM((4, n_keys, LANES), BF16),
               pltpu.VMEM((n_rows, D_MODEL), F32),
               pltpu.VMEM((2, 2, ROW_CHUNK, n_cols), F32),
               pltpu.VMEM((2, 2, ROW_CHUNK, n_cols), BF16),
               pltpu.VMEM((2, ROW_CHUNK, LANES), F32),
               pltpu.VMEM((2 * LANES, D_MODEL), BF16),
               pltpu.VMEM((D_MODEL, 2 * LANES), BF16)]
    if latent:
        scratch.append(pltpu.VMEM((1 + 2 * WINDOW // ROW_CHUNK, ROW_CHUNK, ROW_CHUNK), F32))
    else:
        scratch += _weight_scratch(w_in, w_out)
    return pl.pallas_call(
        functools.partial(_even_kernel, latent, layer, n_rows, seq, alpha),
        grid=grid, in_specs=in_specs, out_specs=out_specs, out_shape=out_shape,
        scratch_shapes=scratch,
        compiler_params=pltpu.CompilerParams(dimension_semantics=("arbitrary",), vmem_limit_bytes=VMEM_LIMIT),
        name="even_layer_latent" if latent else "even_layer_context",
    )(*args)


def _odd_layer(x, mod, layer, w_in, w_out, lams, sub, ln_g, ln_b, latent, seq, n_rows, alpha, lam_init, extras=()):
    total = x.shape[0]
    grid = (total // n_rows,)
    row_blk = lambda width: pl.BlockSpec((n_rows, width), lambda i: (i, 0))
    single = pl.Buffered(1)
    in_specs = [row_blk(D_MODEL),
                pl.BlockSpec((1, 8, 3 * D_MODEL), lambda i: (layer, 0, 0)),
                *_weight_specs(latent, w_in, w_out),
                _full((1, HEAD_DIM)), _full((1, HEAD_DIM)), _full((1, HEAD_DIM)), _full((1, HEAD_DIM)),
                _full((1, LANES)),
                _full(ln_g.shape), _full(ln_b.shape)]
    args = [x, mod, w_in, w_out, *lams, sub, ln_g, ln_b]
    y_shape = jax.ShapeDtypeStruct((total, D_MODEL), F32)
    n_heads = D_MODEL // LANES
    n_blocks = n_rows // ROW_CHUNK
    if latent:
        cos, sin, cck, ccv = extras
        n_past = cck.shape[2]
        in_specs += [_full(cos.shape, pipeline_mode=single), _full(sin.shape, pipeline_mode=single)]
        in_specs += [pl.BlockSpec((1, 1, n_past, n_heads, LANES), lambda i: (i, layer // 2, 0, 0, 0))] * 2
        args += [cos, sin, cck, ccv]
        out_specs = row_blk(D_MODEL)
        out_shape = y_shape
        n_keys = seq + n_past
    else:
        hbm = pl.BlockSpec(memory_space=pl.ANY)
        out_specs = [row_blk(D_MODEL), hbm, hbm, hbm, hbm]
        out_shape = ([y_shape] + [jax.ShapeDtypeStruct((total // seq, 1, seq, n_heads, LANES), F32)] * 2
                     + [jax.ShapeDtypeStruct(w_in.shape, BF16), jax.ShapeDtypeStruct(w_out.shape, BF16)])
        n_keys = n_rows
    n_cols = n_keys if latent else ROW_CHUNK
    scratch = [pltpu.VMEM((n_rows, D_MODEL), BF16),
               pltpu.VMEM((n_rows, D_MODEL), BF16),
               pltpu.VMEM((2, n_keys, D_MODEL), BF16),
               pltpu.VMEM((n_keys, D_MODEL), BF16),
               pltpu.VMEM((n_rows, D_MODEL), F32),
               pltpu.VMEM((2, 2, ROW_CHUNK, n_cols), F32),
               pltpu.VMEM((2, 2, ROW_CHUNK, n_cols), BF16)]
    if not latent:
        scratch += [pltpu.VMEM((n_blocks, 2, ROW_CHUNK, D_MODEL), F32),
                    pltpu.SemaphoreType.DMA((n_blocks, 2))]
        scratch += _weight_scratch(w_in, w_out)
    return pl.pallas_call(
        functools.partial(_odd_kernel, latent, layer, n_rows, seq, alpha, lam_init),
        grid=grid, in_specs=in_specs, out_specs=out_specs, out_shape=out_shape,
        scratch_shapes=scratch,
        compiler_params=pltpu.CompilerParams(dimension_semantics=("arbitrary",), vmem_limit_bytes=VMEM_LIMIT),
        name="odd_layer_latent" if latent else "odd_layer_context",
    )(*args)


def kernel(x_prompt, x_sample, cache_a_k, cache_a_v, cache_b_k, cache_b_v, cache_c_k, cache_c_v, c, c_ctx,
           w_mod, b_mod, ln_g, ln_b, w_in_even, w_out_even, q_norm_a, k_norm_a, sink_b, w_in_odd, w_out_odd,
           lambda_q1, lambda_k1, lambda_q2, lambda_k2, subln_c):
    depth = w_mod.shape[0]
    batch, seq, _ = x_prompt.shape
    dec_batch, dec_seq, _ = x_sample.shape
    n_past = cache_a_k.shape[2]
    alpha = (2 * depth) ** 0.25
    assert seq == ROW_CHUNK and n_past % ROW_CHUNK == 0 and dec_seq % ROW_CHUNK == 0

    mod = _modulation(c, c_ctx, w_mod, b_mod)
    cos, sin, cos_t, sin_t = _rope_tables(dec_seq)

    bf16_weights = {}

    def run(x, latent, n_batch, s, rows_even, rows_odd):
        kv = {"a_k": [], "a_v": [], "b_k": [], "b_v": [], "c_k": [], "c_v": []}
        for l in range(depth):
            if l % 2 == 0:
                e = l // 2
                extras = ()
                if latent:
                    k_t = lambda t: t[:, e].transpose(0, 2, 3, 1).reshape(n_batch, LANES, n_past)
                    extras = (cos, sin, cos_t, sin_t,
                              k_t(cache_a_k), k_t(cache_a_v), k_t(cache_b_k), k_t(cache_b_v))
                w_in, w_out = bf16_weights[l] if latent else (w_in_even[e], w_out_even[e])
                res = _even_layer(x, mod, l, w_in, w_out, q_norm_a[e], k_norm_a[e],
                                  sink_b[e], ln_g, ln_b, latent, s, rows_even, alpha, extras)
                if latent:
                    x = res
                else:
                    x = res[0]
                    bf16_weights[l] = res[5:7]
                    for name, t in zip(("a_k", "a_v", "b_k", "b_v"), res[1:5]):
                        kv[name].append(t.reshape(n_batch, 2, HEAD_DIM, s).transpose(0, 3, 1, 2))
            else:
                o = l // 2
                lam_init = 0.8 - 0.6 * math.exp(-0.3 * l)
                extras = (cos, sin, cache_c_k, cache_c_v) if latent else ()
                lams = [t[o][None, :] for t in (lambda_q1, lambda_k1, lambda_q2, lambda_k2)]
                w_in, w_out = bf16_weights[l] if latent else (w_in_odd[o], w_out_odd[o])
                res = _odd_layer(x, mod, l, w_in, w_out, lams,
                                 subln_c[o][None, :], ln_g, ln_b, latent, s, rows_odd, alpha, lam_init, extras)
                if latent:
                    x = res
                else:
                    x = res[0]
                    bf16_weights[l] = res[3:5]
                    kv["c_k"].append(res[1][:, 0])
                    kv["c_v"].append(res[2][:, 0])
        return x, kv

    y_ctx, kv = run(x_prompt.reshape(batch * seq, D_MODEL), False, batch, seq, 1024, 512)
    y_lat, _ = run(x_sample.reshape(dec_batch * dec_seq, D_MODEL), True, dec_batch, dec_seq, dec_seq, dec_seq)

    stack = lambda name: jnp.stack(kv[name], axis=1)
    return (y_ctx.reshape(batch, seq, D_MODEL), y_lat.reshape(dec_batch, dec_seq, D_MODEL),
            stack("a_k"), stack("a_v"), stack("b_k"), stack("b_v"), stack("c_k"), stack("c_v"))
```

```python
import functools
import math

import jax
import jax.numpy as jnp
import numpy as np
from jax import lax
from jax.experimental import pallas as pl
from jax.experimental.pallas import tpu as pltpu

F32 = jnp.float32
BF16 = jnp.bfloat16

D_MODEL = 1024
HEAD_DIM = 64
GRID_W = 64
WINDOW = 128
ROPE_THETA = 10000.0
EPS = 1e-6
NEG_INF = -1e30
LOG2E = 1.4426950408889634
Q_SCALE = HEAD_DIM ** -0.5 * LOG2E
LANES = 128
ROW_CHUNK = 256
SOFTMAX_VREGS = 40
VMEM_LIMIT = 60000 * 1024
W_SLAB_ROWS = 128


def _silu(x):
    return x / (1.0 + jnp.exp(-x))


def _dot(a, b):
    return jnp.dot(a, b, preferred_element_type=F32)


def _dot_nt(a, b):
    return lax.dot_general(a, b, (((1,), (1,)), ((), ())), preferred_element_type=F32)


def _lane_iota(rows):
    return lax.broadcasted_iota(jnp.int32, (rows, LANES), 1)


def _chunk_rows(i):
    if isinstance(i, int):
        return pl.ds(i * ROW_CHUNK, ROW_CHUNK)
    return pl.ds(pl.multiple_of(i * ROW_CHUNK, ROW_CHUNK), ROW_CHUNK)


def _softmax_rows(n_cols):
    rows = 8
    while rows * 2 * n_cols <= SOFTMAX_VREGS * 1024 and rows * 2 <= ROW_CHUNK:
        rows *= 2
    return rows


def _rope(a, cos, sin_signed):
    lane = _lane_iota(a.shape[0])
    fwd = pltpu.roll(a, LANES - 16, 1)
    bwd = pltpu.roll(a, 16, 1)
    partner = jnp.where((lane & 16) == 0, fwd, bwd)
    return a * cos + partner * sin_signed


def _rope_t(a, cos_t, sin_t):
    blocks = [a[16 * b:16 * (b + 1), :] for b in range(a.shape[0] // 16)]
    partner = jnp.concatenate([blocks[b ^ 1] for b in range(len(blocks))], axis=0)
    return a * cos_t + partner * sin_t


def _store_kt_variants(scr, chunk, kt):
    width = scr.shape[-1]
    per_block = kt.shape[1] // width
    zero = jnp.zeros((HEAD_DIM, kt.shape[1]), F32)
    for j in range(2):
        kj = kt[HEAD_DIM * j:HEAD_DIM * (j + 1), :]
        for par, full in enumerate((jnp.concatenate([kj, zero], axis=0), jnp.concatenate([zero, kj], axis=0))):
            full = full.astype(BF16)
            for c in range(per_block):
                scr[2 * j + par, chunk * per_block + c] = full[:, width * c:width * (c + 1)]


def _store_v_variants(scr, rows, a):
    lane = _lane_iota(a.shape[0])
    lo = lane < HEAD_DIM
    swapped = pltpu.roll(a, HEAD_DIM, 1)
    one = jnp.ones_like(a)
    scr[0, rows, :] = jnp.where(lo, a, one).astype(BF16)
    scr[1, rows, :] = jnp.where(lo, one, swapped).astype(BF16)
    scr[2, rows, :] = jnp.where(lo, swapped, one).astype(BF16)
    scr[3, rows, :] = jnp.where(lo, one, a).astype(BF16)


def _layer_norm_rows(z, g, b):
    mu = jnp.mean(z, axis=-1, keepdims=True)
    zc = z - mu
    var = jnp.mean(zc * zc, axis=-1, keepdims=True)
    return zc * lax.rsqrt(var + EPS) * g + b


def _modulate(x_ref, mod_ref, mod_row, h_scr, n_rows):
    shift = mod_ref[0, pl.ds(mod_row, 1), 0:D_MODEL]
    scale = mod_ref[0, pl.ds(mod_row, 1), D_MODEL:2 * D_MODEL]

    def body(i, carry):
        rows = _chunk_rows(i)
        h_scr[rows, :] = (x_ref[rows, :] * (1.0 + scale) + shift).astype(BF16)
        return carry

    lax.fori_loop(0, n_rows // ROW_CHUNK, body, 0)


def _out_proj_norm(x_ref, mod_ref, mod_row, attn_scr, w_out_ref, lng_ref, lnb_ref, layer, y_ref, n_rows, alpha):
    gate = mod_ref[0, pl.ds(mod_row, 1), 2 * D_MODEL:3 * D_MODEL]
    g = lng_ref[layer:layer + 1, :]
    b = lnb_ref[layer:layer + 1, :]

    def body(i, carry):
        rows = _chunk_rows(i)
        out = _dot(attn_scr[rows, :], w_out_ref[...])
        z = alpha * x_ref[rows, :] + gate * out
        y_ref[rows, :] = _layer_norm_rows(z, g, b)
        return carry

    lax.fori_loop(0, n_rows // ROW_CHUNK, body, 0, unroll=True)


class _ContextWeights:
    def __init__(self, step, w_in, w_out, stage, sems, out_sems):
        self.step, self.w_in, self.w_out, self.stage, self.sems = step, w_in, w_out, stage, sems
        self.out_copies = [pltpu.make_async_copy(w[1], w[2], out_sems.at[n]) for n, w in enumerate((w_in, w_out))]

    def _slab_copies(self, w_hbm):
        n_cols = w_hbm.shape[1]
        return [pltpu.make_async_copy(w_hbm.at[pl.ds(s * W_SLAB_ROWS, W_SLAB_ROWS), :],
                                      self.stage.at[s, :, pl.ds(0, n_cols)], self.sems.at[s])
                for s in range(w_hbm.shape[0] // W_SLAB_ROWS)]

    def _cast(self, w_hbm, w_scr):
        n_cols = w_hbm.shape[1]
        for s, copy in enumerate(self._slab_copies(w_hbm)):
            copy.wait()
            w_scr[pl.ds(s * W_SLAB_ROWS, W_SLAB_ROWS), :] = self.stage[s, :, 0:n_cols].astype(BF16)

    def load_in_proj(self):
        @pl.when(self.step == 0)
        def _():
            for copy in self._slab_copies(self.w_in[0]):
                copy.start()
            self._cast(self.w_in[0], self.w_in[1])
            self.out_copies[0].start()
            for copy in self._slab_copies(self.w_out[0]):
                copy.start()

    def load_out_proj(self):
        @pl.when(self.step == 0)
        def _():
            self._cast(self.w_out[0], self.w_out[1])
            self.out_copies[1].start()

    def finish(self):
        @pl.when(self.step == 0)
        def _():
            for copy in self.out_copies:
                copy.wait()


def _run_pipeline(n_items, stages):
    for u in range(n_items + len(stages) - 1):
        for k, stage in enumerate(stages):
            t = u - k
            if 0 <= t < n_items:
                stage(t, t % 2)


def _attend_blocks(block_stages, n_blocks, n_items, unrolled):
    assert n_items % 2 == 0
    if unrolled:
        per_block = [block_stages(i) for i in range(n_blocks)]
        stages = [lambda g, slot, k=k: per_block[g // n_items][k](g % n_items, slot) for k in range(3)]
        _run_pipeline(n_blocks * n_items, stages)
    else:
        def body(i, carry):
            _run_pipeline(n_items, block_stages(i))
            return carry

        lax.fori_loop(0, n_blocks, body, 0)


def _even_kernel(latent, layer, n_rows, seq, alpha, *refs):
    if latent:
        (x_ref, mod_ref, w_in_ref, w_out_ref, qn_ref, knt_ref, sink_ref, lng_ref, lnb_ref,
         cos_ref, sin_ref, cost_ref, sint_ref, cakt_ref, cav_ref, cbkt_ref, cbv_ref,
         y_ref,
         ha_scr, qa_scr, qb_scr, ka_scr, va_scr, kb_scr, vb_scr, g_scr, s_scr, p_scr, es_scr, wkt_scr, wv_scr,
         bias_scr) = refs
    else:
        (x_ref, mod_ref, w_in_hbm, w_out_hbm, qn_ref, knt_ref, sink_ref, lng_ref, lnb_ref,
         y_ref, nakt_ref, navt_ref, nbkt_ref, nbvt_ref, w_in_bf_hbm, w_out_bf_hbm,
         ha_scr, qa_scr, qb_scr, ka_scr, va_scr, kb_scr, vb_scr, g_scr, s_scr, p_scr, es_scr, wkt_scr,
         wv_scr, w_in_ref, w_out_ref, w_stage, w_sems, w_out_sems) = refs

    step = pl.program_id(0)
    if not latent:
        weights = _ContextWeights(step, (w_in_hbm, w_in_ref, w_in_bf_hbm), (w_out_hbm, w_out_ref, w_out_bf_hbm),
                                  w_stage, w_sems, w_out_sems)
        weights.load_in_proj()
    mod_row = step + 1 if latent else 0
    _modulate(x_ref, mod_ref, mod_row, ha_scr, n_rows)

    col_ka, col_va, col_kb, col_vb = 512, 640, 1792, 1920

    @pl.when(step == 0)
    def _():
        for r, c0 in enumerate((col_ka, col_kb)):
            wkt_scr[LANES * r:LANES * (r + 1), :] = w_in_ref[:, c0:c0 + LANES].T
        wv_scr[:, 0:LANES] = w_in_ref[:, col_va:col_va + LANES]
        wv_scr[:, LANES:2 * LANES] = w_in_ref[:, col_vb:col_vb + LANES]

    n_lat_chunks = seq // ROW_CHUNK
    if latent:
        n_past = cav_ref.shape[2]
        past_rows = pl.ds(seq, n_past)
        _store_kt_variants(ka_scr, n_lat_chunks, cakt_ref[0])
        _store_kt_variants(kb_scr, n_lat_chunks, cbkt_ref[0])
        _store_v_variants(va_scr, past_rows, cav_ref[0].T)
        _store_v_variants(vb_scr, past_rows, cbv_ref[0].T)

    qn = qn_ref[...]
    knt = knt_ref[...]

    def proj(i, carry):
        rows = _chunk_rows(i)
        hh = ha_scr[rows, :]
        if latent:
            cos = cos_ref[rows, :]
            sin = sin_ref[rows, :]
            rot = lambda a: _rope(a, cos, sin)
            rot_t = lambda a: _rope_t(a, cost_ref[i], sint_ref[i])
        else:
            rot = rot_t = lambda a: a

        acc = _dot(hh, w_in_ref[:, 0:512])
        lo_lanes = _lane_iota(ROW_CHUNK) < HEAD_DIM
        for j in range(4):
            a = acc[:, LANES * j:LANES * (j + 1)]
            sq = a * a
            first = jnp.sum(jnp.where(lo_lanes, sq, 0.0), axis=1, keepdims=True)
            second = jnp.sum(jnp.where(lo_lanes, 0.0, sq), axis=1, keepdims=True)
            ms = jnp.where(lo_lanes, first, second) * (1.0 / HEAD_DIM)
            a = rot(a * lax.rsqrt(ms + EPS) * qn)
            qa_scr[rows, LANES * j:LANES * (j + 1)] = (a * Q_SCALE).astype(BF16)
        acc = _dot(hh, w_in_ref[:, 1280:1792])
        for j in range(4):
            a = rot(acc[:, LANES * j:LANES * (j + 1)])
            qb_scr[rows, LANES * j:LANES * (j + 1)] = (a * Q_SCALE).astype(BF16)
        g_scr[rows, 0:512] = _silu(_dot(hh, w_in_ref[:, 768:1280]))
        g_scr[rows, 512:1024] = _silu(_dot(hh, w_in_ref[:, 2048:2560]))
        v = _dot(hh, wv_scr[...])
        _store_v_variants(va_scr, rows, v[:, 0:LANES])
        _store_v_variants(vb_scr, rows, v[:, LANES:2 * LANES])

        kt = _dot_nt(wkt_scr[0:2 * LANES, :], hh)
        heads = [kt[HEAD_DIM * h:HEAD_DIM * (h + 1), :] for h in range(2)]
        kat = jnp.concatenate([blk * lax.rsqrt(jnp.mean(blk * blk, axis=0, keepdims=True) + EPS) for blk in heads],
                              axis=0) * knt
        kbt = kt[LANES:2 * LANES, :]
        if not latent:
            vt = v.T
            nakt_ref[i] = kat
            nbkt_ref[i] = kbt
            navt_ref[i] = vt[0:LANES, :]
            nbvt_ref[i] = vt[LANES:2 * LANES, :]
        _store_kt_variants(ka_scr, i, rot_t(kat))
        _store_kt_variants(kb_scr, i, rot_t(kbt))
        return carry

    lax.fori_loop(0, n_rows // ROW_CHUNK, proj, 0, unroll=2)

    sinks = [sink_ref[h] * LOG2E for h in range(8)]
    ck = ROW_CHUNK
    bk = kb_scr.shape[-1]
    win = ROW_CHUNK + 2 * WINDOW
    n_items = 8

    def block_stages(i):
        rows = _chunk_rows(i)
        if latent:
            a_chunks = list(range(n_lat_chunks + n_past // ck))
            a_keys = pl.ds(0, seq + n_past)
            w0 = jnp.clip(i * (ck // bk) - WINDOW // bk, 0, (seq - win) // bk)
            win_rows = pl.ds(pl.multiple_of(w0 * bk, bk), win)
            dist = (lax.broadcasted_iota(jnp.int32, (ROW_CHUNK, ck), 1)
                    - lax.broadcasted_iota(jnp.int32, (ROW_CHUNK, ck), 0))
            for c in range(win // ck):
                off = w0 * bk + c * ck - i * ck
                bias_scr[c] = jnp.where(jnp.abs(dist + off) <= WINDOW, 0.0, NEG_INF).astype(F32)
            b_first = [w0 + c * (ck // bk) for c in range(win // ck)] + [seq // bk]
            n_biased = win // ck
            b_cols = win + n_past
        else:
            a_chunks = [i]
            a_keys = rows
            b_first = [i * (ck // bk)]
            n_biased = 0
            b_cols = ck
        a_cols = len(a_chunks) * ck

        def qk(t, slot):
            p, branch = divmod(t, 2)
            cols = slice(LANES * p, LANES * (p + 1))
            kvh = p // 2
            q = (qb_scr if branch else qa_scr)[rows, cols]
            for par in (0, 1):
                var = 2 * kvh + par
                if branch:
                    tiles = [jnp.concatenate([kb_scr[var, first + d] for d in range(ck // bk)], axis=1)
                             for first in b_first]
                else:
                    tiles = [ka_scr[var, chunk] for chunk in a_chunks]
                for c, kt in enumerate(tiles):
                    s = _dot(q, kt)
                    if branch and c < n_biased:
                        s = s + bias_scr[c]
                    s_scr[slot, par, :, c * ck:(c + 1) * ck] = s

        def softmax(t, slot):
            p, branch = divmod(t, 2)
            n_cols = b_cols if branch else a_cols
            rb = _softmax_rows(n_cols)
            for par in (0, 1):
                for r in range(ROW_CHUNK // rb):
                    sub = slice(r * rb, (r + 1) * rb)
                    s = s_scr[slot, par, sub, 0:n_cols]
                    m = jnp.max(s, axis=1, keepdims=True)
                    if branch:
                        sink = sinks[2 * p + par]
                        m = jnp.maximum(m, sink)
                        es_scr[slot, sub, HEAD_DIM * par:HEAD_DIM * (par + 1)] = jnp.broadcast_to(
                            jnp.exp2(sink - m), (rb, HEAD_DIM))
                    p_scr[slot, par, sub, 0:n_cols] = jnp.exp2((s - m).astype(BF16))

        def pv(t, slot):
            p, branch = divmod(t, 2)
            kvh = p // 2
            v_scr = vb_scr if branch else va_scr
            accs = []
            for par in (0, 1):
                var = 2 * kvh + par
                if latent and branch:
                    n_loc = win
                    accs.append(_dot(p_scr[slot, par, :, 0:n_loc], v_scr[var, win_rows, :])
                                + _dot(p_scr[slot, par, :, n_loc:b_cols], v_scr[var, past_rows, :]))
                else:
                    accs.append(_dot(p_scr[slot, par, :, 0:a_cols], v_scr[var, a_keys, :]))
            lo = _lane_iota(ROW_CHUNK) < HEAD_DIM
            denom = pltpu.roll(jnp.where(lo, accs[1], accs[0]), HEAD_DIM, 1)
            if branch:
                denom = denom + es_scr[slot]
            o = jnp.where(lo, accs[0], accs[1]) / denom
            ocols = slice(512 * branch + LANES * p, 512 * branch + LANES * (p + 1))
            ha_scr[rows, ocols] = (o * g_scr[rows, ocols]).astype(BF16)

        return qk, softmax, pv

    _attend_blocks(block_stages, n_rows // ROW_CHUNK, n_items, unrolled=not latent)

    if not latent:
        weights.load_out_proj()
    _out_proj_norm(x_ref, mod_ref, mod_row, ha_scr, w_out_ref, lng_ref, lnb_ref, layer, y_ref, n_rows, alpha)
    if not latent:
        weights.finish()


def _odd_kernel(latent, layer, n_rows, seq, alpha, lam_init, *refs):
    if latent:
        (x_ref, mod_ref, w_in_ref, w_out_ref, lq1_ref, lk1_ref, lq2_ref, lk2_ref, sub_ref, lng_ref, lnb_ref,
         cos_ref, sin_ref, cck_ref, ccv_ref,
         y_ref,
         ha_scr, q_scr, k_scr, v_scr, g_scr, s_scr, p_scr) = refs
    else:
        (x_ref, mod_ref, w_in_hbm, w_out_hbm, lq1_ref, lk1_ref, lq2_ref, lk2_ref, sub_ref, lng_ref, lnb_ref,
         y_ref, nck_hbm, ncv_hbm, w_in_bf_hbm, w_out_bf_hbm,
         ha_scr, q_scr, k_scr, v_scr, g_scr, s_scr, p_scr, kv_stage, kv_sems,
         w_in_ref, w_out_ref, w_stage, w_sems, w_out_sems) = refs

    step = pl.program_id(0)
    if not latent:
        weights = _ContextWeights(step, (w_in_hbm, w_in_ref, w_in_bf_hbm), (w_out_hbm, w_out_ref, w_out_bf_hbm),
                                  w_stage, w_sems, w_out_sems)
        weights.load_in_proj()
    mod_row = step + 1 if latent else 0
    _modulate(x_ref, mod_ref, mod_row, ha_scr, n_rows)

    n_heads = D_MODEL // LANES
    n_blocks = n_rows // ROW_CHUNK
    lo = _lane_iota(ROW_CHUNK) < HEAD_DIM

    def kv_out_copies(blk):
        elem = step * n_blocks + blk
        return [pltpu.make_async_copy(kv_stage.at[blk, t, :, pl.ds(LANES * h, LANES)],
                                      out.at[elem, 0, :, h, :], kv_sems.at[blk, t])
                for t, out in enumerate((nck_hbm, ncv_hbm)) for h in range(n_heads)]

    def store_k(rows, h, a):
        cols = slice(LANES * h, LANES * (h + 1))
        zero = jnp.zeros_like(a)
        k_scr[0, rows, cols] = jnp.where(lo, a, zero).astype(BF16)
        k_scr[1, rows, cols] = jnp.where(lo, zero, a).astype(BF16)

    if latent:
        n_past = cck_ref.shape[2]
        past = pl.ds(seq, n_past)
        for h in range(n_heads):
            store_k(past, h, cck_ref[0, 0, :, h, :])
            v_scr[past, LANES * h:LANES * (h + 1)] = ccv_ref[0, 0, :, h, :].astype(BF16)

    def proj(i, carry):
        rows = _chunk_rows(i)
        hh = ha_scr[rows, :]
        if latent:
            cos = cos_ref[rows, :]
            sin = sin_ref[rows, :]
            rot = lambda a: _rope(a, cos, sin)
        else:
            rot = lambda a: a
        for half in range(2):
            acc = _dot(hh, w_in_ref[:, 512 * half:512 * (half + 1)])
            for j in range(4):
                a = rot(acc[:, LANES * j:LANES * (j + 1)])
                cols = slice(512 * half + LANES * j, 512 * half + LANES * (j + 1))
                q_scr[rows, cols] = (a * Q_SCALE).astype(BF16)
        for half in range(2):
            acc = _dot(hh, w_in_ref[:, 1024 + 512 * half:1024 + 512 * (half + 1)])
            if not latent:
                kv_stage[i, 0, :, 512 * half:512 * (half + 1)] = acc
            for j in range(4):
                store_k(rows, 4 * half + j, rot(acc[:, LANES * j:LANES * (j + 1)]))
        for half in range(2):
            acc = _dot(hh, w_in_ref[:, 2048 + 512 * half:2048 + 512 * (half + 1)])
            if not latent:
                kv_stage[i, 1, :, 512 * half:512 * (half + 1)] = acc
            v_scr[rows, 512 * half:512 * (half + 1)] = acc.astype(BF16)
        if not latent:
            for copy in kv_out_copies(i):
                copy.start()
        for half in range(2):
            acc = _dot(hh, w_in_ref[:, 3072 + 512 * half:3072 + 512 * (half + 1)])
            g_scr[rows, 512 * half:512 * (half + 1)] = _silu(acc)
        return carry

    if latent:
        lax.fori_loop(0, n_blocks, proj, 0, unroll=2)
    else:
        for blk in range(n_blocks):
            proj(blk, 0)

    lam = (jnp.exp(jnp.sum(lq1_ref[...] * lk1_ref[...], axis=1, keepdims=True))
           - jnp.exp(jnp.sum(lq2_ref[...] * lk2_ref[...], axis=1, keepdims=True)) + lam_init)
    sub = sub_ref[...] * (1.0 - lam_init)
    n_keys = seq + n_past if latent else ROW_CHUNK
    rb = _softmax_rows(n_keys)
    ones = jnp.ones((n_keys, LANES), BF16)

    def block_stages(i):
        rows = _chunk_rows(i)
        keys = pl.ds(0, n_keys) if latent else rows

        def qk(h, slot):
            cols = slice(LANES * h, LANES * (h + 1))
            q = q_scr[rows, cols]
            for m in (0, 1):
                s_scr[slot, m] = _dot_nt(q, k_scr[m, keys, cols])

        def softmax(h, slot):
            for m in (0, 1):
                for r in range(ROW_CHUNK // rb):
                    sub_rows = slice(r * rb, (r + 1) * rb)
                    s = s_scr[slot, m, sub_rows, :]
                    top = jnp.max(s, axis=1, keepdims=True)
                    p_scr[slot, m, sub_rows, :] = jnp.exp2((s - top).astype(BF16))

        def pv(h, slot):
            cols = slice(LANES * h, LANES * (h + 1))
            v_ext = jnp.concatenate([v_scr[keys, cols], ones], axis=1)
            maps = []
            for m in (0, 1):
                acc = _dot(p_scr[slot, m], v_ext)
                maps.append(acc[:, 0:LANES] / acc[:, LANES:2 * LANES])
            o = maps[0] - lam * maps[1]
            ms = jnp.mean(o * o, axis=1, keepdims=True)
            o = o * lax.rsqrt(ms + EPS) * sub
            ha_scr[rows, cols] = (o * g_scr[rows, cols]).astype(BF16)

        return qk, softmax, pv

    _attend_blocks(block_stages, n_blocks, n_heads, unrolled=not latent)

    if not latent:
        weights.load_out_proj()
    _out_proj_norm(x_ref, mod_ref, mod_row, ha_scr, w_out_ref, lng_ref, lnb_ref, layer, y_ref, n_rows, alpha)

    if not latent:
        for blk in range(n_blocks):
            for copy in kv_out_copies(blk):
                copy.wait()
        weights.finish()


MOD_SLAB_ROWS = 128
MOD_COL_BLOCK = 1024
MOD_SLOTS = 4


def _mod_kernel(n_cond, cvb_ref, w_hbm, b_ref, o_ref, sb_scr, ring, acc_scr, sems):
    depth, n_in, n_out = w_hbm.shape
    sublanes = 8
    slabs_per_layer = n_in // MOD_SLAB_ROWS
    slabs = [(l, rs) for l in range(depth) for rs in range(slabs_per_layer)]

    def slab_copy(n):
        l, rs = slabs[n]
        return pltpu.make_async_copy(w_hbm.at[l, pl.ds(rs * MOD_SLAB_ROWS, MOD_SLAB_ROWS), :],
                                     ring.at[n % MOD_SLOTS], sems.at[n % MOD_SLOTS])

    for n in range(min(MOD_SLOTS, len(slabs))):
        slab_copy(n).start()
    sb_scr[...] = _silu(cvb_ref[...])

    for n, (l, rs) in enumerate(slabs):
        slab_copy(n).wait()
        for cb in range(n_out // MOD_COL_BLOCK):
            cols = pl.ds(cb * MOD_COL_BLOCK, MOD_COL_BLOCK)
            if rs == 0:
                accs = (jnp.zeros((sublanes, MOD_COL_BLOCK), F32),) * n_cond
            else:
                accs = tuple(acc_scr[r, :, cols] for r in range(n_cond))

            def body(kb, accs, n=n, rs=rs, cols=cols):
                w = ring[n % MOD_SLOTS, pl.ds(pl.multiple_of(kb * sublanes, sublanes), sublanes), cols]
                s_rows = pl.ds(pl.multiple_of(rs * MOD_SLAB_ROWS + kb * sublanes, sublanes), sublanes)
                return tuple(acc + w * jnp.tile(sb_scr[r, s_rows, :], (1, MOD_COL_BLOCK // LANES))
                             for r, acc in enumerate(accs))

            accs = lax.fori_loop(0, MOD_SLAB_ROWS // sublanes, body, accs, unroll=8)
            if rs < slabs_per_layer - 1:
                for r in range(n_cond):
                    acc_scr[r, :, cols] = accs[r]
            else:
                rows = [jnp.sum(acc, axis=0, keepdims=True) + b_ref[l:l + 1, cols] for acc in accs]
                o_ref[l, :, cols] = jnp.concatenate(rows + [jnp.zeros((8 - n_cond, MOD_COL_BLOCK), F32)], axis=0)
        if n + MOD_SLOTS < len(slabs):
            slab_copy(n + MOD_SLOTS).start()


def _full(shape, **kw):
    zeros = (0,) * len(shape)
    return pl.BlockSpec(shape, lambda i: zeros, **kw)


def _weight_specs(latent, w_in, w_out):
    if latent:
        single = pl.Buffered(1)
        return [_full(w_in.shape, pipeline_mode=single), _full(w_out.shape, pipeline_mode=single)]
    return [pl.BlockSpec(memory_space=pl.ANY), pl.BlockSpec(memory_space=pl.ANY)]


def _weight_scratch(w_in, w_out):
    assert w_in.shape[0] % W_SLAB_ROWS == 0 and w_out.shape[0] % W_SLAB_ROWS == 0
    n_slabs = max(w_in.shape[0], w_out.shape[0]) // W_SLAB_ROWS
    return [pltpu.VMEM(w_in.shape, BF16), pltpu.VMEM(w_out.shape, BF16),
            pltpu.VMEM((n_slabs, W_SLAB_ROWS, max(w_in.shape[1], w_out.shape[1])), F32),
            pltpu.SemaphoreType.DMA((n_slabs,)), pltpu.SemaphoreType.DMA((2,))]


def _rope_tables(seq):
    t = np.arange(seq)
    n_freq = HEAD_DIM // 4
    freqs = ROPE_THETA ** (-np.arange(n_freq, dtype=np.float64) / n_freq)
    ang_row = (t // GRID_W)[:, None] * freqs
    ang_col = (t % GRID_W)[:, None] * freqs
    ang = np.concatenate([ang_row, ang_row, ang_col, ang_col], axis=1)
    sign = np.concatenate([-np.ones(n_freq), np.ones(n_freq)] * 2)[None, :]
    cos = np.tile(np.cos(ang), (1, 2)).astype(np.float32)
    sin = np.tile(np.sin(ang) * sign, (1, 2)).astype(np.float32)
    chunked_t = lambda a: a.reshape(seq // ROW_CHUNK, ROW_CHUNK, LANES).transpose(0, 2, 1)
    return jnp.asarray(cos), jnp.asarray(sin), jnp.asarray(chunked_t(cos)), jnp.asarray(chunked_t(sin))


def _modulation(c, c_ctx, w_mod, b_mod):
    depth = w_mod.shape[0]
    cv = jnp.concatenate([c_ctx[None, :], c], axis=0)
    n_cond = cv.shape[0]
    cvb = jnp.broadcast_to(cv[:, :, None], (n_cond, D_MODEL, LANES))
    assert D_MODEL % MOD_SLAB_ROWS == 0 and (3 * D_MODEL) % MOD_COL_BLOCK == 0
    return pl.pallas_call(
        functools.partial(_mod_kernel, n_cond),
        grid=(1,),
        in_specs=[_full(cvb.shape), pl.BlockSpec(memory_space=pl.ANY), _full(b_mod.shape)],
        out_specs=_full((depth, 8, 3 * D_MODEL)),
        out_shape=jax.ShapeDtypeStruct((depth, 8, 3 * D_MODEL), F32),
        scratch_shapes=[pltpu.VMEM((n_cond, D_MODEL, LANES), F32),
                        pltpu.VMEM((MOD_SLOTS, MOD_SLAB_ROWS, 3 * D_MODEL), F32),
                        pltpu.VMEM((n_cond, 8, 3 * D_MODEL), F32),
                        pltpu.SemaphoreType.DMA((MOD_SLOTS,))],
        compiler_params=pltpu.CompilerParams(dimension_semantics=("arbitrary",)),
        name="adaln_modulation",
    )(cvb, w_mod, b_mod)


def _even_layer(x, mod, layer, w_in, w_out, q_norm, k_norm, sink, ln_g, ln_b, latent, seq, n_rows, alpha, extras=()):
    total = x.shape[0]
    grid = (total // n_rows,)
    single = pl.Buffered(1)
    qn = jnp.tile(q_norm, 2)[None, :]
    knt = jnp.broadcast_to(jnp.tile(k_norm, 2)[:, None], (LANES, ROW_CHUNK))

    row_blk = lambda width: pl.BlockSpec((n_rows, width), lambda i: (i, 0))
    in_specs = [row_blk(D_MODEL),
                pl.BlockSpec((1, 8, 3 * D_MODEL), lambda i: (layer, 0, 0)),
                *_weight_specs(latent, w_in, w_out),
                _full((1, LANES)), _full((LANES, ROW_CHUNK)),
                pl.BlockSpec(memory_space=pltpu.SMEM),
                _full(ln_g.shape), _full(ln_b.shape)]
    args = [x, mod, w_in, w_out, qn, knt, sink, ln_g, ln_b]
    y_shape = jax.ShapeDtypeStruct((total, D_MODEL), F32)
    n_blocks = n_rows // ROW_CHUNK
    if latent:
        cos, sin, cos_t, sin_t, cakt, cav, cbkt, cbv = extras
        n_past = cav.shape[2]
        in_specs += [_full(cos.shape, pipeline_mode=single), _full(sin.shape, pipeline_mode=single),
                     _full(cos_t.shape, pipeline_mode=single), _full(sin_t.shape, pipeline_mode=single)]
        in_specs += [pl.BlockSpec((1, LANES, n_past), lambda i: (i, 0, 0))] * 4
        args += [cos, sin, cos_t, sin_t, cakt, cav, cbkt, cbv]
        out_specs = row_blk(D_MODEL)
        out_shape = y_shape
        n_keys = seq + n_past
    else:
        kv_blk = pl.BlockSpec((n_blocks, LANES, ROW_CHUNK), lambda i: (i, 0, 0))
        hbm = pl.BlockSpec(memory_space=pl.ANY)
        out_specs = [row_blk(D_MODEL)] + [kv_blk] * 4 + [hbm, hbm]
        out_shape = ([y_shape] + [jax.ShapeDtypeStruct((total // seq, LANES, seq), F32)] * 4
                     + [jax.ShapeDtypeStruct(w_in.shape, BF16), jax.ShapeDtypeStruct(w_out.shape, BF16)])
        n_keys = n_rows
    n_kchunks = n_keys // ROW_CHUNK
    n_cols = n_keys if latent else ROW_CHUNK
    scratch = [pltpu.VMEM((n_rows, D_MODEL), BF16),
               pltpu.VMEM((n_rows, 512), BF16), pltpu.VMEM((n_rows, 512), BF16),
               pltpu.VMEM((4, n_kchunks, LANES, ROW_CHUNK), BF16), pltpu.VMEM((4, n_keys, LANES), BF16),
               pltpu.VMEM((4, n_keys // WINDOW, LANES, WINDOW), BF16), pltpu.VMEM((4, n_keys, LANES), BF16),
               pltpu.VMEM((n_rows, D_MODEL), F32),
               pltpu.VMEM((2, 2, ROW_CHUNK, n_cols), F32),
               pltpu.VMEM((2, 2, ROW_CHUNK, n_cols), BF16),
               pltpu.VMEM((2, ROW_CHUNK, LANES), F32),
               pltpu.VMEM((2 * LANES, D_MODEL), BF16),
               pltpu.VMEM((D_MODEL, 2 * LANES), BF16)]
    if latent:
        scratch.append(pltpu.VMEM((1 + 2 * WINDOW // ROW_CHUNK, ROW_CHUNK, ROW_CHUNK), F32))
    else:
        scratch += _weight_scratch(w_in, w_out)
    return pl.pallas_call(
        functools.partial(_even_kernel, latent, layer, n_rows, seq, alpha),
        grid=grid, in_specs=in_specs, out_specs=out_specs, out_shape=out_shape,
        scratch_shapes=scratch,
        compiler_params=pltpu.CompilerParams(dimension_semantics=("arbitrary",), vmem_limit_bytes=VMEM_LIMIT),
        name="even_layer_latent" if latent else "even_layer_context",
    )(*args)


def _odd_layer(x, mod, layer, w_in, w_out, lams, sub, ln_g, ln_b, latent, seq, n_rows, alpha, lam_init, extras=()):
    total = x.shape[0]
    grid = (total // n_rows,)
    row_blk = lambda width: pl.BlockSpec((n_rows, width), lambda i: (i, 0))
    single = pl.Buffered(1)
    in_specs = [row_blk(D_MODEL),
                pl.BlockSpec((1, 8, 3 * D_MODEL), lambda i: (layer, 0, 0)),
                *_weight_specs(latent, w_in, w_out),
                _full((1, HEAD_DIM)), _full((1, HEAD_DIM)), _full((1, HEAD_DIM)), _full((1, HEAD_DIM)),
                _full((1, LANES)),
                _full(ln_g.shape), _full(ln_b.shape)]
    args = [x, mod, w_in, w_out, *lams, sub, ln_g, ln_b]
    y_shape = jax.ShapeDtypeStruct((total, D_MODEL), F32)
    n_heads = D_MODEL // LANES
    n_blocks = n_rows // ROW_CHUNK
    if latent:
        cos, sin, cck, ccv = extras
        n_past = cck.shape[2]
        in_specs += [_full(cos.shape, pipeline_mode=single), _full(sin.shape, pipeline_mode=single)]
        in_specs += [pl.BlockSpec((1, 1, n_past, n_heads, LANES), lambda i: (i, layer // 2, 0, 0, 0))] * 2
        args += [cos, sin, cck, ccv]
        out_specs = row_blk(D_MODEL)
        out_shape = y_shape
        n_keys = seq + n_past
    else:
        hbm = pl.BlockSpec(memory_space=pl.ANY)
        out_specs = [row_blk(D_MODEL), hbm, hbm, hbm, hbm]
        out_shape = ([y_shape] + [jax.ShapeDtypeStruct((total // seq, 1, seq, n_heads, LANES), F32)] * 2
                     + [jax.ShapeDtypeStruct(w_in.shape, BF16), jax.ShapeDtypeStruct(w_out.shape, BF16)])
        n_keys = n_rows
    n_cols = n_keys if latent else ROW_CHUNK
    scratch = [pltpu.VMEM((n_rows, D_MODEL), BF16),
               pltpu.VMEM((n_rows, D_MODEL), BF16),
               pltpu.VMEM((2, n_keys, D_MODEL), BF16),
               pltpu.VMEM((n_keys, D_MODEL), BF16),
               pltpu.VMEM((n_rows, D_MODEL), F32),
               pltpu.VMEM((2, 2, ROW_CHUNK, n_cols), F32),
               pltpu.VMEM((2, 2, ROW_CHUNK, n_cols), BF16)]
    if not latent:
        scratch += [pltpu.VMEM((n_blocks, 2, ROW_CHUNK, D_MODEL), F32),
                    pltpu.SemaphoreType.DMA((n_blocks, 2))]
        scratch += _weight_scratch(w_in, w_out)
    return pl.pallas_call(
        functools.partial(_odd_kernel, latent, layer, n_rows, seq, alpha, lam_init),
        grid=grid, in_specs=in_specs, out_specs=out_specs, out_shape=out_shape,
        scratch_shapes=scratch,
        compiler_params=pltpu.CompilerParams(dimension_semantics=("arbitrary",), vmem_limit_bytes=VMEM_LIMIT),
        name="odd_layer_latent" if latent else "odd_layer_context",
    )(*args)


def kernel(x_prompt, x_sample, cache_a_k, cache_a_v, cache_b_k, cache_b_v, cache_c_k, cache_c_v, c, c_ctx,
           w_mod, b_mod, ln_g, ln_b, w_in_even, w_out_even, q_norm_a, k_norm_a, sink_b, w_in_odd, w_out_odd,
           lambda_q1, lambda_k1, lambda_q2, lambda_k2, subln_c):
    depth = w_mod.shape[0]
    batch, seq, _ = x_prompt.shape
    dec_batch, dec_seq, _ = x_sample.shape
    n_past = cache_a_k.shape[2]
    alpha = (2 * depth) ** 0.25
    assert seq == ROW_CHUNK and n_past % ROW_CHUNK == 0 and dec_seq % ROW_CHUNK == 0

    mod = _modulation(c, c_ctx, w_mod, b_mod)
    cos, sin, cos_t, sin_t = _rope_tables(dec_seq)

    bf16_weights = {}

    def run(x, latent, n_batch, s, rows_even, rows_odd):
        kv = {"a_k": [], "a_v": [], "b_k": [], "b_v": [], "c_k": [], "c_v": []}
        for l in range(depth):
            if l % 2 == 0:
                e = l // 2
                extras = ()
                if latent:
                    k_t = lambda t: t[:, e].transpose(0, 2, 3, 1).reshape(n_batch, LANES, n_past)
                    extras = (cos, sin, cos_t, sin_t,
                              k_t(cache_a_k), k_t(cache_a_v), k_t(cache_b_k), k_t(cache_b_v))
                w_in, w_out = bf16_weights[l] if latent else (w_in_even[e], w_out_even[e])
                res = _even_layer(x, mod, l, w_in, w_out, q_norm_a[e], k_norm_a[e],
                                  sink_b[e], ln_g, ln_b, latent, s, rows_even, alpha, extras)
                if latent:
                    x = res
                else:
                    x = res[0]
                    bf16_weights[l] = res[5:7]
                    for name, t in zip(("a_k", "a_v", "b_k", "b_v"), res[1:5]):
                        kv[name].append(t.reshape(n_batch, 2, HEAD_DIM, s).transpose(0, 3, 1, 2))
            else:
                o = l // 2
                lam_init = 0.8 - 0.6 * math.exp(-0.3 * l)
                extras = (cos, sin, cache_c_k, cache_c_v) if latent else ()
                lams = [t[o][None, :] for t in (lambda_q1, lambda_k1, lambda_q2, lambda_k2)]
                w_in, w_out = bf16_weights[l] if latent else (w_in_odd[o], w_out_odd[o])
                res = _odd_layer(x, mod, l, w_in, w_out, lams,
                                 subln_c[o][None, :], ln_g, ln_b, latent, s, rows_odd, alpha, lam_init, extras)
                if latent:
                    x = res
                else:
                    x = res[0]
                    bf16_weights[l] = res[3:5]
                    kv["c_k"].append(res[1][:, 0])
                    kv["c_v"].append(res[2][:, 0])
        return x, kv

    y_ctx, kv = run(x_prompt.reshape(batch * seq, D_MODEL), False, batch, seq, 1024, 512)
    y_lat, _ = run(x_sample.reshape(dec_batch * dec_seq, D_MODEL), True, dec_batch, dec_seq, dec_seq, dec_seq)

    stack = lambda name: jnp.stack(kv[name], axis=1)
    return (y_ctx.reshape(batch, seq, D_MODEL), y_lat.reshape(dec_batch, dec_seq, D_MODEL),
            stack("a_k"), stack("a_v"), stack("b_k"), stack("b_v"), stack("c_k"), stack("c_v"))
```

```python
import functools
import math

import jax
import jax.numpy as jnp
import numpy as np
from jax import lax
from jax.experimental import pallas as pl
from jax.experimental.pallas import tpu as pltpu

F32 = jnp.float32
BF16 = jnp.bfloat16

D_MODEL = 1024
HEAD_DIM = 64
GRID_W = 64
WINDOW = 128
ROPE_THETA = 10000.0
EPS = 1e-6
NEG_INF = -1e30
LOG2E = 1.4426950408889634
Q_SCALE = HEAD_DIM ** -0.5 * LOG2E
LANES = 128
ROW_CHUNK = 256
SOFTMAX_VREGS = 40
VMEM_LIMIT = 60000 * 1024
W_SLAB_ROWS = 128


def _silu(x):
    return x / (1.0 + jnp.exp(-x))


def _dot(a, b):
    return jnp.dot(a, b, preferred_element_type=F32)


def _dot_nt(a, b):
    return lax.dot_general(a, b, (((1,), (1,)), ((), ())), preferred_element_type=F32)


def _lane_iota(rows):
    return lax.broadcasted_iota(jnp.int32, (rows, LANES), 1)


def _chunk_rows(i):
    if isinstance(i, int):
        return pl.ds(i * ROW_CHUNK, ROW_CHUNK)
    return pl.ds(pl.multiple_of(i * ROW_CHUNK, ROW_CHUNK), ROW_CHUNK)


def _softmax_rows(n_cols):
    rows = 8
    while rows * 2 * n_cols <= SOFTMAX_VREGS * 1024 and rows * 2 <= ROW_CHUNK:
        rows *= 2
    return rows


def _rope(a, cos, sin_signed):
    lane = _lane_iota(a.shape[0])
    fwd = pltpu.roll(a, LANES - 16, 1)
    bwd = pltpu.roll(a, 16, 1)
    partner = jnp.where((lane & 16) == 0, fwd, bwd)
    return a * cos + partner * sin_signed


def _rope_t(a, cos_t, sin_t):
    blocks = [a[16 * b:16 * (b + 1), :] for b in range(a.shape[0] // 16)]
    partner = jnp.concatenate([blocks[b ^ 1] for b in range(len(blocks))], axis=0)
    return a * cos_t + partner * sin_t


def _store_kt_variants(scr, chunk, kt):
    width = scr.shape[-1]
    per_block = kt.shape[1] // width
    zero = jnp.zeros((HEAD_DIM, kt.shape[1]), F32)
    for j in range(2):
        kj = kt[HEAD_DIM * j:HEAD_DIM * (j + 1), :]
        for par, full in enumerate((jnp.concatenate([kj, zero], axis=0), jnp.concatenate([zero, kj], axis=0))):
            full = full.astype(BF16)
            for c in range(per_block):
                scr[2 * j + par, chunk * per_block + c] = full[:, width * c:width * (c + 1)]


def _store_v_variants(scr, rows, a):
    lane = _lane_iota(a.shape[0])
    lo = lane < HEAD_DIM
    swapped = pltpu.roll(a, HEAD_DIM, 1)
    one = jnp.ones_like(a)
    scr[0, rows, :] = jnp.where(lo, a, one).astype(BF16)
    scr[1, rows, :] = jnp.where(lo, one, swapped).astype(BF16)
    scr[2, rows, :] = jnp.where(lo, swapped, one).astype(BF16)
    scr[3, rows, :] = jnp.where(lo, one, a).astype(BF16)


def _layer_norm_rows(z, g, b):
    mu = jnp.mean(z, axis=-1, keepdims=True)
    zc = z - mu
    var = jnp.mean(zc * zc, axis=-1, keepdims=True)
    return zc * lax.rsqrt(var + EPS) * g + b


def _modulate(x_ref, mod_ref, mod_row, h_scr, n_rows):
    shift = mod_ref[0, pl.ds(mod_row, 1), 0:D_MODEL]
    scale = mod_ref[0, pl.ds(mod_row, 1), D_MODEL:2 * D_MODEL]

    def body(i, carry):
        rows = _chunk_rows(i)
        h_scr[rows, :] = (x_ref[rows, :] * (1.0 + scale) + shift).astype(BF16)
        return carry

    lax.fori_loop(0, n_rows // ROW_CHUNK, body, 0)


def _out_proj_norm(x_ref, mod_ref, mod_row, attn_scr, w_out_ref, lng_ref, lnb_ref, layer, y_ref, n_rows, alpha):
    gate = mod_ref[0, pl.ds(mod_row, 1), 2 * D_MODEL:3 * D_MODEL]
    g = lng_ref[layer:layer + 1, :]
    b = lnb_ref[layer:layer + 1, :]

    def body(i, carry):
        rows = _chunk_rows(i)
        out = _dot(attn_scr[rows, :], w_out_ref[...])
        z = alpha * x_ref[rows, :] + gate * out
        y_ref[rows, :] = _layer_norm_rows(z, g, b)
        return carry

    lax.fori_loop(0, n_rows // ROW_CHUNK, body, 0, unroll=True)


class _ContextWeights:
    def __init__(self, step, w_in, w_out, stage, sems, out_sems):
        self.step, self.w_in, self.w_out, self.stage, self.sems = step, w_in, w_out, stage, sems
        self.out_copies = [pltpu.make_async_copy(w[1], w[2], out_sems.at[n]) for n, w in enumerate((w_in, w_out))]

    def _slab_copies(self, w_hbm):
        n_cols = w_hbm.shape[1]
        return [pltpu.make_async_copy(w_hbm.at[pl.ds(s * W_SLAB_ROWS, W_SLAB_ROWS), :],
                                      self.stage.at[s, :, pl.ds(0, n_cols)], self.sems.at[s])
                for s in range(w_hbm.shape[0] // W_SLAB_ROWS)]

    def _cast(self, w_hbm, w_scr):
        n_cols = w_hbm.shape[1]
        for s, copy in enumerate(self._slab_copies(w_hbm)):
            copy.wait()
            w_scr[pl.ds(s * W_SLAB_ROWS, W_SLAB_ROWS), :] = self.stage[s, :, 0:n_cols].astype(BF16)

    def load_in_proj(self):
        @pl.when(self.step == 0)
        def _():
            for copy in self._slab_copies(self.w_in[0]):
                copy.start()
            self._cast(self.w_in[0], self.w_in[1])
            self.out_copies[0].start()
            for copy in self._slab_copies(self.w_out[0]):
                copy.start()

    def load_out_proj(self):
        @pl.when(self.step == 0)
        def _():
            self._cast(self.w_out[0], self.w_out[1])
            self.out_copies[1].start()

    def finish(self):
        @pl.when(self.step == 0)
        def _():
            for copy in self.out_copies:
                copy.wait()


def _run_pipeline(n_items, stages):
    for u in range(n_items + len(stages) - 1):
        for k, stage in enumerate(stages):
            t = u - k
            if 0 <= t < n_items:
                stage(t, t % 2)


def _attend_blocks(block_stages, n_blocks, n_items, unrolled):
    assert n_items % 2 == 0
    if unrolled:
        per_block = [block_stages(i) for i in range(n_blocks)]
        stages = [lambda g, slot, k=k: per_block[g // n_items][k](g % n_items, slot) for k in range(3)]
        _run_pipeline(n_blocks * n_items, stages)
    else:
        def body(i, carry):
            _run_pipeline(n_items, block_stages(i))
            return carry

        lax.fori_loop(0, n_blocks, body, 0)


def _even_kernel(latent, layer, n_rows, seq, alpha, *refs):
    if latent:
        (x_ref, mod_ref, w_in_ref, w_out_ref, norms_ref, sink_ref, lng_ref, lnb_ref,
         cos_ref, sin_ref, cost_ref, sint_ref, cakt_ref, cav_ref, cbkt_ref, cbv_ref,
         y_ref,
         ha_scr, qa_scr, qb_scr, ka_scr, va_scr, kb_scr, vb_scr, g_scr, s_scr, p_scr, es_scr, wkt_scr, wv_scr,
         bias_scr) = refs
    else:
        (x_ref, mod_ref, w_in_hbm, w_out_hbm, norms_ref, sink_ref, lng_ref, lnb_ref,
         y_ref, nakt_ref, navt_ref, nbkt_ref, nbvt_ref, w_in_bf_hbm, w_out_bf_hbm,
         ha_scr, qa_scr, qb_scr, ka_scr, va_scr, kb_scr, vb_scr, g_scr, s_scr, p_scr, es_scr, wkt_scr,
         wv_scr, w_in_ref, w_out_ref, w_stage, w_sems, w_out_sems) = refs

    step = pl.program_id(0)
    if not latent:
        weights = _ContextWeights(step, (w_in_hbm, w_in_ref, w_in_bf_hbm), (w_out_hbm, w_out_ref, w_out_bf_hbm),
                                  w_stage, w_sems, w_out_sems)
        weights.load_in_proj()
    mod_row = step + 1 if latent else 0
    _modulate(x_ref, mod_ref, mod_row, ha_scr, n_rows)

    col_ka, col_va, col_kb, col_vb = 512, 640, 1792, 1920

    @pl.when(step == 0)
    def _():
        for r, c0 in enumerate((col_ka, col_kb)):
            wkt_scr[LANES * r:LANES * (r + 1), :] = w_in_ref[:, c0:c0 + LANES].T
        wv_scr[:, 0:LANES] = w_in_ref[:, col_va:col_va + LANES]
        wv_scr[:, LANES:2 * LANES] = w_in_ref[:, col_vb:col_vb + LANES]

    n_lat_chunks = seq // ROW_CHUNK
    if latent:
        n_past = cav_ref.shape[2]
        past_rows = pl.ds(seq, n_past)
        _store_kt_variants(ka_scr, n_lat_chunks, cakt_ref[0])
        _store_kt_variants(kb_scr, n_lat_chunks, cbkt_ref[0])
        _store_v_variants(va_scr, past_rows, cav_ref[0].T)
        _store_v_variants(vb_scr, past_rows, cbv_ref[0].T)

    knt = norms_ref[0:LANES, :]
    qn = norms_ref[LANES:LANES + 1, 0:LANES]

    def proj(i, carry):
        rows = _chunk_rows(i)
        hh = ha_scr[rows, :]
        if latent:
            cos = cos_ref[rows, :]
            sin = sin_ref[rows, :]
            rot = lambda a: _rope(a, cos, sin)
            rot_t = lambda a: _rope_t(a, cost_ref[i], sint_ref[i])
        else:
            rot = rot_t = lambda a: a

        acc = _dot(hh, w_in_ref[:, 0:512])
        lo_lanes = _lane_iota(ROW_CHUNK) < HEAD_DIM
        for j in range(4):
            a = acc[:, LANES * j:LANES * (j + 1)]
            sq = a * a
            first = jnp.sum(jnp.where(lo_lanes, sq, 0.0), axis=1, keepdims=True)
            second = jnp.sum(jnp.where(lo_lanes, 0.0, sq), axis=1, keepdims=True)
            ms = jnp.where(lo_lanes, first, second) * (1.0 / HEAD_DIM)
            a = rot(a * lax.rsqrt(ms + EPS) * qn)
            qa_scr[rows, LANES * j:LANES * (j + 1)] = (a * Q_SCALE).astype(BF16)
        acc = _dot(hh, w_in_ref[:, 1280:1792])
        for j in range(4):
            a = rot(acc[:, LANES * j:LANES * (j + 1)])
            qb_scr[rows, LANES * j:LANES * (j + 1)] = (a * Q_SCALE).astype(BF16)
        g_scr[rows, 0:512] = _silu(_dot(hh, w_in_ref[:, 768:1280]))
        g_scr[rows, 512:1024] = _silu(_dot(hh, w_in_ref[:, 2048:2560]))
        v = _dot(hh, wv_scr[...])
        _store_v_variants(va_scr, rows, v[:, 0:LANES])
        _store_v_variants(vb_scr, rows, v[:, LANES:2 * LANES])

        kt = _dot_nt(wkt_scr[0:2 * LANES, :], hh)
        heads = [kt[HEAD_DIM * h:HEAD_DIM * (h + 1), :] for h in range(2)]
        kat = jnp.concatenate([blk * lax.rsqrt(jnp.mean(blk * blk, axis=0, keepdims=True) + EPS) for blk in heads],
                              axis=0) * knt
        kbt = kt[LANES:2 * LANES, :]
        if not latent:
            vt = v.T
            nakt_ref[i] = kat
            nbkt_ref[i] = kbt
            navt_ref[i] = vt[0:LANES, :]
            nbvt_ref[i] = vt[LANES:2 * LANES, :]
        _store_kt_variants(ka_scr, i, rot_t(kat))
        _store_kt_variants(kb_scr, i, rot_t(kbt))
        return carry

    lax.fori_loop(0, n_rows // ROW_CHUNK, proj, 0, unroll=2)

    sinks = [sink_ref[h] * LOG2E for h in range(8)]
    ck = ROW_CHUNK
    bk = kb_scr.shape[-1]
    win = ROW_CHUNK + 2 * WINDOW
    n_items = 8

    def block_stages(i):
        rows = _chunk_rows(i)
        if latent:
            a_chunks = list(range(n_lat_chunks + n_past // ck))
            a_keys = pl.ds(0, seq + n_past)
            w0 = jnp.clip(i * (ck // bk) - WINDOW // bk, 0, (seq - win) // bk)
            win_rows = pl.ds(pl.multiple_of(w0 * bk, bk), win)
            dist = (lax.broadcasted_iota(jnp.int32, (ROW_CHUNK, ck), 1)
                    - lax.broadcasted_iota(jnp.int32, (ROW_CHUNK, ck), 0))
            for c in range(win // ck):
                off = w0 * bk + c * ck - i * ck
                bias_scr[c] = jnp.where(jnp.abs(dist + off) <= WINDOW, 0.0, NEG_INF).astype(F32)
            b_first = [w0 + c * (ck // bk) for c in range(win // ck)] + [seq // bk]
            n_biased = win // ck
            b_cols = win + n_past
        else:
            a_chunks = [i]
            a_keys = rows
            b_first = [i * (ck // bk)]
            n_biased = 0
            b_cols = ck
        a_cols = len(a_chunks) * ck

        def qk(t, slot):
            p, branch = divmod(t, 2)
            cols = slice(LANES * p, LANES * (p + 1))
            kvh = p // 2
            q = (qb_scr if branch else qa_scr)[rows, cols]
            for par in (0, 1):
                var = 2 * kvh + par
                if branch:
                    tiles = [jnp.concatenate([kb_scr[var, first + d] for d in range(ck // bk)], axis=1)
                             for first in b_first]
                else:
                    tiles = [ka_scr[var, chunk] for chunk in a_chunks]
                for c, kt in enumerate(tiles):
                    s = _dot(q, kt)
                    if branch and c < n_biased:
                        s = s + bias_scr[c]
                    s_scr[slot, par, :, c * ck:(c + 1) * ck] = s

        def softmax(t, slot):
            p, branch = divmod(t, 2)
            n_cols = b_cols if branch else a_cols
            rb = _softmax_rows(n_cols)
            for par in (0, 1):
                for r in range(ROW_CHUNK // rb):
                    sub = slice(r * rb, (r + 1) * rb)
                    s = s_scr[slot, par, sub, 0:n_cols]
                    m = jnp.max(s, axis=1, keepdims=True)
                    if branch:
                        sink = sinks[2 * p + par]
                        m = jnp.maximum(m, sink)
                        es_scr[slot, sub, HEAD_DIM * par:HEAD_DIM * (par + 1)] = jnp.broadcast_to(
                            jnp.exp2(sink - m), (rb, HEAD_DIM))
                    p_scr[slot, par, sub, 0:n_cols] = jnp.exp2((s - m).astype(BF16))

        def pv(t, slot):
            p, branch = divmod(t, 2)
            kvh = p // 2
            v_scr = vb_scr if branch else va_scr
            accs = []
            for par in (0, 1):
                var = 2 * kvh + par
                if latent and branch:
                    n_loc = win
                    accs.append(_dot(p_scr[slot, par, :, 0:n_loc], v_scr[var, win_rows, :])
                                + _dot(p_scr[slot, par, :, n_loc:b_cols], v_scr[var, past_rows, :]))
                else:
                    accs.append(_dot(p_scr[slot, par, :, 0:a_cols], v_scr[var, a_keys, :]))
            lo = _lane_iota(ROW_CHUNK) < HEAD_DIM
            denom = pltpu.roll(jnp.where(lo, accs[1], accs[0]), HEAD_DIM, 1)
            if branch:
                denom = denom + es_scr[slot]
            o = jnp.where(lo, accs[0], accs[1]) / denom
            ocols = slice(512 * branch + LANES * p, 512 * branch + LANES * (p + 1))
            ha_scr[rows, ocols] = (o * g_scr[rows, ocols]).astype(BF16)

        return qk, softmax, pv

    _attend_blocks(block_stages, n_rows // ROW_CHUNK, n_items, unrolled=not latent)

    if not latent:
        weights.load_out_proj()
    _out_proj_norm(x_ref, mod_ref, mod_row, ha_scr, w_out_ref, lng_ref, lnb_ref, layer, y_ref, n_rows, alpha)
    if not latent:
        weights.finish()


def _odd_kernel(latent, layer, n_rows, seq, alpha, lam_init, *refs):
    if latent:
        (x_ref, mod_ref, w_in_ref, w_out_ref, lq1_ref, lk1_ref, lq2_ref, lk2_ref, sub_ref, lng_ref, lnb_ref,
         cos_ref, sin_ref, cck_ref, ccv_ref,
         y_ref,
         ha_scr, q_scr, k_scr, v_scr, g_scr, s_scr, p_scr) = refs
    else:
        (x_ref, mod_ref, w_in_hbm, w_out_hbm, lq1_ref, lk1_ref, lq2_ref, lk2_ref, sub_ref, lng_ref, lnb_ref,
         y_ref, nck_hbm, ncv_hbm, w_in_bf_hbm, w_out_bf_hbm,
         ha_scr, q_scr, k_scr, v_scr, g_scr, s_scr, p_scr, kv_stage, kv_sems,
         w_in_ref, w_out_ref, w_stage, w_sems, w_out_sems) = refs

    step = pl.program_id(0)
    if not latent:
        weights = _ContextWeights(step, (w_in_hbm, w_in_ref, w_in_bf_hbm), (w_out_hbm, w_out_ref, w_out_bf_hbm),
                                  w_stage, w_sems, w_out_sems)
        weights.load_in_proj()
    mod_row = step + 1 if latent else 0
    _modulate(x_ref, mod_ref, mod_row, ha_scr, n_rows)

    n_heads = D_MODEL // LANES
    n_blocks = n_rows // ROW_CHUNK
    lo = _lane_iota(ROW_CHUNK) < HEAD_DIM

    def kv_out_copies(blk):
        elem = step * n_blocks + blk
        return [pltpu.make_async_copy(kv_stage.at[blk, t, :, pl.ds(LANES * h, LANES)],
                                      out.at[elem, 0, :, h, :], kv_sems.at[blk, t])
                for t, out in enumerate((nck_hbm, ncv_hbm)) for h in range(n_heads)]

    def store_k(rows, h, a):
        cols = slice(LANES * h, LANES * (h + 1))
        zero = jnp.zeros_like(a)
        k_scr[0, rows, cols] = jnp.where(lo, a, zero).astype(BF16)
        k_scr[1, rows, cols] = jnp.where(lo, zero, a).astype(BF16)

    if latent:
        n_past = cck_ref.shape[2]
        past = pl.ds(seq, n_past)
        for h in range(n_heads):
            store_k(past, h, cck_ref[0, 0, :, h, :])
            v_scr[past, LANES * h:LANES * (h + 1)] = ccv_ref[0, 0, :, h, :].astype(BF16)

    def proj(i, carry):
        rows = _chunk_rows(i)
        hh = ha_scr[rows, :]
        if latent:
            cos = cos_ref[rows, :]
            sin = sin_ref[rows, :]
            rot = lambda a: _rope(a, cos, sin)
        else:
            rot = lambda a: a
        for half in range(2):
            acc = _dot(hh, w_in_ref[:, 512 * half:512 * (half + 1)])
            for j in range(4):
                a = rot(acc[:, LANES * j:LANES * (j + 1)])
                cols = slice(512 * half + LANES * j, 512 * half + LANES * (j + 1))
                q_scr[rows, cols] = (a * Q_SCALE).astype(BF16)
        for half in range(2):
            acc = _dot(hh, w_in_ref[:, 1024 + 512 * half:1024 + 512 * (half + 1)])
            if not latent:
                kv_stage[i, 0, :, 512 * half:512 * (half + 1)] = acc
            for j in range(4):
                store_k(rows, 4 * half + j, rot(acc[:, LANES * j:LANES * (j + 1)]))
        for half in range(2):
            acc = _dot(hh, w_in_ref[:, 2048 + 512 * half:2048 + 512 * (half + 1)])
            if not latent:
                kv_stage[i, 1, :, 512 * half:512 * (half + 1)] = acc
            v_scr[rows, 512 * half:512 * (half + 1)] = acc.astype(BF16)
        if not latent:
            for copy in kv_out_copies(i):
                copy.start()
        for half in range(2):
            acc = _dot(hh, w_in_ref[:, 3072 + 512 * half:3072 + 512 * (half + 1)])
            g_scr[rows, 512 * half:512 * (half + 1)] = _silu(acc)
        return carry

    if latent:
        lax.fori_loop(0, n_blocks, proj, 0, unroll=2)
    else:
        for blk in range(n_blocks):
            proj(blk, 0)

    lam = (jnp.exp(jnp.sum(lq1_ref[...] * lk1_ref[...], axis=1, keepdims=True))
           - jnp.exp(jnp.sum(lq2_ref[...] * lk2_ref[...], axis=1, keepdims=True)) + lam_init)
    sub = sub_ref[...] * (1.0 - lam_init)
    n_keys = seq + n_past if latent else ROW_CHUNK
    rb = _softmax_rows(n_keys)
    ones = jnp.ones((n_keys, LANES), BF16)

    def block_stages(i):
        rows = _chunk_rows(i)
        keys = pl.ds(0, n_keys) if latent else rows

        def qk(h, slot):
            cols = slice(LANES * h, LANES * (h + 1))
            q = q_scr[rows, cols]
            for m in (0, 1):
                s_scr[slot, m] = _dot_nt(q, k_scr[m, keys, cols])

        def softmax(h, slot):
            for m in (0, 1):
                for r in range(ROW_CHUNK // rb):
                    sub_rows = slice(r * rb, (r + 1) * rb)
                    s = s_scr[slot, m, sub_rows, :]
                    top = jnp.max(s, axis=1, keepdims=True)
                    p_scr[slot, m, sub_rows, :] = jnp.exp2((s - top).astype(BF16))

        def pv(h, slot):
            cols = slice(LANES * h, LANES * (h + 1))
            v_ext = jnp.concatenate([v_scr[keys, cols], ones], axis=1)
            maps = []
            for m in (0, 1):
                acc = _dot(p_scr[slot, m], v_ext)
                maps.append(acc[:, 0:LANES] / acc[:, LANES:2 * LANES])
            o = maps[0] - lam * maps[1]
            ms = jnp.mean(o * o, axis=1, keepdims=True)
            o = o * lax.rsqrt(ms + EPS) * sub
            ha_scr[rows, cols] = (o * g_scr[rows, cols]).astype(BF16)

        return qk, softmax, pv

    _attend_blocks(block_stages, n_blocks, n_heads, unrolled=not latent)

    if not latent:
        weights.load_out_proj()
    _out_proj_norm(x_ref, mod_ref, mod_row, ha_scr, w_out_ref, lng_ref, lnb_ref, layer, y_ref, n_rows, alpha)

    if not latent:
        for blk in range(n_blocks):
            for copy in kv_out_copies(blk):
                copy.wait()
        weights.finish()


MOD_SLAB_ROWS = 128
MOD_COL_BLOCK = 1024
MOD_SLOTS = 4


def _mod_kernel(n_cond, cv_ref, w_hbm, b_ref, o_ref, sb_scr, ring, acc_scr, sems):
    depth, n_in, n_out = w_hbm.shape
    sublanes = 8
    slabs_per_layer = n_in // MOD_SLAB_ROWS
    slabs = [(l, rs) for l in range(depth) for rs in range(slabs_per_layer)]

    def slab_copy(n):
        l, rs = slabs[n]
        return pltpu.make_async_copy(w_hbm.at[l, pl.ds(rs * MOD_SLAB_ROWS, MOD_SLAB_ROWS), :],
                                     ring.at[n % MOD_SLOTS], sems.at[n % MOD_SLOTS])

    for n in range(min(MOD_SLOTS, len(slabs))):
        slab_copy(n).start()
    s_t = _silu(cv_ref[...]).T
    for r in range(n_cond):
        sb_scr[r] = jnp.broadcast_to(s_t[:, r:r + 1], (n_in, LANES))

    for n, (l, rs) in enumerate(slabs):
        slab_copy(n).wait()
        for cb in range(n_out // MOD_COL_BLOCK):
            cols = pl.ds(cb * MOD_COL_BLOCK, MOD_COL_BLOCK)
            if rs == 0:
                accs = (jnp.zeros((sublanes, MOD_COL_BLOCK), F32),) * n_cond
            else:
                accs = tuple(acc_scr[r, :, cols] for r in range(n_cond))

            def body(kb, accs, n=n, rs=rs, cols=cols):
                w = ring[n % MOD_SLOTS, pl.ds(pl.multiple_of(kb * sublanes, sublanes), sublanes), cols]
                s_rows = pl.ds(pl.multiple_of(rs * MOD_SLAB_ROWS + kb * sublanes, sublanes), sublanes)
                return tuple(acc + w * jnp.tile(sb_scr[r, s_rows, :], (1, MOD_COL_BLOCK // LANES))
                             for r, acc in enumerate(accs))

            accs = lax.fori_loop(0, MOD_SLAB_ROWS // sublanes, body, accs, unroll=8)
            if rs < slabs_per_layer - 1:
                for r in range(n_cond):
                    acc_scr[r, :, cols] = accs[r]
            else:
                rows = [jnp.sum(acc, axis=0, keepdims=True) + b_ref[l:l + 1, cols] for acc in accs]
                o_ref[l, :, cols] = jnp.concatenate(rows + [jnp.zeros((8 - n_cond, MOD_COL_BLOCK), F32)], axis=0)
        if n + MOD_SLOTS < len(slabs):
            slab_copy(n + MOD_SLOTS).start()


def _full(shape, **kw):
    zeros = (0,) * len(shape)
    return pl.BlockSpec(shape, lambda i: zeros, **kw)


def _weight_specs(latent, w_in, w_out):
    if latent:
        single = pl.Buffered(1)
        return [_full(w_in.shape, pipeline_mode=single), _full(w_out.shape, pipeline_mode=single)]
    return [pl.BlockSpec(memory_space=pl.ANY), pl.BlockSpec(memory_space=pl.ANY)]


def _weight_scratch(w_in, w_out):
    assert w_in.shape[0] % W_SLAB_ROWS == 0 and w_out.shape[0] % W_SLAB_ROWS == 0
    n_slabs = max(w_in.shape[0], w_out.shape[0]) // W_SLAB_ROWS
    return [pltpu.VMEM(w_in.shape, BF16), pltpu.VMEM(w_out.shape, BF16),
            pltpu.VMEM((n_slabs, W_SLAB_ROWS, max(w_in.shape[1], w_out.shape[1])), F32),
            pltpu.SemaphoreType.DMA((n_slabs,)), pltpu.SemaphoreType.DMA((2,))]


def _rope_tables(seq):
    t = np.arange(seq)
    n_freq = HEAD_DIM // 4
    freqs = ROPE_THETA ** (-np.arange(n_freq, dtype=np.float64) / n_freq)
    ang_row = (t // GRID_W)[:, None] * freqs
    ang_col = (t % GRID_W)[:, None] * freqs
    ang = np.concatenate([ang_row, ang_row, ang_col, ang_col], axis=1)
    sign = np.concatenate([-np.ones(n_freq), np.ones(n_freq)] * 2)[None, :]
    cos = np.tile(np.cos(ang), (1, 2)).astype(np.float32)
    sin = np.tile(np.sin(ang) * sign, (1, 2)).astype(np.float32)
    chunked_t = lambda a: a.reshape(seq // ROW_CHUNK, ROW_CHUNK, LANES).transpose(0, 2, 1)
    return jnp.asarray(cos), jnp.asarray(sin), jnp.asarray(chunked_t(cos)), jnp.asarray(chunked_t(sin))


def _modulation(c, c_ctx, w_mod, b_mod):
    depth = w_mod.shape[0]
    n_cond = 1 + c.shape[0]
    cv = jnp.concatenate([c_ctx[None, :], c, jnp.zeros((8 - n_cond, D_MODEL), F32)], axis=0)
    assert D_MODEL % MOD_SLAB_ROWS == 0 and (3 * D_MODEL) % MOD_COL_BLOCK == 0
    return pl.pallas_call(
        functools.partial(_mod_kernel, n_cond),
        grid=(1,),
        in_specs=[_full(cv.shape), pl.BlockSpec(memory_space=pl.ANY), _full(b_mod.shape)],
        out_specs=_full((depth, 8, 3 * D_MODEL)),
        out_shape=jax.ShapeDtypeStruct((depth, 8, 3 * D_MODEL), F32),
        scratch_shapes=[pltpu.VMEM((n_cond, D_MODEL, LANES), F32),
                        pltpu.VMEM((MOD_SLOTS, MOD_SLAB_ROWS, 3 * D_MODEL), F32),
                        pltpu.VMEM((n_cond, 8, 3 * D_MODEL), F32),
                        pltpu.SemaphoreType.DMA((MOD_SLOTS,))],
        compiler_params=pltpu.CompilerParams(dimension_semantics=("arbitrary",)),
        name="adaln_modulation",
    )(cv, w_mod, b_mod)


def _even_layer(x, mod, layer, w_in, w_out, q_norm, k_norm, sink, ln_g, ln_b, latent, seq, n_rows, alpha, extras=()):
    total = x.shape[0]
    grid = (total // n_rows,)
    single = pl.Buffered(1)
    norms = jnp.concatenate([jnp.broadcast_to(jnp.tile(k_norm, 2)[:, None], (LANES, ROW_CHUNK)),
                             jnp.broadcast_to(jnp.tile(q_norm, 2 * ROW_CHUNK // LANES)[None, :], (8, ROW_CHUNK))], axis=0)

    row_blk = lambda width: pl.BlockSpec((n_rows, width), lambda i: (i, 0))
    in_specs = [row_blk(D_MODEL),
                pl.BlockSpec((1, 8, 3 * D_MODEL), lambda i: (layer, 0, 0)),
                *_weight_specs(latent, w_in, w_out),
                _full(norms.shape),
                pl.BlockSpec(memory_space=pltpu.SMEM),
                _full(ln_g.shape), _full(ln_b.shape)]
    args = [x, mod, w_in, w_out, norms, sink, ln_g, ln_b]
    y_shape = jax.ShapeDtypeStruct((total, D_MODEL), F32)
    n_blocks = n_rows // ROW_CHUNK
    if latent:
        cos, sin, cos_t, sin_t, cakt, cav, cbkt, cbv = extras
        n_past = cav.shape[2]
        in_specs += [_full(cos.shape, pipeline_mode=single), _full(sin.shape, pipeline_mode=single),
                     _full(cos_t.shape, pipeline_mode=single), _full(sin_t.shape, pipeline_mode=single)]
        in_specs += [pl.BlockSpec((1, LANES, n_past), lambda i: (i, 0, 0))] * 4
        args += [cos, sin, cos_t, sin_t, cakt, cav, cbkt, cbv]
        out_specs = row_blk(D_MODEL)
        out_shape = y_shape
        n_keys = seq + n_past
    else:
        kv_blk = pl.BlockSpec((n_blocks, LANES, ROW_CHUNK), lambda i: (i, 0, 0))
        hbm = pl.BlockSpec(memory_space=pl.ANY)
        out_specs = [row_blk(D_MODEL)] + [kv_blk] * 4 + [hbm, hbm]
        out_shape = ([y_shape] + [jax.ShapeDtypeStruct((total // seq, LANES, seq), F32)] * 4
                     + [jax.ShapeDtypeStruct(w_in.shape, BF16), jax.ShapeDtypeStruct(w_out.shape, BF16)])
        n_keys = n_rows
    n_kchunks = n_keys // ROW_CHUNK
    n_cols = n_keys if latent else ROW_CHUNK
    scratch = [pltpu.VMEM((n_rows, D_MODEL), BF16),
               pltpu.VMEM((n_rows, 512), BF16), pltpu.VMEM((n_rows, 512), BF16),
               pltpu.VMEM((4, n_kchunks, LANES, ROW_CHUNK), BF16), pltpu.VMEM((4, n_keys, LANES), BF16),
               pltpu.VMEM((4, n_keys // WINDOW, LANES, WINDOW), BF16), pltpu.VMEM((4, n_keys, LANES), BF16),
               pltpu.VMEM((n_rows, D_MODEL), F32),
               pltpu.VMEM((2, 2, ROW_CHUNK, n_cols), F32),
               pltpu.VMEM((2, 2, ROW_CHUNK, n_cols), BF16),
               pltpu.VMEM((2, ROW_CHUNK, LANES), F32),
               pltpu.VMEM((2 * LANES, D_MODEL), BF16),
               pltpu.VMEM((D_MODEL, 2 * LANES), BF16)]
    if latent:
        scratch.append(pltpu.VMEM((1 + 2 * WINDOW // ROW_CHUNK, ROW_CHUNK, ROW_CHUNK), F32))
    else:
        scratch += _weight_scratch(w_in, w_out)
    return pl.pallas_call(
        functools.partial(_even_kernel, latent, layer, n_rows, seq, alpha),
        grid=grid, in_specs=in_specs, out_specs=out_specs, out_shape=out_shape,
        scratch_shapes=scratch,
        compiler_params=pltpu.CompilerParams(dimension_semantics=("arbitrary",), vmem_limit_bytes=VMEM_LIMIT),
        name="even_layer_latent" if latent else "even_layer_context",
    )(*args)


def _odd_layer(x, mod, layer, w_in, w_out, lams, sub, ln_g, ln_b, latent, seq, n_rows, alpha, lam_init, extras=()):
    total = x.shape[0]
    grid = (total // n_rows,)
    row_blk = lambda width: pl.BlockSpec((n_rows, width), lambda i: (i, 0))
    single = pl.Buffered(1)
    in_specs = [row_blk(D_MODEL),
                pl.BlockSpec((1, 8, 3 * D_MODEL), lambda i: (layer, 0, 0)),
                *_weight_specs(latent, w_in, w_out),
                _full((1, HEAD_DIM)), _full((1, HEAD_DIM)), _full((1, HEAD_DIM)), _full((1, HEAD_DIM)),
                _full((1, LANES)),
                _full(ln_g.shape), _full(ln_b.shape)]
    args = [x, mod, w_in, w_out, *lams, sub, ln_g, ln_b]
    y_shape = jax.ShapeDtypeStruct((total, D_MODEL), F32)
    n_heads = D_MODEL // LANES
    n_blocks = n_rows // ROW_CHUNK
    if latent:
        cos, sin, cck, ccv = extras
        n_past = cck.shape[2]
        in_specs += [_full(cos.shape, pipeline_mode=single), _full(sin.shape, pipeline_mode=single)]
        in_specs += [pl.BlockSpec((1, 1, n_past, n_heads, LANES), lambda i: (i, layer // 2, 0, 0, 0))] * 2
        args += [cos, sin, cck, ccv]
        out_specs = row_blk(D_MODEL)
        out_shape = y_shape
        n_keys = seq + n_past
    else:
        hbm = pl.BlockSpec(memory_space=pl.ANY)
        out_specs = [row_blk(D_MODEL), hbm, hbm, hbm, hbm]
        out_shape = ([y_shape] + [jax.ShapeDtypeStruct((total // seq, 1, seq, n_heads, LANES), F32)] * 2
                     + [jax.ShapeDtypeStruct(w_in.shape, BF16), jax.ShapeDtypeStruct(w_out.shape, BF16)])
        n_keys = n_rows
    n_cols = n_keys if latent else ROW_CHUNK
    scratch = [pltpu.VMEM((n_rows, D_MODEL), BF16),
               pltpu.VMEM((n_rows, D_MODEL), BF16),
               pltpu.VMEM((2, n_keys, D_MODEL), BF16),
               pltpu.VMEM((n_keys, D_MODEL), BF16),
               pltpu.VMEM((n_rows, D_MODEL), F32),
               pltpu.VMEM((2, 2, ROW_CHUNK, n_cols), F32),
               pltpu.VMEM((2, 2, ROW_CHUNK, n_cols), BF16)]
    if not latent:
        scratch += [pltpu.VMEM((n_blocks, 2, ROW_CHUNK, D_MODEL), F32),
                    pltpu.SemaphoreType.DMA((n_blocks, 2))]
        scratch += _weight_scratch(w_in, w_out)
    return pl.pallas_call(
        functools.partial(_odd_kernel, latent, layer, n_rows, seq, alpha, lam_init),
        grid=grid, in_specs=in_specs, out_specs=out_specs, out_shape=out_shape,
        scratch_shapes=scratch,
        compiler_params=pltpu.CompilerParams(dimension_semantics=("arbitrary",), vmem_limit_bytes=VMEM_LIMIT),
        name="odd_layer_latent" if latent else "odd_layer_context",
    )(*args)


def kernel(x_prompt, x_sample, cache_a_k, cache_a_v, cache_b_k, cache_b_v, cache_c_k, cache_c_v, c, c_ctx,
           w_mod, b_mod, ln_g, ln_b, w_in_even, w_out_even, q_norm_a, k_norm_a, sink_b, w_in_odd, w_out_odd,
           lambda_q1, lambda_k1, lambda_q2, lambda_k2, subln_c):
    depth = w_mod.shape[0]
    batch, seq, _ = x_prompt.shape
    dec_batch, dec_seq, _ = x_sample.shape
    n_past = cache_a_k.shape[2]
    alpha = (2 * depth) ** 0.25
    assert seq == ROW_CHUNK and n_past % ROW_CHUNK == 0 and dec_seq % ROW_CHUNK == 0

    mod = _modulation(c, c_ctx, w_mod, b_mod)
    cos, sin, cos_t, sin_t = _rope_tables(dec_seq)

    bf16_weights = {}

    def run(x, latent, n_batch, s, rows_even, rows_odd):
        kv = {"a_k": [], "a_v": [], "b_k": [], "b_v": [], "c_k": [], "c_v": []}
        for l in range(depth):
            if l % 2 == 0:
                e = l // 2
                extras = ()
                if latent:
                    k_t = lambda t: t[:, e].transpose(0, 2, 3, 1).reshape(n_batch, LANES, n_past)
                    extras = (cos, sin, cos_t, sin_t,
                              k_t(cache_a_k), k_t(cache_a_v), k_t(cache_b_k), k_t(cache_b_v))
                w_in, w_out = bf16_weights[l] if latent else (w_in_even[e], w_out_even[e])
                res = _even_layer(x, mod, l, w_in, w_out, q_norm_a[e], k_norm_a[e],
                                  sink_b[e], ln_g, ln_b, latent, s, rows_even, alpha, extras)
                if latent:
                    x = res
                else:
                    x = res[0]
                    bf16_weights[l] = res[5:7]
                    for name, t in zip(("a_k", "a_v", "b_k", "b_v"), res[1:5]):
                        kv[name].append(t.reshape(n_batch, 2, HEAD_DIM, s).transpose(0, 3, 1, 2))
            else:
                o = l // 2
                lam_init = 0.8 - 0.6 * math.exp(-0.3 * l)
                extras = (cos, sin, cache_c_k, cache_c_v) if latent else ()
                lams = [t[o][None, :] for t in (lambda_q1, lambda_k1, lambda_q2, lambda_k2)]
                w_in, w_out = bf16_weights[l] if latent else (w_in_odd[o], w_out_odd[o])
                res = _odd_layer(x, mod, l, w_in, w_out, lams,
                                 subln_c[o][None, :], ln_g, ln_b, latent, s, rows_odd, alpha, lam_init, extras)
                if latent:
                    x = res
                else:
                    x = res[0]
                    bf16_weights[l] = res[3:5]
                    kv["c_k"].append(res[1][:, 0])
                    kv["c_v"].append(res[2][:, 0])
        return x, kv

    y_ctx, kv = run(x_prompt.reshape(batch * seq, D_MODEL), False, batch, seq, 1024, 512)
    y_lat, _ = run(x_sample.reshape(dec_batch * dec_seq, D_MODEL), True, dec_batch, dec_seq, dec_seq, dec_seq)

    stack = lambda name: jnp.stack(kv[name], axis=1)
    return (y_ctx.reshape(batch, seq, D_MODEL), y_lat.reshape(dec_batch, dec_seq, D_MODEL),
            stack("a_k"), stack("a_v"), stack("b_k"), stack("b_v"), stack("c_k"), stack("c_v"))
```

```python
import functools
import math

import jax
import jax.numpy as jnp
import numpy as np
from jax import lax
from jax.experimental import pallas as pl
from jax.experimental.pallas import tpu as pltpu

F32 = jnp.float32
BF16 = jnp.bfloat16

D_MODEL = 1024
HEAD_DIM = 64
GRID_W = 64
WINDOW = 128
ROPE_THETA = 10000.0
EPS = 1e-6
NEG_INF = -1e30
LOG2E = 1.4426950408889634
Q_SCALE = HEAD_DIM ** -0.5 * LOG2E
LANES = 128
ROW_CHUNK = 256
SOFTMAX_VREGS = 40
VMEM_LIMIT = 60000 * 1024
W_SLAB_ROWS = 128


def _silu(x):
    return x / (1.0 + jnp.exp(-x))


def _dot(a, b):
    return jnp.dot(a, b, preferred_element_type=F32)


def _dot_nt(a, b):
    return lax.dot_general(a, b, (((1,), (1,)), ((), ())), preferred_element_type=F32)


def _lane_iota(rows):
    return lax.broadcasted_iota(jnp.int32, (rows, LANES), 1)


def _chunk_rows(i):
    if isinstance(i, int):
        return pl.ds(i * ROW_CHUNK, ROW_CHUNK)
    return pl.ds(pl.multiple_of(i * ROW_CHUNK, ROW_CHUNK), ROW_CHUNK)


def _softmax_rows(n_cols):
    rows = 8
    while rows * 2 * n_cols <= SOFTMAX_VREGS * 1024 and rows * 2 <= ROW_CHUNK:
        rows *= 2
    return rows


def _rope(a, cos, sin_signed):
    lane = _lane_iota(a.shape[0])
    fwd = pltpu.roll(a, LANES - 16, 1)
    bwd = pltpu.roll(a, 16, 1)
    partner = jnp.where((lane & 16) == 0, fwd, bwd)
    return a * cos + partner * sin_signed


def _rope_t(a, cos_t, sin_t):
    blocks = [a[16 * b:16 * (b + 1), :] for b in range(a.shape[0] // 16)]
    partner = jnp.concatenate([blocks[b ^ 1] for b in range(len(blocks))], axis=0)
    return a * cos_t + partner * sin_t


def _store_kt_variants(scr, chunk, kt):
    width = scr.shape[-1]
    per_block = kt.shape[1] // width
    zero = jnp.zeros((HEAD_DIM, kt.shape[1]), F32)
    for j in range(2):
        kj = kt[HEAD_DIM * j:HEAD_DIM * (j + 1), :]
        for par, full in enumerate((jnp.concatenate([kj, zero], axis=0), jnp.concatenate([zero, kj], axis=0))):
            full = full.astype(BF16)
            for c in range(per_block):
                scr[2 * j + par, chunk * per_block + c] = full[:, width * c:width * (c + 1)]


def _store_v_variants(scr, rows, a):
    lane = _lane_iota(a.shape[0])
    lo = lane < HEAD_DIM
    swapped = pltpu.roll(a, HEAD_DIM, 1)
    one = jnp.ones_like(a)
    scr[0, rows, :] = jnp.where(lo, a, one).astype(BF16)
    scr[1, rows, :] = jnp.where(lo, one, swapped).astype(BF16)
    scr[2, rows, :] = jnp.where(lo, swapped, one).astype(BF16)
    scr[3, rows, :] = jnp.where(lo, one, a).astype(BF16)


def _layer_norm_rows(z, g, b):
    mu = jnp.mean(z, axis=-1, keepdims=True)
    zc = z - mu
    var = jnp.mean(zc * zc, axis=-1, keepdims=True)
    return zc * lax.rsqrt(var + EPS) * g + b


def _modulate(x_ref, mod_ref, mod_row, h_scr, n_rows):
    shift = mod_ref[0, pl.ds(mod_row, 1), 0:D_MODEL]
    scale = mod_ref[0, pl.ds(mod_row, 1), D_MODEL:2 * D_MODEL]

    def body(i, carry):
        rows = _chunk_rows(i)
        h_scr[rows, :] = (x_ref[rows, :] * (1.0 + scale) + shift).astype(BF16)
        return carry

    lax.fori_loop(0, n_rows // ROW_CHUNK, body, 0)


def _out_proj_norm(x_ref, mod_ref, mod_row, attn_scr, w_out_ref, lng_ref, lnb_ref, layer, y_ref, n_rows, alpha):
    gate = mod_ref[0, pl.ds(mod_row, 1), 2 * D_MODEL:3 * D_MODEL]
    g = lng_ref[layer:layer + 1, :]
    b = lnb_ref[layer:layer + 1, :]

    def body(i, carry):
        rows = _chunk_rows(i)
        out = _dot(attn_scr[rows, :], w_out_ref[...])
        z = alpha * x_ref[rows, :] + gate * out
        y_ref[rows, :] = _layer_norm_rows(z, g, b)
        return carry

    lax.fori_loop(0, n_rows // ROW_CHUNK, body, 0, unroll=True)


class _ContextWeights:
    def __init__(self, step, w_in, w_out, out_sems):
        self.step, self.w_in, self.w_out = step, w_in, w_out
        self.out_copies = [pltpu.make_async_copy(w[1], w[2], out_sems.at[n]) for n, w in enumerate((w_in, w_out))]

    @staticmethod
    def _cast(w_f32, w_scr):
        for s in range(w_f32.shape[0] // W_SLAB_ROWS):
            rows = pl.ds(s * W_SLAB_ROWS, W_SLAB_ROWS)
            w_scr[rows, :] = w_f32[rows, :].astype(BF16)

    def load_in_proj(self):
        @pl.when(self.step == 0)
        def _():
            self._cast(self.w_in[0], self.w_in[1])
            self.out_copies[0].start()

    def load_out_proj(self):
        @pl.when(self.step == 0)
        def _():
            self._cast(self.w_out[0], self.w_out[1])
            self.out_copies[1].start()

    def finish(self):
        @pl.when(self.step == 0)
        def _():
            for copy in self.out_copies:
                copy.wait()


def _run_pipeline(n_items, stages):
    for u in range(n_items + len(stages) - 1):
        for k, stage in enumerate(stages):
            t = u - k
            if 0 <= t < n_items:
                stage(t, t % 2)


def _attend_blocks(block_stages, n_blocks, n_items, unrolled):
    assert n_items % 2 == 0
    if unrolled:
        per_block = [block_stages(i) for i in range(n_blocks)]
        stages = [lambda g, slot, k=k: per_block[g // n_items][k](g % n_items, slot) for k in range(3)]
        _run_pipeline(n_blocks * n_items, stages)
    else:
        def body(i, carry):
            _run_pipeline(n_items, block_stages(i))
            return carry

        lax.fori_loop(0, n_blocks, body, 0)


def _even_kernel(latent, layer, n_rows, seq, alpha, *refs):
    if latent:
        (x_ref, mod_ref, w_in_ref, w_out_ref, norms_ref, sink_ref, lng_ref, lnb_ref,
         cos_ref, sin_ref, cost_ref, sint_ref, cakt_ref, cav_ref, cbkt_ref, cbv_ref,
         y_ref,
         ha_scr, qa_scr, qb_scr, ka_scr, va_scr, kb_scr, vb_scr, g_scr, s_scr, p_scr, es_scr, wkt_scr, wv_scr,
         bias_scr) = refs
    else:
        (x_ref, mod_ref, w_in_f32, w_out_f32, norms_ref, sink_ref, lng_ref, lnb_ref,
         y_ref, nakt_ref, navt_ref, nbkt_ref, nbvt_ref, w_in_bf_hbm, w_out_bf_hbm,
         ha_scr, qa_scr, qb_scr, ka_scr, va_scr, kb_scr, vb_scr, g_scr, s_scr, p_scr, es_scr, wkt_scr,
         wv_scr, w_in_ref, w_out_ref, w_out_sems) = refs

    step = pl.program_id(0)
    if not latent:
        weights = _ContextWeights(step, (w_in_f32, w_in_ref, w_in_bf_hbm), (w_out_f32, w_out_ref, w_out_bf_hbm),
                                  w_out_sems)
        weights.load_in_proj()
    mod_row = step + 1 if latent else 0
    _modulate(x_ref, mod_ref, mod_row, ha_scr, n_rows)

    col_ka, col_va, col_kb, col_vb = 512, 640, 1792, 1920

    @pl.when(step == 0)
    def _():
        for r, c0 in enumerate((col_ka, col_kb)):
            wkt_scr[LANES * r:LANES * (r + 1), :] = w_in_ref[:, c0:c0 + LANES].T
        wv_scr[:, 0:LANES] = w_in_ref[:, col_va:col_va + LANES]
        wv_scr[:, LANES:2 * LANES] = w_in_ref[:, col_vb:col_vb + LANES]

    n_lat_chunks = seq // ROW_CHUNK
    if latent:
        n_past = cav_ref.shape[2]
        past_rows = pl.ds(seq, n_past)
        _store_kt_variants(ka_scr, n_lat_chunks, cakt_ref[0])
        _store_kt_variants(kb_scr, n_lat_chunks, cbkt_ref[0])
        _store_v_variants(va_scr, past_rows, cav_ref[0].T)
        _store_v_variants(vb_scr, past_rows, cbv_ref[0].T)

    knt = norms_ref[0:LANES, :]
    qn = norms_ref[LANES:LANES + 1, 0:LANES]

    def proj(i, carry):
        rows = _chunk_rows(i)
        hh = ha_scr[rows, :]
        if latent:
            cos = cos_ref[rows, :]
            sin = sin_ref[rows, :]
            rot = lambda a: _rope(a, cos, sin)
            rot_t = lambda a: _rope_t(a, cost_ref[i], sint_ref[i])
        else:
            rot = rot_t = lambda a: a

        acc = _dot(hh, w_in_ref[:, 0:512])
        lo_lanes = _lane_iota(ROW_CHUNK) < HEAD_DIM
        for j in range(4):
            a = acc[:, LANES * j:LANES * (j + 1)]
            sq = a * a
            first = jnp.sum(jnp.where(lo_lanes, sq, 0.0), axis=1, keepdims=True)
            second = jnp.sum(jnp.where(lo_lanes, 0.0, sq), axis=1, keepdims=True)
            ms = jnp.where(lo_lanes, first, second) * (1.0 / HEAD_DIM)
            a = rot(a * lax.rsqrt(ms + EPS) * qn)
            qa_scr[rows, LANES * j:LANES * (j + 1)] = (a * Q_SCALE).astype(BF16)
        acc = _dot(hh, w_in_ref[:, 1280:1792])
        for j in range(4):
            a = rot(acc[:, LANES * j:LANES * (j + 1)])
            qb_scr[rows, LANES * j:LANES * (j + 1)] = (a * Q_SCALE).astype(BF16)
        g_scr[rows, 0:512] = _silu(_dot(hh, w_in_ref[:, 768:1280]))
        g_scr[rows, 512:1024] = _silu(_dot(hh, w_in_ref[:, 2048:2560]))
        v = _dot(hh, wv_scr[...])
        _store_v_variants(va_scr, rows, v[:, 0:LANES])
        _store_v_variants(vb_scr, rows, v[:, LANES:2 * LANES])

        kt = _dot_nt(wkt_scr[0:2 * LANES, :], hh)
        heads = [kt[HEAD_DIM * h:HEAD_DIM * (h + 1), :] for h in range(2)]
        kat = jnp.concatenate([blk * lax.rsqrt(jnp.mean(blk * blk, axis=0, keepdims=True) + EPS) for blk in heads],
                              axis=0) * knt
        kbt = kt[LANES:2 * LANES, :]
        if not latent:
            vt = v.T
            nakt_ref[i] = kat
            nbkt_ref[i] = kbt
            navt_ref[i] = vt[0:LANES, :]
            nbvt_ref[i] = vt[LANES:2 * LANES, :]
        _store_kt_variants(ka_scr, i, rot_t(kat))
        _store_kt_variants(kb_scr, i, rot_t(kbt))
        return carry

    lax.fori_loop(0, n_rows // ROW_CHUNK, proj, 0, unroll=2)

    sinks = [sink_ref[h] * LOG2E for h in range(8)]
    ck = ROW_CHUNK
    bk = kb_scr.shape[-1]
    win = ROW_CHUNK + 2 * WINDOW
    n_items = 8

    def block_stages(i):
        rows = _chunk_rows(i)
        if latent:
            a_chunks = list(range(n_lat_chunks + n_past // ck))
            a_keys = pl.ds(0, seq + n_past)
            w0 = jnp.clip(i * (ck // bk) - WINDOW // bk, 0, (seq - win) // bk)
            win_rows = pl.ds(pl.multiple_of(w0 * bk, bk), win)
            dist = (lax.broadcasted_iota(jnp.int32, (ROW_CHUNK, ck), 1)
                    - lax.broadcasted_iota(jnp.int32, (ROW_CHUNK, ck), 0))
            for c in range(win // ck):
                off = w0 * bk + c * ck - i * ck
                bias_scr[c] = jnp.where(jnp.abs(dist + off) <= WINDOW, 0.0, NEG_INF).astype(F32)
            b_first = [w0 + c * (ck // bk) for c in range(win // ck)] + [seq // bk]
            n_biased = win // ck
            b_cols = win + n_past
        else:
            a_chunks = [i]
            a_keys = rows
            b_first = [i * (ck // bk)]
            n_biased = 0
            b_cols = ck
        a_cols = len(a_chunks) * ck

        def qk(t, slot):
            p, branch = divmod(t, 2)
            cols = slice(LANES * p, LANES * (p + 1))
            kvh = p // 2
            q = (qb_scr if branch else qa_scr)[rows, cols]
            for par in (0, 1):
                var = 2 * kvh + par
                if branch:
                    tiles = [jnp.concatenate([kb_scr[var, first + d] for d in range(ck // bk)], axis=1)
                             for first in b_first]
                else:
                    tiles = [ka_scr[var, chunk] for chunk in a_chunks]
                for c, kt in enumerate(tiles):
                    s = _dot(q, kt)
                    if branch and c < n_biased:
                        s = s + bias_scr[c]
                    s_scr[slot, par, :, c * ck:(c + 1) * ck] = s

        def softmax(t, slot):
            p, branch = divmod(t, 2)
            n_cols = b_cols if branch else a_cols
            rb = _softmax_rows(n_cols)
            for par in (0, 1):
                for r in range(ROW_CHUNK // rb):
                    sub = slice(r * rb, (r + 1) * rb)
                    s = s_scr[slot, par, sub, 0:n_cols]
                    m = jnp.max(s, axis=1, keepdims=True)
                    if branch:
                        sink = sinks[2 * p + par]
                        m = jnp.maximum(m, sink)
                        es_scr[slot, sub, HEAD_DIM * par:HEAD_DIM * (par + 1)] = jnp.broadcast_to(
                            jnp.exp2(sink - m), (rb, HEAD_DIM))
                    p_scr[slot, par, sub, 0:n_cols] = jnp.exp2((s - m).astype(BF16))

        def pv(t, slot):
            p, branch = divmod(t, 2)
            kvh = p // 2
            v_scr = vb_scr if branch else va_scr
            accs = []
            for par in (0, 1):
                var = 2 * kvh + par
                if latent and branch:
                    n_loc = win
                    accs.append(_dot(p_scr[slot, par, :, 0:n_loc], v_scr[var, win_rows, :])
                                + _dot(p_scr[slot, par, :, n_loc:b_cols], v_scr[var, past_rows, :]))
                else:
                    accs.append(_dot(p_scr[slot, par, :, 0:a_cols], v_scr[var, a_keys, :]))
            lo = _lane_iota(ROW_CHUNK) < HEAD_DIM
            denom = pltpu.roll(jnp.where(lo, accs[1], accs[0]), HEAD_DIM, 1)
            if branch:
                denom = denom + es_scr[slot]
            o = jnp.where(lo, accs[0], accs[1]) / denom
            ocols = slice(512 * branch + LANES * p, 512 * branch + LANES * (p + 1))
            ha_scr[rows, ocols] = (o * g_scr[rows, ocols]).astype(BF16)

        return qk, softmax, pv

    _attend_blocks(block_stages, n_rows // ROW_CHUNK, n_items, unrolled=not latent)

    if not latent:
        weights.load_out_proj()
    _out_proj_norm(x_ref, mod_ref, mod_row, ha_scr, w_out_ref, lng_ref, lnb_ref, layer, y_ref, n_rows, alpha)
    if not latent:
        weights.finish()


def _odd_kernel(latent, layer, n_rows, seq, alpha, lam_init, *refs):
    if latent:
        (x_ref, mod_ref, w_in_ref, w_out_ref, lq1_ref, lk1_ref, lq2_ref, lk2_ref, sub_ref, lng_ref, lnb_ref,
         cos_ref, sin_ref, cck_ref, ccv_ref,
         y_ref,
         ha_scr, q_scr, k_scr, v_scr, g_scr, s_scr, p_scr) = refs
    else:
        (x_ref, mod_ref, w_in_f32, w_out_f32, lq1_ref, lk1_ref, lq2_ref, lk2_ref, sub_ref, lng_ref, lnb_ref,
         y_ref, nck_hbm, ncv_hbm, w_in_bf_hbm, w_out_bf_hbm,
         ha_scr, q_scr, k_scr, v_scr, g_scr, s_scr, p_scr, kv_stage, kv_sems,
         w_in_ref, w_out_ref, w_out_sems) = refs

    step = pl.program_id(0)
    if not latent:
        weights = _ContextWeights(step, (w_in_f32, w_in_ref, w_in_bf_hbm), (w_out_f32, w_out_ref, w_out_bf_hbm),
                                  w_out_sems)
        weights.load_in_proj()
    mod_row = step + 1 if latent else 0
    _modulate(x_ref, mod_ref, mod_row, ha_scr, n_rows)

    n_heads = D_MODEL // LANES
    n_blocks = n_rows // ROW_CHUNK
    lo = _lane_iota(ROW_CHUNK) < HEAD_DIM

    def kv_out_copies(blk):
        elem = step * n_blocks + blk
        return [pltpu.make_async_copy(kv_stage.at[blk, t, :, pl.ds(LANES * h, LANES)],
                                      out.at[elem, 0, :, h, :], kv_sems.at[blk, t])
                for t, out in enumerate((nck_hbm, ncv_hbm)) for h in range(n_heads)]

    def store_k(rows, h, a):
        cols = slice(LANES * h, LANES * (h + 1))
        zero = jnp.zeros_like(a)
        k_scr[0, rows, cols] = jnp.where(lo, a, zero).astype(BF16)
        k_scr[1, rows, cols] = jnp.where(lo, zero, a).astype(BF16)

    if latent:
        n_past = cck_ref.shape[2]
        past = pl.ds(seq, n_past)
        for h in range(n_heads):
            store_k(past, h, cck_ref[0, 0, :, h, :])
            v_scr[past, LANES * h:LANES * (h + 1)] = ccv_ref[0, 0, :, h, :].astype(BF16)

    def proj(i, carry):
        rows = _chunk_rows(i)
        hh = ha_scr[rows, :]
        if latent:
            cos = cos_ref[rows, :]
            sin = sin_ref[rows, :]
            rot = lambda a: _rope(a, cos, sin)
        else:
            rot = lambda a: a
        for half in range(2):
            acc = _dot(hh, w_in_ref[:, 512 * half:512 * (half + 1)])
            for j in range(4):
                a = rot(acc[:, LANES * j:LANES * (j + 1)])
                cols = slice(512 * half + LANES * j, 512 * half + LANES * (j + 1))
                q_scr[rows, cols] = (a * Q_SCALE).astype(BF16)
        for half in range(2):
            acc = _dot(hh, w_in_ref[:, 1024 + 512 * half:1024 + 512 * (half + 1)])
            if not latent:
                kv_stage[i, 0, :, 512 * half:512 * (half + 1)] = acc
            for j in range(4):
                store_k(rows, 4 * half + j, rot(acc[:, LANES * j:LANES * (j + 1)]))
        for half in range(2):
            acc = _dot(hh, w_in_ref[:, 2048 + 512 * half:2048 + 512 * (half + 1)])
            if not latent:
                kv_stage[i, 1, :, 512 * half:512 * (half + 1)] = acc
            v_scr[rows, 512 * half:512 * (half + 1)] = acc.astype(BF16)
        if not latent:
            for copy in kv_out_copies(i):
                copy.start()
        for half in range(2):
            acc = _dot(hh, w_in_ref[:, 3072 + 512 * half:3072 + 512 * (half + 1)])
            g_scr[rows, 512 * half:512 * (half + 1)] = _silu(acc)
        return carry

    if latent:
        lax.fori_loop(0, n_blocks, proj, 0, unroll=2)
    else:
        for blk in range(n_blocks):
            proj(blk, 0)

    lam = (jnp.exp(jnp.sum(lq1_ref[...] * lk1_ref[...], axis=1, keepdims=True))
           - jnp.exp(jnp.sum(lq2_ref[...] * lk2_ref[...], axis=1, keepdims=True)) + lam_init)
    sub = sub_ref[...] * (1.0 - lam_init)
    n_keys = seq + n_past if latent else ROW_CHUNK
    rb = _softmax_rows(n_keys)
    ones = jnp.ones((n_keys, LANES), BF16)

    def block_stages(i):
        rows = _chunk_rows(i)
        keys = pl.ds(0, n_keys) if latent else rows

        def qk(h, slot):
            cols = slice(LANES * h, LANES * (h + 1))
            q = q_scr[rows, cols]
            for m in (0, 1):
                s_scr[slot, m] = _dot_nt(q, k_scr[m, keys, cols])

        def softmax(h, slot):
            for m in (0, 1):
                for r in range(ROW_CHUNK // rb):
                    sub_rows = slice(r * rb, (r + 1) * rb)
                    s = s_scr[slot, m, sub_rows, :]
                    top = jnp.max(s, axis=1, keepdims=True)
                    p_scr[slot, m, sub_rows, :] = jnp.exp2((s - top).astype(BF16))

        def pv(h, slot):
            cols = slice(LANES * h, LANES * (h + 1))
            v_ext = jnp.concatenate([v_scr[keys, cols], ones], axis=1)
            maps = []
            for m in (0, 1):
                acc = _dot(p_scr[slot, m], v_ext)
                maps.append(acc[:, 0:LANES] / acc[:, LANES:2 * LANES])
            o = maps[0] - lam * maps[1]
            ms = jnp.mean(o * o, axis=1, keepdims=True)
            o = o * lax.rsqrt(ms + EPS) * sub
            ha_scr[rows, cols] = (o * g_scr[rows, cols]).astype(BF16)

        return qk, softmax, pv

    _attend_blocks(block_stages, n_blocks, n_heads, unrolled=not latent)

    if not latent:
        weights.load_out_proj()
    _out_proj_norm(x_ref, mod_ref, mod_row, ha_scr, w_out_ref, lng_ref, lnb_ref, layer, y_ref, n_rows, alpha)

    if not latent:
        for blk in range(n_blocks):
            for copy in kv_out_copies(blk):
                copy.wait()
        weights.finish()


MOD_SLAB_ROWS = 128
MOD_COL_BLOCK = 1024
MOD_SLOTS = 4


def _mod_kernel(n_cond, cv_ref, w_hbm, b_ref, o_ref, sb_scr, ring, acc_scr, sems):
    depth, n_in, n_out = w_hbm.shape
    sublanes = 8
    slabs_per_layer = n_in // MOD_SLAB_ROWS
    slabs = [(l, rs) for l in range(depth) for rs in range(slabs_per_layer)]

    def slab_copy(n):
        l, rs = slabs[n]
        return pltpu.make_async_copy(w_hbm.at[l, pl.ds(rs * MOD_SLAB_ROWS, MOD_SLAB_ROWS), :],
                                     ring.at[n % MOD_SLOTS], sems.at[n % MOD_SLOTS])

    for n in range(min(MOD_SLOTS, len(slabs))):
        slab_copy(n).start()
    s_t = _silu(cv_ref[...]).T
    for r in range(n_cond):
        sb_scr[r] = jnp.broadcast_to(s_t[:, r:r + 1], (n_in, LANES))

    for n, (l, rs) in enumerate(slabs):
        slab_copy(n).wait()
        for cb in range(n_out // MOD_COL_BLOCK):
            cols = pl.ds(cb * MOD_COL_BLOCK, MOD_COL_BLOCK)
            if rs == 0:
                accs = (jnp.zeros((sublanes, MOD_COL_BLOCK), F32),) * n_cond
            else:
                accs = tuple(acc_scr[r, :, cols] for r in range(n_cond))

            def body(kb, accs, n=n, rs=rs, cols=cols):
                w = ring[n % MOD_SLOTS, pl.ds(pl.multiple_of(kb * sublanes, sublanes), sublanes), cols]
                s_rows = pl.ds(pl.multiple_of(rs * MOD_SLAB_ROWS + kb * sublanes, sublanes), sublanes)
                return tuple(acc + w * jnp.tile(sb_scr[r, s_rows, :], (1, MOD_COL_BLOCK // LANES))
                             for r, acc in enumerate(accs))

            accs = lax.fori_loop(0, MOD_SLAB_ROWS // sublanes, body, accs, unroll=8)
            if rs < slabs_per_layer - 1:
                for r in range(n_cond):
                    acc_scr[r, :, cols] = accs[r]
            else:
                rows = [jnp.sum(acc, axis=0, keepdims=True) + b_ref[l:l + 1, cols] for acc in accs]
                o_ref[l, :, cols] = jnp.concatenate(rows + [jnp.zeros((8 - n_cond, MOD_COL_BLOCK), F32)], axis=0)
        if n + MOD_SLOTS < len(slabs):
            slab_copy(n + MOD_SLOTS).start()


def _full(shape, **kw):
    zeros = (0,) * len(shape)
    return pl.BlockSpec(shape, lambda i: zeros, **kw)


def _weight_specs(latent, w_in, w_out):
    single = pl.Buffered(1)
    return [_full(w_in.shape, pipeline_mode=single), _full(w_out.shape, pipeline_mode=single)]


def _weight_scratch(w_in, w_out):
    assert w_in.shape[0] % W_SLAB_ROWS == 0 and w_out.shape[0] % W_SLAB_ROWS == 0
    return [pltpu.VMEM(w_in.shape, BF16), pltpu.VMEM(w_out.shape, BF16), pltpu.SemaphoreType.DMA((2,))]


def _rope_tables(seq):
    t = np.arange(seq)
    n_freq = HEAD_DIM // 4
    freqs = ROPE_THETA ** (-np.arange(n_freq, dtype=np.float64) / n_freq)
    ang_row = (t // GRID_W)[:, None] * freqs
    ang_col = (t % GRID_W)[:, None] * freqs
    ang = np.concatenate([ang_row, ang_row, ang_col, ang_col], axis=1)
    sign = np.concatenate([-np.ones(n_freq), np.ones(n_freq)] * 2)[None, :]
    cos = np.tile(np.cos(ang), (1, 2)).astype(np.float32)
    sin = np.tile(np.sin(ang) * sign, (1, 2)).astype(np.float32)
    chunked_t = lambda a: a.reshape(seq // ROW_CHUNK, ROW_CHUNK, LANES).transpose(0, 2, 1)
    return jnp.asarray(cos), jnp.asarray(sin), jnp.asarray(chunked_t(cos)), jnp.asarray(chunked_t(sin))


def _modulation(c, c_ctx, w_mod, b_mod):
    depth = w_mod.shape[0]
    n_cond = 1 + c.shape[0]
    cv = jnp.concatenate([c_ctx[None, :], c, jnp.zeros((8 - n_cond, D_MODEL), F32)], axis=0)
    assert D_MODEL % MOD_SLAB_ROWS == 0 and (3 * D_MODEL) % MOD_COL_BLOCK == 0
    return pl.pallas_call(
        functools.partial(_mod_kernel, n_cond),
        grid=(1,),
        in_specs=[_full(cv.shape), pl.BlockSpec(memory_space=pl.ANY), _full(b_mod.shape)],
        out_specs=_full((depth, 8, 3 * D_MODEL)),
        out_shape=jax.ShapeDtypeStruct((depth, 8, 3 * D_MODEL), F32),
        scratch_shapes=[pltpu.VMEM((n_cond, D_MODEL, LANES), F32),
                        pltpu.VMEM((MOD_SLOTS, MOD_SLAB_ROWS, 3 * D_MODEL), F32),
                        pltpu.VMEM((n_cond, 8, 3 * D_MODEL), F32),
                        pltpu.SemaphoreType.DMA((MOD_SLOTS,))],
        compiler_params=pltpu.CompilerParams(dimension_semantics=("arbitrary",)),
        name="adaln_modulation",
    )(cv, w_mod, b_mod)


def _even_layer(x, mod, layer, w_in, w_out, q_norm, k_norm, sink, ln_g, ln_b, latent, seq, n_rows, alpha, extras=()):
    total = x.shape[0]
    grid = (total // n_rows,)
    single = pl.Buffered(1)
    norms = jnp.concatenate([jnp.broadcast_to(jnp.tile(k_norm, 2)[:, None], (LANES, ROW_CHUNK)),
                             jnp.broadcast_to(jnp.tile(q_norm, 2 * ROW_CHUNK // LANES)[None, :], (8, ROW_CHUNK))], axis=0)

    row_blk = lambda width: pl.BlockSpec((n_rows, width), lambda i: (i, 0))
    in_specs = [row_blk(D_MODEL),
                pl.BlockSpec((1, 8, 3 * D_MODEL), lambda i: (layer, 0, 0)),
                *_weight_specs(latent, w_in, w_out),
                _full(norms.shape),
                pl.BlockSpec(memory_space=pltpu.SMEM),
                _full(ln_g.shape), _full(ln_b.shape)]
    args = [x, mod, w_in, w_out, norms, sink, ln_g, ln_b]
    y_shape = jax.ShapeDtypeStruct((total, D_MODEL), F32)
    n_blocks = n_rows // ROW_CHUNK
    if latent:
        cos, sin, cos_t, sin_t, cakt, cav, cbkt, cbv = extras
        n_past = cav.shape[2]
        in_specs += [_full(cos.shape, pipeline_mode=single), _full(sin.shape, pipeline_mode=single),
                     _full(cos_t.shape, pipeline_mode=single), _full(sin_t.shape, pipeline_mode=single)]
        in_specs += [pl.BlockSpec((1, LANES, n_past), lambda i: (i, 0, 0))] * 4
        args += [cos, sin, cos_t, sin_t, cakt, cav, cbkt, cbv]
        out_specs = row_blk(D_MODEL)
        out_shape = y_shape
        n_keys = seq + n_past
    else:
        kv_blk = pl.BlockSpec((n_blocks, LANES, ROW_CHUNK), lambda i: (i, 0, 0))
        hbm = pl.BlockSpec(memory_space=pl.ANY)
        out_specs = [row_blk(D_MODEL)] + [kv_blk] * 4 + [hbm, hbm]
        out_shape = ([y_shape] + [jax.ShapeDtypeStruct((total // seq, LANES, seq), F32)] * 4
                     + [jax.ShapeDtypeStruct(w_in.shape, BF16), jax.ShapeDtypeStruct(w_out.shape, BF16)])
        n_keys = n_rows
    n_kchunks = n_keys // ROW_CHUNK
    n_cols = n_keys if latent else ROW_CHUNK
    scratch = [pltpu.VMEM((n_rows, D_MODEL), BF16),
               pltpu.VMEM((n_rows, 512), BF16), pltpu.VMEM((n_rows, 512), BF16),
               pltpu.VMEM((4, n_kchunks, LANES, ROW_CHUNK), BF16), pltpu.VMEM((4, n_keys, LANES), BF16),
               pltpu.VMEM((4, n_keys // WINDOW, LANES, WINDOW), BF16), pltpu.VMEM((4, n_keys, LANES), BF16),
               pltpu.VMEM((n_rows, D_MODEL), F32),
               pltpu.VMEM((2, 2, ROW_CHUNK, n_cols), F32),
               pltpu.VMEM((2, 2, ROW_CHUNK, n_cols), BF16),
               pltpu.VMEM((2, ROW_CHUNK, LANES), F32),
               pltpu.VMEM((2 * LANES, D_MODEL), BF16),
               pltpu.VMEM((D_MODEL, 2 * LANES), BF16)]
    if latent:
        scratch.append(pltpu.VMEM((1 + 2 * WINDOW // ROW_CHUNK, ROW_CHUNK, ROW_CHUNK), F32))
    else:
        scratch += _weight_scratch(w_in, w_out)
    return pl.pallas_call(
        functools.partial(_even_kernel, latent, layer, n_rows, seq, alpha),
        grid=grid, in_specs=in_specs, out_specs=out_specs, out_shape=out_shape,
        scratch_shapes=scratch,
        compiler_params=pltpu.CompilerParams(dimension_semantics=("arbitrary",), vmem_limit_bytes=VMEM_LIMIT),
        name="even_layer_latent" if latent else "even_layer_context",
    )(*args)


def _odd_layer(x, mod, layer, w_in, w_out, lams, sub, ln_g, ln_b, latent, seq, n_rows, alpha, lam_init, extras=()):
    total = x.shape[0]
    grid = (total // n_rows,)
    row_blk = lambda width: pl.BlockSpec((n_rows, width), lambda i: (i, 0))
    single = pl.Buffered(1)
    in_specs = [row_blk(D_MODEL),
                pl.BlockSpec((1, 8, 3 * D_MODEL), lambda i: (layer, 0, 0)),
                *_weight_specs(latent, w_in, w_out),
                _full((1, HEAD_DIM)), _full((1, HEAD_DIM)), _full((1, HEAD_DIM)), _full((1, HEAD_DIM)),
                _full((1, LANES)),
                _full(ln_g.shape), _full(ln_b.shape)]
    args = [x, mod, w_in, w_out, *lams, sub, ln_g, ln_b]
    y_shape = jax.ShapeDtypeStruct((total, D_MODEL), F32)
    n_heads = D_MODEL // LANES
    n_blocks = n_rows // ROW_CHUNK
    if latent:
        cos, sin, cck, ccv = extras
        n_past = cck.shape[2]
        in_specs += [_full(cos.shape, pipeline_mode=single), _full(sin.shape, pipeline_mode=single)]
        in_specs += [pl.BlockSpec((1, 1, n_past, n_heads, LANES), lambda i: (i, layer // 2, 0, 0, 0))] * 2
        args += [cos, sin, cck, ccv]
        out_specs = row_blk(D_MODEL)
        out_shape = y_shape
        n_keys = seq + n_past
    else:
        hbm = pl.BlockSpec(memory_space=pl.ANY)
        out_specs = [row_blk(D_MODEL), hbm, hbm, hbm, hbm]
        out_shape = ([y_shape] + [jax.ShapeDtypeStruct((total // seq, 1, seq, n_heads, LANES), F32)] * 2
                     + [jax.ShapeDtypeStruct(w_in.shape, BF16), jax.ShapeDtypeStruct(w_out.shape, BF16)])
        n_keys = n_rows
    n_cols = n_keys if latent else ROW_CHUNK
    scratch = [pltpu.VMEM((n_rows, D_MODEL), BF16),
               pltpu.VMEM((n_rows, D_MODEL), BF16),
               pltpu.VMEM((2, n_keys, D_MODEL), BF16),
               pltpu.VMEM((n_keys, D_MODEL), BF16),
               pltpu.VMEM((n_rows, D_MODEL), F32),
               pltpu.VMEM((2, 2, ROW_CHUNK, n_cols), F32),
               pltpu.VMEM((2, 2, ROW_CHUNK, n_cols), BF16)]
    if not latent:
        scratch += [pltpu.VMEM((n_blocks, 2, ROW_CHUNK, D_MODEL), F32),
                    pltpu.SemaphoreType.DMA((n_blocks, 2))]
        scratch += _weight_scratch(w_in, w_out)
    return pl.pallas_call(
        functools.partial(_odd_kernel, latent, layer, n_rows, seq, alpha, lam_init),
        grid=grid, in_specs=in_specs, out_specs=out_specs, out_shape=out_shape,
        scratch_shapes=scratch,
        compiler_params=pltpu.CompilerParams(dimension_semantics=("arbitrary",), vmem_limit_bytes=VMEM_LIMIT),
        name="odd_layer_latent" if latent else "odd_layer_context",
    )(*args)


def kernel(x_prompt, x_sample, cache_a_k, cache_a_v, cache_b_k, cache_b_v, cache_c_k, cache_c_v, c, c_ctx,
           w_mod, b_mod, ln_g, ln_b, w_in_even, w_out_even, q_norm_a, k_norm_a, sink_b, w_in_odd, w_out_odd,
           lambda_q1, lambda_k1, lambda_q2, lambda_k2, subln_c):
    depth = w_mod.shape[0]
    batch, seq, _ = x_prompt.shape
    dec_batch, dec_seq, _ = x_sample.shape
    n_past = cache_a_k.shape[2]
    alpha = (2 * depth) ** 0.25
    assert seq == ROW_CHUNK and n_past % ROW_CHUNK == 0 and dec_seq % ROW_CHUNK == 0

    mod = _modulation(c, c_ctx, w_mod, b_mod)
    cos, sin, cos_t, sin_t = _rope_tables(dec_seq)

    bf16_weights = {}

    def run(x, latent, n_batch, s, rows_even, rows_odd):
        kv = {"a_k": [], "a_v": [], "b_k": [], "b_v": [], "c_k": [], "c_v": []}
        for l in range(depth):
            if l % 2 == 0:
                e = l // 2
                extras = ()
                if latent:
                    k_t = lambda t: t[:, e].transpose(0, 2, 3, 1).reshape(n_batch, LANES, n_past)
                    extras = (cos, sin, cos_t, sin_t,
                              k_t(cache_a_k), k_t(cache_a_v), k_t(cache_b_k), k_t(cache_b_v))
                w_in, w_out = bf16_weights[l] if latent else (w_in_even[e], w_out_even[e])
                res = _even_layer(x, mod, l, w_in, w_out, q_norm_a[e], k_norm_a[e],
                                  sink_b[e], ln_g, ln_b, latent, s, rows_even, alpha, extras)
                if latent:
                    x = res
                else:
                    x = res[0]
                    bf16_weights[l] = res[5:7]
                    for name, t in zip(("a_k", "a_v", "b_k", "b_v"), res[1:5]):
                        kv[name].append(t.reshape(n_batch, 2, HEAD_DIM, s).transpose(0, 3, 1, 2))
            else:
                o = l // 2
                lam_init = 0.8 - 0.6 * math.exp(-0.3 * l)
                extras = (cos, sin, cache_c_k, cache_c_v) if latent else ()
                lams = [t[o][None, :] for t in (lambda_q1, lambda_k1, lambda_q2, lambda_k2)]
                w_in, w_out = bf16_weights[l] if latent else (w_in_odd[o], w_out_odd[o])
                res = _odd_layer(x, mod, l, w_in, w_out, lams,
                                 subln_c[o][None, :], ln_g, ln_b, latent, s, rows_odd, alpha, lam_init, extras)
                if latent:
                    x = res
                else:
                    x = res[0]
                    bf16_weights[l] = res[3:5]
                    kv["c_k"].append(res[1][:, 0])
                    kv["c_v"].append(res[2][:, 0])
        return x, kv

    y_ctx, kv = run(x_prompt.reshape(batch * seq, D_MODEL), False, batch, seq, 1024, 512)
    y_lat, _ = run(x_sample.reshape(dec_batch * dec_seq, D_MODEL), True, dec_batch, dec_seq, dec_seq, dec_seq)

    stack = lambda name: jnp.stack(kv[name], axis=1)
    return (y_ctx.reshape(batch, seq, D_MODEL), y_lat.reshape(dec_batch, dec_seq, D_MODEL),
            stack("a_k"), stack("a_v"), stack("b_k"), stack("b_v"), stack("c_k"), stack("c_v"))
```

```python
import functools
import math

import jax
import jax.numpy as jnp
import numpy as np
from jax import lax
from jax.experimental import pallas as pl
from jax.experimental.pallas import tpu as pltpu

F32 = jnp.float32
BF16 = jnp.bfloat16

D_MODEL = 1024
HEAD_DIM = 64
GRID_W = 64
WINDOW = 128
ROPE_THETA = 10000.0
EPS = 1e-6
NEG_INF = -1e30
LOG2E = 1.4426950408889634
Q_SCALE = HEAD_DIM ** -0.5 * LOG2E
LANES = 128
ROW_CHUNK = 256
SOFTMAX_VREGS = 40
VMEM_LIMIT = 60000 * 1024
W_SLAB_ROWS = 128


def _silu(x):
    return x / (1.0 + jnp.exp(-x))


def _dot(a, b):
    return jnp.dot(a, b, preferred_element_type=F32)


def _dot_nt(a, b):
    return lax.dot_general(a, b, (((1,), (1,)), ((), ())), preferred_element_type=F32)


def _lane_iota(rows):
    return lax.broadcasted_iota(jnp.int32, (rows, LANES), 1)


def _chunk_rows(i):
    if isinstance(i, int):
        return pl.ds(i * ROW_CHUNK, ROW_CHUNK)
    return pl.ds(pl.multiple_of(i * ROW_CHUNK, ROW_CHUNK), ROW_CHUNK)


def _softmax_rows(n_cols):
    rows = 8
    while rows * 2 * n_cols <= SOFTMAX_VREGS * 1024 and rows * 2 <= ROW_CHUNK:
        rows *= 2
    return rows


def _rope(a, cos, sin_signed):
    lane = _lane_iota(a.shape[0])
    fwd = pltpu.roll(a, LANES - 16, 1)
    bwd = pltpu.roll(a, 16, 1)
    partner = jnp.where((lane & 16) == 0, fwd, bwd)
    return a * cos + partner * sin_signed


def _rope_t(a, cos_t, sin_t):
    blocks = [a[16 * b:16 * (b + 1), :] for b in range(a.shape[0] // 16)]
    partner = jnp.concatenate([blocks[b ^ 1] for b in range(len(blocks))], axis=0)
    return a * cos_t + partner * sin_t


def _store_kt_variants(scr, chunk, kt):
    width = scr.shape[-1]
    per_block = kt.shape[1] // width
    zero = jnp.zeros((HEAD_DIM, kt.shape[1]), F32)
    for j in range(2):
        kj = kt[HEAD_DIM * j:HEAD_DIM * (j + 1), :]
        for par, full in enumerate((jnp.concatenate([kj, zero], axis=0), jnp.concatenate([zero, kj], axis=0))):
            full = full.astype(BF16)
            for c in range(per_block):
                scr[2 * j + par, chunk * per_block + c] = full[:, width * c:width * (c + 1)]


def _store_v_variants(scr, rows, a):
    lane = _lane_iota(a.shape[0])
    lo = lane < HEAD_DIM
    swapped = pltpu.roll(a, HEAD_DIM, 1)
    one = jnp.ones_like(a)
    scr[0, rows, :] = jnp.where(lo, a, one).astype(BF16)
    scr[1, rows, :] = jnp.where(lo, one, swapped).astype(BF16)
    scr[2, rows, :] = jnp.where(lo, swapped, one).astype(BF16)
    scr[3, rows, :] = jnp.where(lo, one, a).astype(BF16)


def _layer_norm_rows(z, g, b):
    mu = jnp.mean(z, axis=-1, keepdims=True)
    zc = z - mu
    var = jnp.mean(zc * zc, axis=-1, keepdims=True)
    return zc * lax.rsqrt(var + EPS) * g + b


def _modulate(x_ref, mod_ref, mod_row, h_scr, n_rows):
    shift = mod_ref[0, pl.ds(mod_row, 1), 0:D_MODEL]
    scale = mod_ref[0, pl.ds(mod_row, 1), D_MODEL:2 * D_MODEL]

    def body(i, carry):
        rows = _chunk_rows(i)
        h_scr[rows, :] = (x_ref[rows, :] * (1.0 + scale) + shift).astype(BF16)
        return carry

    lax.fori_loop(0, n_rows // ROW_CHUNK, body, 0)


def _out_proj_norm(x_ref, mod_ref, mod_row, attn_scr, w_out_ref, lng_ref, lnb_ref, layer, y_ref, n_rows, alpha):
    gate = mod_ref[0, pl.ds(mod_row, 1), 2 * D_MODEL:3 * D_MODEL]
    g = lng_ref[layer:layer + 1, :]
    b = lnb_ref[layer:layer + 1, :]

    def body(i, carry):
        rows = _chunk_rows(i)
        out = _dot(attn_scr[rows, :], w_out_ref[...])
        z = alpha * x_ref[rows, :] + gate * out
        y_ref[rows, :] = _layer_norm_rows(z, g, b)
        return carry

    lax.fori_loop(0, n_rows // ROW_CHUNK, body, 0, unroll=True)


class _ContextWeights:
    def __init__(self, step, w_in, w_out, stage, sems, out_sems):
        self.step, self.w_in, self.w_out, self.stage, self.sems = step, w_in, w_out, stage, sems
        self.out_copies = [pltpu.make_async_copy(w[1], w[2], out_sems.at[n]) for n, w in enumerate((w_in, w_out))]

    def _slab_copies(self, w_hbm):
        n_cols = w_hbm.shape[1]
        return [pltpu.make_async_copy(w_hbm.at[pl.ds(s * W_SLAB_ROWS, W_SLAB_ROWS), :],
                                      self.stage.at[s, :, pl.ds(0, n_cols)], self.sems.at[s])
                for s in range(w_hbm.shape[0] // W_SLAB_ROWS)]

    def _cast(self, w_hbm, w_scr):
        n_cols = w_hbm.shape[1]
        for s, copy in enumerate(self._slab_copies(w_hbm)):
            copy.wait()
            w_scr[pl.ds(s * W_SLAB_ROWS, W_SLAB_ROWS), :] = self.stage[s, :, 0:n_cols].astype(BF16)

    def load_in_proj(self):
        @pl.when(self.step == 0)
        def _():
            for copy in self._slab_copies(self.w_in[0]):
                copy.start()
            self._cast(self.w_in[0], self.w_in[1])
            self.out_copies[0].start()
            for copy in self._slab_copies(self.w_out[0]):
                copy.start()

    def load_out_proj(self):
        @pl.when(self.step == 0)
        def _():
            self._cast(self.w_out[0], self.w_out[1])
            self.out_copies[1].start()

    def finish(self):
        @pl.when(self.step == 0)
        def _():
            for copy in self.out_copies:
                copy.wait()


def _run_pipeline(n_items, stages):
    for u in range(n_items + len(stages) - 1):
        for k, stage in enumerate(stages):
            t = u - k
            if 0 <= t < n_items:
                stage(t, t % 2)


def _attend_blocks(block_stages, n_blocks, n_items, unrolled):
    assert n_items % 2 == 0
    if unrolled:
        per_block = [block_stages(i) for i in range(n_blocks)]
        stages = [lambda g, slot, k=k: per_block[g // n_items][k](g % n_items, slot) for k in range(3)]
        _run_pipeline(n_blocks * n_items, stages)
    else:
        def body(i, carry):
            _run_pipeline(n_items, block_stages(i))
            return carry

        lax.fori_loop(0, n_blocks, body, 0)


def _even_kernel(latent, layer, n_rows, seq, alpha, *refs):
    if latent:
        (x_ref, mod_ref, w_in_ref, w_out_ref, norms_ref, sink_ref, lng_ref, lnb_ref,
         cos_ref, sin_ref, cost_ref, sint_ref, cakt_ref, cav_ref, cbkt_ref, cbv_ref,
         y_ref,
         ha_scr, qa_scr, qb_scr, ka_scr, va_scr, kb_scr, vb_scr, g_scr, s_scr, p_scr, es_scr, wkt_scr, wv_scr,
         bias_scr) = refs
    else:
        (x_ref, mod_ref, w_in_hbm, w_out_hbm, norms_ref, sink_ref, lng_ref, lnb_ref,
         y_ref, nakt_ref, navt_ref, nbkt_ref, nbvt_ref, w_in_bf_hbm, w_out_bf_hbm,
         ha_scr, qa_scr, qb_scr, ka_scr, va_scr, kb_scr, vb_scr, g_scr, s_scr, p_scr, es_scr, wkt_scr,
         wv_scr, w_in_ref, w_out_ref, w_stage, w_sems, w_out_sems) = refs

    step = pl.program_id(0)
    if not latent:
        weights = _ContextWeights(step, (w_in_hbm, w_in_ref, w_in_bf_hbm), (w_out_hbm, w_out_ref, w_out_bf_hbm),
                                  w_stage, w_sems, w_out_sems)
        weights.load_in_proj()
    mod_row = step + 1 if latent else 0
    _modulate(x_ref, mod_ref, mod_row, ha_scr, n_rows)

    col_ka, col_va, col_kb, col_vb = 512, 640, 1792, 1920

    @pl.when(step == 0)
    def _():
        for r, c0 in enumerate((col_ka, col_kb)):
            wkt_scr[LANES * r:LANES * (r + 1), :] = w_in_ref[:, c0:c0 + LANES].T
        wv_scr[:, 0:LANES] = w_in_ref[:, col_va:col_va + LANES]
        wv_scr[:, LANES:2 * LANES] = w_in_ref[:, col_vb:col_vb + LANES]

    n_lat_chunks = seq // ROW_CHUNK
    if latent:
        n_past = cav_ref.shape[2]
        past_rows = pl.ds(seq, n_past)
        _store_kt_variants(ka_scr, n_lat_chunks, cakt_ref[0])
        _store_kt_variants(kb_scr, n_lat_chunks, cbkt_ref[0])
        _store_v_variants(va_scr, past_rows, cav_ref[0].T)
        _store_v_variants(vb_scr, past_rows, cbv_ref[0].T)

    knt = norms_ref[0:LANES, :]
    qn = norms_ref[LANES:LANES + 1, 0:LANES]

    def proj(i, carry):
        rows = _chunk_rows(i)
        hh = ha_scr[rows, :]
        if latent:
            cos = cos_ref[rows, :]
            sin = sin_ref[rows, :]
            rot = lambda a: _rope(a, cos, sin)
            rot_t = lambda a: _rope_t(a, cost_ref[i], sint_ref[i])
        else:
            rot = rot_t = lambda a: a

        acc = _dot(hh, w_in_ref[:, 0:512])
        lo_lanes = _lane_iota(ROW_CHUNK) < HEAD_DIM
        for j in range(4):
            a = acc[:, LANES * j:LANES * (j + 1)]
            sq = a * a
            first = jnp.sum(jnp.where(lo_lanes, sq, 0.0), axis=1, keepdims=True)
            second = jnp.sum(jnp.where(lo_lanes, 0.0, sq), axis=1, keepdims=True)
            ms = jnp.where(lo_lanes, first, second) * (1.0 / HEAD_DIM)
            a = rot(a * lax.rsqrt(ms + EPS) * qn)
            qa_scr[rows, LANES * j:LANES * (j + 1)] = (a * Q_SCALE).astype(BF16)
        acc = _dot(hh, w_in_ref[:, 1280:1792])
        for j in range(4):
            a = rot(acc[:, LANES * j:LANES * (j + 1)])
            qb_scr[rows, LANES * j:LANES * (j + 1)] = (a * Q_SCALE).astype(BF16)
        g_scr[rows, 0:512] = _silu(_dot(hh, w_in_ref[:, 768:1280]))
        g_scr[rows, 512:1024] = _silu(_dot(hh, w_in_ref[:, 2048:2560]))
        v = _dot(hh, wv_scr[...])
        _store_v_variants(va_scr, rows, v[:, 0:LANES])
        _store_v_variants(vb_scr, rows, v[:, LANES:2 * LANES])

        kt = _dot_nt(wkt_scr[0:2 * LANES, :], hh)
        heads = [kt[HEAD_DIM * h:HEAD_DIM * (h + 1), :] for h in range(2)]
        kat = jnp.concatenate([blk * lax.rsqrt(jnp.mean(blk * blk, axis=0, keepdims=True) + EPS) for blk in heads],
                              axis=0) * knt
        kbt = kt[LANES:2 * LANES, :]
        if not latent:
            vt = v.T
            nakt_ref[i] = kat
            nbkt_ref[i] = kbt
            navt_ref[i] = vt[0:LANES, :]
            nbvt_ref[i] = vt[LANES:2 * LANES, :]
        _store_kt_variants(ka_scr, i, rot_t(kat))
        _store_kt_variants(kb_scr, i, rot_t(kbt))
        return carry

    lax.fori_loop(0, n_rows // ROW_CHUNK, proj, 0, unroll=2)

    sinks = [sink_ref[h] * LOG2E for h in range(8)]
    ck = ROW_CHUNK
    bk = kb_scr.shape[-1]
    win = ROW_CHUNK + 2 * WINDOW
    n_items = 8

    def block_stages(i):
        rows = _chunk_rows(i)
        if latent:
            a_chunks = list(range(n_lat_chunks + n_past // ck))
            a_keys = pl.ds(0, seq + n_past)
            w0 = jnp.clip(i * (ck // bk) - WINDOW // bk, 0, (seq - win) // bk)
            win_rows = pl.ds(pl.multiple_of(w0 * bk, bk), win)
            dist = (lax.broadcasted_iota(jnp.int32, (ROW_CHUNK, ck), 1)
                    - lax.broadcasted_iota(jnp.int32, (ROW_CHUNK, ck), 0))
            for c in range(win // ck):
                off = w0 * bk + c * ck - i * ck
                bias_scr[c] = jnp.where(jnp.abs(dist + off) <= WINDOW, 0.0, NEG_INF).astype(F32)
            b_first = [w0 + c * (ck // bk) for c in range(win // ck)] + [seq // bk]
            n_biased = win // ck
            b_cols = win + n_past
        else:
            a_chunks = [i]
            a_keys = rows
            b_first = [i * (ck // bk)]
            n_biased = 0
            b_cols = ck
        a_cols = len(a_chunks) * ck

        def qk(t, slot):
            p, branch = divmod(t, 2)
            cols = slice(LANES * p, LANES * (p + 1))
            kvh = p // 2
            q = (qb_scr if branch else qa_scr)[rows, cols]
            for par in (0, 1):
                var = 2 * kvh + par
                if branch:
                    tiles = [jnp.concatenate([kb_scr[var, first + d] for d in range(ck // bk)], axis=1)
                             for first in b_first]
                else:
                    tiles = [ka_scr[var, chunk] for chunk in a_chunks]
                for c, kt in enumerate(tiles):
                    s = _dot(q, kt)
                    if branch and c < n_biased:
                        s = s + bias_scr[c]
                    s_scr[slot, par, :, c * ck:(c + 1) * ck] = s

        def softmax(t, slot):
            p, branch = divmod(t, 2)
            n_cols = b_cols if branch else a_cols
            rb = _softmax_rows(n_cols)
            for par in (0, 1):
                for r in range(ROW_CHUNK // rb):
                    sub = slice(r * rb, (r + 1) * rb)
                    s = s_scr[slot, par, sub, 0:n_cols]
                    m = jnp.max(s, axis=1, keepdims=True)
                    if branch:
                        sink = sinks[2 * p + par]
                        m = jnp.maximum(m, sink)
                        es_scr[slot, sub, HEAD_DIM * par:HEAD_DIM * (par + 1)] = jnp.broadcast_to(
                            jnp.exp2(sink - m), (rb, HEAD_DIM))
                    p_scr[slot, par, sub, 0:n_cols] = jnp.exp2((s - m).astype(BF16))

        def pv(t, slot):
            p, branch = divmod(t, 2)
            kvh = p // 2
            v_scr = vb_scr if branch else va_scr
            accs = []
            for par in (0, 1):
                var = 2 * kvh + par
                if latent and branch:
                    n_loc = win
                    accs.append(_dot(p_scr[slot, par, :, 0:n_loc], v_scr[var, win_rows, :])
                                + _dot(p_scr[slot, par, :, n_loc:b_cols], v_scr[var, past_rows, :]))
                else:
                    accs.append(_dot(p_scr[slot, par, :, 0:a_cols], v_scr[var, a_keys, :]))
            lo = _lane_iota(ROW_CHUNK) < HEAD_DIM
            denom = pltpu.roll(jnp.where(lo, accs[1], accs[0]), HEAD_DIM, 1)
            if branch:
                denom = denom + es_scr[slot]
            o = jnp.where(lo, accs[0], accs[1]) / denom
            ocols = slice(512 * branch + LANES * p, 512 * branch + LANES * (p + 1))
            ha_scr[rows, ocols] = (o * g_scr[rows, ocols]).astype(BF16)

        return qk, softmax, pv

    _attend_blocks(block_stages, n_rows // ROW_CHUNK, n_items, unrolled=not latent)

    if not latent:
        weights.load_out_proj()
    _out_proj_norm(x_ref, mod_ref, mod_row, ha_scr, w_out_ref, lng_ref, lnb_ref, layer, y_ref, n_rows, alpha)
    if not latent:
        weights.finish()


def _odd_kernel(latent, layer, n_rows, seq, alpha, lam_init, *refs):
    if latent:
        (x_ref, mod_ref, w_in_ref, w_out_ref, lq1_ref, lk1_ref, lq2_ref, lk2_ref, sub_ref, lng_ref, lnb_ref,
         cos_ref, sin_ref, cck_hbm, ccv_hbm,
         y_ref,
         ha_scr, q_scr, k_scr, v_scr, g_scr, s_scr, p_scr, past_stage, past_sems) = refs
    else:
        (x_ref, mod_ref, w_in_hbm, w_out_hbm, lq1_ref, lk1_ref, lq2_ref, lk2_ref, sub_ref, lng_ref, lnb_ref,
         y_ref, nck_hbm, ncv_hbm, w_in_bf_hbm, w_out_bf_hbm,
         ha_scr, q_scr, k_scr, v_scr, g_scr, s_scr, p_scr, kv_stage, kv_sems,
         w_in_ref, w_out_ref, w_stage, w_sems, w_out_sems) = refs

    step = pl.program_id(0)
    if not latent:
        weights = _ContextWeights(step, (w_in_hbm, w_in_ref, w_in_bf_hbm), (w_out_hbm, w_out_ref, w_out_bf_hbm),
                                  w_stage, w_sems, w_out_sems)
        weights.load_in_proj()
    mod_row = step + 1 if latent else 0
    _modulate(x_ref, mod_ref, mod_row, ha_scr, n_rows)

    n_heads = D_MODEL // LANES
    n_blocks = n_rows // ROW_CHUNK
    lo = _lane_iota(ROW_CHUNK) < HEAD_DIM

    def kv_out_copies(blk):
        elem = step * n_blocks + blk
        return [pltpu.make_async_copy(kv_stage.at[blk, t, :, pl.ds(LANES * h, LANES)],
                                      out.at[elem, 0, :, h, :], kv_sems.at[blk, t])
                for t, out in enumerate((nck_hbm, ncv_hbm)) for h in range(n_heads)]

    def store_k(rows, h, a):
        cols = slice(LANES * h, LANES * (h + 1))
        zero = jnp.zeros_like(a)
        k_scr[0, rows, cols] = jnp.where(lo, a, zero).astype(BF16)
        k_scr[1, rows, cols] = jnp.where(lo, zero, a).astype(BF16)

    if latent:
        n_past = cck_hbm.shape[2]
        past = pl.ds(seq, n_past)
        past_copies = [pltpu.make_async_copy(cache.at[step, layer // 2, :, h, :],
                                             past_stage.at[t, :, pl.ds(LANES * h, LANES)], past_sems.at[t])
                       for t, cache in enumerate((cck_hbm, ccv_hbm)) for h in range(n_heads)]
        for copy in past_copies:
            copy.start()

    def proj(i, carry):
        rows = _chunk_rows(i)
        hh = ha_scr[rows, :]
        if latent:
            cos = cos_ref[rows, :]
            sin = sin_ref[rows, :]
            rot = lambda a: _rope(a, cos, sin)
        else:
            rot = lambda a: a
        for half in range(2):
            acc = _dot(hh, w_in_ref[:, 512 * half:512 * (half + 1)])
            for j in range(4):
                a = rot(acc[:, LANES * j:LANES * (j + 1)])
                cols = slice(512 * half + LANES * j, 512 * half + LANES * (j + 1))
                q_scr[rows, cols] = (a * Q_SCALE).astype(BF16)
        for half in range(2):
            acc = _dot(hh, w_in_ref[:, 1024 + 512 * half:1024 + 512 * (half + 1)])
            if not latent:
                kv_stage[i, 0, :, 512 * half:512 * (half + 1)] = acc
            for j in range(4):
                store_k(rows, 4 * half + j, rot(acc[:, LANES * j:LANES * (j + 1)]))
        for half in range(2):
            acc = _dot(hh, w_in_ref[:, 2048 + 512 * half:2048 + 512 * (half + 1)])
            if not latent:
                kv_stage[i, 1, :, 512 * half:512 * (half + 1)] = acc
            v_scr[rows, 512 * half:512 * (half + 1)] = acc.astype(BF16)
        if not latent:
            for copy in kv_out_copies(i):
                copy.start()
        for half in range(2):
            acc = _dot(hh, w_in_ref[:, 3072 + 512 * half:3072 + 512 * (half + 1)])
            g_scr[rows, 512 * half:512 * (half + 1)] = _silu(acc)
        return carry

    if latent:
        lax.fori_loop(0, n_blocks, proj, 0, unroll=2)
    else:
        for blk in range(n_blocks):
            proj(blk, 0)

    if latent:
        for copy in past_copies:
            copy.wait()
        for h in range(n_heads):
            store_k(past, h, past_stage[0, :, LANES * h:LANES * (h + 1)])
        v_scr[past, :] = past_stage[1].astype(BF16)

    lam = (jnp.exp(jnp.sum(lq1_ref[...] * lk1_ref[...], axis=1, keepdims=True))
           - jnp.exp(jnp.sum(lq2_ref[...] * lk2_ref[...], axis=1, keepdims=True)) + lam_init)
    sub = sub_ref[...] * (1.0 - lam_init)
    n_keys = seq + n_past if latent else ROW_CHUNK
    rb = _softmax_rows(n_keys)
    ones = jnp.ones((n_keys, LANES), BF16)

    def block_stages(i):
        rows = _chunk_rows(i)
        keys = pl.ds(0, n_keys) if latent else rows

        def qk(h, slot):
            cols = slice(LANES * h, LANES * (h + 1))
            q = q_scr[rows, cols]
            for m in (0, 1):
                s_scr[slot, m] = _dot_nt(q, k_scr[m, keys, cols])

        def softmax(h, slot):
            for m in (0, 1):
                for r in range(ROW_CHUNK // rb):
                    sub_rows = slice(r * rb, (r + 1) * rb)
                    s = s_scr[slot, m, sub_rows, :]
                    top = jnp.max(s, axis=1, keepdims=True)
                    p_scr[slot, m, sub_rows, :] = jnp.exp2((s - top).astype(BF16))

        def pv(h, slot):
            cols = slice(LANES * h, LANES * (h + 1))
            v_ext = jnp.concatenate([v_scr[keys, cols], ones], axis=1)
            maps = []
            for m in (0, 1):
                acc = _dot(p_scr[slot, m], v_ext)
                maps.append(acc[:, 0:LANES] / acc[:, LANES:2 * LANES])
            o = maps[0] - lam * maps[1]
            ms = jnp.mean(o * o, axis=1, keepdims=True)
            o = o * lax.rsqrt(ms + EPS) * sub
            ha_scr[rows, cols] = (o * g_scr[rows, cols]).astype(BF16)

        return qk, softmax, pv

    _attend_blocks(block_stages, n_blocks, n_heads, unrolled=not latent)

    if not latent:
        weights.load_out_proj()
    _out_proj_norm(x_ref, mod_ref, mod_row, ha_scr, w_out_ref, lng_ref, lnb_ref, layer, y_ref, n_rows, alpha)

    if not latent:
        for blk in range(n_blocks):
            for copy in kv_out_copies(blk):
                copy.wait()
        weights.finish()


MOD_SLAB_ROWS = 128
MOD_COL_BLOCK = 1024
MOD_SLOTS = 4


def _mod_kernel(n_cond, cv_ref, w_hbm, b_ref, o_ref, sb_scr, ring, acc_scr, sems):
    depth, n_in, n_out = w_hbm.shape
    sublanes = 8
    slabs_per_layer = n_in // MOD_SLAB_ROWS
    slabs = [(l, rs) for l in range(depth) for rs in range(slabs_per_layer)]

    def slab_copy(n):
        l, rs = slabs[n]
        return pltpu.make_async_copy(w_hbm.at[l, pl.ds(rs * MOD_SLAB_ROWS, MOD_SLAB_ROWS), :],
                                     ring.at[n % MOD_SLOTS], sems.at[n % MOD_SLOTS])

    for n in range(min(MOD_SLOTS, len(slabs))):
        slab_copy(n).start()
    s_t = _silu(cv_ref[...]).T
    for r in range(n_cond):
        sb_scr[r] = jnp.broadcast_to(s_t[:, r:r + 1], (n_in, LANES))

    for n, (l, rs) in enumerate(slabs):
        slab_copy(n).wait()
        for cb in range(n_out // MOD_COL_BLOCK):
            cols = pl.ds(cb * MOD_COL_BLOCK, MOD_COL_BLOCK)
            if rs == 0:
                accs = (jnp.zeros((sublanes, MOD_COL_BLOCK), F32),) * n_cond
            else:
                accs = tuple(acc_scr[r, :, cols] for r in range(n_cond))

            def body(kb, accs, n=n, rs=rs, cols=cols):
                w = ring[n % MOD_SLOTS, pl.ds(pl.multiple_of(kb * sublanes, sublanes), sublanes), cols]
                s_rows = pl.ds(pl.multiple_of(rs * MOD_SLAB_ROWS + kb * sublanes, sublanes), sublanes)
                return tuple(acc + w * jnp.tile(sb_scr[r, s_rows, :], (1, MOD_COL_BLOCK // LANES))
                             for r, acc in enumerate(accs))

            accs = lax.fori_loop(0, MOD_SLAB_ROWS // sublanes, body, accs, unroll=8)
            if rs < slabs_per_layer - 1:
                for r in range(n_cond):
                    acc_scr[r, :, cols] = accs[r]
            else:
                rows = [jnp.sum(acc, axis=0, keepdims=True) + b_ref[l:l + 1, cols] for acc in accs]
                o_ref[l, :, cols] = jnp.concatenate(rows + [jnp.zeros((8 - n_cond, MOD_COL_BLOCK), F32)], axis=0)
        if n + MOD_SLOTS < len(slabs):
            slab_copy(n + MOD_SLOTS).start()


def _full(shape, **kw):
    zeros = (0,) * len(shape)
    return pl.BlockSpec(shape, lambda i: zeros, **kw)


def _weight_specs(latent, w_in, w_out):
    if latent:
        single = pl.Buffered(1)
        return [_full(w_in.shape, pipeline_mode=single), _full(w_out.shape, pipeline_mode=single)]
    return [pl.BlockSpec(memory_space=pl.ANY), pl.BlockSpec(memory_space=pl.ANY)]


def _weight_scratch(w_in, w_out):
    assert w_in.shape[0] % W_SLAB_ROWS == 0 and w_out.shape[0] % W_SLAB_ROWS == 0
    n_slabs = max(w_in.shape[0], w_out.shape[0]) // W_SLAB_ROWS
    return [pltpu.VMEM(w_in.shape, BF16), pltpu.VMEM(w_out.shape, BF16),
            pltpu.VMEM((n_slabs, W_SLAB_ROWS, max(w_in.shape[1], w_out.shape[1])), F32),
            pltpu.SemaphoreType.DMA((n_slabs,)), pltpu.SemaphoreType.DMA((2,))]


def _rope_tables(seq):
    t = np.arange(seq)
    n_freq = HEAD_DIM // 4
    freqs = ROPE_THETA ** (-np.arange(n_freq, dtype=np.float64) / n_freq)
    ang_row = (t // GRID_W)[:, None] * freqs
    ang_col = (t % GRID_W)[:, None] * freqs
    ang = np.concatenate([ang_row, ang_row, ang_col, ang_col], axis=1)
    sign = np.concatenate([-np.ones(n_freq), np.ones(n_freq)] * 2)[None, :]
    cos = np.tile(np.cos(ang), (1, 2)).astype(np.float32)
    sin = np.tile(np.sin(ang) * sign, (1, 2)).astype(np.float32)
    chunked_t = lambda a: a.reshape(seq // ROW_CHUNK, ROW_CHUNK, LANES).transpose(0, 2, 1)
    return jnp.asarray(cos), jnp.asarray(sin), jnp.asarray(chunked_t(cos)), jnp.asarray(chunked_t(sin))


def _modulation(c, c_ctx, w_mod, b_mod):
    depth = w_mod.shape[0]
    n_cond = 1 + c.shape[0]
    cv = jnp.concatenate([c_ctx[None, :], c, jnp.zeros((8 - n_cond, D_MODEL), F32)], axis=0)
    assert D_MODEL % MOD_SLAB_ROWS == 0 and (3 * D_MODEL) % MOD_COL_BLOCK == 0
    return pl.pallas_call(
        functools.partial(_mod_kernel, n_cond),
        grid=(1,),
        in_specs=[_full(cv.shape), pl.BlockSpec(memory_space=pl.ANY), _full(b_mod.shape)],
        out_specs=_full((depth, 8, 3 * D_MODEL)),
        out_shape=jax.ShapeDtypeStruct((depth, 8, 3 * D_MODEL), F32),
        scratch_shapes=[pltpu.VMEM((n_cond, D_MODEL, LANES), F32),
                        pltpu.VMEM((MOD_SLOTS, MOD_SLAB_ROWS, 3 * D_MODEL), F32),
                        pltpu.VMEM((n_cond, 8, 3 * D_MODEL), F32),
                        pltpu.SemaphoreType.DMA((MOD_SLOTS,))],
        compiler_params=pltpu.CompilerParams(dimension_semantics=("arbitrary",)),
        name="adaln_modulation",
    )(cv, w_mod, b_mod)


def _even_layer(x, mod, layer, w_in, w_out, q_norm, k_norm, sink, ln_g, ln_b, latent, seq, n_rows, alpha, extras=()):
    total = x.shape[0]
    grid = (total // n_rows,)
    single = pl.Buffered(1)
    norms = jnp.concatenate([jnp.broadcast_to(jnp.tile(k_norm, 2)[:, None], (LANES, ROW_CHUNK)),
                             jnp.broadcast_to(jnp.tile(q_norm, 2 * ROW_CHUNK // LANES)[None, :], (8, ROW_CHUNK))], axis=0)

    row_blk = lambda width: pl.BlockSpec((n_rows, width), lambda i: (i, 0))
    in_specs = [row_blk(D_MODEL),
                pl.BlockSpec((1, 8, 3 * D_MODEL), lambda i: (layer, 0, 0)),
                *_weight_specs(latent, w_in, w_out),
                _full(norms.shape),
                pl.BlockSpec(memory_space=pltpu.SMEM),
                _full(ln_g.shape), _full(ln_b.shape)]
    args = [x, mod, w_in, w_out, norms, sink, ln_g, ln_b]
    y_shape = jax.ShapeDtypeStruct((total, D_MODEL), F32)
    n_blocks = n_rows // ROW_CHUNK
    if latent:
        cos, sin, cos_t, sin_t, cakt, cav, cbkt, cbv = extras
        n_past = cav.shape[2]
        in_specs += [_full(cos.shape, pipeline_mode=single), _full(sin.shape, pipeline_mode=single),
                     _full(cos_t.shape, pipeline_mode=single), _full(sin_t.shape, pipeline_mode=single)]
        in_specs += [pl.BlockSpec((1, LANES, n_past), lambda i: (i, 0, 0))] * 4
        args += [cos, sin, cos_t, sin_t, cakt, cav, cbkt, cbv]
        out_specs = row_blk(D_MODEL)
        out_shape = y_shape
        n_keys = seq + n_past
    else:
        kv_blk = pl.BlockSpec((n_blocks, LANES, ROW_CHUNK), lambda i: (i, 0, 0))
        hbm = pl.BlockSpec(memory_space=pl.ANY)
        out_specs = [row_blk(D_MODEL)] + [kv_blk] * 4 + [hbm, hbm]
        out_shape = ([y_shape] + [jax.ShapeDtypeStruct((total // seq, LANES, seq), F32)] * 4
                     + [jax.ShapeDtypeStruct(w_in.shape, BF16), jax.ShapeDtypeStruct(w_out.shape, BF16)])
        n_keys = n_rows
    n_kchunks = n_keys // ROW_CHUNK
    n_cols = n_keys if latent else ROW_CHUNK
    scratch = [pltpu.VMEM((n_rows, D_MODEL), BF16),
               pltpu.VMEM((n_rows, 512), BF16), pltpu.VMEM((n_rows, 512), BF16),
               pltpu.VMEM((4, n_kchunks, LANES, ROW_CHUNK), BF16), pltpu.VMEM((4, n_keys, LANES), BF16),
               pltpu.VMEM((4, n_keys // WINDOW, LANES, WINDOW), BF16), pltpu.VMEM((4, n_keys, LANES), BF16),
               pltpu.VMEM((n_rows, D_MODEL), F32),
               pltpu.VMEM((2, 2, ROW_CHUNK, n_cols), F32),
               pltpu.VMEM((2, 2, ROW_CHUNK, n_cols), BF16),
               pltpu.VMEM((2, ROW_CHUNK, LANES), F32),
               pltpu.VMEM((2 * LANES, D_MODEL), BF16),
               pltpu.VMEM((D_MODEL, 2 * LANES), BF16)]
    if latent:
        scratch.append(pltpu.VMEM((1 + 2 * WINDOW // ROW_CHUNK, ROW_CHUNK, ROW_CHUNK), F32))
    else:
        scratch += _weight_scratch(w_in, w_out)
    return pl.pallas_call(
        functools.partial(_even_kernel, latent, layer, n_rows, seq, alpha),
        grid=grid, in_specs=in_specs, out_specs=out_specs, out_shape=out_shape,
        scratch_shapes=scratch,
        compiler_params=pltpu.CompilerParams(dimension_semantics=("arbitrary",), vmem_limit_bytes=VMEM_LIMIT),
        name="even_layer_latent" if latent else "even_layer_context",
    )(*args)


def _odd_layer(x, mod, layer, w_in, w_out, lams, sub, ln_g, ln_b, latent, seq, n_rows, alpha, lam_init, extras=()):
    total = x.shape[0]
    grid = (total // n_rows,)
    row_blk = lambda width: pl.BlockSpec((n_rows, width), lambda i: (i, 0))
    single = pl.Buffered(1)
    in_specs = [row_blk(D_MODEL),
                pl.BlockSpec((1, 8, 3 * D_MODEL), lambda i: (layer, 0, 0)),
                *_weight_specs(latent, w_in, w_out),
                _full((1, HEAD_DIM)), _full((1, HEAD_DIM)), _full((1, HEAD_DIM)), _full((1, HEAD_DIM)),
                _full((1, LANES)),
                _full(ln_g.shape), _full(ln_b.shape)]
    args = [x, mod, w_in, w_out, *lams, sub, ln_g, ln_b]
    y_shape = jax.ShapeDtypeStruct((total, D_MODEL), F32)
    n_heads = D_MODEL // LANES
    n_blocks = n_rows // ROW_CHUNK
    if latent:
        cos, sin, cck, ccv = extras
        n_past = cck.shape[2]
        in_specs += [_full(cos.shape, pipeline_mode=single), _full(sin.shape, pipeline_mode=single)]
        in_specs += [pl.BlockSpec(memory_space=pl.ANY)] * 2
        args += [cos, sin, cck, ccv]
        out_specs = row_blk(D_MODEL)
        out_shape = y_shape
        n_keys = seq + n_past
    else:
        hbm = pl.BlockSpec(memory_space=pl.ANY)
        out_specs = [row_blk(D_MODEL), hbm, hbm, hbm, hbm]
        out_shape = ([y_shape] + [jax.ShapeDtypeStruct((total // seq, 1, seq, n_heads, LANES), F32)] * 2
                     + [jax.ShapeDtypeStruct(w_in.shape, BF16), jax.ShapeDtypeStruct(w_out.shape, BF16)])
        n_keys = n_rows
    n_cols = n_keys if latent else ROW_CHUNK
    scratch = [pltpu.VMEM((n_rows, D_MODEL), BF16),
               pltpu.VMEM((n_rows, D_MODEL), BF16),
               pltpu.VMEM((2, n_keys, D_MODEL), BF16),
               pltpu.VMEM((n_keys, D_MODEL), BF16),
               pltpu.VMEM((n_rows, D_MODEL), F32),
               pltpu.VMEM((2, 2, ROW_CHUNK, n_cols), F32),
               pltpu.VMEM((2, 2, ROW_CHUNK, n_cols), BF16)]
    if latent:
        scratch += [pltpu.VMEM((2, n_past, D_MODEL), F32), pltpu.SemaphoreType.DMA((2,))]
    else:
        scratch += [pltpu.VMEM((n_blocks, 2, ROW_CHUNK, D_MODEL), F32),
                    pltpu.SemaphoreType.DMA((n_blocks, 2))]
        scratch += _weight_scratch(w_in, w_out)
    return pl.pallas_call(
        functools.partial(_odd_kernel, latent, layer, n_rows, seq, alpha, lam_init),
        grid=grid, in_specs=in_specs, out_specs=out_specs, out_shape=out_shape,
        scratch_shapes=scratch,
        compiler_params=pltpu.CompilerParams(dimension_semantics=("arbitrary",), vmem_limit_bytes=VMEM_LIMIT),
        name="odd_layer_latent" if latent else "odd_layer_context",
    )(*args)


def kernel(x_prompt, x_sample, cache_a_k, cache_a_v, cache_b_k, cache_b_v, cache_c_k, cache_c_v, c, c_ctx,
           w_mod, b_mod, ln_g, ln_b, w_in_even, w_out_even, q_norm_a, k_norm_a, sink_b, w_in_odd, w_out_odd,
           lambda_q1, lambda_k1, lambda_q2, lambda_k2, subln_c):
    depth = w_mod.shape[0]
    batch, seq, _ = x_prompt.shape
    dec_batch, dec_seq, _ = x_sample.shape
    n_past = cache_a_k.shape[2]
    alpha = (2 * depth) ** 0.25
    assert seq == ROW_CHUNK and n_past % ROW_CHUNK == 0 and dec_seq % ROW_CHUNK == 0

    mod = _modulation(c, c_ctx, w_mod, b_mod)
    cos, sin, cos_t, sin_t = _rope_tables(dec_seq)

    bf16_weights = {}

    def run(x, latent, n_batch, s, rows_even, rows_odd):
        kv = {"a_k": [], "a_v": [], "b_k": [], "b_v": [], "c_k": [], "c_v": []}
        for l in range(depth):
            if l % 2 == 0:
                e = l // 2
                extras = ()
                if latent:
                    k_t = lambda t: t[:, e].transpose(0, 2, 3, 1).reshape(n_batch, LANES, n_past)
                    extras = (cos, sin, cos_t, sin_t,
                              k_t(cache_a_k), k_t(cache_a_v), k_t(cache_b_k), k_t(cache_b_v))
                w_in, w_out = bf16_weights[l] if latent else (w_in_even[e], w_out_even[e])
                res = _even_layer(x, mod, l, w_in, w_out, q_norm_a[e], k_norm_a[e],
                                  sink_b[e], ln_g, ln_b, latent, s, rows_even, alpha, extras)
                if latent:
                    x = res
                else:
                    x = res[0]
                    bf16_weights[l] = res[5:7]
                    for name, t in zip(("a_k", "a_v", "b_k", "b_v"), res[1:5]):
                        kv[name].append(t.reshape(n_batch, 2, HEAD_DIM, s).transpose(0, 3, 1, 2))
            else:
                o = l // 2
                lam_init = 0.8 - 0.6 * math.exp(-0.3 * l)
                extras = (cos, sin, cache_c_k, cache_c_v) if latent else ()
                lams = [t[o][None, :] for t in (lambda_q1, lambda_k1, lambda_q2, lambda_k2)]
                w_in, w_out = bf16_weights[l] if latent else (w_in_odd[o], w_out_odd[o])
                res = _odd_layer(x, mod, l, w_in, w_out, lams,
                                 subln_c[o][None, :], ln_g, ln_b, latent, s, rows_odd, alpha, lam_init, extras)
                if latent:
                    x = res
                else:
                    x = res[0]
                    bf16_weights[l] = res[3:5]
                    kv["c_k"].append(res[1][:, 0])
                    kv["c_v"].append(res[2][:, 0])
        return x, kv

    y_ctx, kv = run(x_prompt.reshape(batch * seq, D_MODEL), False, batch, seq, 1024, 512)
    y_lat, _ = run(x_sample.reshape(dec_batch * dec_seq, D_MODEL), True, dec_batch, dec_seq, dec_seq, dec_seq)

    stack = lambda name: jnp.stack(kv[name], axis=1)
    return (y_ctx.reshape(batch, seq, D_MODEL), y_lat.reshape(dec_batch, dec_seq, D_MODEL),
            stack("a_k"), stack("a_v"), stack("b_k"), stack("b_v"), stack("c_k"), stack("c_v"))
```

```python
import functools
import math

import jax
import jax.numpy as jnp
import numpy as np
from jax import lax
from jax.experimental import pallas as pl
from jax.experimental.pallas import tpu as pltpu

F32 = jnp.float32
BF16 = jnp.bfloat16

D_MODEL = 1024
HEAD_DIM = 64
GRID_W = 64
WINDOW = 128
ROPE_THETA = 10000.0
EPS = 1e-6
NEG_INF = -1e30
LOG2E = 1.4426950408889634
Q_SCALE = HEAD_DIM ** -0.5 * LOG2E
LANES = 128
ROW_CHUNK = 256
SOFTMAX_VREGS = 40
VMEM_LIMIT = 60000 * 1024
W_SLAB_ROWS = 128


def _silu(x):
    return x / (1.0 + jnp.exp(-x))


def _dot(a, b):
    return jnp.dot(a, b, preferred_element_type=F32)


def _dot_nt(a, b):
    return lax.dot_general(a, b, (((1,), (1,)), ((), ())), preferred_element_type=F32)


def _lane_iota(rows):
    return lax.broadcasted_iota(jnp.int32, (rows, LANES), 1)


def _chunk_rows(i):
    if isinstance(i, int):
        return pl.ds(i * ROW_CHUNK, ROW_CHUNK)
    return pl.ds(pl.multiple_of(i * ROW_CHUNK, ROW_CHUNK), ROW_CHUNK)


def _softmax_rows(n_cols):
    rows = 8
    while rows * 2 * n_cols <= SOFTMAX_VREGS * 1024 and rows * 2 <= ROW_CHUNK:
        rows *= 2
    return rows


def _rope(a, cos, sin_signed):
    lane = _lane_iota(a.shape[0])
    fwd = pltpu.roll(a, LANES - 16, 1)
    bwd = pltpu.roll(a, 16, 1)
    partner = jnp.where((lane & 16) == 0, fwd, bwd)
    return a * cos + partner * sin_signed


def _rope_t(a, cos_t, sin_t):
    blocks = [a[16 * b:16 * (b + 1), :] for b in range(a.shape[0] // 16)]
    partner = jnp.concatenate([blocks[b ^ 1] for b in range(len(blocks))], axis=0)
    return a * cos_t + partner * sin_t


def _store_kt_variants(scr, chunk, kt):
    width = scr.shape[-1]
    per_block = kt.shape[1] // width
    zero = jnp.zeros((HEAD_DIM, kt.shape[1]), F32)
    for j in range(2):
        kj = kt[HEAD_DIM * j:HEAD_DIM * (j + 1), :]
        for par, full in enumerate((jnp.concatenate([kj, zero], axis=0), jnp.concatenate([zero, kj], axis=0))):
            full = full.astype(BF16)
            for c in range(per_block):
                scr[2 * j + par, chunk * per_block + c] = full[:, width * c:width * (c + 1)]


def _store_v_variants(scr, rows, a):
    lane = _lane_iota(a.shape[0])
    lo = lane < HEAD_DIM
    swapped = pltpu.roll(a, HEAD_DIM, 1)
    one = jnp.ones_like(a)
    scr[0, rows, :] = jnp.where(lo, a, one).astype(BF16)
    scr[1, rows, :] = jnp.where(lo, one, swapped).astype(BF16)
    scr[2, rows, :] = jnp.where(lo, swapped, one).astype(BF16)
    scr[3, rows, :] = jnp.where(lo, one, a).astype(BF16)


def _layer_norm_rows(z, g, b):
    mu = jnp.mean(z, axis=-1, keepdims=True)
    zc = z - mu
    var = jnp.mean(zc * zc, axis=-1, keepdims=True)
    return zc * lax.rsqrt(var + EPS) * g + b


def _modulate(x_ref, mod_ref, mod_row, h_scr, n_rows):
    shift = mod_ref[0, pl.ds(mod_row, 1), 0:D_MODEL]
    scale = mod_ref[0, pl.ds(mod_row, 1), D_MODEL:2 * D_MODEL]

    def body(i, carry):
        rows = _chunk_rows(i)
        h_scr[rows, :] = (x_ref[rows, :] * (1.0 + scale) + shift).astype(BF16)
        return carry

    lax.fori_loop(0, n_rows // ROW_CHUNK, body, 0)


def _out_proj_norm(x_ref, mod_ref, mod_row, attn_scr, w_out_ref, lng_ref, lnb_ref, layer, y_ref, n_rows, alpha):
    gate = mod_ref[0, pl.ds(mod_row, 1), 2 * D_MODEL:3 * D_MODEL]
    g = lng_ref[layer:layer + 1, :]
    b = lnb_ref[layer:layer + 1, :]

    def body(i, carry):
        rows = _chunk_rows(i)
        out = _dot(attn_scr[rows, :], w_out_ref[...])
        z = alpha * x_ref[rows, :] + gate * out
        y_ref[rows, :] = _layer_norm_rows(z, g, b)
        return carry

    lax.fori_loop(0, n_rows // ROW_CHUNK, body, 0, unroll=True)


class _LateWeight:
    def __init__(self, step, w_hbm, w_scr, sem):
        self.step, self.copy = step, pltpu.make_async_copy(w_hbm, w_scr, sem.at[0])

    def start(self):
        @pl.when(self.step == 0)
        def _():
            self.copy.start()

    def wait(self):
        @pl.when(self.step == 0)
        def _():
            self.copy.wait()


class _ContextWeights:
    def __init__(self, step, w_in, w_out, stage, sems, out_sems):
        self.step, self.w_in, self.w_out, self.stage, self.sems = step, w_in, w_out, stage, sems
        self.out_copies = [pltpu.make_async_copy(w[1], w[2], out_sems.at[n]) for n, w in enumerate((w_in, w_out))]

    def _slab_copies(self, w_hbm):
        n_cols = w_hbm.shape[1]
        return [pltpu.make_async_copy(w_hbm.at[pl.ds(s * W_SLAB_ROWS, W_SLAB_ROWS), :],
                                      self.stage.at[s, :, pl.ds(0, n_cols)], self.sems.at[s])
                for s in range(w_hbm.shape[0] // W_SLAB_ROWS)]

    def _cast(self, w_hbm, w_scr):
        n_cols = w_hbm.shape[1]
        for s, copy in enumerate(self._slab_copies(w_hbm)):
            copy.wait()
            w_scr[pl.ds(s * W_SLAB_ROWS, W_SLAB_ROWS), :] = self.stage[s, :, 0:n_cols].astype(BF16)

    def load_in_proj(self):
        @pl.when(self.step == 0)
        def _():
            for copy in self._slab_copies(self.w_in[0]):
                copy.start()
            self._cast(self.w_in[0], self.w_in[1])
            self.out_copies[0].start()
            for copy in self._slab_copies(self.w_out[0]):
                copy.start()

    def load_out_proj(self):
        @pl.when(self.step == 0)
        def _():
            self._cast(self.w_out[0], self.w_out[1])
            self.out_copies[1].start()

    def finish(self):
        @pl.when(self.step == 0)
        def _():
            for copy in self.out_copies:
                copy.wait()


def _run_pipeline(n_items, stages):
    for u in range(n_items + len(stages) - 1):
        for k, stage in enumerate(stages):
            t = u - k
            if 0 <= t < n_items:
                stage(t, t % 2)


def _attend_blocks(block_stages, n_blocks, n_items, unrolled):
    assert n_items % 2 == 0
    if unrolled:
        per_block = [block_stages(i) for i in range(n_blocks)]
        stages = [lambda g, slot, k=k: per_block[g // n_items][k](g % n_items, slot) for k in range(3)]
        _run_pipeline(n_blocks * n_items, stages)
    else:
        def body(i, carry):
            _run_pipeline(n_items, block_stages(i))
            return carry

        lax.fori_loop(0, n_blocks, body, 0)


def _even_kernel(latent, layer, n_rows, seq, alpha, *refs):
    if latent:
        (x_ref, mod_ref, w_in_ref, w_out_hbm, norms_ref, sink_ref, lng_ref, lnb_ref,
         cos_ref, sin_ref, cost_ref, sint_ref, cakt_ref, cav_ref, cbkt_ref, cbv_ref,
         y_ref,
         ha_scr, qa_scr, qb_scr, ka_scr, va_scr, kb_scr, vb_scr, g_scr, s_scr, p_scr, es_scr, wkt_scr, wv_scr,
         bias_scr, w_out_ref, w_out_sem) = refs
    else:
        (x_ref, mod_ref, w_in_hbm, w_out_hbm, norms_ref, sink_ref, lng_ref, lnb_ref,
         y_ref, nakt_ref, navt_ref, nbkt_ref, nbvt_ref, w_in_bf_hbm, w_out_bf_hbm,
         ha_scr, qa_scr, qb_scr, ka_scr, va_scr, kb_scr, vb_scr, g_scr, s_scr, p_scr, es_scr, wkt_scr,
         wv_scr, w_in_ref, w_out_ref, w_stage, w_sems, w_out_sems) = refs

    step = pl.program_id(0)
    if latent:
        weights = _LateWeight(step, w_out_hbm, w_out_ref, w_out_sem)
        weights.start()
    else:
        weights = _ContextWeights(step, (w_in_hbm, w_in_ref, w_in_bf_hbm), (w_out_hbm, w_out_ref, w_out_bf_hbm),
                                  w_stage, w_sems, w_out_sems)
        weights.load_in_proj()
    mod_row = step + 1 if latent else 0
    _modulate(x_ref, mod_ref, mod_row, ha_scr, n_rows)

    col_ka, col_va, col_kb, col_vb = 512, 640, 1792, 1920

    @pl.when(step == 0)
    def _():
        for r, c0 in enumerate((col_ka, col_kb)):
            wkt_scr[LANES * r:LANES * (r + 1), :] = w_in_ref[:, c0:c0 + LANES].T
        wv_scr[:, 0:LANES] = w_in_ref[:, col_va:col_va + LANES]
        wv_scr[:, LANES:2 * LANES] = w_in_ref[:, col_vb:col_vb + LANES]

    n_lat_chunks = seq // ROW_CHUNK
    if latent:
        n_past = cav_ref.shape[2]
        past_rows = pl.ds(seq, n_past)
        _store_kt_variants(ka_scr, n_lat_chunks, cakt_ref[0])
        _store_kt_variants(kb_scr, n_lat_chunks, cbkt_ref[0])
        _store_v_variants(va_scr, past_rows, cav_ref[0].T)
        _store_v_variants(vb_scr, past_rows, cbv_ref[0].T)

    knt = norms_ref[0:LANES, :]
    qn = norms_ref[LANES:LANES + 1, 0:LANES]

    def proj(i, carry):
        rows = _chunk_rows(i)
        hh = ha_scr[rows, :]
        if latent:
            cos = cos_ref[rows, :]
            sin = sin_ref[rows, :]
            rot = lambda a: _rope(a, cos, sin)
            rot_t = lambda a: _rope_t(a, cost_ref[i], sint_ref[i])
        else:
            rot = rot_t = lambda a: a

        acc = _dot(hh, w_in_ref[:, 0:512])
        lo_lanes = _lane_iota(ROW_CHUNK) < HEAD_DIM
        for j in range(4):
            a = acc[:, LANES * j:LANES * (j + 1)]
            sq = a * a
            first = jnp.sum(jnp.where(lo_lanes, sq, 0.0), axis=1, keepdims=True)
            second = jnp.sum(jnp.where(lo_lanes, 0.0, sq), axis=1, keepdims=True)
            ms = jnp.where(lo_lanes, first, second) * (1.0 / HEAD_DIM)
            a = rot(a * lax.rsqrt(ms + EPS) * qn)
            qa_scr[rows, LANES * j:LANES * (j + 1)] = (a * Q_SCALE).astype(BF16)
        acc = _dot(hh, w_in_ref[:, 1280:1792])
        for j in range(4):
            a = rot(acc[:, LANES * j:LANES * (j + 1)])
            qb_scr[rows, LANES * j:LANES * (j + 1)] = (a * Q_SCALE).astype(BF16)
        g_scr[rows, 0:512] = _silu(_dot(hh, w_in_ref[:, 768:1280]))
        g_scr[rows, 512:1024] = _silu(_dot(hh, w_in_ref[:, 2048:2560]))
        v = _dot(hh, wv_scr[...])
        _store_v_variants(va_scr, rows, v[:, 0:LANES])
        _store_v_variants(vb_scr, rows, v[:, LANES:2 * LANES])

        kt = _dot_nt(wkt_scr[0:2 * LANES, :], hh)
        heads = [kt[HEAD_DIM * h:HEAD_DIM * (h + 1), :] for h in range(2)]
        kat = jnp.concatenate([blk * lax.rsqrt(jnp.mean(blk * blk, axis=0, keepdims=True) + EPS) for blk in heads],
                              axis=0) * knt
        kbt = kt[LANES:2 * LANES, :]
        if not latent:
            vt = v.T
            nakt_ref[i] = kat
            nbkt_ref[i] = kbt
            navt_ref[i] = vt[0:LANES, :]
            nbvt_ref[i] = vt[LANES:2 * LANES, :]
        _store_kt_variants(ka_scr, i, rot_t(kat))
        _store_kt_variants(kb_scr, i, rot_t(kbt))
        return carry

    lax.fori_loop(0, n_rows // ROW_CHUNK, proj, 0, unroll=2)

    sinks = [sink_ref[h] * LOG2E for h in range(8)]
    ck = ROW_CHUNK
    bk = kb_scr.shape[-1]
    win = ROW_CHUNK + 2 * WINDOW
    n_items = 8

    def block_stages(i):
        rows = _chunk_rows(i)
        if latent:
            a_chunks = list(range(n_lat_chunks + n_past // ck))
            a_keys = pl.ds(0, seq + n_past)
            w0 = jnp.clip(i * (ck // bk) - WINDOW // bk, 0, (seq - win) // bk)
            win_rows = pl.ds(pl.multiple_of(w0 * bk, bk), win)
            dist = (lax.broadcasted_iota(jnp.int32, (ROW_CHUNK, ck), 1)
                    - lax.broadcasted_iota(jnp.int32, (ROW_CHUNK, ck), 0))
            for c in range(win // ck):
                off = w0 * bk + c * ck - i * ck
                bias_scr[c] = jnp.where(jnp.abs(dist + off) <= WINDOW, 0.0, NEG_INF).astype(F32)
            b_first = [w0 + c * (ck // bk) for c in range(win // ck)] + [seq // bk]
            n_biased = win // ck
            b_cols = win + n_past
        else:
            a_chunks = [i]
            a_keys = rows
            b_first = [i * (ck // bk)]
            n_biased = 0
            b_cols = ck
        a_cols = len(a_chunks) * ck

        def qk(t, slot):
            p, branch = divmod(t, 2)
            cols = slice(LANES * p, LANES * (p + 1))
            kvh = p // 2
            q = (qb_scr if branch else qa_scr)[rows, cols]
            for par in (0, 1):
                var = 2 * kvh + par
                if branch:
                    tiles = [jnp.concatenate([kb_scr[var, first + d] for d in range(ck // bk)], axis=1)
                             for first in b_first]
                else:
                    tiles = [ka_scr[var, chunk] for chunk in a_chunks]
                for c, kt in enumerate(tiles):
                    s = _dot(q, kt)
                    if branch and c < n_biased:
                        s = s + bias_scr[c]
                    s_scr[slot, par, :, c * ck:(c + 1) * ck] = s

        def softmax(t, slot):
            p, branch = divmod(t, 2)
            n_cols = b_cols if branch else a_cols
            rb = _softmax_rows(n_cols)
            for par in (0, 1):
                for r in range(ROW_CHUNK // rb):
                    sub = slice(r * rb, (r + 1) * rb)
                    s = s_scr[slot, par, sub, 0:n_cols]
                    m = jnp.max(s, axis=1, keepdims=True)
                    if branch:
                        sink = sinks[2 * p + par]
                        m = jnp.maximum(m, sink)
                        es_scr[slot, sub, HEAD_DIM * par:HEAD_DIM * (par + 1)] = jnp.broadcast_to(
                            jnp.exp2(sink - m), (rb, HEAD_DIM))
                    p_scr[slot, par, sub, 0:n_cols] = jnp.exp2((s - m).astype(BF16))

        def pv(t, slot):
            p, branch = divmod(t, 2)
            kvh = p // 2
            v_scr = vb_scr if branch else va_scr
            accs = []
            for par in (0, 1):
                var = 2 * kvh + par
                if latent and branch:
                    n_loc = win
                    accs.append(_dot(p_scr[slot, par, :, 0:n_loc], v_scr[var, win_rows, :])
                                + _dot(p_scr[slot, par, :, n_loc:b_cols], v_scr[var, past_rows, :]))
                else:
                    accs.append(_dot(p_scr[slot, par, :, 0:a_cols], v_scr[var, a_keys, :]))
            lo = _lane_iota(ROW_CHUNK) < HEAD_DIM
            denom = pltpu.roll(jnp.where(lo, accs[1], accs[0]), HEAD_DIM, 1)
            if branch:
                denom = denom + es_scr[slot]
            o = jnp.where(lo, accs[0], accs[1]) / denom
            ocols = slice(512 * branch + LANES * p, 512 * branch + LANES * (p + 1))
            ha_scr[rows, ocols] = (o * g_scr[rows, ocols]).astype(BF16)

        return qk, softmax, pv

    _attend_blocks(block_stages, n_rows // ROW_CHUNK, n_items, unrolled=not latent)

    if latent:
        weights.wait()
    else:
        weights.load_out_proj()
    _out_proj_norm(x_ref, mod_ref, mod_row, ha_scr, w_out_ref, lng_ref, lnb_ref, layer, y_ref, n_rows, alpha)
    if not latent:
        weights.finish()


def _odd_kernel(latent, layer, n_rows, seq, alpha, lam_init, *refs):
    if latent:
        (x_ref, mod_ref, w_in_ref, w_out_hbm, lq1_ref, lk1_ref, lq2_ref, lk2_ref, sub_ref, lng_ref, lnb_ref,
         cos_ref, sin_ref, cck_hbm, ccv_hbm,
         y_ref,
         ha_scr, q_scr, k_scr, v_scr, g_scr, s_scr, p_scr, past_stage, past_sems, w_out_ref, w_out_sem) = refs
    else:
        (x_ref, mod_ref, w_in_hbm, w_out_hbm, lq1_ref, lk1_ref, lq2_ref, lk2_ref, sub_ref, lng_ref, lnb_ref,
         y_ref, nck_hbm, ncv_hbm, w_in_bf_hbm, w_out_bf_hbm,
         ha_scr, q_scr, k_scr, v_scr, g_scr, s_scr, p_scr, kv_stage, kv_sems,
         w_in_ref, w_out_ref, w_stage, w_sems, w_out_sems) = refs

    step = pl.program_id(0)
    if latent:
        weights = _LateWeight(step, w_out_hbm, w_out_ref, w_out_sem)
        weights.start()
    else:
        weights = _ContextWeights(step, (w_in_hbm, w_in_ref, w_in_bf_hbm), (w_out_hbm, w_out_ref, w_out_bf_hbm),
                                  w_stage, w_sems, w_out_sems)
        weights.load_in_proj()
    mod_row = step + 1 if latent else 0
    _modulate(x_ref, mod_ref, mod_row, ha_scr, n_rows)

    n_heads = D_MODEL // LANES
    n_blocks = n_rows // ROW_CHUNK
    lo = _lane_iota(ROW_CHUNK) < HEAD_DIM

    def kv_out_copies(blk):
        elem = step * n_blocks + blk
        return [pltpu.make_async_copy(kv_stage.at[blk, t, :, pl.ds(LANES * h, LANES)],
                                      out.at[elem, 0, :, h, :], kv_sems.at[blk, t])
                for t, out in enumerate((nck_hbm, ncv_hbm)) for h in range(n_heads)]

    def store_k(rows, h, a):
        cols = slice(LANES * h, LANES * (h + 1))
        zero = jnp.zeros_like(a)
        k_scr[0, rows, cols] = jnp.where(lo, a, zero).astype(BF16)
        k_scr[1, rows, cols] = jnp.where(lo, zero, a).astype(BF16)

    if latent:
        n_past = cck_hbm.shape[2]
        past = pl.ds(seq, n_past)
        past_copies = [pltpu.make_async_copy(cache.at[step, layer // 2, :, h, :],
                                             past_stage.at[t, :, pl.ds(LANES * h, LANES)], past_sems.at[t])
                       for t, cache in enumerate((cck_hbm, ccv_hbm)) for h in range(n_heads)]
        for copy in past_copies:
            copy.start()

    def proj(i, carry):
        rows = _chunk_rows(i)
        hh = ha_scr[rows, :]
        if latent:
            cos = cos_ref[rows, :]
            sin = sin_ref[rows, :]
            rot = lambda a: _rope(a, cos, sin)
        else:
            rot = lambda a: a
        for half in range(2):
            acc = _dot(hh, w_in_ref[:, 512 * half:512 * (half + 1)])
            for j in range(4):
                a = rot(acc[:, LANES * j:LANES * (j + 1)])
                cols = slice(512 * half + LANES * j, 512 * half + LANES * (j + 1))
                q_scr[rows, cols] = (a * Q_SCALE).astype(BF16)
        for half in range(2):
            acc = _dot(hh, w_in_ref[:, 1024 + 512 * half:1024 + 512 * (half + 1)])
            if not latent:
                kv_stage[i, 0, :, 512 * half:512 * (half + 1)] = acc
            for j in range(4):
                store_k(rows, 4 * half + j, rot(acc[:, LANES * j:LANES * (j + 1)]))
        for half in range(2):
            acc = _dot(hh, w_in_ref[:, 2048 + 512 * half:2048 + 512 * (half + 1)])
            if not latent:
                kv_stage[i, 1, :, 512 * half:512 * (half + 1)] = acc
            v_scr[rows, 512 * half:512 * (half + 1)] = acc.astype(BF16)
        if not latent:
            for copy in kv_out_copies(i):
                copy.start()
        for half in range(2):
            acc = _dot(hh, w_in_ref[:, 3072 + 512 * half:3072 + 512 * (half + 1)])
            g_scr[rows, 512 * half:512 * (half + 1)] = _silu(acc)
        return carry

    if latent:
        lax.fori_loop(0, n_blocks, proj, 0, unroll=2)
    else:
        for blk in range(n_blocks):
            proj(blk, 0)

    if latent:
        for copy in past_copies:
            copy.wait()
        for h in range(n_heads):
            store_k(past, h, past_stage[0, :, LANES * h:LANES * (h + 1)])
        v_scr[past, :] = past_stage[1].astype(BF16)

    lam = (jnp.exp(jnp.sum(lq1_ref[...] * lk1_ref[...], axis=1, keepdims=True))
           - jnp.exp(jnp.sum(lq2_ref[...] * lk2_ref[...], axis=1, keepdims=True)) + lam_init)
    sub = sub_ref[...] * (1.0 - lam_init)
    n_keys = seq + n_past if latent else ROW_CHUNK
    rb = _softmax_rows(n_keys)
    ones = jnp.ones((n_keys, LANES), BF16)

    def block_stages(i):
        rows = _chunk_rows(i)
        keys = pl.ds(0, n_keys) if latent else rows

        def qk(h, slot):
            cols = slice(LANES * h, LANES * (h + 1))
            q = q_scr[rows, cols]
            for m in (0, 1):
                s_scr[slot, m] = _dot_nt(q, k_scr[m, keys, cols])

        def softmax(h, slot):
            for m in (0, 1):
                for r in range(ROW_CHUNK // rb):
                    sub_rows = slice(r * rb, (r + 1) * rb)
                    s = s_scr[slot, m, sub_rows, :]
                    top = jnp.max(s, axis=1, keepdims=True)
                    p_scr[slot, m, sub_rows, :] = jnp.exp2((s - top).astype(BF16))

        def pv(h, slot):
            cols = slice(LANES * h, LANES * (h + 1))
            v_ext = jnp.concatenate([v_scr[keys, cols], ones], axis=1)
            maps = []
            for m in (0, 1):
                acc = _dot(p_scr[slot, m], v_ext)
                maps.append(acc[:, 0:LANES] / acc[:, LANES:2 * LANES])
            o = maps[0] - lam * maps[1]
            ms = jnp.mean(o * o, axis=1, keepdims=True)
            o = o * lax.rsqrt(ms + EPS) * sub
            ha_scr[rows, cols] = (o * g_scr[rows, cols]).astype(BF16)

        return qk, softmax, pv

    _attend_blocks(block_stages, n_blocks, n_heads, unrolled=not latent)

    if latent:
        weights.wait()
    else:
        weights.load_out_proj()
    _out_proj_norm(x_ref, mod_ref, mod_row, ha_scr, w_out_ref, lng_ref, lnb_ref, layer, y_ref, n_rows, alpha)

    if not latent:
        for blk in range(n_blocks):
            for copy in kv_out_copies(blk):
                copy.wait()
        weights.finish()


MOD_SLAB_ROWS = 128
MOD_COL_BLOCK = 1024
MOD_SLOTS = 4


def _mod_kernel(n_cond, cv_ref, w_hbm, b_ref, o_ref, sb_scr, ring, acc_scr, sems):
    depth, n_in, n_out = w_hbm.shape
    sublanes = 8
    slabs_per_layer = n_in // MOD_SLAB_ROWS
    slabs = [(l, rs) for l in range(depth) for rs in range(slabs_per_layer)]

    def slab_copy(n):
        l, rs = slabs[n]
        return pltpu.make_async_copy(w_hbm.at[l, pl.ds(rs * MOD_SLAB_ROWS, MOD_SLAB_ROWS), :],
                                     ring.at[n % MOD_SLOTS], sems.at[n % MOD_SLOTS])

    for n in range(min(MOD_SLOTS, len(slabs))):
        slab_copy(n).start()
    s_t = _silu(cv_ref[...]).T
    for r in range(n_cond):
        sb_scr[r] = jnp.broadcast_to(s_t[:, r:r + 1], (n_in, LANES))

    for n, (l, rs) in enumerate(slabs):
        slab_copy(n).wait()
        for cb in range(n_out // MOD_COL_BLOCK):
            cols = pl.ds(cb * MOD_COL_BLOCK, MOD_COL_BLOCK)
            if rs == 0:
                accs = (jnp.zeros((sublanes, MOD_COL_BLOCK), F32),) * n_cond
            else:
                accs = tuple(acc_scr[r, :, cols] for r in range(n_cond))

            def body(kb, accs, n=n, rs=rs, cols=cols):
                w = ring[n % MOD_SLOTS, pl.ds(pl.multiple_of(kb * sublanes, sublanes), sublanes), cols]
                s_rows = pl.ds(pl.multiple_of(rs * MOD_SLAB_ROWS + kb * sublanes, sublanes), sublanes)
                return tuple(acc + w * jnp.tile(sb_scr[r, s_rows, :], (1, MOD_COL_BLOCK // LANES))
                             for r, acc in enumerate(accs))

            accs = lax.fori_loop(0, MOD_SLAB_ROWS // sublanes, body, accs, unroll=8)
            if rs < slabs_per_layer - 1:
                for r in range(n_cond):
                    acc_scr[r, :, cols] = accs[r]
            else:
                rows = [jnp.sum(acc, axis=0, keepdims=True) + b_ref[l:l + 1, cols] for acc in accs]
                o_ref[l, :, cols] = jnp.concatenate(rows + [jnp.zeros((8 - n_cond, MOD_COL_BLOCK), F32)], axis=0)
        if n + MOD_SLOTS < len(slabs):
            slab_copy(n + MOD_SLOTS).start()


def _full(shape, **kw):
    zeros = (0,) * len(shape)
    return pl.BlockSpec(shape, lambda i: zeros, **kw)


def _weight_specs(latent, w_in, w_out):
    if latent:
        return [_full(w_in.shape, pipeline_mode=pl.Buffered(1)), pl.BlockSpec(memory_space=pl.ANY)]
    return [pl.BlockSpec(memory_space=pl.ANY), pl.BlockSpec(memory_space=pl.ANY)]


def _weight_scratch(w_in, w_out):
    assert w_in.shape[0] % W_SLAB_ROWS == 0 and w_out.shape[0] % W_SLAB_ROWS == 0
    n_slabs = max(w_in.shape[0], w_out.shape[0]) // W_SLAB_ROWS
    return [pltpu.VMEM(w_in.shape, BF16), pltpu.VMEM(w_out.shape, BF16),
            pltpu.VMEM((n_slabs, W_SLAB_ROWS, max(w_in.shape[1], w_out.shape[1])), F32),
            pltpu.SemaphoreType.DMA((n_slabs,)), pltpu.SemaphoreType.DMA((2,))]


def _rope_tables(seq):
    t = np.arange(seq)
    n_freq = HEAD_DIM // 4
    freqs = ROPE_THETA ** (-np.arange(n_freq, dtype=np.float64) / n_freq)
    ang_row = (t // GRID_W)[:, None] * freqs
    ang_col = (t % GRID_W)[:, None] * freqs
    ang = np.concatenate([ang_row, ang_row, ang_col, ang_col], axis=1)
    sign = np.concatenate([-np.ones(n_freq), np.ones(n_freq)] * 2)[None, :]
    cos = np.tile(np.cos(ang), (1, 2)).astype(np.float32)
    sin = np.tile(np.sin(ang) * sign, (1, 2)).astype(np.float32)
    chunked_t = lambda a: a.reshape(seq // ROW_CHUNK, ROW_CHUNK, LANES).transpose(0, 2, 1)
    return jnp.asarray(cos), jnp.asarray(sin), jnp.asarray(chunked_t(cos)), jnp.asarray(chunked_t(sin))


def _modulation(c, c_ctx, w_mod, b_mod):
    depth = w_mod.shape[0]
    n_cond = 1 + c.shape[0]
    cv = jnp.concatenate([c_ctx[None, :], c, jnp.zeros((8 - n_cond, D_MODEL), F32)], axis=0)
    assert D_MODEL % MOD_SLAB_ROWS == 0 and (3 * D_MODEL) % MOD_COL_BLOCK == 0
    return pl.pallas_call(
        functools.partial(_mod_kernel, n_cond),
        grid=(1,),
        in_specs=[_full(cv.shape), pl.BlockSpec(memory_space=pl.ANY), _full(b_mod.shape)],
        out_specs=_full((depth, 8, 3 * D_MODEL)),
        out_shape=jax.ShapeDtypeStruct((depth, 8, 3 * D_MODEL), F32),
        scratch_shapes=[pltpu.VMEM((n_cond, D_MODEL, LANES), F32),
                        pltpu.VMEM((MOD_SLOTS, MOD_SLAB_ROWS, 3 * D_MODEL), F32),
                        pltpu.VMEM((n_cond, 8, 3 * D_MODEL), F32),
                        pltpu.SemaphoreType.DMA((MOD_SLOTS,))],
        compiler_params=pltpu.CompilerParams(dimension_semantics=("arbitrary",)),
        name="adaln_modulation",
    )(cv, w_mod, b_mod)


def _even_layer(x, mod, layer, w_in, w_out, q_norm, k_norm, sink, ln_g, ln_b, latent, seq, n_rows, alpha, extras=()):
    total = x.shape[0]
    grid = (total // n_rows,)
    single = pl.Buffered(1)
    norms = jnp.concatenate([jnp.broadcast_to(jnp.tile(k_norm, 2)[:, None], (LANES, ROW_CHUNK)),
                             jnp.broadcast_to(jnp.tile(q_norm, 2 * ROW_CHUNK // LANES)[None, :], (8, ROW_CHUNK))], axis=0)

    row_blk = lambda width: pl.BlockSpec((n_rows, width), lambda i: (i, 0))
    in_specs = [row_blk(D_MODEL),
                pl.BlockSpec((1, 8, 3 * D_MODEL), lambda i: (layer, 0, 0)),
                *_weight_specs(latent, w_in, w_out),
                _full(norms.shape),
                pl.BlockSpec(memory_space=pltpu.SMEM),
                _full(ln_g.shape), _full(ln_b.shape)]
    args = [x, mod, w_in, w_out, norms, sink, ln_g, ln_b]
    y_shape = jax.ShapeDtypeStruct((total, D_MODEL), F32)
    n_blocks = n_rows // ROW_CHUNK
    if latent:
        cos, sin, cos_t, sin_t, cakt, cav, cbkt, cbv = extras
        n_past = cav.shape[2]
        in_specs += [_full(cos.shape, pipeline_mode=single), _full(sin.shape, pipeline_mode=single),
                     _full(cos_t.shape, pipeline_mode=single), _full(sin_t.shape, pipeline_mode=single)]
        in_specs += [pl.BlockSpec((1, LANES, n_past), lambda i: (i, 0, 0))] * 4
        args += [cos, sin, cos_t, sin_t, cakt, cav, cbkt, cbv]
        out_specs = row_blk(D_MODEL)
        out_shape = y_shape
        n_keys = seq + n_past
    else:
        kv_blk = pl.BlockSpec((n_blocks, LANES, ROW_CHUNK), lambda i: (i, 0, 0))
        hbm = pl.BlockSpec(memory_space=pl.ANY)
        out_specs = [row_blk(D_MODEL)] + [kv_blk] * 4 + [hbm, hbm]
        out_shape = ([y_shape] + [jax.ShapeDtypeStruct((total // seq, LANES, seq), F32)] * 4
                     + [jax.ShapeDtypeStruct(w_in.shape, BF16), jax.ShapeDtypeStruct(w_out.shape, BF16)])
        n_keys = n_rows
    n_kchunks = n_keys // ROW_CHUNK
    n_cols = n_keys if latent else ROW_CHUNK
    scratch = [pltpu.VMEM((n_rows, D_MODEL), BF16),
               pltpu.VMEM((n_rows, 512), BF16), pltpu.VMEM((n_rows, 512), BF16),
               pltpu.VMEM((4, n_kchunks, LANES, ROW_CHUNK), BF16), pltpu.VMEM((4, n_keys, LANES), BF16),
               pltpu.VMEM((4, n_keys // WINDOW, LANES, WINDOW), BF16), pltpu.VMEM((4, n_keys, LANES), BF16),
               pltpu.VMEM((n_rows, D_MODEL), F32),
               pltpu.VMEM((2, 2, ROW_CHUNK, n_cols), F32),
               pltpu.VMEM((2, 2, ROW_CHUNK, n_cols), BF16),
               pltpu.VMEM((2, ROW_CHUNK, LANES), F32),
               pltpu.VMEM((2 * LANES, D_MODEL), BF16),
               pltpu.VMEM((D_MODEL, 2 * LANES), BF16)]
    if latent:
        scratch.append(pltpu.VMEM((1 + 2 * WINDOW // ROW_CHUNK, ROW_CHUNK, ROW_CHUNK), F32))
        scratch += [pltpu.VMEM(w_out.shape, BF16), pltpu.SemaphoreType.DMA((1,))]
    else:
        scratch += _weight_scratch(w_in, w_out)
    return pl.pallas_call(
        functools.partial(_even_kernel, latent, layer, n_rows, seq, alpha),
        grid=grid, in_specs=in_specs, out_specs=out_specs, out_shape=out_shape,
        scratch_shapes=scratch,
        compiler_params=pltpu.CompilerParams(dimension_semantics=("arbitrary",), vmem_limit_bytes=VMEM_LIMIT),
        name="even_layer_latent" if latent else "even_layer_context",
    )(*args)


def _odd_layer(x, mod, layer, w_in, w_out, lams, sub, ln_g, ln_b, latent, seq, n_rows, alpha, lam_init, extras=()):
    total = x.shape[0]
    grid = (total // n_rows,)
    row_blk = lambda width: pl.BlockSpec((n_rows, width), lambda i: (i, 0))
    single = pl.Buffered(1)
    in_specs = [row_blk(D_MODEL),
                pl.BlockSpec((1, 8, 3 * D_MODEL), lambda i: (layer, 0, 0)),
                *_weight_specs(latent, w_in, w_out),
                _full((1, HEAD_DIM)), _full((1, HEAD_DIM)), _full((1, HEAD_DIM)), _full((1, HEAD_DIM)),
                _full((1, LANES)),
                _full(ln_g.shape), _full(ln_b.shape)]
    args = [x, mod, w_in, w_out, *lams, sub, ln_g, ln_b]
    y_shape = jax.ShapeDtypeStruct((total, D_MODEL), F32)
    n_heads = D_MODEL // LANES
    n_blocks = n_rows // ROW_CHUNK
    if latent:
        cos, sin, cck, ccv = extras
        n_past = cck.shape[2]
        in_specs += [_full(cos.shape, pipeline_mode=single), _full(sin.shape, pipeline_mode=single)]
        in_specs += [pl.BlockSpec(memory_space=pl.ANY)] * 2
        args += [cos, sin, cck, ccv]
        out_specs = row_blk(D_MODEL)
        out_shape = y_shape
        n_keys = seq + n_past
    else:
        hbm = pl.BlockSpec(memory_space=pl.ANY)
        out_specs = [row_blk(D_MODEL), hbm, hbm, hbm, hbm]
        out_shape = ([y_shape] + [jax.ShapeDtypeStruct((total // seq, 1, seq, n_heads, LANES), F32)] * 2
                     + [jax.ShapeDtypeStruct(w_in.shape, BF16), jax.ShapeDtypeStruct(w_out.shape, BF16)])
        n_keys = n_rows
    n_cols = n_keys if latent else ROW_CHUNK
    scratch = [pltpu.VMEM((n_rows, D_MODEL), BF16),
               pltpu.VMEM((n_rows, D_MODEL), BF16),
               pltpu.VMEM((2, n_keys, D_MODEL), BF16),
               pltpu.VMEM((n_keys, D_MODEL), BF16),
               pltpu.VMEM((n_rows, D_MODEL), F32),
               pltpu.VMEM((2, 2, ROW_CHUNK, n_cols), F32),
               pltpu.VMEM((2, 2, ROW_CHUNK, n_cols), BF16)]
    if latent:
        scratch += [pltpu.VMEM((2, n_past, D_MODEL), F32), pltpu.SemaphoreType.DMA((2,))]
        scratch += [pltpu.VMEM(w_out.shape, BF16), pltpu.SemaphoreType.DMA((1,))]
    else:
        scratch += [pltpu.VMEM((n_blocks, 2, ROW_CHUNK, D_MODEL), F32),
                    pltpu.SemaphoreType.DMA((n_blocks, 2))]
        scratch += _weight_scratch(w_in, w_out)
    return pl.pallas_call(
        functools.partial(_odd_kernel, latent, layer, n_rows, seq, alpha, lam_init),
        grid=grid, in_specs=in_specs, out_specs=out_specs, out_shape=out_shape,
        scratch_shapes=scratch,
        compiler_params=pltpu.CompilerParams(dimension_semantics=("arbitrary",), vmem_limit_bytes=VMEM_LIMIT),
        name="odd_layer_latent" if latent else "odd_layer_context",
    )(*args)


def kernel(x_prompt, x_sample, cache_a_k, cache_a_v, cache_b_k, cache_b_v, cache_c_k, cache_c_v, c, c_ctx,
           w_mod, b_mod, ln_g, ln_b, w_in_even, w_out_even, q_norm_a, k_norm_a, sink_b, w_in_odd, w_out_odd,
           lambda_q1, lambda_k1, lambda_q2, lambda_k2, subln_c):
    depth = w_mod.shape[0]
    batch, seq, _ = x_prompt.shape
    dec_batch, dec_seq, _ = x_sample.shape
    n_past = cache_a_k.shape[2]
    alpha = (2 * depth) ** 0.25
    assert seq == ROW_CHUNK and n_past % ROW_CHUNK == 0 and dec_seq % ROW_CHUNK == 0

    mod = _modulation(c, c_ctx, w_mod, b_mod)
    cos, sin, cos_t, sin_t = _rope_tables(dec_seq)

    bf16_weights = {}

    def run(x, latent, n_batch, s, rows_even, rows_odd):
        kv = {"a_k": [], "a_v": [], "b_k": [], "b_v": [], "c_k": [], "c_v": []}
        for l in range(depth):
            if l % 2 == 0:
                e = l // 2
                extras = ()
                if latent:
                    k_t = lambda t: t[:, e].transpose(0, 2, 3, 1).reshape(n_batch, LANES, n_past)
                    extras = (cos, sin, cos_t, sin_t,
                              k_t(cache_a_k), k_t(cache_a_v), k_t(cache_b_k), k_t(cache_b_v))
                w_in, w_out = bf16_weights[l] if latent else (w_in_even[e], w_out_even[e])
                res = _even_layer(x, mod, l, w_in, w_out, q_norm_a[e], k_norm_a[e],
                                  sink_b[e], ln_g, ln_b, latent, s, rows_even, alpha, extras)
                if latent:
                    x = res
                else:
                    x = res[0]
                    bf16_weights[l] = res[5:7]
                    for name, t in zip(("a_k", "a_v", "b_k", "b_v"), res[1:5]):
                        kv[name].append(t.reshape(n_batch, 2, HEAD_DIM, s).transpose(0, 3, 1, 2))
            else:
                o = l // 2
                lam_init = 0.8 - 0.6 * math.exp(-0.3 * l)
                extras = (cos, sin, cache_c_k, cache_c_v) if latent else ()
                lams = [t[o][None, :] for t in (lambda_q1, lambda_k1, lambda_q2, lambda_k2)]
                w_in, w_out = bf16_weights[l] if latent else (w_in_odd[o], w_out_odd[o])
                res = _odd_layer(x, mod, l, w_in, w_out, lams,
                                 subln_c[o][None, :], ln_g, ln_b, latent, s, rows_odd, alpha, lam_init, extras)
                if latent:
                    x = res
                else:
                    x = res[0]
                    bf16_weights[l] = res[3:5]
                    kv["c_k"].append(res[1][:, 0])
                    kv["c_v"].append(res[2][:, 0])
        return x, kv

    y_ctx, kv = run(x_prompt.reshape(batch * seq, D_MODEL), False, batch, seq, 1024, 512)
    y_lat, _ = run(x_sample.reshape(dec_batch * dec_seq, D_MODEL), True, dec_batch, dec_seq, dec_seq, dec_seq)

    stack = lambda name: jnp.stack(kv[name], axis=1)
    return (y_ctx.reshape(batch, seq, D_MODEL), y_lat.reshape(dec_batch, dec_seq, D_MODEL),
            stack("a_k"), stack("a_v"), stack("b_k"), stack("b_v"), stack("c_k"), stack("c_v"))
```

```python
import functools
import math

import jax
import jax.numpy as jnp
import numpy as np
from jax import lax
from jax.experimental import pallas as pl
from jax.experimental.pallas import tpu as pltpu

F32 = jnp.float32
BF16 = jnp.bfloat16

D_MODEL = 1024
HEAD_DIM = 64
GRID_W = 64
WINDOW = 128
ROPE_THETA = 10000.0
EPS = 1e-6
NEG_INF = -1e30
LOG2E = 1.4426950408889634
Q_SCALE = HEAD_DIM ** -0.5 * LOG2E
LANES = 128
ROW_CHUNK = 256
SOFTMAX_VREGS = 40
VMEM_LIMIT = 60000 * 1024
W_SLAB_ROWS = 128


def _silu(x):
    return x / (1.0 + jnp.exp(-x))


def _dot(a, b):
    return jnp.dot(a, b, preferred_element_type=F32)


def _dot_nt(a, b):
    return lax.dot_general(a, b, (((1,), (1,)), ((), ())), preferred_element_type=F32)


def _lane_iota(rows):
    return lax.broadcasted_iota(jnp.int32, (rows, LANES), 1)


def _chunk_rows(i):
    if isinstance(i, int):
        return pl.ds(i * ROW_CHUNK, ROW_CHUNK)
    return pl.ds(pl.multiple_of(i * ROW_CHUNK, ROW_CHUNK), ROW_CHUNK)


def _softmax_rows(n_cols):
    rows = 8
    while rows * 2 * n_cols <= SOFTMAX_VREGS * 1024 and rows * 2 <= ROW_CHUNK:
        rows *= 2
    return rows


def _rope(a, cos, sin_signed):
    lane = _lane_iota(a.shape[0])
    fwd = pltpu.roll(a, LANES - 16, 1)
    bwd = pltpu.roll(a, 16, 1)
    partner = jnp.where((lane & 16) == 0, fwd, bwd)
    return a * cos + partner * sin_signed


def _rope_t(a, cos_t, sin_t):
    blocks = [a[16 * b:16 * (b + 1), :] for b in range(a.shape[0] // 16)]
    partner = jnp.concatenate([blocks[b ^ 1] for b in range(len(blocks))], axis=0)
    return a * cos_t + partner * sin_t


def _store_kt_variants(scr, chunk, kt):
    width = scr.shape[-1]
    per_block = kt.shape[1] // width
    zero = jnp.zeros((HEAD_DIM, kt.shape[1]), F32)
    for j in range(2):
        kj = kt[HEAD_DIM * j:HEAD_DIM * (j + 1), :]
        for par, full in enumerate((jnp.concatenate([kj, zero], axis=0), jnp.concatenate([zero, kj], axis=0))):
            full = full.astype(BF16)
            for c in range(per_block):
                scr[2 * j + par, chunk * per_block + c] = full[:, width * c:width * (c + 1)]


def _store_v_variants(scr, rows, a):
    lane = _lane_iota(a.shape[0])
    lo = lane < HEAD_DIM
    swapped = pltpu.roll(a, HEAD_DIM, 1)
    one = jnp.ones_like(a)
    scr[0, rows, :] = jnp.where(lo, a, one).astype(BF16)
    scr[1, rows, :] = jnp.where(lo, one, swapped).astype(BF16)
    scr[2, rows, :] = jnp.where(lo, swapped, one).astype(BF16)
    scr[3, rows, :] = jnp.where(lo, one, a).astype(BF16)


def _layer_norm_rows(z, g, b):
    mu = jnp.mean(z, axis=-1, keepdims=True)
    zc = z - mu
    var = jnp.mean(zc * zc, axis=-1, keepdims=True)
    return zc * lax.rsqrt(var + EPS) * g + b


def _modulate(x_ref, mod_ref, mod_row, h_scr, n_rows):
    shift = mod_ref[0, pl.ds(mod_row, 1), 0:D_MODEL]
    scale = mod_ref[0, pl.ds(mod_row, 1), D_MODEL:2 * D_MODEL]

    def body(i, carry):
        rows = _chunk_rows(i)
        h_scr[rows, :] = (x_ref[rows, :] * (1.0 + scale) + shift).astype(BF16)
        return carry

    lax.fori_loop(0, n_rows // ROW_CHUNK, body, 0)


def _out_proj_norm(x_ref, mod_ref, mod_row, attn_scr, w_out_ref, lng_ref, lnb_ref, layer, y_ref, n_rows, alpha):
    gate = mod_ref[0, pl.ds(mod_row, 1), 2 * D_MODEL:3 * D_MODEL]
    g = lng_ref[layer:layer + 1, :]
    b = lnb_ref[layer:layer + 1, :]

    def body(i, carry):
        rows = _chunk_rows(i)
        out = _dot(attn_scr[rows, :], w_out_ref[...])
        z = alpha * x_ref[rows, :] + gate * out
        y_ref[rows, :] = _layer_norm_rows(z, g, b)
        return carry

    lax.fori_loop(0, n_rows // ROW_CHUNK, body, 0, unroll=True)


class _ContextWeights:
    def __init__(self, step, w_in, w_out, stage, sems, out_sems):
        self.step, self.w_in, self.w_out, self.stage, self.sems = step, w_in, w_out, stage, sems
        self.out_copies = [pltpu.make_async_copy(w[1], w[2], out_sems.at[n]) for n, w in enumerate((w_in, w_out))]

    def _slab_copies(self, w_hbm):
        n_cols = w_hbm.shape[1]
        return [pltpu.make_async_copy(w_hbm.at[pl.ds(s * W_SLAB_ROWS, W_SLAB_ROWS), :],
                                      self.stage.at[s, :, pl.ds(0, n_cols)], self.sems.at[s])
                for s in range(w_hbm.shape[0] // W_SLAB_ROWS)]

    def _cast(self, w_hbm, w_scr):
        n_cols = w_hbm.shape[1]
        for s, copy in enumerate(self._slab_copies(w_hbm)):
            copy.wait()
            w_scr[pl.ds(s * W_SLAB_ROWS, W_SLAB_ROWS), :] = self.stage[s, :, 0:n_cols].astype(BF16)

    def load_in_proj(self):
        @pl.when(self.step == 0)
        def _():
            for copy in self._slab_copies(self.w_in[0]):
                copy.start()
            self._cast(self.w_in[0], self.w_in[1])
            self.out_copies[0].start()
            for copy in self._slab_copies(self.w_out[0]):
                copy.start()

    def load_out_proj(self):
        @pl.when(self.step == 0)
        def _():
            self._cast(self.w_out[0], self.w_out[1])
            self.out_copies[1].start()

    def finish(self):
        @pl.when(self.step == 0)
        def _():
            for copy in self.out_copies:
                copy.wait()


def _run_pipeline(n_items, stages):
    for u in range(n_items + len(stages) - 1):
        for k, stage in enumerate(stages):
            t = u - k
            if 0 <= t < n_items:
                stage(t, t % 2)


def _attend_blocks(block_stages, n_blocks, n_items, unrolled):
    assert n_items % 2 == 0
    if unrolled:
        per_block = [block_stages(i) for i in range(n_blocks)]
        stages = [lambda g, slot, k=k: per_block[g // n_items][k](g % n_items, slot) for k in range(3)]
        _run_pipeline(n_blocks * n_items, stages)
    else:
        def body(i, carry):
            _run_pipeline(n_items, block_stages(i))
            return carry

        lax.fori_loop(0, n_blocks, body, 0)


def _even_kernel(latent, layer, n_rows, seq, alpha, *refs):
    if latent:
        (x_ref, mod_ref, w_in_ref, w_out_ref, qnorm_ref, knorm_ref, sink_ref, lng_ref, lnb_ref,
         cos_ref, sin_ref, cost_ref, sint_ref, cakt_ref, cav_ref, cbkt_ref, cbv_ref,
         y_ref,
         ha_scr, qa_scr, qb_scr, ka_scr, va_scr, kb_scr, vb_scr, g_scr, s_scr, p_scr, es_scr, wkt_scr, wv_scr,
         bias_scr) = refs
    else:
        (x_ref, mod_ref, w_in_hbm, w_out_hbm, qnorm_ref, knorm_ref, sink_ref, lng_ref, lnb_ref,
         y_ref, nakt_ref, navt_ref, nbkt_ref, nbvt_ref, w_in_bf_hbm, w_out_bf_hbm,
         ha_scr, qa_scr, qb_scr, ka_scr, va_scr, kb_scr, vb_scr, g_scr, s_scr, p_scr, es_scr, wkt_scr,
         wv_scr, w_in_ref, w_out_ref, w_stage, w_sems, w_out_sems) = refs

    step = pl.program_id(0)
    if not latent:
        weights = _ContextWeights(step, (w_in_hbm, w_in_ref, w_in_bf_hbm), (w_out_hbm, w_out_ref, w_out_bf_hbm),
                                  w_stage, w_sems, w_out_sems)
        weights.load_in_proj()
    mod_row = step + 1 if latent else 0
    _modulate(x_ref, mod_ref, mod_row, ha_scr, n_rows)

    col_ka, col_va, col_kb, col_vb = 512, 640, 1792, 1920

    @pl.when(step == 0)
    def _():
        for r, c0 in enumerate((col_ka, col_kb)):
            wkt_scr[LANES * r:LANES * (r + 1), :] = w_in_ref[:, c0:c0 + LANES].T
        wv_scr[:, 0:LANES] = w_in_ref[:, col_va:col_va + LANES]
        wv_scr[:, LANES:2 * LANES] = w_in_ref[:, col_vb:col_vb + LANES]

    n_lat_chunks = seq // ROW_CHUNK
    if latent:
        n_past = cav_ref.shape[2]
        past_rows = pl.ds(seq, n_past)
        _store_kt_variants(ka_scr, n_lat_chunks, cakt_ref[0])
        _store_kt_variants(kb_scr, n_lat_chunks, cbkt_ref[0])
        _store_v_variants(va_scr, past_rows, cav_ref[0].T)
        _store_v_variants(vb_scr, past_rows, cbv_ref[0].T)

    e = layer // 2
    two_heads = lambda w: jnp.concatenate([w, w], axis=1)
    qn = two_heads(qnorm_ref[e:e + 1, :])
    kn_col = jnp.broadcast_to(two_heads(knorm_ref[e:e + 1, :]), (8, LANES)).T[:, 0:1]
    knt = jnp.broadcast_to(kn_col, (LANES, ROW_CHUNK))

    def proj(i, carry):
        rows = _chunk_rows(i)
        hh = ha_scr[rows, :]
        if latent:
            cos = cos_ref[rows, :]
            sin = sin_ref[rows, :]
            rot = lambda a: _rope(a, cos, sin)
            rot_t = lambda a: _rope_t(a, cost_ref[i], sint_ref[i])
        else:
            rot = rot_t = lambda a: a

        acc = _dot(hh, w_in_ref[:, 0:512])
        lo_lanes = _lane_iota(ROW_CHUNK) < HEAD_DIM
        for j in range(4):
            a = acc[:, LANES * j:LANES * (j + 1)]
            sq = a * a
            first = jnp.sum(jnp.where(lo_lanes, sq, 0.0), axis=1, keepdims=True)
            second = jnp.sum(jnp.where(lo_lanes, 0.0, sq), axis=1, keepdims=True)
            ms = jnp.where(lo_lanes, first, second) * (1.0 / HEAD_DIM)
            a = rot(a * lax.rsqrt(ms + EPS) * qn)
            qa_scr[rows, LANES * j:LANES * (j + 1)] = (a * Q_SCALE).astype(BF16)
        acc = _dot(hh, w_in_ref[:, 1280:1792])
        for j in range(4):
            a = rot(acc[:, LANES * j:LANES * (j + 1)])
            qb_scr[rows, LANES * j:LANES * (j + 1)] = (a * Q_SCALE).astype(BF16)
        g_scr[rows, 0:512] = _silu(_dot(hh, w_in_ref[:, 768:1280]))
        g_scr[rows, 512:1024] = _silu(_dot(hh, w_in_ref[:, 2048:2560]))
        v = _dot(hh, wv_scr[...])
        _store_v_variants(va_scr, rows, v[:, 0:LANES])
        _store_v_variants(vb_scr, rows, v[:, LANES:2 * LANES])

        kt = _dot_nt(wkt_scr[0:2 * LANES, :], hh)
        heads = [kt[HEAD_DIM * h:HEAD_DIM * (h + 1), :] for h in range(2)]
        kat = jnp.concatenate([blk * lax.rsqrt(jnp.mean(blk * blk, axis=0, keepdims=True) + EPS) for blk in heads],
                              axis=0) * knt
        kbt = kt[LANES:2 * LANES, :]
        if not latent:
            vt = v.T
            nakt_ref[i] = kat
            nbkt_ref[i] = kbt
            navt_ref[i] = vt[0:LANES, :]
            nbvt_ref[i] = vt[LANES:2 * LANES, :]
        _store_kt_variants(ka_scr, i, rot_t(kat))
        _store_kt_variants(kb_scr, i, rot_t(kbt))
        return carry

    lax.fori_loop(0, n_rows // ROW_CHUNK, proj, 0, unroll=2)

    sinks = [sink_ref[h] * LOG2E for h in range(8)]
    ck = ROW_CHUNK
    bk = kb_scr.shape[-1]
    win = ROW_CHUNK + 2 * WINDOW
    n_items = 8

    def block_stages(i):
        rows = _chunk_rows(i)
        if latent:
            a_chunks = list(range(n_lat_chunks + n_past // ck))
            a_keys = pl.ds(0, seq + n_past)
            w0 = jnp.clip(i * (ck // bk) - WINDOW // bk, 0, (seq - win) // bk)
            win_rows = pl.ds(pl.multiple_of(w0 * bk, bk), win)
            dist = (lax.broadcasted_iota(jnp.int32, (ROW_CHUNK, ck), 1)
                    - lax.broadcasted_iota(jnp.int32, (ROW_CHUNK, ck), 0))
            for c in range(win // ck):
                off = w0 * bk + c * ck - i * ck
                bias_scr[c] = jnp.where(jnp.abs(dist + off) <= WINDOW, 0.0, NEG_INF).astype(F32)
            b_first = [w0 + c * (ck // bk) for c in range(win // ck)] + [seq // bk]
            n_biased = win // ck
            b_cols = win + n_past
        else:
            a_chunks = [i]
            a_keys = rows
            b_first = [i * (ck // bk)]
            n_biased = 0
            b_cols = ck
        a_cols = len(a_chunks) * ck

        def qk(t, slot):
            p, branch = divmod(t, 2)
            cols = slice(LANES * p, LANES * (p + 1))
            kvh = p // 2
            q = (qb_scr if branch else qa_scr)[rows, cols]
            for par in (0, 1):
                var = 2 * kvh + par
                if branch:
                    tiles = [jnp.concatenate([kb_scr[var, first + d] for d in range(ck // bk)], axis=1)
                             for first in b_first]
                else:
                    tiles = [ka_scr[var, chunk] for chunk in a_chunks]
                for c, kt in enumerate(tiles):
                    s = _dot(q, kt)
                    if branch and c < n_biased:
                        s = s + bias_scr[c]
                    s_scr[slot, par, :, c * ck:(c + 1) * ck] = s

        def softmax(t, slot):
            p, branch = divmod(t, 2)
            n_cols = b_cols if branch else a_cols
            rb = _softmax_rows(n_cols)
            for par in (0, 1):
                for r in range(ROW_CHUNK // rb):
                    sub = slice(r * rb, (r + 1) * rb)
                    s = s_scr[slot, par, sub, 0:n_cols]
                    m = jnp.max(s, axis=1, keepdims=True)
                    if branch:
                        sink = sinks[2 * p + par]
                        m = jnp.maximum(m, sink)
                        es_scr[slot, sub, HEAD_DIM * par:HEAD_DIM * (par + 1)] = jnp.broadcast_to(
                            jnp.exp2(sink - m), (rb, HEAD_DIM))
                    p_scr[slot, par, sub, 0:n_cols] = jnp.exp2((s - m).astype(BF16))

        def pv(t, slot):
            p, branch = divmod(t, 2)
            kvh = p // 2
            v_scr = vb_scr if branch else va_scr
            accs = []
            for par in (0, 1):
                var = 2 * kvh + par
                if latent and branch:
                    n_loc = win
                    accs.append(_dot(p_scr[slot, par, :, 0:n_loc], v_scr[var, win_rows, :])
                                + _dot(p_scr[slot, par, :, n_loc:b_cols], v_scr[var, past_rows, :]))
                else:
                    accs.append(_dot(p_scr[slot, par, :, 0:a_cols], v_scr[var, a_keys, :]))
            lo = _lane_iota(ROW_CHUNK) < HEAD_DIM
            denom = pltpu.roll(jnp.where(lo, accs[1], accs[0]), HEAD_DIM, 1)
            if branch:
                denom = denom + es_scr[slot]
            o = jnp.where(lo, accs[0], accs[1]) / denom
            ocols = slice(512 * branch + LANES * p, 512 * branch + LANES * (p + 1))
            ha_scr[rows, ocols] = (o * g_scr[rows, ocols]).astype(BF16)

        return qk, softmax, pv

    _attend_blocks(block_stages, n_rows // ROW_CHUNK, n_items, unrolled=not latent)

    if not latent:
        weights.load_out_proj()
    _out_proj_norm(x_ref, mod_ref, mod_row, ha_scr, w_out_ref, lng_ref, lnb_ref, layer, y_ref, n_rows, alpha)
    if not latent:
        weights.finish()


def _odd_kernel(latent, layer, n_rows, seq, alpha, lam_init, *refs):
    if latent:
        (x_ref, mod_ref, w_in_ref, w_out_ref, lq1_ref, lk1_ref, lq2_ref, lk2_ref, sub_ref, lng_ref, lnb_ref,
         cos_ref, sin_ref, cck_hbm, ccv_hbm,
         y_ref,
         ha_scr, q_scr, k_scr, v_scr, g_scr, s_scr, p_scr, past_stage, past_sems) = refs
    else:
        (x_ref, mod_ref, w_in_hbm, w_out_hbm, lq1_ref, lk1_ref, lq2_ref, lk2_ref, sub_ref, lng_ref, lnb_ref,
         y_ref, nck_hbm, ncv_hbm, w_in_bf_hbm, w_out_bf_hbm,
         ha_scr, q_scr, k_scr, v_scr, g_scr, s_scr, p_scr, kv_stage, kv_sems,
         w_in_ref, w_out_ref, w_stage, w_sems, w_out_sems) = refs

    step = pl.program_id(0)
    if not latent:
        weights = _ContextWeights(step, (w_in_hbm, w_in_ref, w_in_bf_hbm), (w_out_hbm, w_out_ref, w_out_bf_hbm),
                                  w_stage, w_sems, w_out_sems)
        weights.load_in_proj()
    mod_row = step + 1 if latent else 0
    _modulate(x_ref, mod_ref, mod_row, ha_scr, n_rows)

    n_heads = D_MODEL // LANES
    n_blocks = n_rows // ROW_CHUNK
    lo = _lane_iota(ROW_CHUNK) < HEAD_DIM

    def kv_out_copies(blk):
        elem = step * n_blocks + blk
        return [pltpu.make_async_copy(kv_stage.at[blk, t, :, pl.ds(LANES * h, LANES)],
                                      out.at[elem, 0, :, h, :], kv_sems.at[blk, t])
                for t, out in enumerate((nck_hbm, ncv_hbm)) for h in range(n_heads)]

    def store_k(rows, h, a):
        cols = slice(LANES * h, LANES * (h + 1))
        zero = jnp.zeros_like(a)
        k_scr[0, rows, cols] = jnp.where(lo, a, zero).astype(BF16)
        k_scr[1, rows, cols] = jnp.where(lo, zero, a).astype(BF16)

    if latent:
        n_past = cck_hbm.shape[2]
        past = pl.ds(seq, n_past)
        past_copies = [pltpu.make_async_copy(cache.at[step, layer // 2, :, h, :],
                                             past_stage.at[t, :, pl.ds(LANES * h, LANES)], past_sems.at[t])
                       for t, cache in enumerate((cck_hbm, ccv_hbm)) for h in range(n_heads)]
        for copy in past_copies:
            copy.start()

    def proj(i, carry):
        rows = _chunk_rows(i)
        hh = ha_scr[rows, :]
        if latent:
            cos = cos_ref[rows, :]
            sin = sin_ref[rows, :]
            rot = lambda a: _rope(a, cos, sin)
        else:
            rot = lambda a: a
        for half in range(2):
            acc = _dot(hh, w_in_ref[:, 512 * half:512 * (half + 1)])
            for j in range(4):
                a = rot(acc[:, LANES * j:LANES * (j + 1)])
                cols = slice(512 * half + LANES * j, 512 * half + LANES * (j + 1))
                q_scr[rows, cols] = (a * Q_SCALE).astype(BF16)
        for half in range(2):
            acc = _dot(hh, w_in_ref[:, 1024 + 512 * half:1024 + 512 * (half + 1)])
            if not latent:
                kv_stage[i, 0, :, 512 * half:512 * (half + 1)] = acc
            for j in range(4):
                store_k(rows, 4 * half + j, rot(acc[:, LANES * j:LANES * (j + 1)]))
        for half in range(2):
            acc = _dot(hh, w_in_ref[:, 2048 + 512 * half:2048 + 512 * (half + 1)])
            if not latent:
                kv_stage[i, 1, :, 512 * half:512 * (half + 1)] = acc
            v_scr[rows, 512 * half:512 * (half + 1)] = acc.astype(BF16)
        if not latent:
            for copy in kv_out_copies(i):
                copy.start()
        for half in range(2):
            acc = _dot(hh, w_in_ref[:, 3072 + 512 * half:3072 + 512 * (half + 1)])
            g_scr[rows, 512 * half:512 * (half + 1)] = _silu(acc)
        return carry

    if latent:
        lax.fori_loop(0, n_blocks, proj, 0, unroll=2)
    else:
        for blk in range(n_blocks):
            proj(blk, 0)

    if latent:
        for copy in past_copies:
            copy.wait()
        for h in range(n_heads):
            store_k(past, h, past_stage[0, :, LANES * h:LANES * (h + 1)])
        v_scr[past, :] = past_stage[1].astype(BF16)

    lam = (jnp.exp(jnp.sum(lq1_ref[...] * lk1_ref[...], axis=1, keepdims=True))
           - jnp.exp(jnp.sum(lq2_ref[...] * lk2_ref[...], axis=1, keepdims=True)) + lam_init)
    sub = sub_ref[...] * (1.0 - lam_init)
    n_keys = seq + n_past if latent else ROW_CHUNK
    rb = _softmax_rows(n_keys)
    ones = jnp.ones((n_keys, LANES), BF16)

    def block_stages(i):
        rows = _chunk_rows(i)
        keys = pl.ds(0, n_keys) if latent else rows

        def qk(h, slot):
            cols = slice(LANES * h, LANES * (h + 1))
            q = q_scr[rows, cols]
            for m in (0, 1):
                s_scr[slot, m] = _dot_nt(q, k_scr[m, keys, cols])

        def softmax(h, slot):
            for m in (0, 1):
                for r in range(ROW_CHUNK // rb):
                    sub_rows = slice(r * rb, (r + 1) * rb)
                    s = s_scr[slot, m, sub_rows, :]
                    top = jnp.max(s, axis=1, keepdims=True)
                    p_scr[slot, m, sub_rows, :] = jnp.exp2((s - top).astype(BF16))

        def pv(h, slot):
            cols = slice(LANES * h, LANES * (h + 1))
            v_ext = jnp.concatenate([v_scr[keys, cols], ones], axis=1)
            maps = []
            for m in (0, 1):
                acc = _dot(p_scr[slot, m], v_ext)
                maps.append(acc[:, 0:LANES] / acc[:, LANES:2 * LANES])
            o = maps[0] - lam * maps[1]
            ms = jnp.mean(o * o, axis=1, keepdims=True)
            o = o * lax.rsqrt(ms + EPS) * sub
            ha_scr[rows, cols] = (o * g_scr[rows, cols]).astype(BF16)

        return qk, softmax, pv

    _attend_blocks(block_stages, n_blocks, n_heads, unrolled=not latent)

    if not latent:
        weights.load_out_proj()
    _out_proj_norm(x_ref, mod_ref, mod_row, ha_scr, w_out_ref, lng_ref, lnb_ref, layer, y_ref, n_rows, alpha)

    if not latent:
        for blk in range(n_blocks):
            for copy in kv_out_copies(blk):
                copy.wait()
        weights.finish()


MOD_SLAB_ROWS = 128
MOD_COL_BLOCK = 1024
MOD_SLOTS = 4


def _mod_kernel(n_cond, cv_ref, w_hbm, b_ref, o_ref, sb_scr, ring, acc_scr, sems):
    depth, n_in, n_out = w_hbm.shape
    sublanes = 8
    slabs_per_layer = n_in // MOD_SLAB_ROWS
    slabs = [(l, rs) for l in range(depth) for rs in range(slabs_per_layer)]

    def slab_copy(n):
        l, rs = slabs[n]
        return pltpu.make_async_copy(w_hbm.at[l, pl.ds(rs * MOD_SLAB_ROWS, MOD_SLAB_ROWS), :],
                                     ring.at[n % MOD_SLOTS], sems.at[n % MOD_SLOTS])

    for n in range(min(MOD_SLOTS, len(slabs))):
        slab_copy(n).start()
    s_t = _silu(cv_ref[...]).T
    for r in range(n_cond):
        sb_scr[r] = jnp.broadcast_to(s_t[:, r:r + 1], (n_in, LANES))

    for n, (l, rs) in enumerate(slabs):
        slab_copy(n).wait()
        for cb in range(n_out // MOD_COL_BLOCK):
            cols = pl.ds(cb * MOD_COL_BLOCK, MOD_COL_BLOCK)
            if rs == 0:
                accs = (jnp.zeros((sublanes, MOD_COL_BLOCK), F32),) * n_cond
            else:
                accs = tuple(acc_scr[r, :, cols] for r in range(n_cond))

            def body(kb, accs, n=n, rs=rs, cols=cols):
                w = ring[n % MOD_SLOTS, pl.ds(pl.multiple_of(kb * sublanes, sublanes), sublanes), cols]
                s_rows = pl.ds(pl.multiple_of(rs * MOD_SLAB_ROWS + kb * sublanes, sublanes), sublanes)
                return tuple(acc + w * jnp.tile(sb_scr[r, s_rows, :], (1, MOD_COL_BLOCK // LANES))
                             for r, acc in enumerate(accs))

            accs = lax.fori_loop(0, MOD_SLAB_ROWS // sublanes, body, accs, unroll=8)
            if rs < slabs_per_layer - 1:
                for r in range(n_cond):
                    acc_scr[r, :, cols] = accs[r]
            else:
                rows = [jnp.sum(acc, axis=0, keepdims=True) + b_ref[l:l + 1, cols] for acc in accs]
                o_ref[l, :, cols] = jnp.concatenate(rows + [jnp.zeros((8 - n_cond, MOD_COL_BLOCK), F32)], axis=0)
        if n + MOD_SLOTS < len(slabs):
            slab_copy(n + MOD_SLOTS).start()


def _full(shape, **kw):
    zeros = (0,) * len(shape)
    return pl.BlockSpec(shape, lambda i: zeros, **kw)


def _weight_specs(latent, w_in, w_out):
    if latent:
        single = pl.Buffered(1)
        return [_full(w_in.shape, pipeline_mode=single), _full(w_out.shape, pipeline_mode=single)]
    return [pl.BlockSpec(memory_space=pl.ANY), pl.BlockSpec(memory_space=pl.ANY)]


def _weight_scratch(w_in, w_out):
    assert w_in.shape[0] % W_SLAB_ROWS == 0 and w_out.shape[0] % W_SLAB_ROWS == 0
    n_slabs = max(w_in.shape[0], w_out.shape[0]) // W_SLAB_ROWS
    return [pltpu.VMEM(w_in.shape, BF16), pltpu.VMEM(w_out.shape, BF16),
            pltpu.VMEM((n_slabs, W_SLAB_ROWS, max(w_in.shape[1], w_out.shape[1])), F32),
            pltpu.SemaphoreType.DMA((n_slabs,)), pltpu.SemaphoreType.DMA((2,))]


def _rope_tables(seq):
    t = np.arange(seq)
    n_freq = HEAD_DIM // 4
    freqs = ROPE_THETA ** (-np.arange(n_freq, dtype=np.float64) / n_freq)
    ang_row = (t // GRID_W)[:, None] * freqs
    ang_col = (t % GRID_W)[:, None] * freqs
    ang = np.concatenate([ang_row, ang_row, ang_col, ang_col], axis=1)
    sign = np.concatenate([-np.ones(n_freq), np.ones(n_freq)] * 2)[None, :]
    cos = np.tile(np.cos(ang), (1, 2)).astype(np.float32)
    sin = np.tile(np.sin(ang) * sign, (1, 2)).astype(np.float32)
    chunked_t = lambda a: a.reshape(seq // ROW_CHUNK, ROW_CHUNK, LANES).transpose(0, 2, 1)
    return jnp.asarray(cos), jnp.asarray(sin), jnp.asarray(chunked_t(cos)), jnp.asarray(chunked_t(sin))


def _modulation(c, c_ctx, w_mod, b_mod):
    depth = w_mod.shape[0]
    n_cond = 1 + c.shape[0]
    cv = jnp.concatenate([c_ctx[None, :], c, jnp.zeros((8 - n_cond, D_MODEL), F32)], axis=0)
    assert D_MODEL % MOD_SLAB_ROWS == 0 and (3 * D_MODEL) % MOD_COL_BLOCK == 0
    return pl.pallas_call(
        functools.partial(_mod_kernel, n_cond),
        grid=(1,),
        in_specs=[_full(cv.shape), pl.BlockSpec(memory_space=pl.ANY), _full(b_mod.shape)],
        out_specs=_full((depth, 8, 3 * D_MODEL)),
        out_shape=jax.ShapeDtypeStruct((depth, 8, 3 * D_MODEL), F32),
        scratch_shapes=[pltpu.VMEM((n_cond, D_MODEL, LANES), F32),
                        pltpu.VMEM((MOD_SLOTS, MOD_SLAB_ROWS, 3 * D_MODEL), F32),
                        pltpu.VMEM((n_cond, 8, 3 * D_MODEL), F32),
                        pltpu.SemaphoreType.DMA((MOD_SLOTS,))],
        compiler_params=pltpu.CompilerParams(dimension_semantics=("arbitrary",)),
        name="adaln_modulation",
    )(cv, w_mod, b_mod)


def _even_layer(x, mod, layer, w_in, w_out, q_norm, k_norm, sink, ln_g, ln_b, latent, seq, n_rows, alpha, extras=()):
    total = x.shape[0]
    grid = (total // n_rows,)
    single = pl.Buffered(1)
    row_blk = lambda width: pl.BlockSpec((n_rows, width), lambda i: (i, 0))
    in_specs = [row_blk(D_MODEL),
                pl.BlockSpec((1, 8, 3 * D_MODEL), lambda i: (layer, 0, 0)),
                *_weight_specs(latent, w_in, w_out),
                _full(q_norm.shape), _full(k_norm.shape),
                pl.BlockSpec(memory_space=pltpu.SMEM),
                _full(ln_g.shape), _full(ln_b.shape)]
    args = [x, mod, w_in, w_out, q_norm, k_norm, sink, ln_g, ln_b]
    y_shape = jax.ShapeDtypeStruct((total, D_MODEL), F32)
    n_blocks = n_rows // ROW_CHUNK
    if latent:
        cos, sin, cos_t, sin_t, cakt, cav, cbkt, cbv = extras
        n_past = cav.shape[2]
        in_specs += [_full(cos.shape, pipeline_mode=single), _full(sin.shape, pipeline_mode=single),
                     _full(cos_t.shape, pipeline_mode=single), _full(sin_t.shape, pipeline_mode=single)]
        in_specs += [pl.BlockSpec((1, LANES, n_past), lambda i: (i, 0, 0))] * 4
        args += [cos, sin, cos_t, sin_t, cakt, cav, cbkt, cbv]
        out_specs = row_blk(D_MODEL)
        out_shape = y_shape
        n_keys = seq + n_past
    else:
        kv_blk = pl.BlockSpec((n_blocks, LANES, ROW_CHUNK), lambda i: (i, 0, 0))
        hbm = pl.BlockSpec(memory_space=pl.ANY)
        out_specs = [row_blk(D_MODEL)] + [kv_blk] * 4 + [hbm, hbm]
        out_shape = ([y_shape] + [jax.ShapeDtypeStruct((total // seq, LANES, seq), F32)] * 4
                     + [jax.ShapeDtypeStruct(w_in.shape, BF16), jax.ShapeDtypeStruct(w_out.shape, BF16)])
        n_keys = n_rows
    n_kchunks = n_keys // ROW_CHUNK
    n_cols = n_keys if latent else ROW_CHUNK
    scratch = [pltpu.VMEM((n_rows, D_MODEL), BF16),
               pltpu.VMEM((n_rows, 512), BF16), pltpu.VMEM((n_rows, 512), BF16),
               pltpu.VMEM((4, n_kchunks, LANES, ROW_CHUNK), BF16), pltpu.VMEM((4, n_keys, LANES), BF16),
               pltpu.VMEM((4, n_keys // WINDOW, LANES, WINDOW), BF16), pltpu.VMEM((4, n_keys, LANES), BF16),
               pltpu.VMEM((n_rows, D_MODEL), F32),
               pltpu.VMEM((2, 2, ROW_CHUNK, n_cols), F32),
               pltpu.VMEM((2, 2, ROW_CHUNK, n_cols), BF16),
               pltpu.VMEM((2, ROW_CHUNK, LANES), F32),
               pltpu.VMEM((2 * LANES, D_MODEL), BF16),
               pltpu.VMEM((D_MODEL, 2 * LANES), BF16)]
    if latent:
        scratch.append(pltpu.VMEM((1 + 2 * WINDOW // ROW_CHUNK, ROW_CHUNK, ROW_CHUNK), F32))
    else:
        scratch += _weight_scratch(w_in, w_out)
    return pl.pallas_call(
        functools.partial(_even_kernel, latent, layer, n_rows, seq, alpha),
        grid=grid, in_specs=in_specs, out_specs=out_specs, out_shape=out_shape,
        scratch_shapes=scratch,
        compiler_params=pltpu.CompilerParams(dimension_semantics=("arbitrary",), vmem_limit_bytes=VMEM_LIMIT),
        name="even_layer_latent" if latent else "even_layer_context",
    )(*args)


def _odd_layer(x, mod, layer, w_in, w_out, lams, sub, ln_g, ln_b, latent, seq, n_rows, alpha, lam_init, extras=()):
    total = x.shape[0]
    grid = (total // n_rows,)
    row_blk = lambda width: pl.BlockSpec((n_rows, width), lambda i: (i, 0))
    single = pl.Buffered(1)
    in_specs = [row_blk(D_MODEL),
                pl.BlockSpec((1, 8, 3 * D_MODEL), lambda i: (layer, 0, 0)),
                *_weight_specs(latent, w_in, w_out),
                _full((1, HEAD_DIM)), _full((1, HEAD_DIM)), _full((1, HEAD_DIM)), _full((1, HEAD_DIM)),
                _full((1, LANES)),
                _full(ln_g.shape), _full(ln_b.shape)]
    args = [x, mod, w_in, w_out, *lams, sub, ln_g, ln_b]
    y_shape = jax.ShapeDtypeStruct((total, D_MODEL), F32)
    n_heads = D_MODEL // LANES
    n_blocks = n_rows // ROW_CHUNK
    if latent:
        cos, sin, cck, ccv = extras
        n_past = cck.shape[2]
        in_specs += [_full(cos.shape, pipeline_mode=single), _full(sin.shape, pipeline_mode=single)]
        in_specs += [pl.BlockSpec(memory_space=pl.ANY)] * 2
        args += [cos, sin, cck, ccv]
        out_specs = row_blk(D_MODEL)
        out_shape = y_shape
        n_keys = seq + n_past
    else:
        hbm = pl.BlockSpec(memory_space=pl.ANY)
        out_specs = [row_blk(D_MODEL), hbm, hbm, hbm, hbm]
        out_shape = ([y_shape] + [jax.ShapeDtypeStruct((total // seq, 1, seq, n_heads, LANES), F32)] * 2
                     + [jax.ShapeDtypeStruct(w_in.shape, BF16), jax.ShapeDtypeStruct(w_out.shape, BF16)])
        n_keys = n_rows
    n_cols = n_keys if latent else ROW_CHUNK
    scratch = [pltpu.VMEM((n_rows, D_MODEL), BF16),
               pltpu.VMEM((n_rows, D_MODEL), BF16),
               pltpu.VMEM((2, n_keys, D_MODEL), BF16),
               pltpu.VMEM((n_keys, D_MODEL), BF16),
               pltpu.VMEM((n_rows, D_MODEL), F32),
               pltpu.VMEM((2, 2, ROW_CHUNK, n_cols), F32),
               pltpu.VMEM((2, 2, ROW_CHUNK, n_cols), BF16)]
    if latent:
        scratch += [pltpu.VMEM((2, n_past, D_MODEL), F32), pltpu.SemaphoreType.DMA((2,))]
    else:
        scratch += [pltpu.VMEM((n_blocks, 2, ROW_CHUNK, D_MODEL), F32),
                    pltpu.SemaphoreType.DMA((n_blocks, 2))]
        scratch += _weight_scratch(w_in, w_out)
    return pl.pallas_call(
        functools.partial(_odd_kernel, latent, layer, n_rows, seq, alpha, lam_init),
        grid=grid, in_specs=in_specs, out_specs=out_specs, out_shape=out_shape,
        scratch_shapes=scratch,
        compiler_params=pltpu.CompilerParams(dimension_semantics=("arbitrary",), vmem_limit_bytes=VMEM_LIMIT),
        name="odd_layer_latent" if latent else "odd_layer_context",
    )(*args)


def kernel(x_prompt, x_sample, cache_a_k, cache_a_v, cache_b_k, cache_b_v, cache_c_k, cache_c_v, c, c_ctx,
           w_mod, b_mod, ln_g, ln_b, w_in_even, w_out_even, q_norm_a, k_norm_a, sink_b, w_in_odd, w_out_odd,
           lambda_q1, lambda_k1, lambda_q2, lambda_k2, subln_c):
    depth = w_mod.shape[0]
    batch, seq, _ = x_prompt.shape
    dec_batch, dec_seq, _ = x_sample.shape
    n_past = cache_a_k.shape[2]
    alpha = (2 * depth) ** 0.25
    assert seq == ROW_CHUNK and n_past % ROW_CHUNK == 0 and dec_seq % ROW_CHUNK == 0

    mod = _modulation(c, c_ctx, w_mod, b_mod)
    cos, sin, cos_t, sin_t = _rope_tables(dec_seq)

    bf16_weights = {}

    def run(x, latent, n_batch, s, rows_even, rows_odd):
        kv = {"a_k": [], "a_v": [], "b_k": [], "b_v": [], "c_k": [], "c_v": []}
        for l in range(depth):
            if l % 2 == 0:
                e = l // 2
                extras = ()
                if latent:
                    k_t = lambda t: t[:, e].transpose(0, 2, 3, 1).reshape(n_batch, LANES, n_past)
                    extras = (cos, sin, cos_t, sin_t,
                              k_t(cache_a_k), k_t(cache_a_v), k_t(cache_b_k), k_t(cache_b_v))
                w_in, w_out = bf16_weights[l] if latent else (w_in_even[e], w_out_even[e])
                res = _even_layer(x, mod, l, w_in, w_out, q_norm_a, k_norm_a,
                                  sink_b[e], ln_g, ln_b, latent, s, rows_even, alpha, extras)
                if latent:
                    x = res
                else:
                    x = res[0]
                    bf16_weights[l] = res[5:7]
                    for name, t in zip(("a_k", "a_v", "b_k", "b_v"), res[1:5]):
                        kv[name].append(t.reshape(n_batch, 2, HEAD_DIM, s).transpose(0, 3, 1, 2))
            else:
                o = l // 2
                lam_init = 0.8 - 0.6 * math.exp(-0.3 * l)
                extras = (cos, sin, cache_c_k, cache_c_v) if latent else ()
                lams = [t[o][None, :] for t in (lambda_q1, lambda_k1, lambda_q2, lambda_k2)]
                w_in, w_out = bf16_weights[l] if latent else (w_in_odd[o], w_out_odd[o])
                res = _odd_layer(x, mod, l, w_in, w_out, lams,
                                 subln_c[o][None, :], ln_g, ln_b, latent, s, rows_odd, alpha, lam_init, extras)
                if latent:
                    x = res
                else:
                    x = res[0]
                    bf16_weights[l] = res[3:5]
                    kv["c_k"].append(res[1][:, 0])
                    kv["c_v"].append(res[2][:, 0])
        return x, kv

    y_ctx, kv = run(x_prompt.reshape(batch * seq, D_MODEL), False, batch, seq, 1024, 512)
    y_lat, _ = run(x_sample.reshape(dec_batch * dec_seq, D_MODEL), True, dec_batch, dec_seq, dec_seq, dec_seq)

    stack = lambda name: jnp.stack(kv[name], axis=1)
    return (y_ctx.reshape(batch, seq, D_MODEL), y_lat.reshape(dec_batch, dec_seq, D_MODEL),
            stack("a_k"), stack("a_v"), stack("b_k"), stack("b_v"), stack("c_k"), stack("c_v"))
```

```python
import functools
import math

import jax
import jax.numpy as jnp
import numpy as np
from jax import lax
from jax.experimental import pallas as pl
from jax.experimental.pallas import tpu as pltpu

F32 = jnp.float32
BF16 = jnp.bfloat16

D_MODEL = 1024
HEAD_DIM = 64
GRID_W = 64
WINDOW = 128
ROPE_THETA = 10000.0
EPS = 1e-6
NEG_INF = -1e30
LOG2E = 1.4426950408889634
Q_SCALE = HEAD_DIM ** -0.5 * LOG2E
LANES = 128
ROW_CHUNK = 256
SOFTMAX_VREGS = 40
VMEM_LIMIT = 60000 * 1024
W_SLAB_ROWS = 128


def _silu(x):
    return x / (1.0 + jnp.exp(-x))


def _dot(a, b):
    return jnp.dot(a, b, preferred_element_type=F32)


def _dot_nt(a, b):
    return lax.dot_general(a, b, (((1,), (1,)), ((), ())), preferred_element_type=F32)


def _lane_iota(rows):
    return lax.broadcasted_iota(jnp.int32, (rows, LANES), 1)


def _chunk_rows(i):
    if isinstance(i, int):
        return pl.ds(i * ROW_CHUNK, ROW_CHUNK)
    return pl.ds(pl.multiple_of(i * ROW_CHUNK, ROW_CHUNK), ROW_CHUNK)


def _softmax_rows(n_cols):
    rows = 8
    while rows * 2 * n_cols <= SOFTMAX_VREGS * 1024 and rows * 2 <= ROW_CHUNK:
        rows *= 2
    return rows


def _rope(a, cos, sin_signed):
    lane = _lane_iota(a.shape[0])
    fwd = pltpu.roll(a, LANES - 16, 1)
    bwd = pltpu.roll(a, 16, 1)
    partner = jnp.where((lane & 16) == 0, fwd, bwd)
    return a * cos + partner * sin_signed


def _rope_t(a, cos_t, sin_t):
    blocks = [a[16 * b:16 * (b + 1), :] for b in range(a.shape[0] // 16)]
    partner = jnp.concatenate([blocks[b ^ 1] for b in range(len(blocks))], axis=0)
    return a * cos_t + partner * sin_t


def _store_kt_variants(scr, chunk, kt):
    width = scr.shape[-1]
    per_block = kt.shape[1] // width
    zero = jnp.zeros((HEAD_DIM, kt.shape[1]), F32)
    for j in range(2):
        kj = kt[HEAD_DIM * j:HEAD_DIM * (j + 1), :]
        for par, full in enumerate((jnp.concatenate([kj, zero], axis=0), jnp.concatenate([zero, kj], axis=0))):
            full = full.astype(BF16)
            for c in range(per_block):
                scr[2 * j + par, chunk * per_block + c] = full[:, width * c:width * (c + 1)]


def _store_v_variants(scr, rows, a):
    lane = _lane_iota(a.shape[0])
    lo = lane < HEAD_DIM
    swapped = pltpu.roll(a, HEAD_DIM, 1)
    one = jnp.ones_like(a)
    scr[0, rows, :] = jnp.where(lo, a, one).astype(BF16)
    scr[1, rows, :] = jnp.where(lo, one, swapped).astype(BF16)
    scr[2, rows, :] = jnp.where(lo, swapped, one).astype(BF16)
    scr[3, rows, :] = jnp.where(lo, one, a).astype(BF16)


def _layer_norm_rows(z, g, b):
    mu = jnp.mean(z, axis=-1, keepdims=True)
    zc = z - mu
    var = jnp.mean(zc * zc, axis=-1, keepdims=True)
    return zc * lax.rsqrt(var + EPS) * g + b


def _modulated(x_ref, mod_ref, mod_row, rows):
    shift = mod_ref[0, pl.ds(mod_row, 1), 0:D_MODEL]
    scale = mod_ref[0, pl.ds(mod_row, 1), D_MODEL:2 * D_MODEL]
    return (x_ref[rows, :] * (1.0 + scale) + shift).astype(BF16)


def _out_proj_norm(x_ref, mod_ref, mod_row, attn_scr, w_out_ref, lng_ref, lnb_ref, layer, y_ref, n_rows, alpha):
    gate = mod_ref[0, pl.ds(mod_row, 1), 2 * D_MODEL:3 * D_MODEL]
    g = lng_ref[layer:layer + 1, :]
    b = lnb_ref[layer:layer + 1, :]

    def body(i, carry):
        rows = _chunk_rows(i)
        out = _dot(attn_scr[rows, :], w_out_ref[...])
        z = alpha * x_ref[rows, :] + gate * out
        y_ref[rows, :] = _layer_norm_rows(z, g, b)
        return carry

    lax.fori_loop(0, n_rows // ROW_CHUNK, body, 0, unroll=True)


class _ContextWeights:
    def __init__(self, step, w_in, w_out, stage, sems, out_sems):
        self.step, self.w_in, self.w_out, self.stage, self.sems = step, w_in, w_out, stage, sems
        self.out_copies = [pltpu.make_async_copy(w[1], w[2], out_sems.at[n]) for n, w in enumerate((w_in, w_out))]

    def _slab_copies(self, w_hbm):
        n_cols = w_hbm.shape[1]
        return [pltpu.make_async_copy(w_hbm.at[pl.ds(s * W_SLAB_ROWS, W_SLAB_ROWS), :],
                                      self.stage.at[s, :, pl.ds(0, n_cols)], self.sems.at[s])
                for s in range(w_hbm.shape[0] // W_SLAB_ROWS)]

    def _cast(self, w_hbm, w_scr):
        n_cols = w_hbm.shape[1]
        for s, copy in enumerate(self._slab_copies(w_hbm)):
            copy.wait()
            w_scr[pl.ds(s * W_SLAB_ROWS, W_SLAB_ROWS), :] = self.stage[s, :, 0:n_cols].astype(BF16)

    def load_in_proj(self):
        @pl.when(self.step == 0)
        def _():
            for copy in self._slab_copies(self.w_in[0]):
                copy.start()
            self._cast(self.w_in[0], self.w_in[1])
            self.out_copies[0].start()
            for copy in self._slab_copies(self.w_out[0]):
                copy.start()

    def load_out_proj(self):
        @pl.when(self.step == 0)
        def _():
            self._cast(self.w_out[0], self.w_out[1])
            self.out_copies[1].start()

    def finish(self):
        @pl.when(self.step == 0)
        def _():
            for copy in self.out_copies:
                copy.wait()


def _run_pipeline(n_items, stages):
    for u in range(n_items + len(stages) - 1):
        for k, stage in enumerate(stages):
            t = u - k
            if 0 <= t < n_items:
                stage(t, t % 2)


def _attend_blocks(block_stages, n_blocks, n_items, unrolled):
    assert n_items % 2 == 0
    if unrolled:
        per_block = [block_stages(i) for i in range(n_blocks)]
        stages = [lambda g, slot, k=k: per_block[g // n_items][k](g % n_items, slot) for k in range(3)]
        _run_pipeline(n_blocks * n_items, stages)
    else:
        def body(i, carry):
            _run_pipeline(n_items, block_stages(i))
            return carry

        lax.fori_loop(0, n_blocks, body, 0)


def _even_kernel(latent, layer, n_rows, seq, alpha, *refs):
    if latent:
        (x_ref, mod_ref, w_in_ref, w_out_ref, norms_ref, sink_ref, lng_ref, lnb_ref,
         cos_ref, sin_ref, cost_ref, sint_ref, cakt_ref, cav_ref, cbkt_ref, cbv_ref,
         y_ref,
         ha_scr, qa_scr, qb_scr, ka_scr, va_scr, kb_scr, vb_scr, g_scr, s_scr, p_scr, es_scr, wkt_scr, wv_scr,
         bias_scr) = refs
    else:
        (x_ref, mod_ref, w_in_hbm, w_out_hbm, norms_ref, sink_ref, lng_ref, lnb_ref,
         y_ref, nakt_ref, navt_ref, nbkt_ref, nbvt_ref, w_in_bf_hbm, w_out_bf_hbm,
         ha_scr, qa_scr, qb_scr, ka_scr, va_scr, kb_scr, vb_scr, g_scr, s_scr, p_scr, es_scr, wkt_scr,
         wv_scr, w_in_ref, w_out_ref, w_stage, w_sems, w_out_sems) = refs

    step = pl.program_id(0)
    if not latent:
        weights = _ContextWeights(step, (w_in_hbm, w_in_ref, w_in_bf_hbm), (w_out_hbm, w_out_ref, w_out_bf_hbm),
                                  w_stage, w_sems, w_out_sems)
        weights.load_in_proj()
    mod_row = step + 1 if latent else 0

    col_ka, col_va, col_kb, col_vb = 512, 640, 1792, 1920

    @pl.when(step == 0)
    def _():
        for r, c0 in enumerate((col_ka, col_kb)):
            wkt_scr[LANES * r:LANES * (r + 1), :] = w_in_ref[:, c0:c0 + LANES].T
        wv_scr[:, 0:LANES] = w_in_ref[:, col_va:col_va + LANES]
        wv_scr[:, LANES:2 * LANES] = w_in_ref[:, col_vb:col_vb + LANES]

    n_lat_chunks = seq // ROW_CHUNK
    if latent:
        n_past = cav_ref.shape[2]
        past_rows = pl.ds(seq, n_past)
        _store_kt_variants(ka_scr, n_lat_chunks, cakt_ref[0])
        _store_kt_variants(kb_scr, n_lat_chunks, cbkt_ref[0])
        _store_v_variants(va_scr, past_rows, cav_ref[0].T)
        _store_v_variants(vb_scr, past_rows, cbv_ref[0].T)

    knt = norms_ref[0:LANES, :]
    qn = norms_ref[LANES:LANES + 1, 0:LANES]

    def proj(i, carry):
        rows = _chunk_rows(i)
        hh = _modulated(x_ref, mod_ref, mod_row, rows)
        if latent:
            cos = cos_ref[rows, :]
            sin = sin_ref[rows, :]
            rot = lambda a: _rope(a, cos, sin)
            rot_t = lambda a: _rope_t(a, cost_ref[i], sint_ref[i])
        else:
            rot = rot_t = lambda a: a

        acc = _dot(hh, w_in_ref[:, 0:512])
        lo_lanes = _lane_iota(ROW_CHUNK) < HEAD_DIM
        for j in range(4):
            a = acc[:, LANES * j:LANES * (j + 1)]
            sq = a * a
            first = jnp.sum(jnp.where(lo_lanes, sq, 0.0), axis=1, keepdims=True)
            second = jnp.sum(jnp.where(lo_lanes, 0.0, sq), axis=1, keepdims=True)
            ms = jnp.where(lo_lanes, first, second) * (1.0 / HEAD_DIM)
            a = rot(a * lax.rsqrt(ms + EPS) * qn)
            qa_scr[rows, LANES * j:LANES * (j + 1)] = (a * Q_SCALE).astype(BF16)
        acc = _dot(hh, w_in_ref[:, 1280:1792])
        for j in range(4):
            a = rot(acc[:, LANES * j:LANES * (j + 1)])
            qb_scr[rows, LANES * j:LANES * (j + 1)] = (a * Q_SCALE).astype(BF16)
        g_scr[rows, 0:512] = _silu(_dot(hh, w_in_ref[:, 768:1280]))
        g_scr[rows, 512:1024] = _silu(_dot(hh, w_in_ref[:, 2048:2560]))
        v = _dot(hh, wv_scr[...])
        _store_v_variants(va_scr, rows, v[:, 0:LANES])
        _store_v_variants(vb_scr, rows, v[:, LANES:2 * LANES])

        kt = _dot_nt(wkt_scr[0:2 * LANES, :], hh)
        heads = [kt[HEAD_DIM * h:HEAD_DIM * (h + 1), :] for h in range(2)]
        kat = jnp.concatenate([blk * lax.rsqrt(jnp.mean(blk * blk, axis=0, keepdims=True) + EPS) for blk in heads],
                              axis=0) * knt
        kbt = kt[LANES:2 * LANES, :]
        if not latent:
            vt = v.T
            nakt_ref[i] = kat
            nbkt_ref[i] = kbt
            navt_ref[i] = vt[0:LANES, :]
            nbvt_ref[i] = vt[LANES:2 * LANES, :]
        _store_kt_variants(ka_scr, i, rot_t(kat))
        _store_kt_variants(kb_scr, i, rot_t(kbt))
        return carry

    lax.fori_loop(0, n_rows // ROW_CHUNK, proj, 0, unroll=2)

    sinks = [sink_ref[h] * LOG2E for h in range(8)]
    ck = ROW_CHUNK
    bk = kb_scr.shape[-1]
    win = ROW_CHUNK + 2 * WINDOW
    n_items = 8

    def block_stages(i):
        rows = _chunk_rows(i)
        if latent:
            a_chunks = list(range(n_lat_chunks + n_past // ck))
            a_keys = pl.ds(0, seq + n_past)
            w0 = jnp.clip(i * (ck // bk) - WINDOW // bk, 0, (seq - win) // bk)
            win_rows = pl.ds(pl.multiple_of(w0 * bk, bk), win)
            dist = (lax.broadcasted_iota(jnp.int32, (ROW_CHUNK, ck), 1)
                    - lax.broadcasted_iota(jnp.int32, (ROW_CHUNK, ck), 0))
            for c in range(win // ck):
                off = w0 * bk + c * ck - i * ck
                bias_scr[c] = jnp.where(jnp.abs(dist + off) <= WINDOW, 0.0, NEG_INF).astype(F32)
            b_first = [w0 + c * (ck // bk) for c in range(win // ck)] + [seq // bk]
            n_biased = win // ck
            b_cols = win + n_past
        else:
            a_chunks = [i]
            a_keys = rows
            b_first = [i * (ck // bk)]
            n_biased = 0
            b_cols = ck
        a_cols = len(a_chunks) * ck

        def qk(t, slot):
            p, branch = divmod(t, 2)
            cols = slice(LANES * p, LANES * (p + 1))
            kvh = p // 2
            q = (qb_scr if branch else qa_scr)[rows, cols]
            for par in (0, 1):
                var = 2 * kvh + par
                if branch:
                    tiles = [jnp.concatenate([kb_scr[var, first + d] for d in range(ck // bk)], axis=1)
                             for first in b_first]
                else:
                    tiles = [ka_scr[var, chunk] for chunk in a_chunks]
                for c, kt in enumerate(tiles):
                    s = _dot(q, kt)
                    if branch and c < n_biased:
                        s = s + bias_scr[c]
                    s_scr[slot, par, :, c * ck:(c + 1) * ck] = s

        def softmax(t, slot):
            p, branch = divmod(t, 2)
            n_cols = b_cols if branch else a_cols
            rb = _softmax_rows(n_cols)
            for par in (0, 1):
                for r in range(ROW_CHUNK // rb):
                    sub = slice(r * rb, (r + 1) * rb)
                    s = s_scr[slot, par, sub, 0:n_cols]
                    m = jnp.max(s, axis=1, keepdims=True)
                    if branch:
                        sink = sinks[2 * p + par]
                        m = jnp.maximum(m, sink)
                        es_scr[slot, sub, HEAD_DIM * par:HEAD_DIM * (par + 1)] = jnp.broadcast_to(
                            jnp.exp2(sink - m), (rb, HEAD_DIM))
                    p_scr[slot, par, sub, 0:n_cols] = jnp.exp2((s - m).astype(BF16))

        def pv(t, slot):
            p, branch = divmod(t, 2)
            kvh = p // 2
            v_scr = vb_scr if branch else va_scr
            accs = []
            for par in (0, 1):
                var = 2 * kvh + par
                if latent and branch:
                    n_loc = win
                    accs.append(_dot(p_scr[slot, par, :, 0:n_loc], v_scr[var, win_rows, :])
                                + _dot(p_scr[slot, par, :, n_loc:b_cols], v_scr[var, past_rows, :]))
                else:
                    accs.append(_dot(p_scr[slot, par, :, 0:a_cols], v_scr[var, a_keys, :]))
            lo = _lane_iota(ROW_CHUNK) < HEAD_DIM
            denom = pltpu.roll(jnp.where(lo, accs[1], accs[0]), HEAD_DIM, 1)
            if branch:
                denom = denom + es_scr[slot]
            o = jnp.where(lo, accs[0], accs[1]) / denom
            ocols = slice(512 * branch + LANES * p, 512 * branch + LANES * (p + 1))
            ha_scr[rows, ocols] = (o * g_scr[rows, ocols]).astype(BF16)

        return qk, softmax, pv

    _attend_blocks(block_stages, n_rows // ROW_CHUNK, n_items, unrolled=not latent)

    if not latent:
        weights.load_out_proj()
    _out_proj_norm(x_ref, mod_ref, mod_row, ha_scr, w_out_ref, lng_ref, lnb_ref, layer, y_ref, n_rows, alpha)
    if not latent:
        weights.finish()


def _odd_kernel(latent, layer, n_rows, seq, alpha, lam_init, *refs):
    if latent:
        (x_ref, mod_ref, w_in_ref, w_out_ref, lq1_ref, lk1_ref, lq2_ref, lk2_ref, sub_ref, lng_ref, lnb_ref,
         cos_ref, sin_ref, cck_hbm, ccv_hbm,
         y_ref,
         ha_scr, q_scr, k_scr, v_scr, g_scr, s_scr, p_scr, past_stage, past_sems) = refs
    else:
        (x_ref, mod_ref, w_in_hbm, w_out_hbm, lq1_ref, lk1_ref, lq2_ref, lk2_ref, sub_ref, lng_ref, lnb_ref,
         y_ref, nck_hbm, ncv_hbm, w_in_bf_hbm, w_out_bf_hbm,
         ha_scr, q_scr, k_scr, v_scr, g_scr, s_scr, p_scr, kv_stage, kv_sems,
         w_in_ref, w_out_ref, w_stage, w_sems, w_out_sems) = refs

    step = pl.program_id(0)
    if not latent:
        weights = _ContextWeights(step, (w_in_hbm, w_in_ref, w_in_bf_hbm), (w_out_hbm, w_out_ref, w_out_bf_hbm),
                                  w_stage, w_sems, w_out_sems)
        weights.load_in_proj()
    mod_row = step + 1 if latent else 0

    n_heads = D_MODEL // LANES
    n_blocks = n_rows // ROW_CHUNK
    lo = _lane_iota(ROW_CHUNK) < HEAD_DIM

    def kv_out_copies(blk):
        elem = step * n_blocks + blk
        return [pltpu.make_async_copy(kv_stage.at[blk, t, :, pl.ds(LANES * h, LANES)],
                                      out.at[elem, 0, :, h, :], kv_sems.at[blk, t])
                for t, out in enumerate((nck_hbm, ncv_hbm)) for h in range(n_heads)]

    def store_k(rows, h, a):
        cols = slice(LANES * h, LANES * (h + 1))
        zero = jnp.zeros_like(a)
        k_scr[0, rows, cols] = jnp.where(lo, a, zero).astype(BF16)
        k_scr[1, rows, cols] = jnp.where(lo, zero, a).astype(BF16)

    if latent:
        n_past = cck_hbm.shape[2]
        past = pl.ds(seq, n_past)
        past_copies = [pltpu.make_async_copy(cache.at[step, layer // 2, :, h, :],
                                             past_stage.at[t, :, pl.ds(LANES * h, LANES)], past_sems.at[t])
                       for t, cache in enumerate((cck_hbm, ccv_hbm)) for h in range(n_heads)]
        for copy in past_copies:
            copy.start()

    def proj(i, carry):
        rows = _chunk_rows(i)
        hh = _modulated(x_ref, mod_ref, mod_row, rows)
        if latent:
            cos = cos_ref[rows, :]
            sin = sin_ref[rows, :]
            rot = lambda a: _rope(a, cos, sin)
        else:
            rot = lambda a: a
        for half in range(2):
            acc = _dot(hh, w_in_ref[:, 512 * half:512 * (half + 1)])
            for j in range(4):
                a = rot(acc[:, LANES * j:LANES * (j + 1)])
                cols = slice(512 * half + LANES * j, 512 * half + LANES * (j + 1))
                q_scr[rows, cols] = (a * Q_SCALE).astype(BF16)
        for half in range(2):
            acc = _dot(hh, w_in_ref[:, 1024 + 512 * half:1024 + 512 * (half + 1)])
            if not latent:
                kv_stage[i, 0, :, 512 * half:512 * (half + 1)] = acc
            for j in range(4):
                store_k(rows, 4 * half + j, rot(acc[:, LANES * j:LANES * (j + 1)]))
        for half in range(2):
            acc = _dot(hh, w_in_ref[:, 2048 + 512 * half:2048 + 512 * (half + 1)])
            if not latent:
                kv_stage[i, 1, :, 512 * half:512 * (half + 1)] = acc
            v_scr[rows, 512 * half:512 * (half + 1)] = acc.astype(BF16)
        if not latent:
            for copy in kv_out_copies(i):
                copy.start()
        for half in range(2):
            acc = _dot(hh, w_in_ref[:, 3072 + 512 * half:3072 + 512 * (half + 1)])
            g_scr[rows, 512 * half:512 * (half + 1)] = _silu(acc)
        return carry

    if latent:
        lax.fori_loop(0, n_blocks, proj, 0, unroll=2)
    else:
        for blk in range(n_blocks):
            proj(blk, 0)

    if latent:
        for copy in past_copies:
            copy.wait()
        for h in range(n_heads):
            store_k(past, h, past_stage[0, :, LANES * h:LANES * (h + 1)])
        v_scr[past, :] = past_stage[1].astype(BF16)

    lam = (jnp.exp(jnp.sum(lq1_ref[...] * lk1_ref[...], axis=1, keepdims=True))
           - jnp.exp(jnp.sum(lq2_ref[...] * lk2_ref[...], axis=1, keepdims=True)) + lam_init)
    sub = sub_ref[...] * (1.0 - lam_init)
    n_keys = seq + n_past if latent else ROW_CHUNK
    rb = _softmax_rows(n_keys)
    ones = jnp.ones((n_keys, LANES), BF16)

    def block_stages(i):
        rows = _chunk_rows(i)
        keys = pl.ds(0, n_keys) if latent else rows

        def qk(h, slot):
            cols = slice(LANES * h, LANES * (h + 1))
            q = q_scr[rows, cols]
            for m in (0, 1):
                s_scr[slot, m] = _dot_nt(q, k_scr[m, keys, cols])

        def softmax(h, slot):
            for m in (0, 1):
                for r in range(ROW_CHUNK // rb):
                    sub_rows = slice(r * rb, (r + 1) * rb)
                    s = s_scr[slot, m, sub_rows, :]
                    top = jnp.max(s, axis=1, keepdims=True)
                    p_scr[slot, m, sub_rows, :] = jnp.exp2((s - top).astype(BF16))

        def pv(h, slot):
            cols = slice(LANES * h, LANES * (h + 1))
            v_ext = jnp.concatenate([v_scr[keys, cols], ones], axis=1)
            maps = []
            for m in (0, 1):
                acc = _dot(p_scr[slot, m], v_ext)
                maps.append(acc[:, 0:LANES] / acc[:, LANES:2 * LANES])
            o = maps[0] - lam * maps[1]
            ms = jnp.mean(o * o, axis=1, keepdims=True)
            o = o * lax.rsqrt(ms + EPS) * sub
            ha_scr[rows, cols] = (o * g_scr[rows, cols]).astype(BF16)

        return qk, softmax, pv

    _attend_blocks(block_stages, n_blocks, n_heads, unrolled=not latent)

    if not latent:
        weights.load_out_proj()
    _out_proj_norm(x_ref, mod_ref, mod_row, ha_scr, w_out_ref, lng_ref, lnb_ref, layer, y_ref, n_rows, alpha)

    if not latent:
        for blk in range(n_blocks):
            for copy in kv_out_copies(blk):
                copy.wait()
        weights.finish()


MOD_SLAB_ROWS = 128
MOD_COL_BLOCK = 1024
MOD_SLOTS = 4


def _mod_kernel(n_cond, cv_ref, w_hbm, b_ref, o_ref, sb_scr, ring, acc_scr, sems):
    depth, n_in, n_out = w_hbm.shape
    sublanes = 8
    slabs_per_layer = n_in // MOD_SLAB_ROWS
    slabs = [(l, rs) for l in range(depth) for rs in range(slabs_per_layer)]

    def slab_copy(n):
        l, rs = slabs[n]
        return pltpu.make_async_copy(w_hbm.at[l, pl.ds(rs * MOD_SLAB_ROWS, MOD_SLAB_ROWS), :],
                                     ring.at[n % MOD_SLOTS], sems.at[n % MOD_SLOTS])

    for n in range(min(MOD_SLOTS, len(slabs))):
        slab_copy(n).start()
    s_t = _silu(cv_ref[...]).T
    for r in range(n_cond):
        sb_scr[r] = jnp.broadcast_to(s_t[:, r:r + 1], (n_in, LANES))

    for n, (l, rs) in enumerate(slabs):
        slab_copy(n).wait()
        for cb in range(n_out // MOD_COL_BLOCK):
            cols = pl.ds(cb * MOD_COL_BLOCK, MOD_COL_BLOCK)
            if rs == 0:
                accs = (jnp.zeros((sublanes, MOD_COL_BLOCK), F32),) * n_cond
            else:
                accs = tuple(acc_scr[r, :, cols] for r in range(n_cond))

            def body(kb, accs, n=n, rs=rs, cols=cols):
                w = ring[n % MOD_SLOTS, pl.ds(pl.multiple_of(kb * sublanes, sublanes), sublanes), cols]
                s_rows = pl.ds(pl.multiple_of(rs * MOD_SLAB_ROWS + kb * sublanes, sublanes), sublanes)
                return tuple(acc + w * jnp.tile(sb_scr[r, s_rows, :], (1, MOD_COL_BLOCK // LANES))
                             for r, acc in enumerate(accs))

            accs = lax.fori_loop(0, MOD_SLAB_ROWS // sublanes, body, accs, unroll=8)
            if rs < slabs_per_layer - 1:
                for r in range(n_cond):
                    acc_scr[r, :, cols] = accs[r]
            else:
                rows = [jnp.sum(acc, axis=0, keepdims=True) + b_ref[l:l + 1, cols] for acc in accs]
                o_ref[l, :, cols] = jnp.concatenate(rows + [jnp.zeros((8 - n_cond, MOD_COL_BLOCK), F32)], axis=0)
        if n + MOD_SLOTS < len(slabs):
            slab_copy(n + MOD_SLOTS).start()


def _full(shape, **kw):
    zeros = (0,) * len(shape)
    return pl.BlockSpec(shape, lambda i: zeros, **kw)


def _weight_specs(latent, w_in, w_out):
    if latent:
        single = pl.Buffered(1)
        return [_full(w_in.shape, pipeline_mode=single), _full(w_out.shape, pipeline_mode=single)]
    return [pl.BlockSpec(memory_space=pl.ANY), pl.BlockSpec(memory_space=pl.ANY)]


def _weight_scratch(w_in, w_out):
    assert w_in.shape[0] % W_SLAB_ROWS == 0 and w_out.shape[0] % W_SLAB_ROWS == 0
    n_slabs = max(w_in.shape[0], w_out.shape[0]) // W_SLAB_ROWS
    return [pltpu.VMEM(w_in.shape, BF16), pltpu.VMEM(w_out.shape, BF16),
            pltpu.VMEM((n_slabs, W_SLAB_ROWS, max(w_in.shape[1], w_out.shape[1])), F32),
            pltpu.SemaphoreType.DMA((n_slabs,)), pltpu.SemaphoreType.DMA((2,))]


def _rope_tables(seq):
    t = np.arange(seq)
    n_freq = HEAD_DIM // 4
    freqs = ROPE_THETA ** (-np.arange(n_freq, dtype=np.float64) / n_freq)
    ang_row = (t // GRID_W)[:, None] * freqs
    ang_col = (t % GRID_W)[:, None] * freqs
    ang = np.concatenate([ang_row, ang_row, ang_col, ang_col], axis=1)
    sign = np.concatenate([-np.ones(n_freq), np.ones(n_freq)] * 2)[None, :]
    cos = np.tile(np.cos(ang), (1, 2)).astype(np.float32)
    sin = np.tile(np.sin(ang) * sign, (1, 2)).astype(np.float32)
    chunked_t = lambda a: a.reshape(seq // ROW_CHUNK, ROW_CHUNK, LANES).transpose(0, 2, 1)
    return jnp.asarray(cos), jnp.asarray(sin), jnp.asarray(chunked_t(cos)), jnp.asarray(chunked_t(sin))


def _modulation(c, c_ctx, w_mod, b_mod):
    depth = w_mod.shape[0]
    n_cond = 1 + c.shape[0]
    cv = jnp.concatenate([c_ctx[None, :], c, jnp.zeros((8 - n_cond, D_MODEL), F32)], axis=0)
    assert D_MODEL % MOD_SLAB_ROWS == 0 and (3 * D_MODEL) % MOD_COL_BLOCK == 0
    return pl.pallas_call(
        functools.partial(_mod_kernel, n_cond),
        grid=(1,),
        in_specs=[_full(cv.shape), pl.BlockSpec(memory_space=pl.ANY), _full(b_mod.shape)],
        out_specs=_full((depth, 8, 3 * D_MODEL)),
        out_shape=jax.ShapeDtypeStruct((depth, 8, 3 * D_MODEL), F32),
        scratch_shapes=[pltpu.VMEM((n_cond, D_MODEL, LANES), F32),
                        pltpu.VMEM((MOD_SLOTS, MOD_SLAB_ROWS, 3 * D_MODEL), F32),
                        pltpu.VMEM((n_cond, 8, 3 * D_MODEL), F32),
                        pltpu.SemaphoreType.DMA((MOD_SLOTS,))],
        compiler_params=pltpu.CompilerParams(dimension_semantics=("arbitrary",)),
        name="adaln_modulation",
    )(cv, w_mod, b_mod)


def _even_layer(x, mod, layer, w_in, w_out, q_norm, k_norm, sink, ln_g, ln_b, latent, seq, n_rows, alpha, extras=()):
    total = x.shape[0]
    grid = (total // n_rows,)
    single = pl.Buffered(1)
    norms = jnp.concatenate([jnp.broadcast_to(jnp.tile(k_norm, 2)[:, None], (LANES, ROW_CHUNK)),
                             jnp.broadcast_to(jnp.tile(q_norm, 2 * ROW_CHUNK // LANES)[None, :], (8, ROW_CHUNK))], axis=0)

    row_blk = lambda width: pl.BlockSpec((n_rows, width), lambda i: (i, 0))
    in_specs = [row_blk(D_MODEL),
                pl.BlockSpec((1, 8, 3 * D_MODEL), lambda i: (layer, 0, 0)),
                *_weight_specs(latent, w_in, w_out),
                _full(norms.shape),
                pl.BlockSpec(memory_space=pltpu.SMEM),
                _full(ln_g.shape), _full(ln_b.shape)]
    args = [x, mod, w_in, w_out, norms, sink, ln_g, ln_b]
    y_shape = jax.ShapeDtypeStruct((total, D_MODEL), F32)
    n_blocks = n_rows // ROW_CHUNK
    if latent:
        cos, sin, cos_t, sin_t, cakt, cav, cbkt, cbv = extras
        n_past = cav.shape[2]
        in_specs += [_full(cos.shape, pipeline_mode=single), _full(sin.shape, pipeline_mode=single),
                     _full(cos_t.shape, pipeline_mode=single), _full(sin_t.shape, pipeline_mode=single)]
        in_specs += [pl.BlockSpec((1, LANES, n_past), lambda i: (i, 0, 0))] * 4
        args += [cos, sin, cos_t, sin_t, cakt, cav, cbkt, cbv]
        out_specs = row_blk(D_MODEL)
        out_shape = y_shape
        n_keys = seq + n_past
    else:
        kv_blk = pl.BlockSpec((n_blocks, LANES, ROW_CHUNK), lambda i: (i, 0, 0))
        hbm = pl.BlockSpec(memory_space=pl.ANY)
        out_specs = [row_blk(D_MODEL)] + [kv_blk] * 4 + [hbm, hbm]
        out_shape = ([y_shape] + [jax.ShapeDtypeStruct((total // seq, LANES, seq), F32)] * 4
                     + [jax.ShapeDtypeStruct(w_in.shape, BF16), jax.ShapeDtypeStruct(w_out.shape, BF16)])
        n_keys = n_rows
    n_kchunks = n_keys // ROW_CHUNK
    n_cols = n_keys if latent else ROW_CHUNK
    scratch = [pltpu.VMEM((n_rows, D_MODEL), BF16),
               pltpu.VMEM((n_rows, 512), BF16), pltpu.VMEM((n_rows, 512), BF16),
               pltpu.VMEM((4, n_kchunks, LANES, ROW_CHUNK), BF16), pltpu.VMEM((4, n_keys, LANES), BF16),
               pltpu.VMEM((4, n_keys // WINDOW, LANES, WINDOW), BF16), pltpu.VMEM((4, n_keys, LANES), BF16),
               pltpu.VMEM((n_rows, D_MODEL), F32),
               pltpu.VMEM((2, 2, ROW_CHUNK, n_cols), F32),
               pltpu.VMEM((2, 2, ROW_CHUNK, n_cols), BF16),
               pltpu.VMEM((2, ROW_CHUNK, LANES), F32),
               pltpu.VMEM((2 * LANES, D_MODEL), BF16),
               pltpu.VMEM((D_MODEL, 2 * LANES), BF16)]
    if latent:
        scratch.append(pltpu.VMEM((1 + 2 * WINDOW // ROW_CHUNK, ROW_CHUNK, ROW_CHUNK), F32))
    else:
        scratch += _weight_scratch(w_in, w_out)
    return pl.pallas_call(
        functools.partial(_even_kernel, latent, layer, n_rows, seq, alpha),
        grid=grid, in_specs=in_specs, out_specs=out_specs, out_shape=out_shape,
        scratch_shapes=scratch,
        compiler_params=pltpu.CompilerParams(dimension_semantics=("arbitrary",), vmem_limit_bytes=VMEM_LIMIT),
        name="even_layer_latent" if latent else "even_layer_context",
    )(*args)


def _odd_layer(x, mod, layer, w_in, w_out, lams, sub, ln_g, ln_b, latent, seq, n_rows, alpha, lam_init, extras=()):
    total = x.shape[0]
    grid = (total // n_rows,)
    row_blk = lambda width: pl.BlockSpec((n_rows, width), lambda i: (i, 0))
    single = pl.Buffered(1)
    in_specs = [row_blk(D_MODEL),
                pl.BlockSpec((1, 8, 3 * D_MODEL), lambda i: (layer, 0, 0)),
                *_weight_specs(latent, w_in, w_out),
                _full((1, HEAD_DIM)), _full((1, HEAD_DIM)), _full((1, HEAD_DIM)), _full((1, HEAD_DIM)),
                _full((1, LANES)),
                _full(ln_g.shape), _full(ln_b.shape)]
    args = [x, mod, w_in, w_out, *lams, sub, ln_g, ln_b]
    y_shape = jax.ShapeDtypeStruct((total, D_MODEL), F32)
    n_heads = D_MODEL // LANES
    n_blocks = n_rows // ROW_CHUNK
    if latent:
        cos, sin, cck, ccv = extras
        n_past = cck.shape[2]
        in_specs += [_full(cos.shape, pipeline_mode=single), _full(sin.shape, pipeline_mode=single)]
        in_specs += [pl.BlockSpec(memory_space=pl.ANY)] * 2
        args += [cos, sin, cck, ccv]
        out_specs = row_blk(D_MODEL)
        out_shape = y_shape
        n_keys = seq + n_past
    else:
        hbm = pl.BlockSpec(memory_space=pl.ANY)
        out_specs = [row_blk(D_MODEL), hbm, hbm, hbm, hbm]
        out_shape = ([y_shape] + [jax.ShapeDtypeStruct((total // seq, 1, seq, n_heads, LANES), F32)] * 2
                     + [jax.ShapeDtypeStruct(w_in.shape, BF16), jax.ShapeDtypeStruct(w_out.shape, BF16)])
        n_keys = n_rows
    n_cols = n_keys if latent else ROW_CHUNK
    scratch = [pltpu.VMEM((n_rows, D_MODEL), BF16),
               pltpu.VMEM((n_rows, D_MODEL), BF16),
               pltpu.VMEM((2, n_keys, D_MODEL), BF16),
               pltpu.VMEM((n_keys, D_MODEL), BF16),
               pltpu.VMEM((n_rows, D_MODEL), F32),
               pltpu.VMEM((2, 2, ROW_CHUNK, n_cols), F32),
               pltpu.VMEM((2, 2, ROW_CHUNK, n_cols), BF16)]
    if latent:
        scratch += [pltpu.VMEM((2, n_past, D_MODEL), F32), pltpu.SemaphoreType.DMA((2,))]
    else:
        scratch += [pltpu.VMEM((n_blocks, 2, ROW_CHUNK, D_MODEL), F32),
                    pltpu.SemaphoreType.DMA((n_blocks, 2))]
        scratch += _weight_scratch(w_in, w_out)
    return pl.pallas_call(
        functools.partial(_odd_kernel, latent, layer, n_rows, seq, alpha, lam_init),
        grid=grid, in_specs=in_specs, out_specs=out_specs, out_shape=out_shape,
        scratch_shapes=scratch,
        compiler_params=pltpu.CompilerParams(dimension_semantics=("arbitrary",), vmem_limit_bytes=VMEM_LIMIT),
        name="odd_layer_latent" if latent else "odd_layer_context",
    )(*args)


def kernel(x_prompt, x_sample, cache_a_k, cache_a_v, cache_b_k, cache_b_v, cache_c_k, cache_c_v, c, c_ctx,
           w_mod, b_mod, ln_g, ln_b, w_in_even, w_out_even, q_norm_a, k_norm_a, sink_b, w_in_odd, w_out_odd,
           lambda_q1, lambda_k1, lambda_q2, lambda_k2, subln_c):
    depth = w_mod.shape[0]
    batch, seq, _ = x_prompt.shape
    dec_batch, dec_seq, _ = x_sample.shape
    n_past = cache_a_k.shape[2]
    alpha = (2 * depth) ** 0.25
    assert seq == ROW_CHUNK and n_past % ROW_CHUNK == 0 and dec_seq % ROW_CHUNK == 0

    mod = _modulation(c, c_ctx, w_mod, b_mod)
    cos, sin, cos_t, sin_t = _rope_tables(dec_seq)

    bf16_weights = {}

    def run(x, latent, n_batch, s, rows_even, rows_odd):
        kv = {"a_k": [], "a_v": [], "b_k": [], "b_v": [], "c_k": [], "c_v": []}
        for l in range(depth):
            if l % 2 == 0:
                e = l // 2
                extras = ()
                if latent:
                    k_t = lambda t: t[:, e].transpose(0, 2, 3, 1).reshape(n_batch, LANES, n_past)
                    extras = (cos, sin, cos_t, sin_t,
                              k_t(cache_a_k), k_t(cache_a_v), k_t(cache_b_k), k_t(cache_b_v))
                w_in, w_out = bf16_weights[l] if latent else (w_in_even[e], w_out_even[e])
                res = _even_layer(x, mod, l, w_in, w_out, q_norm_a[e], k_norm_a[e],
                                  sink_b[e], ln_g, ln_b, latent, s, rows_even, alpha, extras)
                if latent:
                    x = res
                else:
                    x = res[0]
                    bf16_weights[l] = res[5:7]
                    for name, t in zip(("a_k", "a_v", "b_k", "b_v"), res[1:5]):
                        kv[name].append(t.reshape(n_batch, 2, HEAD_DIM, s).transpose(0, 3, 1, 2))
            else:
                o = l // 2
                lam_init = 0.8 - 0.6 * math.exp(-0.3 * l)
                extras = (cos, sin, cache_c_k, cache_c_v) if latent else ()
                lams = [t[o][None, :] for t in (lambda_q1, lambda_k1, lambda_q2, lambda_k2)]
                w_in, w_out = bf16_weights[l] if latent else (w_in_odd[o], w_out_odd[o])
                res = _odd_layer(x, mod, l, w_in, w_out, lams,
                                 subln_c[o][None, :], ln_g, ln_b, latent, s, rows_odd, alpha, lam_init, extras)
                if latent:
                    x = res
                else:
                    x = res[0]
                    bf16_weights[l] = res[3:5]
                    kv["c_k"].append(res[1][:, 0])
                    kv["c_v"].append(res[2][:, 0])
        return x, kv

    y_ctx, kv = run(x_prompt.reshape(batch * seq, D_MODEL), False, batch, seq, 1024, 512)
    y_lat, _ = run(x_sample.reshape(dec_batch * dec_seq, D_MODEL), True, dec_batch, dec_seq, dec_seq, dec_seq)

    stack = lambda name: jnp.stack(kv[name], axis=1)
    return (y_ctx.reshape(batch, seq, D_MODEL), y_lat.reshape(dec_batch, dec_seq, D_MODEL),
            stack("a_k"), stack("a_v"), stack("b_k"), stack("b_v"), stack("c_k"), stack("c_v"))
```

```python
import functools
import math

import jax
import jax.numpy as jnp
import numpy as np
from jax import lax
from jax.experimental import pallas as pl
from jax.experimental.pallas import tpu as pltpu

F32 = jnp.float32
BF16 = jnp.bfloat16

D_MODEL = 1024
HEAD_DIM = 64
GRID_W = 64
WINDOW = 128
ROPE_THETA = 10000.0
EPS = 1e-6
NEG_INF = -1e30
LOG2E = 1.4426950408889634
Q_SCALE = HEAD_DIM ** -0.5 * LOG2E
LANES = 128
ROW_CHUNK = 256
SOFTMAX_VREGS = 40
VMEM_LIMIT = 60000 * 1024
W_SLAB_ROWS = 128


def _silu(x):
    return x / (1.0 + jnp.exp(-x))


def _dot(a, b):
    return jnp.dot(a, b, preferred_element_type=F32)


def _dot_nt(a, b):
    return lax.dot_general(a, b, (((1,), (1,)), ((), ())), preferred_element_type=F32)


def _lane_iota(rows):
    return lax.broadcasted_iota(jnp.int32, (rows, LANES), 1)


def _chunk_rows(i):
    if isinstance(i, int):
        return pl.ds(i * ROW_CHUNK, ROW_CHUNK)
    return pl.ds(pl.multiple_of(i * ROW_CHUNK, ROW_CHUNK), ROW_CHUNK)


def _softmax_rows(n_cols):
    rows = 8
    while rows * 2 * n_cols <= SOFTMAX_VREGS * 1024 and rows * 2 <= ROW_CHUNK:
        rows *= 2
    return rows


def _rope(a, cos, sin_signed):
    lane = _lane_iota(a.shape[0])
    fwd = pltpu.roll(a, LANES - 16, 1)
    bwd = pltpu.roll(a, 16, 1)
    partner = jnp.where((lane & 16) == 0, fwd, bwd)
    return a * cos + partner * sin_signed


def _rope_t(a, cos_t, sin_t):
    blocks = [a[16 * b:16 * (b + 1), :] for b in range(a.shape[0] // 16)]
    partner = jnp.concatenate([blocks[b ^ 1] for b in range(len(blocks))], axis=0)
    return a * cos_t + partner * sin_t


def _store_kt_variants(scr, chunk, kt):
    width = scr.shape[-1]
    per_block = kt.shape[1] // width
    zero = jnp.zeros((HEAD_DIM, kt.shape[1]), F32)
    for j in range(2):
        kj = kt[HEAD_DIM * j:HEAD_DIM * (j + 1), :]
        for par, full in enumerate((jnp.concatenate([kj, zero], axis=0), jnp.concatenate([zero, kj], axis=0))):
            full = full.astype(BF16)
            for c in range(per_block):
                scr[2 * j + par, chunk * per_block + c] = full[:, width * c:width * (c + 1)]


def _store_v_variants(scr, rows, a):
    lane = _lane_iota(a.shape[0])
    lo = lane < HEAD_DIM
    swapped = pltpu.roll(a, HEAD_DIM, 1)
    one = jnp.ones_like(a)
    scr[0, rows, :] = jnp.where(lo, a, one).astype(BF16)
    scr[1, rows, :] = jnp.where(lo, one, swapped).astype(BF16)
    scr[2, rows, :] = jnp.where(lo, swapped, one).astype(BF16)
    scr[3, rows, :] = jnp.where(lo, one, a).astype(BF16)


def _layer_norm_rows(z, g, b):
    mu = jnp.mean(z, axis=-1, keepdims=True)
    zc = z - mu
    var = jnp.mean(zc * zc, axis=-1, keepdims=True)
    return zc * lax.rsqrt(var + EPS) * g + b


def _modulated(x_ref, mod_ref, mod_row, rows):
    shift = mod_ref[0, pl.ds(mod_row, 1), 0:D_MODEL]
    scale = mod_ref[0, pl.ds(mod_row, 1), D_MODEL:2 * D_MODEL]
    return (x_ref[rows, :] * (1.0 + scale) + shift).astype(BF16)


def _out_proj_norm(x_ref, mod_ref, mod_row, attn_scr, w_out_ref, lng_ref, lnb_ref, layer, y_ref, n_rows, alpha):
    gate = mod_ref[0, pl.ds(mod_row, 1), 2 * D_MODEL:3 * D_MODEL]
    g = lng_ref[layer:layer + 1, :]
    b = lnb_ref[layer:layer + 1, :]

    def body(i, carry):
        rows = _chunk_rows(i)
        out = _dot(attn_scr[rows, :], w_out_ref[...])
        z = alpha * x_ref[rows, :] + gate * out
        y_ref[rows, :] = _layer_norm_rows(z, g, b)
        return carry

    lax.fori_loop(0, n_rows // ROW_CHUNK, body, 0, unroll=True)


class _ContextWeights:
    def __init__(self, step, w_in, w_out, stage, sems, out_sems):
        self.step, self.w_in, self.w_out, self.stage, self.sems = step, w_in, w_out, stage, sems
        self.out_copies = [pltpu.make_async_copy(w[1], w[2], out_sems.at[n]) for n, w in enumerate((w_in, w_out))]

    def _slab_copies(self, w_hbm):
        n_cols = w_hbm.shape[1]
        return [pltpu.make_async_copy(w_hbm.at[pl.ds(s * W_SLAB_ROWS, W_SLAB_ROWS), :],
                                      self.stage.at[s, :, pl.ds(0, n_cols)], self.sems.at[s])
                for s in range(w_hbm.shape[0] // W_SLAB_ROWS)]

    def _cast(self, w_hbm, w_scr):
        n_cols = w_hbm.shape[1]
        for s, copy in enumerate(self._slab_copies(w_hbm)):
            copy.wait()
            w_scr[pl.ds(s * W_SLAB_ROWS, W_SLAB_ROWS), :] = self.stage[s, :, 0:n_cols].astype(BF16)

    def load_in_proj(self):
        @pl.when(self.step == 0)
        def _():
            for copy in self._slab_copies(self.w_in[0]):
                copy.start()
            self._cast(self.w_in[0], self.w_in[1])
            self.out_copies[0].start()
            for copy in self._slab_copies(self.w_out[0]):
                copy.start()

    def load_out_proj(self):
        @pl.when(self.step == 0)
        def _():
            self._cast(self.w_out[0], self.w_out[1])
            self.out_copies[1].start()

    def finish(self):
        @pl.when(self.step == 0)
        def _():
            for copy in self.out_copies:
                copy.wait()


def _run_pipeline(n_items, stages):
    for u in range(n_items + len(stages) - 1):
        for k, stage in enumerate(stages):
            t = u - k
            if 0 <= t < n_items:
                stage(t, t % 2)


def _attend_blocks(block_stages, n_blocks, n_items, unrolled):
    assert n_items % 2 == 0
    if unrolled:
        per_block = [block_stages(i) for i in range(n_blocks)]
        stages = [lambda g, slot, k=k: per_block[g // n_items][k](g % n_items, slot) for k in range(3)]
        _run_pipeline(n_blocks * n_items, stages)
    else:
        def body(i, carry):
            _run_pipeline(n_items, block_stages(i))
            return carry

        lax.fori_loop(0, n_blocks, body, 0)


def _even_kernel(latent, layer, n_rows, seq, alpha, *refs):
    if latent:
        (x_ref, mod_ref, w_in_ref, w_out_ref, norms_ref, sink_ref, lng_ref, lnb_ref,
         cos_ref, sin_ref, cost_ref, sint_ref, cakt_ref, cav_ref, cbkt_ref, cbv_ref,
         y_ref,
         attn_scr, qa_scr, qb_scr, ka_scr, va_scr, kb_scr, vb_scr, g_scr, s_scr, p_scr, es_scr, wkt_scr, wv_scr,
         bias_scr) = refs
    else:
        (x_ref, mod_ref, w_in_hbm, w_out_hbm, norms_ref, sink_ref, lng_ref, lnb_ref,
         y_ref, nakt_ref, navt_ref, nbkt_ref, nbvt_ref, w_in_bf_hbm, w_out_bf_hbm,
         attn_scr, qa_scr, qb_scr, ka_scr, va_scr, kb_scr, vb_scr, g_scr, s_scr, p_scr, es_scr, wkt_scr,
         wv_scr, w_in_ref, w_out_ref, w_stage, w_sems, w_out_sems) = refs

    step = pl.program_id(0)
    if not latent:
        weights = _ContextWeights(step, (w_in_hbm, w_in_ref, w_in_bf_hbm), (w_out_hbm, w_out_ref, w_out_bf_hbm),
                                  w_stage, w_sems, w_out_sems)
        weights.load_in_proj()
    mod_row = step + 1 if latent else 0

    col_ka, col_va, col_kb, col_vb = 512, 640, 1792, 1920

    @pl.when(step == 0)
    def _():
        for r, c0 in enumerate((col_ka, col_kb)):
            wkt_scr[LANES * r:LANES * (r + 1), :] = w_in_ref[:, c0:c0 + LANES].T
        wv_scr[:, 0:LANES] = w_in_ref[:, col_va:col_va + LANES]
        wv_scr[:, LANES:2 * LANES] = w_in_ref[:, col_vb:col_vb + LANES]

    n_lat_chunks = seq // ROW_CHUNK
    if latent:
        n_past = cav_ref.shape[2]
        past_rows = pl.ds(seq, n_past)
        _store_kt_variants(ka_scr, n_lat_chunks, cakt_ref[0])
        _store_kt_variants(kb_scr, n_lat_chunks, cbkt_ref[0])
        _store_v_variants(va_scr, past_rows, cav_ref[0].T)
        _store_v_variants(vb_scr, past_rows, cbv_ref[0].T)

    knt = norms_ref[0:LANES, :]
    qn = norms_ref[LANES:LANES + 1, 0:LANES]

    def proj(i, carry):
        rows = _chunk_rows(i)
        hh = _modulated(x_ref, mod_ref, mod_row, rows)
        if latent:
            cos = cos_ref[rows, :]
            sin = sin_ref[rows, :]
            rot = lambda a: _rope(a, cos, sin)
            rot_t = lambda a: _rope_t(a, cost_ref[i], sint_ref[i])
        else:
            rot = rot_t = lambda a: a

        acc = _dot(hh, w_in_ref[:, 0:512])
        lo_lanes = _lane_iota(ROW_CHUNK) < HEAD_DIM
        for j in range(4):
            a = acc[:, LANES * j:LANES * (j + 1)]
            sq = a * a
            first = jnp.sum(jnp.where(lo_lanes, sq, 0.0), axis=1, keepdims=True)
            second = jnp.sum(jnp.where(lo_lanes, 0.0, sq), axis=1, keepdims=True)
            ms = jnp.where(lo_lanes, first, second) * (1.0 / HEAD_DIM)
            a = rot(a * lax.rsqrt(ms + EPS) * qn)
            qa_scr[rows, LANES * j:LANES * (j + 1)] = (a * Q_SCALE).astype(BF16)
        acc = _dot(hh, w_in_ref[:, 1280:1792])
        for j in range(4):
            a = rot(acc[:, LANES * j:LANES * (j + 1)])
            qb_scr[rows, LANES * j:LANES * (j + 1)] = (a * Q_SCALE).astype(BF16)
        g_scr[rows, 0:512] = _silu(_dot(hh, w_in_ref[:, 768:1280]))
        g_scr[rows, 512:1024] = _silu(_dot(hh, w_in_ref[:, 2048:2560]))
        v = _dot(hh, wv_scr[...])
        _store_v_variants(va_scr, rows, v[:, 0:LANES])
        _store_v_variants(vb_scr, rows, v[:, LANES:2 * LANES])

        kt = _dot_nt(wkt_scr[0:2 * LANES, :], hh)
        heads = [kt[HEAD_DIM * h:HEAD_DIM * (h + 1), :] for h in range(2)]
        kat = jnp.concatenate([blk * lax.rsqrt(jnp.mean(blk * blk, axis=0, keepdims=True) + EPS) for blk in heads],
                              axis=0) * knt
        kbt = kt[LANES:2 * LANES, :]
        if not latent:
            vt = v.T
            nakt_ref[i] = kat
            nbkt_ref[i] = kbt
            navt_ref[i] = vt[0:LANES, :]
            nbvt_ref[i] = vt[LANES:2 * LANES, :]
        _store_kt_variants(ka_scr, i, rot_t(kat))
        _store_kt_variants(kb_scr, i, rot_t(kbt))
        return carry

    lax.fori_loop(0, n_rows // ROW_CHUNK, proj, 0, unroll=2)

    sinks = [sink_ref[h] * LOG2E for h in range(8)]
    ck = ROW_CHUNK
    bk = kb_scr.shape[-1]
    win = ROW_CHUNK + 2 * WINDOW
    n_items = 8

    def block_stages(i):
        rows = _chunk_rows(i)
        if latent:
            a_chunks = list(range(n_lat_chunks + n_past // ck))
            a_keys = pl.ds(0, seq + n_past)
            w0 = jnp.clip(i * (ck // bk) - WINDOW // bk, 0, (seq - win) // bk)
            win_rows = pl.ds(pl.multiple_of(w0 * bk, bk), win)
            dist = (lax.broadcasted_iota(jnp.int32, (ROW_CHUNK, ck), 1)
                    - lax.broadcasted_iota(jnp.int32, (ROW_CHUNK, ck), 0))
            for c in range(win // ck):
                off = w0 * bk + c * ck - i * ck
                bias_scr[c] = jnp.where(jnp.abs(dist + off) <= WINDOW, 0.0, NEG_INF).astype(F32)
            b_first = [w0 + c * (ck // bk) for c in range(win // ck)] + [seq // bk]
            n_biased = win // ck
            b_cols = win + n_past
        else:
            a_chunks = [i]
            a_keys = rows
            b_first = [i * (ck // bk)]
            n_biased = 0
            b_cols = ck
        a_cols = len(a_chunks) * ck

        def qk(t, slot):
            p, branch = divmod(t, 2)
            cols = slice(LANES * p, LANES * (p + 1))
            kvh = p // 2
            q = (qb_scr if branch else qa_scr)[rows, cols]
            for par in (0, 1):
                var = 2 * kvh + par
                if branch:
                    tiles = [jnp.concatenate([kb_scr[var, first + d] for d in range(ck // bk)], axis=1)
                             for first in b_first]
                else:
                    tiles = [ka_scr[var, chunk] for chunk in a_chunks]
                for c, kt in enumerate(tiles):
                    s = _dot(q, kt)
                    if branch and c < n_biased:
                        s = s + bias_scr[c]
                    s_scr[slot, par, :, c * ck:(c + 1) * ck] = s

        def softmax(t, slot):
            p, branch = divmod(t, 2)
            n_cols = b_cols if branch else a_cols
            rb = _softmax_rows(n_cols)
            for par in (0, 1):
                for r in range(ROW_CHUNK // rb):
                    sub = slice(r * rb, (r + 1) * rb)
                    s = s_scr[slot, par, sub, 0:n_cols]
                    m = jnp.max(s, axis=1, keepdims=True)
                    if branch:
                        sink = sinks[2 * p + par]
                        m = jnp.maximum(m, sink)
                        es_scr[slot, sub, HEAD_DIM * par:HEAD_DIM * (par + 1)] = jnp.broadcast_to(
                            jnp.exp2(sink - m), (rb, HEAD_DIM))
                    p_scr[slot, par, sub, 0:n_cols] = jnp.exp2((s - m).astype(BF16))

        def pv(t, slot):
            p, branch = divmod(t, 2)
            kvh = p // 2
            v_scr = vb_scr if branch else va_scr
            accs = []
            for par in (0, 1):
                var = 2 * kvh + par
                if latent and branch:
                    n_loc = win
                    accs.append(_dot(p_scr[slot, par, :, 0:n_loc], v_scr[var, win_rows, :])
                                + _dot(p_scr[slot, par, :, n_loc:b_cols], v_scr[var, past_rows, :]))
                else:
                    accs.append(_dot(p_scr[slot, par, :, 0:a_cols], v_scr[var, a_keys, :]))
            lo = _lane_iota(ROW_CHUNK) < HEAD_DIM
            denom = pltpu.roll(jnp.where(lo, accs[1], accs[0]), HEAD_DIM, 1)
            if branch:
                denom = denom + es_scr[slot]
            o = jnp.where(lo, accs[0], accs[1]) / denom
            ocols = slice(512 * branch + LANES * p, 512 * branch + LANES * (p + 1))
            attn_scr[rows, ocols] = (o * g_scr[rows, ocols]).astype(BF16)

        return qk, softmax, pv

    _attend_blocks(block_stages, n_rows // ROW_CHUNK, n_items, unrolled=not latent)

    if not latent:
        weights.load_out_proj()
    _out_proj_norm(x_ref, mod_ref, mod_row, attn_scr, w_out_ref, lng_ref, lnb_ref, layer, y_ref, n_rows, alpha)
    if not latent:
        weights.finish()


def _odd_kernel(latent, layer, n_rows, seq, alpha, lam_init, *refs):
    if latent:
        (x_ref, mod_ref, w_in_ref, w_out_ref, lq1_ref, lk1_ref, lq2_ref, lk2_ref, sub_ref, lng_ref, lnb_ref,
         cos_ref, sin_ref, cck_hbm, ccv_hbm,
         y_ref,
         attn_scr, q_scr, k_scr, v_scr, g_scr, s_scr, p_scr, past_stage, past_sems) = refs
    else:
        (x_ref, mod_ref, w_in_hbm, w_out_hbm, lq1_ref, lk1_ref, lq2_ref, lk2_ref, sub_ref, lng_ref, lnb_ref,
         y_ref, nck_hbm, ncv_hbm, w_in_bf_hbm, w_out_bf_hbm,
         attn_scr, q_scr, k_scr, v_scr, g_scr, s_scr, p_scr, kv_stage, kv_sems,
         w_in_ref, w_out_ref, w_stage, w_sems, w_out_sems) = refs

    step = pl.program_id(0)
    if not latent:
        weights = _ContextWeights(step, (w_in_hbm, w_in_ref, w_in_bf_hbm), (w_out_hbm, w_out_ref, w_out_bf_hbm),
                                  w_stage, w_sems, w_out_sems)
        weights.load_in_proj()
    mod_row = step + 1 if latent else 0

    n_heads = D_MODEL // LANES
    n_blocks = n_rows // ROW_CHUNK
    lo = _lane_iota(ROW_CHUNK) < HEAD_DIM

    def kv_out_copies(blk):
        elem = step * n_blocks + blk
        return [pltpu.make_async_copy(kv_stage.at[blk, t, :, pl.ds(LANES * h, LANES)],
                                      out.at[elem, 0, :, h, :], kv_sems.at[blk, t])
                for t, out in enumerate((nck_hbm, ncv_hbm)) for h in range(n_heads)]

    def store_k(rows, h, a):
        cols = slice(LANES * h, LANES * (h + 1))
        zero = jnp.zeros_like(a)
        k_scr[0, rows, cols] = jnp.where(lo, a, zero).astype(BF16)
        k_scr[1, rows, cols] = jnp.where(lo, zero, a).astype(BF16)

    if latent:
        n_past = cck_hbm.shape[2]
        past = pl.ds(seq, n_past)
        past_copies = [pltpu.make_async_copy(cache.at[step, layer // 2, :, h, :],
                                             past_stage.at[t, :, pl.ds(LANES * h, LANES)], past_sems.at[t])
                       for t, cache in enumerate((cck_hbm, ccv_hbm)) for h in range(n_heads)]
        for copy in past_copies:
            copy.start()

    def proj(i, carry):
        rows = _chunk_rows(i)
        hh = _modulated(x_ref, mod_ref, mod_row, rows)
        if latent:
            cos = cos_ref[rows, :]
            sin = sin_ref[rows, :]
            rot = lambda a: _rope(a, cos, sin)
        else:
            rot = lambda a: a
        for half in range(2):
            acc = _dot(hh, w_in_ref[:, 512 * half:512 * (half + 1)])
            for j in range(4):
                a = rot(acc[:, LANES * j:LANES * (j + 1)])
                cols = slice(512 * half + LANES * j, 512 * half + LANES * (j + 1))
                q_scr[rows, cols] = (a * Q_SCALE).astype(BF16)
        for half in range(2):
            acc = _dot(hh, w_in_ref[:, 1024 + 512 * half:1024 + 512 * (half + 1)])
            if not latent:
                kv_stage[i, 0, :, 512 * half:512 * (half + 1)] = acc
            for j in range(4):
                store_k(rows, 4 * half + j, rot(acc[:, LANES * j:LANES * (j + 1)]))
        for half in range(2):
            acc = _dot(hh, w_in_ref[:, 2048 + 512 * half:2048 + 512 * (half + 1)])
            if not latent:
                kv_stage[i, 1, :, 512 * half:512 * (half + 1)] = acc
            v_scr[rows, 512 * half:512 * (half + 1)] = acc.astype(BF16)
        if not latent:
            for copy in kv_out_copies(i):
                copy.start()
        for half in range(2):
            acc = _dot(hh, w_in_ref[:, 3072 + 512 * half:3072 + 512 * (half + 1)])
            g_scr[rows, 512 * half:512 * (half + 1)] = _silu(acc)
        return carry

    if latent:
        lax.fori_loop(0, n_blocks, proj, 0, unroll=2)
    else:
        for blk in range(n_blocks):
            proj(blk, 0)

    if latent:
        for copy in past_copies:
            copy.wait()
        for h in range(n_heads):
            store_k(past, h, past_stage[0, :, LANES * h:LANES * (h + 1)])
        v_scr[past, :] = past_stage[1].astype(BF16)

    lam = (jnp.exp(jnp.sum(lq1_ref[...] * lk1_ref[...], axis=1, keepdims=True))
           - jnp.exp(jnp.sum(lq2_ref[...] * lk2_ref[...], axis=1, keepdims=True)) + lam_init)
    sub = sub_ref[...] * (1.0 - lam_init)
    n_keys = seq + n_past if latent else ROW_CHUNK
    rb = _softmax_rows(n_keys)
    ones = jnp.ones((n_keys, LANES), BF16)

    def block_stages(i):
        rows = _chunk_rows(i)
        keys = pl.ds(0, n_keys) if latent else rows

        def qk(h, slot):
            cols = slice(LANES * h, LANES * (h + 1))
            q = q_scr[rows, cols]
            for m in (0, 1):
                s_scr[slot, m] = _dot_nt(q, k_scr[m, keys, cols])

        def softmax(h, slot):
            for m in (0, 1):
                for r in range(ROW_CHUNK // rb):
                    sub_rows = slice(r * rb, (r + 1) * rb)
                    s = s_scr[slot, m, sub_rows, :]
                    top = jnp.max(s, axis=1, keepdims=True)
                    p_scr[slot, m, sub_rows, :] = jnp.exp2((s - top).astype(BF16))

        def pv(h, slot):
            cols = slice(LANES * h, LANES * (h + 1))
            v_ext = jnp.concatenate([v_scr[keys, cols], ones], axis=1)
            maps = []
            for m in (0, 1):
                acc = _dot(p_scr[slot, m], v_ext)
                maps.append(acc[:, 0:LANES] / acc[:, LANES:2 * LANES])
            o = maps[0] - lam * maps[1]
            ms = jnp.mean(o * o, axis=1, keepdims=True)
            o = o * lax.rsqrt(ms + EPS) * sub
            attn_scr[rows, cols] = (o * g_scr[rows, cols]).astype(BF16)

        return qk, softmax, pv

    _attend_blocks(block_stages, n_blocks, n_heads, unrolled=not latent)

    if not latent:
        weights.load_out_proj()
    _out_proj_norm(x_ref, mod_ref, mod_row, attn_scr, w_out_ref, lng_ref, lnb_ref, layer, y_ref, n_rows, alpha)

    if not latent:
        for blk in range(n_blocks):
            for copy in kv_out_copies(blk):
                copy.wait()
        weights.finish()


MOD_SLAB_ROWS = 128
MOD_COL_BLOCK = 1024
MOD_SLOTS = 4


def _mod_kernel(n_cond, cv_ref, w_hbm, b_ref, o_ref, sb_scr, ring, acc_scr, sems):
    depth, n_in, n_out = w_hbm.shape
    sublanes = 8
    slabs_per_layer = n_in // MOD_SLAB_ROWS
    slabs = [(l, rs) for l in range(depth) for rs in range(slabs_per_layer)]

    def slab_copy(n):
        l, rs = slabs[n]
        return pltpu.make_async_copy(w_hbm.at[l, pl.ds(rs * MOD_SLAB_ROWS, MOD_SLAB_ROWS), :],
                                     ring.at[n % MOD_SLOTS], sems.at[n % MOD_SLOTS])

    for n in range(min(MOD_SLOTS, len(slabs))):
        slab_copy(n).start()
    s_t = _silu(cv_ref[...]).T
    for r in range(n_cond):
        sb_scr[r] = jnp.broadcast_to(s_t[:, r:r + 1], (n_in, LANES))

    for n, (l, rs) in enumerate(slabs):
        slab_copy(n).wait()
        for cb in range(n_out // MOD_COL_BLOCK):
            cols = pl.ds(cb * MOD_COL_BLOCK, MOD_COL_BLOCK)
            if rs == 0:
                accs = (jnp.zeros((sublanes, MOD_COL_BLOCK), F32),) * n_cond
            else:
                accs = tuple(acc_scr[r, :, cols] for r in range(n_cond))

            def body(kb, accs, n=n, rs=rs, cols=cols):
                w = ring[n % MOD_SLOTS, pl.ds(pl.multiple_of(kb * sublanes, sublanes), sublanes), cols]
                s_rows = pl.ds(pl.multiple_of(rs * MOD_SLAB_ROWS + kb * sublanes, sublanes), sublanes)
                return tuple(acc + w * jnp.tile(sb_scr[r, s_rows, :], (1, MOD_COL_BLOCK // LANES))
                             for r, acc in enumerate(accs))

            accs = lax.fori_loop(0, MOD_SLAB_ROWS // sublanes, body, accs, unroll=8)
            if rs < slabs_per_layer - 1:
                for r in range(n_cond):
                    acc_scr[r, :, cols] = accs[r]
            else:
                rows = [jnp.sum(acc, axis=0, keepdims=True) + b_ref[l:l + 1, cols] for acc in accs]
                o_ref[l, :, cols] = jnp.concatenate(rows + [jnp.zeros((8 - n_cond, MOD_COL_BLOCK), F32)], axis=0)
        if n + MOD_SLOTS < len(slabs):
            slab_copy(n + MOD_SLOTS).start()


def _full(shape, **kw):
    zeros = (0,) * len(shape)
    return pl.BlockSpec(shape, lambda i: zeros, **kw)


def _weight_specs(latent, w_in, w_out):
    if latent:
        single = pl.Buffered(1)
        return [_full(w_in.shape, pipeline_mode=single), _full(w_out.shape, pipeline_mode=single)]
    return [pl.BlockSpec(memory_space=pl.ANY), pl.BlockSpec(memory_space=pl.ANY)]


def _weight_scratch(w_in, w_out):
    assert w_in.shape[0] % W_SLAB_ROWS == 0 and w_out.shape[0] % W_SLAB_ROWS == 0
    n_slabs = max(w_in.shape[0], w_out.shape[0]) // W_SLAB_ROWS
    return [pltpu.VMEM(w_in.shape, BF16), pltpu.VMEM(w_out.shape, BF16),
            pltpu.VMEM((n_slabs, W_SLAB_ROWS, max(w_in.shape[1], w_out.shape[1])), F32),
            pltpu.SemaphoreType.DMA((n_slabs,)), pltpu.SemaphoreType.DMA((2,))]


def _rope_tables(seq):
    t = np.arange(seq)
    n_freq = HEAD_DIM // 4
    freqs = ROPE_THETA ** (-np.arange(n_freq, dtype=np.float64) / n_freq)
    ang_row = (t // GRID_W)[:, None] * freqs
    ang_col = (t % GRID_W)[:, None] * freqs
    ang = np.concatenate([ang_row, ang_row, ang_col, ang_col], axis=1)
    sign = np.concatenate([-np.ones(n_freq), np.ones(n_freq)] * 2)[None, :]
    cos = np.tile(np.cos(ang), (1, 2)).astype(np.float32)
    sin = np.tile(np.sin(ang) * sign, (1, 2)).astype(np.float32)
    chunked_t = lambda a: a.reshape(seq // ROW_CHUNK, ROW_CHUNK, LANES).transpose(0, 2, 1)
    return jnp.asarray(cos), jnp.asarray(sin), jnp.asarray(chunked_t(cos)), jnp.asarray(chunked_t(sin))


def _modulation(c, c_ctx, w_mod, b_mod):
    depth = w_mod.shape[0]
    n_cond = 1 + c.shape[0]
    cv = jnp.concatenate([c_ctx[None, :], c, jnp.zeros((8 - n_cond, D_MODEL), F32)], axis=0)
    assert D_MODEL % MOD_SLAB_ROWS == 0 and (3 * D_MODEL) % MOD_COL_BLOCK == 0
    return pl.pallas_call(
        functools.partial(_mod_kernel, n_cond),
        grid=(1,),
        in_specs=[_full(cv.shape), pl.BlockSpec(memory_space=pl.ANY), _full(b_mod.shape)],
        out_specs=_full((depth, 8, 3 * D_MODEL)),
        out_shape=jax.ShapeDtypeStruct((depth, 8, 3 * D_MODEL), F32),
        scratch_shapes=[pltpu.VMEM((n_cond, D_MODEL, LANES), F32),
                        pltpu.VMEM((MOD_SLOTS, MOD_SLAB_ROWS, 3 * D_MODEL), F32),
                        pltpu.VMEM((n_cond, 8, 3 * D_MODEL), F32),
                        pltpu.SemaphoreType.DMA((MOD_SLOTS,))],
        compiler_params=pltpu.CompilerParams(dimension_semantics=("arbitrary",)),
        name="adaln_modulation",
    )(cv, w_mod, b_mod)


def _even_layer(x, mod, layer, w_in, w_out, q_norm, k_norm, sink, ln_g, ln_b, latent, seq, n_rows, alpha, extras=()):
    total = x.shape[0]
    grid = (total // n_rows,)
    single = pl.Buffered(1)
    norms = jnp.concatenate([jnp.broadcast_to(jnp.tile(k_norm, 2)[:, None], (LANES, ROW_CHUNK)),
                             jnp.broadcast_to(jnp.tile(q_norm, 2 * ROW_CHUNK // LANES)[None, :], (8, ROW_CHUNK))], axis=0)

    row_blk = lambda width: pl.BlockSpec((n_rows, width), lambda i: (i, 0))
    in_specs = [row_blk(D_MODEL),
                pl.BlockSpec((1, 8, 3 * D_MODEL), lambda i: (layer, 0, 0)),
                *_weight_specs(latent, w_in, w_out),
                _full(norms.shape),
                pl.BlockSpec(memory_space=pltpu.SMEM),
                _full(ln_g.shape), _full(ln_b.shape)]
    args = [x, mod, w_in, w_out, norms, sink, ln_g, ln_b]
    y_shape = jax.ShapeDtypeStruct((total, D_MODEL), F32)
    n_blocks = n_rows // ROW_CHUNK
    if latent:
        cos, sin, cos_t, sin_t, cakt, cav, cbkt, cbv = extras
        n_past = cav.shape[2]
        in_specs += [_full(cos.shape, pipeline_mode=single), _full(sin.shape, pipeline_mode=single),
                     _full(cos_t.shape, pipeline_mode=single), _full(sin_t.shape, pipeline_mode=single)]
        in_specs += [pl.BlockSpec((1, LANES, n_past), lambda i: (i, 0, 0))] * 4
        args += [cos, sin, cos_t, sin_t, cakt, cav, cbkt, cbv]
        out_specs = row_blk(D_MODEL)
        out_shape = y_shape
        n_keys = seq + n_past
    else:
        kv_blk = pl.BlockSpec((n_blocks, LANES, ROW_CHUNK), lambda i: (i, 0, 0))
        hbm = pl.BlockSpec(memory_space=pl.ANY)
        out_specs = [row_blk(D_MODEL)] + [kv_blk] * 4 + [hbm, hbm]
        out_shape = ([y_shape] + [jax.ShapeDtypeStruct((total // seq, LANES, seq), F32)] * 4
                     + [jax.ShapeDtypeStruct(w_in.shape, BF16), jax.ShapeDtypeStruct(w_out.shape, BF16)])
        n_keys = n_rows
    n_kchunks = n_keys // ROW_CHUNK
    n_cols = n_keys if latent else ROW_CHUNK
    scratch = [pltpu.VMEM((n_rows, D_MODEL), BF16),
               pltpu.VMEM((n_rows, 512), BF16), pltpu.VMEM((n_rows, 512), BF16),
               pltpu.VMEM((4, n_kchunks, LANES, ROW_CHUNK), BF16), pltpu.VMEM((4, n_keys, LANES), BF16),
               pltpu.VMEM((4, n_keys // WINDOW, LANES, WINDOW), BF16), pltpu.VMEM((4, n_keys, LANES), BF16),
               pltpu.VMEM((n_rows, D_MODEL), F32),
               pltpu.VMEM((2, 2, ROW_CHUNK, n_cols), F32),
               pltpu.VMEM((2, 2, ROW_CHUNK, n_cols), BF16),
               pltpu.VMEM((2, ROW_CHUNK, LANES), F32),
               pltpu.VMEM((2 * LANES, D_MODEL), BF16),
               pltpu.VMEM((D_MODEL, 2 * LANES), BF16)]
    if latent:
        scratch.append(pltpu.VMEM((1 + 2 * WINDOW // ROW_CHUNK, ROW_CHUNK, ROW_CHUNK), F32))
    else:
        scratch += _weight_scratch(w_in, w_out)
    return pl.pallas_call(
        functools.partial(_even_kernel, latent, layer, n_rows, seq, alpha),
        grid=grid, in_specs=in_specs, out_specs=out_specs, out_shape=out_shape,
        scratch_shapes=scratch,
        compiler_params=pltpu.CompilerParams(dimension_semantics=("arbitrary",), vmem_limit_bytes=VMEM_LIMIT),
        name="even_layer_latent" if latent else "even_layer_context",
    )(*args)


def _odd_layer(x, mod, layer, w_in, w_out, lams, sub, ln_g, ln_b, latent, seq, n_rows, alpha, lam_init, extras=()):
    total = x.shape[0]
    grid = (total // n_rows,)
    row_blk = lambda width: pl.BlockSpec((n_rows, width), lambda i: (i, 0))
    single = pl.Buffered(1)
    in_specs = [row_blk(D_MODEL),
                pl.BlockSpec((1, 8, 3 * D_MODEL), lambda i: (layer, 0, 0)),
                *_weight_specs(latent, w_in, w_out),
                _full((1, HEAD_DIM)), _full((1, HEAD_DIM)), _full((1, HEAD_DIM)), _full((1, HEAD_DIM)),
                _full((1, LANES)),
                _full(ln_g.shape), _full(ln_b.shape)]
    args = [x, mod, w_in, w_out, *lams, sub, ln_g, ln_b]
    y_shape = jax.ShapeDtypeStruct((total, D_MODEL), F32)
    n_heads = D_MODEL // LANES
    n_blocks = n_rows // ROW_CHUNK
    if latent:
        cos, sin, cck, ccv = extras
        n_past = cck.shape[2]
        in_specs += [_full(cos.shape, pipeline_mode=single), _full(sin.shape, pipeline_mode=single)]
        in_specs += [pl.BlockSpec(memory_space=pl.ANY)] * 2
        args += [cos, sin, cck, ccv]
        out_specs = row_blk(D_MODEL)
        out_shape = y_shape
        n_keys = seq + n_past
    else:
        hbm = pl.BlockSpec(memory_space=pl.ANY)
        out_specs = [row_blk(D_MODEL), hbm, hbm, hbm, hbm]
        out_shape = ([y_shape] + [jax.ShapeDtypeStruct((total // seq, 1, seq, n_heads, LANES), F32)] * 2
                     + [jax.ShapeDtypeStruct(w_in.shape, BF16), jax.ShapeDtypeStruct(w_out.shape, BF16)])
        n_keys = n_rows
    n_cols = n_keys if latent else ROW_CHUNK
    scratch = [pltpu.VMEM((n_rows, D_MODEL), BF16),
               pltpu.VMEM((n_rows, D_MODEL), BF16),
               pltpu.VMEM((2, n_keys, D_MODEL), BF16),
               pltpu.VMEM((n_keys, D_MODEL), BF16),
               pltpu.VMEM((n_rows, D_MODEL), F32),
               pltpu.VMEM((2, 2, ROW_CHUNK, n_cols), F32),
               pltpu.VMEM((2, 2, ROW_CHUNK, n_cols), BF16)]
    if latent:
        scratch += [pltpu.VMEM((2, n_past, D_MODEL), F32), pltpu.SemaphoreType.DMA((2,))]
    else:
        scratch += [pltpu.VMEM((n_blocks, 2, ROW_CHUNK, D_MODEL), F32),
                    pltpu.SemaphoreType.DMA((n_blocks, 2))]
        scratch += _weight_scratch(w_in, w_out)
    return pl.pallas_call(
        functools.partial(_odd_kernel, latent, layer, n_rows, seq, alpha, lam_init),
        grid=grid, in_specs=in_specs, out_specs=out_specs, out_shape=out_shape,
        scratch_shapes=scratch,
        compiler_params=pltpu.CompilerParams(dimension_semantics=("arbitrary",), vmem_limit_bytes=VMEM_LIMIT),
        name="odd_layer_latent" if latent else "odd_layer_context",
    )(*args)


def kernel(x_prompt, x_sample, cache_a_k, cache_a_v, cache_b_k, cache_b_v, cache_c_k, cache_c_v, c, c_ctx,
           w_mod, b_mod, ln_g, ln_b, w_in_even, w_out_even, q_norm_a, k_norm_a, sink_b, w_in_odd, w_out_odd,
           lambda_q1, lambda_k1, lambda_q2, lambda_k2, subln_c):
    depth = w_mod.shape[0]
    batch, seq, _ = x_prompt.shape
    dec_batch, dec_seq, _ = x_sample.shape
    n_past = cache_a_k.shape[2]
    alpha = (2 * depth) ** 0.25
    assert seq == ROW_CHUNK and n_past % ROW_CHUNK == 0 and dec_seq % ROW_CHUNK == 0

    mod = _modulation(c, c_ctx, w_mod, b_mod)
    cos, sin, cos_t, sin_t = _rope_tables(dec_seq)

    bf16_weights = {}

    def run(x, latent, n_batch, s, rows_even, rows_odd):
        kv = {"a_k": [], "a_v": [], "b_k": [], "b_v": [], "c_k": [], "c_v": []}
        for l in range(depth):
            if l % 2 == 0:
                e = l // 2
                extras = ()
                if latent:
                    k_t = lambda t: t[:, e].transpose(0, 2, 3, 1).reshape(n_batch, LANES, n_past)
                    extras = (cos, sin, cos_t, sin_t,
                              k_t(cache_a_k), k_t(cache_a_v), k_t(cache_b_k), k_t(cache_b_v))
                w_in, w_out = bf16_weights[l] if latent else (w_in_even[e], w_out_even[e])
                res = _even_layer(x, mod, l, w_in, w_out, q_norm_a[e], k_norm_a[e],
                                  sink_b[e], ln_g, ln_b, latent, s, rows_even, alpha, extras)
                if latent:
                    x = res
                else:
                    x = res[0]
                    bf16_weights[l] = res[5:7]
                    for name, t in zip(("a_k", "a_v", "b_k", "b_v"), res[1:5]):
                        kv[name].append(t.reshape(n_batch, 2, HEAD_DIM, s).transpose(0, 3, 1, 2))
            else:
                o = l // 2
                lam_init = 0.8 - 0.6 * math.exp(-0.3 * l)
                extras = (cos, sin, cache_c_k, cache_c_v) if latent else ()
                lams = [t[o][None, :] for t in (lambda_q1, lambda_k1, lambda_q2, lambda_k2)]
                w_in, w_out = bf16_weights[l] if latent else (w_in_odd[o], w_out_odd[o])
                res = _odd_layer(x, mod, l, w_in, w_out, lams,
                                 subln_c[o][None, :], ln_g, ln_b, latent, s, rows_odd, alpha, lam_init, extras)
                if latent:
                    x = res
                else:
                    x = res[0]
                    bf16_weights[l] = res[3:5]
                    kv["c_k"].append(res[1][:, 0])
                    kv["c_v"].append(res[2][:, 0])
        return x, kv

    y_ctx, kv = run(x_prompt.reshape(batch * seq, D_MODEL), False, batch, seq, 1024, 512)
    y_lat, _ = run(x_sample.reshape(dec_batch * dec_seq, D_MODEL), True, dec_batch, dec_seq, dec_seq, dec_seq)

    stack = lambda name: jnp.stack(kv[name], axis=1)
    return (y_ctx.reshape(batch, seq, D_MODEL), y_lat.reshape(dec_batch, dec_seq, D_MODEL),
            stack("a_k"), stack("a_v"), stack("b_k"), stack("b_v"), stack("c_k"), stack("c_v"))
```

```python
import functools
import math

import jax
import jax.numpy as jnp
import numpy as np
from jax import lax
from jax.experimental import pallas as pl
from jax.experimental.pallas import tpu as pltpu

F32 = jnp.float32
BF16 = jnp.bfloat16

D_MODEL = 1024
HEAD_DIM = 64
GRID_W = 64
WINDOW = 128
ROPE_THETA = 10000.0
EPS = 1e-6
NEG_INF = -1e30
LOG2E = 1.4426950408889634
Q_SCALE = HEAD_DIM ** -0.5 * LOG2E
LANES = 128
ROW_CHUNK = 256
SOFTMAX_VREGS = 40
VMEM_LIMIT = 60000 * 1024
W_SLAB_ROWS = 128


def _silu(x):
    return x / (1.0 + jnp.exp(-x))


def _dot(a, b):
    return jnp.dot(a, b, preferred_element_type=F32)


def _dot_nt(a, b):
    return lax.dot_general(a, b, (((1,), (1,)), ((), ())), preferred_element_type=F32)


def _lane_iota(rows):
    return lax.broadcasted_iota(jnp.int32, (rows, LANES), 1)


def _chunk_rows(i):
    if isinstance(i, int):
        return pl.ds(i * ROW_CHUNK, ROW_CHUNK)
    return pl.ds(pl.multiple_of(i * ROW_CHUNK, ROW_CHUNK), ROW_CHUNK)


def _softmax_rows(n_cols):
    rows = 8
    while rows * 2 * n_cols <= SOFTMAX_VREGS * 1024 and rows * 2 <= ROW_CHUNK:
        rows *= 2
    return rows


def _rope(a, cos, sin_signed):
    lane = _lane_iota(a.shape[0])
    fwd = pltpu.roll(a, LANES - 16, 1)
    bwd = pltpu.roll(a, 16, 1)
    partner = jnp.where((lane & 16) == 0, fwd, bwd)
    return a * cos + partner * sin_signed


def _rope_t(a, cos_t, sin_t):
    blocks = [a[16 * b:16 * (b + 1), :] for b in range(a.shape[0] // 16)]
    partner = jnp.concatenate([blocks[b ^ 1] for b in range(len(blocks))], axis=0)
    return a * cos_t + partner * sin_t


def _store_kt_variants(scr, chunk, kt):
    width = scr.shape[-1]
    per_block = kt.shape[1] // width
    zero = jnp.zeros((HEAD_DIM, kt.shape[1]), F32)
    for j in range(2):
        kj = kt[HEAD_DIM * j:HEAD_DIM * (j + 1), :]
        for par, full in enumerate((jnp.concatenate([kj, zero], axis=0), jnp.concatenate([zero, kj], axis=0))):
            full = full.astype(BF16)
            for c in range(per_block):
                scr[2 * j + par, chunk * per_block + c] = full[:, width * c:width * (c + 1)]


def _store_v_variants(scr, rows, a):
    lane = _lane_iota(a.shape[0])
    lo = lane < HEAD_DIM
    swapped = pltpu.roll(a, HEAD_DIM, 1)
    one = jnp.ones_like(a)
    scr[0, rows, :] = jnp.where(lo, a, one).astype(BF16)
    scr[1, rows, :] = jnp.where(lo, one, swapped).astype(BF16)
    scr[2, rows, :] = jnp.where(lo, swapped, one).astype(BF16)
    scr[3, rows, :] = jnp.where(lo, one, a).astype(BF16)


def _layer_norm_rows(z, g, b):
    mu = jnp.mean(z, axis=-1, keepdims=True)
    zc = z - mu
    var = jnp.mean(zc * zc, axis=-1, keepdims=True)
    return zc * lax.rsqrt(var + EPS) * g + b


def _modulated(x_ref, mod_ref, mod_row, rows):
    shift = mod_ref[0, pl.ds(mod_row, 1), 0:D_MODEL]
    scale = mod_ref[0, pl.ds(mod_row, 1), D_MODEL:2 * D_MODEL]
    return (x_ref[rows, :] * (1.0 + scale) + shift).astype(BF16)


def _out_proj_norm(x_ref, mod_ref, mod_row, attn_scr, w_out_ref, lng_ref, lnb_ref, layer, y_ref, n_rows, alpha,
                   y_dma=None):
    gate = mod_ref[0, pl.ds(mod_row, 1), 2 * D_MODEL:3 * D_MODEL]
    g = lng_ref[layer:layer + 1, :]
    b = lnb_ref[layer:layer + 1, :]

    def chunk(i):
        rows = _chunk_rows(i)
        out = _dot(attn_scr[rows, :], w_out_ref[...])
        z = alpha * x_ref[rows, :] + gate * out
        return _layer_norm_rows(z, g, b)

    if y_dma is None:
        def body(i, carry):
            y_ref[_chunk_rows(i), :] = chunk(i)
            return carry

        lax.fori_loop(0, n_rows // ROW_CHUNK, body, 0, unroll=True)
    else:
        stage, sems, first_row = y_dma
        copies = []
        for i in range(n_rows // ROW_CHUNK):
            stage[i] = chunk(i)
            dst_rows = pl.ds(pl.multiple_of(first_row + i * ROW_CHUNK, ROW_CHUNK), ROW_CHUNK)
            copies.append(pltpu.make_async_copy(stage.at[i], y_ref.at[dst_rows, :], sems.at[i]))
            copies[-1].start()
        for copy in copies:
            copy.wait()


class _ContextWeights:
    def __init__(self, step, w_in, w_out, stage, sems, out_sems):
        self.step, self.w_in, self.w_out, self.stage, self.sems = step, w_in, w_out, stage, sems
        self.out_copies = [pltpu.make_async_copy(w[1], w[2], out_sems.at[n]) for n, w in enumerate((w_in, w_out))]

    def _slab_copies(self, w_hbm):
        n_cols = w_hbm.shape[1]
        return [pltpu.make_async_copy(w_hbm.at[pl.ds(s * W_SLAB_ROWS, W_SLAB_ROWS), :],
                                      self.stage.at[s, :, pl.ds(0, n_cols)], self.sems.at[s])
                for s in range(w_hbm.shape[0] // W_SLAB_ROWS)]

    def _cast(self, w_hbm, w_scr):
        n_cols = w_hbm.shape[1]
        for s, copy in enumerate(self._slab_copies(w_hbm)):
            copy.wait()
            w_scr[pl.ds(s * W_SLAB_ROWS, W_SLAB_ROWS), :] = self.stage[s, :, 0:n_cols].astype(BF16)

    def load_in_proj(self):
        @pl.when(self.step == 0)
        def _():
            for copy in self._slab_copies(self.w_in[0]):
                copy.start()
            self._cast(self.w_in[0], self.w_in[1])
            self.out_copies[0].start()
            for copy in self._slab_copies(self.w_out[0]):
                copy.start()

    def load_out_proj(self):
        @pl.when(self.step == 0)
        def _():
            self._cast(self.w_out[0], self.w_out[1])
            self.out_copies[1].start()

    def finish(self):
        @pl.when(self.step == 0)
        def _():
            for copy in self.out_copies:
                copy.wait()


def _run_pipeline(n_items, stages):
    for u in range(n_items + len(stages) - 1):
        for k, stage in enumerate(stages):
            t = u - k
            if 0 <= t < n_items:
                stage(t, t % 2)


def _attend_blocks(block_stages, n_blocks, n_items, unrolled):
    assert n_items % 2 == 0
    if unrolled:
        per_block = [block_stages(i) for i in range(n_blocks)]
        stages = [lambda g, slot, k=k: per_block[g // n_items][k](g % n_items, slot) for k in range(3)]
        _run_pipeline(n_blocks * n_items, stages)
    else:
        def body(i, carry):
            _run_pipeline(n_items, block_stages(i))
            return carry

        lax.fori_loop(0, n_blocks, body, 0)


def _even_kernel(latent, layer, n_rows, seq, alpha, *refs):
    if latent:
        (x_ref, mod_ref, w_in_ref, w_out_ref, norms_ref, sink_ref, lng_ref, lnb_ref,
         cos_ref, sin_ref, cost_ref, sint_ref, cakt_ref, cav_ref, cbkt_ref, cbv_ref,
         y_ref,
         attn_scr, qa_scr, qb_scr, ka_scr, va_scr, kb_scr, vb_scr, g_scr, s_scr, p_scr, es_scr, wkt_scr, wv_scr,
         bias_scr, y_stage, y_sems) = refs
    else:
        (x_ref, mod_ref, w_in_hbm, w_out_hbm, norms_ref, sink_ref, lng_ref, lnb_ref,
         y_ref, nakt_ref, navt_ref, nbkt_ref, nbvt_ref, w_in_bf_hbm, w_out_bf_hbm,
         attn_scr, qa_scr, qb_scr, ka_scr, va_scr, kb_scr, vb_scr, g_scr, s_scr, p_scr, es_scr, wkt_scr,
         wv_scr, w_in_ref, w_out_ref, w_stage, w_sems, w_out_sems) = refs

    step = pl.program_id(0)
    if not latent:
        weights = _ContextWeights(step, (w_in_hbm, w_in_ref, w_in_bf_hbm), (w_out_hbm, w_out_ref, w_out_bf_hbm),
                                  w_stage, w_sems, w_out_sems)
        weights.load_in_proj()
    mod_row = step + 1 if latent else 0

    col_ka, col_va, col_kb, col_vb = 512, 640, 1792, 1920

    @pl.when(step == 0)
    def _():
        for r, c0 in enumerate((col_ka, col_kb)):
            wkt_scr[LANES * r:LANES * (r + 1), :] = w_in_ref[:, c0:c0 + LANES].T
        wv_scr[:, 0:LANES] = w_in_ref[:, col_va:col_va + LANES]
        wv_scr[:, LANES:2 * LANES] = w_in_ref[:, col_vb:col_vb + LANES]

    n_lat_chunks = seq // ROW_CHUNK
    if latent:
        n_past = cav_ref.shape[2]
        past_rows = pl.ds(seq, n_past)
        _store_kt_variants(ka_scr, n_lat_chunks, cakt_ref[0])
        _store_kt_variants(kb_scr, n_lat_chunks, cbkt_ref[0])
        _store_v_variants(va_scr, past_rows, cav_ref[0].T)
        _store_v_variants(vb_scr, past_rows, cbv_ref[0].T)

    knt = norms_ref[0:LANES, :]
    qn = norms_ref[LANES:LANES + 1, 0:LANES]

    def proj(i, carry):
        rows = _chunk_rows(i)
        hh = _modulated(x_ref, mod_ref, mod_row, rows)
        if latent:
            cos = cos_ref[rows, :]
            sin = sin_ref[rows, :]
            rot = lambda a: _rope(a, cos, sin)
            rot_t = lambda a: _rope_t(a, cost_ref[i], sint_ref[i])
        else:
            rot = rot_t = lambda a: a

        acc = _dot(hh, w_in_ref[:, 0:512])
        lo_lanes = _lane_iota(ROW_CHUNK) < HEAD_DIM
        for j in range(4):
            a = acc[:, LANES * j:LANES * (j + 1)]
            sq = a * a
            first = jnp.sum(jnp.where(lo_lanes, sq, 0.0), axis=1, keepdims=True)
            second = jnp.sum(jnp.where(lo_lanes, 0.0, sq), axis=1, keepdims=True)
            ms = jnp.where(lo_lanes, first, second) * (1.0 / HEAD_DIM)
            a = rot(a * lax.rsqrt(ms + EPS) * qn)
            qa_scr[rows, LANES * j:LANES * (j + 1)] = (a * Q_SCALE).astype(BF16)
        acc = _dot(hh, w_in_ref[:, 1280:1792])
        for j in range(4):
            a = rot(acc[:, LANES * j:LANES * (j + 1)])
            qb_scr[rows, LANES * j:LANES * (j + 1)] = (a * Q_SCALE).astype(BF16)
        g_scr[rows, 0:512] = _silu(_dot(hh, w_in_ref[:, 768:1280]))
        g_scr[rows, 512:1024] = _silu(_dot(hh, w_in_ref[:, 2048:2560]))
        v = _dot(hh, wv_scr[...])
        _store_v_variants(va_scr, rows, v[:, 0:LANES])
        _store_v_variants(vb_scr, rows, v[:, LANES:2 * LANES])

        kt = _dot_nt(wkt_scr[0:2 * LANES, :], hh)
        heads = [kt[HEAD_DIM * h:HEAD_DIM * (h + 1), :] for h in range(2)]
        kat = jnp.concatenate([blk * lax.rsqrt(jnp.mean(blk * blk, axis=0, keepdims=True) + EPS) for blk in heads],
                              axis=0) * knt
        kbt = kt[LANES:2 * LANES, :]
        if not latent:
            vt = v.T
            nakt_ref[i] = kat
            nbkt_ref[i] = kbt
            navt_ref[i] = vt[0:LANES, :]
            nbvt_ref[i] = vt[LANES:2 * LANES, :]
        _store_kt_variants(ka_scr, i, rot_t(kat))
        _store_kt_variants(kb_scr, i, rot_t(kbt))
        return carry

    lax.fori_loop(0, n_rows // ROW_CHUNK, proj, 0, unroll=2)

    sinks = [sink_ref[h] * LOG2E for h in range(8)]
    ck = ROW_CHUNK
    bk = kb_scr.shape[-1]
    win = ROW_CHUNK + 2 * WINDOW
    n_items = 8

    def block_stages(i):
        rows = _chunk_rows(i)
        if latent:
            a_chunks = list(range(n_lat_chunks + n_past // ck))
            a_keys = pl.ds(0, seq + n_past)
            w0 = jnp.clip(i * (ck // bk) - WINDOW // bk, 0, (seq - win) // bk)
            win_rows = pl.ds(pl.multiple_of(w0 * bk, bk), win)
            dist = (lax.broadcasted_iota(jnp.int32, (ROW_CHUNK, ck), 1)
                    - lax.broadcasted_iota(jnp.int32, (ROW_CHUNK, ck), 0))
            for c in range(win // ck):
                off = w0 * bk + c * ck - i * ck
                bias_scr[c] = jnp.where(jnp.abs(dist + off) <= WINDOW, 0.0, NEG_INF).astype(F32)
            b_first = [w0 + c * (ck // bk) for c in range(win // ck)] + [seq // bk]
            n_biased = win // ck
            b_cols = win + n_past
        else:
            a_chunks = [i]
            a_keys = rows
            b_first = [i * (ck // bk)]
            n_biased = 0
            b_cols = ck
        a_cols = len(a_chunks) * ck

        def qk(t, slot):
            p, branch = divmod(t, 2)
            cols = slice(LANES * p, LANES * (p + 1))
            kvh = p // 2
            q = (qb_scr if branch else qa_scr)[rows, cols]
            for par in (0, 1):
                var = 2 * kvh + par
                if branch:
                    tiles = [jnp.concatenate([kb_scr[var, first + d] for d in range(ck // bk)], axis=1)
                             for first in b_first]
                else:
                    tiles = [ka_scr[var, chunk] for chunk in a_chunks]
                for c, kt in enumerate(tiles):
                    s = _dot(q, kt)
                    if branch and c < n_biased:
                        s = s + bias_scr[c]
                    s_scr[slot, par, :, c * ck:(c + 1) * ck] = s

        def softmax(t, slot):
            p, branch = divmod(t, 2)
            n_cols = b_cols if branch else a_cols
            rb = _softmax_rows(n_cols)
            for par in (0, 1):
                for r in range(ROW_CHUNK // rb):
                    sub = slice(r * rb, (r + 1) * rb)
                    s = s_scr[slot, par, sub, 0:n_cols]
                    m = jnp.max(s, axis=1, keepdims=True)
                    if branch:
                        sink = sinks[2 * p + par]
                        m = jnp.maximum(m, sink)
                        es_scr[slot, sub, HEAD_DIM * par:HEAD_DIM * (par + 1)] = jnp.broadcast_to(
                            jnp.exp2(sink - m), (rb, HEAD_DIM))
                    p_scr[slot, par, sub, 0:n_cols] = jnp.exp2((s - m).astype(BF16))

        def pv(t, slot):
            p, branch = divmod(t, 2)
            kvh = p // 2
            v_scr = vb_scr if branch else va_scr
            accs = []
            for par in (0, 1):
                var = 2 * kvh + par
                if latent and branch:
                    n_loc = win
                    accs.append(_dot(p_scr[slot, par, :, 0:n_loc], v_scr[var, win_rows, :])
                                + _dot(p_scr[slot, par, :, n_loc:b_cols], v_scr[var, past_rows, :]))
                else:
                    accs.append(_dot(p_scr[slot, par, :, 0:a_cols], v_scr[var, a_keys, :]))
            lo = _lane_iota(ROW_CHUNK) < HEAD_DIM
            denom = pltpu.roll(jnp.where(lo, accs[1], accs[0]), HEAD_DIM, 1)
            if branch:
                denom = denom + es_scr[slot]
            o = jnp.where(lo, accs[0], accs[1]) / denom
            ocols = slice(512 * branch + LANES * p, 512 * branch + LANES * (p + 1))
            attn_scr[rows, ocols] = (o * g_scr[rows, ocols]).astype(BF16)

        return qk, softmax, pv

    _attend_blocks(block_stages, n_rows // ROW_CHUNK, n_items, unrolled=not latent)

    if not latent:
        weights.load_out_proj()
    y_dma = (y_stage, y_sems, step * n_rows) if latent else None
    _out_proj_norm(x_ref, mod_ref, mod_row, attn_scr, w_out_ref, lng_ref, lnb_ref, layer, y_ref, n_rows, alpha, y_dma)
    if not latent:
        weights.finish()


def _odd_kernel(latent, layer, n_rows, seq, alpha, lam_init, *refs):
    if latent:
        (x_ref, mod_ref, w_in_ref, w_out_ref, lq1_ref, lk1_ref, lq2_ref, lk2_ref, sub_ref, lng_ref, lnb_ref,
         cos_ref, sin_ref, cck_hbm, ccv_hbm,
         y_ref,
         attn_scr, q_scr, k_scr, v_scr, g_scr, s_scr, p_scr, past_stage, past_sems, y_stage, y_sems) = refs
    else:
        (x_ref, mod_ref, w_in_hbm, w_out_hbm, lq1_ref, lk1_ref, lq2_ref, lk2_ref, sub_ref, lng_ref, lnb_ref,
         y_ref, nck_hbm, ncv_hbm, w_in_bf_hbm, w_out_bf_hbm,
         attn_scr, q_scr, k_scr, v_scr, g_scr, s_scr, p_scr, kv_stage, kv_sems,
         w_in_ref, w_out_ref, w_stage, w_sems, w_out_sems) = refs

    step = pl.program_id(0)
    if not latent:
        weights = _ContextWeights(step, (w_in_hbm, w_in_ref, w_in_bf_hbm), (w_out_hbm, w_out_ref, w_out_bf_hbm),
                                  w_stage, w_sems, w_out_sems)
        weights.load_in_proj()
    mod_row = step + 1 if latent else 0

    n_heads = D_MODEL // LANES
    n_blocks = n_rows // ROW_CHUNK
    lo = _lane_iota(ROW_CHUNK) < HEAD_DIM

    def kv_out_copies(blk):
        elem = step * n_blocks + blk
        return [pltpu.make_async_copy(kv_stage.at[blk, t, :, pl.ds(LANES * h, LANES)],
                                      out.at[elem, 0, :, h, :], kv_sems.at[blk, t])
                for t, out in enumerate((nck_hbm, ncv_hbm)) for h in range(n_heads)]

    def store_k(rows, h, a):
        cols = slice(LANES * h, LANES * (h + 1))
        zero = jnp.zeros_like(a)
        k_scr[0, rows, cols] = jnp.where(lo, a, zero).astype(BF16)
        k_scr[1, rows, cols] = jnp.where(lo, zero, a).astype(BF16)

    if latent:
        n_past = cck_hbm.shape[2]
        past = pl.ds(seq, n_past)
        past_copies = [pltpu.make_async_copy(cache.at[step, layer // 2, :, h, :],
                                             past_stage.at[t, :, pl.ds(LANES * h, LANES)], past_sems.at[t])
                       for t, cache in enumerate((cck_hbm, ccv_hbm)) for h in range(n_heads)]
        for copy in past_copies:
            copy.start()

    def proj(i, carry):
        rows = _chunk_rows(i)
        hh = _modulated(x_ref, mod_ref, mod_row, rows)
        if latent:
            cos = cos_ref[rows, :]
            sin = sin_ref[rows, :]
            rot = lambda a: _rope(a, cos, sin)
        else:
            rot = lambda a: a
        for half in range(2):
            acc = _dot(hh, w_in_ref[:, 512 * half:512 * (half + 1)])
            for j in range(4):
                a = rot(acc[:, LANES * j:LANES * (j + 1)])
                cols = slice(512 * half + LANES * j, 512 * half + LANES * (j + 1))
                q_scr[rows, cols] = (a * Q_SCALE).astype(BF16)
        for half in range(2):
            acc = _dot(hh, w_in_ref[:, 1024 + 512 * half:1024 + 512 * (half + 1)])
            if not latent:
                kv_stage[i, 0, :, 512 * half:512 * (half + 1)] = acc
            for j in range(4):
                store_k(rows, 4 * half + j, rot(acc[:, LANES * j:LANES * (j + 1)]))
        for half in range(2):
            acc = _dot(hh, w_in_ref[:, 2048 + 512 * half:2048 + 512 * (half + 1)])
            if not latent:
                kv_stage[i, 1, :, 512 * half:512 * (half + 1)] = acc
            v_scr[rows, 512 * half:512 * (half + 1)] = acc.astype(BF16)
        if not latent:
            for copy in kv_out_copies(i):
                copy.start()
        for half in range(2):
            acc = _dot(hh, w_in_ref[:, 3072 + 512 * half:3072 + 512 * (half + 1)])
            g_scr[rows, 512 * half:512 * (half + 1)] = _silu(acc)
        return carry

    if latent:
        lax.fori_loop(0, n_blocks, proj, 0, unroll=2)
    else:
        for blk in range(n_blocks):
            proj(blk, 0)

    if latent:
        for copy in past_copies:
            copy.wait()
        for h in range(n_heads):
            store_k(past, h, past_stage[0, :, LANES * h:LANES * (h + 1)])
        v_scr[past, :] = past_stage[1].astype(BF16)

    lam = (jnp.exp(jnp.sum(lq1_ref[...] * lk1_ref[...], axis=1, keepdims=True))
           - jnp.exp(jnp.sum(lq2_ref[...] * lk2_ref[...], axis=1, keepdims=True)) + lam_init)
    sub = sub_ref[...] * (1.0 - lam_init)
    n_keys = seq + n_past if latent else ROW_CHUNK
    rb = _softmax_rows(n_keys)
    ones = jnp.ones((n_keys, LANES), BF16)

    def block_stages(i):
        rows = _chunk_rows(i)
        keys = pl.ds(0, n_keys) if latent else rows

        def qk(h, slot):
            cols = slice(LANES * h, LANES * (h + 1))
            q = q_scr[rows, cols]
            for m in (0, 1):
                s_scr[slot, m] = _dot_nt(q, k_scr[m, keys, cols])

        def softmax(h, slot):
            for m in (0, 1):
                for r in range(ROW_CHUNK // rb):
                    sub_rows = slice(r * rb, (r + 1) * rb)
                    s = s_scr[slot, m, sub_rows, :]
                    top = jnp.max(s, axis=1, keepdims=True)
                    p_scr[slot, m, sub_rows, :] = jnp.exp2((s - top).astype(BF16))

        def pv(h, slot):
            cols = slice(LANES * h, LANES * (h + 1))
            v_ext = jnp.concatenate([v_scr[keys, cols], ones], axis=1)
            maps = []
            for m in (0, 1):
                acc = _dot(p_scr[slot, m], v_ext)
                maps.append(acc[:, 0:LANES] / acc[:, LANES:2 * LANES])
            o = maps[0] - lam * maps[1]
            ms = jnp.mean(o * o, axis=1, keepdims=True)
            o = o * lax.rsqrt(ms + EPS) * sub
            attn_scr[rows, cols] = (o * g_scr[rows, cols]).astype(BF16)

        return qk, softmax, pv

    _attend_blocks(block_stages, n_blocks, n_heads, unrolled=not latent)

    if not latent:
        weights.load_out_proj()
    y_dma = (y_stage, y_sems, step * n_rows) if latent else None
    _out_proj_norm(x_ref, mod_ref, mod_row, attn_scr, w_out_ref, lng_ref, lnb_ref, layer, y_ref, n_rows, alpha, y_dma)

    if not latent:
        for blk in range(n_blocks):
            for copy in kv_out_copies(blk):
                copy.wait()
        weights.finish()


MOD_SLAB_ROWS = 128
MOD_COL_BLOCK = 1024
MOD_SLOTS = 4


def _mod_kernel(n_cond, cv_ref, w_hbm, b_ref, o_ref, sb_scr, ring, acc_scr, sems):
    depth, n_in, n_out = w_hbm.shape
    sublanes = 8
    slabs_per_layer = n_in // MOD_SLAB_ROWS
    slabs = [(l, rs) for l in range(depth) for rs in range(slabs_per_layer)]

    def slab_copy(n):
        l, rs = slabs[n]
        return pltpu.make_async_copy(w_hbm.at[l, pl.ds(rs * MOD_SLAB_ROWS, MOD_SLAB_ROWS), :],
                                     ring.at[n % MOD_SLOTS], sems.at[n % MOD_SLOTS])

    for n in range(min(MOD_SLOTS, len(slabs))):
        slab_copy(n).start()
    s_t = _silu(cv_ref[...]).T
    for r in range(n_cond):
        sb_scr[r] = jnp.broadcast_to(s_t[:, r:r + 1], (n_in, LANES))

    for n, (l, rs) in enumerate(slabs):
        slab_copy(n).wait()
        for cb in range(n_out // MOD_COL_BLOCK):
            cols = pl.ds(cb * MOD_COL_BLOCK, MOD_COL_BLOCK)
            if rs == 0:
                accs = (jnp.zeros((sublanes, MOD_COL_BLOCK), F32),) * n_cond
            else:
                accs = tuple(acc_scr[r, :, cols] for r in range(n_cond))

            def body(kb, accs, n=n, rs=rs, cols=cols):
                w = ring[n % MOD_SLOTS, pl.ds(pl.multiple_of(kb * sublanes, sublanes), sublanes), cols]
                s_rows = pl.ds(pl.multiple_of(rs * MOD_SLAB_ROWS + kb * sublanes, sublanes), sublanes)
                return tuple(acc + w * jnp.tile(sb_scr[r, s_rows, :], (1, MOD_COL_BLOCK // LANES))
                             for r, acc in enumerate(accs))

            accs = lax.fori_loop(0, MOD_SLAB_ROWS // sublanes, body, accs, unroll=8)
            if rs < slabs_per_layer - 1:
                for r in range(n_cond):
                    acc_scr[r, :, cols] = accs[r]
            else:
                rows = [jnp.sum(acc, axis=0, keepdims=True) + b_ref[l:l + 1, cols] for acc in accs]
                o_ref[l, :, cols] = jnp.concatenate(rows + [jnp.zeros((8 - n_cond, MOD_COL_BLOCK), F32)], axis=0)
        if n + MOD_SLOTS < len(slabs):
            slab_copy(n + MOD_SLOTS).start()


def _full(shape, **kw):
    zeros = (0,) * len(shape)
    return pl.BlockSpec(shape, lambda i: zeros, **kw)


def _weight_specs(latent, w_in, w_out):
    if latent:
        single = pl.Buffered(1)
        return [_full(w_in.shape, pipeline_mode=single), _full(w_out.shape, pipeline_mode=single)]
    return [pl.BlockSpec(memory_space=pl.ANY), pl.BlockSpec(memory_space=pl.ANY)]


def _weight_scratch(w_in, w_out):
    assert w_in.shape[0] % W_SLAB_ROWS == 0 and w_out.shape[0] % W_SLAB_ROWS == 0
    n_slabs = max(w_in.shape[0], w_out.shape[0]) // W_SLAB_ROWS
    return [pltpu.VMEM(w_in.shape, BF16), pltpu.VMEM(w_out.shape, BF16),
            pltpu.VMEM((n_slabs, W_SLAB_ROWS, max(w_in.shape[1], w_out.shape[1])), F32),
            pltpu.SemaphoreType.DMA((n_slabs,)), pltpu.SemaphoreType.DMA((2,))]


def _rope_tables(seq):
    t = np.arange(seq)
    n_freq = HEAD_DIM // 4
    freqs = ROPE_THETA ** (-np.arange(n_freq, dtype=np.float64) / n_freq)
    ang_row = (t // GRID_W)[:, None] * freqs
    ang_col = (t % GRID_W)[:, None] * freqs
    ang = np.concatenate([ang_row, ang_row, ang_col, ang_col], axis=1)
    sign = np.concatenate([-np.ones(n_freq), np.ones(n_freq)] * 2)[None, :]
    cos = np.tile(np.cos(ang), (1, 2)).astype(np.float32)
    sin = np.tile(np.sin(ang) * sign, (1, 2)).astype(np.float32)
    chunked_t = lambda a: a.reshape(seq // ROW_CHUNK, ROW_CHUNK, LANES).transpose(0, 2, 1)
    return jnp.asarray(cos), jnp.asarray(sin), jnp.asarray(chunked_t(cos)), jnp.asarray(chunked_t(sin))


def _modulation(c, c_ctx, w_mod, b_mod):
    depth = w_mod.shape[0]
    n_cond = 1 + c.shape[0]
    cv = jnp.concatenate([c_ctx[None, :], c, jnp.zeros((8 - n_cond, D_MODEL), F32)], axis=0)
    assert D_MODEL % MOD_SLAB_ROWS == 0 and (3 * D_MODEL) % MOD_COL_BLOCK == 0
    return pl.pallas_call(
        functools.partial(_mod_kernel, n_cond),
        grid=(1,),
        in_specs=[_full(cv.shape), pl.BlockSpec(memory_space=pl.ANY), _full(b_mod.shape)],
        out_specs=_full((depth, 8, 3 * D_MODEL)),
        out_shape=jax.ShapeDtypeStruct((depth, 8, 3 * D_MODEL), F32),
        scratch_shapes=[pltpu.VMEM((n_cond, D_MODEL, LANES), F32),
                        pltpu.VMEM((MOD_SLOTS, MOD_SLAB_ROWS, 3 * D_MODEL), F32),
                        pltpu.VMEM((n_cond, 8, 3 * D_MODEL), F32),
                        pltpu.SemaphoreType.DMA((MOD_SLOTS,))],
        compiler_params=pltpu.CompilerParams(dimension_semantics=("arbitrary",)),
        name="adaln_modulation",
    )(cv, w_mod, b_mod)


def _even_layer(x, mod, layer, w_in, w_out, q_norm, k_norm, sink, ln_g, ln_b, latent, seq, n_rows, alpha, extras=()):
    total = x.shape[0]
    grid = (total // n_rows,)
    single = pl.Buffered(1)
    norms = jnp.concatenate([jnp.broadcast_to(jnp.tile(k_norm, 2)[:, None], (LANES, ROW_CHUNK)),
                             jnp.broadcast_to(jnp.tile(q_norm, 2 * ROW_CHUNK // LANES)[None, :], (8, ROW_CHUNK))], axis=0)

    row_blk = lambda width: pl.BlockSpec((n_rows, width), lambda i: (i, 0))
    in_specs = [row_blk(D_MODEL),
                pl.BlockSpec((1, 8, 3 * D_MODEL), lambda i: (layer, 0, 0)),
                *_weight_specs(latent, w_in, w_out),
                _full(norms.shape),
                pl.BlockSpec(memory_space=pltpu.SMEM),
                _full(ln_g.shape), _full(ln_b.shape)]
    args = [x, mod, w_in, w_out, norms, sink, ln_g, ln_b]
    y_shape = jax.ShapeDtypeStruct((total, D_MODEL), F32)
    n_blocks = n_rows // ROW_CHUNK
    if latent:
        cos, sin, cos_t, sin_t, cakt, cav, cbkt, cbv = extras
        n_past = cav.shape[2]
        in_specs += [_full(cos.shape, pipeline_mode=single), _full(sin.shape, pipeline_mode=single),
                     _full(cos_t.shape, pipeline_mode=single), _full(sin_t.shape, pipeline_mode=single)]
        in_specs += [pl.BlockSpec((1, LANES, n_past), lambda i: (i, 0, 0))] * 4
        args += [cos, sin, cos_t, sin_t, cakt, cav, cbkt, cbv]
        out_specs = pl.BlockSpec(memory_space=pl.ANY)
        out_shape = y_shape
        n_keys = seq + n_past
    else:
        kv_blk = pl.BlockSpec((n_blocks, LANES, ROW_CHUNK), lambda i: (i, 0, 0))
        hbm = pl.BlockSpec(memory_space=pl.ANY)
        out_specs = [row_blk(D_MODEL)] + [kv_blk] * 4 + [hbm, hbm]
        out_shape = ([y_shape] + [jax.ShapeDtypeStruct((total // seq, LANES, seq), F32)] * 4
                     + [jax.ShapeDtypeStruct(w_in.shape, BF16), jax.ShapeDtypeStruct(w_out.shape, BF16)])
        n_keys = n_rows
    n_kchunks = n_keys // ROW_CHUNK
    n_cols = n_keys if latent else ROW_CHUNK
    scratch = [pltpu.VMEM((n_rows, D_MODEL), BF16),
               pltpu.VMEM((n_rows, 512), BF16), pltpu.VMEM((n_rows, 512), BF16),
               pltpu.VMEM((4, n_kchunks, LANES, ROW_CHUNK), BF16), pltpu.VMEM((4, n_keys, LANES), BF16),
               pltpu.VMEM((4, n_keys // WINDOW, LANES, WINDOW), BF16), pltpu.VMEM((4, n_keys, LANES), BF16),
               pltpu.VMEM((n_rows, D_MODEL), F32),
               pltpu.VMEM((2, 2, ROW_CHUNK, n_cols), F32),
               pltpu.VMEM((2, 2, ROW_CHUNK, n_cols), BF16),
               pltpu.VMEM((2, ROW_CHUNK, LANES), F32),
               pltpu.VMEM((2 * LANES, D_MODEL), BF16),
               pltpu.VMEM((D_MODEL, 2 * LANES), BF16)]
    if latent:
        scratch.append(pltpu.VMEM((1 + 2 * WINDOW // ROW_CHUNK, ROW_CHUNK, ROW_CHUNK), F32))
        scratch += [pltpu.VMEM((n_blocks, ROW_CHUNK, D_MODEL), F32), pltpu.SemaphoreType.DMA((n_blocks,))]
    else:
        scratch += _weight_scratch(w_in, w_out)
    return pl.pallas_call(
        functools.partial(_even_kernel, latent, layer, n_rows, seq, alpha),
        grid=grid, in_specs=in_specs, out_specs=out_specs, out_shape=out_shape,
        scratch_shapes=scratch,
        compiler_params=pltpu.CompilerParams(dimension_semantics=("arbitrary",), vmem_limit_bytes=VMEM_LIMIT),
        name="even_layer_latent" if latent else "even_layer_context",
    )(*args)


def _odd_layer(x, mod, layer, w_in, w_out, lams, sub, ln_g, ln_b, latent, seq, n_rows, alpha, lam_init, extras=()):
    total = x.shape[0]
    grid = (total // n_rows,)
    row_blk = lambda width: pl.BlockSpec((n_rows, width), lambda i: (i, 0))
    single = pl.Buffered(1)
    in_specs = [row_blk(D_MODEL),
                pl.BlockSpec((1, 8, 3 * D_MODEL), lambda i: (layer, 0, 0)),
                *_weight_specs(latent, w_in, w_out),
                _full((1, HEAD_DIM)), _full((1, HEAD_DIM)), _full((1, HEAD_DIM)), _full((1, HEAD_DIM)),
                _full((1, LANES)),
                _full(ln_g.shape), _full(ln_b.shape)]
    args = [x, mod, w_in, w_out, *lams, sub, ln_g, ln_b]
    y_shape = jax.ShapeDtypeStruct((total, D_MODEL), F32)
    n_heads = D_MODEL // LANES
    n_blocks = n_rows // ROW_CHUNK
    if latent:
        cos, sin, cck, ccv = extras
        n_past = cck.shape[2]
        in_specs += [_full(cos.shape, pipeline_mode=single), _full(sin.shape, pipeline_mode=single)]
        in_specs += [pl.BlockSpec(memory_space=pl.ANY)] * 2
        args += [cos, sin, cck, ccv]
        out_specs = pl.BlockSpec(memory_space=pl.ANY)
        out_shape = y_shape
        n_keys = seq + n_past
    else:
        hbm = pl.BlockSpec(memory_space=pl.ANY)
        out_specs = [row_blk(D_MODEL), hbm, hbm, hbm, hbm]
        out_shape = ([y_shape] + [jax.ShapeDtypeStruct((total // seq, 1, seq, n_heads, LANES), F32)] * 2
                     + [jax.ShapeDtypeStruct(w_in.shape, BF16), jax.ShapeDtypeStruct(w_out.shape, BF16)])
        n_keys = n_rows
    n_cols = n_keys if latent else ROW_CHUNK
    scratch = [pltpu.VMEM((n_rows, D_MODEL), BF16),
               pltpu.VMEM((n_rows, D_MODEL), BF16),
               pltpu.VMEM((2, n_keys, D_MODEL), BF16),
               pltpu.VMEM((n_keys, D_MODEL), BF16),
               pltpu.VMEM((n_rows, D_MODEL), F32),
               pltpu.VMEM((2, 2, ROW_CHUNK, n_cols), F32),
               pltpu.VMEM((2, 2, ROW_CHUNK, n_cols), BF16)]
    if latent:
        scratch += [pltpu.VMEM((2, n_past, D_MODEL), F32), pltpu.SemaphoreType.DMA((2,))]
        scratch += [pltpu.VMEM((n_blocks, ROW_CHUNK, D_MODEL), F32), pltpu.SemaphoreType.DMA((n_blocks,))]
    else:
        scratch += [pltpu.VMEM((n_blocks, 2, ROW_CHUNK, D_MODEL), F32),
                    pltpu.SemaphoreType.DMA((n_blocks, 2))]
        scratch += _weight_scratch(w_in, w_out)
    return pl.pallas_call(
        functools.partial(_odd_kernel, latent, layer, n_rows, seq, alpha, lam_init),
        grid=grid, in_specs=in_specs, out_specs=out_specs, out_shape=out_shape,
        scratch_shapes=scratch,
        compiler_params=pltpu.CompilerParams(dimension_semantics=("arbitrary",), vmem_limit_bytes=VMEM_LIMIT),
        name="odd_layer_latent" if latent else "odd_layer_context",
    )(*args)


def kernel(x_prompt, x_sample, cache_a_k, cache_a_v, cache_b_k, cache_b_v, cache_c_k, cache_c_v, c, c_ctx,
           w_mod, b_mod, ln_g, ln_b, w_in_even, w_out_even, q_norm_a, k_norm_a, sink_b, w_in_odd, w_out_odd,
           lambda_q1, lambda_k1, lambda_q2, lambda_k2, subln_c):
    depth = w_mod.shape[0]
    batch, seq, _ = x_prompt.shape
    dec_batch, dec_seq, _ = x_sample.shape
    n_past = cache_a_k.shape[2]
    alpha = (2 * depth) ** 0.25
    assert seq == ROW_CHUNK and n_past % ROW_CHUNK == 0 and dec_seq % ROW_CHUNK == 0

    mod = _modulation(c, c_ctx, w_mod, b_mod)
    cos, sin, cos_t, sin_t = _rope_tables(dec_seq)

    bf16_weights = {}

    def run(x, latent, n_batch, s, rows_even, rows_odd):
        kv = {"a_k": [], "a_v": [], "b_k": [], "b_v": [], "c_k": [], "c_v": []}
        for l in range(depth):
            if l % 2 == 0:
                e = l // 2
                extras = ()
                if latent:
                    k_t = lambda t: t[:, e].transpose(0, 2, 3, 1).reshape(n_batch, LANES, n_past)
                    extras = (cos, sin, cos_t, sin_t,
                              k_t(cache_a_k), k_t(cache_a_v), k_t(cache_b_k), k_t(cache_b_v))
                w_in, w_out = bf16_weights[l] if latent else (w_in_even[e], w_out_even[e])
                res = _even_layer(x, mod, l, w_in, w_out, q_norm_a[e], k_norm_a[e],
                                  sink_b[e], ln_g, ln_b, latent, s, rows_even, alpha, extras)
                if latent:
                    x = res
                else:
                    x = res[0]
                    bf16_weights[l] = res[5:7]
                    for name, t in zip(("a_k", "a_v", "b_k", "b_v"), res[1:5]):
                        kv[name].append(t.reshape(n_batch, 2, HEAD_DIM, s).transpose(0, 3, 1, 2))
            else:
                o = l // 2
                lam_init = 0.8 - 0.6 * math.exp(-0.3 * l)
                extras = (cos, sin, cache_c_k, cache_c_v) if latent else ()
                lams = [t[o][None, :] for t in (lambda_q1, lambda_k1, lambda_q2, lambda_k2)]
                w_in, w_out = bf16_weights[l] if latent else (w_in_odd[o], w_out_odd[o])
                res = _odd_layer(x, mod, l, w_in, w_out, lams,
                                 subln_c[o][None, :], ln_g, ln_b, latent, s, rows_odd, alpha, lam_init, extras)
                if latent:
                    x = res
                else:
                    x = res[0]
                    bf16_weights[l] = res[3:5]
                    kv["c_k"].append(res[1][:, 0])
                    kv["c_v"].append(res[2][:, 0])
        return x, kv

    y_ctx, kv = run(x_prompt.reshape(batch * seq, D_MODEL), False, batch, seq, 1024, 512)
    y_lat, _ = run(x_sample.reshape(dec_batch * dec_seq, D_MODEL), True, dec_batch, dec_seq, dec_seq, dec_seq)

    stack = lambda name: jnp.stack(kv[name], axis=1)
    return (y_ctx.reshape(batch, seq, D_MODEL), y_lat.reshape(dec_batch, dec_seq, D_MODEL),
            stack("a_k"), stack("a_v"), stack("b_k"), stack("b_v"), stack("c_k"), stack("c_v"))
```

```python
import functools
import math

import jax
import jax.numpy as jnp
import numpy as np
from jax import lax
from jax.experimental import pallas as pl
from jax.experimental.pallas import tpu as pltpu

F32 = jnp.float32
BF16 = jnp.bfloat16

D_MODEL = 1024
HEAD_DIM = 64
GRID_W = 64
WINDOW = 128
ROPE_THETA = 10000.0
EPS = 1e-6
NEG_INF = -1e30
LOG2E = 1.4426950408889634
Q_SCALE = HEAD_DIM ** -0.5 * LOG2E
LANES = 128
ROW_CHUNK = 256
SOFTMAX_VREGS = 40
VMEM_LIMIT = 60000 * 1024
W_SLAB_ROWS = 128


def _silu(x):
    return x / (1.0 + jnp.exp(-x))


def _dot(a, b):
    return jnp.dot(a, b, preferred_element_type=F32)


def _dot_nt(a, b):
    return lax.dot_general(a, b, (((1,), (1,)), ((), ())), preferred_element_type=F32)


def _lane_iota(rows):
    return lax.broadcasted_iota(jnp.int32, (rows, LANES), 1)


def _chunk_rows(i):
    if isinstance(i, int):
        return pl.ds(i * ROW_CHUNK, ROW_CHUNK)
    return pl.ds(pl.multiple_of(i * ROW_CHUNK, ROW_CHUNK), ROW_CHUNK)


def _softmax_rows(n_cols):
    rows = 8
    while rows * 2 * n_cols <= SOFTMAX_VREGS * 1024 and rows * 2 <= ROW_CHUNK:
        rows *= 2
    return rows


def _rope(a, cos, sin_signed):
    lane = _lane_iota(a.shape[0])
    fwd = pltpu.roll(a, LANES - 16, 1)
    bwd = pltpu.roll(a, 16, 1)
    partner = jnp.where((lane & 16) == 0, fwd, bwd)
    return a * cos + partner * sin_signed


def _rope_t(a, cos_t, sin_t):
    blocks = [a[16 * b:16 * (b + 1), :] for b in range(a.shape[0] // 16)]
    partner = jnp.concatenate([blocks[b ^ 1] for b in range(len(blocks))], axis=0)
    return a * cos_t + partner * sin_t


def _store_kt_variants(scr, chunk, kt):
    width = scr.shape[-1]
    per_block = kt.shape[1] // width
    zero = jnp.zeros((HEAD_DIM, kt.shape[1]), F32)
    for j in range(2):
        kj = kt[HEAD_DIM * j:HEAD_DIM * (j + 1), :]
        for par, full in enumerate((jnp.concatenate([kj, zero], axis=0), jnp.concatenate([zero, kj], axis=0))):
            full = full.astype(BF16)
            for c in range(per_block):
                scr[2 * j + par, chunk * per_block + c] = full[:, width * c:width * (c + 1)]


def _store_v_variants(scr, rows, a):
    lane = _lane_iota(a.shape[0])
    lo = lane < HEAD_DIM
    swapped = pltpu.roll(a, HEAD_DIM, 1)
    one = jnp.ones_like(a)
    scr[0, rows, :] = jnp.where(lo, a, one).astype(BF16)
    scr[1, rows, :] = jnp.where(lo, one, swapped).astype(BF16)
    scr[2, rows, :] = jnp.where(lo, swapped, one).astype(BF16)
    scr[3, rows, :] = jnp.where(lo, one, a).astype(BF16)


def _layer_norm_rows(z, g, b):
    mu = jnp.mean(z, axis=-1, keepdims=True)
    zc = z - mu
    var = jnp.mean(zc * zc, axis=-1, keepdims=True)
    return zc * lax.rsqrt(var + EPS) * g + b


def _modulated(x_ref, mod_ref, mod_row, rows):
    shift = mod_ref[0, pl.ds(mod_row, 1), 0:D_MODEL]
    scale = mod_ref[0, pl.ds(mod_row, 1), D_MODEL:2 * D_MODEL]
    return (x_ref[rows, :] * (1.0 + scale) + shift).astype(BF16)


def _out_proj_norm(x_ref, mod_ref, mod_row, attn_scr, w_out_ref, lng_ref, lnb_ref, layer, y_ref, n_rows, alpha):
    gate = mod_ref[0, pl.ds(mod_row, 1), 2 * D_MODEL:3 * D_MODEL]
    g = lng_ref[layer:layer + 1, :]
    b = lnb_ref[layer:layer + 1, :]

    def body(i, carry):
        rows = _chunk_rows(i)
        out = _dot(attn_scr[rows, :], w_out_ref[...])
        z = alpha * x_ref[rows, :] + gate * out
        y_ref[rows, :] = _layer_norm_rows(z, g, b)
        return carry

    lax.fori_loop(0, n_rows // ROW_CHUNK, body, 0, unroll=True)


class _ContextWeights:
    def __init__(self, step, w_in, w_out, stage, sems, out_sems):
        self.step, self.w_in, self.w_out, self.stage, self.sems = step, w_in, w_out, stage, sems
        self.out_copies = [pltpu.make_async_copy(w[1], w[2], out_sems.at[n]) for n, w in enumerate((w_in, w_out))]

    def _slab_copies(self, w_hbm):
        n_cols = w_hbm.shape[1]
        return [pltpu.make_async_copy(w_hbm.at[pl.ds(s * W_SLAB_ROWS, W_SLAB_ROWS), :],
                                      self.stage.at[s, :, pl.ds(0, n_cols)], self.sems.at[s])
                for s in range(w_hbm.shape[0] // W_SLAB_ROWS)]

    def _cast(self, w_hbm, w_scr):
        n_cols = w_hbm.shape[1]
        for s, copy in enumerate(self._slab_copies(w_hbm)):
            copy.wait()
            w_scr[pl.ds(s * W_SLAB_ROWS, W_SLAB_ROWS), :] = self.stage[s, :, 0:n_cols].astype(BF16)

    def load_in_proj(self):
        @pl.when(self.step == 0)
        def _():
            for copy in self._slab_copies(self.w_in[0]):
                copy.start()
            self._cast(self.w_in[0], self.w_in[1])
            self.out_copies[0].start()
            for copy in self._slab_copies(self.w_out[0]):
                copy.start()

    def load_out_proj(self):
        @pl.when(self.step == 0)
        def _():
            self._cast(self.w_out[0], self.w_out[1])
            self.out_copies[1].start()

    def finish(self):
        @pl.when(self.step == 0)
        def _():
            for copy in self.out_copies:
                copy.wait()


def _run_pipeline(n_items, stages):
    for u in range(n_items + len(stages) - 1):
        for k, stage in enumerate(stages):
            t = u - k
            if 0 <= t < n_items:
                stage(t, t % 2)


def _attend_blocks(block_stages, n_blocks, n_items, unrolled):
    assert n_items % 2 == 0
    if unrolled:
        per_block = [block_stages(i) for i in range(n_blocks)]
        stages = [lambda g, slot, k=k: per_block[g // n_items][k](g % n_items, slot) for k in range(3)]
        _run_pipeline(n_blocks * n_items, stages)
    else:
        def body(i, carry):
            _run_pipeline(n_items, block_stages(i))
            return carry

        lax.fori_loop(0, n_blocks, body, 0)


def _even_kernel(latent, layer, n_rows, seq, alpha, *refs):
    if latent:
        (x_ref, mod_ref, w_in_ref, w_out_ref, norms_ref, sink_ref, lng_ref, lnb_ref,
         cos_ref, sin_ref, cost_ref, sint_ref, cakt_ref, cav_ref, cbkt_ref, cbv_ref,
         y_ref,
         attn_scr, qa_scr, qb_scr, ka_scr, va_scr, kb_scr, vb_scr, g_scr, s_scr, p_scr, es_scr, wkt_scr, wv_scr,
         bias_scr) = refs
    else:
        (x_ref, mod_ref, w_in_hbm, w_out_hbm, norms_ref, sink_ref, lng_ref, lnb_ref,
         y_ref, nakt_ref, navt_ref, nbkt_ref, nbvt_ref, w_in_bf_hbm, w_out_bf_hbm,
         attn_scr, qa_scr, qb_scr, ka_scr, va_scr, kb_scr, vb_scr, g_scr, s_scr, p_scr, es_scr, wkt_scr,
         wv_scr, w_in_ref, w_out_ref, w_stage, w_sems, w_out_sems) = refs

    step = pl.program_id(0)
    if not latent:
        weights = _ContextWeights(step, (w_in_hbm, w_in_ref, w_in_bf_hbm), (w_out_hbm, w_out_ref, w_out_bf_hbm),
                                  w_stage, w_sems, w_out_sems)
        weights.load_in_proj()
    mod_row = step + 1 if latent else 0

    col_ka, col_va, col_kb, col_vb = 512, 640, 1792, 1920

    @pl.when(step == 0)
    def _():
        for r, c0 in enumerate((col_ka, col_kb)):
            wkt_scr[LANES * r:LANES * (r + 1), :] = w_in_ref[:, c0:c0 + LANES].T
        wv_scr[:, 0:LANES] = w_in_ref[:, col_va:col_va + LANES]
        wv_scr[:, LANES:2 * LANES] = w_in_ref[:, col_vb:col_vb + LANES]

    n_lat_chunks = seq // ROW_CHUNK
    if latent:
        n_past = cav_ref.shape[2]
        past_rows = pl.ds(seq, n_past)
        _store_kt_variants(ka_scr, n_lat_chunks, cakt_ref[0])
        _store_kt_variants(kb_scr, n_lat_chunks, cbkt_ref[0])
        _store_v_variants(va_scr, past_rows, cav_ref[0].T)
        _store_v_variants(vb_scr, past_rows, cbv_ref[0].T)

    knt = norms_ref[0:LANES, :]
    qn = norms_ref[LANES:LANES + 1, 0:LANES]

    def proj(i, carry):
        rows = _chunk_rows(i)
        hh = _modulated(x_ref, mod_ref, mod_row, rows)
        if latent:
            cos = cos_ref[rows, :]
            sin = sin_ref[rows, :]
            rot = lambda a: _rope(a, cos, sin)
            rot_t = lambda a: _rope_t(a, cost_ref[i], sint_ref[i])
        else:
            rot = rot_t = lambda a: a

        acc = _dot(hh, w_in_ref[:, 0:512])
        lo_lanes = _lane_iota(ROW_CHUNK) < HEAD_DIM
        for j in range(4):
            a = acc[:, LANES * j:LANES * (j + 1)]
            sq = a * a
            first = jnp.sum(jnp.where(lo_lanes, sq, 0.0), axis=1, keepdims=True)
            second = jnp.sum(jnp.where(lo_lanes, 0.0, sq), axis=1, keepdims=True)
            ms = jnp.where(lo_lanes, first, second) * (1.0 / HEAD_DIM)
            a = rot(a * lax.rsqrt(ms + EPS) * qn)
            qa_scr[rows, LANES * j:LANES * (j + 1)] = (a * Q_SCALE).astype(BF16)
        acc = _dot(hh, w_in_ref[:, 1280:1792])
        for j in range(4):
            a = rot(acc[:, LANES * j:LANES * (j + 1)])
            qb_scr[rows, LANES * j:LANES * (j + 1)] = (a * Q_SCALE).astype(BF16)
        g_scr[rows, 0:512] = _silu(_dot(hh, w_in_ref[:, 768:1280]))
        g_scr[rows, 512:1024] = _silu(_dot(hh, w_in_ref[:, 2048:2560]))
        v = _dot(hh, wv_scr[...])
        _store_v_variants(va_scr, rows, v[:, 0:LANES])
        _store_v_variants(vb_scr, rows, v[:, LANES:2 * LANES])

        kt = _dot_nt(wkt_scr[0:2 * LANES, :], hh)
        heads = [kt[HEAD_DIM * h:HEAD_DIM * (h + 1), :] for h in range(2)]
        kat = jnp.concatenate([blk * lax.rsqrt(jnp.mean(blk * blk, axis=0, keepdims=True) + EPS) for blk in heads],
                              axis=0) * knt
        kbt = kt[LANES:2 * LANES, :]
        if not latent:
            vt = v.T
            nakt_ref[i] = kat
            nbkt_ref[i] = kbt
            navt_ref[i] = vt[0:LANES, :]
            nbvt_ref[i] = vt[LANES:2 * LANES, :]
        _store_kt_variants(ka_scr, i, rot_t(kat))
        _store_kt_variants(kb_scr, i, rot_t(kbt))
        return carry

    lax.fori_loop(0, n_rows // ROW_CHUNK, proj, 0, unroll=2)

    sinks = [sink_ref[h] * LOG2E for h in range(8)]
    ck = ROW_CHUNK
    bk = kb_scr.shape[-1]
    win = ROW_CHUNK + 2 * WINDOW
    n_items = 8

    def block_stages(i):
        rows = _chunk_rows(i)
        if latent:
            a_chunks = list(range(n_lat_chunks + n_past // ck))
            a_keys = pl.ds(0, seq + n_past)
            w0 = jnp.clip(i * (ck // bk) - WINDOW // bk, 0, (seq - win) // bk)
            win_rows = pl.ds(pl.multiple_of(w0 * bk, bk), win)
            dist = (lax.broadcasted_iota(jnp.int32, (ROW_CHUNK, ck), 1)
                    - lax.broadcasted_iota(jnp.int32, (ROW_CHUNK, ck), 0))
            for c in range(win // ck):
                off = w0 * bk + c * ck - i * ck
                bias_scr[c] = jnp.where(jnp.abs(dist + off) <= WINDOW, 0.0, NEG_INF).astype(F32)
            b_first = [w0 + c * (ck // bk) for c in range(win // ck)] + [seq // bk]
            n_biased = win // ck
            b_cols = win + n_past
        else:
            a_chunks = [i]
            a_keys = rows
            b_first = [i * (ck // bk)]
            n_biased = 0
            b_cols = ck
        a_cols = len(a_chunks) * ck

        def qk(t, slot):
            p, branch = divmod(t, 2)
            cols = slice(LANES * p, LANES * (p + 1))
            kvh = p // 2
            q = (qb_scr if branch else qa_scr)[rows, cols]
            for par in (0, 1):
                var = 2 * kvh + par
                if branch:
                    tiles = [jnp.concatenate([kb_scr[var, first + d] for d in range(ck // bk)], axis=1)
                             for first in b_first]
                else:
                    tiles = [ka_scr[var, chunk] for chunk in a_chunks]
                for c, kt in enumerate(tiles):
                    s = _dot(q, kt)
                    if branch and c < n_biased:
                        s = s + bias_scr[c]
                    s_scr[slot, par, :, c * ck:(c + 1) * ck] = s

        def softmax(t, slot):
            p, branch = divmod(t, 2)
            n_cols = b_cols if branch else a_cols
            rb = _softmax_rows(n_cols)
            for par in (0, 1):
                for r in range(ROW_CHUNK // rb):
                    sub = slice(r * rb, (r + 1) * rb)
                    s = s_scr[slot, par, sub, 0:n_cols]
                    m = jnp.max(s, axis=1, keepdims=True)
                    if branch:
                        sink = sinks[2 * p + par]
                        m = jnp.maximum(m, sink)
                        es_scr[slot, sub, HEAD_DIM * par:HEAD_DIM * (par + 1)] = jnp.broadcast_to(
                            jnp.exp2(sink - m), (rb, HEAD_DIM))
                    p_scr[slot, par, sub, 0:n_cols] = jnp.exp2((s - m).astype(BF16))

        def pv(t, slot):
            p, branch = divmod(t, 2)
            kvh = p // 2
            v_scr = vb_scr if branch else va_scr
            accs = []
            for par in (0, 1):
                var = 2 * kvh + par
                if latent and branch:
                    n_loc = win
                    accs.append(_dot(p_scr[slot, par, :, 0:n_loc], v_scr[var, win_rows, :])
                                + _dot(p_scr[slot, par, :, n_loc:b_cols], v_scr[var, past_rows, :]))
                else:
                    accs.append(_dot(p_scr[slot, par, :, 0:a_cols], v_scr[var, a_keys, :]))
            lo = _lane_iota(ROW_CHUNK) < HEAD_DIM
            denom = pltpu.roll(jnp.where(lo, accs[1], accs[0]), HEAD_DIM, 1)
            if branch:
                denom = denom + es_scr[slot]
            o = jnp.where(lo, accs[0], accs[1]) / denom
            ocols = slice(512 * branch + LANES * p, 512 * branch + LANES * (p + 1))
            attn_scr[rows, ocols] = (o * g_scr[rows, ocols]).astype(BF16)

        return qk, softmax, pv

    _attend_blocks(block_stages, n_rows // ROW_CHUNK, n_items, unrolled=not latent)

    if not latent:
        weights.load_out_proj()
    _out_proj_norm(x_ref, mod_ref, mod_row, attn_scr, w_out_ref, lng_ref, lnb_ref, layer, y_ref, n_rows, alpha)
    if not latent:
        weights.finish()


def _odd_kernel(latent, layer, n_rows, seq, alpha, lam_init, *refs):
    if latent:
        (x_ref, mod_ref, w_in_ref, w_out_ref, lq1_ref, lk1_ref, lq2_ref, lk2_ref, sub_ref, lng_ref, lnb_ref,
         cos_ref, sin_ref, cck_hbm, ccv_hbm,
         y_ref,
         attn_scr, q_scr, k_scr, v_scr, g_scr, s_scr, p_scr, past_stage, past_sems) = refs
    else:
        (x_ref, mod_ref, w_in_hbm, w_out_hbm, lq1_ref, lk1_ref, lq2_ref, lk2_ref, sub_ref, lng_ref, lnb_ref,
         y_ref, nck_hbm, ncv_hbm, w_in_bf_hbm, w_out_bf_hbm,
         attn_scr, q_scr, k_scr, v_scr, g_scr, s_scr, p_scr, kv_stage, kv_sems,
         w_in_ref, w_out_ref, w_stage, w_sems, w_out_sems) = refs

    step = pl.program_id(0)
    if not latent:
        weights = _ContextWeights(step, (w_in_hbm, w_in_ref, w_in_bf_hbm), (w_out_hbm, w_out_ref, w_out_bf_hbm),
                                  w_stage, w_sems, w_out_sems)
        weights.load_in_proj()
    mod_row = step + 1 if latent else 0

    n_heads = D_MODEL // LANES
    n_blocks = n_rows // ROW_CHUNK
    lo = _lane_iota(ROW_CHUNK) < HEAD_DIM

    def kv_out_copies(blk):
        elem = step * n_blocks + blk
        return [pltpu.make_async_copy(kv_stage.at[blk, t, :, pl.ds(LANES * h, LANES)],
                                      out.at[elem, 0, :, h, :], kv_sems.at[blk, t])
                for t, out in enumerate((nck_hbm, ncv_hbm)) for h in range(n_heads)]

    def store_k(rows, h, a):
        cols = slice(LANES * h, LANES * (h + 1))
        zero = jnp.zeros_like(a)
        k_scr[0, rows, cols] = jnp.where(lo, a, zero).astype(BF16)
        k_scr[1, rows, cols] = jnp.where(lo, zero, a).astype(BF16)

    if latent:
        n_past = cck_hbm.shape[2]
        past = pl.ds(seq, n_past)
        past_copies = [pltpu.make_async_copy(cache.at[step, layer // 2, :, h, :],
                                             past_stage.at[t, :, pl.ds(LANES * h, LANES)], past_sems.at[t])
                       for t, cache in enumerate((cck_hbm, ccv_hbm)) for h in range(n_heads)]
        for copy in past_copies:
            copy.start()

    def proj(i, carry):
        rows = _chunk_rows(i)
        hh = _modulated(x_ref, mod_ref, mod_row, rows)
        if latent:
            cos = cos_ref[rows, :]
            sin = sin_ref[rows, :]
            rot = lambda a: _rope(a, cos, sin)
        else:
            rot = lambda a: a
        for half in range(2):
            acc = _dot(hh, w_in_ref[:, 512 * half:512 * (half + 1)])
            for j in range(4):
                a = rot(acc[:, LANES * j:LANES * (j + 1)])
                cols = slice(512 * half + LANES * j, 512 * half + LANES * (j + 1))
                q_scr[rows, cols] = (a * Q_SCALE).astype(BF16)
        for half in range(2):
            acc = _dot(hh, w_in_ref[:, 1024 + 512 * half:1024 + 512 * (half + 1)])
            if not latent:
                kv_stage[i, 0, :, 512 * half:512 * (half + 1)] = acc
            for j in range(4):
                store_k(rows, 4 * half + j, rot(acc[:, LANES * j:LANES * (j + 1)]))
        for half in range(2):
            acc = _dot(hh, w_in_ref[:, 2048 + 512 * half:2048 + 512 * (half + 1)])
            if not latent:
                kv_stage[i, 1, :, 512 * half:512 * (half + 1)] = acc
            v_scr[rows, 512 * half:512 * (half + 1)] = acc.astype(BF16)
        if not latent:
            for copy in kv_out_copies(i):
                copy.start()
        for half in range(2):
            acc = _dot(hh, w_in_ref[:, 3072 + 512 * half:3072 + 512 * (half + 1)])
            g_scr[rows, 512 * half:512 * (half + 1)] = _silu(acc)
        return carry

    if latent:
        lax.fori_loop(0, n_blocks, proj, 0, unroll=2)
    else:
        for blk in range(n_blocks):
            proj(blk, 0)

    if latent:
        for copy in past_copies:
            copy.wait()
        for h in range(n_heads):
            store_k(past, h, past_stage[0, :, LANES * h:LANES * (h + 1)])
        v_scr[past, :] = past_stage[1].astype(BF16)

    lam = (jnp.exp(jnp.sum(lq1_ref[...] * lk1_ref[...], axis=1, keepdims=True))
           - jnp.exp(jnp.sum(lq2_ref[...] * lk2_ref[...], axis=1, keepdims=True)) + lam_init)
    sub = sub_ref[...] * (1.0 - lam_init)
    n_keys = seq + n_past if latent else ROW_CHUNK
    rb = _softmax_rows(n_keys)
    ones = jnp.ones((n_keys, LANES), BF16)

    def block_stages(i):
        rows = _chunk_rows(i)
        keys = pl.ds(0, n_keys) if latent else rows

        def qk(h, slot):
            cols = slice(LANES * h, LANES * (h + 1))
            q = q_scr[rows, cols]
            for m in (0, 1):
                s_scr[slot, m] = _dot_nt(q, k_scr[m, keys, cols])

        def softmax(h, slot):
            for m in (0, 1):
                for r in range(ROW_CHUNK // rb):
                    sub_rows = slice(r * rb, (r + 1) * rb)
                    s = s_scr[slot, m, sub_rows, :]
                    top = jnp.max(s, axis=1, keepdims=True)
                    p_scr[slot, m, sub_rows, :] = jnp.exp2((s - top).astype(BF16))

        def pv(h, slot):
            cols = slice(LANES * h, LANES * (h + 1))
            v_ext = jnp.concatenate([v_scr[keys, cols], ones], axis=1)
            maps = []
            for m in (0, 1):
                acc = _dot(p_scr[slot, m], v_ext)
                maps.append(acc[:, 0:LANES] / acc[:, LANES:2 * LANES])
            o = maps[0] - lam * maps[1]
            ms = jnp.mean(o * o, axis=1, keepdims=True)
            o = o * lax.rsqrt(ms + EPS) * sub
            attn_scr[rows, cols] = (o * g_scr[rows, cols]).astype(BF16)

        return qk, softmax, pv

    _attend_blocks(block_stages, n_blocks, n_heads, unrolled=not latent)

    if not latent:
        weights.load_out_proj()
    _out_proj_norm(x_ref, mod_ref, mod_row, attn_scr, w_out_ref, lng_ref, lnb_ref, layer, y_ref, n_rows, alpha)

    if not latent:
        for blk in range(n_blocks):
            for copy in kv_out_copies(blk):
                copy.wait()
        weights.finish()


MOD_SLAB_ROWS = 128
MOD_COL_BLOCK = 1024
MOD_SLOTS = 8


def _mod_kernel(n_cond, cv_ref, w_hbm, b_ref, o_ref, sb_scr, ring, acc_scr, sems):
    depth, n_in, n_out = w_hbm.shape
    sublanes = 8
    slabs_per_layer = n_in // MOD_SLAB_ROWS
    slabs = [(l, rs) for l in range(depth) for rs in range(slabs_per_layer)]

    def slab_copy(n):
        l, rs = slabs[n]
        return pltpu.make_async_copy(w_hbm.at[l, pl.ds(rs * MOD_SLAB_ROWS, MOD_SLAB_ROWS), :],
                                     ring.at[n % MOD_SLOTS], sems.at[n % MOD_SLOTS])

    for n in range(min(MOD_SLOTS, len(slabs))):
        slab_copy(n).start()
    s_t = _silu(cv_ref[...]).T
    for r in range(n_cond):
        sb_scr[r] = jnp.broadcast_to(s_t[:, r:r + 1], (n_in, LANES))

    for n, (l, rs) in enumerate(slabs):
        slab_copy(n).wait()
        for cb in range(n_out // MOD_COL_BLOCK):
            cols = pl.ds(cb * MOD_COL_BLOCK, MOD_COL_BLOCK)
            if rs == 0:
                accs = (jnp.zeros((sublanes, MOD_COL_BLOCK), F32),) * n_cond
            else:
                accs = tuple(acc_scr[r, :, cols] for r in range(n_cond))

            def body(kb, accs, n=n, rs=rs, cols=cols):
                w = ring[n % MOD_SLOTS, pl.ds(pl.multiple_of(kb * sublanes, sublanes), sublanes), cols]
                s_rows = pl.ds(pl.multiple_of(rs * MOD_SLAB_ROWS + kb * sublanes, sublanes), sublanes)
                return tuple(acc + w * jnp.tile(sb_scr[r, s_rows, :], (1, MOD_COL_BLOCK // LANES))
                             for r, acc in enumerate(accs))

            accs = lax.fori_loop(0, MOD_SLAB_ROWS // sublanes, body, accs, unroll=8)
            if rs < slabs_per_layer - 1:
                for r in range(n_cond):
                    acc_scr[r, :, cols] = accs[r]
            else:
                rows = [jnp.sum(acc, axis=0, keepdims=True) + b_ref[l:l + 1, cols] for acc in accs]
                o_ref[l, :, cols] = jnp.concatenate(rows + [jnp.zeros((8 - n_cond, MOD_COL_BLOCK), F32)], axis=0)
        if n + MOD_SLOTS < len(slabs):
            slab_copy(n + MOD_SLOTS).start()


def _full(shape, **kw):
    zeros = (0,) * len(shape)
    return pl.BlockSpec(shape, lambda i: zeros, **kw)


def _weight_specs(latent, w_in, w_out):
    if latent:
        single = pl.Buffered(1)
        return [_full(w_in.shape, pipeline_mode=single), _full(w_out.shape, pipeline_mode=single)]
    return [pl.BlockSpec(memory_space=pl.ANY), pl.BlockSpec(memory_space=pl.ANY)]


def _weight_scratch(w_in, w_out):
    assert w_in.shape[0] % W_SLAB_ROWS == 0 and w_out.shape[0] % W_SLAB_ROWS == 0
    n_slabs = max(w_in.shape[0], w_out.shape[0]) // W_SLAB_ROWS
    return [pltpu.VMEM(w_in.shape, BF16), pltpu.VMEM(w_out.shape, BF16),
            pltpu.VMEM((n_slabs, W_SLAB_ROWS, max(w_in.shape[1], w_out.shape[1])), F32),
            pltpu.SemaphoreType.DMA((n_slabs,)), pltpu.SemaphoreType.DMA((2,))]


def _rope_tables(seq):
    t = np.arange(seq)
    n_freq = HEAD_DIM // 4
    freqs = ROPE_THETA ** (-np.arange(n_freq, dtype=np.float64) / n_freq)
    ang_row = (t // GRID_W)[:, None] * freqs
    ang_col = (t % GRID_W)[:, None] * freqs
    ang = np.concatenate([ang_row, ang_row, ang_col, ang_col], axis=1)
    sign = np.concatenate([-np.ones(n_freq), np.ones(n_freq)] * 2)[None, :]
    cos = np.tile(np.cos(ang), (1, 2)).astype(np.float32)
    sin = np.tile(np.sin(ang) * sign, (1, 2)).astype(np.float32)
    chunked_t = lambda a: a.reshape(seq // ROW_CHUNK, ROW_CHUNK, LANES).transpose(0, 2, 1)
    return jnp.asarray(cos), jnp.asarray(sin), jnp.asarray(chunked_t(cos)), jnp.asarray(chunked_t(sin))


def _modulation(c, c_ctx, w_mod, b_mod):
    depth = w_mod.shape[0]
    n_cond = 1 + c.shape[0]
    cv = jnp.concatenate([c_ctx[None, :], c, jnp.zeros((8 - n_cond, D_MODEL), F32)], axis=0)
    assert D_MODEL % MOD_SLAB_ROWS == 0 and (3 * D_MODEL) % MOD_COL_BLOCK == 0
    return pl.pallas_call(
        functools.partial(_mod_kernel, n_cond),
        grid=(1,),
        in_specs=[_full(cv.shape), pl.BlockSpec(memory_space=pl.ANY), _full(b_mod.shape)],
        out_specs=_full((depth, 8, 3 * D_MODEL)),
        out_shape=jax.ShapeDtypeStruct((depth, 8, 3 * D_MODEL), F32),
        scratch_shapes=[pltpu.VMEM((n_cond, D_MODEL, LANES), F32),
                        pltpu.VMEM((MOD_SLOTS, MOD_SLAB_ROWS, 3 * D_MODEL), F32),
                        pltpu.VMEM((n_cond, 8, 3 * D_MODEL), F32),
                        pltpu.SemaphoreType.DMA((MOD_SLOTS,))],
        compiler_params=pltpu.CompilerParams(dimension_semantics=("arbitrary",)),
        name="adaln_modulation",
    )(cv, w_mod, b_mod)


def _even_layer(x, mod, layer, w_in, w_out, q_norm, k_norm, sink, ln_g, ln_b, latent, seq, n_rows, alpha, extras=()):
    total = x.shape[0]
    grid = (total // n_rows,)
    single = pl.Buffered(1)
    norms = jnp.concatenate([jnp.broadcast_to(jnp.tile(k_norm, 2)[:, None], (LANES, ROW_CHUNK)),
                             jnp.broadcast_to(jnp.tile(q_norm, 2 * ROW_CHUNK // LANES)[None, :], (8, ROW_CHUNK))], axis=0)

    row_blk = lambda width: pl.BlockSpec((n_rows, width), lambda i: (i, 0))
    in_specs = [row_blk(D_MODEL),
                pl.BlockSpec((1, 8, 3 * D_MODEL), lambda i: (layer, 0, 0)),
                *_weight_specs(latent, w_in, w_out),
                _full(norms.shape),
                pl.BlockSpec(memory_space=pltpu.SMEM),
                _full(ln_g.shape), _full(ln_b.shape)]
    args = [x, mod, w_in, w_out, norms, sink, ln_g, ln_b]
    y_shape = jax.ShapeDtypeStruct((total, D_MODEL), F32)
    n_blocks = n_rows // ROW_CHUNK
    if latent:
        cos, sin, cos_t, sin_t, cakt, cav, cbkt, cbv = extras
        n_past = cav.shape[2]
        in_specs += [_full(cos.shape, pipeline_mode=single), _full(sin.shape, pipeline_mode=single),
                     _full(cos_t.shape, pipeline_mode=single), _full(sin_t.shape, pipeline_mode=single)]
        in_specs += [pl.BlockSpec((1, LANES, n_past), lambda i: (i, 0, 0))] * 4
        args += [cos, sin, cos_t, sin_t, cakt, cav, cbkt, cbv]
        out_specs = row_blk(D_MODEL)
        out_shape = y_shape
        n_keys = seq + n_past
    else:
        kv_blk = pl.BlockSpec((n_blocks, LANES, ROW_CHUNK), lambda i: (i, 0, 0))
        hbm = pl.BlockSpec(memory_space=pl.ANY)
        out_specs = [row_blk(D_MODEL)] + [kv_blk] * 4 + [hbm, hbm]
        out_shape = ([y_shape] + [jax.ShapeDtypeStruct((total // seq, LANES, seq), F32)] * 4
                     + [jax.ShapeDtypeStruct(w_in.shape, BF16), jax.ShapeDtypeStruct(w_out.shape, BF16)])
        n_keys = n_rows
    n_kchunks = n_keys // ROW_CHUNK
    n_cols = n_keys if latent else ROW_CHUNK
    scratch = [pltpu.VMEM((n_rows, D_MODEL), BF16),
               pltpu.VMEM((n_rows, 512), BF16), pltpu.VMEM((n_rows, 512), BF16),
               pltpu.VMEM((4, n_kchunks, LANES, ROW_CHUNK), BF16), pltpu.VMEM((4, n_keys, LANES), BF16),
               pltpu.VMEM((4, n_keys // WINDOW, LANES, WINDOW), BF16), pltpu.VMEM((4, n_keys, LANES), BF16),
               pltpu.VMEM((n_rows, D_MODEL), F32),
               pltpu.VMEM((2, 2, ROW_CHUNK, n_cols), F32),
               pltpu.VMEM((2, 2, ROW_CHUNK, n_cols), BF16),
               pltpu.VMEM((2, ROW_CHUNK, LANES), F32),
               pltpu.VMEM((2 * LANES, D_MODEL), BF16),
               pltpu.VMEM((D_MODEL, 2 * LANES), BF16)]
    if latent:
        scratch.append(pltpu.VMEM((1 + 2 * WINDOW // ROW_CHUNK, ROW_CHUNK, ROW_CHUNK), F32))
    else:
        scratch += _weight_scratch(w_in, w_out)
    return pl.pallas_call(
        functools.partial(_even_kernel, latent, layer, n_rows, seq, alpha),
        grid=grid, in_specs=in_specs, out_specs=out_specs, out_shape=out_shape,
        scratch_shapes=scratch,
        compiler_params=pltpu.CompilerParams(dimension_semantics=("arbitrary",), vmem_limit_bytes=VMEM_LIMIT),
        name="even_layer_latent" if latent else "even_layer_context",
    )(*args)


def _odd_layer(x, mod, layer, w_in, w_out, lams, sub, ln_g, ln_b, latent, seq, n_rows, alpha, lam_init, extras=()):
    total = x.shape[0]
    grid = (total // n_rows,)
    row_blk = lambda width: pl.BlockSpec((n_rows, width), lambda i: (i, 0))
    single = pl.Buffered(1)
    in_specs = [row_blk(D_MODEL),
                pl.BlockSpec((1, 8, 3 * D_MODEL), lambda i: (layer, 0, 0)),
                *_weight_specs(latent, w_in, w_out),
                _full((1, HEAD_DIM)), _full((1, HEAD_DIM)), _full((1, HEAD_DIM)), _full((1, HEAD_DIM)),
                _full((1, LANES)),
                _full(ln_g.shape), _full(ln_b.shape)]
    args = [x, mod, w_in, w_out, *lams, sub, ln_g, ln_b]
    y_shape = jax.ShapeDtypeStruct((total, D_MODEL), F32)
    n_heads = D_MODEL // LANES
    n_blocks = n_rows // ROW_CHUNK
    if latent:
        cos, sin, cck, ccv = extras
        n_past = cck.shape[2]
        in_specs += [_full(cos.shape, pipeline_mode=single), _full(sin.shape, pipeline_mode=single)]
        in_specs += [pl.BlockSpec(memory_space=pl.ANY)] * 2
        args += [cos, sin, cck, ccv]
        out_specs = row_blk(D_MODEL)
        out_shape = y_shape
        n_keys = seq + n_past
    else:
        hbm = pl.BlockSpec(memory_space=pl.ANY)
        out_specs = [row_blk(D_MODEL), hbm, hbm, hbm, hbm]
        out_shape = ([y_shape] + [jax.ShapeDtypeStruct((total // seq, 1, seq, n_heads, LANES), F32)] * 2
                     + [jax.ShapeDtypeStruct(w_in.shape, BF16), jax.ShapeDtypeStruct(w_out.shape, BF16)])
        n_keys = n_rows
    n_cols = n_keys if latent else ROW_CHUNK
    scratch = [pltpu.VMEM((n_rows, D_MODEL), BF16),
               pltpu.VMEM((n_rows, D_MODEL), BF16),
               pltpu.VMEM((2, n_keys, D_MODEL), BF16),
               pltpu.VMEM((n_keys, D_MODEL), BF16),
               pltpu.VMEM((n_rows, D_MODEL), F32),
               pltpu.VMEM((2, 2, ROW_CHUNK, n_cols), F32),
               pltpu.VMEM((2, 2, ROW_CHUNK, n_cols), BF16)]
    if latent:
        scratch += [pltpu.VMEM((2, n_past, D_MODEL), F32), pltpu.SemaphoreType.DMA((2,))]
    else:
        scratch += [pltpu.VMEM((n_blocks, 2, ROW_CHUNK, D_MODEL), F32),
                    pltpu.SemaphoreType.DMA((n_blocks, 2))]
        scratch += _weight_scratch(w_in, w_out)
    return pl.pallas_call(
        functools.partial(_odd_kernel, latent, layer, n_rows, seq, alpha, lam_init),
        grid=grid, in_specs=in_specs, out_specs=out_specs, out_shape=out_shape,
        scratch_shapes=scratch,
        compiler_params=pltpu.CompilerParams(dimension_semantics=("arbitrary",), vmem_limit_bytes=VMEM_LIMIT),
        name="odd_layer_latent" if latent else "odd_layer_context",
    )(*args)


def kernel(x_prompt, x_sample, cache_a_k, cache_a_v, cache_b_k, cache_b_v, cache_c_k, cache_c_v, c, c_ctx,
           w_mod, b_mod, ln_g, ln_b, w_in_even, w_out_even, q_norm_a, k_norm_a, sink_b, w_in_odd, w_out_odd,
           lambda_q1, lambda_k1, lambda_q2, lambda_k2, subln_c):
    depth = w_mod.shape[0]
    batch, seq, _ = x_prompt.shape
    dec_batch, dec_seq, _ = x_sample.shape
    n_past = cache_a_k.shape[2]
    alpha = (2 * depth) ** 0.25
    assert seq == ROW_CHUNK and n_past % ROW_CHUNK == 0 and dec_seq % ROW_CHUNK == 0

    mod = _modulation(c, c_ctx, w_mod, b_mod)
    cos, sin, cos_t, sin_t = _rope_tables(dec_seq)

    bf16_weights = {}

    def run(x, latent, n_batch, s, rows_even, rows_odd):
        kv = {"a_k": [], "a_v": [], "b_k": [], "b_v": [], "c_k": [], "c_v": []}
        for l in range(depth):
            if l % 2 == 0:
                e = l // 2
                extras = ()
                if latent:
                    k_t = lambda t: t[:, e].transpose(0, 2, 3, 1).reshape(n_batch, LANES, n_past)
                    extras = (cos, sin, cos_t, sin_t,
                              k_t(cache_a_k), k_t(cache_a_v), k_t(cache_b_k), k_t(cache_b_v))
                w_in, w_out = bf16_weights[l] if latent else (w_in_even[e], w_out_even[e])
                res = _even_layer(x, mod, l, w_in, w_out, q_norm_a[e], k_norm_a[e],
                                  sink_b[e], ln_g, ln_b, latent, s, rows_even, alpha, extras)
                if latent:
                    x = res
                else:
                    x = res[0]
                    bf16_weights[l] = res[5:7]
                    for name, t in zip(("a_k", "a_v", "b_k", "b_v"), res[1:5]):
                        kv[name].append(t.reshape(n_batch, 2, HEAD_DIM, s).transpose(0, 3, 1, 2))
            else:
                o = l // 2
                lam_init = 0.8 - 0.6 * math.exp(-0.3 * l)
                extras = (cos, sin, cache_c_k, cache_c_v) if latent else ()
                lams = [t[o][None, :] for t in (lambda_q1, lambda_k1, lambda_q2, lambda_k2)]
                w_in, w_out = bf16_weights[l] if latent else (w_in_odd[o], w_out_odd[o])
                res = _odd_layer(x, mod, l, w_in, w_out, lams,
                                 subln_c[o][None, :], ln_g, ln_b, latent, s, rows_odd, alpha, lam_init, extras)
                if latent:
                    x = res
                else:
                    x = res[0]
                    bf16_weights[l] = res[3:5]
                    kv["c_k"].append(res[1][:, 0])
                    kv["c_v"].append(res[2][:, 0])
        return x, kv

    y_ctx, kv = run(x_prompt.reshape(batch * seq, D_MODEL), False, batch, seq, 1024, 512)
    y_lat, _ = run(x_sample.reshape(dec_batch * dec_seq, D_MODEL), True, dec_batch, dec_seq, dec_seq, dec_seq)

    stack = lambda name: jnp.stack(kv[name], axis=1)
    return (y_ctx.reshape(batch, seq, D_MODEL), y_lat.reshape(dec_batch, dec_seq, D_MODEL),
            stack("a_k"), stack("a_v"), stack("b_k"), stack("b_v"), stack("c_k"), stack("c_v"))
```

```python
import functools
import math

import jax
import jax.numpy as jnp
import numpy as np
from jax import lax
from jax.experimental import pallas as pl
from jax.experimental.pallas import tpu as pltpu

F32 = jnp.float32
BF16 = jnp.bfloat16

D_MODEL = 1024
HEAD_DIM = 64
GRID_W = 64
WINDOW = 128
ROPE_THETA = 10000.0
EPS = 1e-6
NEG_INF = -1e30
LOG2E = 1.4426950408889634
Q_SCALE = HEAD_DIM ** -0.5 * LOG2E
LANES = 128
ROW_CHUNK = 256
SOFTMAX_VREGS = 40
VMEM_LIMIT = 60000 * 1024
W_SLAB_ROWS = 128


def _silu(x):
    return x / (1.0 + jnp.exp(-x))


def _dot(a, b):
    return jnp.dot(a, b, preferred_element_type=F32)


def _dot_nt(a, b):
    return lax.dot_general(a, b, (((1,), (1,)), ((), ())), preferred_element_type=F32)


def _lane_iota(rows):
    return lax.broadcasted_iota(jnp.int32, (rows, LANES), 1)


def _chunk_rows(i):
    if isinstance(i, int):
        return pl.ds(i * ROW_CHUNK, ROW_CHUNK)
    return pl.ds(pl.multiple_of(i * ROW_CHUNK, ROW_CHUNK), ROW_CHUNK)


def _softmax_rows(n_cols):
    rows = 8
    while rows * 2 * n_cols <= SOFTMAX_VREGS * 1024 and rows * 2 <= ROW_CHUNK:
        rows *= 2
    return rows


def _rope(a, cos, sin_signed):
    lane = _lane_iota(a.shape[0])
    fwd = pltpu.roll(a, LANES - 16, 1)
    bwd = pltpu.roll(a, 16, 1)
    partner = jnp.where((lane & 16) == 0, fwd, bwd)
    return a * cos + partner * sin_signed


def _rope_t(a, cos_t, sin_t):
    blocks = [a[16 * b:16 * (b + 1), :] for b in range(a.shape[0] // 16)]
    partner = jnp.concatenate([blocks[b ^ 1] for b in range(len(blocks))], axis=0)
    return a * cos_t + partner * sin_t


def _store_kt_variants(scr, chunk, kt):
    width = scr.shape[-1]
    per_block = kt.shape[1] // width
    zero = jnp.zeros((HEAD_DIM, kt.shape[1]), F32)
    for j in range(2):
        kj = kt[HEAD_DIM * j:HEAD_DIM * (j + 1), :]
        for par, full in enumerate((jnp.concatenate([kj, zero], axis=0), jnp.concatenate([zero, kj], axis=0))):
            full = full.astype(BF16)
            for c in range(per_block):
                scr[2 * j + par, chunk * per_block + c] = full[:, width * c:width * (c + 1)]


def _store_v_variants(scr, rows, a):
    lane = _lane_iota(a.shape[0])
    lo = lane < HEAD_DIM
    swapped = pltpu.roll(a, HEAD_DIM, 1)
    one = jnp.ones_like(a)
    scr[0, rows, :] = jnp.where(lo, a, one).astype(BF16)
    scr[1, rows, :] = jnp.where(lo, one, swapped).astype(BF16)
    scr[2, rows, :] = jnp.where(lo, swapped, one).astype(BF16)
    scr[3, rows, :] = jnp.where(lo, one, a).astype(BF16)


def _layer_norm_rows(z, g, b):
    mu = jnp.mean(z, axis=-1, keepdims=True)
    zc = z - mu
    var = jnp.mean(zc * zc, axis=-1, keepdims=True)
    return zc * lax.rsqrt(var + EPS) * g + b


def _modulated(x_ref, mod_ref, mod_row, rows):
    shift = mod_ref[0, pl.ds(mod_row, 1), 0:D_MODEL]
    scale = mod_ref[0, pl.ds(mod_row, 1), D_MODEL:2 * D_MODEL]
    return (x_ref[rows, :] * (1.0 + scale) + shift).astype(BF16)


def _out_proj_norm(x_ref, mod_ref, mod_row, attn_scr, w_out_ref, lng_ref, lnb_ref, layer, y_ref, n_rows, alpha):
    gate = mod_ref[0, pl.ds(mod_row, 1), 2 * D_MODEL:3 * D_MODEL]
    g = lng_ref[layer:layer + 1, :]
    b = lnb_ref[layer:layer + 1, :]

    def body(i, carry):
        rows = _chunk_rows(i)
        out = _dot(attn_scr[rows, :], w_out_ref[...])
        z = alpha * x_ref[rows, :] + gate * out
        y_ref[rows, :] = _layer_norm_rows(z, g, b)
        return carry

    lax.fori_loop(0, n_rows // ROW_CHUNK, body, 0, unroll=True)


class _ContextWeights:
    def __init__(self, step, w_in, w_out, stage, sems, out_sems):
        self.step, self.w_in, self.w_out, self.stage, self.sems = step, w_in, w_out, stage, sems
        self.out_copies = [pltpu.make_async_copy(w[1], w[2], out_sems.at[n]) for n, w in enumerate((w_in, w_out))]

    def _slab_copies(self, w_hbm):
        n_cols = w_hbm.shape[1]
        return [pltpu.make_async_copy(w_hbm.at[pl.ds(s * W_SLAB_ROWS, W_SLAB_ROWS), :],
                                      self.stage.at[s, :, pl.ds(0, n_cols)], self.sems.at[s])
                for s in range(w_hbm.shape[0] // W_SLAB_ROWS)]

    def _cast(self, w_hbm, w_scr):
        n_cols = w_hbm.shape[1]
        for s, copy in enumerate(self._slab_copies(w_hbm)):
            copy.wait()
            w_scr[pl.ds(s * W_SLAB_ROWS, W_SLAB_ROWS), :] = self.stage[s, :, 0:n_cols].astype(BF16)

    def load_in_proj(self):
        @pl.when(self.step == 0)
        def _():
            for copy in self._slab_copies(self.w_in[0]):
                copy.start()
            self._cast(self.w_in[0], self.w_in[1])
            self.out_copies[0].start()
            for copy in self._slab_copies(self.w_out[0]):
                copy.start()

    def load_out_proj(self):
        @pl.when(self.step == 0)
        def _():
            self._cast(self.w_out[0], self.w_out[1])
            self.out_copies[1].start()

    def finish(self):
        @pl.when(self.step == 0)
        def _():
            for copy in self.out_copies:
                copy.wait()


def _run_pipeline(n_items, stages):
    for u in range(n_items + len(stages) - 1):
        for k, stage in enumerate(stages):
            t = u - k
            if 0 <= t < n_items:
                stage(t, t % 2)


def _attend_blocks(block_stages, n_blocks, n_items, unrolled):
    assert n_items % 2 == 0
    if unrolled:
        per_block = [block_stages(i) for i in range(n_blocks)]
        stages = [lambda g, slot, k=k: per_block[g // n_items][k](g % n_items, slot) for k in range(3)]
        _run_pipeline(n_blocks * n_items, stages)
    else:
        def body(i, carry):
            _run_pipeline(n_items, block_stages(i))
            return carry

        lax.fori_loop(0, n_blocks, body, 0)


def _even_kernel(latent, layer, n_rows, seq, alpha, *refs):
    if latent:
        (x_ref, mod_ref, w_in_ref, w_out_ref, norms_ref, sink_ref, lng_ref, lnb_ref,
         cos_ref, sin_ref, cost_ref, sint_ref, cakt_ref, cav_ref, cbkt_ref, cbv_ref,
         y_ref,
         attn_scr, qa_scr, qb_scr, ka_scr, va_scr, kb_scr, vb_scr, g_scr, s_scr, p_scr, es_scr, wkt_scr, wv_scr,
         bias_scr) = refs
    else:
        (x_ref, mod_ref, w_in_hbm, w_out_hbm, norms_ref, sink_ref, lng_ref, lnb_ref,
         y_ref, nakt_ref, navt_ref, nbkt_ref, nbvt_ref, w_in_bf_hbm, w_out_bf_hbm,
         attn_scr, qa_scr, qb_scr, ka_scr, va_scr, kb_scr, vb_scr, g_scr, s_scr, p_scr, es_scr, wkt_scr,
         wv_scr, w_in_ref, w_out_ref, w_stage, w_sems, w_out_sems) = refs

    step = pl.program_id(0)
    if not latent:
        weights = _ContextWeights(step, (w_in_hbm, w_in_ref, w_in_bf_hbm), (w_out_hbm, w_out_ref, w_out_bf_hbm),
                                  w_stage, w_sems, w_out_sems)
        weights.load_in_proj()
    mod_row = step + 1 if latent else 0

    col_ka, col_va, col_kb, col_vb = 512, 640, 1792, 1920

    @pl.when(step == 0)
    def _():
        for r, c0 in enumerate((col_ka, col_kb)):
            wkt_scr[LANES * r:LANES * (r + 1), :] = w_in_ref[:, c0:c0 + LANES].T
        wv_scr[:, 0:LANES] = w_in_ref[:, col_va:col_va + LANES]
        wv_scr[:, LANES:2 * LANES] = w_in_ref[:, col_vb:col_vb + LANES]

    n_lat_chunks = seq // ROW_CHUNK
    if latent:
        n_past = cav_ref.shape[2]
        past_rows = pl.ds(seq, n_past)
        _store_kt_variants(ka_scr, n_lat_chunks, cakt_ref[0])
        _store_kt_variants(kb_scr, n_lat_chunks, cbkt_ref[0])
        _store_v_variants(va_scr, past_rows, cav_ref[0].T)
        _store_v_variants(vb_scr, past_rows, cbv_ref[0].T)

    knt = norms_ref[0:LANES, :]
    qn = norms_ref[LANES:LANES + 1, 0:LANES]

    def proj(i, carry):
        rows = _chunk_rows(i)
        hh = _modulated(x_ref, mod_ref, mod_row, rows)
        if latent:
            cos = cos_ref[rows, :]
            sin = sin_ref[rows, :]
            rot = lambda a: _rope(a, cos, sin)
            rot_t = lambda a: _rope_t(a, cost_ref[i], sint_ref[i])
        else:
            rot = rot_t = lambda a: a

        acc = _dot(hh, w_in_ref[:, 0:512])
        lo_lanes = _lane_iota(ROW_CHUNK) < HEAD_DIM
        for j in range(4):
            a = acc[:, LANES * j:LANES * (j + 1)]
            sq = a * a
            first = jnp.sum(jnp.where(lo_lanes, sq, 0.0), axis=1, keepdims=True)
            second = jnp.sum(jnp.where(lo_lanes, 0.0, sq), axis=1, keepdims=True)
            ms = jnp.where(lo_lanes, first, second) * (1.0 / HEAD_DIM)
            a = rot(a * lax.rsqrt(ms + EPS) * qn)
            qa_scr[rows, LANES * j:LANES * (j + 1)] = (a * Q_SCALE).astype(BF16)
        acc = _dot(hh, w_in_ref[:, 1280:1792])
        for j in range(4):
            a = rot(acc[:, LANES * j:LANES * (j + 1)])
            qb_scr[rows, LANES * j:LANES * (j + 1)] = (a * Q_SCALE).astype(BF16)
        g_scr[rows, 0:512] = _silu(_dot(hh, w_in_ref[:, 768:1280]))
        g_scr[rows, 512:1024] = _silu(_dot(hh, w_in_ref[:, 2048:2560]))
        v = _dot(hh, wv_scr[...])
        _store_v_variants(va_scr, rows, v[:, 0:LANES])
        _store_v_variants(vb_scr, rows, v[:, LANES:2 * LANES])

        kt = _dot_nt(wkt_scr[0:2 * LANES, :], hh)
        heads = [kt[HEAD_DIM * h:HEAD_DIM * (h + 1), :] for h in range(2)]
        kat = jnp.concatenate([blk * lax.rsqrt(jnp.mean(blk * blk, axis=0, keepdims=True) + EPS) for blk in heads],
                              axis=0) * knt
        kbt = kt[LANES:2 * LANES, :]
        if not latent:
            vt = v.T
            nakt_ref[i] = kat
            nbkt_ref[i] = kbt
            navt_ref[i] = vt[0:LANES, :]
            nbvt_ref[i] = vt[LANES:2 * LANES, :]
        _store_kt_variants(ka_scr, i, rot_t(kat))
        _store_kt_variants(kb_scr, i, rot_t(kbt))
        return carry

    lax.fori_loop(0, n_rows // ROW_CHUNK, proj, 0, unroll=2)

    sinks = [sink_ref[h] * LOG2E for h in range(8)]
    ck = ROW_CHUNK
    bk = kb_scr.shape[-1]
    win = ROW_CHUNK + 2 * WINDOW
    n_items = 8

    def block_stages(i):
        rows = _chunk_rows(i)
        if latent:
            a_chunks = list(range(n_lat_chunks + n_past // ck))
            a_keys = pl.ds(0, seq + n_past)
            w0 = jnp.clip(i * (ck // bk) - WINDOW // bk, 0, (seq - win) // bk)
            win_rows = pl.ds(pl.multiple_of(w0 * bk, bk), win)
            dist = (lax.broadcasted_iota(jnp.int32, (ROW_CHUNK, ck), 1)
                    - lax.broadcasted_iota(jnp.int32, (ROW_CHUNK, ck), 0))
            for c in range(win // ck):
                off = w0 * bk + c * ck - i * ck
                bias_scr[c] = jnp.where(jnp.abs(dist + off) <= WINDOW, 0.0, NEG_INF).astype(F32)
            b_first = [w0 + c * (ck // bk) for c in range(win // ck)] + [seq // bk]
            n_biased = win // ck
            b_cols = win + n_past
        else:
            a_chunks = [i]
            a_keys = rows
            b_first = [i * (ck // bk)]
            n_biased = 0
            b_cols = ck
        a_cols = len(a_chunks) * ck

        def qk(t, slot):
            p, branch = divmod(t, 2)
            cols = slice(LANES * p, LANES * (p + 1))
            kvh = p // 2
            q = (qb_scr if branch else qa_scr)[rows, cols]
            for par in (0, 1):
                var = 2 * kvh + par
                if branch:
                    tiles = [jnp.concatenate([kb_scr[var, first + d] for d in range(ck // bk)], axis=1)
                             for first in b_first]
                else:
                    tiles = [ka_scr[var, chunk] for chunk in a_chunks]
                for c, kt in enumerate(tiles):
                    s = _dot(q, kt)
                    if branch and c < n_biased:
                        s = s + bias_scr[c]
                    s_scr[slot, par, :, c * ck:(c + 1) * ck] = s

        def softmax(t, slot):
            p, branch = divmod(t, 2)
            n_cols = b_cols if branch else a_cols
            rb = _softmax_rows(n_cols)
            for par in (0, 1):
                for r in range(ROW_CHUNK // rb):
                    sub = slice(r * rb, (r + 1) * rb)
                    s = s_scr[slot, par, sub, 0:n_cols]
                    m = jnp.max(s, axis=1, keepdims=True)
                    if branch:
                        sink = sinks[2 * p + par]
                        m = jnp.maximum(m, sink)
                        es_scr[slot, sub, HEAD_DIM * par:HEAD_DIM * (par + 1)] = jnp.broadcast_to(
                            jnp.exp2(sink - m), (rb, HEAD_DIM))
                    p_scr[slot, par, sub, 0:n_cols] = jnp.exp2((s - m).astype(BF16))

        def pv(t, slot):
            p, branch = divmod(t, 2)
            kvh = p // 2
            v_scr = vb_scr if branch else va_scr
            accs = []
            for par in (0, 1):
                var = 2 * kvh + par
                if latent and branch:
                    n_loc = win
                    accs.append(_dot(p_scr[slot, par, :, 0:n_loc], v_scr[var, win_rows, :])
                                + _dot(p_scr[slot, par, :, n_loc:b_cols], v_scr[var, past_rows, :]))
                else:
                    accs.append(_dot(p_scr[slot, par, :, 0:a_cols], v_scr[var, a_keys, :]))
            lo = _lane_iota(ROW_CHUNK) < HEAD_DIM
            denom = pltpu.roll(jnp.where(lo, accs[1], accs[0]), HEAD_DIM, 1)
            if branch:
                denom = denom + es_scr[slot]
            o = jnp.where(lo, accs[0], accs[1]) / denom
            ocols = slice(512 * branch + LANES * p, 512 * branch + LANES * (p + 1))
            attn_scr[rows, ocols] = (o * g_scr[rows, ocols]).astype(BF16)

        return qk, softmax, pv

    _attend_blocks(block_stages, n_rows // ROW_CHUNK, n_items, unrolled=not latent)

    if not latent:
        weights.load_out_proj()
    _out_proj_norm(x_ref, mod_ref, mod_row, attn_scr, w_out_ref, lng_ref, lnb_ref, layer, y_ref, n_rows, alpha)
    if not latent:
        weights.finish()


def _odd_kernel(latent, layer, n_rows, seq, alpha, lam_init, *refs):
    if latent:
        (x_ref, mod_ref, w_in_ref, w_out_ref, lq1_ref, lk1_ref, lq2_ref, lk2_ref, sub_ref, lng_ref, lnb_ref,
         cos_ref, sin_ref, cck_hbm, ccv_hbm,
         y_ref,
         attn_scr, q_scr, k_scr, v_scr, g_scr, s_scr, p_scr, past_stage, past_sems) = refs
    else:
        (x_ref, mod_ref, w_in_hbm, w_out_hbm, lq1_ref, lk1_ref, lq2_ref, lk2_ref, sub_ref, lng_ref, lnb_ref,
         y_ref, nck_hbm, ncv_hbm, w_in_bf_hbm, w_out_bf_hbm,
         attn_scr, q_scr, k_scr, v_scr, g_scr, s_scr, p_scr, kv_stage, kv_sems,
         w_in_ref, w_out_ref, w_stage, w_sems, w_out_sems) = refs

    step = pl.program_id(0)
    if not latent:
        weights = _ContextWeights(step, (w_in_hbm, w_in_ref, w_in_bf_hbm), (w_out_hbm, w_out_ref, w_out_bf_hbm),
                                  w_stage, w_sems, w_out_sems)
        weights.load_in_proj()
    mod_row = step + 1 if latent else 0

    n_heads = D_MODEL // LANES
    n_blocks = n_rows // ROW_CHUNK
    lo = _lane_iota(ROW_CHUNK) < HEAD_DIM

    def kv_out_copies(blk):
        elem = step * n_blocks + blk
        return [pltpu.make_async_copy(kv_stage.at[blk, t, :, pl.ds(LANES * h, LANES)],
                                      out.at[elem, 0, :, h, :], kv_sems.at[blk, t])
                for t, out in enumerate((nck_hbm, ncv_hbm)) for h in range(n_heads)]

    def store_k(rows, h, a):
        cols = slice(LANES * h, LANES * (h + 1))
        zero = jnp.zeros_like(a)
        k_scr[0, rows, cols] = jnp.where(lo, a, zero).astype(BF16)
        k_scr[1, rows, cols] = jnp.where(lo, zero, a).astype(BF16)

    if latent:
        n_past = cck_hbm.shape[2]
        past = pl.ds(seq, n_past)
        past_copies = [pltpu.make_async_copy(cache.at[step, layer // 2, :, h, :],
                                             past_stage.at[t, :, pl.ds(LANES * h, LANES)], past_sems.at[t])
                       for t, cache in enumerate((cck_hbm, ccv_hbm)) for h in range(n_heads)]
        for copy in past_copies:
            copy.start()

    def proj(i, carry):
        rows = _chunk_rows(i)
        hh = _modulated(x_ref, mod_ref, mod_row, rows)
        if latent:
            cos = cos_ref[rows, :]
            sin = sin_ref[rows, :]
            rot = lambda a: _rope(a, cos, sin)
        else:
            rot = lambda a: a
        for half in range(2):
            acc = _dot(hh, w_in_ref[:, 512 * half:512 * (half + 1)])
            for j in range(4):
                a = rot(acc[:, LANES * j:LANES * (j + 1)])
                cols = slice(512 * half + LANES * j, 512 * half + LANES * (j + 1))
                q_scr[rows, cols] = (a * Q_SCALE).astype(BF16)
        for half in range(2):
            acc = _dot(hh, w_in_ref[:, 1024 + 512 * half:1024 + 512 * (half + 1)])
            if not latent:
                kv_stage[i, 0, :, 512 * half:512 * (half + 1)] = acc
            for j in range(4):
                store_k(rows, 4 * half + j, rot(acc[:, LANES * j:LANES * (j + 1)]))
        for half in range(2):
            acc = _dot(hh, w_in_ref[:, 2048 + 512 * half:2048 + 512 * (half + 1)])
            if not latent:
                kv_stage[i, 1, :, 512 * half:512 * (half + 1)] = acc
            v_scr[rows, 512 * half:512 * (half + 1)] = acc.astype(BF16)
        if not latent:
            for copy in kv_out_copies(i):
                copy.start()
        for half in range(2):
            acc = _dot(hh, w_in_ref[:, 3072 + 512 * half:3072 + 512 * (half + 1)])
            g_scr[rows, 512 * half:512 * (half + 1)] = _silu(acc)
        return carry

    if latent:
        lax.fori_loop(0, n_blocks, proj, 0, unroll=2)
    else:
        for blk in range(n_blocks):
            proj(blk, 0)

    if latent:
        for copy in past_copies:
            copy.wait()
        for h in range(n_heads):
            store_k(past, h, past_stage[0, :, LANES * h:LANES * (h + 1)])
        v_scr[past, :] = past_stage[1].astype(BF16)

    lam = (jnp.exp(jnp.sum(lq1_ref[...] * lk1_ref[...], axis=1, keepdims=True))
           - jnp.exp(jnp.sum(lq2_ref[...] * lk2_ref[...], axis=1, keepdims=True)) + lam_init)
    sub = sub_ref[...] * (1.0 - lam_init)
    n_keys = seq + n_past if latent else ROW_CHUNK
    rb = _softmax_rows(n_keys)
    ones = jnp.ones((n_keys, LANES), BF16)

    def block_stages(i):
        rows = _chunk_rows(i)
        keys = pl.ds(0, n_keys) if latent else rows

        def qk(h, slot):
            cols = slice(LANES * h, LANES * (h + 1))
            q = q_scr[rows, cols]
            for m in (0, 1):
                s_scr[slot, m] = _dot_nt(q, k_scr[m, keys, cols])

        def softmax(h, slot):
            for m in (0, 1):
                for r in range(ROW_CHUNK // rb):
                    sub_rows = slice(r * rb, (r + 1) * rb)
                    s = s_scr[slot, m, sub_rows, :]
                    top = jnp.max(s, axis=1, keepdims=True)
                    p_scr[slot, m, sub_rows, :] = jnp.exp2((s - top).astype(BF16))

        def pv(h, slot):
            cols = slice(LANES * h, LANES * (h + 1))
            v_ext = jnp.concatenate([v_scr[keys, cols], ones], axis=1)
            maps = []
            for m in (0, 1):
                acc = _dot(p_scr[slot, m], v_ext)
                maps.append(acc[:, 0:LANES] / acc[:, LANES:2 * LANES])
            o = maps[0] - lam * maps[1]
            ms = jnp.mean(o * o, axis=1, keepdims=True)
            o = o * lax.rsqrt(ms + EPS) * sub
            attn_scr[rows, cols] = (o * g_scr[rows, cols]).astype(BF16)

        return qk, softmax, pv

    _attend_blocks(block_stages, n_blocks, n_heads, unrolled=not latent)

    if not latent:
        weights.load_out_proj()
    _out_proj_norm(x_ref, mod_ref, mod_row, attn_scr, w_out_ref, lng_ref, lnb_ref, layer, y_ref, n_rows, alpha)

    if not latent:
        for blk in range(n_blocks):
            for copy in kv_out_copies(blk):
                copy.wait()
        weights.finish()


MOD_SLAB_ROWS = 128
MOD_COL_BLOCK = 1024
MOD_SLOTS = 16


def _mod_kernel(n_cond, cv_ref, w_hbm, b_ref, o_ref, sb_scr, ring, acc_scr, sems):
    depth, n_in, n_out = w_hbm.shape
    sublanes = 8
    slabs_per_layer = n_in // MOD_SLAB_ROWS
    slabs = [(l, rs) for l in range(depth) for rs in range(slabs_per_layer)]

    def slab_copy(n):
        l, rs = slabs[n]
        return pltpu.make_async_copy(w_hbm.at[l, pl.ds(rs * MOD_SLAB_ROWS, MOD_SLAB_ROWS), :],
                                     ring.at[n % MOD_SLOTS], sems.at[n % MOD_SLOTS])

    for n in range(min(MOD_SLOTS, len(slabs))):
        slab_copy(n).start()
    s_t = _silu(cv_ref[...]).T
    for r in range(n_cond):
        sb_scr[r] = jnp.broadcast_to(s_t[:, r:r + 1], (n_in, LANES))

    for n, (l, rs) in enumerate(slabs):
        slab_copy(n).wait()
        for cb in range(n_out // MOD_COL_BLOCK):
            cols = pl.ds(cb * MOD_COL_BLOCK, MOD_COL_BLOCK)
            if rs == 0:
                accs = (jnp.zeros((sublanes, MOD_COL_BLOCK), F32),) * n_cond
            else:
                accs = tuple(acc_scr[r, :, cols] for r in range(n_cond))

            def body(kb, accs, n=n, rs=rs, cols=cols):
                w = ring[n % MOD_SLOTS, pl.ds(pl.multiple_of(kb * sublanes, sublanes), sublanes), cols]
                s_rows = pl.ds(pl.multiple_of(rs * MOD_SLAB_ROWS + kb * sublanes, sublanes), sublanes)
                return tuple(acc + w * jnp.tile(sb_scr[r, s_rows, :], (1, MOD_COL_BLOCK // LANES))
                             for r, acc in enumerate(accs))

            accs = lax.fori_loop(0, MOD_SLAB_ROWS // sublanes, body, accs, unroll=8)
            if rs < slabs_per_layer - 1:
                for r in range(n_cond):
                    acc_scr[r, :, cols] = accs[r]
            else:
                rows = [jnp.sum(acc, axis=0, keepdims=True) + b_ref[l:l + 1, cols] for acc in accs]
                o_ref[l, :, cols] = jnp.concatenate(rows + [jnp.zeros((8 - n_cond, MOD_COL_BLOCK), F32)], axis=0)
        if n + MOD_SLOTS < len(slabs):
            slab_copy(n + MOD_SLOTS).start()


def _full(shape, **kw):
    zeros = (0,) * len(shape)
    return pl.BlockSpec(shape, lambda i: zeros, **kw)


def _weight_specs(latent, w_in, w_out):
    if latent:
        single = pl.Buffered(1)
        return [_full(w_in.shape, pipeline_mode=single), _full(w_out.shape, pipeline_mode=single)]
    return [pl.BlockSpec(memory_space=pl.ANY), pl.BlockSpec(memory_space=pl.ANY)]


def _weight_scratch(w_in, w_out):
    assert w_in.shape[0] % W_SLAB_ROWS == 0 and w_out.shape[0] % W_SLAB_ROWS == 0
    n_slabs = max(w_in.shape[0], w_out.shape[0]) // W_SLAB_ROWS
    return [pltpu.VMEM(w_in.shape, BF16), pltpu.VMEM(w_out.shape, BF16),
            pltpu.VMEM((n_slabs, W_SLAB_ROWS, max(w_in.shape[1], w_out.shape[1])), F32),
            pltpu.SemaphoreType.DMA((n_slabs,)), pltpu.SemaphoreType.DMA((2,))]


def _rope_tables(seq):
    t = np.arange(seq)
    n_freq = HEAD_DIM // 4
    freqs = ROPE_THETA ** (-np.arange(n_freq, dtype=np.float64) / n_freq)
    ang_row = (t // GRID_W)[:, None] * freqs
    ang_col = (t % GRID_W)[:, None] * freqs
    ang = np.concatenate([ang_row, ang_row, ang_col, ang_col], axis=1)
    sign = np.concatenate([-np.ones(n_freq), np.ones(n_freq)] * 2)[None, :]
    cos = np.tile(np.cos(ang), (1, 2)).astype(np.float32)
    sin = np.tile(np.sin(ang) * sign, (1, 2)).astype(np.float32)
    chunked_t = lambda a: a.reshape(seq // ROW_CHUNK, ROW_CHUNK, LANES).transpose(0, 2, 1)
    return jnp.asarray(cos), jnp.asarray(sin), jnp.asarray(chunked_t(cos)), jnp.asarray(chunked_t(sin))


def _modulation(c, c_ctx, w_mod, b_mod):
    depth = w_mod.shape[0]
    n_cond = 1 + c.shape[0]
    cv = jnp.concatenate([c_ctx[None, :], c, jnp.zeros((8 - n_cond, D_MODEL), F32)], axis=0)
    assert D_MODEL % MOD_SLAB_ROWS == 0 and (3 * D_MODEL) % MOD_COL_BLOCK == 0
    return pl.pallas_call(
        functools.partial(_mod_kernel, n_cond),
        grid=(1,),
        in_specs=[_full(cv.shape), pl.BlockSpec(memory_space=pl.ANY), _full(b_mod.shape)],
        out_specs=_full((depth, 8, 3 * D_MODEL)),
        out_shape=jax.ShapeDtypeStruct((depth, 8, 3 * D_MODEL), F32),
        scratch_shapes=[pltpu.VMEM((n_cond, D_MODEL, LANES), F32),
                        pltpu.VMEM((MOD_SLOTS, MOD_SLAB_ROWS, 3 * D_MODEL), F32),
                        pltpu.VMEM((n_cond, 8, 3 * D_MODEL), F32),
                        pltpu.SemaphoreType.DMA((MOD_SLOTS,))],
        compiler_params=pltpu.CompilerParams(dimension_semantics=("arbitrary",)),
        name="adaln_modulation",
    )(cv, w_mod, b_mod)


def _even_layer(x, mod, layer, w_in, w_out, q_norm, k_norm, sink, ln_g, ln_b, latent, seq, n_rows, alpha, extras=()):
    total = x.shape[0]
    grid = (total // n_rows,)
    single = pl.Buffered(1)
    norms = jnp.concatenate([jnp.broadcast_to(jnp.tile(k_norm, 2)[:, None], (LANES, ROW_CHUNK)),
                             jnp.broadcast_to(jnp.tile(q_norm, 2 * ROW_CHUNK // LANES)[None, :], (8, ROW_CHUNK))], axis=0)

    row_blk = lambda width: pl.BlockSpec((n_rows, width), lambda i: (i, 0))
    in_specs = [row_blk(D_MODEL),
                pl.BlockSpec((1, 8, 3 * D_MODEL), lambda i: (layer, 0, 0)),
                *_weight_specs(latent, w_in, w_out),
                _full(norms.shape),
                pl.BlockSpec(memory_space=pltpu.SMEM),
                _full(ln_g.shape), _full(ln_b.shape)]
    args = [x, mod, w_in, w_out, norms, sink, ln_g, ln_b]
    y_shape = jax.ShapeDtypeStruct((total, D_MODEL), F32)
    n_blocks = n_rows // ROW_CHUNK
    if latent:
        cos, sin, cos_t, sin_t, cakt, cav, cbkt, cbv = extras
        n_past = cav.shape[2]
        in_specs += [_full(cos.shape, pipeline_mode=single), _full(sin.shape, pipeline_mode=single),
                     _full(cos_t.shape, pipeline_mode=single), _full(sin_t.shape, pipeline_mode=single)]
        in_specs += [pl.BlockSpec((1, LANES, n_past), lambda i: (i, 0, 0))] * 4
        args += [cos, sin, cos_t, sin_t, cakt, cav, cbkt, cbv]
        out_specs = row_blk(D_MODEL)
        out_shape = y_shape
        n_keys = seq + n_past
    else:
        kv_blk = pl.BlockSpec((n_blocks, LANES, ROW_CHUNK), lambda i: (i, 0, 0))
        hbm = pl.BlockSpec(memory_space=pl.ANY)
        out_specs = [row_blk(D_MODEL)] + [kv_blk] * 4 + [hbm, hbm]
        out_shape = ([y_shape] + [jax.ShapeDtypeStruct((total // seq, LANES, seq), F32)] * 4
                     + [jax.ShapeDtypeStruct(w_in.shape, BF16), jax.ShapeDtypeStruct(w_out.shape, BF16)])
        n_keys = n_rows
    n_kchunks = n_keys // ROW_CHUNK
    n_cols = n_keys if latent else ROW_CHUNK
    scratch = [pltpu.VMEM((n_rows, D_MODEL), BF16),
               pltpu.VMEM((n_rows, 512), BF16), pltpu.VMEM((n_rows, 512), BF16),
               pltpu.VMEM((4, n_kchunks, LANES, ROW_CHUNK), BF16), pltpu.VMEM((4, n_keys, LANES), BF16),
               pltpu.VMEM((4, n_keys // WINDOW, LANES, WINDOW), BF16), pltpu.VMEM((4, n_keys, LANES), BF16),
               pltpu.VMEM((n_rows, D_MODEL), F32),
               pltpu.VMEM((2, 2, ROW_CHUNK, n_cols), F32),
               pltpu.VMEM((2, 2, ROW_CHUNK, n_cols), BF16),
               pltpu.VMEM((2, ROW_CHUNK, LANES), F32),
               pltpu.VMEM((2 * LANES, D_MODEL), BF16),
               pltpu.VMEM((D_MODEL, 2 * LANES), BF16)]
    if latent:
        scratch.append(pltpu.VMEM((1 + 2 * WINDOW // ROW_CHUNK, ROW_CHUNK, ROW_CHUNK), F32))
    else:
        scratch += _weight_scratch(w_in, w_out)
    return pl.pallas_call(
        functools.partial(_even_kernel, latent, layer, n_rows, seq, alpha),
        grid=grid, in_specs=in_specs, out_specs=out_specs, out_shape=out_shape,
        scratch_shapes=scratch,
        compiler_params=pltpu.CompilerParams(dimension_semantics=("arbitrary",), vmem_limit_bytes=VMEM_LIMIT),
        name="even_layer_latent" if latent else "even_layer_context",
    )(*args)


def _odd_layer(x, mod, layer, w_in, w_out, lams, sub, ln_g, ln_b, latent, seq, n_rows, alpha, lam_init, extras=()):
    total = x.shape[0]
    grid = (total // n_rows,)
    row_blk = lambda width: pl.BlockSpec((n_rows, width), lambda i: (i, 0))
    single = pl.Buffered(1)
    in_specs = [row_blk(D_MODEL),
                pl.BlockSpec((1, 8, 3 * D_MODEL), lambda i: (layer, 0, 0)),
                *_weight_specs(latent, w_in, w_out),
                _full((1, HEAD_DIM)), _full((1, HEAD_DIM)), _full((1, HEAD_DIM)), _full((1, HEAD_DIM)),
                _full((1, LANES)),
                _full(ln_g.shape), _full(ln_b.shape)]
    args = [x, mod, w_in, w_out, *lams, sub, ln_g, ln_b]
    y_shape = jax.ShapeDtypeStruct((total, D_MODEL), F32)
    n_heads = D_MODEL // LANES
    n_blocks = n_rows // ROW_CHUNK
    if latent:
        cos, sin, cck, ccv = extras
        n_past = cck.shape[2]
        in_specs += [_full(cos.shape, pipeline_mode=single), _full(sin.shape, pipeline_mode=single)]
        in_specs += [pl.BlockSpec(memory_space=pl.ANY)] * 2
        args += [cos, sin, cck, ccv]
        out_specs = row_blk(D_MODEL)
        out_shape = y_shape
        n_keys = seq + n_past
    else:
        hbm = pl.BlockSpec(memory_space=pl.ANY)
        out_specs = [row_blk(D_MODEL), hbm, hbm, hbm, hbm]
        out_shape = ([y_shape] + [jax.ShapeDtypeStruct((total // seq, 1, seq, n_heads, LANES), F32)] * 2
                     + [jax.ShapeDtypeStruct(w_in.shape, BF16), jax.ShapeDtypeStruct(w_out.shape, BF16)])
        n_keys = n_rows
    n_cols = n_keys if latent else ROW_CHUNK
    scratch = [pltpu.VMEM((n_rows, D_MODEL), BF16),
               pltpu.VMEM((n_rows, D_MODEL), BF16),
               pltpu.VMEM((2, n_keys, D_MODEL), BF16),
               pltpu.VMEM((n_keys, D_MODEL), BF16),
               pltpu.VMEM((n_rows, D_MODEL), F32),
               pltpu.VMEM((2, 2, ROW_CHUNK, n_cols), F32),
               pltpu.VMEM((2, 2, ROW_CHUNK, n_cols), BF16)]
    if latent:
        scratch += [pltpu.VMEM((2, n_past, D_MODEL), F32), pltpu.SemaphoreType.DMA((2,))]
    else:
        scratch += [pltpu.VMEM((n_blocks, 2, ROW_CHUNK, D_MODEL), F32),
                    pltpu.SemaphoreType.DMA((n_blocks, 2))]
        scratch += _weight_scratch(w_in, w_out)
    return pl.pallas_call(
        functools.partial(_odd_kernel, latent, layer, n_rows, seq, alpha, lam_init),
        grid=grid, in_specs=in_specs, out_specs=out_specs, out_shape=out_shape,
        scratch_shapes=scratch,
        compiler_params=pltpu.CompilerParams(dimension_semantics=("arbitrary",), vmem_limit_bytes=VMEM_LIMIT),
        name="odd_layer_latent" if latent else "odd_layer_context",
    )(*args)


def kernel(x_prompt, x_sample, cache_a_k, cache_a_v, cache_b_k, cache_b_v, cache_c_k, cache_c_v, c, c_ctx,
           w_mod, b_mod, ln_g, ln_b, w_in_even, w_out_even, q_norm_a, k_norm_a, sink_b, w_in_odd, w_out_odd,
           lambda_q1, lambda_k1, lambda_q2, lambda_k2, subln_c):
    depth = w_mod.shape[0]
    batch, seq, _ = x_prompt.shape
    dec_batch, dec_seq, _ = x_sample.shape
    n_past = cache_a_k.shape[2]
    alpha = (2 * depth) ** 0.25
    assert seq == ROW_CHUNK and n_past % ROW_CHUNK == 0 and dec_seq % ROW_CHUNK == 0

    mod = _modulation(c, c_ctx, w_mod, b_mod)
    cos, sin, cos_t, sin_t = _rope_tables(dec_seq)

    bf16_weights = {}

    def run(x, latent, n_batch, s, rows_even, rows_odd):
        kv = {"a_k": [], "a_v": [], "b_k": [], "b_v": [], "c_k": [], "c_v": []}
        for l in range(depth):
            if l % 2 == 0:
                e = l // 2
                extras = ()
                if latent:
                    k_t = lambda t: t[:, e].transpose(0, 2, 3, 1).reshape(n_batch, LANES, n_past)
                    extras = (cos, sin, cos_t, sin_t,
                              k_t(cache_a_k), k_t(cache_a_v), k_t(cache_b_k), k_t(cache_b_v))
                w_in, w_out = bf16_weights[l] if latent else (w_in_even[e], w_out_even[e])
                res = _even_layer(x, mod, l, w_in, w_out, q_norm_a[e], k_norm_a[e],
                                  sink_b[e], ln_g, ln_b, latent, s, rows_even, alpha, extras)
                if latent:
                    x = res
                else:
                    x = res[0]
                    bf16_weights[l] = res[5:7]
                    for name, t in zip(("a_k", "a_v", "b_k", "b_v"), res[1:5]):
                        kv[name].append(t.reshape(n_batch, 2, HEAD_DIM, s).transpose(0, 3, 1, 2))
            else:
                o = l // 2
                lam_init = 0.8 - 0.6 * math.exp(-0.3 * l)
                extras = (cos, sin, cache_c_k, cache_c_v) if latent else ()
                lams = [t[o][None, :] for t in (lambda_q1, lambda_k1, lambda_q2, lambda_k2)]
                w_in, w_out = bf16_weights[l] if latent else (w_in_odd[o], w_out_odd[o])
                res = _odd_layer(x, mod, l, w_in, w_out, lams,
                                 subln_c[o][None, :], ln_g, ln_b, latent, s, rows_odd, alpha, lam_init, extras)
                if latent:
                    x = res
                else:
                    x = res[0]
                    bf16_weights[l] = res[3:5]
                    kv["c_k"].append(res[1][:, 0])
                    kv["c_v"].append(res[2][:, 0])
        return x, kv

    y_ctx, kv = run(x_prompt.reshape(batch * seq, D_MODEL), False, batch, seq, 1024, 512)
    y_lat, _ = run(x_sample.reshape(dec_batch * dec_seq, D_MODEL), True, dec_batch, dec_seq, dec_seq, dec_seq)

    stack = lambda name: jnp.stack(kv[name], axis=1)
    return (y_ctx.reshape(batch, seq, D_MODEL), y_lat.reshape(dec_batch, dec_seq, D_MODEL),
            stack("a_k"), stack("a_v"), stack("b_k"), stack("b_v"), stack("c_k"), stack("c_v"))
```

```python
import functools
import math

import jax
import jax.numpy as jnp
import numpy as np
from jax import lax
from jax.experimental import pallas as pl
from jax.experimental.pallas import tpu as pltpu

F32 = jnp.float32
BF16 = jnp.bfloat16

D_MODEL = 1024
HEAD_DIM = 64
GRID_W = 64
WINDOW = 128
ROPE_THETA = 10000.0
EPS = 1e-6
NEG_INF = -1e30
LOG2E = 1.4426950408889634
Q_SCALE = HEAD_DIM ** -0.5 * LOG2E
LANES = 128
ROW_CHUNK = 256
SOFTMAX_VREGS = 40
VMEM_LIMIT = 60000 * 1024
W_SLAB_ROWS = 128


def _silu(x):
    return x / (1.0 + jnp.exp(-x))


def _dot(a, b):
    return jnp.dot(a, b, preferred_element_type=F32)


def _dot_nt(a, b):
    return lax.dot_general(a, b, (((1,), (1,)), ((), ())), preferred_element_type=F32)


def _lane_iota(rows):
    return lax.broadcasted_iota(jnp.int32, (rows, LANES), 1)


def _chunk_rows(i):
    if isinstance(i, int):
        return pl.ds(i * ROW_CHUNK, ROW_CHUNK)
    return pl.ds(pl.multiple_of(i * ROW_CHUNK, ROW_CHUNK), ROW_CHUNK)


def _softmax_rows(n_cols):
    rows = 8
    while rows * 2 * n_cols <= SOFTMAX_VREGS * 1024 and rows * 2 <= ROW_CHUNK:
        rows *= 2
    return rows


def _rope(a, cos, sin_signed):
    lane = _lane_iota(a.shape[0])
    fwd = pltpu.roll(a, LANES - 16, 1)
    bwd = pltpu.roll(a, 16, 1)
    partner = jnp.where((lane & 16) == 0, fwd, bwd)
    return a * cos + partner * sin_signed


def _rope_t(a, cos_t, sin_t):
    blocks = [a[16 * b:16 * (b + 1), :] for b in range(a.shape[0] // 16)]
    partner = jnp.concatenate([blocks[b ^ 1] for b in range(len(blocks))], axis=0)
    return a * cos_t + partner * sin_t


def _store_kt_variants(scr, chunk, kt):
    width = scr.shape[-1]
    per_block = kt.shape[1] // width
    zero = jnp.zeros((HEAD_DIM, kt.shape[1]), F32)
    for j in range(2):
        kj = kt[HEAD_DIM * j:HEAD_DIM * (j + 1), :]
        for par, full in enumerate((jnp.concatenate([kj, zero], axis=0), jnp.concatenate([zero, kj], axis=0))):
            full = full.astype(BF16)
            for c in range(per_block):
                scr[2 * j + par, chunk * per_block + c] = full[:, width * c:width * (c + 1)]


def _store_v_variants(scr, rows, a):
    lane = _lane_iota(a.shape[0])
    lo = lane < HEAD_DIM
    swapped = pltpu.roll(a, HEAD_DIM, 1)
    one = jnp.ones_like(a)
    scr[0, rows, :] = jnp.where(lo, a, one).astype(BF16)
    scr[1, rows, :] = jnp.where(lo, one, swapped).astype(BF16)
    scr[2, rows, :] = jnp.where(lo, swapped, one).astype(BF16)
    scr[3, rows, :] = jnp.where(lo, one, a).astype(BF16)


def _layer_norm_rows(z, g, b):
    mu = jnp.mean(z, axis=-1, keepdims=True)
    zc = z - mu
    var = jnp.mean(zc * zc, axis=-1, keepdims=True)
    return zc * lax.rsqrt(var + EPS) * g + b


def _modulated(x_ref, mod_ref, mod_row, rows):
    shift = mod_ref[0, pl.ds(mod_row, 1), 0:D_MODEL]
    scale = mod_ref[0, pl.ds(mod_row, 1), D_MODEL:2 * D_MODEL]
    return (x_ref[rows, :] * (1.0 + scale) + shift).astype(BF16)


def _out_proj_norm(x_ref, mod_ref, mod_row, attn_scr, w_out_ref, lng_ref, lnb_ref, layer, y_ref, n_rows, alpha):
    gate = mod_ref[0, pl.ds(mod_row, 1), 2 * D_MODEL:3 * D_MODEL]
    g = lng_ref[layer:layer + 1, :]
    b = lnb_ref[layer:layer + 1, :]

    def body(i, carry):
        rows = _chunk_rows(i)
        out = _dot(attn_scr[rows, :], w_out_ref[...])
        z = alpha * x_ref[rows, :] + gate * out
        y_ref[rows, :] = _layer_norm_rows(z, g, b)
        return carry

    lax.fori_loop(0, n_rows // ROW_CHUNK, body, 0, unroll=True)


class _ContextWeights:
    def __init__(self, step, w_in, w_out, stage, sems, out_sems):
        self.step, self.w_in, self.w_out, self.stage, self.sems = step, w_in, w_out, stage, sems
        self.out_copies = [pltpu.make_async_copy(w[1], w[2], out_sems.at[n]) for n, w in enumerate((w_in, w_out))]

    def _slab_copies(self, w_hbm):
        n_cols = w_hbm.shape[1]
        return [pltpu.make_async_copy(w_hbm.at[pl.ds(s * W_SLAB_ROWS, W_SLAB_ROWS), :],
                                      self.stage.at[s, :, pl.ds(0, n_cols)], self.sems.at[s])
                for s in range(w_hbm.shape[0] // W_SLAB_ROWS)]

    def _cast(self, w_hbm, w_scr):
        n_cols = w_hbm.shape[1]
        for s, copy in enumerate(self._slab_copies(w_hbm)):
            copy.wait()
            w_scr[pl.ds(s * W_SLAB_ROWS, W_SLAB_ROWS), :] = self.stage[s, :, 0:n_cols].astype(BF16)

    def load_in_proj(self):
        @pl.when(self.step == 0)
        def _():
            for copy in self._slab_copies(self.w_in[0]):
                copy.start()
            self._cast(self.w_in[0], self.w_in[1])
            self.out_copies[0].start()
            for copy in self._slab_copies(self.w_out[0]):
                copy.start()

    def load_out_proj(self):
        @pl.when(self.step == 0)
        def _():
            self._cast(self.w_out[0], self.w_out[1])
            self.out_copies[1].start()

    def finish(self):
        @pl.when(self.step == 0)
        def _():
            for copy in self.out_copies:
                copy.wait()


def _run_pipeline(n_items, stages):
    for u in range(n_items + len(stages) - 1):
        for k, stage in enumerate(stages):
            t = u - k
            if 0 <= t < n_items:
                stage(t, t % 2)


def _attend_blocks(block_stages, n_blocks, n_items, unrolled):
    assert n_items % 2 == 0
    if unrolled:
        per_block = [block_stages(i) for i in range(n_blocks)]
        stages = [lambda g, slot, k=k: per_block[g // n_items][k](g % n_items, slot) for k in range(3)]
        _run_pipeline(n_blocks * n_items, stages)
    else:
        def body(i, carry):
            _run_pipeline(n_items, block_stages(i))
            return carry

        lax.fori_loop(0, n_blocks, body, 0)


def _even_kernel(latent, layer, n_rows, seq, alpha, *refs):
    if latent:
        (x_ref, mod_ref, w_in_ref, w_out_ref, norms_ref, sink_ref, lng_ref, lnb_ref,
         cos_ref, sin_ref, cost_ref, sint_ref, cakt_ref, cav_ref, cbkt_ref, cbv_ref,
         y_ref,
         attn_scr, qa_scr, qb_scr, ka_scr, va_scr, kb_scr, vb_scr, g_scr, s_scr, p_scr, es_scr, wkt_scr, wv_scr,
         bias_scr) = refs
    else:
        (x_ref, mod_ref, w_in_hbm, w_out_hbm, norms_ref, sink_ref, lng_ref, lnb_ref,
         y_ref, nakt_ref, navt_ref, nbkt_ref, nbvt_ref, w_in_bf_hbm, w_out_bf_hbm,
         attn_scr, qa_scr, qb_scr, ka_scr, va_scr, kb_scr, vb_scr, g_scr, s_scr, p_scr, es_scr, wkt_scr,
         wv_scr, w_in_ref, w_out_ref, w_stage, w_sems, w_out_sems) = refs

    step = pl.program_id(0)
    if not latent:
        weights = _ContextWeights(step, (w_in_hbm, w_in_ref, w_in_bf_hbm), (w_out_hbm, w_out_ref, w_out_bf_hbm),
                                  w_stage, w_sems, w_out_sems)
        weights.load_in_proj()
    mod_row = step + 1 if latent else 0

    col_ka, col_va, col_kb, col_vb = 512, 640, 1792, 1920

    @pl.when(step == 0)
    def _():
        for r, c0 in enumerate((col_ka, col_kb)):
            wkt_scr[LANES * r:LANES * (r + 1), :] = w_in_ref[:, c0:c0 + LANES].T
        wv_scr[:, 0:LANES] = w_in_ref[:, col_va:col_va + LANES]
        wv_scr[:, LANES:2 * LANES] = w_in_ref[:, col_vb:col_vb + LANES]

    n_lat_chunks = seq // ROW_CHUNK
    if latent:
        n_past = cav_ref.shape[2]
        past_rows = pl.ds(seq, n_past)
        _store_kt_variants(ka_scr, n_lat_chunks, cakt_ref[0])
        _store_kt_variants(kb_scr, n_lat_chunks, cbkt_ref[0])
        _store_v_variants(va_scr, past_rows, cav_ref[0].T)
        _store_v_variants(vb_scr, past_rows, cbv_ref[0].T)

    knt = norms_ref[0:LANES, :]
    qn = norms_ref[LANES:LANES + 1, 0:LANES]

    def proj(i, carry):
        rows = _chunk_rows(i)
        hh = _modulated(x_ref, mod_ref, mod_row, rows)
        if latent:
            cos = cos_ref[rows, :]
            sin = sin_ref[rows, :]
            rot = lambda a: _rope(a, cos, sin)
            rot_t = lambda a: _rope_t(a, cost_ref[i], sint_ref[i])
        else:
            rot = rot_t = lambda a: a

        acc = _dot(hh, w_in_ref[:, 0:512])
        lo_lanes = _lane_iota(ROW_CHUNK) < HEAD_DIM
        for j in range(4):
            a = acc[:, LANES * j:LANES * (j + 1)]
            sq = a * a
            first = jnp.sum(jnp.where(lo_lanes, sq, 0.0), axis=1, keepdims=True)
            second = jnp.sum(jnp.where(lo_lanes, 0.0, sq), axis=1, keepdims=True)
            ms = jnp.where(lo_lanes, first, second) * (1.0 / HEAD_DIM)
            a = rot(a * lax.rsqrt(ms + EPS) * qn)
            qa_scr[rows, LANES * j:LANES * (j + 1)] = (a * Q_SCALE).astype(BF16)
        acc = _dot(hh, w_in_ref[:, 1280:1792])
        for j in range(4):
            a = rot(acc[:, LANES * j:LANES * (j + 1)])
            qb_scr[rows, LANES * j:LANES * (j + 1)] = (a * Q_SCALE).astype(BF16)
        g_scr[rows, 0:512] = _silu(_dot(hh, w_in_ref[:, 768:1280]))
        g_scr[rows, 512:1024] = _silu(_dot(hh, w_in_ref[:, 2048:2560]))
        v = _dot(hh, wv_scr[...])
        _store_v_variants(va_scr, rows, v[:, 0:LANES])
        _store_v_variants(vb_scr, rows, v[:, LANES:2 * LANES])

        kt = _dot_nt(wkt_scr[0:2 * LANES, :], hh)
        heads = [kt[HEAD_DIM * h:HEAD_DIM * (h + 1), :] for h in range(2)]
        kat = jnp.concatenate([blk * lax.rsqrt(jnp.mean(blk * blk, axis=0, keepdims=True) + EPS) for blk in heads],
                              axis=0) * knt
        kbt = kt[LANES:2 * LANES, :]
        if not latent:
            vt = v.T
            nakt_ref[i] = kat
            nbkt_ref[i] = kbt
            navt_ref[i] = vt[0:LANES, :]
            nbvt_ref[i] = vt[LANES:2 * LANES, :]
        _store_kt_variants(ka_scr, i, rot_t(kat))
        _store_kt_variants(kb_scr, i, rot_t(kbt))
        return carry

    lax.fori_loop(0, n_rows // ROW_CHUNK, proj, 0, unroll=2)

    sinks = [sink_ref[h] * LOG2E for h in range(8)]
    ck = ROW_CHUNK
    bk = kb_scr.shape[-1]
    win = ROW_CHUNK + 2 * WINDOW
    n_items = 8

    def block_stages(i):
        rows = _chunk_rows(i)
        if latent:
            a_chunks = list(range(n_lat_chunks + n_past // ck))
            a_keys = pl.ds(0, seq + n_past)
            w0 = jnp.clip(i * (ck // bk) - WINDOW // bk, 0, (seq - win) // bk)
            win_rows = pl.ds(pl.multiple_of(w0 * bk, bk), win)
            dist = (lax.broadcasted_iota(jnp.int32, (ROW_CHUNK, ck), 1)
                    - lax.broadcasted_iota(jnp.int32, (ROW_CHUNK, ck), 0))
            for c in range(win // ck):
                off = w0 * bk + c * ck - i * ck
                bias_scr[c] = jnp.where(jnp.abs(dist + off) <= WINDOW, 0.0, NEG_INF).astype(F32)
            b_first = [w0 + c * (ck // bk) for c in range(win // ck)] + [seq // bk]
            n_biased = win // ck
            b_cols = win + n_past
        else:
            a_chunks = [i]
            a_keys = rows
            b_first = [i * (ck // bk)]
            n_biased = 0
            b_cols = ck
        a_cols = len(a_chunks) * ck

        def qk(t, slot):
            p, branch = divmod(t, 2)
            cols = slice(LANES * p, LANES * (p + 1))
            kvh = p // 2
            q = (qb_scr if branch else qa_scr)[rows, cols]
            for par in (0, 1):
                var = 2 * kvh + par
                if branch:
                    tiles = [jnp.concatenate([kb_scr[var, first + d] for d in range(ck // bk)], axis=1)
                             for first in b_first]
                else:
                    tiles = [ka_scr[var, chunk] for chunk in a_chunks]
                for c, kt in enumerate(tiles):
                    s = _dot(q, kt)
                    if branch and c < n_biased:
                        s = s + bias_scr[c]
                    s_scr[slot, par, :, c * ck:(c + 1) * ck] = s

        def softmax(t, slot):
            p, branch = divmod(t, 2)
            n_cols = b_cols if branch else a_cols
            rb = _softmax_rows(n_cols)
            for par in (0, 1):
                for r in range(ROW_CHUNK // rb):
                    sub = slice(r * rb, (r + 1) * rb)
                    s = s_scr[slot, par, sub, 0:n_cols]
                    m = jnp.max(s, axis=1, keepdims=True)
                    if branch:
                        sink = sinks[2 * p + par]
                        m = jnp.maximum(m, sink)
                        es_scr[slot, sub, HEAD_DIM * par:HEAD_DIM * (par + 1)] = jnp.broadcast_to(
                            jnp.exp2(sink - m), (rb, HEAD_DIM))
                    p_scr[slot, par, sub, 0:n_cols] = jnp.exp2((s - m).astype(BF16))

        def pv(t, slot):
            p, branch = divmod(t, 2)
            kvh = p // 2
            v_scr = vb_scr if branch else va_scr
            accs = []
            for par in (0, 1):
                var = 2 * kvh + par
                if latent and branch:
                    n_loc = win
                    accs.append(_dot(p_scr[slot, par, :, 0:n_loc], v_scr[var, win_rows, :])
                                + _dot(p_scr[slot, par, :, n_loc:b_cols], v_scr[var, past_rows, :]))
                else:
                    accs.append(_dot(p_scr[slot, par, :, 0:a_cols], v_scr[var, a_keys, :]))
            lo = _lane_iota(ROW_CHUNK) < HEAD_DIM
            denom = pltpu.roll(jnp.where(lo, accs[1], accs[0]), HEAD_DIM, 1)
            if branch:
                denom = denom + es_scr[slot]
            o = jnp.where(lo, accs[0], accs[1]) / denom
            ocols = slice(512 * branch + LANES * p, 512 * branch + LANES * (p + 1))
            attn_scr[rows, ocols] = (o * g_scr[rows, ocols]).astype(BF16)

        return qk, softmax, pv

    _attend_blocks(block_stages, n_rows // ROW_CHUNK, n_items, unrolled=not latent)

    if not latent:
        weights.load_out_proj()
    _out_proj_norm(x_ref, mod_ref, mod_row, attn_scr, w_out_ref, lng_ref, lnb_ref, layer, y_ref, n_rows, alpha)
    if not latent:
        weights.finish()


def _odd_kernel(latent, layer, n_rows, seq, alpha, lam_init, *refs):
    if latent:
        (x_ref, mod_ref, w_in_ref, w_out_ref, lq1_ref, lk1_ref, lq2_ref, lk2_ref, sub_ref, lng_ref, lnb_ref,
         cos_ref, sin_ref, cck_hbm, ccv_hbm,
         y_ref,
         attn_scr, q_scr, k_scr, v_scr, g_scr, s_scr, p_scr, past_stage, past_sems) = refs
    else:
        (x_ref, mod_ref, w_in_hbm, w_out_hbm, lq1_ref, lk1_ref, lq2_ref, lk2_ref, sub_ref, lng_ref, lnb_ref,
         y_ref, nck_hbm, ncv_hbm, w_in_bf_hbm, w_out_bf_hbm,
         attn_scr, q_scr, k_scr, v_scr, g_scr, s_scr, p_scr, kv_stage, kv_sems,
         w_in_ref, w_out_ref, w_stage, w_sems, w_out_sems) = refs

    step = pl.program_id(0)
    if not latent:
        weights = _ContextWeights(step, (w_in_hbm, w_in_ref, w_in_bf_hbm), (w_out_hbm, w_out_ref, w_out_bf_hbm),
                                  w_stage, w_sems, w_out_sems)
        weights.load_in_proj()
    mod_row = step + 1 if latent else 0

    n_heads = D_MODEL // LANES
    n_blocks = n_rows // ROW_CHUNK
    lo = _lane_iota(ROW_CHUNK) < HEAD_DIM

    def kv_out_copies(blk):
        elem = step * n_blocks + blk
        return [pltpu.make_async_copy(kv_stage.at[blk, t, :, pl.ds(LANES * h, LANES)],
                                      out.at[elem, 0, :, h, :], kv_sems.at[blk, t])
                for t, out in enumerate((nck_hbm, ncv_hbm)) for h in range(n_heads)]

    def store_k(rows, h, a):
        cols = slice(LANES * h, LANES * (h + 1))
        zero = jnp.zeros_like(a)
        k_scr[0, rows, cols] = jnp.where(lo, a, zero).astype(BF16)
        k_scr[1, rows, cols] = jnp.where(lo, zero, a).astype(BF16)

    if latent:
        n_past = cck_hbm.shape[2]
        past = pl.ds(seq, n_past)
        past_copies = [pltpu.make_async_copy(cache.at[step, layer // 2, :, h, :],
                                             past_stage.at[t, :, pl.ds(LANES * h, LANES)], past_sems.at[t])
                       for t, cache in enumerate((cck_hbm, ccv_hbm)) for h in range(n_heads)]
        for copy in past_copies:
            copy.start()

    def proj(i, carry):
        rows = _chunk_rows(i)
        hh = _modulated(x_ref, mod_ref, mod_row, rows)
        if latent:
            cos = cos_ref[rows, :]
            sin = sin_ref[rows, :]
            rot = lambda a: _rope(a, cos, sin)
        else:
            rot = lambda a: a
        for half in range(2):
            acc = _dot(hh, w_in_ref[:, 512 * half:512 * (half + 1)])
            for j in range(4):
                a = rot(acc[:, LANES * j:LANES * (j + 1)])
                cols = slice(512 * half + LANES * j, 512 * half + LANES * (j + 1))
                q_scr[rows, cols] = (a * Q_SCALE).astype(BF16)
        for half in range(2):
            acc = _dot(hh, w_in_ref[:, 1024 + 512 * half:1024 + 512 * (half + 1)])
            if not latent:
                kv_stage[i, 0, :, 512 * half:512 * (half + 1)] = acc
            for j in range(4):
                store_k(rows, 4 * half + j, rot(acc[:, LANES * j:LANES * (j + 1)]))
        for half in range(2):
            acc = _dot(hh, w_in_ref[:, 2048 + 512 * half:2048 + 512 * (half + 1)])
            if not latent:
                kv_stage[i, 1, :, 512 * half:512 * (half + 1)] = acc
            v_scr[rows, 512 * half:512 * (half + 1)] = acc.astype(BF16)
        if not latent:
            for copy in kv_out_copies(i):
                copy.start()
        for half in range(2):
            acc = _dot(hh, w_in_ref[:, 3072 + 512 * half:3072 + 512 * (half + 1)])
            g_scr[rows, 512 * half:512 * (half + 1)] = _silu(acc)
        return carry

    if latent:
        lax.fori_loop(0, n_blocks, proj, 0, unroll=2)
    else:
        for blk in range(n_blocks):
            proj(blk, 0)

    if latent:
        for copy in past_copies:
            copy.wait()
        for h in range(n_heads):
            store_k(past, h, past_stage[0, :, LANES * h:LANES * (h + 1)])
        v_scr[past, :] = past_stage[1].astype(BF16)

    lam = (jnp.exp(jnp.sum(lq1_ref[...] * lk1_ref[...], axis=1, keepdims=True))
           - jnp.exp(jnp.sum(lq2_ref[...] * lk2_ref[...], axis=1, keepdims=True)) + lam_init)
    sub = sub_ref[...] * (1.0 - lam_init)
    n_keys = seq + n_past if latent else ROW_CHUNK
    rb = _softmax_rows(n_keys)
    ones = jnp.ones((n_keys, LANES), BF16)

    def block_stages(i):
        rows = _chunk_rows(i)
        keys = pl.ds(0, n_keys) if latent else rows

        def qk(h, slot):
            cols = slice(LANES * h, LANES * (h + 1))
            q = q_scr[rows, cols]
            for m in (0, 1):
                s_scr[slot, m] = _dot_nt(q, k_scr[m, keys, cols])

        def softmax(h, slot):
            for m in (0, 1):
                for r in range(ROW_CHUNK // rb):
                    sub_rows = slice(r * rb, (r + 1) * rb)
                    s = s_scr[slot, m, sub_rows, :]
                    top = jnp.max(s, axis=1, keepdims=True)
                    p_scr[slot, m, sub_rows, :] = jnp.exp2((s - top).astype(BF16))

        def pv(h, slot):
            cols = slice(LANES * h, LANES * (h + 1))
            v_ext = jnp.concatenate([v_scr[keys, cols], ones], axis=1)
            maps = []
            for m in (0, 1):
                acc = _dot(p_scr[slot, m], v_ext)
                maps.append(acc[:, 0:LANES] / acc[:, LANES:2 * LANES])
            o = maps[0] - lam * maps[1]
            ms = jnp.mean(o * o, axis=1, keepdims=True)
            o = o * lax.rsqrt(ms + EPS) * sub
            attn_scr[rows, cols] = (o * g_scr[rows, cols]).astype(BF16)

        return qk, softmax, pv

    _attend_blocks(block_stages, n_blocks, n_heads, unrolled=not latent)

    if not latent:
        weights.load_out_proj()
    _out_proj_norm(x_ref, mod_ref, mod_row, attn_scr, w_out_ref, lng_ref, lnb_ref, layer, y_ref, n_rows, alpha)

    if not latent:
        for blk in range(n_blocks):
            for copy in kv_out_copies(blk):
                copy.wait()
        weights.finish()


MOD_SLAB_ROWS = 128
MOD_COL_BLOCK = 1024
MOD_SLOTS = 8


def _mod_kernel(n_cond, cv_ref, w_hbm, b_ref, o_ref, sb_scr, ring, acc_scr, sems):
    depth, n_in, n_out = w_hbm.shape
    sublanes = 8
    slabs_per_layer = n_in // MOD_SLAB_ROWS
    slabs = [(l, rs) for l in range(depth) for rs in range(slabs_per_layer)]

    def slab_copy(n):
        l, rs = slabs[n]
        return pltpu.make_async_copy(w_hbm.at[l, pl.ds(rs * MOD_SLAB_ROWS, MOD_SLAB_ROWS), :],
                                     ring.at[n % MOD_SLOTS], sems.at[n % MOD_SLOTS])

    for n in range(min(MOD_SLOTS, len(slabs))):
        slab_copy(n).start()
    s_t = _silu(cv_ref[...]).T
    for r in range(n_cond):
        sb_scr[r] = jnp.broadcast_to(s_t[:, r:r + 1], (n_in, LANES))

    for n, (l, rs) in enumerate(slabs):
        slab_copy(n).wait()
        for cb in range(n_out // MOD_COL_BLOCK):
            cols = pl.ds(cb * MOD_COL_BLOCK, MOD_COL_BLOCK)
            if rs == 0:
                accs = (jnp.zeros((sublanes, MOD_COL_BLOCK), F32),) * n_cond
            else:
                accs = tuple(acc_scr[r, :, cols] for r in range(n_cond))

            def body(kb, accs, n=n, rs=rs, cols=cols):
                w = ring[n % MOD_SLOTS, pl.ds(pl.multiple_of(kb * sublanes, sublanes), sublanes), cols]
                s_rows = pl.ds(pl.multiple_of(rs * MOD_SLAB_ROWS + kb * sublanes, sublanes), sublanes)
                return tuple(acc + w * jnp.tile(sb_scr[r, s_rows, :], (1, MOD_COL_BLOCK // LANES))
                             for r, acc in enumerate(accs))

            accs = lax.fori_loop(0, MOD_SLAB_ROWS // sublanes, body, accs, unroll=8)
            if rs < slabs_per_layer - 1:
                for r in range(n_cond):
                    acc_scr[r, :, cols] = accs[r]
            else:
                rows = [jnp.sum(acc, axis=0, keepdims=True) + b_ref[l:l + 1, cols] for acc in accs]
                o_ref[l, :, cols] = jnp.concatenate(rows + [jnp.zeros((8 - n_cond, MOD_COL_BLOCK), F32)], axis=0)
        if n + MOD_SLOTS < len(slabs):
            slab_copy(n + MOD_SLOTS).start()


def _full(shape, **kw):
    zeros = (0,) * len(shape)
    return pl.BlockSpec(shape, lambda i: zeros, **kw)


def _weight_specs(latent, w_in, w_out):
    if latent:
        single = pl.Buffered(1)
        return [_full(w_in.shape, pipeline_mode=single), _full(w_out.shape, pipeline_mode=single)]
    return [pl.BlockSpec(memory_space=pl.ANY), pl.BlockSpec(memory_space=pl.ANY)]


def _weight_scratch(w_in, w_out):
    assert w_in.shape[0] % W_SLAB_ROWS == 0 and w_out.shape[0] % W_SLAB_ROWS == 0
    n_slabs = max(w_in.shape[0], w_out.shape[0]) // W_SLAB_ROWS
    return [pltpu.VMEM(w_in.shape, BF16), pltpu.VMEM(w_out.shape, BF16),
            pltpu.VMEM((n_slabs, W_SLAB_ROWS, max(w_in.shape[1], w_out.shape[1])), F32),
            pltpu.SemaphoreType.DMA((n_slabs,)), pltpu.SemaphoreType.DMA((2,))]


def _rope_tables(seq):
    t = np.arange(seq)
    n_freq = HEAD_DIM // 4
    freqs = ROPE_THETA ** (-np.arange(n_freq, dtype=np.float64) / n_freq)
    ang_row = (t // GRID_W)[:, None] * freqs
    ang_col = (t % GRID_W)[:, None] * freqs
    ang = np.concatenate([ang_row, ang_row, ang_col, ang_col], axis=1)
    sign = np.concatenate([-np.ones(n_freq), np.ones(n_freq)] * 2)[None, :]
    cos = np.tile(np.cos(ang), (1, 2)).astype(np.float32)
    sin = np.tile(np.sin(ang) * sign, (1, 2)).astype(np.float32)
    chunked_t = lambda a: a.reshape(seq // ROW_CHUNK, ROW_CHUNK, LANES).transpose(0, 2, 1)
    return jnp.asarray(cos), jnp.asarray(sin), jnp.asarray(chunked_t(cos)), jnp.asarray(chunked_t(sin))


def _modulation(c, c_ctx, w_mod, b_mod):
    depth = w_mod.shape[0]
    n_cond = 1 + c.shape[0]
    cv = jnp.concatenate([c_ctx[None, :], c, jnp.zeros((8 - n_cond, D_MODEL), F32)], axis=0)
    assert D_MODEL % MOD_SLAB_ROWS == 0 and (3 * D_MODEL) % MOD_COL_BLOCK == 0
    return pl.pallas_call(
        functools.partial(_mod_kernel, n_cond),
        grid=(1,),
        in_specs=[_full(cv.shape), pl.BlockSpec(memory_space=pl.ANY), _full(b_mod.shape)],
        out_specs=_full((depth, 8, 3 * D_MODEL)),
        out_shape=jax.ShapeDtypeStruct((depth, 8, 3 * D_MODEL), F32),
        scratch_shapes=[pltpu.VMEM((n_cond, D_MODEL, LANES), F32),
                        pltpu.VMEM((MOD_SLOTS, MOD_SLAB_ROWS, 3 * D_MODEL), F32),
                        pltpu.VMEM((n_cond, 8, 3 * D_MODEL), F32),
                        pltpu.SemaphoreType.DMA((MOD_SLOTS,))],
        compiler_params=pltpu.CompilerParams(dimension_semantics=("arbitrary",)),
        name="adaln_modulation",
    )(cv, w_mod, b_mod)


def _even_layer(x, mod, layer, w_in, w_out, q_norm, k_norm, sink, ln_g, ln_b, latent, seq, n_rows, alpha, extras=()):
    total = x.shape[0]
    grid = (total // n_rows,)
    single = pl.Buffered(1)
    norms = jnp.concatenate([jnp.broadcast_to(jnp.tile(k_norm, 2)[:, None], (LANES, ROW_CHUNK)),
                             jnp.broadcast_to(jnp.tile(q_norm, 2 * ROW_CHUNK // LANES)[None, :], (8, ROW_CHUNK))], axis=0)

    row_blk = lambda width: pl.BlockSpec((n_rows, width), lambda i: (i, 0))
    in_specs = [row_blk(D_MODEL),
                pl.BlockSpec((1, 8, 3 * D_MODEL), lambda i: (layer, 0, 0)),
                *_weight_specs(latent, w_in, w_out),
                _full(norms.shape),
                pl.BlockSpec(memory_space=pltpu.SMEM),
                _full(ln_g.shape), _full(ln_b.shape)]
    args = [x, mod, w_in, w_out, norms, sink, ln_g, ln_b]
    y_shape = jax.ShapeDtypeStruct((total, D_MODEL), F32)
    n_blocks = n_rows // ROW_CHUNK
    if latent:
        cos, sin, cos_t, sin_t, cakt, cav, cbkt, cbv = extras
        n_past = cav.shape[2]
        in_specs += [_full(cos.shape, pipeline_mode=single), _full(sin.shape, pipeline_mode=single),
                     _full(cos_t.shape, pipeline_mode=single), _full(sin_t.shape, pipeline_mode=single)]
        in_specs += [pl.BlockSpec((1, LANES, n_past), lambda i: (i, 0, 0))] * 4
        args += [cos, sin, cos_t, sin_t, cakt, cav, cbkt, cbv]
        out_specs = row_blk(D_MODEL)
        out_shape = y_shape
        n_keys = seq + n_past
    else:
        kv_blk = pl.BlockSpec((n_blocks, LANES, ROW_CHUNK), lambda i: (i, 0, 0))
        hbm = pl.BlockSpec(memory_space=pl.ANY)
        out_specs = [row_blk(D_MODEL)] + [kv_blk] * 4 + [hbm, hbm]
        out_shape = ([y_shape] + [jax.ShapeDtypeStruct((total // seq, LANES, seq), F32)] * 4
                     + [jax.ShapeDtypeStruct(w_in.shape, BF16), jax.ShapeDtypeStruct(w_out.shape, BF16)])
        n_keys = n_rows
    n_kchunks = n_keys // ROW_CHUNK
    n_cols = n_keys if latent else ROW_CHUNK
    scratch = [pltpu.VMEM((n_rows, D_MODEL), BF16),
               pltpu.VMEM((n_rows, 512), BF16), pltpu.VMEM((n_rows, 512), BF16),
               pltpu.VMEM((4, n_kchunks, LANES, ROW_CHUNK), BF16), pltpu.VMEM((4, n_keys, LANES), BF16),
               pltpu.VMEM((4, n_keys // WINDOW, LANES, WINDOW), BF16), pltpu.VMEM((4, n_keys, LANES), BF16),
               pltpu.VMEM((n_rows, D_MODEL), F32),
               pltpu.VMEM((2, 2, ROW_CHUNK, n_cols), F32),
               pltpu.VMEM((2, 2, ROW_CHUNK, n_cols), BF16),
               pltpu.VMEM((2, ROW_CHUNK, LANES), F32),
               pltpu.VMEM((2 * LANES, D_MODEL), BF16),
               pltpu.VMEM((D_MODEL, 2 * LANES), BF16)]
    if latent:
        scratch.append(pltpu.VMEM((1 + 2 * WINDOW // ROW_CHUNK, ROW_CHUNK, ROW_CHUNK), F32))
    else:
        scratch += _weight_scratch(w_in, w_out)
    return pl.pallas_call(
        functools.partial(_even_kernel, latent, layer, n_rows, seq, alpha),
        grid=grid, in_specs=in_specs, out_specs=out_specs, out_shape=out_shape,
        scratch_shapes=scratch,
        compiler_params=pltpu.CompilerParams(dimension_semantics=("arbitrary",), vmem_limit_bytes=VMEM_LIMIT),
        name="even_layer_latent" if latent else "even_layer_context",
    )(*args)


def _odd_layer(x, mod, layer, w_in, w_out, lams, sub, ln_g, ln_b, latent, seq, n_rows, alpha, lam_init, extras=()):
    total = x.shape[0]
    grid = (total // n_rows,)
    row_blk = lambda width: pl.BlockSpec((n_rows, width), lambda i: (i, 0))
    single = pl.Buffered(1)
    in_specs = [row_blk(D_MODEL),
                pl.BlockSpec((1, 8, 3 * D_MODEL), lambda i: (layer, 0, 0)),
                *_weight_specs(latent, w_in, w_out),
                _full((1, HEAD_DIM)), _full((1, HEAD_DIM)), _full((1, HEAD_DIM)), _full((1, HEAD_DIM)),
                _full((1, LANES)),
                _full(ln_g.shape), _full(ln_b.shape)]
    args = [x, mod, w_in, w_out, *lams, sub, ln_g, ln_b]
    y_shape = jax.ShapeDtypeStruct((total, D_MODEL), F32)
    n_heads = D_MODEL // LANES
    n_blocks = n_rows // ROW_CHUNK
    if latent:
        cos, sin, cck, ccv = extras
        n_past = cck.shape[2]
        in_specs += [_full(cos.shape, pipeline_mode=single), _full(sin.shape, pipeline_mode=single)]
        in_specs += [pl.BlockSpec(memory_space=pl.ANY)] * 2
        args += [cos, sin, cck, ccv]
        out_specs = row_blk(D_MODEL)
        out_shape = y_shape
        n_keys = seq + n_past
    else:
        hbm = pl.BlockSpec(memory_space=pl.ANY)
        out_specs = [row_blk(D_MODEL), hbm, hbm, hbm, hbm]
        out_shape = ([y_shape] + [jax.ShapeDtypeStruct((total // seq, 1, seq, n_heads, LANES), F32)] * 2
                     + [jax.ShapeDtypeStruct(w_in.shape, BF16), jax.ShapeDtypeStruct(w_out.shape, BF16)])
        n_keys = n_rows
    n_cols = n_keys if latent else ROW_CHUNK
    scratch = [pltpu.VMEM((n_rows, D_MODEL), BF16),
               pltpu.VMEM((n_rows, D_MODEL), BF16),
               pltpu.VMEM((2, n_keys, D_MODEL), BF16),
               pltpu.VMEM((n_keys, D_MODEL), BF16),
               pltpu.VMEM((n_rows, D_MODEL), F32),
               pltpu.VMEM((2, 2, ROW_CHUNK, n_cols), F32),
               pltpu.VMEM((2, 2, ROW_CHUNK, n_cols), BF16)]
    if latent:
        scratch += [pltpu.VMEM((2, n_past, D_MODEL), F32), pltpu.SemaphoreType.DMA((2,))]
    else:
        scratch += [pltpu.VMEM((n_blocks, 2, ROW_CHUNK, D_MODEL), F32),
                    pltpu.SemaphoreType.DMA((n_blocks, 2))]
        scratch += _weight_scratch(w_in, w_out)
    return pl.pallas_call(
        functools.partial(_odd_kernel, latent, layer, n_rows, seq, alpha, lam_init),
        grid=grid, in_specs=in_specs, out_specs=out_specs, out_shape=out_shape,
        scratch_shapes=scratch,
        compiler_params=pltpu.CompilerParams(dimension_semantics=("arbitrary",), vmem_limit_bytes=VMEM_LIMIT),
        name="odd_layer_latent" if latent else "odd_layer_context",
    )(*args)


def kernel(x_prompt, x_sample, cache_a_k, cache_a_v, cache_b_k, cache_b_v, cache_c_k, cache_c_v, c, c_ctx,
           w_mod, b_mod, ln_g, ln_b, w_in_even, w_out_even, q_norm_a, k_norm_a, sink_b, w_in_odd, w_out_odd,
           lambda_q1, lambda_k1, lambda_q2, lambda_k2, subln_c):
    depth = w_mod.shape[0]
    batch, seq, _ = x_prompt.shape
    dec_batch, dec_seq, _ = x_sample.shape
    n_past = cache_a_k.shape[2]
    alpha = (2 * depth) ** 0.25
    assert seq == ROW_CHUNK and n_past % ROW_CHUNK == 0 and dec_seq % ROW_CHUNK == 0

    mod = _modulation(c, c_ctx, w_mod, b_mod)
    cos, sin, cos_t, sin_t = _rope_tables(dec_seq)

    bf16_weights = {}

    def run(x, latent, n_batch, s, rows_even, rows_odd):
        kv = {"a_k": [], "a_v": [], "b_k": [], "b_v": [], "c_k": [], "c_v": []}
        for l in range(depth):
            if l % 2 == 0:
                e = l // 2
                extras = ()
                if latent:
                    k_t = lambda t: t[:, e].transpose(0, 2, 3, 1).reshape(n_batch, LANES, n_past)
                    extras = (cos, sin, cos_t, sin_t,
                              k_t(cache_a_k), k_t(cache_a_v), k_t(cache_b_k), k_t(cache_b_v))
                w_in, w_out = bf16_weights[l] if latent else (w_in_even[e], w_out_even[e])
                res = _even_layer(x, mod, l, w_in, w_out, q_norm_a[e], k_norm_a[e],
                                  sink_b[e], ln_g, ln_b, latent, s, rows_even, alpha, extras)
                if latent:
                    x = res
                else:
                    x = res[0]
                    bf16_weights[l] = res[5:7]
                    for name, t in zip(("a_k", "a_v", "b_k", "b_v"), res[1:5]):
                        kv[name].append(t.reshape(n_batch, 2, HEAD_DIM, s).transpose(0, 3, 1, 2))
            else:
                o = l // 2
                lam_init = 0.8 - 0.6 * math.exp(-0.3 * l)
                extras = (cos, sin, cache_c_k, cache_c_v) if latent else ()
                lams = [t[o][None, :] for t in (lambda_q1, lambda_k1, lambda_q2, lambda_k2)]
                w_in, w_out = bf16_weights[l] if latent else (w_in_odd[o], w_out_odd[o])
                res = _odd_layer(x, mod, l, w_in, w_out, lams,
                                 subln_c[o][None, :], ln_g, ln_b, latent, s, rows_odd, alpha, lam_init, extras)
                if latent:
                    x = res
                else:
                    x = res[0]
                    bf16_weights[l] = res[3:5]
                    kv["c_k"].append(res[1][:, 0])
                    kv["c_v"].append(res[2][:, 0])
        return x, kv

    y_ctx, kv = run(x_prompt.reshape(batch * seq, D_MODEL), False, batch, seq, 512, 512)
    y_lat, _ = run(x_sample.reshape(dec_batch * dec_seq, D_MODEL), True, dec_batch, dec_seq, dec_seq, dec_seq)

    stack = lambda name: jnp.stack(kv[name], axis=1)
    return (y_ctx.reshape(batch, seq, D_MODEL), y_lat.reshape(dec_batch, dec_seq, D_MODEL),
            stack("a_k"), stack("a_v"), stack("b_k"), stack("b_v"), stack("c_k"), stack("c_v"))
```

```python
import functools
import math

import jax
import jax.numpy as jnp
import numpy as np
from jax import lax
from jax.experimental import pallas as pl
from jax.experimental.pallas import tpu as pltpu

F32 = jnp.float32
BF16 = jnp.bfloat16

D_MODEL = 1024
HEAD_DIM = 64
GRID_W = 64
WINDOW = 128
ROPE_THETA = 10000.0
EPS = 1e-6
NEG_INF = -1e30
LOG2E = 1.4426950408889634
Q_SCALE = HEAD_DIM ** -0.5 * LOG2E
LANES = 128
ROW_CHUNK = 256
SOFTMAX_VREGS = 40
VMEM_LIMIT = 60000 * 1024
W_SLAB_ROWS = 64


def _silu(x):
    return x / (1.0 + jnp.exp(-x))


def _dot(a, b):
    return jnp.dot(a, b, preferred_element_type=F32)


def _dot_nt(a, b):
    return lax.dot_general(a, b, (((1,), (1,)), ((), ())), preferred_element_type=F32)


def _lane_iota(rows):
    return lax.broadcasted_iota(jnp.int32, (rows, LANES), 1)


def _chunk_rows(i):
    if isinstance(i, int):
        return pl.ds(i * ROW_CHUNK, ROW_CHUNK)
    return pl.ds(pl.multiple_of(i * ROW_CHUNK, ROW_CHUNK), ROW_CHUNK)


def _softmax_rows(n_cols):
    rows = 8
    while rows * 2 * n_cols <= SOFTMAX_VREGS * 1024 and rows * 2 <= ROW_CHUNK:
        rows *= 2
    return rows


def _rope(a, cos, sin_signed):
    lane = _lane_iota(a.shape[0])
    fwd = pltpu.roll(a, LANES - 16, 1)
    bwd = pltpu.roll(a, 16, 1)
    partner = jnp.where((lane & 16) == 0, fwd, bwd)
    return a * cos + partner * sin_signed


def _rope_t(a, cos_t, sin_t):
    blocks = [a[16 * b:16 * (b + 1), :] for b in range(a.shape[0] // 16)]
    partner = jnp.concatenate([blocks[b ^ 1] for b in range(len(blocks))], axis=0)
    return a * cos_t + partner * sin_t


def _store_kt_variants(scr, chunk, kt):
    width = scr.shape[-1]
    per_block = kt.shape[1] // width
    zero = jnp.zeros((HEAD_DIM, kt.shape[1]), F32)
    for j in range(2):
        kj = kt[HEAD_DIM * j:HEAD_DIM * (j + 1), :]
        for par, full in enumerate((jnp.concatenate([kj, zero], axis=0), jnp.concatenate([zero, kj], axis=0))):
            full = full.astype(BF16)
            for c in range(per_block):
                scr[2 * j + par, chunk * per_block + c] = full[:, width * c:width * (c + 1)]


def _store_v_variants(scr, rows, a):
    lane = _lane_iota(a.shape[0])
    lo = lane < HEAD_DIM
    swapped = pltpu.roll(a, HEAD_DIM, 1)
    one = jnp.ones_like(a)
    scr[0, rows, :] = jnp.where(lo, a, one).astype(BF16)
    scr[1, rows, :] = jnp.where(lo, one, swapped).astype(BF16)
    scr[2, rows, :] = jnp.where(lo, swapped, one).astype(BF16)
    scr[3, rows, :] = jnp.where(lo, one, a).astype(BF16)


def _layer_norm_rows(z, g, b):
    mu = jnp.mean(z, axis=-1, keepdims=True)
    zc = z - mu
    var = jnp.mean(zc * zc, axis=-1, keepdims=True)
    return zc * lax.rsqrt(var + EPS) * g + b


def _modulated(x_ref, mod_ref, mod_row, rows):
    shift = mod_ref[0, pl.ds(mod_row, 1), 0:D_MODEL]
    scale = mod_ref[0, pl.ds(mod_row, 1), D_MODEL:2 * D_MODEL]
    return (x_ref[rows, :] * (1.0 + scale) + shift).astype(BF16)


def _out_proj_norm(x_ref, mod_ref, mod_row, attn_scr, w_out_ref, lng_ref, lnb_ref, layer, y_ref, n_rows, alpha):
    gate = mod_ref[0, pl.ds(mod_row, 1), 2 * D_MODEL:3 * D_MODEL]
    g = lng_ref[layer:layer + 1, :]
    b = lnb_ref[layer:layer + 1, :]

    def body(i, carry):
        rows = _chunk_rows(i)
        out = _dot(attn_scr[rows, :], w_out_ref[...])
        z = alpha * x_ref[rows, :] + gate * out
        y_ref[rows, :] = _layer_norm_rows(z, g, b)
        return carry

    lax.fori_loop(0, n_rows // ROW_CHUNK, body, 0, unroll=True)


class _ContextWeights:
    def __init__(self, step, w_in, w_out, stage, sems, out_sems):
        self.step, self.w_in, self.w_out, self.stage, self.sems = step, w_in, w_out, stage, sems
        self.out_copies = [pltpu.make_async_copy(w[1], w[2], out_sems.at[n]) for n, w in enumerate((w_in, w_out))]

    def _slab_copies(self, w_hbm):
        n_cols = w_hbm.shape[1]
        return [pltpu.make_async_copy(w_hbm.at[pl.ds(s * W_SLAB_ROWS, W_SLAB_ROWS), :],
                                      self.stage.at[s, :, pl.ds(0, n_cols)], self.sems.at[s])
                for s in range(w_hbm.shape[0] // W_SLAB_ROWS)]

    def _cast(self, w_hbm, w_scr):
        n_cols = w_hbm.shape[1]
        for s, copy in enumerate(self._slab_copies(w_hbm)):
            copy.wait()
            w_scr[pl.ds(s * W_SLAB_ROWS, W_SLAB_ROWS), :] = self.stage[s, :, 0:n_cols].astype(BF16)

    def load_in_proj(self):
        @pl.when(self.step == 0)
        def _():
            for copy in self._slab_copies(self.w_in[0]):
                copy.start()
            self._cast(self.w_in[0], self.w_in[1])
            self.out_copies[0].start()
            for copy in self._slab_copies(self.w_out[0]):
                copy.start()

    def load_out_proj(self):
        @pl.when(self.step == 0)
        def _():
            self._cast(self.w_out[0], self.w_out[1])
            self.out_copies[1].start()

    def finish(self):
        @pl.when(self.step == 0)
        def _():
            for copy in self.out_copies:
                copy.wait()


def _run_pipeline(n_items, stages):
    for u in range(n_items + len(stages) - 1):
        for k, stage in enumerate(stages):
            t = u - k
            if 0 <= t < n_items:
                stage(t, t % 2)


def _attend_blocks(block_stages, n_blocks, n_items, unrolled):
    assert n_items % 2 == 0
    if unrolled:
        per_block = [block_stages(i) for i in range(n_blocks)]
        stages = [lambda g, slot, k=k: per_block[g // n_items][k](g % n_items, slot) for k in range(3)]
        _run_pipeline(n_blocks * n_items, stages)
    else:
        def body(i, carry):
            _run_pipeline(n_items, block_stages(i))
            return carry

        lax.fori_loop(0, n_blocks, body, 0)


def _even_kernel(latent, layer, n_rows, seq, alpha, *refs):
    if latent:
        (x_ref, mod_ref, w_in_ref, w_out_ref, norms_ref, sink_ref, lng_ref, lnb_ref,
         cos_ref, sin_ref, cost_ref, sint_ref, cakt_ref, cav_ref, cbkt_ref, cbv_ref,
         y_ref,
         attn_scr, qa_scr, qb_scr, ka_scr, va_scr, kb_scr, vb_scr, g_scr, s_scr, p_scr, es_scr, wkt_scr, wv_scr,
         bias_scr) = refs
    else:
        (x_ref, mod_ref, w_in_hbm, w_out_hbm, norms_ref, sink_ref, lng_ref, lnb_ref,
         y_ref, nakt_ref, navt_ref, nbkt_ref, nbvt_ref, w_in_bf_hbm, w_out_bf_hbm,
         attn_scr, qa_scr, qb_scr, ka_scr, va_scr, kb_scr, vb_scr, g_scr, s_scr, p_scr, es_scr, wkt_scr,
         wv_scr, w_in_ref, w_out_ref, w_stage, w_sems, w_out_sems) = refs

    step = pl.program_id(0)
    if not latent:
        weights = _ContextWeights(step, (w_in_hbm, w_in_ref, w_in_bf_hbm), (w_out_hbm, w_out_ref, w_out_bf_hbm),
                                  w_stage, w_sems, w_out_sems)
        weights.load_in_proj()
    mod_row = step + 1 if latent else 0

    col_ka, col_va, col_kb, col_vb = 512, 640, 1792, 1920

    @pl.when(step == 0)
    def _():
        for r, c0 in enumerate((col_ka, col_kb)):
            wkt_scr[LANES * r:LANES * (r + 1), :] = w_in_ref[:, c0:c0 + LANES].T
        wv_scr[:, 0:LANES] = w_in_ref[:, col_va:col_va + LANES]
        wv_scr[:, LANES:2 * LANES] = w_in_ref[:, col_vb:col_vb + LANES]

    n_lat_chunks = seq // ROW_CHUNK
    if latent:
        n_past = cav_ref.shape[2]
        past_rows = pl.ds(seq, n_past)
        _store_kt_variants(ka_scr, n_lat_chunks, cakt_ref[0])
        _store_kt_variants(kb_scr, n_lat_chunks, cbkt_ref[0])
        _store_v_variants(va_scr, past_rows, cav_ref[0].T)
        _store_v_variants(vb_scr, past_rows, cbv_ref[0].T)

    knt = norms_ref[0:LANES, :]
    qn = norms_ref[LANES:LANES + 1, 0:LANES]

    def proj(i, carry):
        rows = _chunk_rows(i)
        hh = _modulated(x_ref, mod_ref, mod_row, rows)
        if latent:
            cos = cos_ref[rows, :]
            sin = sin_ref[rows, :]
            rot = lambda a: _rope(a, cos, sin)
            rot_t = lambda a: _rope_t(a, cost_ref[i], sint_ref[i])
        else:
            rot = rot_t = lambda a: a

        acc = _dot(hh, w_in_ref[:, 0:512])
        lo_lanes = _lane_iota(ROW_CHUNK) < HEAD_DIM
        for j in range(4):
            a = acc[:, LANES * j:LANES * (j + 1)]
            sq = a * a
            first = jnp.sum(jnp.where(lo_lanes, sq, 0.0), axis=1, keepdims=True)
            second = jnp.sum(jnp.where(lo_lanes, 0.0, sq), axis=1, keepdims=True)
            ms = jnp.where(lo_lanes, first, second) * (1.0 / HEAD_DIM)
            a = rot(a * lax.rsqrt(ms + EPS) * qn)
            qa_scr[rows, LANES * j:LANES * (j + 1)] = (a * Q_SCALE).astype(BF16)
        acc = _dot(hh, w_in_ref[:, 1280:1792])
        for j in range(4):
            a = rot(acc[:, LANES * j:LANES * (j + 1)])
            qb_scr[rows, LANES * j:LANES * (j + 1)] = (a * Q_SCALE).astype(BF16)
        g_scr[rows, 0:512] = _silu(_dot(hh, w_in_ref[:, 768:1280]))
        g_scr[rows, 512:1024] = _silu(_dot(hh, w_in_ref[:, 2048:2560]))
        v = _dot(hh, wv_scr[...])
        _store_v_variants(va_scr, rows, v[:, 0:LANES])
        _store_v_variants(vb_scr, rows, v[:, LANES:2 * LANES])

        kt = _dot_nt(wkt_scr[0:2 * LANES, :], hh)
        heads = [kt[HEAD_DIM * h:HEAD_DIM * (h + 1), :] for h in range(2)]
        kat = jnp.concatenate([blk * lax.rsqrt(jnp.mean(blk * blk, axis=0, keepdims=True) + EPS) for blk in heads],
                              axis=0) * knt
        kbt = kt[LANES:2 * LANES, :]
        if not latent:
            vt = v.T
            nakt_ref[i] = kat
            nbkt_ref[i] = kbt
            navt_ref[i] = vt[0:LANES, :]
            nbvt_ref[i] = vt[LANES:2 * LANES, :]
        _store_kt_variants(ka_scr, i, rot_t(kat))
        _store_kt_variants(kb_scr, i, rot_t(kbt))
        return carry

    lax.fori_loop(0, n_rows // ROW_CHUNK, proj, 0, unroll=2)

    sinks = [sink_ref[h] * LOG2E for h in range(8)]
    ck = ROW_CHUNK
    bk = kb_scr.shape[-1]
    win = ROW_CHUNK + 2 * WINDOW
    n_items = 8

    def block_stages(i):
        rows = _chunk_rows(i)
        if latent:
            a_chunks = list(range(n_lat_chunks + n_past // ck))
            a_keys = pl.ds(0, seq + n_past)
            w0 = jnp.clip(i * (ck // bk) - WINDOW // bk, 0, (seq - win) // bk)
            win_rows = pl.ds(pl.multiple_of(w0 * bk, bk), win)
            dist = (lax.broadcasted_iota(jnp.int32, (ROW_CHUNK, ck), 1)
                    - lax.broadcasted_iota(jnp.int32, (ROW_CHUNK, ck), 0))
            for c in range(win // ck):
                off = w0 * bk + c * ck - i * ck
                bias_scr[c] = jnp.where(jnp.abs(dist + off) <= WINDOW, 0.0, NEG_INF).astype(F32)
            b_first = [w0 + c * (ck // bk) for c in range(win // ck)] + [seq // bk]
            n_biased = win // ck
            b_cols = win + n_past
        else:
            a_chunks = [i]
            a_keys = rows
            b_first = [i * (ck // bk)]
            n_biased = 0
            b_cols = ck
        a_cols = len(a_chunks) * ck

        def qk(t, slot):
            p, branch = divmod(t, 2)
            cols = slice(LANES * p, LANES * (p + 1))
            kvh = p // 2
            q = (qb_scr if branch else qa_scr)[rows, cols]
            for par in (0, 1):
                var = 2 * kvh + par
                if branch:
                    tiles = [jnp.concatenate([kb_scr[var, first + d] for d in range(ck // bk)], axis=1)
                             for first in b_first]
                else:
                    tiles = [ka_scr[var, chunk] for chunk in a_chunks]
                for c, kt in enumerate(tiles):
                    s = _dot(q, kt)
                    if branch and c < n_biased:
                        s = s + bias_scr[c]
                    s_scr[slot, par, :, c * ck:(c + 1) * ck] = s

        def softmax(t, slot):
            p, branch = divmod(t, 2)
            n_cols = b_cols if branch else a_cols
            rb = _softmax_rows(n_cols)
            for par in (0, 1):
                for r in range(ROW_CHUNK // rb):
                    sub = slice(r * rb, (r + 1) * rb)
                    s = s_scr[slot, par, sub, 0:n_cols]
                    m = jnp.max(s, axis=1, keepdims=True)
                    if branch:
                        sink = sinks[2 * p + par]
                        m = jnp.maximum(m, sink)
                        es_scr[slot, sub, HEAD_DIM * par:HEAD_DIM * (par + 1)] = jnp.broadcast_to(
                            jnp.exp2(sink - m), (rb, HEAD_DIM))
                    p_scr[slot, par, sub, 0:n_cols] = jnp.exp2((s - m).astype(BF16))

        def pv(t, slot):
            p, branch = divmod(t, 2)
            kvh = p // 2
            v_scr = vb_scr if branch else va_scr
            accs = []
            for par in (0, 1):
                var = 2 * kvh + par
                if latent and branch:
                    n_loc = win
                    accs.append(_dot(p_scr[slot, par, :, 0:n_loc], v_scr[var, win_rows, :])
                                + _dot(p_scr[slot, par, :, n_loc:b_cols], v_scr[var, past_rows, :]))
                else:
                    accs.append(_dot(p_scr[slot, par, :, 0:a_cols], v_scr[var, a_keys, :]))
            lo = _lane_iota(ROW_CHUNK) < HEAD_DIM
            denom = pltpu.roll(jnp.where(lo, accs[1], accs[0]), HEAD_DIM, 1)
            if branch:
                denom = denom + es_scr[slot]
            o = jnp.where(lo, accs[0], accs[1]) / denom
            ocols = slice(512 * branch + LANES * p, 512 * branch + LANES * (p + 1))
            attn_scr[rows, ocols] = (o * g_scr[rows, ocols]).astype(BF16)

        return qk, softmax, pv

    _attend_blocks(block_stages, n_rows // ROW_CHUNK, n_items, unrolled=not latent)

    if not latent:
        weights.load_out_proj()
    _out_proj_norm(x_ref, mod_ref, mod_row, attn_scr, w_out_ref, lng_ref, lnb_ref, layer, y_ref, n_rows, alpha)
    if not latent:
        weights.finish()


def _odd_kernel(latent, layer, n_rows, seq, alpha, lam_init, *refs):
    if latent:
        (x_ref, mod_ref, w_in_ref, w_out_ref, lq1_ref, lk1_ref, lq2_ref, lk2_ref, sub_ref, lng_ref, lnb_ref,
         cos_ref, sin_ref, cck_hbm, ccv_hbm,
         y_ref,
         attn_scr, q_scr, k_scr, v_scr, g_scr, s_scr, p_scr, past_stage, past_sems) = refs
    else:
        (x_ref, mod_ref, w_in_hbm, w_out_hbm, lq1_ref, lk1_ref, lq2_ref, lk2_ref, sub_ref, lng_ref, lnb_ref,
         y_ref, nck_hbm, ncv_hbm, w_in_bf_hbm, w_out_bf_hbm,
         attn_scr, q_scr, k_scr, v_scr, g_scr, s_scr, p_scr, kv_stage, kv_sems,
         w_in_ref, w_out_ref, w_stage, w_sems, w_out_sems) = refs

    step = pl.program_id(0)
    if not latent:
        weights = _ContextWeights(step, (w_in_hbm, w_in_ref, w_in_bf_hbm), (w_out_hbm, w_out_ref, w_out_bf_hbm),
                                  w_stage, w_sems, w_out_sems)
        weights.load_in_proj()
    mod_row = step + 1 if latent else 0

    n_heads = D_MODEL // LANES
    n_blocks = n_rows // ROW_CHUNK
    lo = _lane_iota(ROW_CHUNK) < HEAD_DIM

    def kv_out_copies(blk):
        elem = step * n_blocks + blk
        return [pltpu.make_async_copy(kv_stage.at[blk, t, :, pl.ds(LANES * h, LANES)],
                                      out.at[elem, 0, :, h, :], kv_sems.at[blk, t])
                for t, out in enumerate((nck_hbm, ncv_hbm)) for h in range(n_heads)]

    def store_k(rows, h, a):
        cols = slice(LANES * h, LANES * (h + 1))
        zero = jnp.zeros_like(a)
        k_scr[0, rows, cols] = jnp.where(lo, a, zero).astype(BF16)
        k_scr[1, rows, cols] = jnp.where(lo, zero, a).astype(BF16)

    if latent:
        n_past = cck_hbm.shape[2]
        past = pl.ds(seq, n_past)
        past_copies = [pltpu.make_async_copy(cache.at[step, layer // 2, :, h, :],
                                             past_stage.at[t, :, pl.ds(LANES * h, LANES)], past_sems.at[t])
                       for t, cache in enumerate((cck_hbm, ccv_hbm)) for h in range(n_heads)]
        for copy in past_copies:
            copy.start()

    def proj(i, carry):
        rows = _chunk_rows(i)
        hh = _modulated(x_ref, mod_ref, mod_row, rows)
        if latent:
            cos = cos_ref[rows, :]
            sin = sin_ref[rows, :]
            rot = lambda a: _rope(a, cos, sin)
        else:
            rot = lambda a: a
        for half in range(2):
            acc = _dot(hh, w_in_ref[:, 512 * half:512 * (half + 1)])
            for j in range(4):
                a = rot(acc[:, LANES * j:LANES * (j + 1)])
                cols = slice(512 * half + LANES * j, 512 * half + LANES * (j + 1))
                q_scr[rows, cols] = (a * Q_SCALE).astype(BF16)
        for half in range(2):
            acc = _dot(hh, w_in_ref[:, 1024 + 512 * half:1024 + 512 * (half + 1)])
            if not latent:
                kv_stage[i, 0, :, 512 * half:512 * (half + 1)] = acc
            for j in range(4):
                store_k(rows, 4 * half + j, rot(acc[:, LANES * j:LANES * (j + 1)]))
        for half in range(2):
            acc = _dot(hh, w_in_ref[:, 2048 + 512 * half:2048 + 512 * (half + 1)])
            if not latent:
                kv_stage[i, 1, :, 512 * half:512 * (half + 1)] = acc
            v_scr[rows, 512 * half:512 * (half + 1)] = acc.astype(BF16)
        if not latent:
            for copy in kv_out_copies(i):
                copy.start()
        for half in range(2):
            acc = _dot(hh, w_in_ref[:, 3072 + 512 * half:3072 + 512 * (half + 1)])
            g_scr[rows, 512 * half:512 * (half + 1)] = _silu(acc)
        return carry

    if latent:
        lax.fori_loop(0, n_blocks, proj, 0, unroll=2)
    else:
        for blk in range(n_blocks):
            proj(blk, 0)

    if latent:
        for copy in past_copies:
            copy.wait()
        for h in range(n_heads):
            store_k(past, h, past_stage[0, :, LANES * h:LANES * (h + 1)])
        v_scr[past, :] = past_stage[1].astype(BF16)

    lam = (jnp.exp(jnp.sum(lq1_ref[...] * lk1_ref[...], axis=1, keepdims=True))
           - jnp.exp(jnp.sum(lq2_ref[...] * lk2_ref[...], axis=1, keepdims=True)) + lam_init)
    sub = sub_ref[...] * (1.0 - lam_init)
    n_keys = seq + n_past if latent else ROW_CHUNK
    rb = _softmax_rows(n_keys)
    ones = jnp.ones((n_keys, LANES), BF16)

    def block_stages(i):
        rows = _chunk_rows(i)
        keys = pl.ds(0, n_keys) if latent else rows

        def qk(h, slot):
            cols = slice(LANES * h, LANES * (h + 1))
            q = q_scr[rows, cols]
            for m in (0, 1):
                s_scr[slot, m] = _dot_nt(q, k_scr[m, keys, cols])

        def softmax(h, slot):
            for m in (0, 1):
                for r in range(ROW_CHUNK // rb):
                    sub_rows = slice(r * rb, (r + 1) * rb)
                    s = s_scr[slot, m, sub_rows, :]
                    top = jnp.max(s, axis=1, keepdims=True)
                    p_scr[slot, m, sub_rows, :] = jnp.exp2((s - top).astype(BF16))

        def pv(h, slot):
            cols = slice(LANES * h, LANES * (h + 1))
            v_ext = jnp.concatenate([v_scr[keys, cols], ones], axis=1)
            maps = []
            for m in (0, 1):
                acc = _dot(p_scr[slot, m], v_ext)
                maps.append(acc[:, 0:LANES] / acc[:, LANES:2 * LANES])
            o = maps[0] - lam * maps[1]
            ms = jnp.mean(o * o, axis=1, keepdims=True)
            o = o * lax.rsqrt(ms + EPS) * sub
            attn_scr[rows, cols] = (o * g_scr[rows, cols]).astype(BF16)

        return qk, softmax, pv

    _attend_blocks(block_stages, n_blocks, n_heads, unrolled=not latent)

    if not latent:
        weights.load_out_proj()
    _out_proj_norm(x_ref, mod_ref, mod_row, attn_scr, w_out_ref, lng_ref, lnb_ref, layer, y_ref, n_rows, alpha)

    if not latent:
        for blk in range(n_blocks):
            for copy in kv_out_copies(blk):
                copy.wait()
        weights.finish()


MOD_SLAB_ROWS = 128
MOD_COL_BLOCK = 1024
MOD_SLOTS = 8


def _mod_kernel(n_cond, cv_ref, w_hbm, b_ref, o_ref, sb_scr, ring, acc_scr, sems):
    depth, n_in, n_out = w_hbm.shape
    sublanes = 8
    slabs_per_layer = n_in // MOD_SLAB_ROWS
    slabs = [(l, rs) for l in range(depth) for rs in range(slabs_per_layer)]

    def slab_copy(n):
        l, rs = slabs[n]
        return pltpu.make_async_copy(w_hbm.at[l, pl.ds(rs * MOD_SLAB_ROWS, MOD_SLAB_ROWS), :],
                                     ring.at[n % MOD_SLOTS], sems.at[n % MOD_SLOTS])

    for n in range(min(MOD_SLOTS, len(slabs))):
        slab_copy(n).start()
    s_t = _silu(cv_ref[...]).T
    for r in range(n_cond):
        sb_scr[r] = jnp.broadcast_to(s_t[:, r:r + 1], (n_in, LANES))

    for n, (l, rs) in enumerate(slabs):
        slab_copy(n).wait()
        for cb in range(n_out // MOD_COL_BLOCK):
            cols = pl.ds(cb * MOD_COL_BLOCK, MOD_COL_BLOCK)
            if rs == 0:
                accs = (jnp.zeros((sublanes, MOD_COL_BLOCK), F32),) * n_cond
            else:
                accs = tuple(acc_scr[r, :, cols] for r in range(n_cond))

            def body(kb, accs, n=n, rs=rs, cols=cols):
                w = ring[n % MOD_SLOTS, pl.ds(pl.multiple_of(kb * sublanes, sublanes), sublanes), cols]
                s_rows = pl.ds(pl.multiple_of(rs * MOD_SLAB_ROWS + kb * sublanes, sublanes), sublanes)
                return tuple(acc + w * jnp.tile(sb_scr[r, s_rows, :], (1, MOD_COL_BLOCK // LANES))
                             for r, acc in enumerate(accs))

            accs = lax.fori_loop(0, MOD_SLAB_ROWS // sublanes, body, accs, unroll=8)
            if rs < slabs_per_layer - 1:
                for r in range(n_cond):
                    acc_scr[r, :, cols] = accs[r]
            else:
                rows = [jnp.sum(acc, axis=0, keepdims=True) + b_ref[l:l + 1, cols] for acc in accs]
                o_ref[l, :, cols] = jnp.concatenate(rows + [jnp.zeros((8 - n_cond, MOD_COL_BLOCK), F32)], axis=0)
        if n + MOD_SLOTS < len(slabs):
            slab_copy(n + MOD_SLOTS).start()


def _full(shape, **kw):
    zeros = (0,) * len(shape)
    return pl.BlockSpec(shape, lambda i: zeros, **kw)


def _weight_specs(latent, w_in, w_out):
    if latent:
        single = pl.Buffered(1)
        return [_full(w_in.shape, pipeline_mode=single), _full(w_out.shape, pipeline_mode=single)]
    return [pl.BlockSpec(memory_space=pl.ANY), pl.BlockSpec(memory_space=pl.ANY)]


def _weight_scratch(w_in, w_out):
    assert w_in.shape[0] % W_SLAB_ROWS == 0 and w_out.shape[0] % W_SLAB_ROWS == 0
    n_slabs = max(w_in.shape[0], w_out.shape[0]) // W_SLAB_ROWS
    return [pltpu.VMEM(w_in.shape, BF16), pltpu.VMEM(w_out.shape, BF16),
            pltpu.VMEM((n_slabs, W_SLAB_ROWS, max(w_in.shape[1], w_out.shape[1])), F32),
            pltpu.SemaphoreType.DMA((n_slabs,)), pltpu.SemaphoreType.DMA((2,))]


def _rope_tables(seq):
    t = np.arange(seq)
    n_freq = HEAD_DIM // 4
    freqs = ROPE_THETA ** (-np.arange(n_freq, dtype=np.float64) / n_freq)
    ang_row = (t // GRID_W)[:, None] * freqs
    ang_col = (t % GRID_W)[:, None] * freqs
    ang = np.concatenate([ang_row, ang_row, ang_col, ang_col], axis=1)
    sign = np.concatenate([-np.ones(n_freq), np.ones(n_freq)] * 2)[None, :]
    cos = np.tile(np.cos(ang), (1, 2)).astype(np.float32)
    sin = np.tile(np.sin(ang) * sign, (1, 2)).astype(np.float32)
    chunked_t = lambda a: a.reshape(seq // ROW_CHUNK, ROW_CHUNK, LANES).transpose(0, 2, 1)
    return jnp.asarray(cos), jnp.asarray(sin), jnp.asarray(chunked_t(cos)), jnp.asarray(chunked_t(sin))


def _modulation(c, c_ctx, w_mod, b_mod):
    depth = w_mod.shape[0]
    n_cond = 1 + c.shape[0]
    cv = jnp.concatenate([c_ctx[None, :], c, jnp.zeros((8 - n_cond, D_MODEL), F32)], axis=0)
    assert D_MODEL % MOD_SLAB_ROWS == 0 and (3 * D_MODEL) % MOD_COL_BLOCK == 0
    return pl.pallas_call(
        functools.partial(_mod_kernel, n_cond),
        grid=(1,),
        in_specs=[_full(cv.shape), pl.BlockSpec(memory_space=pl.ANY), _full(b_mod.shape)],
        out_specs=_full((depth, 8, 3 * D_MODEL)),
        out_shape=jax.ShapeDtypeStruct((depth, 8, 3 * D_MODEL), F32),
        scratch_shapes=[pltpu.VMEM((n_cond, D_MODEL, LANES), F32),
                        pltpu.VMEM((MOD_SLOTS, MOD_SLAB_ROWS, 3 * D_MODEL), F32),
                        pltpu.VMEM((n_cond, 8, 3 * D_MODEL), F32),
                        pltpu.SemaphoreType.DMA((MOD_SLOTS,))],
        compiler_params=pltpu.CompilerParams(dimension_semantics=("arbitrary",)),
        name="adaln_modulation",
    )(cv, w_mod, b_mod)


def _even_layer(x, mod, layer, w_in, w_out, q_norm, k_norm, sink, ln_g, ln_b, latent, seq, n_rows, alpha, extras=()):
    total = x.shape[0]
    grid = (total // n_rows,)
    single = pl.Buffered(1)
    norms = jnp.concatenate([jnp.broadcast_to(jnp.tile(k_norm, 2)[:, None], (LANES, ROW_CHUNK)),
                             jnp.broadcast_to(jnp.tile(q_norm, 2 * ROW_CHUNK // LANES)[None, :], (8, ROW_CHUNK))], axis=0)

    row_blk = lambda width: pl.BlockSpec((n_rows, width), lambda i: (i, 0))
    in_specs = [row_blk(D_MODEL),
                pl.BlockSpec((1, 8, 3 * D_MODEL), lambda i: (layer, 0, 0)),
                *_weight_specs(latent, w_in, w_out),
                _full(norms.shape),
                pl.BlockSpec(memory_space=pltpu.SMEM),
                _full(ln_g.shape), _full(ln_b.shape)]
    args = [x, mod, w_in, w_out, norms, sink, ln_g, ln_b]
    y_shape = jax.ShapeDtypeStruct((total, D_MODEL), F32)
    n_blocks = n_rows // ROW_CHUNK
    if latent:
        cos, sin, cos_t, sin_t, cakt, cav, cbkt, cbv = extras
        n_past = cav.shape[2]
        in_specs += [_full(cos.shape, pipeline_mode=single), _full(sin.shape, pipeline_mode=single),
                     _full(cos_t.shape, pipeline_mode=single), _full(sin_t.shape, pipeline_mode=single)]
        in_specs += [pl.BlockSpec((1, LANES, n_past), lambda i: (i, 0, 0))] * 4
        args += [cos, sin, cos_t, sin_t, cakt, cav, cbkt, cbv]
        out_specs = row_blk(D_MODEL)
        out_shape = y_shape
        n_keys = seq + n_past
    else:
        kv_blk = pl.BlockSpec((n_blocks, LANES, ROW_CHUNK), lambda i: (i, 0, 0))
        hbm = pl.BlockSpec(memory_space=pl.ANY)
        out_specs = [row_blk(D_MODEL)] + [kv_blk] * 4 + [hbm, hbm]
        out_shape = ([y_shape] + [jax.ShapeDtypeStruct((total // seq, LANES, seq), F32)] * 4
                     + [jax.ShapeDtypeStruct(w_in.shape, BF16), jax.ShapeDtypeStruct(w_out.shape, BF16)])
        n_keys = n_rows
    n_kchunks = n_keys // ROW_CHUNK
    n_cols = n_keys if latent else ROW_CHUNK
    scratch = [pltpu.VMEM((n_rows, D_MODEL), BF16),
               pltpu.VMEM((n_rows, 512), BF16), pltpu.VMEM((n_rows, 512), BF16),
               pltpu.VMEM((4, n_kchunks, LANES, ROW_CHUNK), BF16), pltpu.VMEM((4, n_keys, LANES), BF16),
               pltpu.VMEM((4, n_keys // WINDOW, LANES, WINDOW), BF16), pltpu.VMEM((4, n_keys, LANES), BF16),
               pltpu.VMEM((n_rows, D_MODEL), F32),
               pltpu.VMEM((2, 2, ROW_CHUNK, n_cols), F32),
               pltpu.VMEM((2, 2, ROW_CHUNK, n_cols), BF16),
               pltpu.VMEM((2, ROW_CHUNK, LANES), F32),
               pltpu.VMEM((2 * LANES, D_MODEL), BF16),
               pltpu.VMEM((D_MODEL, 2 * LANES), BF16)]
    if latent:
        scratch.append(pltpu.VMEM((1 + 2 * WINDOW // ROW_CHUNK, ROW_CHUNK, ROW_CHUNK), F32))
    else:
        scratch += _weight_scratch(w_in, w_out)
    return pl.pallas_call(
        functools.partial(_even_kernel, latent, layer, n_rows, seq, alpha),
        grid=grid, in_specs=in_specs, out_specs=out_specs, out_shape=out_shape,
        scratch_shapes=scratch,
        compiler_params=pltpu.CompilerParams(dimension_semantics=("arbitrary",), vmem_limit_bytes=VMEM_LIMIT),
        name="even_layer_latent" if latent else "even_layer_context",
    )(*args)


def _odd_layer(x, mod, layer, w_in, w_out, lams, sub, ln_g, ln_b, latent, seq, n_rows, alpha, lam_init, extras=()):
    total = x.shape[0]
    grid = (total // n_rows,)
    row_blk = lambda width: pl.BlockSpec((n_rows, width), lambda i: (i, 0))
    single = pl.Buffered(1)
    in_specs = [row_blk(D_MODEL),
                pl.BlockSpec((1, 8, 3 * D_MODEL), lambda i: (layer, 0, 0)),
                *_weight_specs(latent, w_in, w_out),
                _full((1, HEAD_DIM)), _full((1, HEAD_DIM)), _full((1, HEAD_DIM)), _full((1, HEAD_DIM)),
                _full((1, LANES)),
                _full(ln_g.shape), _full(ln_b.shape)]
    args = [x, mod, w_in, w_out, *lams, sub, ln_g, ln_b]
    y_shape = jax.ShapeDtypeStruct((total, D_MODEL), F32)
    n_heads = D_MODEL // LANES
    n_blocks = n_rows // ROW_CHUNK
    if latent:
        cos, sin, cck, ccv = extras
        n_past = cck.shape[2]
        in_specs += [_full(cos.shape, pipeline_mode=single), _full(sin.shape, pipeline_mode=single)]
        in_specs += [pl.BlockSpec(memory_space=pl.ANY)] * 2
        args += [cos, sin, cck, ccv]
        out_specs = row_blk(D_MODEL)
        out_shape = y_shape
        n_keys = seq + n_past
    else:
        hbm = pl.BlockSpec(memory_space=pl.ANY)
        out_specs = [row_blk(D_MODEL), hbm, hbm, hbm, hbm]
        out_shape = ([y_shape] + [jax.ShapeDtypeStruct((total // seq, 1, seq, n_heads, LANES), F32)] * 2
                     + [jax.ShapeDtypeStruct(w_in.shape, BF16), jax.ShapeDtypeStruct(w_out.shape, BF16)])
        n_keys = n_rows
    n_cols = n_keys if latent else ROW_CHUNK
    scratch = [pltpu.VMEM((n_rows, D_MODEL), BF16),
               pltpu.VMEM((n_rows, D_MODEL), BF16),
               pltpu.VMEM((2, n_keys, D_MODEL), BF16),
               pltpu.VMEM((n_keys, D_MODEL), BF16),
               pltpu.VMEM((n_rows, D_MODEL), F32),
               pltpu.VMEM((2, 2, ROW_CHUNK, n_cols), F32),
               pltpu.VMEM((2, 2, ROW_CHUNK, n_cols), BF16)]
    if latent:
        scratch += [pltpu.VMEM((2, n_past, D_MODEL), F32), pltpu.SemaphoreType.DMA((2,))]
    else:
        scratch += [pltpu.VMEM((n_blocks, 2, ROW_CHUNK, D_MODEL), F32),
                    pltpu.SemaphoreType.DMA((n_blocks, 2))]
        scratch += _weight_scratch(w_in, w_out)
    return pl.pallas_call(
        functools.partial(_odd_kernel, latent, layer, n_rows, seq, alpha, lam_init),
        grid=grid, in_specs=in_specs, out_specs=out_specs, out_shape=out_shape,
        scratch_shapes=scratch,
        compiler_params=pltpu.CompilerParams(dimension_semantics=("arbitrary",), vmem_limit_bytes=VMEM_LIMIT),
        name="odd_layer_latent" if latent else "odd_layer_context",
    )(*args)


def kernel(x_prompt, x_sample, cache_a_k, cache_a_v, cache_b_k, cache_b_v, cache_c_k, cache_c_v, c, c_ctx,
           w_mod, b_mod, ln_g, ln_b, w_in_even, w_out_even, q_norm_a, k_norm_a, sink_b, w_in_odd, w_out_odd,
           lambda_q1, lambda_k1, lambda_q2, lambda_k2, subln_c):
    depth = w_mod.shape[0]
    batch, seq, _ = x_prompt.shape
    dec_batch, dec_seq, _ = x_sample.shape
    n_past = cache_a_k.shape[2]
    alpha = (2 * depth) ** 0.25
    assert seq == ROW_CHUNK and n_past % ROW_CHUNK == 0 and dec_seq % ROW_CHUNK == 0

    mod = _modulation(c, c_ctx, w_mod, b_mod)
    cos, sin, cos_t, sin_t = _rope_tables(dec_seq)

    bf16_weights = {}

    def run(x, latent, n_batch, s, rows_even, rows_odd):
        kv = {"a_k": [], "a_v": [], "b_k": [], "b_v": [], "c_k": [], "c_v": []}
        for l in range(depth):
            if l % 2 == 0:
                e = l // 2
                extras = ()
                if latent:
                    k_t = lambda t: t[:, e].transpose(0, 2, 3, 1).reshape(n_batch, LANES, n_past)
                    extras = (cos, sin, cos_t, sin_t,
                              k_t(cache_a_k), k_t(cache_a_v), k_t(cache_b_k), k_t(cache_b_v))
                w_in, w_out = bf16_weights[l] if latent else (w_in_even[e], w_out_even[e])
                res = _even_layer(x, mod, l, w_in, w_out, q_norm_a[e], k_norm_a[e],
                                  sink_b[e], ln_g, ln_b, latent, s, rows_even, alpha, extras)
                if latent:
                    x = res
                else:
                    x = res[0]
                    bf16_weights[l] = res[5:7]
                    for name, t in zip(("a_k", "a_v", "b_k", "b_v"), res[1:5]):
                        kv[name].append(t.reshape(n_batch, 2, HEAD_DIM, s).transpose(0, 3, 1, 2))
            else:
                o = l // 2
                lam_init = 0.8 - 0.6 * math.exp(-0.3 * l)
                extras = (cos, sin, cache_c_k, cache_c_v) if latent else ()
                lams = [t[o][None, :] for t in (lambda_q1, lambda_k1, lambda_q2, lambda_k2)]
                w_in, w_out = bf16_weights[l] if latent else (w_in_odd[o], w_out_odd[o])
                res = _odd_layer(x, mod, l, w_in, w_out, lams,
                                 subln_c[o][None, :], ln_g, ln_b, latent, s, rows_odd, alpha, lam_init, extras)
                if latent:
                    x = res
                else:
                    x = res[0]
                    bf16_weights[l] = res[3:5]
                    kv["c_k"].append(res[1][:, 0])
                    kv["c_v"].append(res[2][:, 0])
        return x, kv

    y_ctx, kv = run(x_prompt.reshape(batch * seq, D_MODEL), False, batch, seq, 1024, 512)
    y_lat, _ = run(x_sample.reshape(dec_batch * dec_seq, D_MODEL), True, dec_batch, dec_seq, dec_seq, dec_seq)

    stack = lambda name: jnp.stack(kv[name], axis=1)
    return (y_ctx.reshape(batch, seq, D_MODEL), y_lat.reshape(dec_batch, dec_seq, D_MODEL),
            stack("a_k"), stack("a_v"), stack("b_k"), stack("b_v"), stack("c_k"), stack("c_v"))
```

```python
import functools
import math

import jax
import jax.numpy as jnp
import numpy as np
from jax import lax
from jax.experimental import pallas as pl
from jax.experimental.pallas import tpu as pltpu

F32 = jnp.float32
BF16 = jnp.bfloat16

D_MODEL = 1024
HEAD_DIM = 64
GRID_W = 64
WINDOW = 128
ROPE_THETA = 10000.0
EPS = 1e-6
NEG_INF = -1e30
LOG2E = 1.4426950408889634
Q_SCALE = HEAD_DIM ** -0.5 * LOG2E
LANES = 128
ROW_CHUNK = 256
SOFTMAX_VREGS = 40
VMEM_LIMIT = 60000 * 1024
W_SLAB_ROWS = 128
DMA_PRIORITIES = 2


def _silu(x):
    return x / (1.0 + jnp.exp(-x))


def _dot(a, b):
    return jnp.dot(a, b, preferred_element_type=F32)


def _dot_nt(a, b):
    return lax.dot_general(a, b, (((1,), (1,)), ((), ())), preferred_element_type=F32)


def _lane_iota(rows):
    return lax.broadcasted_iota(jnp.int32, (rows, LANES), 1)


def _chunk_rows(i):
    if isinstance(i, int):
        return pl.ds(i * ROW_CHUNK, ROW_CHUNK)
    return pl.ds(pl.multiple_of(i * ROW_CHUNK, ROW_CHUNK), ROW_CHUNK)


def _softmax_rows(n_cols):
    rows = 8
    while rows * 2 * n_cols <= SOFTMAX_VREGS * 1024 and rows * 2 <= ROW_CHUNK:
        rows *= 2
    return rows


def _rope(a, cos, sin_signed):
    lane = _lane_iota(a.shape[0])
    fwd = pltpu.roll(a, LANES - 16, 1)
    bwd = pltpu.roll(a, 16, 1)
    partner = jnp.where((lane & 16) == 0, fwd, bwd)
    return a * cos + partner * sin_signed


def _rope_t(a, cos_t, sin_t):
    blocks = [a[16 * b:16 * (b + 1), :] for b in range(a.shape[0] // 16)]
    partner = jnp.concatenate([blocks[b ^ 1] for b in range(len(blocks))], axis=0)
    return a * cos_t + partner * sin_t


def _store_kt_variants(scr, chunk, kt):
    width = scr.shape[-1]
    per_block = kt.shape[1] // width
    zero = jnp.zeros((HEAD_DIM, kt.shape[1]), F32)
    for j in range(2):
        kj = kt[HEAD_DIM * j:HEAD_DIM * (j + 1), :]
        for par, full in enumerate((jnp.concatenate([kj, zero], axis=0), jnp.concatenate([zero, kj], axis=0))):
            full = full.astype(BF16)
            for c in range(per_block):
                scr[2 * j + par, chunk * per_block + c] = full[:, width * c:width * (c + 1)]


def _store_v_variants(scr, rows, a):
    lane = _lane_iota(a.shape[0])
    lo = lane < HEAD_DIM
    swapped = pltpu.roll(a, HEAD_DIM, 1)
    one = jnp.ones_like(a)
    scr[0, rows, :] = jnp.where(lo, a, one).astype(BF16)
    scr[1, rows, :] = jnp.where(lo, one, swapped).astype(BF16)
    scr[2, rows, :] = jnp.where(lo, swapped, one).astype(BF16)
    scr[3, rows, :] = jnp.where(lo, one, a).astype(BF16)


def _layer_norm_rows(z, g, b):
    mu = jnp.mean(z, axis=-1, keepdims=True)
    zc = z - mu
    var = jnp.mean(zc * zc, axis=-1, keepdims=True)
    return zc * lax.rsqrt(var + EPS) * g + b


def _modulated(x_ref, mod_ref, mod_row, rows):
    shift = mod_ref[0, pl.ds(mod_row, 1), 0:D_MODEL]
    scale = mod_ref[0, pl.ds(mod_row, 1), D_MODEL:2 * D_MODEL]
    return (x_ref[rows, :] * (1.0 + scale) + shift).astype(BF16)


def _out_proj_norm(x_ref, mod_ref, mod_row, attn_scr, w_out_ref, lng_ref, lnb_ref, layer, y_ref, n_rows, alpha):
    gate = mod_ref[0, pl.ds(mod_row, 1), 2 * D_MODEL:3 * D_MODEL]
    g = lng_ref[layer:layer + 1, :]
    b = lnb_ref[layer:layer + 1, :]

    def body(i, carry):
        rows = _chunk_rows(i)
        out = _dot(attn_scr[rows, :], w_out_ref[...])
        z = alpha * x_ref[rows, :] + gate * out
        y_ref[rows, :] = _layer_norm_rows(z, g, b)
        return carry

    lax.fori_loop(0, n_rows // ROW_CHUNK, body, 0, unroll=True)


class _ContextWeights:
    def __init__(self, step, w_in, w_out, stage, sems, out_sems):
        self.step, self.w_in, self.w_out, self.stage, self.sems = step, w_in, w_out, stage, sems
        self.out_copies = [pltpu.make_async_copy(w[1], w[2], out_sems.at[n]) for n, w in enumerate((w_in, w_out))]

    def _slab_copies(self, w_hbm):
        n_cols = w_hbm.shape[1]
        return [pltpu.make_async_copy(w_hbm.at[pl.ds(s * W_SLAB_ROWS, W_SLAB_ROWS), :],
                                      self.stage.at[s, :, pl.ds(0, n_cols)], self.sems.at[s])
                for s in range(w_hbm.shape[0] // W_SLAB_ROWS)]

    def _cast(self, w_hbm, w_scr):
        n_cols = w_hbm.shape[1]
        for s, copy in enumerate(self._slab_copies(w_hbm)):
            copy.wait()
            w_scr[pl.ds(s * W_SLAB_ROWS, W_SLAB_ROWS), :] = self.stage[s, :, 0:n_cols].astype(BF16)

    def load_in_proj(self):
        @pl.when(self.step == 0)
        def _():
            for s, copy in enumerate(self._slab_copies(self.w_in[0])):
                copy.start(priority=s % DMA_PRIORITIES)
            self._cast(self.w_in[0], self.w_in[1])
            self.out_copies[0].start()
            for s, copy in enumerate(self._slab_copies(self.w_out[0])):
                copy.start(priority=s % DMA_PRIORITIES)

    def load_out_proj(self):
        @pl.when(self.step == 0)
        def _():
            self._cast(self.w_out[0], self.w_out[1])
            self.out_copies[1].start()

    def finish(self):
        @pl.when(self.step == 0)
        def _():
            for copy in self.out_copies:
                copy.wait()


def _run_pipeline(n_items, stages):
    for u in range(n_items + len(stages) - 1):
        for k, stage in enumerate(stages):
            t = u - k
            if 0 <= t < n_items:
                stage(t, t % 2)


def _attend_blocks(block_stages, n_blocks, n_items, unrolled):
    assert n_items % 2 == 0
    if unrolled:
        per_block = [block_stages(i) for i in range(n_blocks)]
        stages = [lambda g, slot, k=k: per_block[g // n_items][k](g % n_items, slot) for k in range(3)]
        _run_pipeline(n_blocks * n_items, stages)
    else:
        def body(i, carry):
            _run_pipeline(n_items, block_stages(i))
            return carry

        lax.fori_loop(0, n_blocks, body, 0)


def _even_kernel(latent, layer, n_rows, seq, alpha, *refs):
    if latent:
        (x_ref, mod_ref, w_in_ref, w_out_ref, norms_ref, sink_ref, lng_ref, lnb_ref,
         cos_ref, sin_ref, cost_ref, sint_ref, cakt_ref, cav_ref, cbkt_ref, cbv_ref,
         y_ref,
         attn_scr, qa_scr, qb_scr, ka_scr, va_scr, kb_scr, vb_scr, g_scr, s_scr, p_scr, es_scr, wkt_scr, wv_scr,
         bias_scr) = refs
    else:
        (x_ref, mod_ref, w_in_hbm, w_out_hbm, norms_ref, sink_ref, lng_ref, lnb_ref,
         y_ref, nakt_ref, navt_ref, nbkt_ref, nbvt_ref, w_in_bf_hbm, w_out_bf_hbm,
         attn_scr, qa_scr, qb_scr, ka_scr, va_scr, kb_scr, vb_scr, g_scr, s_scr, p_scr, es_scr, wkt_scr,
         wv_scr, w_in_ref, w_out_ref, w_stage, w_sems, w_out_sems) = refs

    step = pl.program_id(0)
    if not latent:
        weights = _ContextWeights(step, (w_in_hbm, w_in_ref, w_in_bf_hbm), (w_out_hbm, w_out_ref, w_out_bf_hbm),
                                  w_stage, w_sems, w_out_sems)
        weights.load_in_proj()
    mod_row = step + 1 if latent else 0

    col_ka, col_va, col_kb, col_vb = 512, 640, 1792, 1920

    @pl.when(step == 0)
    def _():
        for r, c0 in enumerate((col_ka, col_kb)):
            wkt_scr[LANES * r:LANES * (r + 1), :] = w_in_ref[:, c0:c0 + LANES].T
        wv_scr[:, 0:LANES] = w_in_ref[:, col_va:col_va + LANES]
        wv_scr[:, LANES:2 * LANES] = w_in_ref[:, col_vb:col_vb + LANES]

    n_lat_chunks = seq // ROW_CHUNK
    if latent:
        n_past = cav_ref.shape[2]
        past_rows = pl.ds(seq, n_past)
        _store_kt_variants(ka_scr, n_lat_chunks, cakt_ref[0])
        _store_kt_variants(kb_scr, n_lat_chunks, cbkt_ref[0])
        _store_v_variants(va_scr, past_rows, cav_ref[0].T)
        _store_v_variants(vb_scr, past_rows, cbv_ref[0].T)

    knt = norms_ref[0:LANES, :]
    qn = norms_ref[LANES:LANES + 1, 0:LANES]

    def proj(i, carry):
        rows = _chunk_rows(i)
        hh = _modulated(x_ref, mod_ref, mod_row, rows)
        if latent:
            cos = cos_ref[rows, :]
            sin = sin_ref[rows, :]
            rot = lambda a: _rope(a, cos, sin)
            rot_t = lambda a: _rope_t(a, cost_ref[i], sint_ref[i])
        else:
            rot = rot_t = lambda a: a

        acc = _dot(hh, w_in_ref[:, 0:512])
        lo_lanes = _lane_iota(ROW_CHUNK) < HEAD_DIM
        for j in range(4):
            a = acc[:, LANES * j:LANES * (j + 1)]
            sq = a * a
            first = jnp.sum(jnp.where(lo_lanes, sq, 0.0), axis=1, keepdims=True)
            second = jnp.sum(jnp.where(lo_lanes, 0.0, sq), axis=1, keepdims=True)
            ms = jnp.where(lo_lanes, first, second) * (1.0 / HEAD_DIM)
            a = rot(a * lax.rsqrt(ms + EPS) * qn)
            qa_scr[rows, LANES * j:LANES * (j + 1)] = (a * Q_SCALE).astype(BF16)
        acc = _dot(hh, w_in_ref[:, 1280:1792])
        for j in range(4):
            a = rot(acc[:, LANES * j:LANES * (j + 1)])
            qb_scr[rows, LANES * j:LANES * (j + 1)] = (a * Q_SCALE).astype(BF16)
        g_scr[rows, 0:512] = _silu(_dot(hh, w_in_ref[:, 768:1280]))
        g_scr[rows, 512:1024] = _silu(_dot(hh, w_in_ref[:, 2048:2560]))
        v = _dot(hh, wv_scr[...])
        _store_v_variants(va_scr, rows, v[:, 0:LANES])
        _store_v_variants(vb_scr, rows, v[:, LANES:2 * LANES])

        kt = _dot_nt(wkt_scr[0:2 * LANES, :], hh)
        heads = [kt[HEAD_DIM * h:HEAD_DIM * (h + 1), :] for h in range(2)]
        kat = jnp.concatenate([blk * lax.rsqrt(jnp.mean(blk * blk, axis=0, keepdims=True) + EPS) for blk in heads],
                              axis=0) * knt
        kbt = kt[LANES:2 * LANES, :]
        if not latent:
            vt = v.T
            nakt_ref[i] = kat
            nbkt_ref[i] = kbt
            navt_ref[i] = vt[0:LANES, :]
            nbvt_ref[i] = vt[LANES:2 * LANES, :]
        _store_kt_variants(ka_scr, i, rot_t(kat))
        _store_kt_variants(kb_scr, i, rot_t(kbt))
        return carry

    lax.fori_loop(0, n_rows // ROW_CHUNK, proj, 0, unroll=2)

    sinks = [sink_ref[h] * LOG2E for h in range(8)]
    ck = ROW_CHUNK
    bk = kb_scr.shape[-1]
    win = ROW_CHUNK + 2 * WINDOW
    n_items = 8

    def block_stages(i):
        rows = _chunk_rows(i)
        if latent:
            a_chunks = list(range(n_lat_chunks + n_past // ck))
            a_keys = pl.ds(0, seq + n_past)
            w0 = jnp.clip(i * (ck // bk) - WINDOW // bk, 0, (seq - win) // bk)
            win_rows = pl.ds(pl.multiple_of(w0 * bk, bk), win)
            dist = (lax.broadcasted_iota(jnp.int32, (ROW_CHUNK, ck), 1)
                    - lax.broadcasted_iota(jnp.int32, (ROW_CHUNK, ck), 0))
            for c in range(win // ck):
                off = w0 * bk + c * ck - i * ck
                bias_scr[c] = jnp.where(jnp.abs(dist + off) <= WINDOW, 0.0, NEG_INF).astype(F32)
            b_first = [w0 + c * (ck // bk) for c in range(win // ck)] + [seq // bk]
            n_biased = win // ck
            b_cols = win + n_past
        else:
            a_chunks = [i]
            a_keys = rows
            b_first = [i * (ck // bk)]
            n_biased = 0
            b_cols = ck
        a_cols = len(a_chunks) * ck

        def qk(t, slot):
            p, branch = divmod(t, 2)
            cols = slice(LANES * p, LANES * (p + 1))
            kvh = p // 2
            q = (qb_scr if branch else qa_scr)[rows, cols]
            for par in (0, 1):
                var = 2 * kvh + par
                if branch:
                    tiles = [jnp.concatenate([kb_scr[var, first + d] for d in range(ck // bk)], axis=1)
                             for first in b_first]
                else:
                    tiles = [ka_scr[var, chunk] for chunk in a_chunks]
                for c, kt in enumerate(tiles):
                    s = _dot(q, kt)
                    if branch and c < n_biased:
                        s = s + bias_scr[c]
                    s_scr[slot, par, :, c * ck:(c + 1) * ck] = s

        def softmax(t, slot):
            p, branch = divmod(t, 2)
            n_cols = b_cols if branch else a_cols
            rb = _softmax_rows(n_cols)
            for par in (0, 1):
                for r in range(ROW_CHUNK // rb):
                    sub = slice(r * rb, (r + 1) * rb)
                    s = s_scr[slot, par, sub, 0:n_cols]
                    m = jnp.max(s, axis=1, keepdims=True)
                    if branch:
                        sink = sinks[2 * p + par]
                        m = jnp.maximum(m, sink)
                        es_scr[slot, sub, HEAD_DIM * par:HEAD_DIM * (par + 1)] = jnp.broadcast_to(
                            jnp.exp2(sink - m), (rb, HEAD_DIM))
                    p_scr[slot, par, sub, 0:n_cols] = jnp.exp2((s - m).astype(BF16))

        def pv(t, slot):
            p, branch = divmod(t, 2)
            kvh = p // 2
            v_scr = vb_scr if branch else va_scr
            accs = []
            for par in (0, 1):
                var = 2 * kvh + par
                if latent and branch:
                    n_loc = win
                    accs.append(_dot(p_scr[slot, par, :, 0:n_loc], v_scr[var, win_rows, :])
                                + _dot(p_scr[slot, par, :, n_loc:b_cols], v_scr[var, past_rows, :]))
                else:
                    accs.append(_dot(p_scr[slot, par, :, 0:a_cols], v_scr[var, a_keys, :]))
            lo = _lane_iota(ROW_CHUNK) < HEAD_DIM
            denom = pltpu.roll(jnp.where(lo, accs[1], accs[0]), HEAD_DIM, 1)
            if branch:
                denom = denom + es_scr[slot]
            o = jnp.where(lo, accs[0], accs[1]) / denom
            ocols = slice(512 * branch + LANES * p, 512 * branch + LANES * (p + 1))
            attn_scr[rows, ocols] = (o * g_scr[rows, ocols]).astype(BF16)

        return qk, softmax, pv

    _attend_blocks(block_stages, n_rows // ROW_CHUNK, n_items, unrolled=not latent)

    if not latent:
        weights.load_out_proj()
    _out_proj_norm(x_ref, mod_ref, mod_row, attn_scr, w_out_ref, lng_ref, lnb_ref, layer, y_ref, n_rows, alpha)
    if not latent:
        weights.finish()


def _odd_kernel(latent, layer, n_rows, seq, alpha, lam_init, *refs):
    if latent:
        (x_ref, mod_ref, w_in_ref, w_out_ref, lq1_ref, lk1_ref, lq2_ref, lk2_ref, sub_ref, lng_ref, lnb_ref,
         cos_ref, sin_ref, cck_hbm, ccv_hbm,
         y_ref,
         attn_scr, q_scr, k_scr, v_scr, g_scr, s_scr, p_scr, past_stage, past_sems) = refs
    else:
        (x_ref, mod_ref, w_in_hbm, w_out_hbm, lq1_ref, lk1_ref, lq2_ref, lk2_ref, sub_ref, lng_ref, lnb_ref,
         y_ref, nck_hbm, ncv_hbm, w_in_bf_hbm, w_out_bf_hbm,
         attn_scr, q_scr, k_scr, v_scr, g_scr, s_scr, p_scr, kv_stage, kv_sems,
         w_in_ref, w_out_ref, w_stage, w_sems, w_out_sems) = refs

    step = pl.program_id(0)
    if not latent:
        weights = _ContextWeights(step, (w_in_hbm, w_in_ref, w_in_bf_hbm), (w_out_hbm, w_out_ref, w_out_bf_hbm),
                                  w_stage, w_sems, w_out_sems)
        weights.load_in_proj()
    mod_row = step + 1 if latent else 0

    n_heads = D_MODEL // LANES
    n_blocks = n_rows // ROW_CHUNK
    lo = _lane_iota(ROW_CHUNK) < HEAD_DIM

    def kv_out_copies(blk):
        elem = step * n_blocks + blk
        return [pltpu.make_async_copy(kv_stage.at[blk, t, :, pl.ds(LANES * h, LANES)],
                                      out.at[elem, 0, :, h, :], kv_sems.at[blk, t])
                for t, out in enumerate((nck_hbm, ncv_hbm)) for h in range(n_heads)]

    def store_k(rows, h, a):
        cols = slice(LANES * h, LANES * (h + 1))
        zero = jnp.zeros_like(a)
        k_scr[0, rows, cols] = jnp.where(lo, a, zero).astype(BF16)
        k_scr[1, rows, cols] = jnp.where(lo, zero, a).astype(BF16)

    if latent:
        n_past = cck_hbm.shape[2]
        past = pl.ds(seq, n_past)
        past_copies = [pltpu.make_async_copy(cache.at[step, layer // 2, :, h, :],
                                             past_stage.at[t, :, pl.ds(LANES * h, LANES)], past_sems.at[t])
                       for t, cache in enumerate((cck_hbm, ccv_hbm)) for h in range(n_heads)]
        for copy in past_copies:
            copy.start()

    def proj(i, carry):
        rows = _chunk_rows(i)
        hh = _modulated(x_ref, mod_ref, mod_row, rows)
        if latent:
            cos = cos_ref[rows, :]
            sin = sin_ref[rows, :]
            rot = lambda a: _rope(a, cos, sin)
        else:
            rot = lambda a: a
        for half in range(2):
            acc = _dot(hh, w_in_ref[:, 512 * half:512 * (half + 1)])
            for j in range(4):
                a = rot(acc[:, LANES * j:LANES * (j + 1)])
                cols = slice(512 * half + LANES * j, 512 * half + LANES * (j + 1))
                q_scr[rows, cols] = (a * Q_SCALE).astype(BF16)
        for half in range(2):
            acc = _dot(hh, w_in_ref[:, 1024 + 512 * half:1024 + 512 * (half + 1)])
            if not latent:
                kv_stage[i, 0, :, 512 * half:512 * (half + 1)] = acc
            for j in range(4):
                store_k(rows, 4 * half + j, rot(acc[:, LANES * j:LANES * (j + 1)]))
        for half in range(2):
            acc = _dot(hh, w_in_ref[:, 2048 + 512 * half:2048 + 512 * (half + 1)])
            if not latent:
                kv_stage[i, 1, :, 512 * half:512 * (half + 1)] = acc
            v_scr[rows, 512 * half:512 * (half + 1)] = acc.astype(BF16)
        if not latent:
            for copy in kv_out_copies(i):
                copy.start()
        for half in range(2):
            acc = _dot(hh, w_in_ref[:, 3072 + 512 * half:3072 + 512 * (half + 1)])
            g_scr[rows, 512 * half:512 * (half + 1)] = _silu(acc)
        return carry

    if latent:
        lax.fori_loop(0, n_blocks, proj, 0, unroll=2)
    else:
        for blk in range(n_blocks):
            proj(blk, 0)

    if latent:
        for copy in past_copies:
            copy.wait()
        for h in range(n_heads):
            store_k(past, h, past_stage[0, :, LANES * h:LANES * (h + 1)])
        v_scr[past, :] = past_stage[1].astype(BF16)

    lam = (jnp.exp(jnp.sum(lq1_ref[...] * lk1_ref[...], axis=1, keepdims=True))
           - jnp.exp(jnp.sum(lq2_ref[...] * lk2_ref[...], axis=1, keepdims=True)) + lam_init)
    sub = sub_ref[...] * (1.0 - lam_init)
    n_keys = seq + n_past if latent else ROW_CHUNK
    rb = _softmax_rows(n_keys)
    ones = jnp.ones((n_keys, LANES), BF16)

    def block_stages(i):
        rows = _chunk_rows(i)
        keys = pl.ds(0, n_keys) if latent else rows

        def qk(h, slot):
            cols = slice(LANES * h, LANES * (h + 1))
            q = q_scr[rows, cols]
            for m in (0, 1):
                s_scr[slot, m] = _dot_nt(q, k_scr[m, keys, cols])

        def softmax(h, slot):
            for m in (0, 1):
                for r in range(ROW_CHUNK // rb):
                    sub_rows = slice(r * rb, (r + 1) * rb)
                    s = s_scr[slot, m, sub_rows, :]
                    top = jnp.max(s, axis=1, keepdims=True)
                    p_scr[slot, m, sub_rows, :] = jnp.exp2((s - top).astype(BF16))

        def pv(h, slot):
            cols = slice(LANES * h, LANES * (h + 1))
            v_ext = jnp.concatenate([v_scr[keys, cols], ones], axis=1)
            maps = []
            for m in (0, 1):
                acc = _dot(p_scr[slot, m], v_ext)
                maps.append(acc[:, 0:LANES] / acc[:, LANES:2 * LANES])
            o = maps[0] - lam * maps[1]
            ms = jnp.mean(o * o, axis=1, keepdims=True)
            o = o * lax.rsqrt(ms + EPS) * sub
            attn_scr[rows, cols] = (o * g_scr[rows, cols]).astype(BF16)

        return qk, softmax, pv

    _attend_blocks(block_stages, n_blocks, n_heads, unrolled=not latent)

    if not latent:
        weights.load_out_proj()
    _out_proj_norm(x_ref, mod_ref, mod_row, attn_scr, w_out_ref, lng_ref, lnb_ref, layer, y_ref, n_rows, alpha)

    if not latent:
        for blk in range(n_blocks):
            for copy in kv_out_copies(blk):
                copy.wait()
        weights.finish()


MOD_SLAB_ROWS = 128
MOD_COL_BLOCK = 1024
MOD_SLOTS = 8


def _mod_kernel(n_cond, cv_ref, w_hbm, b_ref, o_ref, sb_scr, ring, acc_scr, sems):
    depth, n_in, n_out = w_hbm.shape
    sublanes = 8
    slabs_per_layer = n_in // MOD_SLAB_ROWS
    slabs = [(l, rs) for l in range(depth) for rs in range(slabs_per_layer)]

    def slab_copy(n):
        l, rs = slabs[n]
        return pltpu.make_async_copy(w_hbm.at[l, pl.ds(rs * MOD_SLAB_ROWS, MOD_SLAB_ROWS), :],
                                     ring.at[n % MOD_SLOTS], sems.at[n % MOD_SLOTS])

    for n in range(min(MOD_SLOTS, len(slabs))):
        slab_copy(n).start(priority=n % DMA_PRIORITIES)
    s_t = _silu(cv_ref[...]).T
    for r in range(n_cond):
        sb_scr[r] = jnp.broadcast_to(s_t[:, r:r + 1], (n_in, LANES))

    for n, (l, rs) in enumerate(slabs):
        slab_copy(n).wait()
        for cb in range(n_out // MOD_COL_BLOCK):
            cols = pl.ds(cb * MOD_COL_BLOCK, MOD_COL_BLOCK)
            if rs == 0:
                accs = (jnp.zeros((sublanes, MOD_COL_BLOCK), F32),) * n_cond
            else:
                accs = tuple(acc_scr[r, :, cols] for r in range(n_cond))

            def body(kb, accs, n=n, rs=rs, cols=cols):
                w = ring[n % MOD_SLOTS, pl.ds(pl.multiple_of(kb * sublanes, sublanes), sublanes), cols]
                s_rows = pl.ds(pl.multiple_of(rs * MOD_SLAB_ROWS + kb * sublanes, sublanes), sublanes)
                return tuple(acc + w * jnp.tile(sb_scr[r, s_rows, :], (1, MOD_COL_BLOCK // LANES))
                             for r, acc in enumerate(accs))

            accs = lax.fori_loop(0, MOD_SLAB_ROWS // sublanes, body, accs, unroll=8)
            if rs < slabs_per_layer - 1:
                for r in range(n_cond):
                    acc_scr[r, :, cols] = accs[r]
            else:
                rows = [jnp.sum(acc, axis=0, keepdims=True) + b_ref[l:l + 1, cols] for acc in accs]
                o_ref[l, :, cols] = jnp.concatenate(rows + [jnp.zeros((8 - n_cond, MOD_COL_BLOCK), F32)], axis=0)
        if n + MOD_SLOTS < len(slabs):
            slab_copy(n + MOD_SLOTS).start(priority=n % DMA_PRIORITIES)


def _full(shape, **kw):
    zeros = (0,) * len(shape)
    return pl.BlockSpec(shape, lambda i: zeros, **kw)


def _weight_specs(latent, w_in, w_out):
    if latent:
        single = pl.Buffered(1)
        return [_full(w_in.shape, pipeline_mode=single), _full(w_out.shape, pipeline_mode=single)]
    return [pl.BlockSpec(memory_space=pl.ANY), pl.BlockSpec(memory_space=pl.ANY)]


def _weight_scratch(w_in, w_out):
    assert w_in.shape[0] % W_SLAB_ROWS == 0 and w_out.shape[0] % W_SLAB_ROWS == 0
    n_slabs = max(w_in.shape[0], w_out.shape[0]) // W_SLAB_ROWS
    return [pltpu.VMEM(w_in.shape, BF16), pltpu.VMEM(w_out.shape, BF16),
            pltpu.VMEM((n_slabs, W_SLAB_ROWS, max(w_in.shape[1], w_out.shape[1])), F32),
            pltpu.SemaphoreType.DMA((n_slabs,)), pltpu.SemaphoreType.DMA((2,))]


def _rope_tables(seq):
    t = np.arange(seq)
    n_freq = HEAD_DIM // 4
    freqs = ROPE_THETA ** (-np.arange(n_freq, dtype=np.float64) / n_freq)
    ang_row = (t // GRID_W)[:, None] * freqs
    ang_col = (t % GRID_W)[:, None] * freqs
    ang = np.concatenate([ang_row, ang_row, ang_col, ang_col], axis=1)
    sign = np.concatenate([-np.ones(n_freq), np.ones(n_freq)] * 2)[None, :]
    cos = np.tile(np.cos(ang), (1, 2)).astype(np.float32)
    sin = np.tile(np.sin(ang) * sign, (1, 2)).astype(np.float32)
    chunked_t = lambda a: a.reshape(seq // ROW_CHUNK, ROW_CHUNK, LANES).transpose(0, 2, 1)
    return jnp.asarray(cos), jnp.asarray(sin), jnp.asarray(chunked_t(cos)), jnp.asarray(chunked_t(sin))


def _modulation(c, c_ctx, w_mod, b_mod):
    depth = w_mod.shape[0]
    n_cond = 1 + c.shape[0]
    cv = jnp.concatenate([c_ctx[None, :], c, jnp.zeros((8 - n_cond, D_MODEL), F32)], axis=0)
    assert D_MODEL % MOD_SLAB_ROWS == 0 and (3 * D_MODEL) % MOD_COL_BLOCK == 0
    return pl.pallas_call(
        functools.partial(_mod_kernel, n_cond),
        grid=(1,),
        in_specs=[_full(cv.shape), pl.BlockSpec(memory_space=pl.ANY), _full(b_mod.shape)],
        out_specs=_full((depth, 8, 3 * D_MODEL)),
        out_shape=jax.ShapeDtypeStruct((depth, 8, 3 * D_MODEL), F32),
        scratch_shapes=[pltpu.VMEM((n_cond, D_MODEL, LANES), F32),
                        pltpu.VMEM((MOD_SLOTS, MOD_SLAB_ROWS, 3 * D_MODEL), F32),
                        pltpu.VMEM((n_cond, 8, 3 * D_MODEL), F32),
                        pltpu.SemaphoreType.DMA((MOD_SLOTS,))],
        compiler_params=pltpu.CompilerParams(dimension_semantics=("arbitrary",)),
        name="adaln_modulation",
    )(cv, w_mod, b_mod)


def _even_layer(x, mod, layer, w_in, w_out, q_norm, k_norm, sink, ln_g, ln_b, latent, seq, n_rows, alpha, extras=()):
    total = x.shape[0]
    grid = (total // n_rows,)
    single = pl.Buffered(1)
    norms = jnp.concatenate([jnp.broadcast_to(jnp.tile(k_norm, 2)[:, None], (LANES, ROW_CHUNK)),
                             jnp.broadcast_to(jnp.tile(q_norm, 2 * ROW_CHUNK // LANES)[None, :], (8, ROW_CHUNK))], axis=0)

    row_blk = lambda width: pl.BlockSpec((n_rows, width), lambda i: (i, 0))
    in_specs = [row_blk(D_MODEL),
                pl.BlockSpec((1, 8, 3 * D_MODEL), lambda i: (layer, 0, 0)),
                *_weight_specs(latent, w_in, w_out),
                _full(norms.shape),
                pl.BlockSpec(memory_space=pltpu.SMEM),
                _full(ln_g.shape), _full(ln_b.shape)]
    args = [x, mod, w_in, w_out, norms, sink, ln_g, ln_b]
    y_shape = jax.ShapeDtypeStruct((total, D_MODEL), F32)
    n_blocks = n_rows // ROW_CHUNK
    if latent:
        cos, sin, cos_t, sin_t, cakt, cav, cbkt, cbv = extras
        n_past = cav.shape[2]
        in_specs += [_full(cos.shape, pipeline_mode=single), _full(sin.shape, pipeline_mode=single),
                     _full(cos_t.shape, pipeline_mode=single), _full(sin_t.shape, pipeline_mode=single)]
        in_specs += [pl.BlockSpec((1, LANES, n_past), lambda i: (i, 0, 0))] * 4
        args += [cos, sin, cos_t, sin_t, cakt, cav, cbkt, cbv]
        out_specs = row_blk(D_MODEL)
        out_shape = y_shape
        n_keys = seq + n_past
    else:
        kv_blk = pl.BlockSpec((n_blocks, LANES, ROW_CHUNK), lambda i: (i, 0, 0))
        hbm = pl.BlockSpec(memory_space=pl.ANY)
        out_specs = [row_blk(D_MODEL)] + [kv_blk] * 4 + [hbm, hbm]
        out_shape = ([y_shape] + [jax.ShapeDtypeStruct((total // seq, LANES, seq), F32)] * 4
                     + [jax.ShapeDtypeStruct(w_in.shape, BF16), jax.ShapeDtypeStruct(w_out.shape, BF16)])
        n_keys = n_rows
    n_kchunks = n_keys // ROW_CHUNK
    n_cols = n_keys if latent else ROW_CHUNK
    scratch = [pltpu.VMEM((n_rows, D_MODEL), BF16),
               pltpu.VMEM((n_rows, 512), BF16), pltpu.VMEM((n_rows, 512), BF16),
               pltpu.VMEM((4, n_kchunks, LANES, ROW_CHUNK), BF16), pltpu.VMEM((4, n_keys, LANES), BF16),
               pltpu.VMEM((4, n_keys // WINDOW, LANES, WINDOW), BF16), pltpu.VMEM((4, n_keys, LANES), BF16),
               pltpu.VMEM((n_rows, D_MODEL), F32),
               pltpu.VMEM((2, 2, ROW_CHUNK, n_cols), F32),
               pltpu.VMEM((2, 2, ROW_CHUNK, n_cols), BF16),
               pltpu.VMEM((2, ROW_CHUNK, LANES), F32),
               pltpu.VMEM((2 * LANES, D_MODEL), BF16),
               pltpu.VMEM((D_MODEL, 2 * LANES), BF16)]
    if latent:
        scratch.append(pltpu.VMEM((1 + 2 * WINDOW // ROW_CHUNK, ROW_CHUNK, ROW_CHUNK), F32))
    else:
        scratch += _weight_scratch(w_in, w_out)
    return pl.pallas_call(
        functools.partial(_even_kernel, latent, layer, n_rows, seq, alpha),
        grid=grid, in_specs=in_specs, out_specs=out_specs, out_shape=out_shape,
        scratch_shapes=scratch,
        compiler_params=pltpu.CompilerParams(dimension_semantics=("arbitrary",), vmem_limit_bytes=VMEM_LIMIT),
        name="even_layer_latent" if latent else "even_layer_context",
    )(*args)


def _odd_layer(x, mod, layer, w_in, w_out, lams, sub, ln_g, ln_b, latent, seq, n_rows, alpha, lam_init, extras=()):
    total = x.shape[0]
    grid = (total // n_rows,)
    row_blk = lambda width: pl.BlockSpec((n_rows, width), lambda i: (i, 0))
    single = pl.Buffered(1)
    in_specs = [row_blk(D_MODEL),
                pl.BlockSpec((1, 8, 3 * D_MODEL), lambda i: (layer, 0, 0)),
                *_weight_specs(latent, w_in, w_out),
                _full((1, HEAD_DIM)), _full((1, HEAD_DIM)), _full((1, HEAD_DIM)), _full((1, HEAD_DIM)),
                _full((1, LANES)),
                _full(ln_g.shape), _full(ln_b.shape)]
    args = [x, mod, w_in, w_out, *lams, sub, ln_g, ln_b]
    y_shape = jax.ShapeDtypeStruct((total, D_MODEL), F32)
    n_heads = D_MODEL // LANES
    n_blocks = n_rows // ROW_CHUNK
    if latent:
        cos, sin, cck, ccv = extras
        n_past = cck.shape[2]
        in_specs += [_full(cos.shape, pipeline_mode=single), _full(sin.shape, pipeline_mode=single)]
        in_specs += [pl.BlockSpec(memory_space=pl.ANY)] * 2
        args += [cos, sin, cck, ccv]
        out_specs = row_blk(D_MODEL)
        out_shape = y_shape
        n_keys = seq + n_past
    else:
        hbm = pl.BlockSpec(memory_space=pl.ANY)
        out_specs = [row_blk(D_MODEL), hbm, hbm, hbm, hbm]
        out_shape = ([y_shape] + [jax.ShapeDtypeStruct((total // seq, 1, seq, n_heads, LANES), F32)] * 2
                     + [jax.ShapeDtypeStruct(w_in.shape, BF16), jax.ShapeDtypeStruct(w_out.shape, BF16)])
        n_keys = n_rows
    n_cols = n_keys if latent else ROW_CHUNK
    scratch = [pltpu.VMEM((n_rows, D_MODEL), BF16),
               pltpu.VMEM((n_rows, D_MODEL), BF16),
               pltpu.VMEM((2, n_keys, D_MODEL), BF16),
               pltpu.VMEM((n_keys, D_MODEL), BF16),
               pltpu.VMEM((n_rows, D_MODEL), F32),
               pltpu.VMEM((2, 2, ROW_CHUNK, n_cols), F32),
               pltpu.VMEM((2, 2, ROW_CHUNK, n_cols), BF16)]
    if latent:
        scratch += [pltpu.VMEM((2, n_past, D_MODEL), F32), pltpu.SemaphoreType.DMA((2,))]
    else:
        scratch += [pltpu.VMEM((n_blocks, 2, ROW_CHUNK, D_MODEL), F32),
                    pltpu.SemaphoreType.DMA((n_blocks, 2))]
        scratch += _weight_scratch(w_in, w_out)
    return pl.pallas_call(
        functools.partial(_odd_kernel, latent, layer, n_rows, seq, alpha, lam_init),
        grid=grid, in_specs=in_specs, out_specs=out_specs, out_shape=out_shape,
        scratch_shapes=scratch,
        compiler_params=pltpu.CompilerParams(dimension_semantics=("arbitrary",), vmem_limit_bytes=VMEM_LIMIT),
        name="odd_layer_latent" if latent else "odd_layer_context",
    )(*args)


def kernel(x_prompt, x_sample, cache_a_k, cache_a_v, cache_b_k, cache_b_v, cache_c_k, cache_c_v, c, c_ctx,
           w_mod, b_mod, ln_g, ln_b, w_in_even, w_out_even, q_norm_a, k_norm_a, sink_b, w_in_odd, w_out_odd,
           lambda_q1, lambda_k1, lambda_q2, lambda_k2, subln_c):
    depth = w_mod.shape[0]
    batch, seq, _ = x_prompt.shape
    dec_batch, dec_seq, _ = x_sample.shape
    n_past = cache_a_k.shape[2]
    alpha = (2 * depth) ** 0.25
    assert seq == ROW_CHUNK and n_past % ROW_CHUNK == 0 and dec_seq % ROW_CHUNK == 0

    mod = _modulation(c, c_ctx, w_mod, b_mod)
    cos, sin, cos_t, sin_t = _rope_tables(dec_seq)

    bf16_weights = {}

    def run(x, latent, n_batch, s, rows_even, rows_odd):
        kv = {"a_k": [], "a_v": [], "b_k": [], "b_v": [], "c_k": [], "c_v": []}
        for l in range(depth):
            if l % 2 == 0:
                e = l // 2
                extras = ()
                if latent:
                    k_t = lambda t: t[:, e].transpose(0, 2, 3, 1).reshape(n_batch, LANES, n_past)
                    extras = (cos, sin, cos_t, sin_t,
                              k_t(cache_a_k), k_t(cache_a_v), k_t(cache_b_k), k_t(cache_b_v))
                w_in, w_out = bf16_weights[l] if latent else (w_in_even[e], w_out_even[e])
                res = _even_layer(x, mod, l, w_in, w_out, q_norm_a[e], k_norm_a[e],
                                  sink_b[e], ln_g, ln_b, latent, s, rows_even, alpha, extras)
                if latent:
                    x = res
                else:
                    x = res[0]
                    bf16_weights[l] = res[5:7]
                    for name, t in zip(("a_k", "a_v", "b_k", "b_v"), res[1:5]):
                        kv[name].append(t.reshape(n_batch, 2, HEAD_DIM, s).transpose(0, 3, 1, 2))
            else:
                o = l // 2
                lam_init = 0.8 - 0.6 * math.exp(-0.3 * l)
                extras = (cos, sin, cache_c_k, cache_c_v) if latent else ()
                lams = [t[o][None, :] for t in (lambda_q1, lambda_k1, lambda_q2, lambda_k2)]
                w_in, w_out = bf16_weights[l] if latent else (w_in_odd[o], w_out_odd[o])
                res = _odd_layer(x, mod, l, w_in, w_out, lams,
                                 subln_c[o][None, :], ln_g, ln_b, latent, s, rows_odd, alpha, lam_init, extras)
                if latent:
                    x = res
                else:
                    x = res[0]
                    bf16_weights[l] = res[3:5]
                    kv["c_k"].append(res[1][:, 0])
                    kv["c_v"].append(res[2][:, 0])
        return x, kv

    y_ctx, kv = run(x_prompt.reshape(batch * seq, D_MODEL), False, batch, seq, 1024, 512)
    y_lat, _ = run(x_sample.reshape(dec_batch * dec_seq, D_MODEL), True, dec_batch, dec_seq, dec_seq, dec_seq)

    stack = lambda name: jnp.stack(kv[name], axis=1)
    return (y_ctx.reshape(batch, seq, D_MODEL), y_lat.reshape(dec_batch, dec_seq, D_MODEL),
            stack("a_k"), stack("a_v"), stack("b_k"), stack("b_v"), stack("c_k"), stack("c_v"))
```
